```python
import jax, jax.numpy as jnp
from jax import lax
import numpy as np

D_MODEL = 1024
BATCH = 4
SEQ = 8192
DEPTH = 1

MEM_LEN = 256
DN_ALPHA = (2 * DEPTH) ** 0.25
DN_BETA = (8 * DEPTH) ** -0.25
LN_EPS = 1e-5
NEG = -1e30
FORCE = 1e9

RET_HEADS = 4
RET_DIM = D_MODEL // 2 // RET_HEADS
RET_CHUNK = 128
RET_ROPE_BASE = 10000.0

NSA_HEADS = 8
NSA_KV_GROUPS = 2
NSA_HPG = NSA_HEADS // NSA_KV_GROUPS
NSA_DIM = D_MODEL // 2 // NSA_HEADS
CMP_LEN = 32
CMP_STRIDE = 16
CMP_HIDDEN = 256
SEL_LEN = 64
SEL_TOPK = 16
WIN = 512
Q_BLOCK = 128
ROPE_THETA = 500000.0
ROPE_DIMS = NSA_DIM // 4

RET_WIDTH = RET_HEADS * RET_DIM
NSA_WIDTH = NSA_HEADS * NSA_DIM
MIX_WIDTH = RET_WIDTH + NSA_WIDTH
KV_WIDTH = NSA_KV_GROUPS * NSA_DIM
IN_SIZES = [RET_WIDTH] * 4 + [NSA_WIDTH] + [KV_WIDTH] * 6 + [3 * NSA_HEADS]
IN_COLS = sum(IN_SIZES)

XATT_HEADS = 4
XATT_DIM = D_MODEL // XATT_HEADS

N_EXPERTS = 64
TOP_K = 8
N_GROUPS = 8
TOPK_GROUPS = 4
EXPERT_FF = 256
SHARED_FF = 256
ROUTED_SCALE = 2.5
MOE_BLOCK = 128

kernel_name = 'hymba_retnet_nsa_deepnorm_moe'


def layer_norm(x, g, b):
    xf = x.astype(jnp.float32)
    mu = xf.mean(-1, keepdims=True)
    var = jnp.square(xf - mu).mean(-1, keepdims=True)
    return ((xf - mu) * lax.rsqrt(var + LN_EPS) * g + b).astype(x.dtype)


def rotary(x, pos, n_rot, base):
    half = n_rot // 2
    inv = base ** (-jnp.arange(half, dtype=jnp.float32) / half)
    ang = pos.astype(jnp.float32)[..., None] * inv
    cos = jnp.cos(ang)[:, :, None, :]
    sin = jnp.sin(ang)[:, :, None, :]
    x1 = x[..., :half].astype(jnp.float32)
    x2 = x[..., half:n_rot].astype(jnp.float32)
    out = jnp.concatenate([x1 * cos - x2 * sin, x2 * cos + x1 * sin,
                           x[..., n_rot:].astype(jnp.float32)], axis=-1)
    return out.astype(x.dtype)


def retention(q, k, v, pos):
    B, T, H, dh = q.shape
    C = RET_CHUNK
    nc = T // C
    q = rotary(q, pos, dh, RET_ROPE_BASE)
    k = rotary(k, pos, dh, RET_ROPE_BASE) * (dh ** -0.5)

    def chunks(a):
        return a.reshape(B, nc, C, H, dh).transpose(0, 3, 1, 2, 4)

    qc, kc, vc = chunks(q), chunks(k), chunks(v)
    log_g = jnp.log(1.0 - 2.0 ** (-5.0 - jnp.arange(H, dtype=jnp.float32)))
    idx = jnp.arange(C, dtype=jnp.float32)
    rel = idx[:, None] - idx[None, :]
    dmask = jnp.where(rel >= 0, jnp.exp(log_g[:, None, None] * jnp.maximum(rel, 0.0)), 0.0)
    scores = jnp.einsum('bhncd,bhnmd->bhncm', qc, kc) * dmask[:, None]
    inner = jnp.einsum('bhncm,bhnme->bhnce', scores, vc)
    zeta = jnp.exp(log_g[:, None] * (C - 1.0 - idx))
    xi = jnp.exp(log_g[:, None] * (idx + 1.0))
    chunk_decay = jnp.exp(log_g * C)[:, None, None]
    kv = jnp.einsum('bhncd,bhnce->nbhde', kc * zeta[:, None, :, None], vc).astype(jnp.float32)

    def step(state, kv_n):
        return state * chunk_decay + kv_n, state

    _, prev = lax.scan(step, jnp.zeros(kv.shape[1:], jnp.float32), kv)
    cross = jnp.einsum('bhncd,nbhde->bhnce', qc, prev) * xi[:, None, :, None]
    o = (inner + cross).astype(jnp.float32).transpose(0, 2, 3, 1, 4).reshape(B, T, H, dh)
    mu = o.mean(-1, keepdims=True)
    var = jnp.square(o - mu).mean(-1, keepdims=True)
    o = (o - mu) * lax.rsqrt(var + LN_EPS)
    return o.reshape(B, T, H * dh)


def native_sparse_attention(q, kc, vc, ks, vs, kw, vw, gates, pos,
                            cmp_pe_k, cmp_pe_v, cmp_w1_k, cmp_w2_k, cmp_w1_v, cmp_w2_v):
    B, T, H, dh = q.shape
    G, hpg = NSA_KV_GROUPS, NSA_HPG
    scale = dh ** -0.5
    q_rot = rotary(q, pos, ROPE_DIMS, ROPE_THETA)
    ks = rotary(ks, pos, ROPE_DIMS, ROPE_THETA)
    kw = rotary(kw, pos, ROPE_DIMS, ROPE_THETA)

    n_cmp = (T - CMP_LEN) // CMP_STRIDE + 1
    blk = jnp.arange(n_cmp)[:, None] * CMP_STRIDE + jnp.arange(CMP_LEN)[None, :]

    def compress(a, pe, w1, w2):
        ab = a[:, blk] + pe[None, None, :, None, :]
        ab = ab.transpose(0, 1, 3, 2, 4).reshape(B, n_cmp, G, CMP_LEN * dh)
        return jax.nn.silu(ab @ w1) @ w2

    k_cmp = compress(kc, cmp_pe_k, cmp_w1_k, cmp_w2_k)
    v_cmp = compress(vc, cmp_pe_v, cmp_w1_v, cmp_w2_v)
    cmp_start = jnp.arange(n_cmp) * CMP_STRIDE
    cmp_last = cmp_start + CMP_LEN - 1

    n_sel = T // SEL_LEN
    n_top = min(SEL_TOPK, n_sel)
    ks_b = ks.reshape(B, n_sel, SEL_LEN, G, dh).transpose(0, 3, 1, 2, 4)
    vs_b = vs.reshape(B, n_sel, SEL_LEN, G, dh).transpose(0, 3, 1, 2, 4)
    sel_start = jnp.arange(n_sel) * SEL_LEN
    overlap = ((cmp_start[:, None] < sel_start[None, :] + SEL_LEN)
               & (cmp_start[:, None] + CMP_LEN > sel_start[None, :])).astype(jnp.float32)
    gather = jax.vmap(jax.vmap(lambda tab, ix: tab[ix]))

    kw_p = jnp.pad(kw, ((0, 0), (WIN, 0), (0, 0), (0, 0)))
    vw_p = jnp.pad(vw, ((0, 0), (WIN, 0), (0, 0), (0, 0)))

    nqb = T // Q_BLOCK

    def to_blocks(a):
        return a.reshape((B, nqb, Q_BLOCK) + a.shape[2:]).swapaxes(0, 1)

    q_g = q.reshape(B, T, G, hpg, dh)
    qr_g = q_rot.reshape(B, T, G, hpg, dh)
    g_g = jax.nn.sigmoid(gates).reshape(B, T, G, hpg, 3)
    j_sel = jnp.arange(n_sel)

    def block_fn(args):
        i, qb, qrb, gb = args
        t = i * Q_BLOCK + jnp.arange(Q_BLOCK)
        s_c = jnp.einsum('bqghd,bcgd->bghqc', qb, k_cmp).astype(jnp.float32) * scale
        valid_c = cmp_last[None, :] <= t[:, None]
        p_c = jnp.where(valid_c, jax.nn.softmax(jnp.where(valid_c, s_c, NEG), axis=-1), 0.0)
        o_c = jnp.einsum('bghqc,bcgd->bqghd', p_c, v_cmp)
        imp = jnp.einsum('bghqc,cs->bgqs', p_c, overlap)
        cur = t // SEL_LEN
        forced = (j_sel[None] == 0) | (j_sel[None] == cur[:, None]) | (j_sel[None] == cur[:, None] - 1)
        imp = jnp.where(forced, FORCE, imp)
        imp = jnp.where(j_sel[None] <= cur[:, None], imp, NEG)
        _, sel = lax.top_k(imp, n_top)
        k_sel = gather(ks_b, sel)
        v_sel = gather(vs_b, sel)
        kpos = sel[..., None] * SEL_LEN + jnp.arange(SEL_LEN)
        m_s = (kpos <= t[:, None, None])[:, :, None]
        s_s = jnp.einsum('bqghd,bgqnld->bghqnl', qrb, k_sel).astype(jnp.float32) * scale
        s_s = jnp.where(m_s, s_s, NEG)
        p_s = jax.nn.softmax(s_s.reshape(B, G, hpg, Q_BLOCK, n_top * SEL_LEN), axis=-1)
        p_s = p_s.reshape(B, G, hpg, Q_BLOCK, n_top, SEL_LEN)
        o_s = jnp.einsum('bghqnl,bgqnld->bqghd', p_s, v_sel)
        start = i * Q_BLOCK
        k_win = lax.dynamic_slice_in_dim(kw_p, start, WIN + Q_BLOCK, axis=1)
        v_win = lax.dynamic_slice_in_dim(vw_p, start, WIN + Q_BLOCK, axis=1)
        wpos = start - WIN + jnp.arange(WIN + Q_BLOCK)
        diff = t[:, None] - wpos[None, :]
        m_w = (wpos[None, :] >= 0) & (diff >= 0) & (diff < WIN)
        s_w = jnp.einsum('bqghd,bkgd->bghqk', qrb, k_win).astype(jnp.float32) * scale
        p_w = jax.nn.softmax(jnp.where(m_w, s_w, NEG), axis=-1)
        o_w = jnp.einsum('bghqk,bkgd->bqghd', p_w, v_win)
        o = gb[..., 0:1] * o_c + gb[..., 1:2] * o_s + gb[..., 2:3] * o_w
        return o.reshape(B, Q_BLOCK, H * dh)

    out = lax.map(block_fn, (jnp.arange(nqb), to_blocks(q_g), to_blocks(qr_g), to_blocks(g_g)))
    return out.swapaxes(0, 1).reshape(B, T, H * dh)


def memory_cross_attention(x, mem, w_xq, w_xkv, w_xo):
    B, T, D = x.shape
    q = (x @ w_xq).reshape(B, T, XATT_HEADS, XATT_DIM)
    k, v = jnp.split(mem @ w_xkv, 2, axis=-1)
    k = k.reshape(B, -1, XATT_HEADS, XATT_DIM)
    v = v.reshape(B, -1, XATT_HEADS, XATT_DIM)
    s = jnp.einsum('bthd,bmhd->bhtm', q, k).astype(jnp.float32) * (XATT_DIM ** -0.5)
    p = jax.nn.softmax(s, axis=-1)
    o = jnp.einsum('bhtm,bmhd->bthd', p, v).reshape(B, T, D)
    return o @ w_xo


def moe(x, w_router, router_bias, w_gate, w_up, w_down, ws_gate, ws_up, ws_down):
    B, T, D = x.shape
    xf = x.reshape(-1, D)
    N = xf.shape[0]
    E = N_EXPERTS
    scores = jax.nn.sigmoid((xf @ w_router).astype(jnp.float32))
    biased = scores + router_bias.astype(jnp.float32)
    group_score = lax.top_k(biased.reshape(N, N_GROUPS, E // N_GROUPS), 2)[0].sum(-1)
    kth = lax.top_k(group_score, TOPK_GROUPS)[0][:, -1:]
    expert_ok = jnp.repeat(group_score >= kth, E // N_GROUPS, axis=1)
    _, top_e = lax.top_k(jnp.where(expert_ok, biased, NEG), TOP_K)
    w = jnp.take_along_axis(scores, top_e, axis=-1)
    w = w / w.sum(-1, keepdims=True) * ROUTED_SCALE
    flat_e = top_e.reshape(-1)
    flat_tok = jnp.repeat(jnp.arange(N, dtype=jnp.int32), TOP_K)
    flat_w = w.reshape(-1)
    order = jnp.argsort(flat_e)
    e_sorted = flat_e[order]
    counts = jnp.bincount(flat_e, length=E)
    starts = jnp.cumsum(counts) - counts
    padded = (counts + MOE_BLOCK - 1) // MOE_BLOCK * MOE_BLOCK
    pad_ends = jnp.cumsum(padded)
    pad_starts = pad_ends - padded
    dest = pad_starts[e_sorted] + (jnp.arange(N * TOP_K) - starts[e_sorted])
    cap = N * TOP_K + E * MOE_BLOCK
    n_blocks = cap // MOE_BLOCK
    slot_tok = jnp.zeros((cap,), jnp.int32).at[dest].set(flat_tok[order])
    slot_w = jnp.zeros((cap,), jnp.float32).at[dest].set(flat_w[order])
    block_e = jnp.minimum(jnp.searchsorted(pad_ends, jnp.arange(n_blocks) * MOE_BLOCK, side='right'), E - 1)

    def body(acc, blk):
        tok, wt, e = blk
        h = xf[tok]
        y = (jax.nn.silu(h @ w_gate[e]) * (h @ w_up[e])) @ w_down[e]
        return acc.at[tok].add(y.astype(jnp.float32) * wt[:, None]), None

    routed, _ = lax.scan(body, jnp.zeros((N, D), jnp.float32),
                         (slot_tok.reshape(n_blocks, MOE_BLOCK), slot_w.reshape(n_blocks, MOE_BLOCK), block_e))
    shared = (jax.nn.silu(xf @ ws_gate) * (xf @ ws_up)) @ ws_down
    return (routed + shared).reshape(B, T, D)


def hybrid_layer(x, mem, positions, w_in, cmp_pe_k, cmp_pe_v, cmp_w1_k, cmp_w2_k, cmp_w1_v, cmp_w2_v,
                 w_out, ln1_g, ln1_b, w_xq, w_xkv, w_xo, ln2_g, ln2_b, w_router, router_bias,
                 w_gate, w_up, w_down, ws_gate, ws_up, ws_down, ln3_g, ln3_b):
    B, T, _ = x.shape
    h = x @ w_in
    (rq, rk, rv, rg, nq, nkc, nvc, nks, nvs, nkw, nvw, ngate) = jnp.split(
        h, list(np.cumsum(IN_SIZES)[:-1]), axis=-1)
    o_ret = retention(rq.reshape(B, T, RET_HEADS, RET_DIM), rk.reshape(B, T, RET_HEADS, RET_DIM),
                      rv.reshape(B, T, RET_HEADS, RET_DIM), positions)
    o_ret = jax.nn.silu(rg) * o_ret
    kvs = lambda a: a.reshape(B, T, NSA_KV_GROUPS, NSA_DIM)
    o_nsa = native_sparse_attention(nq.reshape(B, T, NSA_HEADS, NSA_DIM), kvs(nkc), kvs(nvc), kvs(nks),
                                    kvs(nvs), kvs(nkw), kvs(nvw), ngate, positions,
                                    cmp_pe_k, cmp_pe_v, cmp_w1_k, cmp_w2_k, cmp_w1_v, cmp_w2_v)
    mix = jnp.concatenate([o_ret.astype(x.dtype), o_nsa.astype(x.dtype)], axis=-1) @ w_out
    x = layer_norm(DN_ALPHA * x + mix, ln1_g, ln1_b)
    x = layer_norm(DN_ALPHA * x + memory_cross_attention(x, mem, w_xq, w_xkv, w_xo), ln2_g, ln2_b)
    y = moe(x, w_router, router_bias, w_gate, w_up, w_down, ws_gate, ws_up, ws_down)
    x = layer_norm(DN_ALPHA * x + y.astype(x.dtype), ln3_g, ln3_b)
    return x


def setup_inputs(seed: int = 0) -> dict:
    key = jax.random.key(seed)
    ks = jax.random.split(key, 32)
    L, D = DEPTH, D_MODEL
    nrm = lambda k, shape, s: jax.random.normal(k, shape, jnp.float32) * s
    return {
        'x': nrm(ks[0], (BATCH, SEQ, D), 1.0),
        'mem': nrm(ks[1], (BATCH, MEM_LEN, D), 1.0),
        'positions': (jnp.arange(SEQ, dtype=jnp.int32)[None, :]
                      + jax.random.randint(ks[2], (BATCH, 1), 0, 1024, dtype=jnp.int32)),
        'w_in': nrm(ks[3], (L, D, IN_COLS), D ** -0.5),
        'cmp_pe_k': nrm(ks[4], (L, CMP_LEN, NSA_DIM), 0.1),
        'cmp_pe_v': nrm(ks[5], (L, CMP_LEN, NSA_DIM), 0.1),
        'cmp_w1_k': nrm(ks[6], (L, CMP_LEN * NSA_DIM, CMP_HIDDEN), (CMP_LEN * NSA_DIM) ** -0.5),
        'cmp_w2_k': nrm(ks[7], (L, CMP_HIDDEN, NSA_DIM), CMP_HIDDEN ** -0.5),
        'cmp_w1_v': nrm(ks[8], (L, CMP_LEN * NSA_DIM, CMP_HIDDEN), (CMP_LEN * NSA_DIM) ** -0.5),
        'cmp_w2_v': nrm(ks[9], (L, CMP_HIDDEN, NSA_DIM), CMP_HIDDEN ** -0.5),
        'w_out': nrm(ks[10], (L, MIX_WIDTH, D), DN_BETA * MIX_WIDTH ** -0.5),
        'ln1_g': 1.0 + nrm(ks[11], (L, D), 0.02),
        'ln1_b': nrm(ks[12], (L, D), 0.02),
        'w_xq': nrm(ks[13], (L, D, D), D ** -0.5),
        'w_xkv': nrm(ks[14], (L, D, 2 * D), D ** -0.5),
        'w_xo': nrm(ks[15], (L, D, D), DN_BETA * D ** -0.5),
        'ln2_g': 1.0 + nrm(ks[16], (L, D), 0.02),
        'ln2_b': nrm(ks[17], (L, D), 0.02),
        'w_router': nrm(ks[18], (L, D, N_EXPERTS), D ** -0.5),
        'router_bias': nrm(ks[19], (L, N_EXPERTS), 0.01),
        'w_gate': nrm(ks[20], (L, N_EXPERTS, D, EXPERT_FF), D ** -0.5),
        'w_up': nrm(ks[21], (L, N_EXPERTS, D, EXPERT_FF), D ** -0.5),
        'w_down': nrm(ks[22], (L, N_EXPERTS, EXPERT_FF, D), DN_BETA * EXPERT_FF ** -0.5),
        'ws_gate': nrm(ks[23], (L, D, SHARED_FF), D ** -0.5),
        'ws_up': nrm(ks[24], (L, D, SHARED_FF), D ** -0.5),
        'ws_down': nrm(ks[25], (L, SHARED_FF, D), DN_BETA * SHARED_FF ** -0.5),
        'ln3_g': 1.0 + nrm(ks[26], (L, D), 0.02),
        'ln3_b': nrm(ks[27], (L, D), 0.02),
    }


def reference(x, mem, positions, w_in, cmp_pe_k, cmp_pe_v, cmp_w1_k, cmp_w2_k, cmp_w1_v, cmp_w2_v,
              w_out, ln1_g, ln1_b, w_xq, w_xkv, w_xo, ln2_g, ln2_b, w_router, router_bias,
              w_gate, w_up, w_down, ws_gate, ws_up, ws_down, ln3_g, ln3_b):
    for l in range(DEPTH):
        x = hybrid_layer(x, mem, positions, w_in[l], cmp_pe_k[l], cmp_pe_v[l], cmp_w1_k[l], cmp_w2_k[l],
                         cmp_w1_v[l], cmp_w2_v[l], w_out[l], ln1_g[l], ln1_b[l], w_xq[l], w_xkv[l],
                         w_xo[l], ln2_g[l], ln2_b[l], w_router[l], router_bias[l], w_gate[l], w_up[l],
                         w_down[l], ws_gate[l], ws_up[l], ws_down[l], ln3_g[l], ln3_b[l])
    return x
```

```python
import functools

import numpy as np
import jax
import jax.numpy as jnp
from jax import lax
from jax.experimental import pallas as pl
from jax.experimental.pallas import tpu as pltpu

D_MODEL = 1024
MEM_LEN = 256
DEPTH = 1
DN_ALPHA = (2 * DEPTH) ** 0.25
LN_EPS = 1e-5
NEG = -1e30
FORCE = 1e9

RET_HEADS = 4
RET_DIM = 128
RET_CHUNK = 128
RET_ROPE_BASE = 10000.0
RET_WIDTH = RET_HEADS * RET_DIM

NSA_HEADS = 8
NSA_KV_GROUPS = 2
NSA_HPG = NSA_HEADS // NSA_KV_GROUPS
NSA_DIM = 64
NSA_WIDTH = NSA_HEADS * NSA_DIM
KV_WIDTH = NSA_KV_GROUPS * NSA_DIM
CMP_LEN = 32
CMP_STRIDE = 16
CMP_HIDDEN = 256
SEL_LEN = 64
SEL_SHIFT = 6
SEL_TOPK = 16
WIN = 512
ROPE_THETA = 500000.0
ROPE_DIMS = NSA_DIM // 4
GATE_LANES = 16

XATT_HEADS = 4
XATT_DIM = D_MODEL // XATT_HEADS

N_EXPERTS = 64
TOP_K = 8
N_GROUPS = 8
TOPK_GROUPS = 4
EXPERT_FF = 256
SHARED_FF = 256
ROUTED_SCALE = 2.5

LANES = 128
VMEM_LIMIT = 56 * 1024 * 1024

F32 = jnp.float32
BF16 = jnp.bfloat16
NT_DIMS = (((1,), (1,)), ((), ()))


def _params(n_axes):
    return pltpu.CompilerParams(dimension_semantics=("arbitrary",) * n_axes,
                                vmem_limit_bytes=VMEM_LIMIT)


def _dot(a, b):
    return jnp.dot(a, b, preferred_element_type=F32)


def _dot_nt(a, b):
    return lax.dot_general(a, b, NT_DIMS, preferred_element_type=F32)


def _layer_norm(v, g, b):
    mu = jnp.mean(v, axis=-1, keepdims=True)
    d = v - mu
    var = jnp.mean(d * d, axis=-1, keepdims=True)
    return d * lax.rsqrt(var + LN_EPS) * g + b


def _inproj_kernel(x_ref, pos_ref, wret_ref, wnq_ref, wkv_ref, wg_ref, invr_ref, invn_ref,
                   rq_ref, rk_ref, rv_ref, rg_ref, nq_ref, nqr_ref, kc_ref, vc_ref,
                   ks_ref, vs_ref, kw_ref, vw_ref, gate_ref):
    xb = x_ref[...].astype(BF16)
    pos = pos_ref[...]
    lane = lax.broadcasted_iota(jnp.int32, (1, LANES), 1)

    ang = pos * invr_ref[...]
    cos_r = jnp.cos(ang)
    sin_r = jnp.sin(ang)
    sin_r = jnp.where(lane < RET_DIM // 2, -sin_r, sin_r)
    for h in range(RET_HEADS):
        cols = slice(h * RET_DIM, (h + 1) * RET_DIM)
        q = _dot(xb, wret_ref[:, cols])
        rq_ref[:, cols] = (q * cos_r + pltpu.roll(q, RET_DIM // 2, 1) * sin_r).astype(BF16)
        k = _dot(xb, wret_ref[:, RET_WIDTH + h * RET_DIM:RET_WIDTH + (h + 1) * RET_DIM])
        k = (k * cos_r + pltpu.roll(k, RET_DIM // 2, 1) * sin_r) * (RET_DIM ** -0.5)
        rk_ref[:, cols] = k.astype(BF16)
    rv_ref[...] = _dot(xb, wret_ref[:, 2 * RET_WIDTH:3 * RET_WIDTH]).astype(BF16)
    rg_ref[...] = _dot(xb, wret_ref[:, 3 * RET_WIDTH:4 * RET_WIDTH]).astype(BF16)

    half = ROPE_DIMS // 2
    j = lane % NSA_DIM
    angn = pos * invn_ref[...]
    cos_n = jnp.cos(angn)
    sin_n = jnp.sin(angn)
    sin_lo = jnp.where(j < half, -sin_n, 0.0)
    sin_hi = jnp.where((j >= half) & (j < 2 * half), sin_n, 0.0)

    def rope_n(v):
        return v * cos_n + pltpu.roll(v, half, 1) * sin_hi + pltpu.roll(v, LANES - half, 1) * sin_lo

    scale = NSA_DIM ** -0.5
    for c in range(NSA_WIDTH // LANES):
        cols = slice(c * LANES, (c + 1) * LANES)
        q = _dot(xb, wnq_ref[:, cols])
        nq_ref[:, cols] = (q * scale).astype(BF16)
        nqr_ref[:, cols] = (rope_n(q) * scale).astype(BF16)

    def kv(i):
        return _dot(xb, wkv_ref[:, i * KV_WIDTH:(i + 1) * KV_WIDTH])

    def split_groups(ref, v):
        for g in range(NSA_KV_GROUPS):
            ref[g] = v[:, g * NSA_DIM:(g + 1) * NSA_DIM].astype(BF16)

    kc_ref[...] = kv(0)
    vc_ref[...] = kv(1)
    split_groups(ks_ref, rope_n(kv(2)))
    split_groups(vs_ref, kv(3))
    split_groups(kw_ref, rope_n(kv(4)))
    split_groups(vw_ref, kv(5))

    gt = jax.nn.sigmoid(_dot(xb, wg_ref[...]))
    for g in range(NSA_KV_GROUPS):
        gate_ref[g] = gt[:, g * GATE_LANES:(g + 1) * GATE_LANES]


def _inproj(x2d, pos_col, w_in):
    n = x2d.shape[0]
    tm = 512 if n % 512 == 0 else n
    off = np.cumsum([0] + [RET_WIDTH] * 4 + [NSA_WIDTH] + [KV_WIDTH] * 6)
    w_ret = w_in[:, :off[4]].astype(BF16)
    w_nq = w_in[:, off[4]:off[5]].astype(BF16)
    w_kv = w_in[:, off[5]:off[11]].astype(BF16)
    wg = w_in[:, off[11]:].reshape(D_MODEL, NSA_KV_GROUPS, NSA_HPG * 3)
    wg = jnp.pad(wg, ((0, 0), (0, 0), (0, GATE_LANES - NSA_HPG * 3)))
    wg = jnp.pad(wg.reshape(D_MODEL, NSA_KV_GROUPS * GATE_LANES),
                 ((0, 0), (0, LANES - NSA_KV_GROUPS * GATE_LANES))).astype(BF16)

    lane = np.arange(LANES)
    half_r = RET_DIM // 2
    inv_r = (np.float32(RET_ROPE_BASE) ** (-np.arange(half_r, dtype=np.float32) / np.float32(half_r)))
    inv_r = inv_r.astype(np.float32)[lane % half_r][None, :]
    half_n = ROPE_DIMS // 2
    inv_n = (np.float32(ROPE_THETA) ** (-np.arange(half_n, dtype=np.float32) / np.float32(half_n)))
    jn = lane % NSA_DIM
    inv_n = np.where(jn < ROPE_DIMS, inv_n.astype(np.float32)[jn % half_n], np.float32(0.0))[None, :]

    row = lambda w: pl.BlockSpec((tm, w), lambda i: (i, 0))
    full = lambda a: pl.BlockSpec(a.shape, lambda i: (0,) * a.ndim)
    grp = lambda w: pl.BlockSpec((NSA_KV_GROUPS, tm, w), lambda i: (0, i, 0))
    bf = lambda w: jax.ShapeDtypeStruct((n, w), BF16)
    gbf = jax.ShapeDtypeStruct((NSA_KV_GROUPS, n, NSA_DIM), BF16)
    inv_r = jnp.asarray(inv_r, F32)
    inv_n = jnp.asarray(inv_n, F32)
    return pl.pallas_call(
        _inproj_kernel,
        grid=(n // tm,),
        in_specs=[row(D_MODEL), row(1), full(w_ret), full(w_nq), full(w_kv), full(wg),
                  full(inv_r), full(inv_n)],
        out_specs=[row(RET_WIDTH)] * 4 + [row(NSA_WIDTH)] * 2 + [row(KV_WIDTH)] * 2
                  + [grp(NSA_DIM)] * 4 + [grp(GATE_LANES)],
        out_shape=[bf(RET_WIDTH)] * 4 + [bf(NSA_WIDTH)] * 2
                  + [jax.ShapeDtypeStruct((n, KV_WIDTH), F32)] * 2 + [gbf] * 4
                  + [jax.ShapeDtypeStruct((NSA_KV_GROUPS, n, GATE_LANES), F32)],
        compiler_params=_params(1),
        name="inproj",
    )(x2d, pos_col, w_ret, w_nq, w_kv, wg, inv_r, inv_n)


def _retention_kernel(q_ref, k_ref, v_ref, g_ref, o_ref, state_ref):
    c = RET_CHUNK

    @pl.when(pl.program_id(1) == 0)
    def _():
        state_ref[...] = jnp.zeros_like(state_ref)

    row = lax.broadcasted_iota(jnp.int32, (c, c), 0)
    col = lax.broadcasted_iota(jnp.int32, (c, c), 1)
    rel = (row - col).astype(F32)
    idx = lax.broadcasted_iota(jnp.int32, (c, 1), 0).astype(F32)
    for h in range(RET_HEADS):
        log_g = float(np.log(np.float32(1.0) - np.float32(2.0) ** np.float32(-5.0 - h)))
        cols = slice(h * RET_DIM, (h + 1) * RET_DIM)
        q = q_ref[:, cols]
        k = k_ref[:, cols]
        v = v_ref[:, cols]
        dmask = jnp.where(rel >= 0, jnp.exp(log_g * jnp.maximum(rel, 0.0)), 0.0)
        scores = _dot_nt(q, k) * dmask
        inner = _dot(scores.astype(BF16), v)
        zeta = jnp.exp(log_g * (c - 1.0 - idx))
        xi = jnp.exp(log_g * (idx + 1.0))
        prev = state_ref[h]
        cross = _dot(q, prev.astype(BF16)) * xi
        kz = (k.astype(F32) * zeta).astype(BF16)
        kv = lax.dot_general(kz, v, (((0,), (0,)), ((), ())), preferred_element_type=F32)
        state_ref[h] = prev * float(np.exp(np.float32(log_g) * np.float32(c))) + kv
        o = inner + cross
        mu = jnp.mean(o, axis=-1, keepdims=True)
        d = o - mu
        var = jnp.mean(d * d, axis=-1, keepdims=True)
        o = d * lax.rsqrt(var + LN_EPS)
        o_ref[:, cols] = (jax.nn.silu(g_ref[:, cols].astype(F32)) * o).astype(BF16)


def _retention(rq, rk, rv, rg, batch, seq):
    nc = seq // RET_CHUNK
    spec = pl.BlockSpec((RET_CHUNK, RET_WIDTH), lambda b, n: (b * nc + n, 0))
    return pl.pallas_call(
        _retention_kernel,
        grid=(batch, nc),
        in_specs=[spec] * 4,
        out_specs=spec,
        out_shape=jax.ShapeDtypeStruct(rq.shape, BF16),
        scratch_shapes=[pltpu.VMEM((RET_HEADS, RET_DIM, RET_DIM), F32)],
        compiler_params=_params(2),
        name="retention",
    )(rq, rk, rv, rg)


def _compress_kernel(a_ref, pe_ref, w1_ref, w2_ref, o_ref, shift_ref, *, n_cmp):
    rows = a_ref.shape[0]
    a = a_ref[...]
    lo = (a + pe_ref[0]).astype(BF16)
    hi = (a + pe_ref[1]).astype(BF16)
    ridx = lax.broadcasted_iota(jnp.int32, (rows, 1), 0)
    shift_ref[rows:rows + 8, :] = jnp.zeros((8, CMP_HIDDEN), F32)
    for g in range(NSA_KV_GROUPS):
        p = _dot(lo, w1_ref[0, g])
        shift_ref[0:rows, :] = _dot(hi, w1_ref[1, g])
        hid = jax.nn.silu(p + shift_ref[pl.ds(1, rows), :])
        out = _dot(hid.astype(BF16), w2_ref[...])
        o_ref[g] = jnp.where(ridx < n_cmp, out, 0.0).astype(BF16)


def _compress(a, pe, w1, w2, batch, seq):
    rows = seq // CMP_STRIDE
    per = CMP_STRIDE * KV_WIDTH
    n_cmp = (seq - CMP_LEN) // CMP_STRIDE + 1
    a2 = a.reshape(batch * rows, per)
    pe2 = jnp.tile(pe.reshape(2, CMP_STRIDE, 1, NSA_DIM), (1, 1, NSA_KV_GROUPS, 1)).reshape(2, 1, per)
    w1r = w1.reshape(2, CMP_STRIDE, 1, NSA_DIM, CMP_HIDDEN)
    eye = jnp.eye(NSA_KV_GROUPS, dtype=w1.dtype).reshape(1, NSA_KV_GROUPS, 1, NSA_KV_GROUPS, 1, 1)
    w1x = (w1r[:, None] * eye).reshape(2, NSA_KV_GROUPS, per, CMP_HIDDEN).astype(BF16)
    w2b = w2.astype(BF16)
    full = lambda arr: pl.BlockSpec(arr.shape, lambda b: (0,) * arr.ndim)
    return pl.pallas_call(
        functools.partial(_compress_kernel, n_cmp=n_cmp),
        grid=(batch,),
        in_specs=[pl.BlockSpec((rows, per), lambda b: (b, 0)), full(pe2), full(w1x), full(w2b)],
        out_specs=pl.BlockSpec((None, NSA_KV_GROUPS, rows, NSA_DIM), lambda b: (b, 0, 0, 0)),
        out_shape=jax.ShapeDtypeStruct((batch, NSA_KV_GROUPS, rows, NSA_DIM), BF16),
        scratch_shapes=[pltpu.VMEM((rows + 8, CMP_HIDDEN), F32)],
        compiler_params=_params(1),
        name="compress",
    )(a2, pe2, w1x, w2b)


def _stack_heads(v):
    return jnp.concatenate([v[:, h * NSA_DIM:(h + 1) * NSA_DIM] for h in range(NSA_HPG)], axis=0)


def _nsa_kernel(qraw_ref, qrot_ref, gate_ref, kcmp_ref, vcmp_ref, ovt_ref,
                ks_ref, vs_ref, kw_ref, vw_ref, o_ref, *, tq, tk, seq):
    i = pl.program_id(2)
    t0 = i * tq
    rows = NSA_HPG * tq
    n_sel = seq // SEL_LEN
    n_cmp_rows = seq // CMP_STRIDE
    q_raw = _stack_heads(qraw_ref[...])
    q_rot = _stack_heads(qrot_ref[...])
    tq_col = t0 + lax.broadcasted_iota(jnp.int32, (tq, 1), 0)

    def per_head(bias):
        return jnp.broadcast_to(bias[None], (NSA_HPG,) + bias.shape).reshape(rows, bias.shape[1])

    s_c = _dot_nt(q_raw, kcmp_ref[...])
    c_idx = lax.broadcasted_iota(jnp.int32, (1, n_cmp_rows), 1)
    valid = per_head(jnp.where(c_idx * CMP_STRIDE + (CMP_LEN - 1) <= tq_col, 1.0, 0.0)) > 0.5
    s_c = jnp.where(valid, s_c, NEG)
    m_c = jnp.max(s_c, axis=-1, keepdims=True)
    e_c = jnp.where(valid, jnp.exp(s_c - m_c), 0.0)
    l_c = jnp.sum(e_c, axis=-1, keepdims=True)
    p_c = e_c / jnp.where(l_c > 0.0, l_c, 1.0)
    o_c = _dot(p_c.astype(BF16), vcmp_ref[...])

    p_sum = p_c[0:tq]
    for h in range(1, NSA_HPG):
        p_sum = p_sum + p_c[h * tq:(h + 1) * tq]
    p_hi = p_sum.astype(BF16)
    p_lo = (p_sum - p_hi.astype(F32)).astype(BF16)
    ovt = ovt_ref[...]
    imp = _dot_nt(ovt, p_hi) + _dot_nt(ovt, p_lo)
    jb = lax.broadcasted_iota(jnp.int32, (n_sel, tq), 0)
    cur = (t0 + lax.broadcasted_iota(jnp.int32, (n_sel, tq), 1)) >> SEL_SHIFT
    forced = (jb == 0) | (jb == cur) | (jb == cur - 1)
    work = jnp.where(forced, FORCE, imp)
    work = jnp.where(jb <= cur, work, NEG)
    sel_t = jnp.zeros((n_sel, tq), F32)
    for _ in range(min(SEL_TOPK, n_sel)):
        best = jnp.max(work, axis=0, keepdims=True)
        first = jnp.min(jnp.where(work == best, jb, n_sel), axis=0, keepdims=True)
        hit = jb == first
        sel_t = jnp.where(hit, 1.0, sel_t)
        work = jnp.where(hit, -jnp.inf, work)
    sel = sel_t.T.astype(BF16)

    def sel_step(kt, carry):
        m, l, acc = carry
        k0 = pl.multiple_of(kt * tk, tk)
        kpos = k0 + lax.broadcasted_iota(jnp.int32, (1, tk), 1)
        blk = (k0 + lax.broadcasted_iota(jnp.int32, (n_sel, tk), 1)) >> SEL_SHIFT
        expand = jnp.where(lax.broadcasted_iota(jnp.int32, (n_sel, tk), 0) == blk, 1.0, 0.0).astype(BF16)
        chosen = _dot(sel, expand)
        bias = per_head(jnp.where((chosen > 0.5) & (kpos <= tq_col), 0.0, NEG))
        s = _dot_nt(q_rot, ks_ref[pl.ds(k0, tk), :]) + bias
        m_new = jnp.maximum(m, jnp.max(s, axis=-1, keepdims=True))
        alpha = jnp.exp(m - m_new)
        p = jnp.exp(s - m_new)
        l = alpha * l + jnp.sum(p, axis=-1, keepdims=True)
        acc = alpha * acc + _dot(p.astype(BF16), vs_ref[pl.ds(k0, tk), :])
        return m_new, l, acc

    n_tiles = (t0 + tq + tk - 1) // tk
    init = (jnp.full((rows, 1), NEG, F32), jnp.zeros((rows, 1), F32), jnp.zeros((rows, NSA_DIM), F32))
    _, l_s, acc_s = lax.fori_loop(0, n_tiles, sel_step, init)
    o_s = acc_s / l_s

    span = WIN + tq
    ws = pl.multiple_of(jnp.maximum(t0 - WIN, 0), tq)
    wpos = ws + lax.broadcasted_iota(jnp.int32, (1, span), 1)
    dist = tq_col - wpos
    bias_w = per_head(jnp.where((dist >= 0) & (dist < WIN), 0.0, NEG))
    s_w = _dot_nt(q_rot, kw_ref[pl.ds(ws, span), :]) + bias_w
    m_w = jnp.max(s_w, axis=-1, keepdims=True)
    p_w = jnp.exp(s_w - m_w)
    l_w = jnp.sum(p_w, axis=-1, keepdims=True)
    o_w = _dot(p_w.astype(BF16), vw_ref[pl.ds(ws, span), :]) / l_w

    gt = gate_ref[...]
    outs = []
    for h in range(NSA_HPG):
        r = slice(h * tq, (h + 1) * tq)
        outs.append(gt[:, 3 * h:3 * h + 1] * o_c[r] + gt[:, 3 * h + 1:3 * h + 2] * o_s[r]
                    + gt[:, 3 * h + 2:3 * h + 3] * o_w[r])
    o_ref[...] = jnp.concatenate(outs, axis=1).astype(BF16)


def _nsa(nq, nqr, gates, kcmp, vcmp, ks, vs, kw, vw, batch, seq):
    n = batch * seq
    tq = 128
    tk = 512 if seq % 512 == 0 else seq
    nqb = seq // tq
    n_sel = seq // SEL_LEN
    rows_c = seq // CMP_STRIDE
    gw = NSA_HPG * NSA_DIM
    cs = np.arange(rows_c)[None, :] * CMP_STRIDE
    ss = np.arange(n_sel)[:, None] * SEL_LEN
    n_cmp = (seq - CMP_LEN) // CMP_STRIDE + 1
    ovt = ((cs < ss + SEL_LEN) & (cs + CMP_LEN > ss) & (np.arange(rows_c)[None, :] < n_cmp))
    ovt = jnp.asarray(ovt.astype(np.float32), BF16)

    qspec = pl.BlockSpec((tq, gw), lambda b, g, i: (b * nqb + i, g))
    cspec = pl.BlockSpec((None, None, rows_c, NSA_DIM), lambda b, g, i: (b, g, 0, 0))
    kspec = pl.BlockSpec((None, seq, NSA_DIM), lambda b, g, i: (g, b, 0))
    return pl.pallas_call(
        functools.partial(_nsa_kernel, tq=tq, tk=tk, seq=seq),
        grid=(batch, NSA_KV_GROUPS, nqb),
        in_specs=[qspec, qspec,
                  pl.BlockSpec((None, tq, GATE_LANES), lambda b, g, i: (g, b * nqb + i, 0)),
                  cspec, cspec, pl.BlockSpec(ovt.shape, lambda b, g, i: (0, 0)),
                  kspec, kspec, kspec, kspec],
        out_specs=qspec,
        out_shape=jax.ShapeDtypeStruct((n, NSA_WIDTH), BF16),
        compiler_params=_params(3),
        name="nsa",
    )(nq, nqr, gates, kcmp, vcmp, ovt, ks, vs, kw, vw)


def _memkv_kernel(mem_ref, w_ref, kv_ref):
    kv_ref[...] = _dot(mem_ref[...].astype(BF16), w_ref[...]).astype(BF16)


def _memkv(mem2d, w_xkv):
    n = mem2d.shape[0]
    w = w_xkv.astype(BF16)
    return pl.pallas_call(
        _memkv_kernel,
        grid=(n // MEM_LEN,),
        in_specs=[pl.BlockSpec((MEM_LEN, D_MODEL), lambda i: (i, 0)),
                  pl.BlockSpec(w.shape, lambda i: (0, 0))],
        out_specs=pl.BlockSpec((MEM_LEN, 2 * D_MODEL), lambda i: (i, 0)),
        out_shape=jax.ShapeDtypeStruct((n, 2 * D_MODEL), BF16),
        compiler_params=_params(1),
        name="memkv",
    )(mem2d, w)


def _postmix_kernel(x_ref, oret_ref, onsa_ref, kv_ref, wout_ref, wq_ref, wo_ref,
                    g1_ref, b1_ref, g2_ref, b2_ref, x2_ref, x2p_ref):
    mixed = jnp.concatenate([oret_ref[...], onsa_ref[...]], axis=1)
    x1 = _layer_norm(DN_ALPHA * x_ref[...] + _dot(mixed, wout_ref[...]), g1_ref[...], b1_ref[...])
    q = (_dot(x1.astype(BF16), wq_ref[...]) * (XATT_DIM ** -0.5)).astype(BF16)
    heads = []
    for h in range(XATT_HEADS):
        cols = slice(h * XATT_DIM, (h + 1) * XATT_DIM)
        s = _dot_nt(q[:, cols], kv_ref[:, cols])
        m = jnp.max(s, axis=-1, keepdims=True)
        p = jnp.exp(s - m)
        l = jnp.sum(p, axis=-1, keepdims=True)
        heads.append(_dot(p.astype(BF16), kv_ref[:, D_MODEL + h * XATT_DIM:D_MODEL + (h + 1) * XATT_DIM]) / l)
    att = jnp.concatenate(heads, axis=1).astype(BF16)
    x2 = _layer_norm(DN_ALPHA * x1 + _dot(att, wo_ref[...]), g2_ref[...], b2_ref[...])
    x2_ref[...] = x2
    half = D_MODEL // 2
    hi = pltpu.bitcast(x2[:, :half].astype(BF16).astype(F32), jnp.uint32)
    lo = pltpu.bitcast(x2[:, half:].astype(BF16).astype(F32), jnp.uint32)
    x2p_ref[...] = hi | (lo >> 16)


def _postmix(x2d, o_ret, o_nsa, kvx, w_out, w_xq, w_xo, ln1_g, ln1_b, ln2_g, ln2_b, batch, seq):
    n = x2d.shape[0]
    tm = 256 if seq % 256 == 0 else seq
    per_b = seq // tm
    row = lambda w: pl.BlockSpec((tm, w), lambda b, i: (b * per_b + i, 0))
    full = lambda a: pl.BlockSpec(a.shape, lambda b, i: (0,) * a.ndim)
    ws = [w_out.astype(BF16), w_xq.astype(BF16), w_xo.astype(BF16)]
    vecs = [v.reshape(1, D_MODEL) for v in (ln1_g, ln1_b, ln2_g, ln2_b)]
    return pl.pallas_call(
        _postmix_kernel,
        grid=(batch, per_b),
        in_specs=[row(D_MODEL), row(RET_WIDTH), row(NSA_WIDTH),
                  pl.BlockSpec((MEM_LEN, 2 * D_MODEL), lambda b, i: (b, 0))]
                 + [full(w) for w in ws] + [full(v) for v in vecs],
        out_specs=[row(D_MODEL), row(D_MODEL // 2)],
        out_shape=[jax.ShapeDtypeStruct((n, D_MODEL), F32),
                   jax.ShapeDtypeStruct((n, D_MODEL // 2), jnp.uint32)],
        compiler_params=_params(2),
        name="postmix",
    )(x2d, o_ret, o_nsa, kvx, *ws, *vecs)


def _router_kernel(x_ref, wr_ref, bias_ref, e_ref, rank_ref, w_ref, cnt_ref, cntrow_ref, carry_ref, carryrow_ref):
    tn = x_ref.shape[0]
    per = N_EXPERTS // N_GROUPS

    @pl.when(pl.program_id(0) == 0)
    def _():
        carry_ref[...] = jnp.zeros_like(carry_ref)
        carryrow_ref[...] = jnp.zeros_like(carryrow_ref)

    logits = _dot_nt(wr_ref[...], x_ref[...].astype(BF16))
    scores = jax.nn.sigmoid(logits)
    biased = scores + bias_ref[...]
    b3 = biased.reshape(N_GROUPS, per, tn)
    member = lax.broadcasted_iota(jnp.int32, (N_GROUPS, per, tn), 1)
    top1 = jnp.max(b3, axis=1, keepdims=True)
    first1 = jnp.min(jnp.where(b3 == top1, member, per), axis=1, keepdims=True)
    top2 = jnp.max(jnp.where(member == first1, -jnp.inf, b3), axis=1, keepdims=True)
    gscore = top1 + top2
    gidx = lax.broadcasted_iota(jnp.int32, (N_GROUPS, 1, tn), 0)
    gwork = gscore
    for _ in range(TOPK_GROUPS - 1):
        gbest = jnp.max(gwork, axis=0, keepdims=True)
        gfirst = jnp.min(jnp.where(gwork == gbest, gidx, N_GROUPS), axis=0, keepdims=True)
        gwork = jnp.where(gidx == gfirst, -jnp.inf, gwork)
    kth = jnp.max(gwork, axis=0, keepdims=True)
    work = jnp.where(gscore >= kth, b3, NEG).reshape(N_EXPERTS, tn)
    eidx = lax.broadcasted_iota(jnp.int32, (N_EXPERTS, tn), 0)
    picks = []
    chosen = jnp.zeros((N_EXPERTS, tn), F32)
    for _ in range(TOP_K):
        best = jnp.max(work, axis=0, keepdims=True)
        first = jnp.min(jnp.where(work == best, eidx, N_EXPERTS), axis=0, keepdims=True)
        hit = eidx == first
        picks.append((first, hit))
        chosen = jnp.where(hit, 1.0, chosen)
        work = jnp.where(hit, -jnp.inf, work)

    r_i = lax.broadcasted_iota(jnp.int32, (tn, tn), 0)
    c_i = lax.broadcasted_iota(jnp.int32, (tn, tn), 1)
    before = jnp.where(r_i < c_i, 1.0, 0.0).astype(BF16)
    chosen_b = chosen.astype(BF16)
    rank = _dot(chosen_b, before) + carry_ref[...]
    carry_ref[...] = carry_ref[...] + jnp.sum(chosen, axis=1, keepdims=True)
    carryrow_ref[...] = carryrow_ref[...] + _dot_nt(jnp.ones((8, tn), BF16), chosen_b)
    cnt_ref[...] = carry_ref[...]
    cntrow_ref[...] = carryrow_ref[...]

    wsel = [jnp.sum(jnp.where(hit, scores, 0.0), axis=0, keepdims=True) for _, hit in picks]
    wsum = wsel[0]
    for v in wsel[1:]:
        wsum = wsum + v
    for kk, (first, hit) in enumerate(picks):
        e_ref[kk:kk + 1, :] = first
        rank_ref[kk:kk + 1, :] = jnp.sum(jnp.where(hit, rank, 0.0), axis=0, keepdims=True).astype(jnp.int32)
        w_ref[kk:kk + 1, :] = wsel[kk] / wsum * ROUTED_SCALE


def _router(x2, w_router, router_bias):
    n = x2.shape[0]
    tn = 512 if n % 512 == 0 else n
    wr_t = w_router.T.astype(BF16)
    bias = router_bias.reshape(N_EXPERTS, 1).astype(F32)
    kspec = pl.BlockSpec((TOP_K, tn), lambda i: (0, i))
    return pl.pallas_call(
        _router_kernel,
        grid=(n // tn,),
        in_specs=[pl.BlockSpec((tn, D_MODEL), lambda i: (i, 0)),
                  pl.BlockSpec(wr_t.shape, lambda i: (0, 0)),
                  pl.BlockSpec(bias.shape, lambda i: (0, 0))],
        out_specs=[kspec, kspec, kspec, pl.BlockSpec((N_EXPERTS, 1), lambda i: (0, 0)),
                   pl.BlockSpec((8, N_EXPERTS), lambda i: (0, 0))],
        out_shape=[jax.ShapeDtypeStruct((TOP_K, n), jnp.int32),
                   jax.ShapeDtypeStruct((TOP_K, n), jnp.int32),
                   jax.ShapeDtypeStruct((TOP_K, n), F32),
                   jax.ShapeDtypeStruct((N_EXPERTS, 1), F32),
                   jax.ShapeDtypeStruct((8, N_EXPERTS), F32)],
        scratch_shapes=[pltpu.VMEM((N_EXPERTS, 1), F32), pltpu.VMEM((8, N_EXPERTS), F32)],
        compiler_params=_params(1),
        name="router",
    )(x2, wr_t, bias)


def _slots_kernel(e_ref, rank_ref, cnt_ref, cntrow_ref, dest_ref, blk_e_ref, nblk_ref, *, blk, n_blocks):
    pad = lambda c: jnp.ceil(c / blk) * blk
    padded = pad(cnt_ref[...])
    padded_row = pad(cntrow_ref[0:1, :])
    r_i = lax.broadcasted_iota(jnp.int32, (N_EXPERTS, N_EXPERTS), 0)
    c_i = lax.broadcasted_iota(jnp.int32, (N_EXPERTS, N_EXPERTS), 1)
    start = jnp.sum(jnp.where(c_i < r_i, padded_row, 0.0), axis=1, keepdims=True)
    end = start + padded
    e = e_ref[...]
    dest = rank_ref[...]
    for ex in range(N_EXPERTS):
        dest = dest + jnp.where(e == ex, start[ex:ex + 1, :].astype(jnp.int32), 0)
    dest_ref[...] = dest
    bstart = (lax.broadcasted_iota(jnp.int32, (1, n_blocks), 1) * blk).astype(F32)
    owner = jnp.sum(jnp.where(end <= bstart, 1.0, 0.0), axis=0, keepdims=True)
    blk_e_ref[...] = jnp.minimum(owner, N_EXPERTS - 1.0).astype(jnp.int32)
    nblk_ref[...] = (end[N_EXPERTS - 1:N_EXPERTS, :] / blk).astype(jnp.int32)


def _slots(e_k, rank_k, counts, counts_row, blk, n_blocks):
    n = e_k.shape[1]
    full = lambda shape: pl.BlockSpec(shape, lambda: (0,) * len(shape))
    return pl.pallas_call(
        functools.partial(_slots_kernel, blk=blk, n_blocks=n_blocks),
        in_specs=[full((TOP_K, n)), full((TOP_K, n)), full((N_EXPERTS, 1)), full((8, N_EXPERTS))],
        out_specs=[full((TOP_K, n)), full((1, n_blocks)), full((1, 1))],
        out_shape=[jax.ShapeDtypeStruct((TOP_K, n), jnp.int32),
                   jax.ShapeDtypeStruct((1, n_blocks), jnp.int32),
                   jax.ShapeDtypeStruct((1, 1), jnp.int32)],
        compiler_params=pltpu.CompilerParams(vmem_limit_bytes=VMEM_LIMIT),
        name="slots",
    )(e_k, rank_k, counts, counts_row)


def _row_copy(src_ref, src_row, dst_ref, dst_row, sem):
    return pltpu.make_async_copy(src_ref.at[pl.ds(src_row, 1)], dst_ref.at[pl.ds(dst_row, 1)], sem)


def _dest_spec(tt):
    return pl.BlockSpec((None, 1, TOP_K * tt), lambda i: (i, 0, 0), memory_space=pltpu.SMEM)


def _dispatch_kernel(dest_ref, x_ref, zero_ref, xs_ref, sem, *, tt):
    del zero_ref

    def issue(j, _):
        for kk in range(TOP_K):
            _row_copy(x_ref, j, xs_ref, dest_ref[0, kk * tt + j], sem).start()
        return 0

    lax.fori_loop(0, tt, issue, 0)

    def drain(j, _):
        for kk in range(TOP_K):
            _row_copy(x_ref, 0, xs_ref, 0, sem).wait()
        return 0

    lax.fori_loop(0, tt, drain, 0)


def _dispatch(dest_tiles, x2p, cap, tt):
    n, w = x2p.shape
    zeros = jnp.zeros((cap, w), x2p.dtype)
    return pl.pallas_call(
        functools.partial(_dispatch_kernel, tt=tt),
        grid=(n // tt,),
        in_specs=[_dest_spec(tt),
                  pl.BlockSpec((tt, w), lambda i: (i, 0)),
                  pl.BlockSpec(memory_space=pl.ANY)],
        out_specs=pl.BlockSpec(memory_space=pl.ANY),
        scratch_shapes=[pltpu.SemaphoreType.DMA(())],
        out_shape=jax.ShapeDtypeStruct((cap, w), x2p.dtype),
        input_output_aliases={2: 0},
        compiler_params=_params(1),
        name="dispatch",
    )(dest_tiles, x2p, zeros)


def _unpack_halves(p):
    hi = pltpu.bitcast(p & jnp.uint32(0xFFFF0000), F32).astype(BF16)
    lo = pltpu.bitcast(p << 16, F32).astype(BF16)
    return hi, lo


def _experts_kernel(blk_e_ref, nblk_ref, xs_ref, wg_ref, wu_ref, wd_ref, y_ref):
    del blk_e_ref

    @pl.when(pl.program_id(0) < nblk_ref[0])
    def _():
        half = D_MODEL // 2
        hi, lo = _unpack_halves(xs_ref[...])
        gate = _dot(hi, wg_ref[:half, :]) + _dot(lo, wg_ref[half:, :])
        up = _dot(hi, wu_ref[:half, :]) + _dot(lo, wu_ref[half:, :])
        y_ref[...] = _dot((jax.nn.silu(gate) * up).astype(BF16), wd_ref[...])

    @pl.when(pl.program_id(0) >= nblk_ref[0])
    def _():
        y_ref[...] = jnp.zeros_like(y_ref)


def _experts(blk_e, nblk, xs, w_gate, w_up, w_down, blk):
    cap, w = xs.shape
    wg, wu, wd = (a.astype(BF16) for a in (w_gate, w_up, w_down))
    wspec = lambda a: pl.BlockSpec((None,) + a.shape[1:], lambda i, be, nb: (be[i], 0, 0))
    return pl.pallas_call(
        _experts_kernel,
        grid_spec=pltpu.PrefetchScalarGridSpec(
            num_scalar_prefetch=2,
            grid=(cap // blk,),
            in_specs=[pl.BlockSpec((blk, w), lambda i, be, nb: (i, 0)), wspec(wg), wspec(wu), wspec(wd)],
            out_specs=pl.BlockSpec((blk, D_MODEL), lambda i, be, nb: (i, 0)),
        ),
        out_shape=jax.ShapeDtypeStruct((cap, D_MODEL), F32),
        compiler_params=_params(1),
        name="experts",
    )(blk_e, nblk, xs, wg, wu, wd)


def _combine_kernel(dest_ref, x_ref, wk_ref, y_ref, wsg_ref, wsu_ref, wsd_ref, g_ref, b_ref,
                    o_ref, buf_ref, sem, *, tt):
    def issue(j, _):
        for kk in range(TOP_K):
            _row_copy(y_ref, dest_ref[0, kk * tt + j], buf_ref.at[kk], j, sem).start()
        return 0

    lax.fori_loop(0, tt, issue, 0)

    x = x_ref[...]
    xb = x.astype(BF16)
    shared = _dot((jax.nn.silu(_dot(xb, wsg_ref[...])) * _dot(xb, wsu_ref[...])).astype(BF16), wsd_ref[...])

    def drain(j, _):
        for kk in range(TOP_K):
            _row_copy(y_ref, 0, buf_ref.at[kk], 0, sem).wait()
        return 0

    lax.fori_loop(0, tt, drain, 0)

    wk = wk_ref[...]
    routed = buf_ref[0] * wk[:, 0:1]
    for kk in range(1, TOP_K):
        routed = routed + buf_ref[kk] * wk[:, kk:kk + 1]
    o_ref[...] = _layer_norm(DN_ALPHA * x + (routed + shared), g_ref[...], b_ref[...])


def _combine(dest_tiles, x2, w_tok, y, ws_gate, ws_up, ws_down, ln3_g, ln3_b, tt):
    n = x2.shape[0]
    ws = [ws_gate.astype(BF16), ws_up.astype(BF16), ws_down.astype(BF16)]
    vecs = [ln3_g.reshape(1, D_MODEL), ln3_b.reshape(1, D_MODEL)]
    full = lambda a: pl.BlockSpec(a.shape, lambda i: (0,) * a.ndim)
    return pl.pallas_call(
        functools.partial(_combine_kernel, tt=tt),
        grid=(n // tt,),
        in_specs=[_dest_spec(tt),
                  pl.BlockSpec((tt, D_MODEL), lambda i: (i, 0)),
                  pl.BlockSpec((tt, TOP_K), lambda i: (i, 0)),
                  pl.BlockSpec(memory_space=pl.ANY)]
                 + [full(a) for a in ws] + [full(v) for v in vecs],
        out_specs=pl.BlockSpec((tt, D_MODEL), lambda i: (i, 0)),
        scratch_shapes=[pltpu.VMEM((TOP_K, tt, D_MODEL), F32), pltpu.SemaphoreType.DMA(())],
        out_shape=jax.ShapeDtypeStruct((n, D_MODEL), F32),
        compiler_params=_params(1),
        name="combine",
    )(dest_tiles, x2, w_tok, y, *ws, *vecs)


def _moe_and_norm(x2, x2p, w_router, router_bias, w_gate, w_up, w_down, ws_gate, ws_up, ws_down,
                  ln3_g, ln3_b):
    n = x2.shape[0]
    blk = 256
    tt = 128 if n % 128 == 0 else n
    cap = n * TOP_K + N_EXPERTS * blk
    n_blocks = cap // blk
    e_k, rank_k, w_k, counts, counts_row = _router(x2, w_router, router_bias)
    dest, blk_e, nblk = _slots(e_k, rank_k, counts, counts_row, blk, n_blocks)
    dest_tiles = dest.reshape(TOP_K, n // tt, tt).transpose(1, 0, 2).reshape(n // tt, 1, TOP_K * tt)
    xs = _dispatch(dest_tiles, x2p, cap, tt)
    y = _experts(blk_e.reshape(-1), nblk.reshape(-1), xs, w_gate, w_up, w_down, blk)
    return _combine(dest_tiles, x2, w_k.T, y, ws_gate, ws_up, ws_down, ln3_g, ln3_b, tt)


def _layer(x, mem, positions, w_in, cmp_pe_k, cmp_pe_v, cmp_w1_k, cmp_w2_k, cmp_w1_v, cmp_w2_v,
           w_out, ln1_g, ln1_b, w_xq, w_xkv, w_xo, ln2_g, ln2_b, w_router, router_bias,
           w_gate, w_up, w_down, ws_gate, ws_up, ws_down, ln3_g, ln3_b):
    batch, seq, _ = x.shape
    n = batch * seq
    x2d = x.reshape(n, D_MODEL)
    pos_col = positions.astype(F32).reshape(n, 1)
    (rq, rk, rv, rg, nq, nqr, kc, vc, ks, vs, kw, vw, gates) = _inproj(x2d, pos_col, w_in)
    o_ret = _retention(rq, rk, rv, rg, batch, seq)
    kcmp = _compress(kc, cmp_pe_k, cmp_w1_k, cmp_w2_k, batch, seq)
    vcmp = _compress(vc, cmp_pe_v, cmp_w1_v, cmp_w2_v, batch, seq)
    o_nsa = _nsa(nq, nqr, gates, kcmp, vcmp, ks, vs, kw, vw, batch, seq)
    kvx = _memkv(mem.reshape(batch * MEM_LEN, D_MODEL), w_xkv)
    x2, x2p = _postmix(x2d, o_ret, o_nsa, kvx, w_out, w_xq, w_xo, ln1_g, ln1_b, ln2_g, ln2_b, batch, seq)
    out = _moe_and_norm(x2, x2p, w_router, router_bias, w_gate, w_up, w_down,
                        ws_gate, ws_up, ws_down, ln3_g, ln3_b)
    return out.reshape(batch, seq, D_MODEL)


def kernel(x, mem, positions, w_in, cmp_pe_k, cmp_pe_v, cmp_w1_k, cmp_w2_k, cmp_w1_v, cmp_w2_v, w_out, ln1_g, ln1_b, w_xq, w_xkv, w_xo, ln2_g, ln2_b, w_router, router_bias, w_gate, w_up, w_down, ws_gate, ws_up, ws_down, ln3_g, ln3_b):
    for l in range(DEPTH):
        x = _layer(x, mem, positions, w_in[l], cmp_pe_k[l], cmp_pe_v[l], cmp_w1_k[l], cmp_w2_k[l],
                   cmp_w1_v[l], cmp_w2_v[l], w_out[l], ln1_g[l], ln1_b[l], w_xq[l], w_xkv[l],
                   w_xo[l], ln2_g[l], ln2_b[l], w_router[l], router_bias[l], w_gate[l], w_up[l],
                   w_down[l], ws_gate[l], ws_up[l], ws_down[l], ln3_g[l], ln3_b[l])
    return x
```

```python
import functools

import numpy as np
import jax
import jax.numpy as jnp
from jax import lax
from jax.experimental import pallas as pl
from jax.experimental.pallas import tpu as pltpu

D_MODEL = 1024
MEM_LEN = 256
DEPTH = 1
DN_ALPHA = (2 * DEPTH) ** 0.25
LN_EPS = 1e-5
NEG = -1e30
FORCE = 1e9

RET_HEADS = 4
RET_DIM = 128
RET_CHUNK = 128
RET_ROPE_BASE = 10000.0
RET_WIDTH = RET_HEADS * RET_DIM

NSA_HEADS = 8
NSA_KV_GROUPS = 2
NSA_HPG = NSA_HEADS // NSA_KV_GROUPS
NSA_DIM = 64
NSA_WIDTH = NSA_HEADS * NSA_DIM
KV_WIDTH = NSA_KV_GROUPS * NSA_DIM
CMP_LEN = 32
CMP_STRIDE = 16
CMP_HIDDEN = 256
SEL_LEN = 64
SEL_SHIFT = 6
SEL_TOPK = 16
WIN = 512
ROPE_THETA = 500000.0
ROPE_DIMS = NSA_DIM // 4
GATE_LANES = 16
NSA_CHAINS = 2

XATT_HEADS = 4
XATT_DIM = D_MODEL // XATT_HEADS

N_EXPERTS = 64
TOP_K = 8
N_GROUPS = 8
TOPK_GROUPS = 4
EXPERT_FF = 256
SHARED_FF = 256
ROUTED_SCALE = 2.5

LANES = 128
VMEM_LIMIT = 56 * 1024 * 1024

F32 = jnp.float32
BF16 = jnp.bfloat16
NT_DIMS = (((1,), (1,)), ((), ()))


def _params(n_axes):
    return pltpu.CompilerParams(dimension_semantics=("arbitrary",) * n_axes,
                                vmem_limit_bytes=VMEM_LIMIT)


def _dot(a, b):
    return jnp.dot(a, b, preferred_element_type=F32)


def _dot_nt(a, b):
    return lax.dot_general(a, b, NT_DIMS, preferred_element_type=F32)


def _layer_norm(v, g, b):
    mu = jnp.mean(v, axis=-1, keepdims=True)
    d = v - mu
    var = jnp.mean(d * d, axis=-1, keepdims=True)
    return d * lax.rsqrt(var + LN_EPS) * g + b


def _inproj_kernel(x_ref, pos_ref, wret_ref, wnq_ref, wkv_ref, wg_ref, invr_ref, invn_ref,
                   rq_ref, rk_ref, rv_ref, rg_ref, nq_ref, nqr_ref, kc_ref, vc_ref,
                   ks_ref, vs_ref, kw_ref, vw_ref, gate_ref):
    xb = x_ref[...].astype(BF16)
    pos = pos_ref[...]
    lane = lax.broadcasted_iota(jnp.int32, (1, LANES), 1)

    ang = pos * invr_ref[...]
    cos_r = jnp.cos(ang)
    sin_r = jnp.sin(ang)
    sin_r = jnp.where(lane < RET_DIM // 2, -sin_r, sin_r)
    for h in range(RET_HEADS):
        cols = slice(h * RET_DIM, (h + 1) * RET_DIM)
        q = _dot(xb, wret_ref[:, cols])
        rq_ref[:, cols] = (q * cos_r + pltpu.roll(q, RET_DIM // 2, 1) * sin_r).astype(BF16)
        k = _dot(xb, wret_ref[:, RET_WIDTH + h * RET_DIM:RET_WIDTH + (h + 1) * RET_DIM])
        k = (k * cos_r + pltpu.roll(k, RET_DIM // 2, 1) * sin_r) * (RET_DIM ** -0.5)
        rk_ref[:, cols] = k.astype(BF16)
    rv_ref[...] = _dot(xb, wret_ref[:, 2 * RET_WIDTH:3 * RET_WIDTH]).astype(BF16)
    rg_ref[...] = _dot(xb, wret_ref[:, 3 * RET_WIDTH:4 * RET_WIDTH]).astype(BF16)

    half = ROPE_DIMS // 2
    j = lane % NSA_DIM
    angn = pos * invn_ref[...]
    cos_n = jnp.cos(angn)
    sin_n = jnp.sin(angn)
    sin_lo = jnp.where(j < half, -sin_n, 0.0)
    sin_hi = jnp.where((j >= half) & (j < 2 * half), sin_n, 0.0)

    def rope_n(v):
        return v * cos_n + pltpu.roll(v, half, 1) * sin_hi + pltpu.roll(v, LANES - half, 1) * sin_lo

    scale = NSA_DIM ** -0.5
    for c in range(NSA_WIDTH // LANES):
        cols = slice(c * LANES, (c + 1) * LANES)
        q = _dot(xb, wnq_ref[:, cols])
        nq_ref[:, cols] = (q * scale).astype(BF16)
        nqr_ref[:, cols] = (rope_n(q) * scale).astype(BF16)

    def kv(i):
        return _dot(xb, wkv_ref[:, i * KV_WIDTH:(i + 1) * KV_WIDTH])

    def split_groups(ref, v):
        for g in range(NSA_KV_GROUPS):
            ref[g] = v[:, g * NSA_DIM:(g + 1) * NSA_DIM].astype(BF16)

    kc_ref[...] = kv(0)
    vc_ref[...] = kv(1)
    split_groups(ks_ref, rope_n(kv(2)))
    split_groups(vs_ref, kv(3))
    split_groups(kw_ref, rope_n(kv(4)))
    split_groups(vw_ref, kv(5))

    gt = jax.nn.sigmoid(_dot_nt(wg_ref[...], xb))
    for g in range(NSA_KV_GROUPS):
        gate_ref[g] = gt[g * GATE_LANES:(g + 1) * GATE_LANES, :]


def _inproj(x2d, pos_col, w_in):
    n = x2d.shape[0]
    tm = 512 if n % 512 == 0 else n
    off = np.cumsum([0] + [RET_WIDTH] * 4 + [NSA_WIDTH] + [KV_WIDTH] * 6)
    w_ret = w_in[:, :off[4]].astype(BF16)
    w_nq = w_in[:, off[4]:off[5]].astype(BF16)
    w_kv = w_in[:, off[5]:off[11]].astype(BF16)
    wg = w_in[:, off[11]:].reshape(D_MODEL, NSA_KV_GROUPS, NSA_HPG * 3)
    wg = jnp.pad(wg, ((0, 0), (0, 0), (0, GATE_LANES - NSA_HPG * 3)))
    wg = wg.reshape(D_MODEL, NSA_KV_GROUPS * GATE_LANES).T.astype(BF16)

    lane = np.arange(LANES)
    half_r = RET_DIM // 2
    inv_r = (np.float32(RET_ROPE_BASE) ** (-np.arange(half_r, dtype=np.float32) / np.float32(half_r)))
    inv_r = inv_r.astype(np.float32)[lane % half_r][None, :]
    half_n = ROPE_DIMS // 2
    inv_n = (np.float32(ROPE_THETA) ** (-np.arange(half_n, dtype=np.float32) / np.float32(half_n)))
    jn = lane % NSA_DIM
    inv_n = np.where(jn < ROPE_DIMS, inv_n.astype(np.float32)[jn % half_n], np.float32(0.0))[None, :]

    row = lambda w: pl.BlockSpec((tm, w), lambda i: (i, 0))
    full = lambda a: pl.BlockSpec(a.shape, lambda i: (0,) * a.ndim)
    grp = lambda w: pl.BlockSpec((NSA_KV_GROUPS, tm, w), lambda i: (0, i, 0))
    bf = lambda w: jax.ShapeDtypeStruct((n, w), BF16)
    gbf = jax.ShapeDtypeStruct((NSA_KV_GROUPS, n, NSA_DIM), BF16)
    inv_r = jnp.asarray(inv_r, F32)
    inv_n = jnp.asarray(inv_n, F32)
    return pl.pallas_call(
        _inproj_kernel,
        grid=(n // tm,),
        in_specs=[row(D_MODEL), row(1), full(w_ret), full(w_nq), full(w_kv), full(wg),
                  full(inv_r), full(inv_n)],
        out_specs=[row(RET_WIDTH)] * 4 + [row(NSA_WIDTH)] * 2 + [row(KV_WIDTH)] * 2
                  + [grp(NSA_DIM)] * 4
                  + [pl.BlockSpec((NSA_KV_GROUPS, GATE_LANES, tm), lambda i: (0, 0, i))],
        out_shape=[bf(RET_WIDTH)] * 4 + [bf(NSA_WIDTH)] * 2
                  + [jax.ShapeDtypeStruct((n, KV_WIDTH), F32)] * 2 + [gbf] * 4
                  + [jax.ShapeDtypeStruct((NSA_KV_GROUPS, GATE_LANES, n), F32)],
        compiler_params=_params(1),
        name="inproj",
    )(x2d, pos_col, w_ret, w_nq, w_kv, wg, inv_r, inv_n)


def _retention_kernel(q_ref, k_ref, v_ref, g_ref, o_ref, state_ref):
    c = RET_CHUNK

    @pl.when(pl.program_id(1) == 0)
    def _():
        state_ref[...] = jnp.zeros_like(state_ref)

    row = lax.broadcasted_iota(jnp.int32, (c, c), 0)
    col = lax.broadcasted_iota(jnp.int32, (c, c), 1)
    rel = (row - col).astype(F32)
    idx = lax.broadcasted_iota(jnp.int32, (c, 1), 0).astype(F32)
    for h in range(RET_HEADS):
        log_g = float(np.log(np.float32(1.0) - np.float32(2.0) ** np.float32(-5.0 - h)))
        cols = slice(h * RET_DIM, (h + 1) * RET_DIM)
        q = q_ref[:, cols]
        k = k_ref[:, cols]
        v = v_ref[:, cols]
        dmask = jnp.where(rel >= 0, jnp.exp(log_g * jnp.maximum(rel, 0.0)), 0.0)
        scores = _dot_nt(q, k) * dmask
        inner = _dot(scores.astype(BF16), v)
        zeta = jnp.exp(log_g * (c - 1.0 - idx))
        xi = jnp.exp(log_g * (idx + 1.0))
        prev = state_ref[h]
        cross = _dot(q, prev.astype(BF16)) * xi
        kz = (k.astype(F32) * zeta).astype(BF16)
        kv = lax.dot_general(kz, v, (((0,), (0,)), ((), ())), preferred_element_type=F32)
        state_ref[h] = prev * float(np.exp(np.float32(log_g) * np.float32(c))) + kv
        o = inner + cross
        mu = jnp.mean(o, axis=-1, keepdims=True)
        d = o - mu
        var = jnp.mean(d * d, axis=-1, keepdims=True)
        o = d * lax.rsqrt(var + LN_EPS)
        o_ref[:, cols] = (jax.nn.silu(g_ref[:, cols].astype(F32)) * o).astype(BF16)


def _retention(rq, rk, rv, rg, batch, seq):
    nc = seq // RET_CHUNK
    spec = pl.BlockSpec((RET_CHUNK, RET_WIDTH), lambda b, n: (b * nc + n, 0))
    return pl.pallas_call(
        _retention_kernel,
        grid=(batch, nc),
        in_specs=[spec] * 4,
        out_specs=spec,
        out_shape=jax.ShapeDtypeStruct(rq.shape, BF16),
        scratch_shapes=[pltpu.VMEM((RET_HEADS, RET_DIM, RET_DIM), F32)],
        compiler_params=_params(2),
        name="retention",
    )(rq, rk, rv, rg)


def _compress_kernel(a_ref, pe_ref, w1_ref, w2_ref, o_ref, shift_ref, *, n_cmp):
    rows = a_ref.shape[0]
    a = a_ref[...]
    lo = (a + pe_ref[0]).astype(BF16)
    hi = (a + pe_ref[1]).astype(BF16)
    ridx = lax.broadcasted_iota(jnp.int32, (rows, 1), 0)
    shift_ref[rows:rows + 8, :] = jnp.zeros((8, CMP_HIDDEN), F32)
    for g in range(NSA_KV_GROUPS):
        p = _dot(lo, w1_ref[0, g])
        shift_ref[0:rows, :] = _dot(hi, w1_ref[1, g])
        hid = jax.nn.silu(p + shift_ref[pl.ds(1, rows), :])
        out = _dot(hid.astype(BF16), w2_ref[...])
        o_ref[g] = jnp.where(ridx < n_cmp, out, 0.0).astype(BF16)


def _compress(a, pe, w1, w2, batch, seq):
    rows = seq // CMP_STRIDE
    per = CMP_STRIDE * KV_WIDTH
    n_cmp = (seq - CMP_LEN) // CMP_STRIDE + 1
    a2 = a.reshape(batch * rows, per)
    pe2 = jnp.tile(pe.reshape(2, CMP_STRIDE, 1, NSA_DIM), (1, 1, NSA_KV_GROUPS, 1)).reshape(2, 1, per)
    w1r = w1.reshape(2, CMP_STRIDE, 1, NSA_DIM, CMP_HIDDEN)
    eye = jnp.eye(NSA_KV_GROUPS, dtype=w1.dtype).reshape(1, NSA_KV_GROUPS, 1, NSA_KV_GROUPS, 1, 1)
    w1x = (w1r[:, None] * eye).reshape(2, NSA_KV_GROUPS, per, CMP_HIDDEN).astype(BF16)
    w2b = w2.astype(BF16)
    full = lambda arr: pl.BlockSpec(arr.shape, lambda b: (0,) * arr.ndim)
    return pl.pallas_call(
        functools.partial(_compress_kernel, n_cmp=n_cmp),
        grid=(batch,),
        in_specs=[pl.BlockSpec((rows, per), lambda b: (b, 0)), full(pe2), full(w1x), full(w2b)],
        out_specs=pl.BlockSpec((None, NSA_KV_GROUPS, rows, NSA_DIM), lambda b: (b, 0, 0, 0)),
        out_shape=jax.ShapeDtypeStruct((batch, NSA_KV_GROUPS, rows, NSA_DIM), BF16),
        scratch_shapes=[pltpu.VMEM((rows + 8, CMP_HIDDEN), F32)],
        compiler_params=_params(1),
        name="compress",
    )(a2, pe2, w1x, w2b)


def _heads_to_lanes(ref):
    vt = ref[...].astype(F32).T
    return jnp.concatenate([vt[h * NSA_DIM:(h + 1) * NSA_DIM] for h in range(NSA_HPG)], axis=1).astype(BF16)


def _tile_heads(v):
    return jnp.concatenate([v] * NSA_HPG, axis=1)


def _transpose_into(dst_ref, src_ref, chunk):
    def step(c, _):
        c0 = pl.multiple_of(c * chunk, chunk)
        dst_ref[:, pl.ds(c0, chunk)] = src_ref[pl.ds(c0, chunk), :].astype(F32).T.astype(BF16)
        return 0
    lax.fori_loop(0, src_ref.shape[0] // chunk, step, 0)


def _nsa_kernel(qraw_ref, qrot_ref, gate_ref, kcmp_ref, vcmp_ref, ovt_ref,
                ks_ref, vs_ref, kw_ref, vw_ref, o_ref, vst_ref, vwt_ref, vct_ref, bias_ref, *, tq, tk, seq):
    i = pl.program_id(2)
    t0 = i * tq
    cols = NSA_HPG * tq
    n_sel = seq // SEL_LEN
    n_cmp_rows = seq // CMP_STRIDE

    @pl.when(i == 0)
    def _():
        chunk = min(512, n_cmp_rows)
        _transpose_into(vst_ref, vs_ref, chunk)
        _transpose_into(vwt_ref, vw_ref, chunk)
        _transpose_into(vct_ref, vcmp_ref, chunk)

    q_raw = _heads_to_lanes(qraw_ref)
    q_rot = _heads_to_lanes(qrot_ref)
    t_row = t0 + lax.broadcasted_iota(jnp.int32, (1, tq), 1)

    chain_w = cols // NSA_CHAINS
    heads_per_chain = chain_w // tq
    chains = [slice(c * chain_w, (c + 1) * chain_w) for c in range(NSA_CHAINS)]
    tile_chain = lambda v: jnp.concatenate([v] * heads_per_chain, axis=1)

    span = WIN + tq
    ws = pl.multiple_of(jnp.maximum(t0 - WIN, 0), tq)
    dist = t_row - (ws + lax.broadcasted_iota(jnp.int32, (span, 1), 0))
    bias_w = tile_chain(jnp.where((dist >= 0) & (dist < WIN), 0.0, NEG))
    k_w = kw_ref[pl.ds(ws, span), :]
    v_w = vwt_ref[:, pl.ds(ws, span)]
    o_w = []
    for c in chains:
        s_w = _dot(k_w, q_rot[:, c]) + bias_w
        p_w = jnp.exp(s_w - jnp.max(s_w, axis=0, keepdims=True))
        o_w.append(_dot(v_w, p_w.astype(BF16)) / jnp.sum(p_w, axis=0, keepdims=True))
    o_w = jnp.concatenate(o_w, axis=1)

    c_idx = lax.broadcasted_iota(jnp.int32, (n_cmp_rows, 1), 0)
    valid = tile_chain(jnp.where(c_idx * CMP_STRIDE + (CMP_LEN - 1) <= t_row, 1.0, 0.0))
    bias_c = (valid - 1.0) * (-NEG)
    o_c = []
    p_sum = None
    for c in chains:
        s_c = _dot(kcmp_ref[...], q_raw[:, c]) + bias_c
        e_c = jnp.exp(s_c - jnp.max(s_c, axis=0, keepdims=True)) * valid
        l_c = jnp.sum(e_c, axis=0, keepdims=True)
        p_c = e_c / jnp.where(l_c > 0.0, l_c, 1.0)
        o_c.append(_dot(vct_ref[...], p_c.astype(BF16)))
        for h in range(heads_per_chain):
            p_h = p_c[:, h * tq:(h + 1) * tq]
            p_sum = p_h if p_sum is None else p_sum + p_h
    o_c = jnp.concatenate(o_c, axis=1)

    p_hi = p_sum.astype(BF16)
    p_lo = (p_sum - p_hi.astype(F32)).astype(BF16)
    ovt = ovt_ref[...]
    imp = _dot(ovt, p_hi) + _dot(ovt, p_lo)
    jb = lax.broadcasted_iota(jnp.int32, (n_sel, tq), 0)
    cur = (t0 + lax.broadcasted_iota(jnp.int32, (n_sel, tq), 1)) >> SEL_SHIFT
    forced = (jb == 0) | (jb == cur) | (jb == cur - 1)
    work = jnp.where(forced, FORCE, imp)
    work = jnp.where(jb <= cur, work, NEG)
    sel_t = jnp.zeros((n_sel, tq), F32)
    for _ in range(min(SEL_TOPK, n_sel)):
        best = jnp.max(work, axis=0, keepdims=True)
        first = jnp.min(jnp.where(work == best, jb, n_sel), axis=0, keepdims=True)
        hit = jb == first
        sel_t = jnp.where(hit, 1.0, sel_t)
        work = jnp.where(hit, -jnp.inf, work)
    bias_ref[...] = jnp.where(sel_t > 0.5, 0.0, NEG)

    blocks_per_tile = tk // SEL_LEN

    def sel_tile(kt, carry, causal):
        k0 = pl.multiple_of(kt * tk, tk)
        bias = jnp.concatenate(
            [jnp.broadcast_to(bias_ref[pl.ds(kt * blocks_per_tile + j, 1), :], (SEL_LEN, tq))
             for j in range(blocks_per_tile)], axis=0)
        if causal:
            kpos = k0 + lax.broadcasted_iota(jnp.int32, (tk, 1), 0)
            bias = jnp.where(kpos <= t_row, bias, NEG)
        bias = tile_chain(bias)
        k_t = ks_ref[pl.ds(k0, tk), :]
        v_t = vst_ref[:, pl.ds(k0, tk)]
        out = []
        scores = [_dot(k_t, q_rot[:, c]) + bias for c in chains]
        for (m, l, acc), s in zip(carry, scores):
            m_new = jnp.maximum(m, jnp.max(s, axis=0, keepdims=True))
            alpha = jnp.exp(m - m_new)
            p = jnp.exp(s - m_new)
            l = alpha * l + jnp.sum(p, axis=0, keepdims=True)
            acc = alpha * acc + _dot(v_t, p.astype(BF16))
            out.append((m_new, l, acc))
        return tuple(out)

    n_full = t0 // tk
    init = tuple((jnp.full((1, chain_w), NEG, F32), jnp.zeros((1, chain_w), F32),
                  jnp.zeros((NSA_DIM, chain_w), F32)) for _ in chains)
    carry = lax.fori_loop(0, n_full, functools.partial(sel_tile, causal=False), init)
    carry = sel_tile(n_full, carry, causal=True)
    o_s = jnp.concatenate([acc / l for _, l, acc in carry], axis=1)

    gt = gate_ref[...]
    outs = []
    for h in range(NSA_HPG):
        c = slice(h * tq, (h + 1) * tq)
        outs.append(gt[3 * h:3 * h + 1] * o_c[:, c] + gt[3 * h + 1:3 * h + 2] * o_s[:, c]
                    + gt[3 * h + 2:3 * h + 3] * o_w[:, c])
    o_ref[...] = jnp.concatenate(outs, axis=0).T.astype(BF16)


def _nsa(nq, nqr, gates, kcmp, vcmp, ks, vs, kw, vw, batch, seq):
    n = batch * seq
    tq = 512
    tk = 512 if seq % 512 == 0 else seq
    nqb = seq // tq
    n_sel = seq // SEL_LEN
    rows_c = seq // CMP_STRIDE
    gw = NSA_HPG * NSA_DIM
    cs = np.arange(rows_c)[None, :] * CMP_STRIDE
    ss = np.arange(n_sel)[:, None] * SEL_LEN
    n_cmp = (seq - CMP_LEN) // CMP_STRIDE + 1
    ovt = ((cs < ss + SEL_LEN) & (cs + CMP_LEN > ss) & (np.arange(rows_c)[None, :] < n_cmp))
    ovt = jnp.asarray(ovt.astype(np.float32), BF16)

    qspec = pl.BlockSpec((tq, gw), lambda b, g, i: (b * nqb + i, g))
    cspec = pl.BlockSpec((None, None, rows_c, NSA_DIM), lambda b, g, i: (b, g, 0, 0))
    kspec = pl.BlockSpec((None, seq, NSA_DIM), lambda b, g, i: (g, b, 0))
    return pl.pallas_call(
        functools.partial(_nsa_kernel, tq=tq, tk=tk, seq=seq),
        grid=(batch, NSA_KV_GROUPS, nqb),
        in_specs=[qspec, qspec,
                  pl.BlockSpec((None, GATE_LANES, tq), lambda b, g, i: (g, 0, b * nqb + i)),
                  cspec, cspec, pl.BlockSpec(ovt.shape, lambda b, g, i: (0, 0)),
                  kspec, kspec, kspec, kspec],
        out_specs=qspec,
        out_shape=jax.ShapeDtypeStruct((n, NSA_WIDTH), BF16),
        scratch_shapes=[pltpu.VMEM((NSA_DIM, seq), BF16), pltpu.VMEM((NSA_DIM, seq), BF16),
                        pltpu.VMEM((NSA_DIM, rows_c), BF16), pltpu.VMEM((n_sel, tq), F32)],
        compiler_params=_params(3),
        name="nsa",
    )(nq, nqr, gates, kcmp, vcmp, ovt, ks, vs, kw, vw)


def _memkv_kernel(mem_ref, w_ref, kv_ref):
    kv_ref[...] = _dot(mem_ref[...].astype(BF16), w_ref[...]).astype(BF16)


def _memkv(mem2d, w_xkv):
    n = mem2d.shape[0]
    w = w_xkv.astype(BF16)
    return pl.pallas_call(
        _memkv_kernel,
        grid=(n // MEM_LEN,),
        in_specs=[pl.BlockSpec((MEM_LEN, D_MODEL), lambda i: (i, 0)),
                  pl.BlockSpec(w.shape, lambda i: (0, 0))],
        out_specs=pl.BlockSpec((MEM_LEN, 2 * D_MODEL), lambda i: (i, 0)),
        out_shape=jax.ShapeDtypeStruct((n, 2 * D_MODEL), BF16),
        compiler_params=_params(1),
        name="memkv",
    )(mem2d, w)


def _postmix_kernel(x_ref, oret_ref, onsa_ref, kv_ref, wout_ref, wq_ref, wo_ref,
                    g1_ref, b1_ref, g2_ref, b2_ref, x2_ref, x2p_ref):
    mixed = jnp.concatenate([oret_ref[...], onsa_ref[...]], axis=1)
    x1 = _layer_norm(DN_ALPHA * x_ref[...] + _dot(mixed, wout_ref[...]), g1_ref[...], b1_ref[...])
    q = (_dot(x1.astype(BF16), wq_ref[...]) * (XATT_DIM ** -0.5)).astype(BF16)
    heads = []
    for h in range(XATT_HEADS):
        cols = slice(h * XATT_DIM, (h + 1) * XATT_DIM)
        s = _dot_nt(q[:, cols], kv_ref[:, cols])
        m = jnp.max(s, axis=-1, keepdims=True)
        p = jnp.exp(s - m)
        l = jnp.sum(p, axis=-1, keepdims=True)
        heads.append(_dot(p.astype(BF16), kv_ref[:, D_MODEL + h * XATT_DIM:D_MODEL + (h + 1) * XATT_DIM]) / l)
    att = jnp.concatenate(heads, axis=1).astype(BF16)
    x2 = _layer_norm(DN_ALPHA * x1 + _dot(att, wo_ref[...]), g2_ref[...], b2_ref[...])
    x2_ref[...] = x2
    half = D_MODEL // 2
    hi = pltpu.bitcast(x2[:, :half].astype(BF16).astype(F32), jnp.uint32)
    lo = pltpu.bitcast(x2[:, half:].astype(BF16).astype(F32), jnp.uint32)
    x2p_ref[...] = hi | (lo >> 16)


def _postmix(x2d, o_ret, o_nsa, kvx, w_out, w_xq, w_xo, ln1_g, ln1_b, ln2_g, ln2_b, batch, seq):
    n = x2d.shape[0]
    tm = 256 if seq % 256 == 0 else seq
    per_b = seq // tm
    row = lambda w: pl.BlockSpec((tm, w), lambda b, i: (b * per_b + i, 0))
    full = lambda a: pl.BlockSpec(a.shape, lambda b, i: (0,) * a.ndim)
    ws = [w_out.astype(BF16), w_xq.astype(BF16), w_xo.astype(BF16)]
    vecs = [v.reshape(1, D_MODEL) for v in (ln1_g, ln1_b, ln2_g, ln2_b)]
    return pl.pallas_call(
        _postmix_kernel,
        grid=(batch, per_b),
        in_specs=[row(D_MODEL), row(RET_WIDTH), row(NSA_WIDTH),
                  pl.BlockSpec((MEM_LEN, 2 * D_MODEL), lambda b, i: (b, 0))]
                 + [full(w) for w in ws] + [full(v) for v in vecs],
        out_specs=[row(D_MODEL), row(D_MODEL // 2)],
        out_shape=[jax.ShapeDtypeStruct((n, D_MODEL), F32),
                   jax.ShapeDtypeStruct((n, D_MODEL // 2), jnp.uint32)],
        compiler_params=_params(2),
        name="postmix",
    )(x2d, o_ret, o_nsa, kvx, *ws, *vecs)


def _router_kernel(x_ref, wr_ref, bias_ref, e_ref, rank_ref, w_ref, cnt_ref, cntrow_ref, carry_ref, carryrow_ref):
    tn = x_ref.shape[0]
    per = N_EXPERTS // N_GROUPS

    @pl.when(pl.program_id(0) == 0)
    def _():
        carry_ref[...] = jnp.zeros_like(carry_ref)
        carryrow_ref[...] = jnp.zeros_like(carryrow_ref)

    logits = _dot_nt(wr_ref[...], x_ref[...].astype(BF16))
    scores = jax.nn.sigmoid(logits)
    biased = scores + bias_ref[...]
    b3 = biased.reshape(N_GROUPS, per, tn)
    member = lax.broadcasted_iota(jnp.int32, (N_GROUPS, per, tn), 1)
    top1 = jnp.max(b3, axis=1, keepdims=True)
    first1 = jnp.min(jnp.where(b3 == top1, member, per), axis=1, keepdims=True)
    top2 = jnp.max(jnp.where(member == first1, -jnp.inf, b3), axis=1, keepdims=True)
    gscore = top1 + top2
    gidx = lax.broadcasted_iota(jnp.int32, (N_GROUPS, 1, tn), 0)
    gwork = gscore
    for _ in range(TOPK_GROUPS - 1):
        gbest = jnp.max(gwork, axis=0, keepdims=True)
        gfirst = jnp.min(jnp.where(gwork == gbest, gidx, N_GROUPS), axis=0, keepdims=True)
        gwork = jnp.where(gidx == gfirst, -jnp.inf, gwork)
    kth = jnp.max(gwork, axis=0, keepdims=True)
    work = jnp.where(gscore >= kth, b3, NEG).reshape(N_EXPERTS, tn)
    eidx = lax.broadcasted_iota(jnp.int32, (N_EXPERTS, tn), 0)
    picks = []
    chosen = jnp.zeros((N_EXPERTS, tn), F32)
    for _ in range(TOP_K):
        best = jnp.max(work, axis=0, keepdims=True)
        first = jnp.min(jnp.where(work == best, eidx, N_EXPERTS), axis=0, keepdims=True)
        hit = eidx == first
        picks.append((first, hit))
        chosen = jnp.where(hit, 1.0, chosen)
        work = jnp.where(hit, -jnp.inf, work)

    r_i = lax.broadcasted_iota(jnp.int32, (tn, tn), 0)
    c_i = lax.broadcasted_iota(jnp.int32, (tn, tn), 1)
    before = jnp.where(r_i < c_i, 1.0, 0.0).astype(BF16)
    chosen_b = chosen.astype(BF16)
    rank = _dot(chosen_b, before) + carry_ref[...]
    carry_ref[...] = carry_ref[...] + jnp.sum(chosen, axis=1, keepdims=True)
    carryrow_ref[...] = carryrow_ref[...] + _dot_nt(jnp.ones((8, tn), BF16), chosen_b)
    cnt_ref[...] = carry_ref[...]
    cntrow_ref[...] = carryrow_ref[...]

    wsel = [jnp.sum(jnp.where(hit, scores, 0.0), axis=0, keepdims=True) for _, hit in picks]
    wsum = wsel[0]
    for v in wsel[1:]:
        wsum = wsum + v
    for kk, (first, hit) in enumerate(picks):
        e_ref[kk:kk + 1, :] = first
        rank_ref[kk:kk + 1, :] = jnp.sum(jnp.where(hit, rank, 0.0), axis=0, keepdims=True).astype(jnp.int32)
        w_ref[kk:kk + 1, :] = wsel[kk] / wsum * ROUTED_SCALE


def _router(x2, w_router, router_bias):
    n = x2.shape[0]
    tn = 512 if n % 512 == 0 else n
    wr_t = w_router.T.astype(BF16)
    bias = router_bias.reshape(N_EXPERTS, 1).astype(F32)
    kspec = pl.BlockSpec((TOP_K, tn), lambda i: (0, i))
    return pl.pallas_call(
        _router_kernel,
        grid=(n // tn,),
        in_specs=[pl.BlockSpec((tn, D_MODEL), lambda i: (i, 0)),
                  pl.BlockSpec(wr_t.shape, lambda i: (0, 0)),
                  pl.BlockSpec(bias.shape, lambda i: (0, 0))],
        out_specs=[kspec, kspec, kspec, pl.BlockSpec((N_EXPERTS, 1), lambda i: (0, 0)),
                   pl.BlockSpec((8, N_EXPERTS), lambda i: (0, 0))],
        out_shape=[jax.ShapeDtypeStruct((TOP_K, n), jnp.int32),
                   jax.ShapeDtypeStruct((TOP_K, n), jnp.int32),
                   jax.ShapeDtypeStruct((TOP_K, n), F32),
                   jax.ShapeDtypeStruct((N_EXPERTS, 1), F32),
                   jax.ShapeDtypeStruct((8, N_EXPERTS), F32)],
        scratch_shapes=[pltpu.VMEM((N_EXPERTS, 1), F32), pltpu.VMEM((8, N_EXPERTS), F32)],
        compiler_params=_params(1),
        name="router",
    )(x2, wr_t, bias)


def _slots_kernel(e_ref, rank_ref, cnt_ref, cntrow_ref, dest_ref, blk_e_ref, nblk_ref, *, blk, n_blocks):
    pad = lambda c: jnp.ceil(c / blk) * blk
    padded = pad(cnt_ref[...])
    padded_row = pad(cntrow_ref[0:1, :])
    r_i = lax.broadcasted_iota(jnp.int32, (N_EXPERTS, N_EXPERTS), 0)
    c_i = lax.broadcasted_iota(jnp.int32, (N_EXPERTS, N_EXPERTS), 1)
    start = jnp.sum(jnp.where(c_i < r_i, padded_row, 0.0), axis=1, keepdims=True)
    end = start + padded
    e = e_ref[...]
    dest = rank_ref[...]
    for ex in range(N_EXPERTS):
        dest = dest + jnp.where(e == ex, start[ex:ex + 1, :].astype(jnp.int32), 0)
    dest_ref[...] = dest
    bstart = (lax.broadcasted_iota(jnp.int32, (1, n_blocks), 1) * blk).astype(F32)
    owner = jnp.sum(jnp.where(end <= bstart, 1.0, 0.0), axis=0, keepdims=True)
    blk_e_ref[...] = jnp.minimum(owner, N_EXPERTS - 1.0).astype(jnp.int32)
    nblk_ref[...] = (end[N_EXPERTS - 1:N_EXPERTS, :] / blk).astype(jnp.int32)


def _slots(e_k, rank_k, counts, counts_row, blk, n_blocks):
    n = e_k.shape[1]
    full = lambda shape: pl.BlockSpec(shape, lambda: (0,) * len(shape))
    return pl.pallas_call(
        functools.partial(_slots_kernel, blk=blk, n_blocks=n_blocks),
        in_specs=[full((TOP_K, n)), full((TOP_K, n)), full((N_EXPERTS, 1)), full((8, N_EXPERTS))],
        out_specs=[full((TOP_K, n)), full((1, n_blocks)), full((1, 1))],
        out_shape=[jax.ShapeDtypeStruct((TOP_K, n), jnp.int32),
                   jax.ShapeDtypeStruct((1, n_blocks), jnp.int32),
                   jax.ShapeDtypeStruct((1, 1), jnp.int32)],
        compiler_params=pltpu.CompilerParams(vmem_limit_bytes=VMEM_LIMIT),
        name="slots",
    )(e_k, rank_k, counts, counts_row)


def _row_copy(src_ref, src_row, dst_ref, dst_row, sem):
    return pltpu.make_async_copy(src_ref.at[pl.ds(src_row, 1)], dst_ref.at[pl.ds(dst_row, 1)], sem)


def _dest_spec(tt):
    return pl.BlockSpec((None, 1, TOP_K * tt), lambda i: (i, 0, 0), memory_space=pltpu.SMEM)


def _dispatch_kernel(dest_ref, x_ref, zero_ref, xs_ref, sem, *, tt):
    del zero_ref

    def issue(j, _):
        for kk in range(TOP_K):
            _row_copy(x_ref, j, xs_ref, dest_ref[0, kk * tt + j], sem).start(priority=kk % 2)
        return 0

    lax.fori_loop(0, tt, issue, 0)

    def drain(j, _):
        for kk in range(TOP_K):
            _row_copy(x_ref, 0, xs_ref, 0, sem).wait()
        return 0

    lax.fori_loop(0, tt, drain, 0)


def _dispatch(dest_tiles, x2p, cap, tt):
    n, w = x2p.shape
    zeros = jnp.zeros((cap, w), x2p.dtype)
    return pl.pallas_call(
        functools.partial(_dispatch_kernel, tt=tt),
        grid=(n // tt,),
        in_specs=[_dest_spec(tt),
                  pl.BlockSpec((tt, w), lambda i: (i, 0)),
                  pl.BlockSpec(memory_space=pl.ANY)],
        out_specs=pl.BlockSpec(memory_space=pl.ANY),
        scratch_shapes=[pltpu.SemaphoreType.DMA(())],
        out_shape=jax.ShapeDtypeStruct((cap, w), x2p.dtype),
        input_output_aliases={2: 0},
        compiler_params=_params(1),
        name="dispatch",
    )(dest_tiles, x2p, zeros)


def _unpack_halves(p):
    hi = pltpu.bitcast(p & jnp.uint32(0xFFFF0000), F32).astype(BF16)
    lo = pltpu.bitcast(p << 16, F32).astype(BF16)
    return hi, lo


def _experts_kernel(blk_e_ref, nblk_ref, xs_ref, wg_ref, wu_ref, wd_ref, y_ref):
    del blk_e_ref

    @pl.when(pl.program_id(0) < nblk_ref[0])
    def _():
        half = D_MODEL // 2
        hi, lo = _unpack_halves(xs_ref[...])
        gate = _dot(hi, wg_ref[:half, :]) + _dot(lo, wg_ref[half:, :])
        up = _dot(hi, wu_ref[:half, :]) + _dot(lo, wu_ref[half:, :])
        y_ref[...] = _dot((jax.nn.silu(gate) * up).astype(BF16), wd_ref[...])

    @pl.when(pl.program_id(0) >= nblk_ref[0])
    def _():
        y_ref[...] = jnp.zeros_like(y_ref)


def _experts(blk_e, nblk, xs, w_gate, w_up, w_down, blk):
    cap, w = xs.shape
    wg, wu, wd = (a.astype(BF16) for a in (w_gate, w_up, w_down))
    wspec = lambda a: pl.BlockSpec((None,) + a.shape[1:], lambda i, be, nb: (be[i], 0, 0))
    return pl.pallas_call(
        _experts_kernel,
        grid_spec=pltpu.PrefetchScalarGridSpec(
            num_scalar_prefetch=2,
            grid=(cap // blk,),
            in_specs=[pl.BlockSpec((blk, w), lambda i, be, nb: (i, 0)), wspec(wg), wspec(wu), wspec(wd)],
            out_specs=pl.BlockSpec((blk, D_MODEL), lambda i, be, nb: (i, 0)),
        ),
        out_shape=jax.ShapeDtypeStruct((cap, D_MODEL), F32),
        compiler_params=_params(1),
        name="experts",
    )(blk_e, nblk, xs, wg, wu, wd)


def _combine_kernel(dest_ref, x_ref, wk_ref, y_ref, wsg_ref, wsu_ref, wsd_ref, g_ref, b_ref,
                    o_ref, buf_ref, sem, *, tt):
    def issue(j, _):
        for kk in range(TOP_K):
            _row_copy(y_ref, dest_ref[0, kk * tt + j], buf_ref.at[kk], j, sem).start(priority=kk % 2)
        return 0

    lax.fori_loop(0, tt, issue, 0)

    x = x_ref[...]
    xb = x.astype(BF16)
    shared = _dot((jax.nn.silu(_dot(xb, wsg_ref[...])) * _dot(xb, wsu_ref[...])).astype(BF16), wsd_ref[...])

    def drain(j, _):
        for kk in range(TOP_K):
            _row_copy(y_ref, 0, buf_ref.at[kk], 0, sem).wait()
        return 0

    lax.fori_loop(0, tt, drain, 0)

    wk = wk_ref[...]
    routed = buf_ref[0] * wk[:, 0:1]
    for kk in range(1, TOP_K):
        routed = routed + buf_ref[kk] * wk[:, kk:kk + 1]
    o_ref[...] = _layer_norm(DN_ALPHA * x + (routed + shared), g_ref[...], b_ref[...])


def _combine(dest_tiles, x2, w_tok, y, ws_gate, ws_up, ws_down, ln3_g, ln3_b, tt):
    n = x2.shape[0]
    ws = [ws_gate.astype(BF16), ws_up.astype(BF16), ws_down.astype(BF16)]
    vecs = [ln3_g.reshape(1, D_MODEL), ln3_b.reshape(1, D_MODEL)]
    full = lambda a: pl.BlockSpec(a.shape, lambda i: (0,) * a.ndim)
    return pl.pallas_call(
        functools.partial(_combine_kernel, tt=tt),
        grid=(n // tt,),
        in_specs=[_dest_spec(tt),
                  pl.BlockSpec((tt, D_MODEL), lambda i: (i, 0)),
                  pl.BlockSpec((tt, TOP_K), lambda i: (i, 0)),
                  pl.BlockSpec(memory_space=pl.ANY)]
                 + [full(a) for a in ws] + [full(v) for v in vecs],
        out_specs=pl.BlockSpec((tt, D_MODEL), lambda i: (i, 0)),
        scratch_shapes=[pltpu.VMEM((TOP_K, tt, D_MODEL), F32), pltpu.SemaphoreType.DMA(())],
        out_shape=jax.ShapeDtypeStruct((n, D_MODEL), F32),
        compiler_params=_params(1),
        name="combine",
    )(dest_tiles, x2, w_tok, y, *ws, *vecs)


def _moe_and_norm(x2, x2p, w_router, router_bias, w_gate, w_up, w_down, ws_gate, ws_up, ws_down,
                  ln3_g, ln3_b):
    n = x2.shape[0]
    blk = 256
    tt = 128 if n % 128 == 0 else n
    cap = n * TOP_K + N_EXPERTS * blk
    n_blocks = cap // blk
    e_k, rank_k, w_k, counts, counts_row = _router(x2, w_router, router_bias)
    dest, blk_e, nblk = _slots(e_k, rank_k, counts, counts_row, blk, n_blocks)
    dest_tiles = dest.reshape(TOP_K, n // tt, tt).transpose(1, 0, 2).reshape(n // tt, 1, TOP_K * tt)
    xs = _dispatch(dest_tiles, x2p, cap, tt)
    y = _experts(blk_e.reshape(-1), nblk.reshape(-1), xs, w_gate, w_up, w_down, blk)
    return _combine(dest_tiles, x2, w_k.T, y, ws_gate, ws_up, ws_down, ln3_g, ln3_b, tt)


def _layer(x, mem, positions, w_in, cmp_pe_k, cmp_pe_v, cmp_w1_k, cmp_w2_k, cmp_w1_v, cmp_w2_v,
           w_out, ln1_g, ln1_b, w_xq, w_xkv, w_xo, ln2_g, ln2_b, w_router, router_bias,
           w_gate, w_up, w_down, ws_gate, ws_up, ws_down, ln3_g, ln3_b):
    batch, seq, _ = x.shape
    n = batch * seq
    x2d = x.reshape(n, D_MODEL)
    pos_col = positions.astype(F32).reshape(n, 1)
    (rq, rk, rv, rg, nq, nqr, kc, vc, ks, vs, kw, vw, gates) = _inproj(x2d, pos_col, w_in)
    o_ret = _retention(rq, rk, rv, rg, batch, seq)
    kcmp = _compress(kc, cmp_pe_k, cmp_w1_k, cmp_w2_k, batch, seq)
    vcmp = _compress(vc, cmp_pe_v, cmp_w1_v, cmp_w2_v, batch, seq)
    o_nsa = _nsa(nq, nqr, gates, kcmp, vcmp, ks, vs, kw, vw, batch, seq)
    kvx = _memkv(mem.reshape(batch * MEM_LEN, D_MODEL), w_xkv)
    x2, x2p = _postmix(x2d, o_ret, o_nsa, kvx, w_out, w_xq, w_xo, ln1_g, ln1_b, ln2_g, ln2_b, batch, seq)
    out = _moe_and_norm(x2, x2p, w_router, router_bias, w_gate, w_up, w_down,
                        ws_gate, ws_up, ws_down, ln3_g, ln3_b)
    return out.reshape(batch, seq, D_MODEL)


def kernel(x, mem, positions, w_in, cmp_pe_k, cmp_pe_v, cmp_w1_k, cmp_w2_k, cmp_w1_v, cmp_w2_v, w_out, ln1_g, ln1_b, w_xq, w_xkv, w_xo, ln2_g, ln2_b, w_router, router_bias, w_gate, w_up, w_down, ws_gate, ws_up, ws_down, ln3_g, ln3_b):
    for l in range(DEPTH):
        x = _layer(x, mem, positions, w_in[l], cmp_pe_k[l], cmp_pe_v[l], cmp_w1_k[l], cmp_w2_k[l],
                   cmp_w1_v[l], cmp_w2_v[l], w_out[l], ln1_g[l], ln1_b[l], w_xq[l], w_xkv[l],
                   w_xo[l], ln2_g[l], ln2_b[l], w_router[l], router_bias[l], w_gate[l], w_up[l],
                   w_down[l], ws_gate[l], ws_up[l], ws_down[l], ln3_g[l], ln3_b[l])
    return x
```

```python
import functools

import numpy as np
import jax
import jax.numpy as jnp
from jax import lax
from jax.experimental import pallas as pl
from jax.experimental.pallas import tpu as pltpu

D_MODEL = 1024
MEM_LEN = 256
DEPTH = 1
DN_ALPHA = (2 * DEPTH) ** 0.25
LN_EPS = 1e-5
NEG = -1e30
FORCE = 1e9

RET_HEADS = 4
RET_DIM = 128
RET_CHUNK = 128
RET_ROPE_BASE = 10000.0
RET_WIDTH = RET_HEADS * RET_DIM

NSA_HEADS = 8
NSA_KV_GROUPS = 2
NSA_HPG = NSA_HEADS // NSA_KV_GROUPS
NSA_DIM = 64
NSA_WIDTH = NSA_HEADS * NSA_DIM
KV_WIDTH = NSA_KV_GROUPS * NSA_DIM
CMP_LEN = 32
CMP_STRIDE = 16
CMP_HIDDEN = 256
SEL_LEN = 64
SEL_SHIFT = 6
SEL_TOPK = 16
WIN = 512
ROPE_THETA = 500000.0
ROPE_DIMS = NSA_DIM // 4
GATE_LANES = 16
NSA_CHAINS = 2

XATT_HEADS = 4
XATT_DIM = D_MODEL // XATT_HEADS

N_EXPERTS = 64
TOP_K = 8
N_GROUPS = 8
TOPK_GROUPS = 4
EXPERT_FF = 256
SHARED_FF = 256
ROUTED_SCALE = 2.5

LANES = 128
VMEM_LIMIT = 56 * 1024 * 1024

F32 = jnp.float32
BF16 = jnp.bfloat16
NT_DIMS = (((1,), (1,)), ((), ()))


def _params(n_axes):
    return pltpu.CompilerParams(dimension_semantics=("arbitrary",) * n_axes,
                                vmem_limit_bytes=VMEM_LIMIT)


def _dot(a, b):
    return jnp.dot(a, b, preferred_element_type=F32)


def _dot_nt(a, b):
    return lax.dot_general(a, b, NT_DIMS, preferred_element_type=F32)


def _layer_norm(v, g, b):
    mu = jnp.mean(v, axis=-1, keepdims=True)
    d = v - mu
    var = jnp.mean(d * d, axis=-1, keepdims=True)
    return d * lax.rsqrt(var + LN_EPS) * g + b


def _inproj_kernel(x_ref, pos_ref, wret_ref, wnq_ref, wkv_ref, wg_ref, invr_ref, invn_ref,
                   rq_ref, rk_ref, rv_ref, rg_ref, nq_ref, nqr_ref, kc_ref, vc_ref,
                   ks_ref, vs_ref, kw_ref, vw_ref, gate_ref):
    xb = x_ref[...].astype(BF16)
    pos = pos_ref[...]
    lane = lax.broadcasted_iota(jnp.int32, (1, LANES), 1)

    ang = pos * invr_ref[...]
    cos_r = jnp.cos(ang)
    sin_r = jnp.sin(ang)
    sin_r = jnp.where(lane < RET_DIM // 2, -sin_r, sin_r)
    for h in range(RET_HEADS):
        cols = slice(h * RET_DIM, (h + 1) * RET_DIM)
        q = _dot(xb, wret_ref[:, cols])
        rq_ref[:, cols] = (q * cos_r + pltpu.roll(q, RET_DIM // 2, 1) * sin_r).astype(BF16)
        k = _dot(xb, wret_ref[:, RET_WIDTH + h * RET_DIM:RET_WIDTH + (h + 1) * RET_DIM])
        k = (k * cos_r + pltpu.roll(k, RET_DIM // 2, 1) * sin_r) * (RET_DIM ** -0.5)
        rk_ref[:, cols] = k.astype(BF16)
    rv_ref[...] = _dot(xb, wret_ref[:, 2 * RET_WIDTH:3 * RET_WIDTH]).astype(BF16)
    rg_ref[...] = _dot(xb, wret_ref[:, 3 * RET_WIDTH:4 * RET_WIDTH]).astype(BF16)

    half = ROPE_DIMS // 2
    j = lane % NSA_DIM
    angn = pos * invn_ref[...]
    cos_n = jnp.cos(angn)
    sin_n = jnp.sin(angn)
    sin_lo = jnp.where(j < half, -sin_n, 0.0)
    sin_hi = jnp.where((j >= half) & (j < 2 * half), sin_n, 0.0)

    def rope_n(v):
        return v * cos_n + pltpu.roll(v, half, 1) * sin_hi + pltpu.roll(v, LANES - half, 1) * sin_lo

    scale = NSA_DIM ** -0.5
    for c in range(NSA_WIDTH // LANES):
        cols = slice(c * LANES, (c + 1) * LANES)
        q = _dot(xb, wnq_ref[:, cols])
        nq_ref[:, cols] = (q * scale).astype(BF16)
        nqr_ref[:, cols] = (rope_n(q) * scale).astype(BF16)

    def kv(i):
        return _dot(xb, wkv_ref[:, i * KV_WIDTH:(i + 1) * KV_WIDTH])

    def split_groups(ref, v):
        for g in range(NSA_KV_GROUPS):
            ref[g] = v[:, g * NSA_DIM:(g + 1) * NSA_DIM].astype(BF16)

    kc_ref[...] = kv(0)
    vc_ref[...] = kv(1)
    split_groups(ks_ref, rope_n(kv(2)))
    split_groups(vs_ref, kv(3))
    split_groups(kw_ref, rope_n(kv(4)))
    split_groups(vw_ref, kv(5))

    gt = jax.nn.sigmoid(_dot_nt(wg_ref[...], xb))
    for g in range(NSA_KV_GROUPS):
        gate_ref[g] = gt[g * GATE_LANES:(g + 1) * GATE_LANES, :]


def _inproj(x2d, pos_col, w_in):
    n = x2d.shape[0]
    tm = 512 if n % 512 == 0 else n
    off = np.cumsum([0] + [RET_WIDTH] * 4 + [NSA_WIDTH] + [KV_WIDTH] * 6)
    w_ret = w_in[:, :off[4]].astype(BF16)
    w_nq = w_in[:, off[4]:off[5]].astype(BF16)
    w_kv = w_in[:, off[5]:off[11]].astype(BF16)
    wg = w_in[:, off[11]:].reshape(D_MODEL, NSA_KV_GROUPS, NSA_HPG * 3)
    wg = jnp.pad(wg, ((0, 0), (0, 0), (0, GATE_LANES - NSA_HPG * 3)))
    wg = wg.reshape(D_MODEL, NSA_KV_GROUPS * GATE_LANES).T.astype(BF16)

    lane = np.arange(LANES)
    half_r = RET_DIM // 2
    inv_r = (np.float32(RET_ROPE_BASE) ** (-np.arange(half_r, dtype=np.float32) / np.float32(half_r)))
    inv_r = inv_r.astype(np.float32)[lane % half_r][None, :]
    half_n = ROPE_DIMS // 2
    inv_n = (np.float32(ROPE_THETA) ** (-np.arange(half_n, dtype=np.float32) / np.float32(half_n)))
    jn = lane % NSA_DIM
    inv_n = np.where(jn < ROPE_DIMS, inv_n.astype(np.float32)[jn % half_n], np.float32(0.0))[None, :]

    row = lambda w: pl.BlockSpec((tm, w), lambda i: (i, 0))
    full = lambda a: pl.BlockSpec(a.shape, lambda i: (0,) * a.ndim)
    grp = lambda w: pl.BlockSpec((NSA_KV_GROUPS, tm, w), lambda i: (0, i, 0))
    bf = lambda w: jax.ShapeDtypeStruct((n, w), BF16)
    gbf = jax.ShapeDtypeStruct((NSA_KV_GROUPS, n, NSA_DIM), BF16)
    inv_r = jnp.asarray(inv_r, F32)
    inv_n = jnp.asarray(inv_n, F32)
    return pl.pallas_call(
        _inproj_kernel,
        grid=(n // tm,),
        in_specs=[row(D_MODEL), row(1), full(w_ret), full(w_nq), full(w_kv), full(wg),
                  full(inv_r), full(inv_n)],
        out_specs=[row(RET_WIDTH)] * 4 + [row(NSA_WIDTH)] * 2 + [row(KV_WIDTH)] * 2
                  + [grp(NSA_DIM)] * 4
                  + [pl.BlockSpec((NSA_KV_GROUPS, GATE_LANES, tm), lambda i: (0, 0, i))],
        out_shape=[bf(RET_WIDTH)] * 4 + [bf(NSA_WIDTH)] * 2
                  + [jax.ShapeDtypeStruct((n, KV_WIDTH), F32)] * 2 + [gbf] * 4
                  + [jax.ShapeDtypeStruct((NSA_KV_GROUPS, GATE_LANES, n), F32)],
        compiler_params=_params(1),
        name="inproj",
    )(x2d, pos_col, w_ret, w_nq, w_kv, wg, inv_r, inv_n)


def _retention_kernel(q_ref, k_ref, v_ref, g_ref, o_ref, state_ref):
    c = RET_CHUNK

    @pl.when(pl.program_id(1) == 0)
    def _():
        state_ref[...] = jnp.zeros_like(state_ref)

    row = lax.broadcasted_iota(jnp.int32, (c, c), 0)
    col = lax.broadcasted_iota(jnp.int32, (c, c), 1)
    rel = (row - col).astype(F32)
    idx = lax.broadcasted_iota(jnp.int32, (c, 1), 0).astype(F32)
    for h in range(RET_HEADS):
        log_g = float(np.log(np.float32(1.0) - np.float32(2.0) ** np.float32(-5.0 - h)))
        cols = slice(h * RET_DIM, (h + 1) * RET_DIM)
        q = q_ref[:, cols]
        k = k_ref[:, cols]
        v = v_ref[:, cols]
        dmask = jnp.where(rel >= 0, jnp.exp(log_g * jnp.maximum(rel, 0.0)), 0.0)
        scores = _dot_nt(q, k) * dmask
        inner = _dot(scores.astype(BF16), v)
        zeta = jnp.exp(log_g * (c - 1.0 - idx))
        xi = jnp.exp(log_g * (idx + 1.0))
        prev = state_ref[h]
        cross = _dot(q, prev.astype(BF16)) * xi
        kz = (k.astype(F32) * zeta).astype(BF16)
        kv = lax.dot_general(kz, v, (((0,), (0,)), ((), ())), preferred_element_type=F32)
        state_ref[h] = prev * float(np.exp(np.float32(log_g) * np.float32(c))) + kv
        o = inner + cross
        mu = jnp.mean(o, axis=-1, keepdims=True)
        d = o - mu
        var = jnp.mean(d * d, axis=-1, keepdims=True)
        o = d * lax.rsqrt(var + LN_EPS)
        o_ref[:, cols] = (jax.nn.silu(g_ref[:, cols].astype(F32)) * o).astype(BF16)


def _retention(rq, rk, rv, rg, batch, seq):
    nc = seq // RET_CHUNK
    spec = pl.BlockSpec((RET_CHUNK, RET_WIDTH), lambda b, n: (b * nc + n, 0))
    return pl.pallas_call(
        _retention_kernel,
        grid=(batch, nc),
        in_specs=[spec] * 4,
        out_specs=spec,
        out_shape=jax.ShapeDtypeStruct(rq.shape, BF16),
        scratch_shapes=[pltpu.VMEM((RET_HEADS, RET_DIM, RET_DIM), F32)],
        compiler_params=_params(2),
        name="retention",
    )(rq, rk, rv, rg)


def _compress_kernel(a_ref, pe_ref, w1_ref, w2_ref, o_ref, shift_ref, *, n_cmp):
    rows = a_ref.shape[0]
    a = a_ref[...]
    lo = (a + pe_ref[0]).astype(BF16)
    hi = (a + pe_ref[1]).astype(BF16)
    ridx = lax.broadcasted_iota(jnp.int32, (rows, 1), 0)
    shift_ref[rows:rows + 8, :] = jnp.zeros((8, CMP_HIDDEN), F32)
    for g in range(NSA_KV_GROUPS):
        p = _dot(lo, w1_ref[0, g])
        shift_ref[0:rows, :] = _dot(hi, w1_ref[1, g])
        hid = jax.nn.silu(p + shift_ref[pl.ds(1, rows), :])
        out = _dot(hid.astype(BF16), w2_ref[...])
        o_ref[g] = jnp.where(ridx < n_cmp, out, 0.0).astype(BF16)


def _compress(a, pe, w1, w2, batch, seq):
    rows = seq // CMP_STRIDE
    per = CMP_STRIDE * KV_WIDTH
    n_cmp = (seq - CMP_LEN) // CMP_STRIDE + 1
    a2 = a.reshape(batch * rows, per)
    pe2 = jnp.tile(pe.reshape(2, CMP_STRIDE, 1, NSA_DIM), (1, 1, NSA_KV_GROUPS, 1)).reshape(2, 1, per)
    w1r = w1.reshape(2, CMP_STRIDE, 1, NSA_DIM, CMP_HIDDEN)
    eye = jnp.eye(NSA_KV_GROUPS, dtype=w1.dtype).reshape(1, NSA_KV_GROUPS, 1, NSA_KV_GROUPS, 1, 1)
    w1x = (w1r[:, None] * eye).reshape(2, NSA_KV_GROUPS, per, CMP_HIDDEN).astype(BF16)
    w2b = w2.astype(BF16)
    full = lambda arr: pl.BlockSpec(arr.shape, lambda b: (0,) * arr.ndim)
    return pl.pallas_call(
        functools.partial(_compress_kernel, n_cmp=n_cmp),
        grid=(batch,),
        in_specs=[pl.BlockSpec((rows, per), lambda b: (b, 0)), full(pe2), full(w1x), full(w2b)],
        out_specs=pl.BlockSpec((None, NSA_KV_GROUPS, rows, NSA_DIM), lambda b: (b, 0, 0, 0)),
        out_shape=jax.ShapeDtypeStruct((batch, NSA_KV_GROUPS, rows, NSA_DIM), BF16),
        scratch_shapes=[pltpu.VMEM((rows + 8, CMP_HIDDEN), F32)],
        compiler_params=_params(1),
        name="compress",
    )(a2, pe2, w1x, w2b)


def _heads_to_lanes(ref):
    vt = ref[...].astype(F32).T
    return jnp.concatenate([vt[h * NSA_DIM:(h + 1) * NSA_DIM] for h in range(NSA_HPG)], axis=1).astype(BF16)


def _tile_heads(v):
    return jnp.concatenate([v] * NSA_HPG, axis=1)


def _transpose_into(dst_ref, src_ref, chunk):
    def step(c, _):
        c0 = pl.multiple_of(c * chunk, chunk)
        dst_ref[:, pl.ds(c0, chunk)] = src_ref[pl.ds(c0, chunk), :].astype(F32).T.astype(BF16)
        return 0
    lax.fori_loop(0, src_ref.shape[0] // chunk, step, 0)


def _nsa_kernel(qraw_ref, qrot_ref, gate_ref, kcmp_ref, vcmp_ref, ovt_ref,
                ks_ref, vs_ref, kw_ref, vw_ref, o_ref, vst_ref, vwt_ref, vct_ref, bias_ref, *, tq, tk, seq):
    i = pl.program_id(2)
    t0 = i * tq
    cols = NSA_HPG * tq
    n_sel = seq // SEL_LEN
    n_cmp_rows = seq // CMP_STRIDE

    @pl.when(i == 0)
    def _():
        chunk = min(512, n_cmp_rows)
        _transpose_into(vst_ref, vs_ref, chunk)
        _transpose_into(vwt_ref, vw_ref, chunk)
        _transpose_into(vct_ref, vcmp_ref, chunk)

    q_raw = _heads_to_lanes(qraw_ref)
    q_rot = _heads_to_lanes(qrot_ref)
    t_row = t0 + lax.broadcasted_iota(jnp.int32, (1, tq), 1)

    chain_w = cols // NSA_CHAINS
    heads_per_chain = chain_w // tq
    chains = [slice(c * chain_w, (c + 1) * chain_w) for c in range(NSA_CHAINS)]
    tile_chain = lambda v: jnp.concatenate([v] * heads_per_chain, axis=1)

    span = WIN + tq
    ws = pl.multiple_of(jnp.maximum(t0 - WIN, 0), tq)
    dist = t_row - (ws + lax.broadcasted_iota(jnp.int32, (span, 1), 0))
    bias_w = tile_chain(jnp.where((dist >= 0) & (dist < WIN), 0.0, NEG))
    k_w = kw_ref[pl.ds(ws, span), :]
    v_w = vwt_ref[:, pl.ds(ws, span)]
    o_w = []
    for c in chains:
        s_w = _dot(k_w, q_rot[:, c]) + bias_w
        p_w = jnp.exp(s_w - jnp.max(s_w, axis=0, keepdims=True))
        o_w.append(_dot(v_w, p_w.astype(BF16)) / jnp.sum(p_w, axis=0, keepdims=True))
    o_w = jnp.concatenate(o_w, axis=1)

    c_idx = lax.broadcasted_iota(jnp.int32, (n_cmp_rows, 1), 0)
    valid = tile_chain(jnp.where(c_idx * CMP_STRIDE + (CMP_LEN - 1) <= t_row, 1.0, 0.0))
    bias_c = (valid - 1.0) * (-NEG)
    o_c = []
    p_sum = None
    for c in chains:
        s_c = _dot(kcmp_ref[...], q_raw[:, c]) + bias_c
        e_c = jnp.exp(s_c - jnp.max(s_c, axis=0, keepdims=True)) * valid
        l_c = jnp.sum(e_c, axis=0, keepdims=True)
        p_c = e_c / jnp.where(l_c > 0.0, l_c, 1.0)
        o_c.append(_dot(vct_ref[...], p_c.astype(BF16)))
        for h in range(heads_per_chain):
            p_h = p_c[:, h * tq:(h + 1) * tq]
            p_sum = p_h if p_sum is None else p_sum + p_h
    o_c = jnp.concatenate(o_c, axis=1)

    p_hi = p_sum.astype(BF16)
    p_lo = (p_sum - p_hi.astype(F32)).astype(BF16)
    ovt = ovt_ref[...]
    imp = _dot(ovt, p_hi) + _dot(ovt, p_lo)
    jb = lax.broadcasted_iota(jnp.int32, (n_sel, tq), 0)
    cur = (t0 + lax.broadcasted_iota(jnp.int32, (n_sel, tq), 1)) >> SEL_SHIFT
    forced = (jb == 0) | (jb == cur) | (jb == cur - 1)
    work = jnp.where(forced, FORCE, imp)
    work = jnp.where(jb <= cur, work, NEG)
    sel_t = jnp.zeros((n_sel, tq), F32)
    for _ in range(min(SEL_TOPK, n_sel)):
        best = jnp.max(work, axis=0, keepdims=True)
        first = jnp.min(jnp.where(work == best, jb, n_sel), axis=0, keepdims=True)
        hit = jb == first
        sel_t = jnp.where(hit, 1.0, sel_t)
        work = jnp.where(hit, -jnp.inf, work)
    bias_ref[...] = jnp.where(sel_t > 0.5, 0.0, NEG)

    blocks_per_tile = tk // SEL_LEN

    def sel_tile(kt, carry, causal):
        k0 = pl.multiple_of(kt * tk, tk)
        bias = jnp.concatenate(
            [jnp.broadcast_to(bias_ref[pl.ds(kt * blocks_per_tile + j, 1), :], (SEL_LEN, tq))
             for j in range(blocks_per_tile)], axis=0)
        if causal:
            kpos = k0 + lax.broadcasted_iota(jnp.int32, (tk, 1), 0)
            bias = jnp.where(kpos <= t_row, bias, NEG)
        bias = tile_chain(bias)
        k_t = ks_ref[pl.ds(k0, tk), :]
        v_t = vst_ref[:, pl.ds(k0, tk)]
        out = []
        scores = [_dot(k_t, q_rot[:, c]) + bias for c in chains]
        for (m, l, acc), s in zip(carry, scores):
            m_new = jnp.maximum(m, jnp.max(s, axis=0, keepdims=True))
            alpha = jnp.exp(m - m_new)
            p = jnp.exp(s - m_new)
            l = alpha * l + jnp.sum(p, axis=0, keepdims=True)
            acc = alpha * acc + _dot(v_t, p.astype(BF16))
            out.append((m_new, l, acc))
        return tuple(out)

    n_full = t0 // tk
    init = tuple((jnp.full((1, chain_w), NEG, F32), jnp.zeros((1, chain_w), F32),
                  jnp.zeros((NSA_DIM, chain_w), F32)) for _ in chains)
    carry = lax.fori_loop(0, n_full, functools.partial(sel_tile, causal=False), init)
    carry = sel_tile(n_full, carry, causal=True)
    o_s = jnp.concatenate([acc / l for _, l, acc in carry], axis=1)

    gt = gate_ref[...]
    outs = []
    for h in range(NSA_HPG):
        c = slice(h * tq, (h + 1) * tq)
        outs.append(gt[3 * h:3 * h + 1] * o_c[:, c] + gt[3 * h + 1:3 * h + 2] * o_s[:, c]
                    + gt[3 * h + 2:3 * h + 3] * o_w[:, c])
    o_ref[...] = jnp.concatenate(outs, axis=0).T.astype(BF16)


def _nsa(nq, nqr, gates, kcmp, vcmp, ks, vs, kw, vw, batch, seq):
    n = batch * seq
    tq = 512
    tk = 512 if seq % 512 == 0 else seq
    nqb = seq // tq
    n_sel = seq // SEL_LEN
    rows_c = seq // CMP_STRIDE
    gw = NSA_HPG * NSA_DIM
    cs = np.arange(rows_c)[None, :] * CMP_STRIDE
    ss = np.arange(n_sel)[:, None] * SEL_LEN
    n_cmp = (seq - CMP_LEN) // CMP_STRIDE + 1
    ovt = ((cs < ss + SEL_LEN) & (cs + CMP_LEN > ss) & (np.arange(rows_c)[None, :] < n_cmp))
    ovt = jnp.asarray(ovt.astype(np.float32), BF16)

    qspec = pl.BlockSpec((tq, gw), lambda b, g, i: (b * nqb + i, g))
    cspec = pl.BlockSpec((None, None, rows_c, NSA_DIM), lambda b, g, i: (b, g, 0, 0))
    kspec = pl.BlockSpec((None, seq, NSA_DIM), lambda b, g, i: (g, b, 0))
    return pl.pallas_call(
        functools.partial(_nsa_kernel, tq=tq, tk=tk, seq=seq),
        grid=(batch, NSA_KV_GROUPS, nqb),
        in_specs=[qspec, qspec,
                  pl.BlockSpec((None, GATE_LANES, tq), lambda b, g, i: (g, 0, b * nqb + i)),
                  cspec, cspec, pl.BlockSpec(ovt.shape, lambda b, g, i: (0, 0)),
                  kspec, kspec, kspec, kspec],
        out_specs=qspec,
        out_shape=jax.ShapeDtypeStruct((n, NSA_WIDTH), BF16),
        scratch_shapes=[pltpu.VMEM((NSA_DIM, seq), BF16), pltpu.VMEM((NSA_DIM, seq), BF16),
                        pltpu.VMEM((NSA_DIM, rows_c), BF16), pltpu.VMEM((n_sel, tq), F32)],
        compiler_params=_params(3),
        name="nsa",
    )(nq, nqr, gates, kcmp, vcmp, ovt, ks, vs, kw, vw)


def _memkv_kernel(mem_ref, w_ref, kv_ref):
    kv_ref[...] = _dot(mem_ref[...].astype(BF16), w_ref[...]).astype(BF16)


def _memkv(mem2d, w_xkv):
    n = mem2d.shape[0]
    w = w_xkv.astype(BF16)
    return pl.pallas_call(
        _memkv_kernel,
        grid=(n // MEM_LEN,),
        in_specs=[pl.BlockSpec((MEM_LEN, D_MODEL), lambda i: (i, 0)),
                  pl.BlockSpec(w.shape, lambda i: (0, 0))],
        out_specs=pl.BlockSpec((MEM_LEN, 2 * D_MODEL), lambda i: (i, 0)),
        out_shape=jax.ShapeDtypeStruct((n, 2 * D_MODEL), BF16),
        compiler_params=_params(1),
        name="memkv",
    )(mem2d, w)


ROW_TILES = D_MODEL // 2 // LANES


def _store_packed_rows(ref, v):
    half = D_MODEL // 2
    rows = v.shape[0]
    hi = pltpu.bitcast(v[:, :half].astype(BF16).astype(F32), jnp.uint32)
    lo = pltpu.bitcast(v[:, half:].astype(BF16).astype(F32), jnp.uint32)
    packed = hi | (lo >> 16)
    for s in range(ROW_TILES):
        ref[pl.ds(s, rows, stride=ROW_TILES), :] = packed[:, s * LANES:(s + 1) * LANES]


def _load_packed_halves(ref):
    rows = ref.shape[0] // ROW_TILES
    words = [ref[pl.ds(s, rows, stride=ROW_TILES), :] for s in range(ROW_TILES)]
    hi = jnp.concatenate([pltpu.bitcast(w & jnp.uint32(0xFFFF0000), F32) for w in words], axis=1)
    lo = jnp.concatenate([pltpu.bitcast(w << 16, F32) for w in words], axis=1)
    return hi, lo


def _postmix_kernel(x_ref, oret_ref, onsa_ref, kv_ref, wout_ref, wq_ref, wo_ref,
                    g1_ref, b1_ref, g2_ref, b2_ref, x2_ref, x2p_ref):
    mixed = jnp.concatenate([oret_ref[...], onsa_ref[...]], axis=1)
    x1 = _layer_norm(DN_ALPHA * x_ref[...] + _dot(mixed, wout_ref[...]), g1_ref[...], b1_ref[...])
    q = (_dot(x1.astype(BF16), wq_ref[...]) * (XATT_DIM ** -0.5)).astype(BF16)
    heads = []
    for h in range(XATT_HEADS):
        cols = slice(h * XATT_DIM, (h + 1) * XATT_DIM)
        s = _dot_nt(q[:, cols], kv_ref[:, cols])
        m = jnp.max(s, axis=-1, keepdims=True)
        p = jnp.exp(s - m)
        l = jnp.sum(p, axis=-1, keepdims=True)
        heads.append(_dot(p.astype(BF16), kv_ref[:, D_MODEL + h * XATT_DIM:D_MODEL + (h + 1) * XATT_DIM]) / l)
    att = jnp.concatenate(heads, axis=1).astype(BF16)
    x2 = _layer_norm(DN_ALPHA * x1 + _dot(att, wo_ref[...]), g2_ref[...], b2_ref[...])
    x2_ref[...] = x2
    _store_packed_rows(x2p_ref, x2)


def _postmix(x2d, o_ret, o_nsa, kvx, w_out, w_xq, w_xo, ln1_g, ln1_b, ln2_g, ln2_b, batch, seq):
    n = x2d.shape[0]
    tm = 256 if seq % 256 == 0 else seq
    per_b = seq // tm
    row = lambda w: pl.BlockSpec((tm, w), lambda b, i: (b * per_b + i, 0))
    full = lambda a: pl.BlockSpec(a.shape, lambda b, i: (0,) * a.ndim)
    ws = [w_out.astype(BF16), w_xq.astype(BF16), w_xo.astype(BF16)]
    vecs = [v.reshape(1, D_MODEL) for v in (ln1_g, ln1_b, ln2_g, ln2_b)]
    return pl.pallas_call(
        _postmix_kernel,
        grid=(batch, per_b),
        in_specs=[row(D_MODEL), row(RET_WIDTH), row(NSA_WIDTH),
                  pl.BlockSpec((MEM_LEN, 2 * D_MODEL), lambda b, i: (b, 0))]
                 + [full(w) for w in ws] + [full(v) for v in vecs],
        out_specs=[row(D_MODEL),
                   pl.BlockSpec((tm * ROW_TILES, LANES), lambda b, i: (b * per_b + i, 0))],
        out_shape=[jax.ShapeDtypeStruct((n, D_MODEL), F32),
                   jax.ShapeDtypeStruct((n * ROW_TILES, LANES), jnp.uint32)],
        compiler_params=_params(2),
        name="postmix",
    )(x2d, o_ret, o_nsa, kvx, *ws, *vecs)


def _router_kernel(x_ref, wr_ref, bias_ref, e_ref, rank_ref, w_ref, cnt_ref, cntrow_ref, carry_ref, carryrow_ref):
    tn = x_ref.shape[0]
    per = N_EXPERTS // N_GROUPS

    @pl.when(pl.program_id(0) == 0)
    def _():
        carry_ref[...] = jnp.zeros_like(carry_ref)
        carryrow_ref[...] = jnp.zeros_like(carryrow_ref)

    logits = _dot_nt(wr_ref[...], x_ref[...].astype(BF16))
    scores = jax.nn.sigmoid(logits)
    biased = scores + bias_ref[...]
    b3 = biased.reshape(N_GROUPS, per, tn)
    member = lax.broadcasted_iota(jnp.int32, (N_GROUPS, per, tn), 1)
    top1 = jnp.max(b3, axis=1, keepdims=True)
    first1 = jnp.min(jnp.where(b3 == top1, member, per), axis=1, keepdims=True)
    top2 = jnp.max(jnp.where(member == first1, -jnp.inf, b3), axis=1, keepdims=True)
    gscore = top1 + top2
    gidx = lax.broadcasted_iota(jnp.int32, (N_GROUPS, 1, tn), 0)
    gwork = gscore
    for _ in range(TOPK_GROUPS - 1):
        gbest = jnp.max(gwork, axis=0, keepdims=True)
        gfirst = jnp.min(jnp.where(gwork == gbest, gidx, N_GROUPS), axis=0, keepdims=True)
        gwork = jnp.where(gidx == gfirst, -jnp.inf, gwork)
    kth = jnp.max(gwork, axis=0, keepdims=True)
    work = jnp.where(gscore >= kth, b3, NEG).reshape(N_EXPERTS, tn)
    eidx = lax.broadcasted_iota(jnp.int32, (N_EXPERTS, tn), 0)
    picks = []
    chosen = jnp.zeros((N_EXPERTS, tn), F32)
    for _ in range(TOP_K):
        best = jnp.max(work, axis=0, keepdims=True)
        first = jnp.min(jnp.where(work == best, eidx, N_EXPERTS), axis=0, keepdims=True)
        hit = eidx == first
        picks.append((first, hit))
        chosen = jnp.where(hit, 1.0, chosen)
        work = jnp.where(hit, -jnp.inf, work)

    r_i = lax.broadcasted_iota(jnp.int32, (tn, tn), 0)
    c_i = lax.broadcasted_iota(jnp.int32, (tn, tn), 1)
    before = jnp.where(r_i < c_i, 1.0, 0.0).astype(BF16)
    chosen_b = chosen.astype(BF16)
    rank = _dot(chosen_b, before) + carry_ref[...]
    carry_ref[...] = carry_ref[...] + jnp.sum(chosen, axis=1, keepdims=True)
    carryrow_ref[...] = carryrow_ref[...] + _dot_nt(jnp.ones((8, tn), BF16), chosen_b)
    cnt_ref[...] = carry_ref[...]
    cntrow_ref[...] = carryrow_ref[...]

    wsel = [jnp.sum(jnp.where(hit, scores, 0.0), axis=0, keepdims=True) for _, hit in picks]
    wsum = wsel[0]
    for v in wsel[1:]:
        wsum = wsum + v
    for kk, (first, hit) in enumerate(picks):
        e_ref[kk:kk + 1, :] = first
        rank_ref[kk:kk + 1, :] = jnp.sum(jnp.where(hit, rank, 0.0), axis=0, keepdims=True).astype(jnp.int32)
        w_ref[kk:kk + 1, :] = wsel[kk] / wsum * ROUTED_SCALE


def _router(x2, w_router, router_bias):
    n = x2.shape[0]
    tn = 512 if n % 512 == 0 else n
    wr_t = w_router.T.astype(BF16)
    bias = router_bias.reshape(N_EXPERTS, 1).astype(F32)
    kspec = pl.BlockSpec((TOP_K, tn), lambda i: (0, i))
    return pl.pallas_call(
        _router_kernel,
        grid=(n // tn,),
        in_specs=[pl.BlockSpec((tn, D_MODEL), lambda i: (i, 0)),
                  pl.BlockSpec(wr_t.shape, lambda i: (0, 0)),
                  pl.BlockSpec(bias.shape, lambda i: (0, 0))],
        out_specs=[kspec, kspec, kspec, pl.BlockSpec((N_EXPERTS, 1), lambda i: (0, 0)),
                   pl.BlockSpec((8, N_EXPERTS), lambda i: (0, 0))],
        out_shape=[jax.ShapeDtypeStruct((TOP_K, n), jnp.int32),
                   jax.ShapeDtypeStruct((TOP_K, n), jnp.int32),
                   jax.ShapeDtypeStruct((TOP_K, n), F32),
                   jax.ShapeDtypeStruct((N_EXPERTS, 1), F32),
                   jax.ShapeDtypeStruct((8, N_EXPERTS), F32)],
        scratch_shapes=[pltpu.VMEM((N_EXPERTS, 1), F32), pltpu.VMEM((8, N_EXPERTS), F32)],
        compiler_params=_params(1),
        name="router",
    )(x2, wr_t, bias)


def _slots_kernel(e_ref, rank_ref, cnt_ref, cntrow_ref, dest_ref, blk_e_ref, nblk_ref, *, blk, n_blocks):
    pad = lambda c: jnp.ceil(c / blk) * blk
    padded = pad(cnt_ref[...])
    padded_row = pad(cntrow_ref[0:1, :])
    r_i = lax.broadcasted_iota(jnp.int32, (N_EXPERTS, N_EXPERTS), 0)
    c_i = lax.broadcasted_iota(jnp.int32, (N_EXPERTS, N_EXPERTS), 1)
    start = jnp.sum(jnp.where(c_i < r_i, padded_row, 0.0), axis=1, keepdims=True)
    end = start + padded
    e = e_ref[...]
    dest = rank_ref[...]
    for ex in range(N_EXPERTS):
        dest = dest + jnp.where(e == ex, start[ex:ex + 1, :].astype(jnp.int32), 0)
    dest_ref[...] = dest
    bstart = (lax.broadcasted_iota(jnp.int32, (1, n_blocks), 1) * blk).astype(F32)
    owner = jnp.sum(jnp.where(end <= bstart, 1.0, 0.0), axis=0, keepdims=True)
    blk_e_ref[...] = jnp.minimum(owner, N_EXPERTS - 1.0).astype(jnp.int32)
    nblk_ref[...] = (end[N_EXPERTS - 1:N_EXPERTS, :] / blk).astype(jnp.int32)


def _slots(e_k, rank_k, counts, counts_row, blk, n_blocks):
    n = e_k.shape[1]
    full = lambda shape: pl.BlockSpec(shape, lambda: (0,) * len(shape))
    return pl.pallas_call(
        functools.partial(_slots_kernel, blk=blk, n_blocks=n_blocks),
        in_specs=[full((TOP_K, n)), full((TOP_K, n)), full((N_EXPERTS, 1)), full((8, N_EXPERTS))],
        out_specs=[full((TOP_K, n)), full((1, n_blocks)), full((1, 1))],
        out_shape=[jax.ShapeDtypeStruct((TOP_K, n), jnp.int32),
                   jax.ShapeDtypeStruct((1, n_blocks), jnp.int32),
                   jax.ShapeDtypeStruct((1, 1), jnp.int32)],
        compiler_params=pltpu.CompilerParams(vmem_limit_bytes=VMEM_LIMIT),
        name="slots",
    )(e_k, rank_k, counts, counts_row)


def _row_copy(src_ref, src_row, dst_ref, dst_row, sem):
    span = lambda r: pl.ds(pl.multiple_of(r * ROW_TILES, ROW_TILES), ROW_TILES)
    return pltpu.make_async_copy(src_ref.at[span(src_row)], dst_ref.at[span(dst_row)], sem)


def _dest_spec(tt):
    return pl.BlockSpec((None, 1, TOP_K * tt), lambda i: (i, 0, 0), memory_space=pltpu.SMEM)


def _dispatch_kernel(dest_ref, x_ref, zero_ref, xs_ref, sem, *, tt):
    del zero_ref

    def issue(j, _):
        for kk in range(TOP_K):
            _row_copy(x_ref, j, xs_ref, dest_ref[0, kk * tt + j], sem).start(priority=kk % 2)
        return 0

    lax.fori_loop(0, tt, issue, 0)

    def drain(j, _):
        for kk in range(TOP_K):
            _row_copy(x_ref, 0, xs_ref, 0, sem).wait()
        return 0

    lax.fori_loop(0, tt, drain, 0)


def _dispatch(dest_tiles, x2p, cap, tt):
    n = x2p.shape[0] // ROW_TILES
    zeros = jnp.zeros((cap * ROW_TILES, LANES), x2p.dtype)
    return pl.pallas_call(
        functools.partial(_dispatch_kernel, tt=tt),
        grid=(n // tt,),
        in_specs=[_dest_spec(tt),
                  pl.BlockSpec((tt * ROW_TILES, LANES), lambda i: (i, 0)),
                  pl.BlockSpec(memory_space=pl.ANY)],
        out_specs=pl.BlockSpec(memory_space=pl.ANY),
        scratch_shapes=[pltpu.SemaphoreType.DMA(())],
        out_shape=jax.ShapeDtypeStruct(zeros.shape, x2p.dtype),
        input_output_aliases={2: 0},
        compiler_params=_params(1),
        name="dispatch",
    )(dest_tiles, x2p, zeros)


def _experts_kernel(blk_e_ref, nblk_ref, xs_ref, wg_ref, wu_ref, wd_ref, y_ref):
    del blk_e_ref

    @pl.when(pl.program_id(0) < nblk_ref[0])
    def _():
        half = D_MODEL // 2
        hi, lo = (v.astype(BF16) for v in _load_packed_halves(xs_ref))
        gate = _dot(hi, wg_ref[:half, :]) + _dot(lo, wg_ref[half:, :])
        up = _dot(hi, wu_ref[:half, :]) + _dot(lo, wu_ref[half:, :])
        _store_packed_rows(y_ref, _dot((jax.nn.silu(gate) * up).astype(BF16), wd_ref[...]))

    @pl.when(pl.program_id(0) >= nblk_ref[0])
    def _():
        y_ref[...] = jnp.zeros_like(y_ref)


def _experts(blk_e, nblk, xs, w_gate, w_up, w_down, blk):
    cap = xs.shape[0] // ROW_TILES
    wg, wu, wd = (a.astype(BF16) for a in (w_gate, w_up, w_down))
    wspec = lambda a: pl.BlockSpec((None,) + a.shape[1:], lambda i, be, nb: (be[i], 0, 0))
    rows = pl.BlockSpec((blk * ROW_TILES, LANES), lambda i, be, nb: (i, 0))
    return pl.pallas_call(
        _experts_kernel,
        grid_spec=pltpu.PrefetchScalarGridSpec(
            num_scalar_prefetch=2,
            grid=(cap // blk,),
            in_specs=[rows, wspec(wg), wspec(wu), wspec(wd)],
            out_specs=rows,
        ),
        out_shape=jax.ShapeDtypeStruct(xs.shape, xs.dtype),
        compiler_params=_params(1),
        name="experts",
    )(blk_e, nblk, xs, wg, wu, wd)


def _combine_kernel(dest_ref, x_ref, wk_ref, y_ref, wsg_ref, wsu_ref, wsd_ref, g_ref, b_ref,
                    o_ref, buf_ref, sem, *, tt):
    def issue(j, _):
        for kk in range(TOP_K):
            _row_copy(y_ref, dest_ref[0, kk * tt + j], buf_ref.at[kk], j, sem).start(priority=kk % 2)
        return 0

    lax.fori_loop(0, tt, issue, 0)

    x = x_ref[...]
    xb = x.astype(BF16)
    shared = _dot((jax.nn.silu(_dot(xb, wsg_ref[...])) * _dot(xb, wsu_ref[...])).astype(BF16), wsd_ref[...])

    def drain(j, _):
        for kk in range(TOP_K):
            _row_copy(y_ref, 0, buf_ref.at[kk], 0, sem).wait()
        return 0

    lax.fori_loop(0, tt, drain, 0)

    wk = wk_ref[...]
    routed_hi = routed_lo = None
    for kk in range(TOP_K):
        hi, lo = _load_packed_halves(buf_ref.at[kk])
        w = wk[:, kk:kk + 1]
        routed_hi = hi * w if kk == 0 else routed_hi + hi * w
        routed_lo = lo * w if kk == 0 else routed_lo + lo * w
    routed = jnp.concatenate([routed_hi, routed_lo], axis=1)
    o_ref[...] = _layer_norm(DN_ALPHA * x + (routed + shared), g_ref[...], b_ref[...])


def _combine(dest_tiles, x2, w_tok, y, ws_gate, ws_up, ws_down, ln3_g, ln3_b, tt):
    n = x2.shape[0]
    ws = [ws_gate.astype(BF16), ws_up.astype(BF16), ws_down.astype(BF16)]
    vecs = [ln3_g.reshape(1, D_MODEL), ln3_b.reshape(1, D_MODEL)]
    full = lambda a: pl.BlockSpec(a.shape, lambda i: (0,) * a.ndim)
    return pl.pallas_call(
        functools.partial(_combine_kernel, tt=tt),
        grid=(n // tt,),
        in_specs=[_dest_spec(tt),
                  pl.BlockSpec((tt, D_MODEL), lambda i: (i, 0)),
                  pl.BlockSpec((tt, TOP_K), lambda i: (i, 0)),
                  pl.BlockSpec(memory_space=pl.ANY)]
                 + [full(a) for a in ws] + [full(v) for v in vecs],
        out_specs=pl.BlockSpec((tt, D_MODEL), lambda i: (i, 0)),
        scratch_shapes=[pltpu.VMEM((TOP_K, tt * ROW_TILES, LANES), jnp.uint32), pltpu.SemaphoreType.DMA(())],
        out_shape=jax.ShapeDtypeStruct((n, D_MODEL), F32),
        compiler_params=_params(1),
        name="combine",
    )(dest_tiles, x2, w_tok, y, *ws, *vecs)


def _moe_and_norm(x2, x2p, w_router, router_bias, w_gate, w_up, w_down, ws_gate, ws_up, ws_down,
                  ln3_g, ln3_b):
    n = x2.shape[0]
    blk = 256
    tt = 128 if n % 128 == 0 else n
    cap = n * TOP_K + N_EXPERTS * blk
    n_blocks = cap // blk
    e_k, rank_k, w_k, counts, counts_row = _router(x2, w_router, router_bias)
    dest, blk_e, nblk = _slots(e_k, rank_k, counts, counts_row, blk, n_blocks)
    dest_tiles = dest.reshape(TOP_K, n // tt, tt).transpose(1, 0, 2).reshape(n // tt, 1, TOP_K * tt)
    xs = _dispatch(dest_tiles, x2p, cap, tt)
    y = _experts(blk_e.reshape(-1), nblk.reshape(-1), xs, w_gate, w_up, w_down, blk)
    return _combine(dest_tiles, x2, w_k.T, y, ws_gate, ws_up, ws_down, ln3_g, ln3_b, tt)


def _layer(x, mem, positions, w_in, cmp_pe_k, cmp_pe_v, cmp_w1_k, cmp_w2_k, cmp_w1_v, cmp_w2_v,
           w_out, ln1_g, ln1_b, w_xq, w_xkv, w_xo, ln2_g, ln2_b, w_router, router_bias,
           w_gate, w_up, w_down, ws_gate, ws_up, ws_down, ln3_g, ln3_b):
    batch, seq, _ = x.shape
    n = batch * seq
    x2d = x.reshape(n, D_MODEL)
    pos_col = positions.astype(F32).reshape(n, 1)
    (rq, rk, rv, rg, nq, nqr, kc, vc, ks, vs, kw, vw, gates) = _inproj(x2d, pos_col, w_in)
    o_ret = _retention(rq, rk, rv, rg, batch, seq)
    kcmp = _compress(kc, cmp_pe_k, cmp_w1_k, cmp_w2_k, batch, seq)
    vcmp = _compress(vc, cmp_pe_v, cmp_w1_v, cmp_w2_v, batch, seq)
    o_nsa = _nsa(nq, nqr, gates, kcmp, vcmp, ks, vs, kw, vw, batch, seq)
    kvx = _memkv(mem.reshape(batch * MEM_LEN, D_MODEL), w_xkv)
    x2, x2p = _postmix(x2d, o_ret, o_nsa, kvx, w_out, w_xq, w_xo, ln1_g, ln1_b, ln2_g, ln2_b, batch, seq)
    out = _moe_and_norm(x2, x2p, w_router, router_bias, w_gate, w_up, w_down,
                        ws_gate, ws_up, ws_down, ln3_g, ln3_b)
    return out.reshape(batch, seq, D_MODEL)


def kernel(x, mem, positions, w_in, cmp_pe_k, cmp_pe_v, cmp_w1_k, cmp_w2_k, cmp_w1_v, cmp_w2_v, w_out, ln1_g, ln1_b, w_xq, w_xkv, w_xo, ln2_g, ln2_b, w_router, router_bias, w_gate, w_up, w_down, ws_gate, ws_up, ws_down, ln3_g, ln3_b):
    for l in range(DEPTH):
        x = _layer(x, mem, positions, w_in[l], cmp_pe_k[l], cmp_pe_v[l], cmp_w1_k[l], cmp_w2_k[l],
                   cmp_w1_v[l], cmp_w2_v[l], w_out[l], ln1_g[l], ln1_b[l], w_xq[l], w_xkv[l],
                   w_xo[l], ln2_g[l], ln2_b[l], w_router[l], router_bias[l], w_gate[l], w_up[l],
                   w_down[l], ws_gate[l], ws_up[l], ws_down[l], ln3_g[l], ln3_b[l])
    return x
```

```python
import functools

import numpy as np
import jax
import jax.numpy as jnp
from jax import lax
from jax.experimental import pallas as pl
from jax.experimental.pallas import tpu as pltpu

D_MODEL = 1024
MEM_LEN = 256
DEPTH = 1
DN_ALPHA = (2 * DEPTH) ** 0.25
LN_EPS = 1e-5
NEG = -1e30
FORCE = 1e9

RET_HEADS = 4
RET_DIM = 128
RET_CHUNK = 128
RET_ROPE_BASE = 10000.0
RET_WIDTH = RET_HEADS * RET_DIM

NSA_HEADS = 8
NSA_KV_GROUPS = 2
NSA_HPG = NSA_HEADS // NSA_KV_GROUPS
NSA_DIM = 64
NSA_WIDTH = NSA_HEADS * NSA_DIM
KV_WIDTH = NSA_KV_GROUPS * NSA_DIM
CMP_LEN = 32
CMP_STRIDE = 16
CMP_HIDDEN = 256
SEL_LEN = 64
SEL_SHIFT = 6
SEL_TOPK = 16
WIN = 512
ROPE_THETA = 500000.0
ROPE_DIMS = NSA_DIM // 4
GATE_LANES = 16
NSA_CHAINS = 2

XATT_HEADS = 4
XATT_DIM = D_MODEL // XATT_HEADS

N_EXPERTS = 64
TOP_K = 8
N_GROUPS = 8
TOPK_GROUPS = 4
EXPERT_FF = 256
SHARED_FF = 256
ROUTED_SCALE = 2.5

LANES = 128
VMEM_LIMIT = 56 * 1024 * 1024

F32 = jnp.float32
BF16 = jnp.bfloat16
NT_DIMS = (((1,), (1,)), ((), ()))


def _params(n_axes):
    return pltpu.CompilerParams(dimension_semantics=("arbitrary",) * n_axes,
                                vmem_limit_bytes=VMEM_LIMIT)


def _dot(a, b):
    return jnp.dot(a, b, preferred_element_type=F32)


def _dot_nt(a, b):
    return lax.dot_general(a, b, NT_DIMS, preferred_element_type=F32)


def _layer_norm(v, g, b):
    mu = jnp.mean(v, axis=-1, keepdims=True)
    d = v - mu
    var = jnp.mean(d * d, axis=-1, keepdims=True)
    return d * lax.rsqrt(var + LN_EPS) * g + b


def _inproj_kernel(x_ref, pos_ref, wret_ref, wnq_ref, wkv_ref, wg_ref, invr_ref, invn_ref,
                   rq_ref, rk_ref, rv_ref, rg_ref, nq_ref, nqr_ref, kc_ref, vc_ref,
                   ks_ref, vs_ref, kw_ref, vw_ref, gate_ref):
    xb = x_ref[...].astype(BF16)
    pos = pos_ref[...]
    lane = lax.broadcasted_iota(jnp.int32, (1, LANES), 1)

    ang = pos * invr_ref[...]
    cos_r = jnp.cos(ang)
    sin_r = jnp.sin(ang)
    sin_r = jnp.where(lane < RET_DIM // 2, -sin_r, sin_r)
    q_all = _dot(xb, wret_ref[:, :RET_WIDTH])
    k_all = _dot(xb, wret_ref[:, RET_WIDTH:2 * RET_WIDTH])
    for h in range(RET_HEADS):
        cols = slice(h * RET_DIM, (h + 1) * RET_DIM)
        q = q_all[:, cols]
        rq_ref[:, cols] = (q * cos_r + pltpu.roll(q, RET_DIM // 2, 1) * sin_r).astype(BF16)
        k = k_all[:, cols]
        k = (k * cos_r + pltpu.roll(k, RET_DIM // 2, 1) * sin_r) * (RET_DIM ** -0.5)
        rk_ref[:, cols] = k.astype(BF16)
    rv_ref[...] = _dot(xb, wret_ref[:, 2 * RET_WIDTH:3 * RET_WIDTH]).astype(BF16)
    rg_ref[...] = _dot(xb, wret_ref[:, 3 * RET_WIDTH:4 * RET_WIDTH]).astype(BF16)

    half = ROPE_DIMS // 2
    j = lane % NSA_DIM
    angn = pos * invn_ref[...]
    cos_n = jnp.cos(angn)
    sin_n = jnp.sin(angn)
    sin_lo = jnp.where(j < half, -sin_n, 0.0)
    sin_hi = jnp.where((j >= half) & (j < 2 * half), sin_n, 0.0)

    def rope_n(v):
        return v * cos_n + pltpu.roll(v, half, 1) * sin_hi + pltpu.roll(v, LANES - half, 1) * sin_lo

    scale = NSA_DIM ** -0.5
    nq_all = _dot(xb, wnq_ref[...])
    for c in range(NSA_WIDTH // LANES):
        cols = slice(c * LANES, (c + 1) * LANES)
        q = nq_all[:, cols]
        nq_ref[:, cols] = (q * scale).astype(BF16)
        nqr_ref[:, cols] = (rope_n(q) * scale).astype(BF16)

    kv_all = _dot(xb, wkv_ref[...])

    def kv(i):
        return kv_all[:, i * KV_WIDTH:(i + 1) * KV_WIDTH]

    def split_groups(ref, v):
        for g in range(NSA_KV_GROUPS):
            ref[g] = v[:, g * NSA_DIM:(g + 1) * NSA_DIM].astype(BF16)

    kc_ref[...] = kv(0)
    vc_ref[...] = kv(1)
    split_groups(ks_ref, rope_n(kv(2)))
    split_groups(vs_ref, kv(3))
    split_groups(kw_ref, rope_n(kv(4)))
    split_groups(vw_ref, kv(5))

    gt = jax.nn.sigmoid(_dot_nt(wg_ref[...], xb))
    for g in range(NSA_KV_GROUPS):
        gate_ref[g] = gt[g * GATE_LANES:(g + 1) * GATE_LANES, :]


def _inproj(x2d, pos_col, w_in):
    n = x2d.shape[0]
    tm = 512 if n % 512 == 0 else n
    off = np.cumsum([0] + [RET_WIDTH] * 4 + [NSA_WIDTH] + [KV_WIDTH] * 6)
    w_ret = w_in[:, :off[4]].astype(BF16)
    w_nq = w_in[:, off[4]:off[5]].astype(BF16)
    w_kv = w_in[:, off[5]:off[11]].astype(BF16)
    wg = w_in[:, off[11]:].reshape(D_MODEL, NSA_KV_GROUPS, NSA_HPG * 3)
    wg = jnp.pad(wg, ((0, 0), (0, 0), (0, GATE_LANES - NSA_HPG * 3)))
    wg = wg.reshape(D_MODEL, NSA_KV_GROUPS * GATE_LANES).T.astype(BF16)

    lane = np.arange(LANES)
    half_r = RET_DIM // 2
    inv_r = (np.float32(RET_ROPE_BASE) ** (-np.arange(half_r, dtype=np.float32) / np.float32(half_r)))
    inv_r = inv_r.astype(np.float32)[lane % half_r][None, :]
    half_n = ROPE_DIMS // 2
    inv_n = (np.float32(ROPE_THETA) ** (-np.arange(half_n, dtype=np.float32) / np.float32(half_n)))
    jn = lane % NSA_DIM
    inv_n = np.where(jn < ROPE_DIMS, inv_n.astype(np.float32)[jn % half_n], np.float32(0.0))[None, :]

    row = lambda w: pl.BlockSpec((tm, w), lambda i: (i, 0))
    full = lambda a: pl.BlockSpec(a.shape, lambda i: (0,) * a.ndim)
    grp = lambda w: pl.BlockSpec((NSA_KV_GROUPS, tm, w), lambda i: (0, i, 0))
    bf = lambda w: jax.ShapeDtypeStruct((n, w), BF16)
    gbf = jax.ShapeDtypeStruct((NSA_KV_GROUPS, n, NSA_DIM), BF16)
    inv_r = jnp.asarray(inv_r, F32)
    inv_n = jnp.asarray(inv_n, F32)
    return pl.pallas_call(
        _inproj_kernel,
        grid=(n // tm,),
        in_specs=[row(D_MODEL), row(1), full(w_ret), full(w_nq), full(w_kv), full(wg),
                  full(inv_r), full(inv_n)],
        out_specs=[row(RET_WIDTH)] * 4 + [row(NSA_WIDTH)] * 2 + [row(KV_WIDTH)] * 2
                  + [grp(NSA_DIM)] * 4
                  + [pl.BlockSpec((NSA_KV_GROUPS, GATE_LANES, tm), lambda i: (0, 0, i))],
        out_shape=[bf(RET_WIDTH)] * 4 + [bf(NSA_WIDTH)] * 2
                  + [jax.ShapeDtypeStruct((n, KV_WIDTH), F32)] * 2 + [gbf] * 4
                  + [jax.ShapeDtypeStruct((NSA_KV_GROUPS, GATE_LANES, n), F32)],
        compiler_params=_params(1),
        name="inproj",
    )(x2d, pos_col, w_ret, w_nq, w_kv, wg, inv_r, inv_n)


def _retention_kernel(q_ref, k_ref, v_ref, g_ref, o_ref, state_ref):
    c = RET_CHUNK

    @pl.when(pl.program_id(1) == 0)
    def _():
        state_ref[...] = jnp.zeros_like(state_ref)

    row = lax.broadcasted_iota(jnp.int32, (c, c), 0)
    col = lax.broadcasted_iota(jnp.int32, (c, c), 1)
    rel = (row - col).astype(F32)
    idx = lax.broadcasted_iota(jnp.int32, (c, 1), 0).astype(F32)
    for h in range(RET_HEADS):
        log_g = float(np.log(np.float32(1.0) - np.float32(2.0) ** np.float32(-5.0 - h)))
        cols = slice(h * RET_DIM, (h + 1) * RET_DIM)
        q = q_ref[:, cols]
        k = k_ref[:, cols]
        v = v_ref[:, cols]
        dmask = jnp.where(rel >= 0, jnp.exp(log_g * jnp.maximum(rel, 0.0)), 0.0)
        scores = _dot_nt(q, k) * dmask
        inner = _dot(scores.astype(BF16), v)
        zeta = jnp.exp(log_g * (c - 1.0 - idx))
        xi = jnp.exp(log_g * (idx + 1.0))
        prev = state_ref[h]
        cross = _dot(q, prev.astype(BF16)) * xi
        kz = (k.astype(F32) * zeta).astype(BF16)
        kv = lax.dot_general(kz, v, (((0,), (0,)), ((), ())), preferred_element_type=F32)
        state_ref[h] = prev * float(np.exp(np.float32(log_g) * np.float32(c))) + kv
        o = inner + cross
        mu = jnp.mean(o, axis=-1, keepdims=True)
        d = o - mu
        var = jnp.mean(d * d, axis=-1, keepdims=True)
        o = d * lax.rsqrt(var + LN_EPS)
        o_ref[:, cols] = (jax.nn.silu(g_ref[:, cols].astype(F32)) * o).astype(BF16)


def _retention(rq, rk, rv, rg, batch, seq):
    nc = seq // RET_CHUNK
    spec = pl.BlockSpec((RET_CHUNK, RET_WIDTH), lambda b, n: (b * nc + n, 0))
    return pl.pallas_call(
        _retention_kernel,
        grid=(batch, nc),
        in_specs=[spec] * 4,
        out_specs=spec,
        out_shape=jax.ShapeDtypeStruct(rq.shape, BF16),
        scratch_shapes=[pltpu.VMEM((RET_HEADS, RET_DIM, RET_DIM), F32)],
        compiler_params=_params(2),
        name="retention",
    )(rq, rk, rv, rg)


def _compress_kernel(a_ref, pe_ref, w1_ref, w2_ref, o_ref, shift_ref, *, n_cmp):
    rows = a_ref.shape[0]
    a = a_ref[...]
    lo = (a + pe_ref[0]).astype(BF16)
    hi = (a + pe_ref[1]).astype(BF16)
    ridx = lax.broadcasted_iota(jnp.int32, (rows, 1), 0)
    shift_ref[rows:rows + 8, :] = jnp.zeros((8, CMP_HIDDEN), F32)
    for g in range(NSA_KV_GROUPS):
        p = _dot(lo, w1_ref[0, g])
        shift_ref[0:rows, :] = _dot(hi, w1_ref[1, g])
        hid = jax.nn.silu(p + shift_ref[pl.ds(1, rows), :])
        out = _dot(hid.astype(BF16), w2_ref[...])
        o_ref[g] = jnp.where(ridx < n_cmp, out, 0.0).astype(BF16)


def _compress(a, pe, w1, w2, batch, seq):
    rows = seq // CMP_STRIDE
    per = CMP_STRIDE * KV_WIDTH
    n_cmp = (seq - CMP_LEN) // CMP_STRIDE + 1
    a2 = a.reshape(batch * rows, per)
    pe2 = jnp.tile(pe.reshape(2, CMP_STRIDE, 1, NSA_DIM), (1, 1, NSA_KV_GROUPS, 1)).reshape(2, 1, per)
    w1r = w1.reshape(2, CMP_STRIDE, 1, NSA_DIM, CMP_HIDDEN)
    eye = jnp.eye(NSA_KV_GROUPS, dtype=w1.dtype).reshape(1, NSA_KV_GROUPS, 1, NSA_KV_GROUPS, 1, 1)
    w1x = (w1r[:, None] * eye).reshape(2, NSA_KV_GROUPS, per, CMP_HIDDEN).astype(BF16)
    w2b = w2.astype(BF16)
    full = lambda arr: pl.BlockSpec(arr.shape, lambda b: (0,) * arr.ndim)
    return pl.pallas_call(
        functools.partial(_compress_kernel, n_cmp=n_cmp),
        grid=(batch,),
        in_specs=[pl.BlockSpec((rows, per), lambda b: (b, 0)), full(pe2), full(w1x), full(w2b)],
        out_specs=pl.BlockSpec((None, NSA_KV_GROUPS, rows, NSA_DIM), lambda b: (b, 0, 0, 0)),
        out_shape=jax.ShapeDtypeStruct((batch, NSA_KV_GROUPS, rows, NSA_DIM), BF16),
        scratch_shapes=[pltpu.VMEM((rows + 8, CMP_HIDDEN), F32)],
        compiler_params=_params(1),
        name="compress",
    )(a2, pe2, w1x, w2b)


def _heads_to_lanes(ref):
    vt = ref[...].astype(F32).T
    return jnp.concatenate([vt[h * NSA_DIM:(h + 1) * NSA_DIM] for h in range(NSA_HPG)], axis=1).astype(BF16)


def _tile_heads(v):
    return jnp.concatenate([v] * NSA_HPG, axis=1)


def _transpose_into(dst_ref, src_ref, chunk):
    def step(c, _):
        c0 = pl.multiple_of(c * chunk, chunk)
        dst_ref[:, pl.ds(c0, chunk)] = src_ref[pl.ds(c0, chunk), :].astype(F32).T.astype(BF16)
        return 0
    lax.fori_loop(0, src_ref.shape[0] // chunk, step, 0)


def _nsa_kernel(qraw_ref, qrot_ref, gate_ref, kcmp_ref, vcmp_ref, ovt_ref,
                ks_ref, vs_ref, kw_ref, vw_ref, o_ref, vst_ref, vwt_ref, vct_ref, bias_ref, *, tq, tk, seq):
    i = pl.program_id(2)
    t0 = i * tq
    cols = NSA_HPG * tq
    n_sel = seq // SEL_LEN
    n_cmp_rows = seq // CMP_STRIDE

    @pl.when(i == 0)
    def _():
        chunk = min(512, n_cmp_rows)
        _transpose_into(vst_ref, vs_ref, chunk)
        _transpose_into(vwt_ref, vw_ref, chunk)
        _transpose_into(vct_ref, vcmp_ref, chunk)

    q_raw = _heads_to_lanes(qraw_ref)
    q_rot = _heads_to_lanes(qrot_ref)
    t_row = t0 + lax.broadcasted_iota(jnp.int32, (1, tq), 1)

    chain_w = cols // NSA_CHAINS
    heads_per_chain = chain_w // tq
    chains = [slice(c * chain_w, (c + 1) * chain_w) for c in range(NSA_CHAINS)]
    tile_chain = lambda v: jnp.concatenate([v] * heads_per_chain, axis=1)

    span = WIN + tq
    ws = pl.multiple_of(jnp.maximum(t0 - WIN, 0), tq)
    dist = t_row - (ws + lax.broadcasted_iota(jnp.int32, (span, 1), 0))
    bias_w = tile_chain(jnp.where((dist >= 0) & (dist < WIN), 0.0, NEG))
    k_w = kw_ref[pl.ds(ws, span), :]
    v_w = vwt_ref[:, pl.ds(ws, span)]
    o_w = []
    for c in chains:
        s_w = _dot(k_w, q_rot[:, c]) + bias_w
        p_w = jnp.exp(s_w - jnp.max(s_w, axis=0, keepdims=True))
        o_w.append(_dot(v_w, p_w.astype(BF16)) / jnp.sum(p_w, axis=0, keepdims=True))
    o_w = jnp.concatenate(o_w, axis=1)

    c_idx = lax.broadcasted_iota(jnp.int32, (n_cmp_rows, 1), 0)
    valid = tile_chain(jnp.where(c_idx * CMP_STRIDE + (CMP_LEN - 1) <= t_row, 1.0, 0.0))
    bias_c = (valid - 1.0) * (-NEG)
    o_c = []
    p_sum = None
    for c in chains:
        s_c = _dot(kcmp_ref[...], q_raw[:, c]) + bias_c
        e_c = jnp.exp(s_c - jnp.max(s_c, axis=0, keepdims=True)) * valid
        l_c = jnp.sum(e_c, axis=0, keepdims=True)
        p_c = e_c / jnp.where(l_c > 0.0, l_c, 1.0)
        o_c.append(_dot(vct_ref[...], p_c.astype(BF16)))
        for h in range(heads_per_chain):
            p_h = p_c[:, h * tq:(h + 1) * tq]
            p_sum = p_h if p_sum is None else p_sum + p_h
    o_c = jnp.concatenate(o_c, axis=1)

    p_hi = p_sum.astype(BF16)
    p_lo = (p_sum - p_hi.astype(F32)).astype(BF16)
    ovt = ovt_ref[...]
    imp = _dot(ovt, p_hi) + _dot(ovt, p_lo)
    jb = lax.broadcasted_iota(jnp.int32, (n_sel, tq), 0)
    cur = (t0 + lax.broadcasted_iota(jnp.int32, (n_sel, tq), 1)) >> SEL_SHIFT
    forced = (jb == 0) | (jb == cur) | (jb == cur - 1)
    work = jnp.where(forced, FORCE, imp)
    work = jnp.where(jb <= cur, work, NEG)
    sel_t = jnp.zeros((n_sel, tq), F32)
    for _ in range(min(SEL_TOPK, n_sel)):
        best = jnp.max(work, axis=0, keepdims=True)
        first = jnp.min(jnp.where(work == best, jb, n_sel), axis=0, keepdims=True)
        hit = jb == first
        sel_t = jnp.where(hit, 1.0, sel_t)
        work = jnp.where(hit, -jnp.inf, work)
    bias_ref[...] = jnp.where(sel_t > 0.5, 0.0, NEG)

    blocks_per_tile = tk // SEL_LEN

    def sel_tile(kt, carry, causal):
        k0 = pl.multiple_of(kt * tk, tk)
        bias = jnp.concatenate(
            [jnp.broadcast_to(bias_ref[pl.ds(kt * blocks_per_tile + j, 1), :], (SEL_LEN, tq))
             for j in range(blocks_per_tile)], axis=0)
        if causal:
            kpos = k0 + lax.broadcasted_iota(jnp.int32, (tk, 1), 0)
            bias = jnp.where(kpos <= t_row, bias, NEG)
        bias = tile_chain(bias)
        k_t = ks_ref[pl.ds(k0, tk), :]
        v_t = vst_ref[:, pl.ds(k0, tk)]
        out = []
        scores = [_dot(k_t, q_rot[:, c]) + bias for c in chains]
        for (m, l, acc), s in zip(carry, scores):
            m_new = jnp.maximum(m, jnp.max(s, axis=0, keepdims=True))
            alpha = jnp.exp(m - m_new)
            p = jnp.exp(s - m_new)
            l = alpha * l + jnp.sum(p, axis=0, keepdims=True)
            acc = alpha * acc + _dot(v_t, p.astype(BF16))
            out.append((m_new, l, acc))
        return tuple(out)

    n_full = t0 // tk
    init = tuple((jnp.full((1, chain_w), NEG, F32), jnp.zeros((1, chain_w), F32),
                  jnp.zeros((NSA_DIM, chain_w), F32)) for _ in chains)
    carry = lax.fori_loop(0, n_full, functools.partial(sel_tile, causal=False), init)
    carry = sel_tile(n_full, carry, causal=True)
    o_s = jnp.concatenate([acc / l for _, l, acc in carry], axis=1)

    gt = gate_ref[...]
    outs = []
    for h in range(NSA_HPG):
        c = slice(h * tq, (h + 1) * tq)
        outs.append(gt[3 * h:3 * h + 1] * o_c[:, c] + gt[3 * h + 1:3 * h + 2] * o_s[:, c]
                    + gt[3 * h + 2:3 * h + 3] * o_w[:, c])
    o_ref[...] = jnp.concatenate(outs, axis=0).T.astype(BF16)


def _nsa(nq, nqr, gates, kcmp, vcmp, ks, vs, kw, vw, batch, seq):
    n = batch * seq
    tq = 512
    tk = 512 if seq % 512 == 0 else seq
    nqb = seq // tq
    n_sel = seq // SEL_LEN
    rows_c = seq // CMP_STRIDE
    gw = NSA_HPG * NSA_DIM
    cs = np.arange(rows_c)[None, :] * CMP_STRIDE
    ss = np.arange(n_sel)[:, None] * SEL_LEN
    n_cmp = (seq - CMP_LEN) // CMP_STRIDE + 1
    ovt = ((cs < ss + SEL_LEN) & (cs + CMP_LEN > ss) & (np.arange(rows_c)[None, :] < n_cmp))
    ovt = jnp.asarray(ovt.astype(np.float32), BF16)

    qspec = pl.BlockSpec((tq, gw), lambda b, g, i: (b * nqb + i, g))
    cspec = pl.BlockSpec((None, None, rows_c, NSA_DIM), lambda b, g, i: (b, g, 0, 0))
    kspec = pl.BlockSpec((None, seq, NSA_DIM), lambda b, g, i: (g, b, 0))
    return pl.pallas_call(
        functools.partial(_nsa_kernel, tq=tq, tk=tk, seq=seq),
        grid=(batch, NSA_KV_GROUPS, nqb),
        in_specs=[qspec, qspec,
                  pl.BlockSpec((None, GATE_LANES, tq), lambda b, g, i: (g, 0, b * nqb + i)),
                  cspec, cspec, pl.BlockSpec(ovt.shape, lambda b, g, i: (0, 0)),
                  kspec, kspec, kspec, kspec],
        out_specs=qspec,
        out_shape=jax.ShapeDtypeStruct((n, NSA_WIDTH), BF16),
        scratch_shapes=[pltpu.VMEM((NSA_DIM, seq), BF16), pltpu.VMEM((NSA_DIM, seq), BF16),
                        pltpu.VMEM((NSA_DIM, rows_c), BF16), pltpu.VMEM((n_sel, tq), F32)],
        compiler_params=_params(3),
        name="nsa",
    )(nq, nqr, gates, kcmp, vcmp, ovt, ks, vs, kw, vw)


def _memkv_kernel(mem_ref, w_ref, kv_ref):
    kv_ref[...] = _dot(mem_ref[...].astype(BF16), w_ref[...]).astype(BF16)


def _memkv(mem2d, w_xkv):
    n = mem2d.shape[0]
    w = w_xkv.astype(BF16)
    return pl.pallas_call(
        _memkv_kernel,
        grid=(n // MEM_LEN,),
        in_specs=[pl.BlockSpec((MEM_LEN, D_MODEL), lambda i: (i, 0)),
                  pl.BlockSpec(w.shape, lambda i: (0, 0))],
        out_specs=pl.BlockSpec((MEM_LEN, 2 * D_MODEL), lambda i: (i, 0)),
        out_shape=jax.ShapeDtypeStruct((n, 2 * D_MODEL), BF16),
        compiler_params=_params(1),
        name="memkv",
    )(mem2d, w)


ROW_TILES = D_MODEL // 2 // LANES


def _store_packed_rows(ref, v):
    half = D_MODEL // 2
    rows = v.shape[0]
    hi = pltpu.bitcast(v[:, :half].astype(BF16).astype(F32), jnp.uint32)
    lo = pltpu.bitcast(v[:, half:].astype(BF16).astype(F32), jnp.uint32)
    packed = hi | (lo >> 16)
    for s in range(ROW_TILES):
        ref[pl.ds(s, rows, stride=ROW_TILES), :] = packed[:, s * LANES:(s + 1) * LANES]


def _load_packed_halves(ref):
    rows = ref.shape[0] // ROW_TILES
    words = [ref[pl.ds(s, rows, stride=ROW_TILES), :] for s in range(ROW_TILES)]
    hi = jnp.concatenate([pltpu.bitcast(w & jnp.uint32(0xFFFF0000), F32) for w in words], axis=1)
    lo = jnp.concatenate([pltpu.bitcast(w << 16, F32) for w in words], axis=1)
    return hi, lo


def _postmix_kernel(x_ref, oret_ref, onsa_ref, kv_ref, wout_ref, wq_ref, wo_ref,
                    g1_ref, b1_ref, g2_ref, b2_ref, x2_ref, x2p_ref):
    mixed = jnp.concatenate([oret_ref[...], onsa_ref[...]], axis=1)
    x1 = _layer_norm(DN_ALPHA * x_ref[...] + _dot(mixed, wout_ref[...]), g1_ref[...], b1_ref[...])
    q = (_dot(x1.astype(BF16), wq_ref[...]) * (XATT_DIM ** -0.5)).astype(BF16)
    heads = []
    for h in range(XATT_HEADS):
        cols = slice(h * XATT_DIM, (h + 1) * XATT_DIM)
        s = _dot_nt(q[:, cols], kv_ref[:, cols])
        m = jnp.max(s, axis=-1, keepdims=True)
        p = jnp.exp(s - m)
        l = jnp.sum(p, axis=-1, keepdims=True)
        heads.append(_dot(p.astype(BF16), kv_ref[:, D_MODEL + h * XATT_DIM:D_MODEL + (h + 1) * XATT_DIM]) / l)
    att = jnp.concatenate(heads, axis=1).astype(BF16)
    x2 = _layer_norm(DN_ALPHA * x1 + _dot(att, wo_ref[...]), g2_ref[...], b2_ref[...])
    x2_ref[...] = x2
    _store_packed_rows(x2p_ref, x2)


def _postmix(x2d, o_ret, o_nsa, kvx, w_out, w_xq, w_xo, ln1_g, ln1_b, ln2_g, ln2_b, batch, seq):
    n = x2d.shape[0]
    tm = 256 if seq % 256 == 0 else seq
    per_b = seq // tm
    row = lambda w: pl.BlockSpec((tm, w), lambda b, i: (b * per_b + i, 0))
    full = lambda a: pl.BlockSpec(a.shape, lambda b, i: (0,) * a.ndim)
    ws = [w_out.astype(BF16), w_xq.astype(BF16), w_xo.astype(BF16)]
    vecs = [v.reshape(1, D_MODEL) for v in (ln1_g, ln1_b, ln2_g, ln2_b)]
    return pl.pallas_call(
        _postmix_kernel,
        grid=(batch, per_b),
        in_specs=[row(D_MODEL), row(RET_WIDTH), row(NSA_WIDTH),
                  pl.BlockSpec((MEM_LEN, 2 * D_MODEL), lambda b, i: (b, 0))]
                 + [full(w) for w in ws] + [full(v) for v in vecs],
        out_specs=[row(D_MODEL),
                   pl.BlockSpec((tm * ROW_TILES, LANES), lambda b, i: (b * per_b + i, 0))],
        out_shape=[jax.ShapeDtypeStruct((n, D_MODEL), F32),
                   jax.ShapeDtypeStruct((n * ROW_TILES, LANES), jnp.uint32)],
        compiler_params=_params(2),
        name="postmix",
    )(x2d, o_ret, o_nsa, kvx, *ws, *vecs)


def _router_kernel(x_ref, wr_ref, bias_ref, e_ref, rank_ref, w_ref, cnt_ref, cntrow_ref, carry_ref, carryrow_ref):
    tn = x_ref.shape[0]
    per = N_EXPERTS // N_GROUPS

    @pl.when(pl.program_id(0) == 0)
    def _():
        carry_ref[...] = jnp.zeros_like(carry_ref)
        carryrow_ref[...] = jnp.zeros_like(carryrow_ref)

    logits = _dot_nt(wr_ref[...], x_ref[...].astype(BF16))
    scores = jax.nn.sigmoid(logits)
    biased = scores + bias_ref[...]
    b3 = biased.reshape(N_GROUPS, per, tn)
    member = lax.broadcasted_iota(jnp.int32, (N_GROUPS, per, tn), 1)
    top1 = jnp.max(b3, axis=1, keepdims=True)
    first1 = jnp.min(jnp.where(b3 == top1, member, per), axis=1, keepdims=True)
    top2 = jnp.max(jnp.where(member == first1, -jnp.inf, b3), axis=1, keepdims=True)
    gscore = top1 + top2
    gidx = lax.broadcasted_iota(jnp.int32, (N_GROUPS, 1, tn), 0)
    gwork = gscore
    for _ in range(TOPK_GROUPS - 1):
        gbest = jnp.max(gwork, axis=0, keepdims=True)
        gfirst = jnp.min(jnp.where(gwork == gbest, gidx, N_GROUPS), axis=0, keepdims=True)
        gwork = jnp.where(gidx == gfirst, -jnp.inf, gwork)
    kth = jnp.max(gwork, axis=0, keepdims=True)
    work = jnp.where(gscore >= kth, b3, NEG).reshape(N_EXPERTS, tn)
    eidx = lax.broadcasted_iota(jnp.int32, (N_EXPERTS, tn), 0)
    picks = []
    chosen = jnp.zeros((N_EXPERTS, tn), F32)
    for _ in range(TOP_K):
        best = jnp.max(work, axis=0, keepdims=True)
        first = jnp.min(jnp.where(work == best, eidx, N_EXPERTS), axis=0, keepdims=True)
        hit = eidx == first
        picks.append((first, hit))
        chosen = jnp.where(hit, 1.0, chosen)
        work = jnp.where(hit, -jnp.inf, work)

    r_i = lax.broadcasted_iota(jnp.int32, (tn, tn), 0)
    c_i = lax.broadcasted_iota(jnp.int32, (tn, tn), 1)
    before = jnp.where(r_i < c_i, 1.0, 0.0).astype(BF16)
    chosen_b = chosen.astype(BF16)
    rank = _dot(chosen_b, before) + carry_ref[...]
    carry_ref[...] = carry_ref[...] + jnp.sum(chosen, axis=1, keepdims=True)
    carryrow_ref[...] = carryrow_ref[...] + _dot_nt(jnp.ones((8, tn), BF16), chosen_b)
    cnt_ref[...] = carry_ref[...]
    cntrow_ref[...] = carryrow_ref[...]

    wsel = [jnp.sum(jnp.where(hit, scores, 0.0), axis=0, keepdims=True) for _, hit in picks]
    wsum = wsel[0]
    for v in wsel[1:]:
        wsum = wsum + v
    for kk, (first, hit) in enumerate(picks):
        e_ref[kk:kk + 1, :] = first
        rank_ref[kk:kk + 1, :] = jnp.sum(jnp.where(hit, rank, 0.0), axis=0, keepdims=True).astype(jnp.int32)
        w_ref[kk:kk + 1, :] = wsel[kk] / wsum * ROUTED_SCALE


def _router(x2, w_router, router_bias):
    n = x2.shape[0]
    tn = 512 if n % 512 == 0 else n
    wr_t = w_router.T.astype(BF16)
    bias = router_bias.reshape(N_EXPERTS, 1).astype(F32)
    kspec = pl.BlockSpec((TOP_K, tn), lambda i: (0, i))
    return pl.pallas_call(
        _router_kernel,
        grid=(n // tn,),
        in_specs=[pl.BlockSpec((tn, D_MODEL), lambda i: (i, 0)),
                  pl.BlockSpec(wr_t.shape, lambda i: (0, 0)),
                  pl.BlockSpec(bias.shape, lambda i: (0, 0))],
        out_specs=[kspec, kspec, kspec, pl.BlockSpec((N_EXPERTS, 1), lambda i: (0, 0)),
                   pl.BlockSpec((8, N_EXPERTS), lambda i: (0, 0))],
        out_shape=[jax.ShapeDtypeStruct((TOP_K, n), jnp.int32),
                   jax.ShapeDtypeStruct((TOP_K, n), jnp.int32),
                   jax.ShapeDtypeStruct((TOP_K, n), F32),
                   jax.ShapeDtypeStruct((N_EXPERTS, 1), F32),
                   jax.ShapeDtypeStruct((8, N_EXPERTS), F32)],
        scratch_shapes=[pltpu.VMEM((N_EXPERTS, 1), F32), pltpu.VMEM((8, N_EXPERTS), F32)],
        compiler_params=_params(1),
        name="router",
    )(x2, wr_t, bias)


def _slots_kernel(e_ref, rank_ref, cnt_ref, cntrow_ref, dest_ref, blk_e_ref, nblk_ref, *, blk, n_blocks):
    pad = lambda c: jnp.ceil(c / blk) * blk
    padded = pad(cnt_ref[...])
    padded_row = pad(cntrow_ref[0:1, :])
    r_i = lax.broadcasted_iota(jnp.int32, (N_EXPERTS, N_EXPERTS), 0)
    c_i = lax.broadcasted_iota(jnp.int32, (N_EXPERTS, N_EXPERTS), 1)
    start = jnp.sum(jnp.where(c_i < r_i, padded_row, 0.0), axis=1, keepdims=True)
    end = start + padded
    e = e_ref[...]
    dest = rank_ref[...]
    for ex in range(N_EXPERTS):
        dest = dest + jnp.where(e == ex, start[ex:ex + 1, :].astype(jnp.int32), 0)
    dest_ref[...] = dest
    bstart = (lax.broadcasted_iota(jnp.int32, (1, n_blocks), 1) * blk).astype(F32)
    owner = jnp.sum(jnp.where(end <= bstart, 1.0, 0.0), axis=0, keepdims=True)
    blk_e_ref[...] = jnp.minimum(owner, N_EXPERTS - 1.0).astype(jnp.int32)
    nblk_ref[...] = (end[N_EXPERTS - 1:N_EXPERTS, :] / blk).astype(jnp.int32)


def _slots(e_k, rank_k, counts, counts_row, blk, n_blocks):
    n = e_k.shape[1]
    full = lambda shape: pl.BlockSpec(shape, lambda: (0,) * len(shape))
    return pl.pallas_call(
        functools.partial(_slots_kernel, blk=blk, n_blocks=n_blocks),
        in_specs=[full((TOP_K, n)), full((TOP_K, n)), full((N_EXPERTS, 1)), full((8, N_EXPERTS))],
        out_specs=[full((TOP_K, n)), full((1, n_blocks)), full((1, 1))],
        out_shape=[jax.ShapeDtypeStruct((TOP_K, n), jnp.int32),
                   jax.ShapeDtypeStruct((1, n_blocks), jnp.int32),
                   jax.ShapeDtypeStruct((1, 1), jnp.int32)],
        compiler_params=pltpu.CompilerParams(vmem_limit_bytes=VMEM_LIMIT),
        name="slots",
    )(e_k, rank_k, counts, counts_row)


def _row_copy(src_ref, src_row, dst_ref, dst_row, sem):
    span = lambda r: pl.ds(pl.multiple_of(r * ROW_TILES, ROW_TILES), ROW_TILES)
    return pltpu.make_async_copy(src_ref.at[span(src_row)], dst_ref.at[span(dst_row)], sem)


def _dest_spec(tt):
    return pl.BlockSpec((None, 1, TOP_K * tt), lambda i: (i, 0, 0), memory_space=pltpu.SMEM)


def _dispatch_kernel(dest_ref, x_ref, zero_ref, xs_ref, sem, *, tt):
    del zero_ref

    def issue(j, _):
        for kk in range(TOP_K):
            _row_copy(x_ref, j, xs_ref, dest_ref[0, kk * tt + j], sem).start(priority=kk % 2)
        return 0

    lax.fori_loop(0, tt, issue, 0)

    def drain(j, _):
        for kk in range(TOP_K):
            _row_copy(x_ref, 0, xs_ref, 0, sem).wait()
        return 0

    lax.fori_loop(0, tt, drain, 0)


def _dispatch(dest_tiles, x2p, cap, tt):
    n = x2p.shape[0] // ROW_TILES
    zeros = jnp.zeros((cap * ROW_TILES, LANES), x2p.dtype)
    return pl.pallas_call(
        functools.partial(_dispatch_kernel, tt=tt),
        grid=(n // tt,),
        in_specs=[_dest_spec(tt),
                  pl.BlockSpec((tt * ROW_TILES, LANES), lambda i: (i, 0)),
                  pl.BlockSpec(memory_space=pl.ANY)],
        out_specs=pl.BlockSpec(memory_space=pl.ANY),
        scratch_shapes=[pltpu.SemaphoreType.DMA(())],
        out_shape=jax.ShapeDtypeStruct(zeros.shape, x2p.dtype),
        input_output_aliases={2: 0},
        compiler_params=_params(1),
        name="dispatch",
    )(dest_tiles, x2p, zeros)


def _experts_kernel(blk_e_ref, nblk_ref, xs_ref, wg_ref, wu_ref, wd_ref, y_ref):
    del blk_e_ref

    @pl.when(pl.program_id(0) < nblk_ref[0])
    def _():
        half = D_MODEL // 2
        hi, lo = (v.astype(BF16) for v in _load_packed_halves(xs_ref))
        gate = _dot(hi, wg_ref[:half, :]) + _dot(lo, wg_ref[half:, :])
        up = _dot(hi, wu_ref[:half, :]) + _dot(lo, wu_ref[half:, :])
        _store_packed_rows(y_ref, _dot((jax.nn.silu(gate) * up).astype(BF16), wd_ref[...]))

    @pl.when(pl.program_id(0) >= nblk_ref[0])
    def _():
        y_ref[...] = jnp.zeros_like(y_ref)


def _experts(blk_e, nblk, xs, w_gate, w_up, w_down, blk):
    cap = xs.shape[0] // ROW_TILES
    wg, wu, wd = (a.astype(BF16) for a in (w_gate, w_up, w_down))
    wspec = lambda a: pl.BlockSpec((None,) + a.shape[1:], lambda i, be, nb: (be[i], 0, 0))
    rows = pl.BlockSpec((blk * ROW_TILES, LANES), lambda i, be, nb: (i, 0))
    return pl.pallas_call(
        _experts_kernel,
        grid_spec=pltpu.PrefetchScalarGridSpec(
            num_scalar_prefetch=2,
            grid=(cap // blk,),
            in_specs=[rows, wspec(wg), wspec(wu), wspec(wd)],
            out_specs=rows,
        ),
        out_shape=jax.ShapeDtypeStruct(xs.shape, xs.dtype),
        compiler_params=_params(1),
        name="experts",
    )(blk_e, nblk, xs, wg, wu, wd)


def _combine_kernel(dest_ref, dnext_ref, x_ref, wk_ref, y_ref, wsg_ref, wsu_ref, wsd_ref, g_ref, b_ref,
                    o_ref, buf_ref, sem, *, tt):
    i = pl.program_id(0)
    slot = i % 2

    def gather(idx_ref, to_slot):
        def issue(j, _):
            for kk in range(TOP_K):
                _row_copy(y_ref, idx_ref[0, kk * tt + j], buf_ref.at[to_slot, kk], j,
                          sem.at[to_slot]).start(priority=kk % 2)
            return 0
        lax.fori_loop(0, tt, issue, 0)

    @pl.when(i == 0)
    def _():
        gather(dest_ref, slot)

    @pl.when(i + 1 < pl.num_programs(0))
    def _():
        gather(dnext_ref, 1 - slot)

    x = x_ref[...]
    xb = x.astype(BF16)
    shared = _dot((jax.nn.silu(_dot(xb, wsg_ref[...])) * _dot(xb, wsu_ref[...])).astype(BF16), wsd_ref[...])

    def drain(j, _):
        for kk in range(TOP_K):
            _row_copy(y_ref, 0, buf_ref.at[slot, kk], 0, sem.at[slot]).wait()
        return 0

    lax.fori_loop(0, tt, drain, 0)

    wk = wk_ref[...]
    routed_hi = routed_lo = None
    for kk in range(TOP_K):
        hi, lo = _load_packed_halves(buf_ref.at[slot, kk])
        w = wk[:, kk:kk + 1]
        routed_hi = hi * w if kk == 0 else routed_hi + hi * w
        routed_lo = lo * w if kk == 0 else routed_lo + lo * w
    routed = jnp.concatenate([routed_hi, routed_lo], axis=1)
    o_ref[...] = _layer_norm(DN_ALPHA * x + (routed + shared), g_ref[...], b_ref[...])


def _combine(dest_tiles, x2, w_tok, y, ws_gate, ws_up, ws_down, ln3_g, ln3_b, tt):
    n = x2.shape[0]
    steps = n // tt
    ws = [ws_gate.astype(BF16), ws_up.astype(BF16), ws_down.astype(BF16)]
    vecs = [ln3_g.reshape(1, D_MODEL), ln3_b.reshape(1, D_MODEL)]
    full = lambda a: pl.BlockSpec(a.shape, lambda i: (0,) * a.ndim)
    next_spec = pl.BlockSpec((None, 1, TOP_K * tt), lambda i: (jnp.minimum(i + 1, steps - 1), 0, 0),
                             memory_space=pltpu.SMEM)
    return pl.pallas_call(
        functools.partial(_combine_kernel, tt=tt),
        grid=(steps,),
        in_specs=[_dest_spec(tt), next_spec,
                  pl.BlockSpec((tt, D_MODEL), lambda i: (i, 0)),
                  pl.BlockSpec((tt, TOP_K), lambda i: (i, 0)),
                  pl.BlockSpec(memory_space=pl.ANY)]
                 + [full(a) for a in ws] + [full(v) for v in vecs],
        out_specs=pl.BlockSpec((tt, D_MODEL), lambda i: (i, 0)),
        scratch_shapes=[pltpu.VMEM((2, TOP_K, tt * ROW_TILES, LANES), jnp.uint32),
                        pltpu.SemaphoreType.DMA((2,))],
        out_shape=jax.ShapeDtypeStruct((n, D_MODEL), F32),
        compiler_params=_params(1),
        name="combine",
    )(dest_tiles, dest_tiles, x2, w_tok, y, *ws, *vecs)


def _moe_and_norm(x2, x2p, w_router, router_bias, w_gate, w_up, w_down, ws_gate, ws_up, ws_down,
                  ln3_g, ln3_b):
    n = x2.shape[0]
    blk = 512
    tt = 128 if n % 128 == 0 else n
    cap = n * TOP_K + N_EXPERTS * blk
    n_blocks = cap // blk
    e_k, rank_k, w_k, counts, counts_row = _router(x2, w_router, router_bias)
    dest, blk_e, nblk = _slots(e_k, rank_k, counts, counts_row, blk, n_blocks)
    dest_tiles = dest.reshape(TOP_K, n // tt, tt).transpose(1, 0, 2).reshape(n // tt, 1, TOP_K * tt)
    xs = _dispatch(dest_tiles, x2p, cap, tt)
    y = _experts(blk_e.reshape(-1), nblk.reshape(-1), xs, w_gate, w_up, w_down, blk)
    return _combine(dest_tiles, x2, w_k.T, y, ws_gate, ws_up, ws_down, ln3_g, ln3_b, tt)


def _layer(x, mem, positions, w_in, cmp_pe_k, cmp_pe_v, cmp_w1_k, cmp_w2_k, cmp_w1_v, cmp_w2_v,
           w_out, ln1_g, ln1_b, w_xq, w_xkv, w_xo, ln2_g, ln2_b, w_router, router_bias,
           w_gate, w_up, w_down, ws_gate, ws_up, ws_down, ln3_g, ln3_b):
    batch, seq, _ = x.shape
    n = batch * seq
    x2d = x.reshape(n, D_MODEL)
    pos_col = positions.astype(F32).reshape(n, 1)
    (rq, rk, rv, rg, nq, nqr, kc, vc, ks, vs, kw, vw, gates) = _inproj(x2d, pos_col, w_in)
    o_ret = _retention(rq, rk, rv, rg, batch, seq)
    kcmp = _compress(kc, cmp_pe_k, cmp_w1_k, cmp_w2_k, batch, seq)
    vcmp = _compress(vc, cmp_pe_v, cmp_w1_v, cmp_w2_v, batch, seq)
    o_nsa = _nsa(nq, nqr, gates, kcmp, vcmp, ks, vs, kw, vw, batch, seq)
    kvx = _memkv(mem.reshape(batch * MEM_LEN, D_MODEL), w_xkv)
    x2, x2p = _postmix(x2d, o_ret, o_nsa, kvx, w_out, w_xq, w_xo, ln1_g, ln1_b, ln2_g, ln2_b, batch, seq)
    out = _moe_and_norm(x2, x2p, w_router, router_bias, w_gate, w_up, w_down,
                        ws_gate, ws_up, ws_down, ln3_g, ln3_b)
    return out.reshape(batch, seq, D_MODEL)


def kernel(x, mem, positions, w_in, cmp_pe_k, cmp_pe_v, cmp_w1_k, cmp_w2_k, cmp_w1_v, cmp_w2_v, w_out, ln1_g, ln1_b, w_xq, w_xkv, w_xo, ln2_g, ln2_b, w_router, router_bias, w_gate, w_up, w_down, ws_gate, ws_up, ws_down, ln3_g, ln3_b):
    for l in range(DEPTH):
        x = _layer(x, mem, positions, w_in[l], cmp_pe_k[l], cmp_pe_v[l], cmp_w1_k[l], cmp_w2_k[l],
                   cmp_w1_v[l], cmp_w2_v[l], w_out[l], ln1_g[l], ln1_b[l], w_xq[l], w_xkv[l],
                   w_xo[l], ln2_g[l], ln2_b[l], w_router[l], router_bias[l], w_gate[l], w_up[l],
                   w_down[l], ws_gate[l], ws_up[l], ws_down[l], ln3_g[l], ln3_b[l])
    return x
```

```python
import functools

import numpy as np
import jax
import jax.numpy as jnp
from jax import lax
from jax.experimental import pallas as pl
from jax.experimental.pallas import tpu as pltpu

D_MODEL = 1024
MEM_LEN = 256
DEPTH = 1
DN_ALPHA = (2 * DEPTH) ** 0.25
LN_EPS = 1e-5
NEG = -1e30
FORCE = 1e9

RET_HEADS = 4
RET_DIM = 128
RET_CHUNK = 128
RET_ROPE_BASE = 10000.0
RET_WIDTH = RET_HEADS * RET_DIM

NSA_HEADS = 8
NSA_KV_GROUPS = 2
NSA_HPG = NSA_HEADS // NSA_KV_GROUPS
NSA_DIM = 64
NSA_WIDTH = NSA_HEADS * NSA_DIM
KV_WIDTH = NSA_KV_GROUPS * NSA_DIM
CMP_LEN = 32
CMP_STRIDE = 16
CMP_HIDDEN = 256
SEL_LEN = 64
SEL_SHIFT = 6
SEL_TOPK = 16
WIN = 512
ROPE_THETA = 500000.0
ROPE_DIMS = NSA_DIM // 4
GATE_LANES = 16
NSA_CHAINS = 2
SUM_ROWS = 16

XATT_HEADS = 4
XATT_DIM = D_MODEL // XATT_HEADS

N_EXPERTS = 64
TOP_K = 8
N_GROUPS = 8
TOPK_GROUPS = 4
EXPERT_FF = 256
SHARED_FF = 256
ROUTED_SCALE = 2.5

LANES = 128
VMEM_LIMIT = 56 * 1024 * 1024

F32 = jnp.float32
BF16 = jnp.bfloat16
NT_DIMS = (((1,), (1,)), ((), ()))


def _params(n_axes):
    return pltpu.CompilerParams(dimension_semantics=("arbitrary",) * n_axes,
                                vmem_limit_bytes=VMEM_LIMIT)


def _dot(a, b):
    return jnp.dot(a, b, preferred_element_type=F32)


def _dot_nt(a, b):
    return lax.dot_general(a, b, NT_DIMS, preferred_element_type=F32)


def _layer_norm(v, g, b):
    mu = jnp.mean(v, axis=-1, keepdims=True)
    d = v - mu
    var = jnp.mean(d * d, axis=-1, keepdims=True)
    return d * lax.rsqrt(var + LN_EPS) * g + b


def _inproj_kernel(x_ref, pos_ref, wret_ref, wnq_ref, wkv_ref, wg_ref, invr_ref, invn_ref,
                   rq_ref, rk_ref, rv_ref, rg_ref, nq_ref, nqr_ref, kc_ref, vc_ref,
                   ks_ref, vs_ref, kw_ref, vw_ref, gate_ref):
    xb = x_ref[...].astype(BF16)
    pos = pos_ref[...]
    lane = lax.broadcasted_iota(jnp.int32, (1, LANES), 1)

    ang = pos * invr_ref[...]
    cos_r = jnp.cos(ang)
    sin_r = jnp.sin(ang)
    sin_r = jnp.where(lane < RET_DIM // 2, -sin_r, sin_r)
    q_all = _dot(xb, wret_ref[:, :RET_WIDTH])
    k_all = _dot(xb, wret_ref[:, RET_WIDTH:2 * RET_WIDTH])
    for h in range(RET_HEADS):
        cols = slice(h * RET_DIM, (h + 1) * RET_DIM)
        q = q_all[:, cols]
        rq_ref[:, cols] = (q * cos_r + pltpu.roll(q, RET_DIM // 2, 1) * sin_r).astype(BF16)
        k = k_all[:, cols]
        k = (k * cos_r + pltpu.roll(k, RET_DIM // 2, 1) * sin_r) * (RET_DIM ** -0.5)
        rk_ref[:, cols] = k.astype(BF16)
    rv_ref[...] = _dot(xb, wret_ref[:, 2 * RET_WIDTH:3 * RET_WIDTH]).astype(BF16)
    rg_ref[...] = _dot(xb, wret_ref[:, 3 * RET_WIDTH:4 * RET_WIDTH]).astype(BF16)

    half = ROPE_DIMS // 2
    j = lane % NSA_DIM
    angn = pos * invn_ref[...]
    cos_n = jnp.cos(angn)
    sin_n = jnp.sin(angn)
    sin_lo = jnp.where(j < half, -sin_n, 0.0)
    sin_hi = jnp.where((j >= half) & (j < 2 * half), sin_n, 0.0)

    def rope_n(v):
        return v * cos_n + pltpu.roll(v, half, 1) * sin_hi + pltpu.roll(v, LANES - half, 1) * sin_lo

    scale = NSA_DIM ** -0.5
    nq_all = _dot(xb, wnq_ref[...])
    for c in range(NSA_WIDTH // LANES):
        cols = slice(c * LANES, (c + 1) * LANES)
        q = nq_all[:, cols]
        nq_ref[:, cols] = (q * scale).astype(BF16)
        nqr_ref[:, cols] = (rope_n(q) * scale).astype(BF16)

    kv_all = _dot(xb, wkv_ref[...])

    def kv(i):
        return kv_all[:, i * KV_WIDTH:(i + 1) * KV_WIDTH]

    def split_groups(ref, v):
        for g in range(NSA_KV_GROUPS):
            ref[g] = v[:, g * NSA_DIM:(g + 1) * NSA_DIM].astype(BF16)

    kc_ref[...] = kv(0)
    vc_ref[...] = kv(1)
    split_groups(ks_ref, rope_n(kv(2)))
    split_groups(vs_ref, kv(3))
    split_groups(kw_ref, rope_n(kv(4)))
    split_groups(vw_ref, kv(5))

    gt = jax.nn.sigmoid(_dot_nt(wg_ref[...], xb))
    for g in range(NSA_KV_GROUPS):
        gate_ref[g] = gt[g * GATE_LANES:(g + 1) * GATE_LANES, :]


def _inproj(x2d, pos_col, w_in):
    n = x2d.shape[0]
    tm = 512 if n % 512 == 0 else n
    off = np.cumsum([0] + [RET_WIDTH] * 4 + [NSA_WIDTH] + [KV_WIDTH] * 6)
    w_ret = w_in[:, :off[4]].astype(BF16)
    w_nq = w_in[:, off[4]:off[5]].astype(BF16)
    w_kv = w_in[:, off[5]:off[11]].astype(BF16)
    wg = w_in[:, off[11]:].reshape(D_MODEL, NSA_KV_GROUPS, NSA_HPG * 3)
    wg = jnp.pad(wg, ((0, 0), (0, 0), (0, GATE_LANES - NSA_HPG * 3)))
    wg = wg.reshape(D_MODEL, NSA_KV_GROUPS * GATE_LANES).T.astype(BF16)

    lane = np.arange(LANES)
    half_r = RET_DIM // 2
    inv_r = (np.float32(RET_ROPE_BASE) ** (-np.arange(half_r, dtype=np.float32) / np.float32(half_r)))
    inv_r = inv_r.astype(np.float32)[lane % half_r][None, :]
    half_n = ROPE_DIMS // 2
    inv_n = (np.float32(ROPE_THETA) ** (-np.arange(half_n, dtype=np.float32) / np.float32(half_n)))
    jn = lane % NSA_DIM
    inv_n = np.where(jn < ROPE_DIMS, inv_n.astype(np.float32)[jn % half_n], np.float32(0.0))[None, :]

    row = lambda w: pl.BlockSpec((tm, w), lambda i: (i, 0))
    full = lambda a: pl.BlockSpec(a.shape, lambda i: (0,) * a.ndim)
    grp = lambda w: pl.BlockSpec((NSA_KV_GROUPS, tm, w), lambda i: (0, i, 0))
    bf = lambda w: jax.ShapeDtypeStruct((n, w), BF16)
    gbf = jax.ShapeDtypeStruct((NSA_KV_GROUPS, n, NSA_DIM), BF16)
    inv_r = jnp.asarray(inv_r, F32)
    inv_n = jnp.asarray(inv_n, F32)
    return pl.pallas_call(
        _inproj_kernel,
        grid=(n // tm,),
        in_specs=[row(D_MODEL), row(1), full(w_ret), full(w_nq), full(w_kv), full(wg),
                  full(inv_r), full(inv_n)],
        out_specs=[row(RET_WIDTH)] * 4 + [row(NSA_WIDTH)] * 2 + [row(KV_WIDTH)] * 2
                  + [grp(NSA_DIM)] * 4
                  + [pl.BlockSpec((NSA_KV_GROUPS, GATE_LANES, tm), lambda i: (0, 0, i))],
        out_shape=[bf(RET_WIDTH)] * 4 + [bf(NSA_WIDTH)] * 2
                  + [jax.ShapeDtypeStruct((n, KV_WIDTH), F32)] * 2 + [gbf] * 4
                  + [jax.ShapeDtypeStruct((NSA_KV_GROUPS, GATE_LANES, n), F32)],
        compiler_params=_params(1),
        name="inproj",
    )(x2d, pos_col, w_ret, w_nq, w_kv, wg, inv_r, inv_n)


def _retention_kernel(q_ref, k_ref, v_ref, g_ref, o_ref, state_ref):
    c = RET_CHUNK

    @pl.when(pl.program_id(1) == 0)
    def _():
        state_ref[...] = jnp.zeros_like(state_ref)

    row = lax.broadcasted_iota(jnp.int32, (c, c), 0)
    col = lax.broadcasted_iota(jnp.int32, (c, c), 1)
    rel = (row - col).astype(F32)
    idx = lax.broadcasted_iota(jnp.int32, (c, 1), 0).astype(F32)
    for h in range(RET_HEADS):
        log_g = float(np.log(np.float32(1.0) - np.float32(2.0) ** np.float32(-5.0 - h)))
        cols = slice(h * RET_DIM, (h + 1) * RET_DIM)
        q = q_ref[:, cols]
        k = k_ref[:, cols]
        v = v_ref[:, cols]
        dmask = jnp.where(rel >= 0, jnp.exp(log_g * jnp.maximum(rel, 0.0)), 0.0)
        scores = _dot_nt(q, k) * dmask
        inner = _dot(scores.astype(BF16), v)
        zeta = jnp.exp(log_g * (c - 1.0 - idx))
        xi = jnp.exp(log_g * (idx + 1.0))
        prev = state_ref[h]
        cross = _dot(q, prev.astype(BF16)) * xi
        kz = (k.astype(F32) * zeta).astype(BF16)
        kv = lax.dot_general(kz, v, (((0,), (0,)), ((), ())), preferred_element_type=F32)
        state_ref[h] = prev * float(np.exp(np.float32(log_g) * np.float32(c))) + kv
        o = inner + cross
        mu = jnp.mean(o, axis=-1, keepdims=True)
        d = o - mu
        var = jnp.mean(d * d, axis=-1, keepdims=True)
        o = d * lax.rsqrt(var + LN_EPS)
        o_ref[:, cols] = (jax.nn.silu(g_ref[:, cols].astype(F32)) * o).astype(BF16)


def _retention(rq, rk, rv, rg, batch, seq):
    nc = seq // RET_CHUNK
    spec = pl.BlockSpec((RET_CHUNK, RET_WIDTH), lambda b, n: (b * nc + n, 0))
    return pl.pallas_call(
        _retention_kernel,
        grid=(batch, nc),
        in_specs=[spec] * 4,
        out_specs=spec,
        out_shape=jax.ShapeDtypeStruct(rq.shape, BF16),
        scratch_shapes=[pltpu.VMEM((RET_HEADS, RET_DIM, RET_DIM), F32)],
        compiler_params=_params(2),
        name="retention",
    )(rq, rk, rv, rg)


def _compress_kernel(a_ref, pe_ref, w1_ref, w2_ref, o_ref, shift_ref, *, n_cmp):
    rows = a_ref.shape[0]
    a = a_ref[...]
    lo = (a + pe_ref[0]).astype(BF16)
    hi = (a + pe_ref[1]).astype(BF16)
    ridx = lax.broadcasted_iota(jnp.int32, (rows, 1), 0)
    shift_ref[rows:rows + 8, :] = jnp.zeros((8, CMP_HIDDEN), F32)
    for g in range(NSA_KV_GROUPS):
        p = _dot(lo, w1_ref[0, g])
        shift_ref[0:rows, :] = _dot(hi, w1_ref[1, g])
        hid = jax.nn.silu(p + shift_ref[pl.ds(1, rows), :])
        out = _dot(hid.astype(BF16), w2_ref[...])
        o_ref[g] = jnp.where(ridx < n_cmp, out, 0.0).astype(BF16)


def _compress(a, pe, w1, w2, batch, seq):
    rows = seq // CMP_STRIDE
    per = CMP_STRIDE * KV_WIDTH
    n_cmp = (seq - CMP_LEN) // CMP_STRIDE + 1
    a2 = a.reshape(batch * rows, per)
    pe2 = jnp.tile(pe.reshape(2, CMP_STRIDE, 1, NSA_DIM), (1, 1, NSA_KV_GROUPS, 1)).reshape(2, 1, per)
    w1r = w1.reshape(2, CMP_STRIDE, 1, NSA_DIM, CMP_HIDDEN)
    eye = jnp.eye(NSA_KV_GROUPS, dtype=w1.dtype).reshape(1, NSA_KV_GROUPS, 1, NSA_KV_GROUPS, 1, 1)
    w1x = (w1r[:, None] * eye).reshape(2, NSA_KV_GROUPS, per, CMP_HIDDEN).astype(BF16)
    w2b = w2.astype(BF16)
    full = lambda arr: pl.BlockSpec(arr.shape, lambda b: (0,) * arr.ndim)
    return pl.pallas_call(
        functools.partial(_compress_kernel, n_cmp=n_cmp),
        grid=(batch,),
        in_specs=[pl.BlockSpec((rows, per), lambda b: (b, 0)), full(pe2), full(w1x), full(w2b)],
        out_specs=pl.BlockSpec((None, NSA_KV_GROUPS, rows, NSA_DIM), lambda b: (b, 0, 0, 0)),
        out_shape=jax.ShapeDtypeStruct((batch, NSA_KV_GROUPS, rows, NSA_DIM), BF16),
        scratch_shapes=[pltpu.VMEM((rows + 8, CMP_HIDDEN), F32)],
        compiler_params=_params(1),
        name="compress",
    )(a2, pe2, w1x, w2b)


def _heads_to_lanes(ref):
    vt = ref[...].astype(F32).T
    return jnp.concatenate([vt[h * NSA_DIM:(h + 1) * NSA_DIM] for h in range(NSA_HPG)], axis=1).astype(BF16)


def _tile_heads(v):
    return jnp.concatenate([v] * NSA_HPG, axis=1)


def _transpose_into(dst_ref, src_ref, chunk):
    def step(c, _):
        c0 = pl.multiple_of(c * chunk, chunk)
        dst_ref[:NSA_DIM, pl.ds(c0, chunk)] = src_ref[pl.ds(c0, chunk), :].astype(F32).T.astype(BF16)
        return 0
    lax.fori_loop(0, src_ref.shape[0] // chunk, step, 0)


def _nsa_kernel(qraw_ref, qrot_ref, gate_ref, kcmp_ref, vcmp_ref, ovt_ref,
                ks_ref, vs_ref, kw_ref, vw_ref, o_ref, vst_ref, vwt_ref, vct_ref, bias_ref, *, tq, tk, seq):
    i = pl.program_id(2)
    t0 = i * tq
    cols = NSA_HPG * tq
    n_sel = seq // SEL_LEN
    n_cmp_rows = seq // CMP_STRIDE

    @pl.when(i == 0)
    def _():
        chunk = min(512, n_cmp_rows)
        _transpose_into(vst_ref, vs_ref, chunk)
        _transpose_into(vwt_ref, vw_ref, chunk)
        _transpose_into(vct_ref, vcmp_ref, chunk)
        vst_ref[NSA_DIM:, :] = jnp.ones((SUM_ROWS, seq), BF16)
        vwt_ref[NSA_DIM:, :] = jnp.ones((SUM_ROWS, seq), BF16)

    def split_sum(acc):
        return acc[:NSA_DIM] / acc[NSA_DIM:NSA_DIM + 1]

    q_raw = _heads_to_lanes(qraw_ref)
    q_rot = _heads_to_lanes(qrot_ref)
    t_row = t0 + lax.broadcasted_iota(jnp.int32, (1, tq), 1)

    chain_w = cols // NSA_CHAINS
    heads_per_chain = chain_w // tq
    chains = [slice(c * chain_w, (c + 1) * chain_w) for c in range(NSA_CHAINS)]
    tile_chain = lambda v: jnp.concatenate([v] * heads_per_chain, axis=1)

    span = WIN + tq
    ws = pl.multiple_of(jnp.maximum(t0 - WIN, 0), tq)
    dist = t_row - (ws + lax.broadcasted_iota(jnp.int32, (span, 1), 0))
    bias_w = tile_chain(jnp.where((dist >= 0) & (dist < WIN), 0.0, NEG))
    k_w = kw_ref[pl.ds(ws, span), :]
    v_w = vwt_ref[:, pl.ds(ws, span)]
    o_w = []
    for c in chains:
        s_w = _dot(k_w, q_rot[:, c]) + bias_w
        p_w = jnp.exp((s_w - jnp.max(s_w, axis=0, keepdims=True)).astype(BF16))
        o_w.append(split_sum(_dot(v_w, p_w)))
    o_w = jnp.concatenate(o_w, axis=1)

    c_idx = lax.broadcasted_iota(jnp.int32, (n_cmp_rows, 1), 0)
    valid = tile_chain(jnp.where(c_idx * CMP_STRIDE + (CMP_LEN - 1) <= t_row, 1.0, 0.0))
    bias_c = (valid - 1.0) * (-NEG)
    o_c = []
    p_sum = None
    for c in chains:
        s_c = _dot(kcmp_ref[...], q_raw[:, c]) + bias_c
        e_c = jnp.exp(s_c - jnp.max(s_c, axis=0, keepdims=True)) * valid
        l_c = jnp.sum(e_c, axis=0, keepdims=True)
        p_c = e_c / jnp.where(l_c > 0.0, l_c, 1.0)
        o_c.append(_dot(vct_ref[...], p_c.astype(BF16)))
        for h in range(heads_per_chain):
            p_h = p_c[:, h * tq:(h + 1) * tq]
            p_sum = p_h if p_sum is None else p_sum + p_h
    o_c = jnp.concatenate(o_c, axis=1)

    p_hi = p_sum.astype(BF16)
    p_lo = (p_sum - p_hi.astype(F32)).astype(BF16)
    ovt = ovt_ref[...]
    imp = _dot(ovt, p_hi) + _dot(ovt, p_lo)
    jb = lax.broadcasted_iota(jnp.int32, (n_sel, tq), 0)
    cur = (t0 + lax.broadcasted_iota(jnp.int32, (n_sel, tq), 1)) >> SEL_SHIFT
    forced = (jb == 0) | (jb == cur) | (jb == cur - 1)
    work = jnp.where(forced, FORCE, imp)
    work = jnp.where(jb <= cur, work, NEG)
    sel_t = jnp.zeros((n_sel, tq), F32)
    for _ in range(min(SEL_TOPK, n_sel)):
        best = jnp.max(work, axis=0, keepdims=True)
        first = jnp.min(jnp.where(work == best, jb, n_sel), axis=0, keepdims=True)
        hit = jb == first
        sel_t = jnp.where(hit, 1.0, sel_t)
        work = jnp.where(hit, -jnp.inf, work)
    bias_ref[...] = jnp.where(sel_t > 0.5, 0.0, NEG)

    blocks_per_tile = tk // SEL_LEN

    def sel_tile(kt, carry, causal):
        k0 = pl.multiple_of(kt * tk, tk)
        bias = jnp.concatenate(
            [jnp.broadcast_to(bias_ref[pl.ds(kt * blocks_per_tile + j, 1), :], (SEL_LEN, tq))
             for j in range(blocks_per_tile)], axis=0)
        if causal:
            kpos = k0 + lax.broadcasted_iota(jnp.int32, (tk, 1), 0)
            bias = jnp.where(kpos <= t_row, bias, NEG)
        bias = tile_chain(bias)
        k_t = ks_ref[pl.ds(k0, tk), :]
        v_t = vst_ref[:, pl.ds(k0, tk)]
        out = []
        scores = [_dot(k_t, q_rot[:, c]) + bias for c in chains]
        for (m, acc), s in zip(carry, scores):
            m_new = jnp.maximum(m, jnp.max(s, axis=0, keepdims=True))
            p = jnp.exp((s - m_new).astype(BF16))
            acc = jnp.exp(m - m_new) * acc + _dot(v_t, p)
            out.append((m_new, acc))
        return tuple(out)

    n_full = t0 // tk
    init = tuple((jnp.full((1, chain_w), NEG, F32), jnp.zeros((NSA_DIM + SUM_ROWS, chain_w), F32))
                 for _ in chains)
    carry = lax.fori_loop(0, n_full, functools.partial(sel_tile, causal=False), init)
    carry = sel_tile(n_full, carry, causal=True)
    o_s = jnp.concatenate([split_sum(acc) for _, acc in carry], axis=1)

    gt = gate_ref[...]
    outs = []
    for h in range(NSA_HPG):
        c = slice(h * tq, (h + 1) * tq)
        outs.append(gt[3 * h:3 * h + 1] * o_c[:, c] + gt[3 * h + 1:3 * h + 2] * o_s[:, c]
                    + gt[3 * h + 2:3 * h + 3] * o_w[:, c])
    o_ref[...] = jnp.concatenate(outs, axis=0).T.astype(BF16)


def _nsa(nq, nqr, gates, kcmp, vcmp, ks, vs, kw, vw, batch, seq):
    n = batch * seq
    tq = 512
    tk = 512 if seq % 512 == 0 else seq
    nqb = seq // tq
    n_sel = seq // SEL_LEN
    rows_c = seq // CMP_STRIDE
    gw = NSA_HPG * NSA_DIM
    cs = np.arange(rows_c)[None, :] * CMP_STRIDE
    ss = np.arange(n_sel)[:, None] * SEL_LEN
    n_cmp = (seq - CMP_LEN) // CMP_STRIDE + 1
    ovt = ((cs < ss + SEL_LEN) & (cs + CMP_LEN > ss) & (np.arange(rows_c)[None, :] < n_cmp))
    ovt = jnp.asarray(ovt.astype(np.float32), BF16)

    qspec = pl.BlockSpec((tq, gw), lambda b, g, i: (b * nqb + i, g))
    cspec = pl.BlockSpec((None, None, rows_c, NSA_DIM), lambda b, g, i: (b, g, 0, 0))
    kspec = pl.BlockSpec((None, seq, NSA_DIM), lambda b, g, i: (g, b, 0))
    return pl.pallas_call(
        functools.partial(_nsa_kernel, tq=tq, tk=tk, seq=seq),
        grid=(batch, NSA_KV_GROUPS, nqb),
        in_specs=[qspec, qspec,
                  pl.BlockSpec((None, GATE_LANES, tq), lambda b, g, i: (g, 0, b * nqb + i)),
                  cspec, cspec, pl.BlockSpec(ovt.shape, lambda b, g, i: (0, 0)),
                  kspec, kspec, kspec, kspec],
        out_specs=qspec,
        out_shape=jax.ShapeDtypeStruct((n, NSA_WIDTH), BF16),
        scratch_shapes=[pltpu.VMEM((NSA_DIM + SUM_ROWS, seq), BF16), pltpu.VMEM((NSA_DIM + SUM_ROWS, seq), BF16),
                        pltpu.VMEM((NSA_DIM, rows_c), BF16), pltpu.VMEM((n_sel, tq), F32)],
        compiler_params=_params(3),
        name="nsa",
    )(nq, nqr, gates, kcmp, vcmp, ovt, ks, vs, kw, vw)


def _memkv_kernel(mem_ref, w_ref, kv_ref):
    kv_ref[...] = _dot(mem_ref[...].astype(BF16), w_ref[...]).astype(BF16)


def _memkv(mem2d, w_xkv):
    n = mem2d.shape[0]
    w = w_xkv.astype(BF16)
    return pl.pallas_call(
        _memkv_kernel,
        grid=(n // MEM_LEN,),
        in_specs=[pl.BlockSpec((MEM_LEN, D_MODEL), lambda i: (i, 0)),
                  pl.BlockSpec(w.shape, lambda i: (0, 0))],
        out_specs=pl.BlockSpec((MEM_LEN, 2 * D_MODEL), lambda i: (i, 0)),
        out_shape=jax.ShapeDtypeStruct((n, 2 * D_MODEL), BF16),
        compiler_params=_params(1),
        name="memkv",
    )(mem2d, w)


ROW_TILES = D_MODEL // 2 // LANES


def _store_packed_rows(ref, v):
    half = D_MODEL // 2
    rows = v.shape[0]
    hi = pltpu.bitcast(v[:, :half].astype(BF16).astype(F32), jnp.uint32)
    lo = pltpu.bitcast(v[:, half:].astype(BF16).astype(F32), jnp.uint32)
    packed = hi | (lo >> 16)
    for s in range(ROW_TILES):
        ref[pl.ds(s, rows, stride=ROW_TILES), :] = packed[:, s * LANES:(s + 1) * LANES]


def _load_packed_halves(ref):
    rows = ref.shape[0] // ROW_TILES
    words = [ref[pl.ds(s, rows, stride=ROW_TILES), :] for s in range(ROW_TILES)]
    hi = jnp.concatenate([pltpu.bitcast(w & jnp.uint32(0xFFFF0000), F32) for w in words], axis=1)
    lo = jnp.concatenate([pltpu.bitcast(w << 16, F32) for w in words], axis=1)
    return hi, lo


def _postmix_kernel(x_ref, oret_ref, onsa_ref, kv_ref, wout_ref, wq_ref, wo_ref,
                    g1_ref, b1_ref, g2_ref, b2_ref, x2_ref, x2p_ref):
    mixed = jnp.concatenate([oret_ref[...], onsa_ref[...]], axis=1)
    x1 = _layer_norm(DN_ALPHA * x_ref[...] + _dot(mixed, wout_ref[...]), g1_ref[...], b1_ref[...])
    q = (_dot(x1.astype(BF16), wq_ref[...]) * (XATT_DIM ** -0.5)).astype(BF16)
    heads = []
    for h in range(XATT_HEADS):
        cols = slice(h * XATT_DIM, (h + 1) * XATT_DIM)
        s = _dot_nt(q[:, cols], kv_ref[:, cols])
        m = jnp.max(s, axis=-1, keepdims=True)
        p = jnp.exp(s - m)
        l = jnp.sum(p, axis=-1, keepdims=True)
        heads.append(_dot(p.astype(BF16), kv_ref[:, D_MODEL + h * XATT_DIM:D_MODEL + (h + 1) * XATT_DIM]) / l)
    att = jnp.concatenate(heads, axis=1).astype(BF16)
    x2 = _layer_norm(DN_ALPHA * x1 + _dot(att, wo_ref[...]), g2_ref[...], b2_ref[...])
    x2_ref[...] = x2
    _store_packed_rows(x2p_ref, x2)


def _postmix(x2d, o_ret, o_nsa, kvx, w_out, w_xq, w_xo, ln1_g, ln1_b, ln2_g, ln2_b, batch, seq):
    n = x2d.shape[0]
    tm = 256 if seq % 256 == 0 else seq
    per_b = seq // tm
    row = lambda w: pl.BlockSpec((tm, w), lambda b, i: (b * per_b + i, 0))
    full = lambda a: pl.BlockSpec(a.shape, lambda b, i: (0,) * a.ndim)
    ws = [w_out.astype(BF16), w_xq.astype(BF16), w_xo.astype(BF16)]
    vecs = [v.reshape(1, D_MODEL) for v in (ln1_g, ln1_b, ln2_g, ln2_b)]
    return pl.pallas_call(
        _postmix_kernel,
        grid=(batch, per_b),
        in_specs=[row(D_MODEL), row(RET_WIDTH), row(NSA_WIDTH),
                  pl.BlockSpec((MEM_LEN, 2 * D_MODEL), lambda b, i: (b, 0))]
                 + [full(w) for w in ws] + [full(v) for v in vecs],
        out_specs=[row(D_MODEL),
                   pl.BlockSpec((tm * ROW_TILES, LANES), lambda b, i: (b * per_b + i, 0))],
        out_shape=[jax.ShapeDtypeStruct((n, D_MODEL), F32),
                   jax.ShapeDtypeStruct((n * ROW_TILES, LANES), jnp.uint32)],
        compiler_params=_params(2),
        name="postmix",
    )(x2d, o_ret, o_nsa, kvx, *ws, *vecs)


def _router_kernel(x_ref, wr_ref, bias_ref, e_ref, rank_ref, w_ref, cnt_ref, cntrow_ref, carry_ref, carryrow_ref):
    tn = x_ref.shape[0]
    per = N_EXPERTS // N_GROUPS

    @pl.when(pl.program_id(0) == 0)
    def _():
        carry_ref[...] = jnp.zeros_like(carry_ref)
        carryrow_ref[...] = jnp.zeros_like(carryrow_ref)

    logits = _dot_nt(wr_ref[...], x_ref[...].astype(BF16))
    scores = jax.nn.sigmoid(logits)
    biased = scores + bias_ref[...]
    b3 = biased.reshape(N_GROUPS, per, tn)
    member = lax.broadcasted_iota(jnp.int32, (N_GROUPS, per, tn), 1)
    top1 = jnp.max(b3, axis=1, keepdims=True)
    first1 = jnp.min(jnp.where(b3 == top1, member, per), axis=1, keepdims=True)
    top2 = jnp.max(jnp.where(member == first1, -jnp.inf, b3), axis=1, keepdims=True)
    gscore = top1 + top2
    gidx = lax.broadcasted_iota(jnp.int32, (N_GROUPS, 1, tn), 0)
    gwork = gscore
    for _ in range(TOPK_GROUPS - 1):
        gbest = jnp.max(gwork, axis=0, keepdims=True)
        gfirst = jnp.min(jnp.where(gwork == gbest, gidx, N_GROUPS), axis=0, keepdims=True)
        gwork = jnp.where(gidx == gfirst, -jnp.inf, gwork)
    kth = jnp.max(gwork, axis=0, keepdims=True)
    work = jnp.where(gscore >= kth, b3, NEG).reshape(N_EXPERTS, tn)
    eidx = lax.broadcasted_iota(jnp.int32, (N_EXPERTS, tn), 0)
    picks = []
    chosen = jnp.zeros((N_EXPERTS, tn), F32)
    for _ in range(TOP_K):
        best = jnp.max(work, axis=0, keepdims=True)
        first = jnp.min(jnp.where(work == best, eidx, N_EXPERTS), axis=0, keepdims=True)
        hit = eidx == first
        picks.append((first, hit))
        chosen = jnp.where(hit, 1.0, chosen)
        work = jnp.where(hit, -jnp.inf, work)

    r_i = lax.broadcasted_iota(jnp.int32, (tn, tn), 0)
    c_i = lax.broadcasted_iota(jnp.int32, (tn, tn), 1)
    before = jnp.where(r_i < c_i, 1.0, 0.0).astype(BF16)
    chosen_b = chosen.astype(BF16)
    rank = _dot(chosen_b, before) + carry_ref[...]
    carry_ref[...] = carry_ref[...] + jnp.sum(chosen, axis=1, keepdims=True)
    carryrow_ref[...] = carryrow_ref[...] + _dot_nt(jnp.ones((8, tn), BF16), chosen_b)
    cnt_ref[...] = carry_ref[...]
    cntrow_ref[...] = carryrow_ref[...]

    wsel = [jnp.sum(jnp.where(hit, scores, 0.0), axis=0, keepdims=True) for _, hit in picks]
    wsum = wsel[0]
    for v in wsel[1:]:
        wsum = wsum + v
    for kk, (first, hit) in enumerate(picks):
        e_ref[kk:kk + 1, :] = first
        rank_ref[kk:kk + 1, :] = jnp.sum(jnp.where(hit, rank, 0.0), axis=0, keepdims=True).astype(jnp.int32)
        w_ref[kk:kk + 1, :] = wsel[kk] / wsum * ROUTED_SCALE


def _router(x2, w_router, router_bias):
    n = x2.shape[0]
    tn = 512 if n % 512 == 0 else n
    wr_t = w_router.T.astype(BF16)
    bias = router_bias.reshape(N_EXPERTS, 1).astype(F32)
    kspec = pl.BlockSpec((TOP_K, tn), lambda i: (0, i))
    return pl.pallas_call(
        _router_kernel,
        grid=(n // tn,),
        in_specs=[pl.BlockSpec((tn, D_MODEL), lambda i: (i, 0)),
                  pl.BlockSpec(wr_t.shape, lambda i: (0, 0)),
                  pl.BlockSpec(bias.shape, lambda i: (0, 0))],
        out_specs=[kspec, kspec, kspec, pl.BlockSpec((N_EXPERTS, 1), lambda i: (0, 0)),
                   pl.BlockSpec((8, N_EXPERTS), lambda i: (0, 0))],
        out_shape=[jax.ShapeDtypeStruct((TOP_K, n), jnp.int32),
                   jax.ShapeDtypeStruct((TOP_K, n), jnp.int32),
                   jax.ShapeDtypeStruct((TOP_K, n), F32),
                   jax.ShapeDtypeStruct((N_EXPERTS, 1), F32),
                   jax.ShapeDtypeStruct((8, N_EXPERTS), F32)],
        scratch_shapes=[pltpu.VMEM((N_EXPERTS, 1), F32), pltpu.VMEM((8, N_EXPERTS), F32)],
        compiler_params=_params(1),
        name="router",
    )(x2, wr_t, bias)


def _slots_kernel(e_ref, rank_ref, cnt_ref, cntrow_ref, dest_ref, blk_e_ref, nblk_ref, *, blk, n_blocks):
    pad = lambda c: jnp.ceil(c / blk) * blk
    padded = pad(cnt_ref[...])
    padded_row = pad(cntrow_ref[0:1, :])
    r_i = lax.broadcasted_iota(jnp.int32, (N_EXPERTS, N_EXPERTS), 0)
    c_i = lax.broadcasted_iota(jnp.int32, (N_EXPERTS, N_EXPERTS), 1)
    start = jnp.sum(jnp.where(c_i < r_i, padded_row, 0.0), axis=1, keepdims=True)
    end = start + padded
    e = e_ref[...]
    dest = rank_ref[...]
    for ex in range(N_EXPERTS):
        dest = dest + jnp.where(e == ex, start[ex:ex + 1, :].astype(jnp.int32), 0)
    dest_ref[...] = dest
    bstart = (lax.broadcasted_iota(jnp.int32, (1, n_blocks), 1) * blk).astype(F32)
    owner = jnp.sum(jnp.where(end <= bstart, 1.0, 0.0), axis=0, keepdims=True)
    blk_e_ref[...] = jnp.minimum(owner, N_EXPERTS - 1.0).astype(jnp.int32)
    nblk_ref[...] = (end[N_EXPERTS - 1:N_EXPERTS, :] / blk).astype(jnp.int32)


def _slots(e_k, rank_k, counts, counts_row, blk, n_blocks):
    n = e_k.shape[1]
    full = lambda shape: pl.BlockSpec(shape, lambda: (0,) * len(shape))
    return pl.pallas_call(
        functools.partial(_slots_kernel, blk=blk, n_blocks=n_blocks),
        in_specs=[full((TOP_K, n)), full((TOP_K, n)), full((N_EXPERTS, 1)), full((8, N_EXPERTS))],
        out_specs=[full((TOP_K, n)), full((1, n_blocks)), full((1, 1))],
        out_shape=[jax.ShapeDtypeStruct((TOP_K, n), jnp.int32),
                   jax.ShapeDtypeStruct((1, n_blocks), jnp.int32),
                   jax.ShapeDtypeStruct((1, 1), jnp.int32)],
        compiler_params=pltpu.CompilerParams(vmem_limit_bytes=VMEM_LIMIT),
        name="slots",
    )(e_k, rank_k, counts, counts_row)


def _row_copy(src_ref, src_row, dst_ref, dst_row, sem):
    span = lambda r: pl.ds(pl.multiple_of(r * ROW_TILES, ROW_TILES), ROW_TILES)
    return pltpu.make_async_copy(src_ref.at[span(src_row)], dst_ref.at[span(dst_row)], sem)


def _dest_spec(tt):
    return pl.BlockSpec((None, 1, TOP_K * tt), lambda i: (i, 0, 0), memory_space=pltpu.SMEM)


def _dispatch_kernel(dest_ref, x_ref, zero_ref, xs_ref, sem, *, tt):
    del zero_ref

    def issue(j, _):
        for kk in range(TOP_K):
            _row_copy(x_ref, j, xs_ref, dest_ref[0, kk * tt + j], sem).start(priority=kk % 2)
        return 0

    lax.fori_loop(0, tt, issue, 0)

    def drain(j, _):
        for kk in range(TOP_K):
            _row_copy(x_ref, 0, xs_ref, 0, sem).wait()
        return 0

    lax.fori_loop(0, tt, drain, 0)


def _dispatch(dest_tiles, x2p, cap, tt):
    n = x2p.shape[0] // ROW_TILES
    zeros = jnp.zeros((cap * ROW_TILES, LANES), x2p.dtype)
    return pl.pallas_call(
        functools.partial(_dispatch_kernel, tt=tt),
        grid=(n // tt,),
        in_specs=[_dest_spec(tt),
                  pl.BlockSpec((tt * ROW_TILES, LANES), lambda i: (i, 0)),
                  pl.BlockSpec(memory_space=pl.ANY)],
        out_specs=pl.BlockSpec(memory_space=pl.ANY),
        scratch_shapes=[pltpu.SemaphoreType.DMA(())],
        out_shape=jax.ShapeDtypeStruct(zeros.shape, x2p.dtype),
        input_output_aliases={2: 0},
        compiler_params=_params(1),
        name="dispatch",
    )(dest_tiles, x2p, zeros)


def _experts_kernel(blk_e_ref, nblk_ref, xs_ref, wg_ref, wu_ref, wd_ref, y_ref):
    del blk_e_ref

    @pl.when(pl.program_id(0) < nblk_ref[0])
    def _():
        half = D_MODEL // 2
        hi, lo = (v.astype(BF16) for v in _load_packed_halves(xs_ref))
        gate = _dot(hi, wg_ref[:half, :]) + _dot(lo, wg_ref[half:, :])
        up = _dot(hi, wu_ref[:half, :]) + _dot(lo, wu_ref[half:, :])
        _store_packed_rows(y_ref, _dot((jax.nn.silu(gate) * up).astype(BF16), wd_ref[...]))

    @pl.when(pl.program_id(0) >= nblk_ref[0])
    def _():
        y_ref[...] = jnp.zeros_like(y_ref)


def _experts(blk_e, nblk, xs, w_gate, w_up, w_down, blk):
    cap = xs.shape[0] // ROW_TILES
    wg, wu, wd = (a.astype(BF16) for a in (w_gate, w_up, w_down))
    wspec = lambda a: pl.BlockSpec((None,) + a.shape[1:], lambda i, be, nb: (be[i], 0, 0))
    rows = pl.BlockSpec((blk * ROW_TILES, LANES), lambda i, be, nb: (i, 0))
    return pl.pallas_call(
        _experts_kernel,
        grid_spec=pltpu.PrefetchScalarGridSpec(
            num_scalar_prefetch=2,
            grid=(cap // blk,),
            in_specs=[rows, wspec(wg), wspec(wu), wspec(wd)],
            out_specs=rows,
        ),
        out_shape=jax.ShapeDtypeStruct(xs.shape, xs.dtype),
        compiler_params=_params(1),
        name="experts",
    )(blk_e, nblk, xs, wg, wu, wd)


def _combine_kernel(dest_ref, dnext_ref, x_ref, wk_ref, y_ref, wsg_ref, wsu_ref, wsd_ref, g_ref, b_ref,
                    o_ref, buf_ref, sem, *, tt):
    i = pl.program_id(0)
    slot = i % 2

    def gather(idx_ref, to_slot):
        def issue(j, _):
            for kk in range(TOP_K):
                _row_copy(y_ref, idx_ref[0, kk * tt + j], buf_ref.at[to_slot, kk], j,
                          sem.at[to_slot]).start(priority=kk % 2)
            return 0
        lax.fori_loop(0, tt, issue, 0)

    @pl.when(i == 0)
    def _():
        gather(dest_ref, slot)

    @pl.when(i + 1 < pl.num_programs(0))
    def _():
        gather(dnext_ref, 1 - slot)

    x = x_ref[...]
    xb = x.astype(BF16)
    shared = _dot((jax.nn.silu(_dot(xb, wsg_ref[...])) * _dot(xb, wsu_ref[...])).astype(BF16), wsd_ref[...])

    def drain(j, _):
        for kk in range(TOP_K):
            _row_copy(y_ref, 0, buf_ref.at[slot, kk], 0, sem.at[slot]).wait()
        return 0

    lax.fori_loop(0, tt, drain, 0)

    wk = wk_ref[...]
    routed_hi = routed_lo = None
    for kk in range(TOP_K):
        hi, lo = _load_packed_halves(buf_ref.at[slot, kk])
        w = wk[:, kk:kk + 1]
        routed_hi = hi * w if kk == 0 else routed_hi + hi * w
        routed_lo = lo * w if kk == 0 else routed_lo + lo * w
    routed = jnp.concatenate([routed_hi, routed_lo], axis=1)
    o_ref[...] = _layer_norm(DN_ALPHA * x + (routed + shared), g_ref[...], b_ref[...])


def _combine(dest_tiles, x2, w_tok, y, ws_gate, ws_up, ws_down, ln3_g, ln3_b, tt):
    n = x2.shape[0]
    steps = n // tt
    ws = [ws_gate.astype(BF16), ws_up.astype(BF16), ws_down.astype(BF16)]
    vecs = [ln3_g.reshape(1, D_MODEL), ln3_b.reshape(1, D_MODEL)]
    full = lambda a: pl.BlockSpec(a.shape, lambda i: (0,) * a.ndim)
    next_spec = pl.BlockSpec((None, 1, TOP_K * tt), lambda i: (jnp.minimum(i + 1, steps - 1), 0, 0),
                             memory_space=pltpu.SMEM)
    return pl.pallas_call(
        functools.partial(_combine_kernel, tt=tt),
        grid=(steps,),
        in_specs=[_dest_spec(tt), next_spec,
                  pl.BlockSpec((tt, D_MODEL), lambda i: (i, 0)),
                  pl.BlockSpec((tt, TOP_K), lambda i: (i, 0)),
                  pl.BlockSpec(memory_space=pl.ANY)]
                 + [full(a) for a in ws] + [full(v) for v in vecs],
        out_specs=pl.BlockSpec((tt, D_MODEL), lambda i: (i, 0)),
        scratch_shapes=[pltpu.VMEM((2, TOP_K, tt * ROW_TILES, LANES), jnp.uint32),
                        pltpu.SemaphoreType.DMA((2,))],
        out_shape=jax.ShapeDtypeStruct((n, D_MODEL), F32),
        compiler_params=_params(1),
        name="combine",
    )(dest_tiles, dest_tiles, x2, w_tok, y, *ws, *vecs)


def _moe_and_norm(x2, x2p, w_router, router_bias, w_gate, w_up, w_down, ws_gate, ws_up, ws_down,
                  ln3_g, ln3_b):
    n = x2.shape[0]
    blk = 512
    tt = 128 if n % 128 == 0 else n
    cap = n * TOP_K + N_EXPERTS * blk
    n_blocks = cap // blk
    e_k, rank_k, w_k, counts, counts_row = _router(x2, w_router, router_bias)
    dest, blk_e, nblk = _slots(e_k, rank_k, counts, counts_row, blk, n_blocks)
    dest_tiles = dest.reshape(TOP_K, n // tt, tt).transpose(1, 0, 2).reshape(n // tt, 1, TOP_K * tt)
    xs = _dispatch(dest_tiles, x2p, cap, tt)
    y = _experts(blk_e.reshape(-1), nblk.reshape(-1), xs, w_gate, w_up, w_down, blk)
    return _combine(dest_tiles, x2, w_k.T, y, ws_gate, ws_up, ws_down, ln3_g, ln3_b, tt)


def _layer(x, mem, positions, w_in, cmp_pe_k, cmp_pe_v, cmp_w1_k, cmp_w2_k, cmp_w1_v, cmp_w2_v,
           w_out, ln1_g, ln1_b, w_xq, w_xkv, w_xo, ln2_g, ln2_b, w_router, router_bias,
           w_gate, w_up, w_down, ws_gate, ws_up, ws_down, ln3_g, ln3_b):
    batch, seq, _ = x.shape
    n = batch * seq
    x2d = x.reshape(n, D_MODEL)
    pos_col = positions.astype(F32).reshape(n, 1)
    (rq, rk, rv, rg, nq, nqr, kc, vc, ks, vs, kw, vw, gates) = _inproj(x2d, pos_col, w_in)
    o_ret = _retention(rq, rk, rv, rg, batch, seq)
    kcmp = _compress(kc, cmp_pe_k, cmp_w1_k, cmp_w2_k, batch, seq)
    vcmp = _compress(vc, cmp_pe_v, cmp_w1_v, cmp_w2_v, batch, seq)
    o_nsa = _nsa(nq, nqr, gates, kcmp, vcmp, ks, vs, kw, vw, batch, seq)
    kvx = _memkv(mem.reshape(batch * MEM_LEN, D_MODEL), w_xkv)
    x2, x2p = _postmix(x2d, o_ret, o_nsa, kvx, w_out, w_xq, w_xo, ln1_g, ln1_b, ln2_g, ln2_b, batch, seq)
    out = _moe_and_norm(x2, x2p, w_router, router_bias, w_gate, w_up, w_down,
                        ws_gate, ws_up, ws_down, ln3_g, ln3_b)
    return out.reshape(batch, seq, D_MODEL)


def kernel(x, mem, positions, w_in, cmp_pe_k, cmp_pe_v, cmp_w1_k, cmp_w2_k, cmp_w1_v, cmp_w2_v, w_out, ln1_g, ln1_b, w_xq, w_xkv, w_xo, ln2_g, ln2_b, w_router, router_bias, w_gate, w_up, w_down, ws_gate, ws_up, ws_down, ln3_g, ln3_b):
    for l in range(DEPTH):
        x = _layer(x, mem, positions, w_in[l], cmp_pe_k[l], cmp_pe_v[l], cmp_w1_k[l], cmp_w2_k[l],
                   cmp_w1_v[l], cmp_w2_v[l], w_out[l], ln1_g[l], ln1_b[l], w_xq[l], w_xkv[l],
                   w_xo[l], ln2_g[l], ln2_b[l], w_router[l], router_bias[l], w_gate[l], w_up[l],
                   w_down[l], ws_gate[l], ws_up[l], ws_down[l], ln3_g[l], ln3_b[l])
    return x
```

```python
import functools

import numpy as np
import jax
import jax.numpy as jnp
from jax import lax
from jax.experimental import pallas as pl
from jax.experimental.pallas import tpu as pltpu
from jax.experimental.pallas import tpu_sc as plsc

D_MODEL = 1024
MEM_LEN = 256
DEPTH = 1
DN_ALPHA = (2 * DEPTH) ** 0.25
LN_EPS = 1e-5
NEG = -1e30
FORCE = 1e9

RET_HEADS = 4
RET_DIM = 128
RET_CHUNK = 128
RET_ROPE_BASE = 10000.0
RET_WIDTH = RET_HEADS * RET_DIM

NSA_HEADS = 8
NSA_KV_GROUPS = 2
NSA_HPG = NSA_HEADS // NSA_KV_GROUPS
NSA_DIM = 64
NSA_WIDTH = NSA_HEADS * NSA_DIM
KV_WIDTH = NSA_KV_GROUPS * NSA_DIM
CMP_LEN = 32
CMP_STRIDE = 16
CMP_HIDDEN = 256
SEL_LEN = 64
SEL_SHIFT = 6
SEL_TOPK = 16
WIN = 512
ROPE_THETA = 500000.0
ROPE_DIMS = NSA_DIM // 4
GATE_LANES = 16
NSA_CHAINS = 2
SUM_ROWS = 16

SC_CORES = 2
SC_SUBCORES = 16
SC_CHUNK = 64

XATT_HEADS = 4
XATT_DIM = D_MODEL // XATT_HEADS

N_EXPERTS = 64
TOP_K = 8
N_GROUPS = 8
TOPK_GROUPS = 4
EXPERT_FF = 256
SHARED_FF = 256
ROUTED_SCALE = 2.5

LANES = 128
VMEM_LIMIT = 56 * 1024 * 1024

F32 = jnp.float32
BF16 = jnp.bfloat16
NT_DIMS = (((1,), (1,)), ((), ()))


def _params(n_axes):
    return pltpu.CompilerParams(dimension_semantics=("arbitrary",) * n_axes,
                                vmem_limit_bytes=VMEM_LIMIT)


def _dot(a, b):
    return jnp.dot(a, b, preferred_element_type=F32)


def _dot_nt(a, b):
    return lax.dot_general(a, b, NT_DIMS, preferred_element_type=F32)


def _layer_norm(v, g, b):
    mu = jnp.mean(v, axis=-1, keepdims=True)
    d = v - mu
    var = jnp.mean(d * d, axis=-1, keepdims=True)
    return d * lax.rsqrt(var + LN_EPS) * g + b


def _inproj_kernel(x_ref, pos_ref, wret_ref, wnq_ref, wkv_ref, wg_ref, invr_ref, invn_ref,
                   rq_ref, rk_ref, rv_ref, rg_ref, nq_ref, nqr_ref, kc_ref, vc_ref,
                   ks_ref, vs_ref, kw_ref, vw_ref, gate_ref):
    xb = x_ref[...].astype(BF16)
    pos = pos_ref[...]
    lane = lax.broadcasted_iota(jnp.int32, (1, LANES), 1)

    ang = pos * invr_ref[...]
    cos_r = jnp.cos(ang)
    sin_r = jnp.sin(ang)
    sin_r = jnp.where(lane < RET_DIM // 2, -sin_r, sin_r)
    q_all = _dot(xb, wret_ref[:, :RET_WIDTH])
    k_all = _dot(xb, wret_ref[:, RET_WIDTH:2 * RET_WIDTH])
    for h in range(RET_HEADS):
        cols = slice(h * RET_DIM, (h + 1) * RET_DIM)
        q = q_all[:, cols]
        rq_ref[:, cols] = (q * cos_r + pltpu.roll(q, RET_DIM // 2, 1) * sin_r).astype(BF16)
        k = k_all[:, cols]
        k = (k * cos_r + pltpu.roll(k, RET_DIM // 2, 1) * sin_r) * (RET_DIM ** -0.5)
        rk_ref[:, cols] = k.astype(BF16)
    rv_ref[...] = _dot(xb, wret_ref[:, 2 * RET_WIDTH:3 * RET_WIDTH]).astype(BF16)
    rg_ref[...] = _dot(xb, wret_ref[:, 3 * RET_WIDTH:4 * RET_WIDTH]).astype(BF16)

    half = ROPE_DIMS // 2
    j = lane % NSA_DIM
    angn = pos * invn_ref[...]
    cos_n = jnp.cos(angn)
    sin_n = jnp.sin(angn)
    sin_lo = jnp.where(j < half, -sin_n, 0.0)
    sin_hi = jnp.where((j >= half) & (j < 2 * half), sin_n, 0.0)

    def rope_n(v):
        return v * cos_n + pltpu.roll(v, half, 1) * sin_hi + pltpu.roll(v, LANES - half, 1) * sin_lo

    scale = NSA_DIM ** -0.5
    nq_all = _dot(xb, wnq_ref[...])
    for c in range(NSA_WIDTH // LANES):
        cols = slice(c * LANES, (c + 1) * LANES)
        q = nq_all[:, cols]
        nq_ref[:, cols] = (q * scale).astype(BF16)
        nqr_ref[:, cols] = (rope_n(q) * scale).astype(BF16)

    kv_all = _dot(xb, wkv_ref[...])

    def kv(i):
        return kv_all[:, i * KV_WIDTH:(i + 1) * KV_WIDTH]

    def split_groups(ref, v):
        for g in range(NSA_KV_GROUPS):
            ref[g] = v[:, g * NSA_DIM:(g + 1) * NSA_DIM].astype(BF16)

    kc_ref[...] = kv(0)
    vc_ref[...] = kv(1)
    split_groups(ks_ref, rope_n(kv(2)))
    split_groups(vs_ref, kv(3))
    split_groups(kw_ref, rope_n(kv(4)))
    split_groups(vw_ref, kv(5))

    gt = jax.nn.sigmoid(_dot_nt(wg_ref[...], xb))
    for g in range(NSA_KV_GROUPS):
        gate_ref[g] = gt[g * GATE_LANES:(g + 1) * GATE_LANES, :]


def _inproj(x2d, pos_col, w_in):
    n = x2d.shape[0]
    tm = 512 if n % 512 == 0 else n
    off = np.cumsum([0] + [RET_WIDTH] * 4 + [NSA_WIDTH] + [KV_WIDTH] * 6)
    w_ret = w_in[:, :off[4]].astype(BF16)
    w_nq = w_in[:, off[4]:off[5]].astype(BF16)
    w_kv = w_in[:, off[5]:off[11]].astype(BF16)
    wg = w_in[:, off[11]:].reshape(D_MODEL, NSA_KV_GROUPS, NSA_HPG * 3)
    wg = jnp.pad(wg, ((0, 0), (0, 0), (0, GATE_LANES - NSA_HPG * 3)))
    wg = wg.reshape(D_MODEL, NSA_KV_GROUPS * GATE_LANES).T.astype(BF16)

    lane = np.arange(LANES)
    half_r = RET_DIM // 2
    inv_r = (np.float32(RET_ROPE_BASE) ** (-np.arange(half_r, dtype=np.float32) / np.float32(half_r)))
    inv_r = inv_r.astype(np.float32)[lane % half_r][None, :]
    half_n = ROPE_DIMS // 2
    inv_n = (np.float32(ROPE_THETA) ** (-np.arange(half_n, dtype=np.float32) / np.float32(half_n)))
    jn = lane % NSA_DIM
    inv_n = np.where(jn < ROPE_DIMS, inv_n.astype(np.float32)[jn % half_n], np.float32(0.0))[None, :]

    row = lambda w: pl.BlockSpec((tm, w), lambda i: (i, 0))
    full = lambda a: pl.BlockSpec(a.shape, lambda i: (0,) * a.ndim)
    grp = lambda w: pl.BlockSpec((NSA_KV_GROUPS, tm, w), lambda i: (0, i, 0))
    bf = lambda w: jax.ShapeDtypeStruct((n, w), BF16)
    gbf = jax.ShapeDtypeStruct((NSA_KV_GROUPS, n, NSA_DIM), BF16)
    inv_r = jnp.asarray(inv_r, F32)
    inv_n = jnp.asarray(inv_n, F32)
    return pl.pallas_call(
        _inproj_kernel,
        grid=(n // tm,),
        in_specs=[row(D_MODEL), row(1), full(w_ret), full(w_nq), full(w_kv), full(wg),
                  full(inv_r), full(inv_n)],
        out_specs=[row(RET_WIDTH)] * 4 + [row(NSA_WIDTH)] * 2 + [row(KV_WIDTH)] * 2
                  + [grp(NSA_DIM)] * 4
                  + [pl.BlockSpec((NSA_KV_GROUPS, GATE_LANES, tm), lambda i: (0, 0, i))],
        out_shape=[bf(RET_WIDTH)] * 4 + [bf(NSA_WIDTH)] * 2
                  + [jax.ShapeDtypeStruct((n, KV_WIDTH), F32)] * 2 + [gbf] * 4
                  + [jax.ShapeDtypeStruct((NSA_KV_GROUPS, GATE_LANES, n), F32)],
        compiler_params=_params(1),
        name="inproj",
    )(x2d, pos_col, w_ret, w_nq, w_kv, wg, inv_r, inv_n)


def _retention_kernel(q_ref, k_ref, v_ref, g_ref, o_ref, state_ref):
    c = RET_CHUNK

    @pl.when(pl.program_id(1) == 0)
    def _():
        state_ref[...] = jnp.zeros_like(state_ref)

    row = lax.broadcasted_iota(jnp.int32, (c, c), 0)
    col = lax.broadcasted_iota(jnp.int32, (c, c), 1)
    rel = (row - col).astype(F32)
    idx = lax.broadcasted_iota(jnp.int32, (c, 1), 0).astype(F32)
    for h in range(RET_HEADS):
        log_g = float(np.log(np.float32(1.0) - np.float32(2.0) ** np.float32(-5.0 - h)))
        cols = slice(h * RET_DIM, (h + 1) * RET_DIM)
        q = q_ref[:, cols]
        k = k_ref[:, cols]
        v = v_ref[:, cols]
        dmask = jnp.where(rel >= 0, jnp.exp(log_g * jnp.maximum(rel, 0.0)), 0.0)
        scores = _dot_nt(q, k) * dmask
        inner = _dot(scores.astype(BF16), v)
        zeta = jnp.exp(log_g * (c - 1.0 - idx))
        xi = jnp.exp(log_g * (idx + 1.0))
        prev = state_ref[h]
        cross = _dot(q, prev.astype(BF16)) * xi
        kz = (k.astype(F32) * zeta).astype(BF16)
        kv = lax.dot_general(kz, v, (((0,), (0,)), ((), ())), preferred_element_type=F32)
        state_ref[h] = prev * float(np.exp(np.float32(log_g) * np.float32(c))) + kv
        o = inner + cross
        mu = jnp.mean(o, axis=-1, keepdims=True)
        d = o - mu
        var = jnp.mean(d * d, axis=-1, keepdims=True)
        o = d * lax.rsqrt(var + LN_EPS)
        o_ref[:, cols] = (jax.nn.silu(g_ref[:, cols].astype(F32)) * o).astype(BF16)


def _retention(rq, rk, rv, rg, batch, seq):
    nc = seq // RET_CHUNK
    spec = pl.BlockSpec((RET_CHUNK, RET_WIDTH), lambda b, n: (b * nc + n, 0))
    return pl.pallas_call(
        _retention_kernel,
        grid=(batch, nc),
        in_specs=[spec] * 4,
        out_specs=spec,
        out_shape=jax.ShapeDtypeStruct(rq.shape, BF16),
        scratch_shapes=[pltpu.VMEM((RET_HEADS, RET_DIM, RET_DIM), F32)],
        compiler_params=_params(2),
        name="retention",
    )(rq, rk, rv, rg)


def _compress_kernel(a_ref, pe_ref, w1_ref, w2_ref, o_ref, shift_ref, *, n_cmp):
    rows = a_ref.shape[0]
    a = a_ref[...]
    lo = (a + pe_ref[0]).astype(BF16)
    hi = (a + pe_ref[1]).astype(BF16)
    ridx = lax.broadcasted_iota(jnp.int32, (rows, 1), 0)
    shift_ref[rows:rows + 8, :] = jnp.zeros((8, CMP_HIDDEN), F32)
    for g in range(NSA_KV_GROUPS):
        p = _dot(lo, w1_ref[0, g])
        shift_ref[0:rows, :] = _dot(hi, w1_ref[1, g])
        hid = jax.nn.silu(p + shift_ref[pl.ds(1, rows), :])
        out = _dot(hid.astype(BF16), w2_ref[...])
        o_ref[g] = jnp.where(ridx < n_cmp, out, 0.0).astype(BF16)


def _compress(a, pe, w1, w2, batch, seq):
    rows = seq // CMP_STRIDE
    per = CMP_STRIDE * KV_WIDTH
    n_cmp = (seq - CMP_LEN) // CMP_STRIDE + 1
    a2 = a.reshape(batch * rows, per)
    pe2 = jnp.tile(pe.reshape(2, CMP_STRIDE, 1, NSA_DIM), (1, 1, NSA_KV_GROUPS, 1)).reshape(2, 1, per)
    w1r = w1.reshape(2, CMP_STRIDE, 1, NSA_DIM, CMP_HIDDEN)
    eye = jnp.eye(NSA_KV_GROUPS, dtype=w1.dtype).reshape(1, NSA_KV_GROUPS, 1, NSA_KV_GROUPS, 1, 1)
    w1x = (w1r[:, None] * eye).reshape(2, NSA_KV_GROUPS, per, CMP_HIDDEN).astype(BF16)
    w2b = w2.astype(BF16)
    full = lambda arr: pl.BlockSpec(arr.shape, lambda b: (0,) * arr.ndim)
    return pl.pallas_call(
        functools.partial(_compress_kernel, n_cmp=n_cmp),
        grid=(batch,),
        in_specs=[pl.BlockSpec((rows, per), lambda b: (b, 0)), full(pe2), full(w1x), full(w2b)],
        out_specs=pl.BlockSpec((None, NSA_KV_GROUPS, rows, NSA_DIM), lambda b: (b, 0, 0, 0)),
        out_shape=jax.ShapeDtypeStruct((batch, NSA_KV_GROUPS, rows, NSA_DIM), BF16),
        scratch_shapes=[pltpu.VMEM((rows + 8, CMP_HIDDEN), F32)],
        compiler_params=_params(1),
        name="compress",
    )(a2, pe2, w1x, w2b)


def _heads_to_lanes(ref):
    vt = ref[...].astype(F32).T
    return jnp.concatenate([vt[h * NSA_DIM:(h + 1) * NSA_DIM] for h in range(NSA_HPG)], axis=1).astype(BF16)


def _tile_heads(v):
    return jnp.concatenate([v] * NSA_HPG, axis=1)


def _transpose_into(dst_ref, src_ref, chunk):
    def step(c, _):
        c0 = pl.multiple_of(c * chunk, chunk)
        dst_ref[:NSA_DIM, pl.ds(c0, chunk)] = src_ref[pl.ds(c0, chunk), :].astype(F32).T.astype(BF16)
        return 0
    lax.fori_loop(0, src_ref.shape[0] // chunk, step, 0)


def _nsa_kernel(qraw_ref, qrot_ref, gate_ref, kcmp_ref, vcmp_ref, ovt_ref,
                ks_ref, vs_ref, kw_ref, vw_ref, o_ref, vst_ref, vwt_ref, vct_ref, bias_ref, *, tq, tk, seq):
    i = pl.program_id(2)
    t0 = i * tq
    cols = NSA_HPG * tq
    n_sel = seq // SEL_LEN
    n_cmp_rows = seq // CMP_STRIDE

    @pl.when(i == 0)
    def _():
        chunk = min(512, n_cmp_rows)
        _transpose_into(vst_ref, vs_ref, chunk)
        _transpose_into(vwt_ref, vw_ref, chunk)
        _transpose_into(vct_ref, vcmp_ref, chunk)
        vst_ref[NSA_DIM:, :] = jnp.ones((SUM_ROWS, seq), BF16)
        vwt_ref[NSA_DIM:, :] = jnp.ones((SUM_ROWS, seq), BF16)

    def split_sum(acc):
        return acc[:NSA_DIM] / acc[NSA_DIM:NSA_DIM + 1]

    q_raw = _heads_to_lanes(qraw_ref)
    q_rot = _heads_to_lanes(qrot_ref)
    t_row = t0 + lax.broadcasted_iota(jnp.int32, (1, tq), 1)

    chain_w = cols // NSA_CHAINS
    heads_per_chain = chain_w // tq
    chains = [slice(c * chain_w, (c + 1) * chain_w) for c in range(NSA_CHAINS)]
    tile_chain = lambda v: jnp.concatenate([v] * heads_per_chain, axis=1)

    span = WIN + tq
    ws = pl.multiple_of(jnp.maximum(t0 - WIN, 0), tq)
    dist = t_row - (ws + lax.broadcasted_iota(jnp.int32, (span, 1), 0))
    bias_w = tile_chain(jnp.where((dist >= 0) & (dist < WIN), 0.0, NEG))
    k_w = kw_ref[pl.ds(ws, span), :]
    v_w = vwt_ref[:, pl.ds(ws, span)]
    o_w = []
    for c in chains:
        s_w = _dot(k_w, q_rot[:, c]) + bias_w
        p_w = jnp.exp((s_w - jnp.max(s_w, axis=0, keepdims=True)).astype(BF16))
        o_w.append(split_sum(_dot(v_w, p_w)))
    o_w = jnp.concatenate(o_w, axis=1)

    c_idx = lax.broadcasted_iota(jnp.int32, (n_cmp_rows, 1), 0)
    valid = tile_chain(jnp.where(c_idx * CMP_STRIDE + (CMP_LEN - 1) <= t_row, 1.0, 0.0))
    bias_c = (valid - 1.0) * (-NEG)
    o_c = []
    p_sum = None
    for c in chains:
        s_c = _dot(kcmp_ref[...], q_raw[:, c]) + bias_c
        e_c = jnp.exp(s_c - jnp.max(s_c, axis=0, keepdims=True)) * valid
        l_c = jnp.sum(e_c, axis=0, keepdims=True)
        p_c = e_c / jnp.where(l_c > 0.0, l_c, 1.0)
        o_c.append(_dot(vct_ref[...], p_c.astype(BF16)))
        for h in range(heads_per_chain):
            p_h = p_c[:, h * tq:(h + 1) * tq]
            p_sum = p_h if p_sum is None else p_sum + p_h
    o_c = jnp.concatenate(o_c, axis=1)

    p_hi = p_sum.astype(BF16)
    p_lo = (p_sum - p_hi.astype(F32)).astype(BF16)
    ovt = ovt_ref[...]
    imp = _dot(ovt, p_hi) + _dot(ovt, p_lo)
    jb = lax.broadcasted_iota(jnp.int32, (n_sel, tq), 0)
    cur = (t0 + lax.broadcasted_iota(jnp.int32, (n_sel, tq), 1)) >> SEL_SHIFT
    forced = (jb == 0) | (jb == cur) | (jb == cur - 1)
    work = jnp.where(forced, FORCE, imp)
    work = jnp.where(jb <= cur, work, NEG)
    sel_t = jnp.zeros((n_sel, tq), F32)
    for _ in range(min(SEL_TOPK, n_sel)):
        best = jnp.max(work, axis=0, keepdims=True)
        first = jnp.min(jnp.where(work == best, jb, n_sel), axis=0, keepdims=True)
        hit = jb == first
        sel_t = jnp.where(hit, 1.0, sel_t)
        work = jnp.where(hit, -jnp.inf, work)
    bias_ref[...] = jnp.where(sel_t > 0.5, 0.0, NEG)

    blocks_per_tile = tk // SEL_LEN

    def sel_tile(kt, carry, causal):
        k0 = pl.multiple_of(kt * tk, tk)
        bias = jnp.concatenate(
            [jnp.broadcast_to(bias_ref[pl.ds(kt * blocks_per_tile + j, 1), :], (SEL_LEN, tq))
             for j in range(blocks_per_tile)], axis=0)
        if causal:
            kpos = k0 + lax.broadcasted_iota(jnp.int32, (tk, 1), 0)
            bias = jnp.where(kpos <= t_row, bias, NEG)
        bias = tile_chain(bias)
        k_t = ks_ref[pl.ds(k0, tk), :]
        v_t = vst_ref[:, pl.ds(k0, tk)]
        out = []
        scores = [_dot(k_t, q_rot[:, c]) + bias for c in chains]
        for (m, acc), s in zip(carry, scores):
            m_new = jnp.maximum(m, jnp.max(s, axis=0, keepdims=True))
            p = jnp.exp((s - m_new).astype(BF16))
            acc = jnp.exp(m - m_new) * acc + _dot(v_t, p)
            out.append((m_new, acc))
        return tuple(out)

    n_full = t0 // tk
    init = tuple((jnp.full((1, chain_w), NEG, F32), jnp.zeros((NSA_DIM + SUM_ROWS, chain_w), F32))
                 for _ in chains)
    carry = lax.fori_loop(0, n_full, functools.partial(sel_tile, causal=False), init)
    carry = sel_tile(n_full, carry, causal=True)
    o_s = jnp.concatenate([split_sum(acc) for _, acc in carry], axis=1)

    gt = gate_ref[...]
    outs = []
    for h in range(NSA_HPG):
        c = slice(h * tq, (h + 1) * tq)
        outs.append(gt[3 * h:3 * h + 1] * o_c[:, c] + gt[3 * h + 1:3 * h + 2] * o_s[:, c]
                    + gt[3 * h + 2:3 * h + 3] * o_w[:, c])
    o_ref[...] = jnp.concatenate(outs, axis=0).T.astype(BF16)


def _nsa(nq, nqr, gates, kcmp, vcmp, ks, vs, kw, vw, batch, seq):
    n = batch * seq
    tq = 512
    tk = 512 if seq % 512 == 0 else seq
    nqb = seq // tq
    n_sel = seq // SEL_LEN
    rows_c = seq // CMP_STRIDE
    gw = NSA_HPG * NSA_DIM
    cs = np.arange(rows_c)[None, :] * CMP_STRIDE
    ss = np.arange(n_sel)[:, None] * SEL_LEN
    n_cmp = (seq - CMP_LEN) // CMP_STRIDE + 1
    ovt = ((cs < ss + SEL_LEN) & (cs + CMP_LEN > ss) & (np.arange(rows_c)[None, :] < n_cmp))
    ovt = jnp.asarray(ovt.astype(np.float32), BF16)

    qspec = pl.BlockSpec((tq, gw), lambda b, g, i: (b * nqb + i, g))
    cspec = pl.BlockSpec((None, None, rows_c, NSA_DIM), lambda b, g, i: (b, g, 0, 0))
    kspec = pl.BlockSpec((None, seq, NSA_DIM), lambda b, g, i: (g, b, 0))
    return pl.pallas_call(
        functools.partial(_nsa_kernel, tq=tq, tk=tk, seq=seq),
        grid=(batch, NSA_KV_GROUPS, nqb),
        in_specs=[qspec, qspec,
                  pl.BlockSpec((None, GATE_LANES, tq), lambda b, g, i: (g, 0, b * nqb + i)),
                  cspec, cspec, pl.BlockSpec(ovt.shape, lambda b, g, i: (0, 0)),
                  kspec, kspec, kspec, kspec],
        out_specs=qspec,
        out_shape=jax.ShapeDtypeStruct((n, NSA_WIDTH), BF16),
        scratch_shapes=[pltpu.VMEM((NSA_DIM + SUM_ROWS, seq), BF16), pltpu.VMEM((NSA_DIM + SUM_ROWS, seq), BF16),
                        pltpu.VMEM((NSA_DIM, rows_c), BF16), pltpu.VMEM((n_sel, tq), F32)],
        compiler_params=_params(3),
        name="nsa",
    )(nq, nqr, gates, kcmp, vcmp, ovt, ks, vs, kw, vw)


def _memkv_kernel(mem_ref, w_ref, kv_ref):
    kv_ref[...] = _dot(mem_ref[...].astype(BF16), w_ref[...]).astype(BF16)


def _memkv(mem2d, w_xkv):
    n = mem2d.shape[0]
    w = w_xkv.astype(BF16)
    return pl.pallas_call(
        _memkv_kernel,
        grid=(n // MEM_LEN,),
        in_specs=[pl.BlockSpec((MEM_LEN, D_MODEL), lambda i: (i, 0)),
                  pl.BlockSpec(w.shape, lambda i: (0, 0))],
        out_specs=pl.BlockSpec((MEM_LEN, 2 * D_MODEL), lambda i: (i, 0)),
        out_shape=jax.ShapeDtypeStruct((n, 2 * D_MODEL), BF16),
        compiler_params=_params(1),
        name="memkv",
    )(mem2d, w)


ROW_TILES = D_MODEL // 2 // LANES


def _pack_halves(v):
    half = D_MODEL // 2
    hi = pltpu.bitcast(v[:, :half].astype(BF16).astype(F32), jnp.uint32)
    lo = pltpu.bitcast(v[:, half:].astype(BF16).astype(F32), jnp.uint32)
    return hi | (lo >> 16)


def _unpack_halves(words):
    return pltpu.bitcast(words & jnp.uint32(0xFFFF0000), F32), pltpu.bitcast(words << 16, F32)


def _store_packed_rows(ref, v):
    rows = v.shape[0]
    packed = _pack_halves(v)
    for s in range(ROW_TILES):
        ref[pl.ds(s, rows, stride=ROW_TILES), :] = packed[:, s * LANES:(s + 1) * LANES]


def _load_packed_halves(ref):
    rows = ref.shape[0] // ROW_TILES
    words = [ref[pl.ds(s, rows, stride=ROW_TILES), :] for s in range(ROW_TILES)]
    hi = jnp.concatenate([pltpu.bitcast(w & jnp.uint32(0xFFFF0000), F32) for w in words], axis=1)
    lo = jnp.concatenate([pltpu.bitcast(w << 16, F32) for w in words], axis=1)
    return hi, lo


def _postmix_kernel(x_ref, oret_ref, onsa_ref, kv_ref, wout_ref, wq_ref, wo_ref,
                    g1_ref, b1_ref, g2_ref, b2_ref, x2_ref, x2p_ref):
    mixed = jnp.concatenate([oret_ref[...], onsa_ref[...]], axis=1)
    x1 = _layer_norm(DN_ALPHA * x_ref[...] + _dot(mixed, wout_ref[...]), g1_ref[...], b1_ref[...])
    q = (_dot(x1.astype(BF16), wq_ref[...]) * (XATT_DIM ** -0.5)).astype(BF16)
    heads = []
    for h in range(XATT_HEADS):
        cols = slice(h * XATT_DIM, (h + 1) * XATT_DIM)
        s = _dot_nt(q[:, cols], kv_ref[:, cols])
        m = jnp.max(s, axis=-1, keepdims=True)
        p = jnp.exp(s - m)
        l = jnp.sum(p, axis=-1, keepdims=True)
        heads.append(_dot(p.astype(BF16), kv_ref[:, D_MODEL + h * XATT_DIM:D_MODEL + (h + 1) * XATT_DIM]) / l)
    att = jnp.concatenate(heads, axis=1).astype(BF16)
    x2 = _layer_norm(DN_ALPHA * x1 + _dot(att, wo_ref[...]), g2_ref[...], b2_ref[...])
    x2_ref[...] = x2
    _store_packed_rows(x2p_ref, x2)


def _postmix(x2d, o_ret, o_nsa, kvx, w_out, w_xq, w_xo, ln1_g, ln1_b, ln2_g, ln2_b, batch, seq):
    n = x2d.shape[0]
    tm = 256 if seq % 256 == 0 else seq
    per_b = seq // tm
    row = lambda w: pl.BlockSpec((tm, w), lambda b, i: (b * per_b + i, 0))
    full = lambda a: pl.BlockSpec(a.shape, lambda b, i: (0,) * a.ndim)
    ws = [w_out.astype(BF16), w_xq.astype(BF16), w_xo.astype(BF16)]
    vecs = [v.reshape(1, D_MODEL) for v in (ln1_g, ln1_b, ln2_g, ln2_b)]
    return pl.pallas_call(
        _postmix_kernel,
        grid=(batch, per_b),
        in_specs=[row(D_MODEL), row(RET_WIDTH), row(NSA_WIDTH),
                  pl.BlockSpec((MEM_LEN, 2 * D_MODEL), lambda b, i: (b, 0))]
                 + [full(w) for w in ws] + [full(v) for v in vecs],
        out_specs=[row(D_MODEL),
                   pl.BlockSpec((tm * ROW_TILES, LANES), lambda b, i: (b * per_b + i, 0))],
        out_shape=[jax.ShapeDtypeStruct((n, D_MODEL), F32),
                   jax.ShapeDtypeStruct((n * ROW_TILES, LANES), jnp.uint32)],
        compiler_params=_params(2),
        name="postmix",
    )(x2d, o_ret, o_nsa, kvx, *ws, *vecs)


def _router_kernel(x_ref, wr_ref, bias_ref, e_ref, rank_ref, w_ref, cnt_ref, cntrow_ref, carry_ref, carryrow_ref):
    tn = x_ref.shape[0]
    per = N_EXPERTS // N_GROUPS

    @pl.when(pl.program_id(0) == 0)
    def _():
        carry_ref[...] = jnp.zeros_like(carry_ref)
        carryrow_ref[...] = jnp.zeros_like(carryrow_ref)

    logits = _dot_nt(wr_ref[...], x_ref[...].astype(BF16))
    scores = jax.nn.sigmoid(logits)
    biased = scores + bias_ref[...]
    b3 = biased.reshape(N_GROUPS, per, tn)
    member = lax.broadcasted_iota(jnp.int32, (N_GROUPS, per, tn), 1)
    top1 = jnp.max(b3, axis=1, keepdims=True)
    first1 = jnp.min(jnp.where(b3 == top1, member, per), axis=1, keepdims=True)
    top2 = jnp.max(jnp.where(member == first1, -jnp.inf, b3), axis=1, keepdims=True)
    gscore = top1 + top2
    gidx = lax.broadcasted_iota(jnp.int32, (N_GROUPS, 1, tn), 0)
    gwork = gscore
    for _ in range(TOPK_GROUPS - 1):
        gbest = jnp.max(gwork, axis=0, keepdims=True)
        gfirst = jnp.min(jnp.where(gwork == gbest, gidx, N_GROUPS), axis=0, keepdims=True)
        gwork = jnp.where(gidx == gfirst, -jnp.inf, gwork)
    kth = jnp.max(gwork, axis=0, keepdims=True)
    work = jnp.where(gscore >= kth, b3, NEG).reshape(N_EXPERTS, tn)
    eidx = lax.broadcasted_iota(jnp.int32, (N_EXPERTS, tn), 0)
    picks = []
    chosen = jnp.zeros((N_EXPERTS, tn), F32)
    for _ in range(TOP_K):
        best = jnp.max(work, axis=0, keepdims=True)
        first = jnp.min(jnp.where(work == best, eidx, N_EXPERTS), axis=0, keepdims=True)
        hit = eidx == first
        picks.append((first, hit))
        chosen = jnp.where(hit, 1.0, chosen)
        work = jnp.where(hit, -jnp.inf, work)

    r_i = lax.broadcasted_iota(jnp.int32, (tn, tn), 0)
    c_i = lax.broadcasted_iota(jnp.int32, (tn, tn), 1)
    before = jnp.where(r_i < c_i, 1.0, 0.0).astype(BF16)
    chosen_b = chosen.astype(BF16)
    rank = _dot(chosen_b, before) + carry_ref[...]
    carry_ref[...] = carry_ref[...] + jnp.sum(chosen, axis=1, keepdims=True)
    carryrow_ref[...] = carryrow_ref[...] + _dot_nt(jnp.ones((8, tn), BF16), chosen_b)
    cnt_ref[...] = carry_ref[...]
    cntrow_ref[...] = carryrow_ref[...]

    wsel = [jnp.sum(jnp.where(hit, scores, 0.0), axis=0, keepdims=True) for _, hit in picks]
    wsum = wsel[0]
    for v in wsel[1:]:
        wsum = wsum + v
    for kk, (first, hit) in enumerate(picks):
        e_ref[kk:kk + 1, :] = first
        rank_ref[kk:kk + 1, :] = jnp.sum(jnp.where(hit, rank, 0.0), axis=0, keepdims=True).astype(jnp.int32)
        w_ref[kk:kk + 1, :] = wsel[kk] / wsum * ROUTED_SCALE


def _router(x2, w_router, router_bias):
    n = x2.shape[0]
    tn = 512 if n % 512 == 0 else n
    wr_t = w_router.T.astype(BF16)
    bias = router_bias.reshape(N_EXPERTS, 1).astype(F32)
    kspec = pl.BlockSpec((TOP_K, tn), lambda i: (0, i))
    return pl.pallas_call(
        _router_kernel,
        grid=(n // tn,),
        in_specs=[pl.BlockSpec((tn, D_MODEL), lambda i: (i, 0)),
                  pl.BlockSpec(wr_t.shape, lambda i: (0, 0)),
                  pl.BlockSpec(bias.shape, lambda i: (0, 0))],
        out_specs=[kspec, kspec, kspec, pl.BlockSpec((N_EXPERTS, 1), lambda i: (0, 0)),
                   pl.BlockSpec((8, N_EXPERTS), lambda i: (0, 0))],
        out_shape=[jax.ShapeDtypeStruct((TOP_K, n), jnp.int32),
                   jax.ShapeDtypeStruct((TOP_K, n), jnp.int32),
                   jax.ShapeDtypeStruct((TOP_K, n), F32),
                   jax.ShapeDtypeStruct((N_EXPERTS, 1), F32),
                   jax.ShapeDtypeStruct((8, N_EXPERTS), F32)],
        scratch_shapes=[pltpu.VMEM((N_EXPERTS, 1), F32), pltpu.VMEM((8, N_EXPERTS), F32)],
        compiler_params=_params(1),
        name="router",
    )(x2, wr_t, bias)


def _slots_kernel(e_ref, rank_ref, cnt_ref, cntrow_ref, dest_ref, blk_e_ref, nblk_ref, *, blk, n_blocks):
    pad = lambda c: jnp.ceil(c / blk) * blk
    padded = pad(cnt_ref[...])
    padded_row = pad(cntrow_ref[0:1, :])
    r_i = lax.broadcasted_iota(jnp.int32, (N_EXPERTS, N_EXPERTS), 0)
    c_i = lax.broadcasted_iota(jnp.int32, (N_EXPERTS, N_EXPERTS), 1)
    start = jnp.sum(jnp.where(c_i < r_i, padded_row, 0.0), axis=1, keepdims=True)
    end = start + padded
    e = e_ref[...]
    dest = rank_ref[...]
    for ex in range(N_EXPERTS):
        dest = dest + jnp.where(e == ex, start[ex:ex + 1, :].astype(jnp.int32), 0)
    dest_ref[...] = dest
    bstart = (lax.broadcasted_iota(jnp.int32, (1, n_blocks), 1) * blk).astype(F32)
    owner = jnp.sum(jnp.where(end <= bstart, 1.0, 0.0), axis=0, keepdims=True)
    blk_e_ref[...] = jnp.minimum(owner, N_EXPERTS - 1.0).astype(jnp.int32)
    nblk_ref[...] = (end[N_EXPERTS - 1:N_EXPERTS, :] / blk).astype(jnp.int32)


def _slots(e_k, rank_k, counts, counts_row, blk, n_blocks):
    n = e_k.shape[1]
    full = lambda shape: pl.BlockSpec(shape, lambda: (0,) * len(shape))
    return pl.pallas_call(
        functools.partial(_slots_kernel, blk=blk, n_blocks=n_blocks),
        in_specs=[full((TOP_K, n)), full((TOP_K, n)), full((N_EXPERTS, 1)), full((8, N_EXPERTS))],
        out_specs=[full((TOP_K, n)), full((1, n_blocks)), full((1, 1))],
        out_shape=[jax.ShapeDtypeStruct((TOP_K, n), jnp.int32),
                   jax.ShapeDtypeStruct((1, n_blocks), jnp.int32),
                   jax.ShapeDtypeStruct((1, 1), jnp.int32)],
        compiler_params=pltpu.CompilerParams(vmem_limit_bytes=VMEM_LIMIT),
        name="slots",
    )(e_k, rank_k, counts, counts_row)


def _row_copy(src_ref, src_row, dst_ref, dst_row, sem):
    span = lambda r: pl.ds(pl.multiple_of(r * ROW_TILES, ROW_TILES), ROW_TILES)
    return pltpu.make_async_copy(src_ref.at[span(src_row)], dst_ref.at[span(dst_row)], sem)


def _dest_spec(tt):
    return pl.BlockSpec((None, 1, TOP_K * tt), lambda i: (i, 0, 0), memory_space=pltpu.SMEM)


def _dispatch_kernel(dest_ref, x_ref, zero_ref, xs_ref, sem, *, tt):
    del zero_ref

    def issue(j, _):
        for kk in range(TOP_K):
            _row_copy(x_ref, j, xs_ref, dest_ref[0, kk * tt + j], sem).start(priority=kk % 2)
        return 0

    lax.fori_loop(0, tt, issue, 0)

    def drain(j, _):
        for kk in range(TOP_K):
            _row_copy(x_ref, 0, xs_ref, 0, sem).wait()
        return 0

    lax.fori_loop(0, tt, drain, 0)


def _dispatch(dest_tiles, x2p, cap, tt):
    n = x2p.shape[0] // ROW_TILES
    zeros = jnp.zeros((cap * ROW_TILES, LANES), x2p.dtype)
    return pl.pallas_call(
        functools.partial(_dispatch_kernel, tt=tt),
        grid=(n // tt,),
        in_specs=[_dest_spec(tt),
                  pl.BlockSpec((tt * ROW_TILES, LANES), lambda i: (i, 0)),
                  pl.BlockSpec(memory_space=pl.ANY)],
        out_specs=pl.BlockSpec(memory_space=pl.ANY),
        scratch_shapes=[pltpu.SemaphoreType.DMA(())],
        out_shape=jax.ShapeDtypeStruct(zeros.shape, x2p.dtype),
        input_output_aliases={2: 0},
        compiler_params=_params(1),
        name="dispatch",
    )(dest_tiles, x2p, zeros)


def _experts_kernel(blk_e_ref, nblk_ref, xs_ref, wg_ref, wu_ref, wd_ref, y_ref):
    del blk_e_ref

    @pl.when(pl.program_id(0) < nblk_ref[0])
    def _():
        half = D_MODEL // 2
        hi, lo = (v.astype(BF16) for v in _load_packed_halves(xs_ref))
        gate = _dot(hi, wg_ref[:half, :]) + _dot(lo, wg_ref[half:, :])
        up = _dot(hi, wu_ref[:half, :]) + _dot(lo, wu_ref[half:, :])
        y_ref[...] = _pack_halves(_dot((jax.nn.silu(gate) * up).astype(BF16), wd_ref[...]))

    @pl.when(pl.program_id(0) >= nblk_ref[0])
    def _():
        y_ref[...] = jnp.zeros_like(y_ref)


def _experts(blk_e, nblk, xs, w_gate, w_up, w_down, blk):
    cap = xs.shape[0] // ROW_TILES
    wg, wu, wd = (a.astype(BF16) for a in (w_gate, w_up, w_down))
    wspec = lambda a: pl.BlockSpec((None,) + a.shape[1:], lambda i, be, nb: (be[i], 0, 0))
    return pl.pallas_call(
        _experts_kernel,
        grid_spec=pltpu.PrefetchScalarGridSpec(
            num_scalar_prefetch=2,
            grid=(cap // blk,),
            in_specs=[pl.BlockSpec((blk * ROW_TILES, LANES), lambda i, be, nb: (i, 0)),
                      wspec(wg), wspec(wu), wspec(wd)],
            out_specs=pl.BlockSpec((blk, D_MODEL // 2), lambda i, be, nb: (i, 0)),
        ),
        out_shape=jax.ShapeDtypeStruct((cap, D_MODEL // 2), xs.dtype),
        compiler_params=_params(1),
        name="experts",
    )(blk_e, nblk, xs, wg, wu, wd)


def _sc_gather_rows(table, idx):
    b, width = idx.shape[0], table.shape[1]
    workers = SC_CORES * SC_SUBCORES
    per_worker = b // workers
    assert per_worker * workers == b and per_worker % SC_CHUNK == 0
    mesh = plsc.VectorSubcoreMesh(core_axis_name="c", subcore_axis_name="s")

    @functools.partial(
        pl.kernel, mesh=mesh,
        out_type=jax.ShapeDtypeStruct((b, width), table.dtype),
        scratch_types=[pltpu.VMEM((SC_CHUNK,), jnp.int32), pltpu.VMEM((SC_CHUNK, width), table.dtype),
                       pltpu.SemaphoreType.DMA],
        name="sc_gather")
    def gather(table_hbm, idx_hbm, out_hbm, idx_v, rows_v, sem):
        base = (lax.axis_index("s") * SC_CORES + lax.axis_index("c")) * per_worker

        @pl.loop(0, per_worker // SC_CHUNK)
        def _(ci):
            off = pl.multiple_of(base + ci * SC_CHUNK, SC_CHUNK)
            pltpu.sync_copy(idx_hbm.at[pl.ds(off, SC_CHUNK)], idx_v)
            pltpu.async_copy(table_hbm.at[idx_v], rows_v, sem).wait()
            pltpu.sync_copy(rows_v, out_hbm.at[pl.ds(off, SC_CHUNK)])

    return gather(table, idx)


def _combine_kernel(x_ref, wk_ref, yk_ref, wsg_ref, wsu_ref, wsd_ref, g_ref, b_ref, o_ref):
    x = x_ref[...]
    xb = x.astype(BF16)
    shared = _dot((jax.nn.silu(_dot(xb, wsg_ref[...])) * _dot(xb, wsu_ref[...])).astype(BF16), wsd_ref[...])
    wk = wk_ref[...]
    routed_hi = routed_lo = None
    for kk in range(TOP_K):
        hi, lo = _unpack_halves(yk_ref[kk])
        w = wk[:, kk:kk + 1]
        routed_hi = hi * w if kk == 0 else routed_hi + hi * w
        routed_lo = lo * w if kk == 0 else routed_lo + lo * w
    routed = jnp.concatenate([routed_hi, routed_lo], axis=1)
    o_ref[...] = _layer_norm(DN_ALPHA * x + (routed + shared), g_ref[...], b_ref[...])


def _combine(x2, w_tok, yk, ws_gate, ws_up, ws_down, ln3_g, ln3_b):
    n = x2.shape[0]
    tt = 256 if n % 256 == 0 else n
    ws = [ws_gate.astype(BF16), ws_up.astype(BF16), ws_down.astype(BF16)]
    vecs = [ln3_g.reshape(1, D_MODEL), ln3_b.reshape(1, D_MODEL)]
    full = lambda a: pl.BlockSpec(a.shape, lambda i: (0,) * a.ndim)
    return pl.pallas_call(
        _combine_kernel,
        grid=(n // tt,),
        in_specs=[pl.BlockSpec((tt, D_MODEL), lambda i: (i, 0)),
                  pl.BlockSpec((tt, TOP_K), lambda i: (i, 0)),
                  pl.BlockSpec((TOP_K, tt, D_MODEL // 2), lambda i: (0, i, 0))]
                 + [full(a) for a in ws] + [full(v) for v in vecs],
        out_specs=pl.BlockSpec((tt, D_MODEL), lambda i: (i, 0)),
        out_shape=jax.ShapeDtypeStruct((n, D_MODEL), F32),
        compiler_params=_params(1),
        name="combine",
    )(x2, w_tok, yk, *ws, *vecs)


def _moe_and_norm(x2, x2p, w_router, router_bias, w_gate, w_up, w_down, ws_gate, ws_up, ws_down,
                  ln3_g, ln3_b):
    n = x2.shape[0]
    blk = 512
    tt = 128 if n % 128 == 0 else n
    cap = n * TOP_K + N_EXPERTS * blk
    n_blocks = cap // blk
    e_k, rank_k, w_k, counts, counts_row = _router(x2, w_router, router_bias)
    dest, blk_e, nblk = _slots(e_k, rank_k, counts, counts_row, blk, n_blocks)
    dest_tiles = dest.reshape(TOP_K, n // tt, tt).transpose(1, 0, 2).reshape(n // tt, 1, TOP_K * tt)
    xs = _dispatch(dest_tiles, x2p, cap, tt)
    y = _experts(blk_e.reshape(-1), nblk.reshape(-1), xs, w_gate, w_up, w_down, blk)
    yk = _sc_gather_rows(y, dest.reshape(-1)).reshape(TOP_K, n, D_MODEL // 2)
    return _combine(x2, w_k.T, yk, ws_gate, ws_up, ws_down, ln3_g, ln3_b)


def _layer(x, mem, positions, w_in, cmp_pe_k, cmp_pe_v, cmp_w1_k, cmp_w2_k, cmp_w1_v, cmp_w2_v,
           w_out, ln1_g, ln1_b, w_xq, w_xkv, w_xo, ln2_g, ln2_b, w_router, router_bias,
           w_gate, w_up, w_down, ws_gate, ws_up, ws_down, ln3_g, ln3_b):
    batch, seq, _ = x.shape
    n = batch * seq
    x2d = x.reshape(n, D_MODEL)
    pos_col = positions.astype(F32).reshape(n, 1)
    (rq, rk, rv, rg, nq, nqr, kc, vc, ks, vs, kw, vw, gates) = _inproj(x2d, pos_col, w_in)
    o_ret = _retention(rq, rk, rv, rg, batch, seq)
    kcmp = _compress(kc, cmp_pe_k, cmp_w1_k, cmp_w2_k, batch, seq)
    vcmp = _compress(vc, cmp_pe_v, cmp_w1_v, cmp_w2_v, batch, seq)
    o_nsa = _nsa(nq, nqr, gates, kcmp, vcmp, ks, vs, kw, vw, batch, seq)
    kvx = _memkv(mem.reshape(batch * MEM_LEN, D_MODEL), w_xkv)
    x2, x2p = _postmix(x2d, o_ret, o_nsa, kvx, w_out, w_xq, w_xo, ln1_g, ln1_b, ln2_g, ln2_b, batch, seq)
    out = _moe_and_norm(x2, x2p, w_router, router_bias, w_gate, w_up, w_down,
                        ws_gate, ws_up, ws_down, ln3_g, ln3_b)
    return out.reshape(batch, seq, D_MODEL)


def kernel(x, mem, positions, w_in, cmp_pe_k, cmp_pe_v, cmp_w1_k, cmp_w2_k, cmp_w1_v, cmp_w2_v, w_out, ln1_g, ln1_b, w_xq, w_xkv, w_xo, ln2_g, ln2_b, w_router, router_bias, w_gate, w_up, w_down, ws_gate, ws_up, ws_down, ln3_g, ln3_b):
    for l in range(DEPTH):
        x = _layer(x, mem, positions, w_in[l], cmp_pe_k[l], cmp_pe_v[l], cmp_w1_k[l], cmp_w2_k[l],
                   cmp_w1_v[l], cmp_w2_v[l], w_out[l], ln1_g[l], ln1_b[l], w_xq[l], w_xkv[l],
                   w_xo[l], ln2_g[l], ln2_b[l], w_router[l], router_bias[l], w_gate[l], w_up[l],
                   w_down[l], ws_gate[l], ws_up[l], ws_down[l], ln3_g[l], ln3_b[l])
    return x
```

```python
import functools

import numpy as np
import jax
import jax.numpy as jnp
from jax import lax
from jax.experimental import pallas as pl
from jax.experimental.pallas import tpu as pltpu
from jax.experimental.pallas import tpu_sc as plsc

D_MODEL = 1024
MEM_LEN = 256
DEPTH = 1
DN_ALPHA = (2 * DEPTH) ** 0.25
LN_EPS = 1e-5
NEG = -1e30
FORCE = 1e9

RET_HEADS = 4
RET_DIM = 128
RET_CHUNK = 128
RET_ROPE_BASE = 10000.0
RET_WIDTH = RET_HEADS * RET_DIM

NSA_HEADS = 8
NSA_KV_GROUPS = 2
NSA_HPG = NSA_HEADS // NSA_KV_GROUPS
NSA_DIM = 64
NSA_WIDTH = NSA_HEADS * NSA_DIM
KV_WIDTH = NSA_KV_GROUPS * NSA_DIM
CMP_LEN = 32
CMP_STRIDE = 16
CMP_HIDDEN = 256
SEL_LEN = 64
SEL_SHIFT = 6
SEL_TOPK = 16
WIN = 512
ROPE_THETA = 500000.0
ROPE_DIMS = NSA_DIM // 4
GATE_LANES = 16
NSA_CHAINS = 2
SUM_ROWS = 16

SC_CORES = 2
SC_SUBCORES = 16
SC_CHUNK = 64

XATT_HEADS = 4
XATT_DIM = D_MODEL // XATT_HEADS

N_EXPERTS = 64
TOP_K = 8
N_GROUPS = 8
TOPK_GROUPS = 4
EXPERT_FF = 256
SHARED_FF = 256
ROUTED_SCALE = 2.5

LANES = 128
VMEM_LIMIT = 56 * 1024 * 1024

F32 = jnp.float32
BF16 = jnp.bfloat16
NT_DIMS = (((1,), (1,)), ((), ()))


def _params(n_axes):
    return pltpu.CompilerParams(dimension_semantics=("arbitrary",) * n_axes,
                                vmem_limit_bytes=VMEM_LIMIT)


def _dot(a, b):
    return jnp.dot(a, b, preferred_element_type=F32)


def _dot_nt(a, b):
    return lax.dot_general(a, b, NT_DIMS, preferred_element_type=F32)


def _layer_norm(v, g, b):
    mu = jnp.mean(v, axis=-1, keepdims=True)
    d = v - mu
    var = jnp.mean(d * d, axis=-1, keepdims=True)
    return d * lax.rsqrt(var + LN_EPS) * g + b


def _inproj_kernel(x_ref, pos_ref, wret_ref, wnq_ref, wkv_ref, wg_ref, invr_ref, invn_ref,
                   rq_ref, rk_ref, rv_ref, rg_ref, nq_ref, nqr_ref, kc_ref, vc_ref,
                   ks_ref, vs_ref, kw_ref, vw_ref, gate_ref):
    xb = x_ref[...].astype(BF16)
    pos = pos_ref[...]
    lane = lax.broadcasted_iota(jnp.int32, (1, LANES), 1)

    ang = pos * invr_ref[...]
    cos_r = jnp.cos(ang)
    sin_r = jnp.sin(ang)
    sin_r = jnp.where(lane < RET_DIM // 2, -sin_r, sin_r)
    q_all = _dot(xb, wret_ref[:, :RET_WIDTH])
    k_all = _dot(xb, wret_ref[:, RET_WIDTH:2 * RET_WIDTH])
    for h in range(RET_HEADS):
        cols = slice(h * RET_DIM, (h + 1) * RET_DIM)
        q = q_all[:, cols]
        rq_ref[:, cols] = (q * cos_r + pltpu.roll(q, RET_DIM // 2, 1) * sin_r).astype(BF16)
        k = k_all[:, cols]
        k = (k * cos_r + pltpu.roll(k, RET_DIM // 2, 1) * sin_r) * (RET_DIM ** -0.5)
        rk_ref[:, cols] = k.astype(BF16)
    rv_ref[...] = _dot(xb, wret_ref[:, 2 * RET_WIDTH:3 * RET_WIDTH]).astype(BF16)
    rg_ref[...] = _dot(xb, wret_ref[:, 3 * RET_WIDTH:4 * RET_WIDTH]).astype(BF16)

    half = ROPE_DIMS // 2
    j = lane % NSA_DIM
    angn = pos * invn_ref[...]
    cos_n = jnp.cos(angn)
    sin_n = jnp.sin(angn)
    sin_lo = jnp.where(j < half, -sin_n, 0.0)
    sin_hi = jnp.where((j >= half) & (j < 2 * half), sin_n, 0.0)

    def rope_n(v):
        return v * cos_n + pltpu.roll(v, half, 1) * sin_hi + pltpu.roll(v, LANES - half, 1) * sin_lo

    scale = NSA_DIM ** -0.5
    nq_all = _dot(xb, wnq_ref[...])
    for c in range(NSA_WIDTH // LANES):
        cols = slice(c * LANES, (c + 1) * LANES)
        q = nq_all[:, cols]
        nq_ref[:, cols] = (q * scale).astype(BF16)
        nqr_ref[:, cols] = (rope_n(q) * scale).astype(BF16)

    kv_all = _dot(xb, wkv_ref[...])

    def kv(i):
        return kv_all[:, i * KV_WIDTH:(i + 1) * KV_WIDTH]

    def split_groups(ref, v):
        for g in range(NSA_KV_GROUPS):
            ref[g] = v[:, g * NSA_DIM:(g + 1) * NSA_DIM].astype(BF16)

    kc_ref[...] = kv(0)
    vc_ref[...] = kv(1)
    split_groups(ks_ref, rope_n(kv(2)))
    split_groups(vs_ref, kv(3))
    split_groups(kw_ref, rope_n(kv(4)))
    split_groups(vw_ref, kv(5))

    gt = jax.nn.sigmoid(_dot_nt(wg_ref[...], xb))
    for g in range(NSA_KV_GROUPS):
        gate_ref[g] = gt[g * GATE_LANES:(g + 1) * GATE_LANES, :]


def _inproj(x2d, pos_col, w_in):
    n = x2d.shape[0]
    tm = 512 if n % 512 == 0 else n
    off = np.cumsum([0] + [RET_WIDTH] * 4 + [NSA_WIDTH] + [KV_WIDTH] * 6)
    w_ret = w_in[:, :off[4]].astype(BF16)
    w_nq = w_in[:, off[4]:off[5]].astype(BF16)
    w_kv = w_in[:, off[5]:off[11]].astype(BF16)
    wg = w_in[:, off[11]:].reshape(D_MODEL, NSA_KV_GROUPS, NSA_HPG * 3)
    wg = jnp.pad(wg, ((0, 0), (0, 0), (0, GATE_LANES - NSA_HPG * 3)))
    wg = wg.reshape(D_MODEL, NSA_KV_GROUPS * GATE_LANES).T.astype(BF16)

    lane = np.arange(LANES)
    half_r = RET_DIM // 2
    inv_r = (np.float32(RET_ROPE_BASE) ** (-np.arange(half_r, dtype=np.float32) / np.float32(half_r)))
    inv_r = inv_r.astype(np.float32)[lane % half_r][None, :]
    half_n = ROPE_DIMS // 2
    inv_n = (np.float32(ROPE_THETA) ** (-np.arange(half_n, dtype=np.float32) / np.float32(half_n)))
    jn = lane % NSA_DIM
    inv_n = np.where(jn < ROPE_DIMS, inv_n.astype(np.float32)[jn % half_n], np.float32(0.0))[None, :]

    row = lambda w: pl.BlockSpec((tm, w), lambda i: (i, 0))
    full = lambda a: pl.BlockSpec(a.shape, lambda i: (0,) * a.ndim)
    grp = lambda w: pl.BlockSpec((NSA_KV_GROUPS, tm, w), lambda i: (0, i, 0))
    bf = lambda w: jax.ShapeDtypeStruct((n, w), BF16)
    gbf = jax.ShapeDtypeStruct((NSA_KV_GROUPS, n, NSA_DIM), BF16)
    inv_r = jnp.asarray(inv_r, F32)
    inv_n = jnp.asarray(inv_n, F32)
    return pl.pallas_call(
        _inproj_kernel,
        grid=(n // tm,),
        in_specs=[row(D_MODEL), row(1), full(w_ret), full(w_nq), full(w_kv), full(wg),
                  full(inv_r), full(inv_n)],
        out_specs=[row(RET_WIDTH)] * 4 + [row(NSA_WIDTH)] * 2 + [row(KV_WIDTH)] * 2
                  + [grp(NSA_DIM)] * 4
                  + [pl.BlockSpec((NSA_KV_GROUPS, GATE_LANES, tm), lambda i: (0, 0, i))],
        out_shape=[bf(RET_WIDTH)] * 4 + [bf(NSA_WIDTH)] * 2
                  + [jax.ShapeDtypeStruct((n, KV_WIDTH), F32)] * 2 + [gbf] * 4
                  + [jax.ShapeDtypeStruct((NSA_KV_GROUPS, GATE_LANES, n), F32)],
        compiler_params=_params(1),
        name="inproj",
    )(x2d, pos_col, w_ret, w_nq, w_kv, wg, inv_r, inv_n)


def _retention_kernel(q_ref, k_ref, v_ref, g_ref, o_ref, state_ref):
    c = RET_CHUNK

    @pl.when(pl.program_id(1) == 0)
    def _():
        state_ref[...] = jnp.zeros_like(state_ref)

    row = lax.broadcasted_iota(jnp.int32, (c, c), 0)
    col = lax.broadcasted_iota(jnp.int32, (c, c), 1)
    rel = (row - col).astype(F32)
    idx = lax.broadcasted_iota(jnp.int32, (c, 1), 0).astype(F32)
    for h in range(RET_HEADS):
        log_g = float(np.log(np.float32(1.0) - np.float32(2.0) ** np.float32(-5.0 - h)))
        cols = slice(h * RET_DIM, (h + 1) * RET_DIM)
        q = q_ref[:, cols]
        k = k_ref[:, cols]
        v = v_ref[:, cols]
        dmask = jnp.where(rel >= 0, jnp.exp(log_g * jnp.maximum(rel, 0.0)), 0.0)
        scores = _dot_nt(q, k) * dmask
        inner = _dot(scores.astype(BF16), v)
        zeta = jnp.exp(log_g * (c - 1.0 - idx))
        xi = jnp.exp(log_g * (idx + 1.0))
        prev = state_ref[h]
        cross = _dot(q, prev.astype(BF16)) * xi
        kz = (k.astype(F32) * zeta).astype(BF16)
        kv = lax.dot_general(kz, v, (((0,), (0,)), ((), ())), preferred_element_type=F32)
        state_ref[h] = prev * float(np.exp(np.float32(log_g) * np.float32(c))) + kv
        o = inner + cross
        mu = jnp.mean(o, axis=-1, keepdims=True)
        d = o - mu
        var = jnp.mean(d * d, axis=-1, keepdims=True)
        o = d * lax.rsqrt(var + LN_EPS)
        o_ref[:, cols] = (jax.nn.silu(g_ref[:, cols].astype(F32)) * o).astype(BF16)


def _retention(rq, rk, rv, rg, batch, seq):
    nc = seq // RET_CHUNK
    spec = pl.BlockSpec((RET_CHUNK, RET_WIDTH), lambda b, n: (b * nc + n, 0))
    return pl.pallas_call(
        _retention_kernel,
        grid=(batch, nc),
        in_specs=[spec] * 4,
        out_specs=spec,
        out_shape=jax.ShapeDtypeStruct(rq.shape, BF16),
        scratch_shapes=[pltpu.VMEM((RET_HEADS, RET_DIM, RET_DIM), F32)],
        compiler_params=_params(2),
        name="retention",
    )(rq, rk, rv, rg)


def _compress_kernel(a_ref, pe_ref, w1_ref, w2_ref, o_ref, shift_ref, *, n_cmp):
    rows = a_ref.shape[0]
    a = a_ref[...]
    lo = (a + pe_ref[0]).astype(BF16)
    hi = (a + pe_ref[1]).astype(BF16)
    ridx = lax.broadcasted_iota(jnp.int32, (rows, 1), 0)
    shift_ref[rows:rows + 8, :] = jnp.zeros((8, CMP_HIDDEN), F32)
    for g in range(NSA_KV_GROUPS):
        p = _dot(lo, w1_ref[0, g])
        shift_ref[0:rows, :] = _dot(hi, w1_ref[1, g])
        hid = jax.nn.silu(p + shift_ref[pl.ds(1, rows), :])
        out = _dot(hid.astype(BF16), w2_ref[...])
        o_ref[g] = jnp.where(ridx < n_cmp, out, 0.0).astype(BF16)


def _compress(a, pe, w1, w2, batch, seq):
    rows = seq // CMP_STRIDE
    per = CMP_STRIDE * KV_WIDTH
    n_cmp = (seq - CMP_LEN) // CMP_STRIDE + 1
    a2 = a.reshape(batch * rows, per)
    pe2 = jnp.tile(pe.reshape(2, CMP_STRIDE, 1, NSA_DIM), (1, 1, NSA_KV_GROUPS, 1)).reshape(2, 1, per)
    w1r = w1.reshape(2, CMP_STRIDE, 1, NSA_DIM, CMP_HIDDEN)
    eye = jnp.eye(NSA_KV_GROUPS, dtype=w1.dtype).reshape(1, NSA_KV_GROUPS, 1, NSA_KV_GROUPS, 1, 1)
    w1x = (w1r[:, None] * eye).reshape(2, NSA_KV_GROUPS, per, CMP_HIDDEN).astype(BF16)
    w2b = w2.astype(BF16)
    full = lambda arr: pl.BlockSpec(arr.shape, lambda b: (0,) * arr.ndim)
    return pl.pallas_call(
        functools.partial(_compress_kernel, n_cmp=n_cmp),
        grid=(batch,),
        in_specs=[pl.BlockSpec((rows, per), lambda b: (b, 0)), full(pe2), full(w1x), full(w2b)],
        out_specs=pl.BlockSpec((None, NSA_KV_GROUPS, rows, NSA_DIM), lambda b: (b, 0, 0, 0)),
        out_shape=jax.ShapeDtypeStruct((batch, NSA_KV_GROUPS, rows, NSA_DIM), BF16),
        scratch_shapes=[pltpu.VMEM((rows + 8, CMP_HIDDEN), F32)],
        compiler_params=_params(1),
        name="compress",
    )(a2, pe2, w1x, w2b)


def _heads_to_lanes(ref):
    vt = ref[...].astype(F32).T
    return jnp.concatenate([vt[h * NSA_DIM:(h + 1) * NSA_DIM] for h in range(NSA_HPG)], axis=1).astype(BF16)


def _tile_heads(v):
    return jnp.concatenate([v] * NSA_HPG, axis=1)


def _transpose_into(dst_ref, src_ref, chunk):
    def step(c, _):
        c0 = pl.multiple_of(c * chunk, chunk)
        dst_ref[:NSA_DIM, pl.ds(c0, chunk)] = src_ref[pl.ds(c0, chunk), :].astype(F32).T.astype(BF16)
        return 0
    lax.fori_loop(0, src_ref.shape[0] // chunk, step, 0)


def _nsa_kernel(qraw_ref, qrot_ref, gate_ref, kcmp_ref, vcmp_ref, ovt_ref,
                ks_ref, vs_ref, kw_ref, vw_ref, o_ref, vst_ref, vwt_ref, vct_ref, bias_ref, *, tq, tk, seq):
    i = pl.program_id(2)
    t0 = i * tq
    cols = NSA_HPG * tq
    n_sel = seq // SEL_LEN
    n_cmp_rows = seq // CMP_STRIDE

    @pl.when(i == 0)
    def _():
        chunk = min(512, n_cmp_rows)
        _transpose_into(vst_ref, vs_ref, chunk)
        _transpose_into(vwt_ref, vw_ref, chunk)
        _transpose_into(vct_ref, vcmp_ref, chunk)
        vst_ref[NSA_DIM:, :] = jnp.ones((SUM_ROWS, seq), BF16)
        vwt_ref[NSA_DIM:, :] = jnp.ones((SUM_ROWS, seq), BF16)

    def split_sum(acc):
        return acc[:NSA_DIM] / acc[NSA_DIM:NSA_DIM + 1]

    q_raw = _heads_to_lanes(qraw_ref)
    q_rot = _heads_to_lanes(qrot_ref)
    t_row = t0 + lax.broadcasted_iota(jnp.int32, (1, tq), 1)

    chain_w = cols // NSA_CHAINS
    heads_per_chain = chain_w // tq
    chains = [slice(c * chain_w, (c + 1) * chain_w) for c in range(NSA_CHAINS)]
    tile_chain = lambda v: jnp.concatenate([v] * heads_per_chain, axis=1)

    span = WIN + tq
    ws = pl.multiple_of(jnp.maximum(t0 - WIN, 0), tq)
    dist = t_row - (ws + lax.broadcasted_iota(jnp.int32, (span, 1), 0))
    bias_w = tile_chain(jnp.where((dist >= 0) & (dist < WIN), 0.0, NEG))
    k_w = kw_ref[pl.ds(ws, span), :]
    v_w = vwt_ref[:, pl.ds(ws, span)]
    o_w = []
    for c in chains:
        s_w = _dot(k_w, q_rot[:, c]) + bias_w
        p_w = jnp.exp((s_w - jnp.max(s_w, axis=0, keepdims=True)).astype(BF16))
        o_w.append(split_sum(_dot(v_w, p_w)))
    o_w = jnp.concatenate(o_w, axis=1)

    c_idx = lax.broadcasted_iota(jnp.int32, (n_cmp_rows, 1), 0)
    valid = tile_chain(jnp.where(c_idx * CMP_STRIDE + (CMP_LEN - 1) <= t_row, 1.0, 0.0))
    bias_c = (valid - 1.0) * (-NEG)
    o_c = []
    p_sum = None
    for c in chains:
        s_c = _dot(kcmp_ref[...], q_raw[:, c]) + bias_c
        e_c = jnp.exp(s_c - jnp.max(s_c, axis=0, keepdims=True)) * valid
        l_c = jnp.sum(e_c, axis=0, keepdims=True)
        p_c = e_c / jnp.where(l_c > 0.0, l_c, 1.0)
        o_c.append(_dot(vct_ref[...], p_c.astype(BF16)))
        for h in range(heads_per_chain):
            p_h = p_c[:, h * tq:(h + 1) * tq]
            p_sum = p_h if p_sum is None else p_sum + p_h
    o_c = jnp.concatenate(o_c, axis=1)

    p_hi = p_sum.astype(BF16)
    p_lo = (p_sum - p_hi.astype(F32)).astype(BF16)
    ovt = ovt_ref[...]
    imp = _dot(ovt, p_hi) + _dot(ovt, p_lo)
    jb = lax.broadcasted_iota(jnp.int32, (n_sel, tq), 0)
    cur = (t0 + lax.broadcasted_iota(jnp.int32, (n_sel, tq), 1)) >> SEL_SHIFT
    forced = (jb == 0) | (jb == cur) | (jb == cur - 1)
    work = jnp.where(forced, FORCE, imp)
    work = jnp.where(jb <= cur, work, NEG)
    sel_t = jnp.zeros((n_sel, tq), F32)
    for _ in range(min(SEL_TOPK, n_sel)):
        best = jnp.max(work, axis=0, keepdims=True)
        first = jnp.min(jnp.where(work == best, jb, n_sel), axis=0, keepdims=True)
        hit = jb == first
        sel_t = jnp.where(hit, 1.0, sel_t)
        work = jnp.where(hit, -jnp.inf, work)
    bias_ref[...] = jnp.where(sel_t > 0.5, 0.0, NEG)

    blocks_per_tile = tk // SEL_LEN

    def sel_tile(kt, carry, causal):
        k0 = pl.multiple_of(kt * tk, tk)
        bias = jnp.concatenate(
            [jnp.broadcast_to(bias_ref[pl.ds(kt * blocks_per_tile + j, 1), :], (SEL_LEN, tq))
             for j in range(blocks_per_tile)], axis=0)
        if causal:
            kpos = k0 + lax.broadcasted_iota(jnp.int32, (tk, 1), 0)
            bias = jnp.where(kpos <= t_row, bias, NEG)
        bias = tile_chain(bias)
        k_t = ks_ref[pl.ds(k0, tk), :]
        v_t = vst_ref[:, pl.ds(k0, tk)]
        out = []
        scores = [_dot(k_t, q_rot[:, c]) + bias for c in chains]
        for (m, acc), s in zip(carry, scores):
            m_new = jnp.maximum(m, jnp.max(s, axis=0, keepdims=True))
            p = jnp.exp((s - m_new).astype(BF16))
            acc = jnp.exp(m - m_new) * acc + _dot(v_t, p)
            out.append((m_new, acc))
        return tuple(out)

    n_full = t0 // tk
    init = tuple((jnp.full((1, chain_w), NEG, F32), jnp.zeros((NSA_DIM + SUM_ROWS, chain_w), F32))
                 for _ in chains)
    carry = lax.fori_loop(0, n_full, functools.partial(sel_tile, causal=False), init)
    carry = sel_tile(n_full, carry, causal=True)
    o_s = jnp.concatenate([split_sum(acc) for _, acc in carry], axis=1)

    gt = gate_ref[...]
    outs = []
    for h in range(NSA_HPG):
        c = slice(h * tq, (h + 1) * tq)
        outs.append(gt[3 * h:3 * h + 1] * o_c[:, c] + gt[3 * h + 1:3 * h + 2] * o_s[:, c]
                    + gt[3 * h + 2:3 * h + 3] * o_w[:, c])
    o_ref[...] = jnp.concatenate(outs, axis=0).T.astype(BF16)


def _nsa(nq, nqr, gates, kcmp, vcmp, ks, vs, kw, vw, batch, seq):
    n = batch * seq
    tq = 512
    tk = 512 if seq % 512 == 0 else seq
    nqb = seq // tq
    n_sel = seq // SEL_LEN
    rows_c = seq // CMP_STRIDE
    gw = NSA_HPG * NSA_DIM
    cs = np.arange(rows_c)[None, :] * CMP_STRIDE
    ss = np.arange(n_sel)[:, None] * SEL_LEN
    n_cmp = (seq - CMP_LEN) // CMP_STRIDE + 1
    ovt = ((cs < ss + SEL_LEN) & (cs + CMP_LEN > ss) & (np.arange(rows_c)[None, :] < n_cmp))
    ovt = jnp.asarray(ovt.astype(np.float32), BF16)

    qspec = pl.BlockSpec((tq, gw), lambda b, g, i: (b * nqb + i, g))
    cspec = pl.BlockSpec((None, None, rows_c, NSA_DIM), lambda b, g, i: (b, g, 0, 0))
    kspec = pl.BlockSpec((None, seq, NSA_DIM), lambda b, g, i: (g, b, 0))
    return pl.pallas_call(
        functools.partial(_nsa_kernel, tq=tq, tk=tk, seq=seq),
        grid=(batch, NSA_KV_GROUPS, nqb),
        in_specs=[qspec, qspec,
                  pl.BlockSpec((None, GATE_LANES, tq), lambda b, g, i: (g, 0, b * nqb + i)),
                  cspec, cspec, pl.BlockSpec(ovt.shape, lambda b, g, i: (0, 0)),
                  kspec, kspec, kspec, kspec],
        out_specs=qspec,
        out_shape=jax.ShapeDtypeStruct((n, NSA_WIDTH), BF16),
        scratch_shapes=[pltpu.VMEM((NSA_DIM + SUM_ROWS, seq), BF16), pltpu.VMEM((NSA_DIM + SUM_ROWS, seq), BF16),
                        pltpu.VMEM((NSA_DIM, rows_c), BF16), pltpu.VMEM((n_sel, tq), F32)],
        compiler_params=_params(3),
        name="nsa",
    )(nq, nqr, gates, kcmp, vcmp, ovt, ks, vs, kw, vw)


def _memkv_kernel(mem_ref, w_ref, kv_ref):
    kv_ref[...] = _dot(mem_ref[...].astype(BF16), w_ref[...]).astype(BF16)


def _memkv(mem2d, w_xkv):
    n = mem2d.shape[0]
    w = w_xkv.astype(BF16)
    return pl.pallas_call(
        _memkv_kernel,
        grid=(n // MEM_LEN,),
        in_specs=[pl.BlockSpec((MEM_LEN, D_MODEL), lambda i: (i, 0)),
                  pl.BlockSpec(w.shape, lambda i: (0, 0))],
        out_specs=pl.BlockSpec((MEM_LEN, 2 * D_MODEL), lambda i: (i, 0)),
        out_shape=jax.ShapeDtypeStruct((n, 2 * D_MODEL), BF16),
        compiler_params=_params(1),
        name="memkv",
    )(mem2d, w)


def _pack_halves(v):
    half = D_MODEL // 2
    hi = pltpu.bitcast(v[:, :half].astype(BF16).astype(F32), jnp.uint32)
    lo = pltpu.bitcast(v[:, half:].astype(BF16).astype(F32), jnp.uint32)
    return hi | (lo >> 16)


def _unpack_halves(words):
    return pltpu.bitcast(words & jnp.uint32(0xFFFF0000), F32), pltpu.bitcast(words << 16, F32)


def _postmix_kernel(x_ref, oret_ref, onsa_ref, kv_ref, wout_ref, wq_ref, wo_ref,
                    g1_ref, b1_ref, g2_ref, b2_ref, x2_ref, x2p_ref):
    mixed = jnp.concatenate([oret_ref[...], onsa_ref[...]], axis=1)
    x1 = _layer_norm(DN_ALPHA * x_ref[...] + _dot(mixed, wout_ref[...]), g1_ref[...], b1_ref[...])
    q = (_dot(x1.astype(BF16), wq_ref[...]) * (XATT_DIM ** -0.5)).astype(BF16)
    heads = []
    for h in range(XATT_HEADS):
        cols = slice(h * XATT_DIM, (h + 1) * XATT_DIM)
        s = _dot_nt(q[:, cols], kv_ref[:, cols])
        m = jnp.max(s, axis=-1, keepdims=True)
        p = jnp.exp(s - m)
        l = jnp.sum(p, axis=-1, keepdims=True)
        heads.append(_dot(p.astype(BF16), kv_ref[:, D_MODEL + h * XATT_DIM:D_MODEL + (h + 1) * XATT_DIM]) / l)
    att = jnp.concatenate(heads, axis=1).astype(BF16)
    x2 = _layer_norm(DN_ALPHA * x1 + _dot(att, wo_ref[...]), g2_ref[...], b2_ref[...])
    x2_ref[...] = x2
    x2p_ref[...] = _pack_halves(x2)


def _postmix(x2d, o_ret, o_nsa, kvx, w_out, w_xq, w_xo, ln1_g, ln1_b, ln2_g, ln2_b, batch, seq):
    n = x2d.shape[0]
    tm = 256 if seq % 256 == 0 else seq
    per_b = seq // tm
    row = lambda w: pl.BlockSpec((tm, w), lambda b, i: (b * per_b + i, 0))
    full = lambda a: pl.BlockSpec(a.shape, lambda b, i: (0,) * a.ndim)
    ws = [w_out.astype(BF16), w_xq.astype(BF16), w_xo.astype(BF16)]
    vecs = [v.reshape(1, D_MODEL) for v in (ln1_g, ln1_b, ln2_g, ln2_b)]
    return pl.pallas_call(
        _postmix_kernel,
        grid=(batch, per_b),
        in_specs=[row(D_MODEL), row(RET_WIDTH), row(NSA_WIDTH),
                  pl.BlockSpec((MEM_LEN, 2 * D_MODEL), lambda b, i: (b, 0))]
                 + [full(w) for w in ws] + [full(v) for v in vecs],
        out_specs=[row(D_MODEL),
                   row(D_MODEL // 2)],
        out_shape=[jax.ShapeDtypeStruct((n, D_MODEL), F32),
                   jax.ShapeDtypeStruct((n, D_MODEL // 2), jnp.uint32)],
        compiler_params=_params(2),
        name="postmix",
    )(x2d, o_ret, o_nsa, kvx, *ws, *vecs)


def _router_kernel(x_ref, wr_ref, bias_ref, e_ref, rank_ref, w_ref, cnt_ref, cntrow_ref, carry_ref, carryrow_ref):
    tn = x_ref.shape[0]
    per = N_EXPERTS // N_GROUPS

    @pl.when(pl.program_id(0) == 0)
    def _():
        carry_ref[...] = jnp.zeros_like(carry_ref)
        carryrow_ref[...] = jnp.zeros_like(carryrow_ref)

    logits = _dot_nt(wr_ref[...], x_ref[...].astype(BF16))
    scores = jax.nn.sigmoid(logits)
    biased = scores + bias_ref[...]
    b3 = biased.reshape(N_GROUPS, per, tn)
    member = lax.broadcasted_iota(jnp.int32, (N_GROUPS, per, tn), 1)
    top1 = jnp.max(b3, axis=1, keepdims=True)
    first1 = jnp.min(jnp.where(b3 == top1, member, per), axis=1, keepdims=True)
    top2 = jnp.max(jnp.where(member == first1, -jnp.inf, b3), axis=1, keepdims=True)
    gscore = top1 + top2
    gidx = lax.broadcasted_iota(jnp.int32, (N_GROUPS, 1, tn), 0)
    gwork = gscore
    for _ in range(TOPK_GROUPS - 1):
        gbest = jnp.max(gwork, axis=0, keepdims=True)
        gfirst = jnp.min(jnp.where(gwork == gbest, gidx, N_GROUPS), axis=0, keepdims=True)
        gwork = jnp.where(gidx == gfirst, -jnp.inf, gwork)
    kth = jnp.max(gwork, axis=0, keepdims=True)
    work = jnp.where(gscore >= kth, b3, NEG).reshape(N_EXPERTS, tn)
    eidx = lax.broadcasted_iota(jnp.int32, (N_EXPERTS, tn), 0)
    picks = []
    chosen = jnp.zeros((N_EXPERTS, tn), F32)
    for _ in range(TOP_K):
        best = jnp.max(work, axis=0, keepdims=True)
        first = jnp.min(jnp.where(work == best, eidx, N_EXPERTS), axis=0, keepdims=True)
        hit = eidx == first
        picks.append((first, hit))
        chosen = jnp.where(hit, 1.0, chosen)
        work = jnp.where(hit, -jnp.inf, work)

    r_i = lax.broadcasted_iota(jnp.int32, (tn, tn), 0)
    c_i = lax.broadcasted_iota(jnp.int32, (tn, tn), 1)
    before = jnp.where(r_i < c_i, 1.0, 0.0).astype(BF16)
    chosen_b = chosen.astype(BF16)
    rank = _dot(chosen_b, before) + carry_ref[...]
    carry_ref[...] = carry_ref[...] + jnp.sum(chosen, axis=1, keepdims=True)
    carryrow_ref[...] = carryrow_ref[...] + _dot_nt(jnp.ones((8, tn), BF16), chosen_b)
    cnt_ref[...] = carry_ref[...]
    cntrow_ref[...] = carryrow_ref[...]

    wsel = [jnp.sum(jnp.where(hit, scores, 0.0), axis=0, keepdims=True) for _, hit in picks]
    wsum = wsel[0]
    for v in wsel[1:]:
        wsum = wsum + v
    for kk, (first, hit) in enumerate(picks):
        e_ref[kk:kk + 1, :] = first
        rank_ref[kk:kk + 1, :] = jnp.sum(jnp.where(hit, rank, 0.0), axis=0, keepdims=True).astype(jnp.int32)
        w_ref[kk:kk + 1, :] = wsel[kk] / wsum * ROUTED_SCALE


def _router(x2, w_router, router_bias):
    n = x2.shape[0]
    tn = 512 if n % 512 == 0 else n
    wr_t = w_router.T.astype(BF16)
    bias = router_bias.reshape(N_EXPERTS, 1).astype(F32)
    kspec = pl.BlockSpec((TOP_K, tn), lambda i: (0, i))
    return pl.pallas_call(
        _router_kernel,
        grid=(n // tn,),
        in_specs=[pl.BlockSpec((tn, D_MODEL), lambda i: (i, 0)),
                  pl.BlockSpec(wr_t.shape, lambda i: (0, 0)),
                  pl.BlockSpec(bias.shape, lambda i: (0, 0))],
        out_specs=[kspec, kspec, kspec, pl.BlockSpec((N_EXPERTS, 1), lambda i: (0, 0)),
                   pl.BlockSpec((8, N_EXPERTS), lambda i: (0, 0))],
        out_shape=[jax.ShapeDtypeStruct((TOP_K, n), jnp.int32),
                   jax.ShapeDtypeStruct((TOP_K, n), jnp.int32),
                   jax.ShapeDtypeStruct((TOP_K, n), F32),
                   jax.ShapeDtypeStruct((N_EXPERTS, 1), F32),
                   jax.ShapeDtypeStruct((8, N_EXPERTS), F32)],
        scratch_shapes=[pltpu.VMEM((N_EXPERTS, 1), F32), pltpu.VMEM((8, N_EXPERTS), F32)],
        compiler_params=_params(1),
        name="router",
    )(x2, wr_t, bias)


def _slots_kernel(e_ref, rank_ref, cnt_ref, cntrow_ref, dest_ref, blk_e_ref, valid_ref, *, blk, n_blocks):
    pad = lambda c: jnp.ceil(c / blk) * blk
    cnt = cnt_ref[...]
    padded = pad(cnt)
    padded_row = pad(cntrow_ref[0:1, :])
    r_i = lax.broadcasted_iota(jnp.int32, (N_EXPERTS, N_EXPERTS), 0)
    c_i = lax.broadcasted_iota(jnp.int32, (N_EXPERTS, N_EXPERTS), 1)
    start = jnp.sum(jnp.where(c_i < r_i, padded_row, 0.0), axis=1, keepdims=True)
    end = start + padded
    e = e_ref[...]
    dest = rank_ref[...]
    for ex in range(N_EXPERTS):
        dest = dest + jnp.where(e == ex, start[ex:ex + 1, :].astype(jnp.int32), 0)
    dest_ref[...] = dest
    bstart = (lax.broadcasted_iota(jnp.int32, (1, n_blocks), 1) * blk).astype(F32)
    owner = jnp.sum(jnp.where(end <= bstart, 1.0, 0.0), axis=0, keepdims=True)
    blk_e_ref[...] = jnp.minimum(owner, N_EXPERTS - 1.0).astype(jnp.int32)
    inside = (start <= bstart) & (bstart < end)
    real = jnp.clip(start + cnt - bstart, 0.0, float(blk))
    valid_ref[...] = jnp.sum(jnp.where(inside, real, 0.0), axis=0, keepdims=True).astype(jnp.int32)


def _slots(e_k, rank_k, counts, counts_row, blk, n_blocks):
    n = e_k.shape[1]
    full = lambda shape: pl.BlockSpec(shape, lambda: (0,) * len(shape))
    return pl.pallas_call(
        functools.partial(_slots_kernel, blk=blk, n_blocks=n_blocks),
        in_specs=[full((TOP_K, n)), full((TOP_K, n)), full((N_EXPERTS, 1)), full((8, N_EXPERTS))],
        out_specs=[full((TOP_K, n)), full((1, n_blocks)), full((1, n_blocks))],
        out_shape=[jax.ShapeDtypeStruct((TOP_K, n), jnp.int32),
                   jax.ShapeDtypeStruct((1, n_blocks), jnp.int32),
                   jax.ShapeDtypeStruct((1, n_blocks), jnp.int32)],
        compiler_params=pltpu.CompilerParams(vmem_limit_bytes=VMEM_LIMIT),
        name="slots",
    )(e_k, rank_k, counts, counts_row)


def _sc_worker_base(per_worker):
    return (lax.axis_index("s") * SC_CORES + lax.axis_index("c")) * per_worker


def _sc_scatter_rows(rows, idx, n_out):
    n, width = rows.shape
    k_lists = idx.shape[0] // n
    workers = SC_CORES * SC_SUBCORES
    per_worker = n // workers
    assert per_worker * workers == n and per_worker % SC_CHUNK == 0
    mesh = plsc.VectorSubcoreMesh(core_axis_name="c", subcore_axis_name="s")

    @functools.partial(
        pl.kernel, mesh=mesh,
        out_type=jax.ShapeDtypeStruct((n_out, width), rows.dtype),
        scratch_types=[pltpu.VMEM((SC_CHUNK, width), rows.dtype)]
                      + [pltpu.VMEM((SC_CHUNK,), jnp.int32)] * k_lists + [pltpu.SemaphoreType.DMA],
        name="sc_scatter")
    def scatter(rows_hbm, idx_hbm, out_hbm, rows_v, *rest):
        idx_vs, sem = rest[:k_lists], rest[k_lists]
        base = _sc_worker_base(per_worker)

        @pl.loop(0, per_worker // SC_CHUNK)
        def _(ci):
            off = pl.multiple_of(base + ci * SC_CHUNK, SC_CHUNK)
            pltpu.sync_copy(rows_hbm.at[pl.ds(off, SC_CHUNK)], rows_v)
            for k in range(k_lists):
                pltpu.sync_copy(idx_hbm.at[pl.ds(pl.multiple_of(k * n + off, SC_CHUNK), SC_CHUNK)], idx_vs[k])
            copies = [pltpu.async_copy(rows_v, out_hbm.at[idx_vs[k]], sem) for k in range(k_lists)]
            for c in copies:
                c.wait()

    return scatter(rows, idx)


def _experts_kernel(blk_e_ref, valid_ref, xs_ref, wg_ref, wu_ref, wd_ref, y_ref):
    del blk_e_ref
    valid = valid_ref[pl.program_id(0)]

    @pl.when(valid > 0)
    def _():
        half = D_MODEL // 2
        row = lax.broadcasted_iota(jnp.int32, (xs_ref.shape[0], 1), 0)
        hi, lo = (v.astype(BF16) for v in _unpack_halves(jnp.where(row < valid, xs_ref[...], jnp.uint32(0))))
        gate = _dot(hi, wg_ref[:half, :]) + _dot(lo, wg_ref[half:, :])
        up = _dot(hi, wu_ref[:half, :]) + _dot(lo, wu_ref[half:, :])
        y_ref[...] = _pack_halves(_dot((jax.nn.silu(gate) * up).astype(BF16), wd_ref[...]))

    @pl.when(valid <= 0)
    def _():
        y_ref[...] = jnp.zeros_like(y_ref)


def _experts(blk_e, valid, xs, w_gate, w_up, w_down, blk):
    cap, width = xs.shape
    wg, wu, wd = (a.astype(BF16) for a in (w_gate, w_up, w_down))
    wspec = lambda a: pl.BlockSpec((None,) + a.shape[1:], lambda i, be, nv: (be[i], 0, 0))
    rows = pl.BlockSpec((blk, width), lambda i, be, nv: (i, 0))
    return pl.pallas_call(
        _experts_kernel,
        grid_spec=pltpu.PrefetchScalarGridSpec(
            num_scalar_prefetch=2,
            grid=(cap // blk,),
            in_specs=[rows, wspec(wg), wspec(wu), wspec(wd)],
            out_specs=rows,
        ),
        out_shape=jax.ShapeDtypeStruct(xs.shape, xs.dtype),
        compiler_params=_params(1),
        name="experts",
    )(blk_e, valid, xs, wg, wu, wd)


def _sc_gather_rows(table, idx):
    b, width = idx.shape[0], table.shape[1]
    workers = SC_CORES * SC_SUBCORES
    per_worker = b // workers
    assert per_worker * workers == b and per_worker % SC_CHUNK == 0
    mesh = plsc.VectorSubcoreMesh(core_axis_name="c", subcore_axis_name="s")

    @functools.partial(
        pl.kernel, mesh=mesh,
        out_type=jax.ShapeDtypeStruct((b, width), table.dtype),
        scratch_types=[pltpu.VMEM((SC_CHUNK,), jnp.int32), pltpu.VMEM((SC_CHUNK, width), table.dtype),
                       pltpu.SemaphoreType.DMA],
        name="sc_gather")
    def gather(table_hbm, idx_hbm, out_hbm, idx_v, rows_v, sem):
        base = _sc_worker_base(per_worker)

        @pl.loop(0, per_worker // SC_CHUNK)
        def _(ci):
            off = pl.multiple_of(base + ci * SC_CHUNK, SC_CHUNK)
            pltpu.sync_copy(idx_hbm.at[pl.ds(off, SC_CHUNK)], idx_v)
            pltpu.async_copy(table_hbm.at[idx_v], rows_v, sem).wait()
            pltpu.sync_copy(rows_v, out_hbm.at[pl.ds(off, SC_CHUNK)])

    return gather(table, idx)


def _combine_kernel(x_ref, wk_ref, yk_ref, wsg_ref, wsu_ref, wsd_ref, g_ref, b_ref, o_ref):
    x = x_ref[...]
    xb = x.astype(BF16)
    shared = _dot((jax.nn.silu(_dot(xb, wsg_ref[...])) * _dot(xb, wsu_ref[...])).astype(BF16), wsd_ref[...])
    wk = wk_ref[...]
    routed_hi = routed_lo = None
    for kk in range(TOP_K):
        hi, lo = _unpack_halves(yk_ref[kk])
        w = wk[:, kk:kk + 1]
        routed_hi = hi * w if kk == 0 else routed_hi + hi * w
        routed_lo = lo * w if kk == 0 else routed_lo + lo * w
    routed = jnp.concatenate([routed_hi, routed_lo], axis=1)
    o_ref[...] = _layer_norm(DN_ALPHA * x + (routed + shared), g_ref[...], b_ref[...])


def _combine(x2, w_tok, yk, ws_gate, ws_up, ws_down, ln3_g, ln3_b):
    n = x2.shape[0]
    tt = 256 if n % 256 == 0 else n
    ws = [ws_gate.astype(BF16), ws_up.astype(BF16), ws_down.astype(BF16)]
    vecs = [ln3_g.reshape(1, D_MODEL), ln3_b.reshape(1, D_MODEL)]
    full = lambda a: pl.BlockSpec(a.shape, lambda i: (0,) * a.ndim)
    return pl.pallas_call(
        _combine_kernel,
        grid=(n // tt,),
        in_specs=[pl.BlockSpec((tt, D_MODEL), lambda i: (i, 0)),
                  pl.BlockSpec((tt, TOP_K), lambda i: (i, 0)),
                  pl.BlockSpec((TOP_K, tt, D_MODEL // 2), lambda i: (0, i, 0))]
                 + [full(a) for a in ws] + [full(v) for v in vecs],
        out_specs=pl.BlockSpec((tt, D_MODEL), lambda i: (i, 0)),
        out_shape=jax.ShapeDtypeStruct((n, D_MODEL), F32),
        compiler_params=_params(1),
        name="combine",
    )(x2, w_tok, yk, *ws, *vecs)


def _moe_and_norm(x2, x2p, w_router, router_bias, w_gate, w_up, w_down, ws_gate, ws_up, ws_down,
                  ln3_g, ln3_b):
    n = x2.shape[0]
    blk = 512
    cap = n * TOP_K + N_EXPERTS * blk
    n_blocks = cap // blk
    e_k, rank_k, w_k, counts, counts_row = _router(x2, w_router, router_bias)
    dest, blk_e, valid = _slots(e_k, rank_k, counts, counts_row, blk, n_blocks)
    dest = dest.reshape(-1)
    xs = _sc_scatter_rows(x2p, dest, cap)
    y = _experts(blk_e.reshape(-1), valid.reshape(-1), xs, w_gate, w_up, w_down, blk)
    yk = _sc_gather_rows(y, dest).reshape(TOP_K, n, D_MODEL // 2)
    return _combine(x2, w_k.T, yk, ws_gate, ws_up, ws_down, ln3_g, ln3_b)


def _layer(x, mem, positions, w_in, cmp_pe_k, cmp_pe_v, cmp_w1_k, cmp_w2_k, cmp_w1_v, cmp_w2_v,
           w_out, ln1_g, ln1_b, w_xq, w_xkv, w_xo, ln2_g, ln2_b, w_router, router_bias,
           w_gate, w_up, w_down, ws_gate, ws_up, ws_down, ln3_g, ln3_b):
    batch, seq, _ = x.shape
    n = batch * seq
    x2d = x.reshape(n, D_MODEL)
    pos_col = positions.astype(F32).reshape(n, 1)
    (rq, rk, rv, rg, nq, nqr, kc, vc, ks, vs, kw, vw, gates) = _inproj(x2d, pos_col, w_in)
    o_ret = _retention(rq, rk, rv, rg, batch, seq)
    kcmp = _compress(kc, cmp_pe_k, cmp_w1_k, cmp_w2_k, batch, seq)
    vcmp = _compress(vc, cmp_pe_v, cmp_w1_v, cmp_w2_v, batch, seq)
    o_nsa = _nsa(nq, nqr, gates, kcmp, vcmp, ks, vs, kw, vw, batch, seq)
    kvx = _memkv(mem.reshape(batch * MEM_LEN, D_MODEL), w_xkv)
    x2, x2p = _postmix(x2d, o_ret, o_nsa, kvx, w_out, w_xq, w_xo, ln1_g, ln1_b, ln2_g, ln2_b, batch, seq)
    out = _moe_and_norm(x2, x2p, w_router, router_bias, w_gate, w_up, w_down,
                        ws_gate, ws_up, ws_down, ln3_g, ln3_b)
    return out.reshape(batch, seq, D_MODEL)


def kernel(x, mem, positions, w_in, cmp_pe_k, cmp_pe_v, cmp_w1_k, cmp_w2_k, cmp_w1_v, cmp_w2_v, w_out, ln1_g, ln1_b, w_xq, w_xkv, w_xo, ln2_g, ln2_b, w_router, router_bias, w_gate, w_up, w_down, ws_gate, ws_up, ws_down, ln3_g, ln3_b):
    for l in range(DEPTH):
        x = _layer(x, mem, positions, w_in[l], cmp_pe_k[l], cmp_pe_v[l], cmp_w1_k[l], cmp_w2_k[l],
                   cmp_w1_v[l], cmp_w2_v[l], w_out[l], ln1_g[l], ln1_b[l], w_xq[l], w_xkv[l],
                   w_xo[l], ln2_g[l], ln2_b[l], w_router[l], router_bias[l], w_gate[l], w_up[l],
                   w_down[l], ws_gate[l], ws_up[l], ws_down[l], ln3_g[l], ln3_b[l])
    return x
```

```python
import functools

import numpy as np
import jax
import jax.numpy as jnp
from jax import lax
from jax.experimental import pallas as pl
from jax.experimental.pallas import tpu as pltpu
from jax.experimental.pallas import tpu_sc as plsc

D_MODEL = 1024
MEM_LEN = 256
DEPTH = 1
DN_ALPHA = (2 * DEPTH) ** 0.25
LN_EPS = 1e-5
NEG = -1e30
FORCE = 1e9

RET_HEADS = 4
RET_DIM = 128
RET_CHUNK = 128
RET_ROPE_BASE = 10000.0
RET_WIDTH = RET_HEADS * RET_DIM

NSA_HEADS = 8
NSA_KV_GROUPS = 2
NSA_HPG = NSA_HEADS // NSA_KV_GROUPS
NSA_DIM = 64
NSA_WIDTH = NSA_HEADS * NSA_DIM
KV_WIDTH = NSA_KV_GROUPS * NSA_DIM
CMP_LEN = 32
CMP_STRIDE = 16
CMP_HIDDEN = 256
SEL_LEN = 64
SEL_SHIFT = 6
SEL_TOPK = 16
WIN = 512
ROPE_THETA = 500000.0
ROPE_DIMS = NSA_DIM // 4
GATE_LANES = 16
NSA_CHAINS = 2
SUM_ROWS = 16

SC_CORES = 2
SC_SUBCORES = 16
SC_CHUNK = 64
SC_INFLIGHT = 2

XATT_HEADS = 4
XATT_DIM = D_MODEL // XATT_HEADS

N_EXPERTS = 64
TOP_K = 8
N_GROUPS = 8
TOPK_GROUPS = 4
EXPERT_FF = 256
SHARED_FF = 256
ROUTED_SCALE = 2.5

LANES = 128
VMEM_LIMIT = 56 * 1024 * 1024

F32 = jnp.float32
BF16 = jnp.bfloat16
NT_DIMS = (((1,), (1,)), ((), ()))


def _params(n_axes):
    return pltpu.CompilerParams(dimension_semantics=("arbitrary",) * n_axes,
                                vmem_limit_bytes=VMEM_LIMIT)


def _dot(a, b):
    return jnp.dot(a, b, preferred_element_type=F32)


def _dot_nt(a, b):
    return lax.dot_general(a, b, NT_DIMS, preferred_element_type=F32)


def _layer_norm(v, g, b):
    mu = jnp.mean(v, axis=-1, keepdims=True)
    d = v - mu
    var = jnp.mean(d * d, axis=-1, keepdims=True)
    return d * lax.rsqrt(var + LN_EPS) * g + b


def _inproj_kernel(x_ref, pos_ref, wret_ref, wnq_ref, wkv_ref, wg_ref, invr_ref, invn_ref,
                   rq_ref, rk_ref, rv_ref, rg_ref, nq_ref, nqr_ref, kc_ref, vc_ref,
                   ks_ref, vs_ref, kw_ref, vw_ref, gate_ref):
    xb = x_ref[...].astype(BF16)
    pos = pos_ref[...]
    lane = lax.broadcasted_iota(jnp.int32, (1, LANES), 1)

    ang = pos * invr_ref[...]
    cos_r = jnp.cos(ang)
    sin_r = jnp.sin(ang)
    sin_r = jnp.where(lane < RET_DIM // 2, -sin_r, sin_r)
    q_all = _dot(xb, wret_ref[:, :RET_WIDTH])
    k_all = _dot(xb, wret_ref[:, RET_WIDTH:2 * RET_WIDTH])
    for h in range(RET_HEADS):
        cols = slice(h * RET_DIM, (h + 1) * RET_DIM)
        q = q_all[:, cols]
        rq_ref[:, cols] = (q * cos_r + pltpu.roll(q, RET_DIM // 2, 1) * sin_r).astype(BF16)
        k = k_all[:, cols]
        k = (k * cos_r + pltpu.roll(k, RET_DIM // 2, 1) * sin_r) * (RET_DIM ** -0.5)
        rk_ref[:, cols] = k.astype(BF16)
    rv_ref[...] = _dot(xb, wret_ref[:, 2 * RET_WIDTH:3 * RET_WIDTH]).astype(BF16)
    rg_ref[...] = _dot(xb, wret_ref[:, 3 * RET_WIDTH:4 * RET_WIDTH]).astype(BF16)

    half = ROPE_DIMS // 2
    j = lane % NSA_DIM
    angn = pos * invn_ref[...]
    cos_n = jnp.cos(angn)
    sin_n = jnp.sin(angn)
    sin_lo = jnp.where(j < half, -sin_n, 0.0)
    sin_hi = jnp.where((j >= half) & (j < 2 * half), sin_n, 0.0)

    def rope_n(v):
        return v * cos_n + pltpu.roll(v, half, 1) * sin_hi + pltpu.roll(v, LANES - half, 1) * sin_lo

    scale = NSA_DIM ** -0.5
    nq_all = _dot(xb, wnq_ref[...])
    for c in range(NSA_WIDTH // LANES):
        cols = slice(c * LANES, (c + 1) * LANES)
        q = nq_all[:, cols]
        nq_ref[:, cols] = (q * scale).astype(BF16)
        nqr_ref[:, cols] = (rope_n(q) * scale).astype(BF16)

    kv_all = _dot(xb, wkv_ref[...])

    def kv(i):
        return kv_all[:, i * KV_WIDTH:(i + 1) * KV_WIDTH]

    def split_groups(ref, v):
        for g in range(NSA_KV_GROUPS):
            ref[g] = v[:, g * NSA_DIM:(g + 1) * NSA_DIM].astype(BF16)

    kc_ref[...] = kv(0)
    vc_ref[...] = kv(1)
    split_groups(ks_ref, rope_n(kv(2)))
    split_groups(vs_ref, kv(3))
    split_groups(kw_ref, rope_n(kv(4)))
    split_groups(vw_ref, kv(5))

    gt = jax.nn.sigmoid(_dot_nt(wg_ref[...], xb))
    for g in range(NSA_KV_GROUPS):
        gate_ref[g] = gt[g * GATE_LANES:(g + 1) * GATE_LANES, :]


def _inproj(x2d, pos_col, w_in):
    n = x2d.shape[0]
    tm = 512 if n % 512 == 0 else n
    off = np.cumsum([0] + [RET_WIDTH] * 4 + [NSA_WIDTH] + [KV_WIDTH] * 6)
    w_ret = w_in[:, :off[4]].astype(BF16)
    w_nq = w_in[:, off[4]:off[5]].astype(BF16)
    w_kv = w_in[:, off[5]:off[11]].astype(BF16)
    wg = w_in[:, off[11]:].reshape(D_MODEL, NSA_KV_GROUPS, NSA_HPG * 3)
    wg = jnp.pad(wg, ((0, 0), (0, 0), (0, GATE_LANES - NSA_HPG * 3)))
    wg = wg.reshape(D_MODEL, NSA_KV_GROUPS * GATE_LANES).T.astype(BF16)

    lane = np.arange(LANES)
    half_r = RET_DIM // 2
    inv_r = (np.float32(RET_ROPE_BASE) ** (-np.arange(half_r, dtype=np.float32) / np.float32(half_r)))
    inv_r = inv_r.astype(np.float32)[lane % half_r][None, :]
    half_n = ROPE_DIMS // 2
    inv_n = (np.float32(ROPE_THETA) ** (-np.arange(half_n, dtype=np.float32) / np.float32(half_n)))
    jn = lane % NSA_DIM
    inv_n = np.where(jn < ROPE_DIMS, inv_n.astype(np.float32)[jn % half_n], np.float32(0.0))[None, :]

    row = lambda w: pl.BlockSpec((tm, w), lambda i: (i, 0))
    full = lambda a: pl.BlockSpec(a.shape, lambda i: (0,) * a.ndim)
    grp = lambda w: pl.BlockSpec((NSA_KV_GROUPS, tm, w), lambda i: (0, i, 0))
    bf = lambda w: jax.ShapeDtypeStruct((n, w), BF16)
    gbf = jax.ShapeDtypeStruct((NSA_KV_GROUPS, n, NSA_DIM), BF16)
    inv_r = jnp.asarray(inv_r, F32)
    inv_n = jnp.asarray(inv_n, F32)
    return pl.pallas_call(
        _inproj_kernel,
        grid=(n // tm,),
        in_specs=[row(D_MODEL), row(1), full(w_ret), full(w_nq), full(w_kv), full(wg),
                  full(inv_r), full(inv_n)],
        out_specs=[row(RET_WIDTH)] * 4 + [row(NSA_WIDTH)] * 2 + [row(KV_WIDTH)] * 2
                  + [grp(NSA_DIM)] * 4
                  + [pl.BlockSpec((NSA_KV_GROUPS, GATE_LANES, tm), lambda i: (0, 0, i))],
        out_shape=[bf(RET_WIDTH)] * 4 + [bf(NSA_WIDTH)] * 2
                  + [jax.ShapeDtypeStruct((n, KV_WIDTH), F32)] * 2 + [gbf] * 4
                  + [jax.ShapeDtypeStruct((NSA_KV_GROUPS, GATE_LANES, n), F32)],
        compiler_params=_params(1),
        name="inproj",
    )(x2d, pos_col, w_ret, w_nq, w_kv, wg, inv_r, inv_n)


def _retention_kernel(q_ref, k_ref, v_ref, g_ref, o_ref, state_ref):
    c = RET_CHUNK

    @pl.when(pl.program_id(1) == 0)
    def _():
        state_ref[...] = jnp.zeros_like(state_ref)

    row = lax.broadcasted_iota(jnp.int32, (c, c), 0)
    col = lax.broadcasted_iota(jnp.int32, (c, c), 1)
    rel = (row - col).astype(F32)
    idx = lax.broadcasted_iota(jnp.int32, (c, 1), 0).astype(F32)
    for h in range(RET_HEADS):
        log_g = float(np.log(np.float32(1.0) - np.float32(2.0) ** np.float32(-5.0 - h)))
        cols = slice(h * RET_DIM, (h + 1) * RET_DIM)
        q = q_ref[:, cols]
        k = k_ref[:, cols]
        v = v_ref[:, cols]
        dmask = jnp.where(rel >= 0, jnp.exp(log_g * jnp.maximum(rel, 0.0)), 0.0)
        scores = _dot_nt(q, k) * dmask
        inner = _dot(scores.astype(BF16), v)
        zeta = jnp.exp(log_g * (c - 1.0 - idx))
        xi = jnp.exp(log_g * (idx + 1.0))
        prev = state_ref[h]
        cross = _dot(q, prev.astype(BF16)) * xi
        kz = (k.astype(F32) * zeta).astype(BF16)
        kv = lax.dot_general(kz, v, (((0,), (0,)), ((), ())), preferred_element_type=F32)
        state_ref[h] = prev * float(np.exp(np.float32(log_g) * np.float32(c))) + kv
        o = inner + cross
        mu = jnp.mean(o, axis=-1, keepdims=True)
        d = o - mu
        var = jnp.mean(d * d, axis=-1, keepdims=True)
        o = d * lax.rsqrt(var + LN_EPS)
        o_ref[:, cols] = (jax.nn.silu(g_ref[:, cols].astype(F32)) * o).astype(BF16)


def _retention(rq, rk, rv, rg, batch, seq):
    nc = seq // RET_CHUNK
    spec = pl.BlockSpec((RET_CHUNK, RET_WIDTH), lambda b, n: (b * nc + n, 0))
    return pl.pallas_call(
        _retention_kernel,
        grid=(batch, nc),
        in_specs=[spec] * 4,
        out_specs=spec,
        out_shape=jax.ShapeDtypeStruct(rq.shape, BF16),
        scratch_shapes=[pltpu.VMEM((RET_HEADS, RET_DIM, RET_DIM), F32)],
        compiler_params=_params(2),
        name="retention",
    )(rq, rk, rv, rg)


def _compress_kernel(a_ref, pe_ref, w1_ref, w2_ref, o_ref, shift_ref, *, n_cmp):
    rows = a_ref.shape[0]
    a = a_ref[...]
    lo = (a + pe_ref[0]).astype(BF16)
    hi = (a + pe_ref[1]).astype(BF16)
    ridx = lax.broadcasted_iota(jnp.int32, (rows, 1), 0)
    shift_ref[rows:rows + 8, :] = jnp.zeros((8, CMP_HIDDEN), F32)
    for g in range(NSA_KV_GROUPS):
        p = _dot(lo, w1_ref[0, g])
        shift_ref[0:rows, :] = _dot(hi, w1_ref[1, g])
        hid = jax.nn.silu(p + shift_ref[pl.ds(1, rows), :])
        out = _dot(hid.astype(BF16), w2_ref[...])
        o_ref[g] = jnp.where(ridx < n_cmp, out, 0.0).astype(BF16)


def _compress(a, pe, w1, w2, batch, seq):
    rows = seq // CMP_STRIDE
    per = CMP_STRIDE * KV_WIDTH
    n_cmp = (seq - CMP_LEN) // CMP_STRIDE + 1
    a2 = a.reshape(batch * rows, per)
    pe2 = jnp.tile(pe.reshape(2, CMP_STRIDE, 1, NSA_DIM), (1, 1, NSA_KV_GROUPS, 1)).reshape(2, 1, per)
    w1r = w1.reshape(2, CMP_STRIDE, 1, NSA_DIM, CMP_HIDDEN)
    eye = jnp.eye(NSA_KV_GROUPS, dtype=w1.dtype).reshape(1, NSA_KV_GROUPS, 1, NSA_KV_GROUPS, 1, 1)
    w1x = (w1r[:, None] * eye).reshape(2, NSA_KV_GROUPS, per, CMP_HIDDEN).astype(BF16)
    w2b = w2.astype(BF16)
    full = lambda arr: pl.BlockSpec(arr.shape, lambda b: (0,) * arr.ndim)
    return pl.pallas_call(
        functools.partial(_compress_kernel, n_cmp=n_cmp),
        grid=(batch,),
        in_specs=[pl.BlockSpec((rows, per), lambda b: (b, 0)), full(pe2), full(w1x), full(w2b)],
        out_specs=pl.BlockSpec((None, NSA_KV_GROUPS, rows, NSA_DIM), lambda b: (b, 0, 0, 0)),
        out_shape=jax.ShapeDtypeStruct((batch, NSA_KV_GROUPS, rows, NSA_DIM), BF16),
        scratch_shapes=[pltpu.VMEM((rows + 8, CMP_HIDDEN), F32)],
        compiler_params=_params(1),
        name="compress",
    )(a2, pe2, w1x, w2b)


def _heads_to_lanes(ref):
    vt = ref[...].astype(F32).T
    return jnp.concatenate([vt[h * NSA_DIM:(h + 1) * NSA_DIM] for h in range(NSA_HPG)], axis=1).astype(BF16)


def _tile_heads(v):
    return jnp.concatenate([v] * NSA_HPG, axis=1)


def _transpose_into(dst_ref, src_ref, chunk):
    def step(c, _):
        c0 = pl.multiple_of(c * chunk, chunk)
        dst_ref[:NSA_DIM, pl.ds(c0, chunk)] = src_ref[pl.ds(c0, chunk), :].astype(F32).T.astype(BF16)
        return 0
    lax.fori_loop(0, src_ref.shape[0] // chunk, step, 0)


def _nsa_kernel(qraw_ref, qrot_ref, gate_ref, kcmp_ref, vcmp_ref, ovt_ref,
                ks_ref, vs_ref, kw_ref, vw_ref, o_ref, vst_ref, vwt_ref, vct_ref, bias_ref, *, tq, tk, seq):
    i = pl.program_id(2)
    t0 = i * tq
    cols = NSA_HPG * tq
    n_sel = seq // SEL_LEN
    n_cmp_rows = seq // CMP_STRIDE

    @pl.when(i == 0)
    def _():
        chunk = min(512, n_cmp_rows)
        _transpose_into(vst_ref, vs_ref, chunk)
        _transpose_into(vwt_ref, vw_ref, chunk)
        _transpose_into(vct_ref, vcmp_ref, chunk)
        vst_ref[NSA_DIM:, :] = jnp.ones((SUM_ROWS, seq), BF16)
        vwt_ref[NSA_DIM:, :] = jnp.ones((SUM_ROWS, seq), BF16)

    def split_sum(acc):
        return acc[:NSA_DIM] / acc[NSA_DIM:NSA_DIM + 1]

    q_raw = _heads_to_lanes(qraw_ref)
    q_rot = _heads_to_lanes(qrot_ref)
    t_row = t0 + lax.broadcasted_iota(jnp.int32, (1, tq), 1)

    chain_w = cols // NSA_CHAINS
    heads_per_chain = chain_w // tq
    chains = [slice(c * chain_w, (c + 1) * chain_w) for c in range(NSA_CHAINS)]
    tile_chain = lambda v: jnp.concatenate([v] * heads_per_chain, axis=1)

    span = WIN + tq
    ws = pl.multiple_of(jnp.maximum(t0 - WIN, 0), tq)
    dist = t_row - (ws + lax.broadcasted_iota(jnp.int32, (span, 1), 0))
    bias_w = tile_chain(jnp.where((dist >= 0) & (dist < WIN), 0.0, NEG))
    k_w = kw_ref[pl.ds(ws, span), :]
    v_w = vwt_ref[:, pl.ds(ws, span)]
    o_w = []
    for c in chains:
        s_w = _dot(k_w, q_rot[:, c]) + bias_w
        p_w = jnp.exp((s_w - jnp.max(s_w, axis=0, keepdims=True)).astype(BF16))
        o_w.append(split_sum(_dot(v_w, p_w)))
    o_w = jnp.concatenate(o_w, axis=1)

    c_idx = lax.broadcasted_iota(jnp.int32, (n_cmp_rows, 1), 0)
    valid = tile_chain(jnp.where(c_idx * CMP_STRIDE + (CMP_LEN - 1) <= t_row, 1.0, 0.0))
    bias_c = (valid - 1.0) * (-NEG)
    o_c = []
    p_sum = None
    for c in chains:
        s_c = _dot(kcmp_ref[...], q_raw[:, c]) + bias_c
        e_c = jnp.exp(s_c - jnp.max(s_c, axis=0, keepdims=True)) * valid
        l_c = jnp.sum(e_c, axis=0, keepdims=True)
        p_c = e_c / jnp.where(l_c > 0.0, l_c, 1.0)
        o_c.append(_dot(vct_ref[...], p_c.astype(BF16)))
        for h in range(heads_per_chain):
            p_h = p_c[:, h * tq:(h + 1) * tq]
            p_sum = p_h if p_sum is None else p_sum + p_h
    o_c = jnp.concatenate(o_c, axis=1)

    p_hi = p_sum.astype(BF16)
    p_lo = (p_sum - p_hi.astype(F32)).astype(BF16)
    ovt = ovt_ref[...]
    imp = _dot(ovt, p_hi) + _dot(ovt, p_lo)
    jb = lax.broadcasted_iota(jnp.int32, (n_sel, tq), 0)
    cur = (t0 + lax.broadcasted_iota(jnp.int32, (n_sel, tq), 1)) >> SEL_SHIFT
    forced = (jb == 0) | (jb == cur) | (jb == cur - 1)
    work = jnp.where(forced, FORCE, imp)
    work = jnp.where(jb <= cur, work, NEG)
    sel_t = jnp.zeros((n_sel, tq), F32)
    for _ in range(min(SEL_TOPK, n_sel)):
        best = jnp.max(work, axis=0, keepdims=True)
        first = jnp.min(jnp.where(work == best, jb, n_sel), axis=0, keepdims=True)
        hit = jb == first
        sel_t = jnp.where(hit, 1.0, sel_t)
        work = jnp.where(hit, -jnp.inf, work)
    bias_ref[...] = jnp.where(sel_t > 0.5, 0.0, NEG)

    blocks_per_tile = tk // SEL_LEN

    def sel_tile(kt, carry, causal):
        k0 = pl.multiple_of(kt * tk, tk)
        bias = jnp.concatenate(
            [jnp.broadcast_to(bias_ref[pl.ds(kt * blocks_per_tile + j, 1), :], (SEL_LEN, tq))
             for j in range(blocks_per_tile)], axis=0)
        if causal:
            kpos = k0 + lax.broadcasted_iota(jnp.int32, (tk, 1), 0)
            bias = jnp.where(kpos <= t_row, bias, NEG)
        bias = tile_chain(bias)
        k_t = ks_ref[pl.ds(k0, tk), :]
        v_t = vst_ref[:, pl.ds(k0, tk)]
        out = []
        scores = [_dot(k_t, q_rot[:, c]) + bias for c in chains]
        for (m, acc), s in zip(carry, scores):
            m_new = jnp.maximum(m, jnp.max(s, axis=0, keepdims=True))
            p = jnp.exp((s - m_new).astype(BF16))
            acc = jnp.exp(m - m_new) * acc + _dot(v_t, p)
            out.append((m_new, acc))
        return tuple(out)

    n_full = t0 // tk
    init = tuple((jnp.full((1, chain_w), NEG, F32), jnp.zeros((NSA_DIM + SUM_ROWS, chain_w), F32))
                 for _ in chains)
    carry = lax.fori_loop(0, n_full, functools.partial(sel_tile, causal=False), init)
    carry = sel_tile(n_full, carry, causal=True)
    o_s = jnp.concatenate([split_sum(acc) for _, acc in carry], axis=1)

    gt = gate_ref[...]
    outs = []
    for h in range(NSA_HPG):
        c = slice(h * tq, (h + 1) * tq)
        outs.append(gt[3 * h:3 * h + 1] * o_c[:, c] + gt[3 * h + 1:3 * h + 2] * o_s[:, c]
                    + gt[3 * h + 2:3 * h + 3] * o_w[:, c])
    o_ref[...] = jnp.concatenate(outs, axis=0).T.astype(BF16)


def _nsa(nq, nqr, gates, kcmp, vcmp, ks, vs, kw, vw, batch, seq):
    n = batch * seq
    tq = 512
    tk = 512 if seq % 512 == 0 else seq
    nqb = seq // tq
    n_sel = seq // SEL_LEN
    rows_c = seq // CMP_STRIDE
    gw = NSA_HPG * NSA_DIM
    cs = np.arange(rows_c)[None, :] * CMP_STRIDE
    ss = np.arange(n_sel)[:, None] * SEL_LEN
    n_cmp = (seq - CMP_LEN) // CMP_STRIDE + 1
    ovt = ((cs < ss + SEL_LEN) & (cs + CMP_LEN > ss) & (np.arange(rows_c)[None, :] < n_cmp))
    ovt = jnp.asarray(ovt.astype(np.float32), BF16)

    qspec = pl.BlockSpec((tq, gw), lambda b, g, i: (b * nqb + i, g))
    cspec = pl.BlockSpec((None, None, rows_c, NSA_DIM), lambda b, g, i: (b, g, 0, 0))
    kspec = pl.BlockSpec((None, seq, NSA_DIM), lambda b, g, i: (g, b, 0))
    return pl.pallas_call(
        functools.partial(_nsa_kernel, tq=tq, tk=tk, seq=seq),
        grid=(batch, NSA_KV_GROUPS, nqb),
        in_specs=[qspec, qspec,
                  pl.BlockSpec((None, GATE_LANES, tq), lambda b, g, i: (g, 0, b * nqb + i)),
                  cspec, cspec, pl.BlockSpec(ovt.shape, lambda b, g, i: (0, 0)),
                  kspec, kspec, kspec, kspec],
        out_specs=qspec,
        out_shape=jax.ShapeDtypeStruct((n, NSA_WIDTH), BF16),
        scratch_shapes=[pltpu.VMEM((NSA_DIM + SUM_ROWS, seq), BF16), pltpu.VMEM((NSA_DIM + SUM_ROWS, seq), BF16),
                        pltpu.VMEM((NSA_DIM, rows_c), BF16), pltpu.VMEM((n_sel, tq), F32)],
        compiler_params=_params(3),
        name="nsa",
    )(nq, nqr, gates, kcmp, vcmp, ovt, ks, vs, kw, vw)


def _memkv_kernel(mem_ref, w_ref, kv_ref):
    kv_ref[...] = _dot(mem_ref[...].astype(BF16), w_ref[...]).astype(BF16)


def _memkv(mem2d, w_xkv):
    n = mem2d.shape[0]
    w = w_xkv.astype(BF16)
    return pl.pallas_call(
        _memkv_kernel,
        grid=(n // MEM_LEN,),
        in_specs=[pl.BlockSpec((MEM_LEN, D_MODEL), lambda i: (i, 0)),
                  pl.BlockSpec(w.shape, lambda i: (0, 0))],
        out_specs=pl.BlockSpec((MEM_LEN, 2 * D_MODEL), lambda i: (i, 0)),
        out_shape=jax.ShapeDtypeStruct((n, 2 * D_MODEL), BF16),
        compiler_params=_params(1),
        name="memkv",
    )(mem2d, w)


def _pack_halves(v):
    half = D_MODEL // 2
    hi = pltpu.bitcast(v[:, :half].astype(BF16).astype(F32), jnp.uint32)
    lo = pltpu.bitcast(v[:, half:].astype(BF16).astype(F32), jnp.uint32)
    return hi | (lo >> 16)


def _unpack_halves(words):
    return pltpu.bitcast(words & jnp.uint32(0xFFFF0000), F32), pltpu.bitcast(words << 16, F32)


def _postmix_kernel(x_ref, oret_ref, onsa_ref, kv_ref, wout_ref, wq_ref, wo_ref,
                    g1_ref, b1_ref, g2_ref, b2_ref, x2_ref, x2p_ref):
    mixed = jnp.concatenate([oret_ref[...], onsa_ref[...]], axis=1)
    x1 = _layer_norm(DN_ALPHA * x_ref[...] + _dot(mixed, wout_ref[...]), g1_ref[...], b1_ref[...])
    q = (_dot(x1.astype(BF16), wq_ref[...]) * (XATT_DIM ** -0.5)).astype(BF16)
    heads = []
    for h in range(XATT_HEADS):
        cols = slice(h * XATT_DIM, (h + 1) * XATT_DIM)
        s = _dot_nt(q[:, cols], kv_ref[:, cols])
        m = jnp.max(s, axis=-1, keepdims=True)
        p = jnp.exp(s - m)
        l = jnp.sum(p, axis=-1, keepdims=True)
        heads.append(_dot(p.astype(BF16), kv_ref[:, D_MODEL + h * XATT_DIM:D_MODEL + (h + 1) * XATT_DIM]) / l)
    att = jnp.concatenate(heads, axis=1).astype(BF16)
    x2 = _layer_norm(DN_ALPHA * x1 + _dot(att, wo_ref[...]), g2_ref[...], b2_ref[...])
    x2_ref[...] = x2
    x2p_ref[...] = _pack_halves(x2)


def _postmix(x2d, o_ret, o_nsa, kvx, w_out, w_xq, w_xo, ln1_g, ln1_b, ln2_g, ln2_b, batch, seq):
    n = x2d.shape[0]
    tm = 256 if seq % 256 == 0 else seq
    per_b = seq // tm
    row = lambda w: pl.BlockSpec((tm, w), lambda b, i: (b * per_b + i, 0))
    full = lambda a: pl.BlockSpec(a.shape, lambda b, i: (0,) * a.ndim)
    ws = [w_out.astype(BF16), w_xq.astype(BF16), w_xo.astype(BF16)]
    vecs = [v.reshape(1, D_MODEL) for v in (ln1_g, ln1_b, ln2_g, ln2_b)]
    return pl.pallas_call(
        _postmix_kernel,
        grid=(batch, per_b),
        in_specs=[row(D_MODEL), row(RET_WIDTH), row(NSA_WIDTH),
                  pl.BlockSpec((MEM_LEN, 2 * D_MODEL), lambda b, i: (b, 0))]
                 + [full(w) for w in ws] + [full(v) for v in vecs],
        out_specs=[row(D_MODEL),
                   row(D_MODEL // 2)],
        out_shape=[jax.ShapeDtypeStruct((n, D_MODEL), F32),
                   jax.ShapeDtypeStruct((n, D_MODEL // 2), jnp.uint32)],
        compiler_params=_params(2),
        name="postmix",
    )(x2d, o_ret, o_nsa, kvx, *ws, *vecs)


def _router_kernel(x_ref, wr_ref, bias_ref, e_ref, rank_ref, w_ref, cnt_ref, cntrow_ref, carry_ref, carryrow_ref):
    tn = x_ref.shape[0]
    per = N_EXPERTS // N_GROUPS

    @pl.when(pl.program_id(0) == 0)
    def _():
        carry_ref[...] = jnp.zeros_like(carry_ref)
        carryrow_ref[...] = jnp.zeros_like(carryrow_ref)

    logits = _dot_nt(wr_ref[...], x_ref[...].astype(BF16))
    scores = jax.nn.sigmoid(logits)
    biased = scores + bias_ref[...]
    b3 = biased.reshape(N_GROUPS, per, tn)
    member = lax.broadcasted_iota(jnp.int32, (N_GROUPS, per, tn), 1)
    top1 = jnp.max(b3, axis=1, keepdims=True)
    first1 = jnp.min(jnp.where(b3 == top1, member, per), axis=1, keepdims=True)
    top2 = jnp.max(jnp.where(member == first1, -jnp.inf, b3), axis=1, keepdims=True)
    gscore = top1 + top2
    gidx = lax.broadcasted_iota(jnp.int32, (N_GROUPS, 1, tn), 0)
    gwork = gscore
    for _ in range(TOPK_GROUPS - 1):
        gbest = jnp.max(gwork, axis=0, keepdims=True)
        gfirst = jnp.min(jnp.where(gwork == gbest, gidx, N_GROUPS), axis=0, keepdims=True)
        gwork = jnp.where(gidx == gfirst, -jnp.inf, gwork)
    kth = jnp.max(gwork, axis=0, keepdims=True)
    work = jnp.where(gscore >= kth, b3, NEG).reshape(N_EXPERTS, tn)
    eidx = lax.broadcasted_iota(jnp.int32, (N_EXPERTS, tn), 0)
    picks = []
    chosen = jnp.zeros((N_EXPERTS, tn), F32)
    for _ in range(TOP_K):
        best = jnp.max(work, axis=0, keepdims=True)
        first = jnp.min(jnp.where(work == best, eidx, N_EXPERTS), axis=0, keepdims=True)
        hit = eidx == first
        picks.append((first, hit))
        chosen = jnp.where(hit, 1.0, chosen)
        work = jnp.where(hit, -jnp.inf, work)

    r_i = lax.broadcasted_iota(jnp.int32, (tn, tn), 0)
    c_i = lax.broadcasted_iota(jnp.int32, (tn, tn), 1)
    before = jnp.where(r_i < c_i, 1.0, 0.0).astype(BF16)
    chosen_b = chosen.astype(BF16)
    rank = _dot(chosen_b, before) + carry_ref[...]
    carry_ref[...] = carry_ref[...] + jnp.sum(chosen, axis=1, keepdims=True)
    carryrow_ref[...] = carryrow_ref[...] + _dot_nt(jnp.ones((8, tn), BF16), chosen_b)
    cnt_ref[...] = carry_ref[...]
    cntrow_ref[...] = carryrow_ref[...]

    wsel = [jnp.sum(jnp.where(hit, scores, 0.0), axis=0, keepdims=True) for _, hit in picks]
    wsum = wsel[0]
    for v in wsel[1:]:
        wsum = wsum + v
    for kk, (first, hit) in enumerate(picks):
        e_ref[kk:kk + 1, :] = first
        rank_ref[kk:kk + 1, :] = jnp.sum(jnp.where(hit, rank, 0.0), axis=0, keepdims=True).astype(jnp.int32)
        w_ref[kk:kk + 1, :] = wsel[kk] / wsum * ROUTED_SCALE


def _router(x2, w_router, router_bias):
    n = x2.shape[0]
    tn = 512 if n % 512 == 0 else n
    wr_t = w_router.T.astype(BF16)
    bias = router_bias.reshape(N_EXPERTS, 1).astype(F32)
    kspec = pl.BlockSpec((TOP_K, tn), lambda i: (0, i))
    return pl.pallas_call(
        _router_kernel,
        grid=(n // tn,),
        in_specs=[pl.BlockSpec((tn, D_MODEL), lambda i: (i, 0)),
                  pl.BlockSpec(wr_t.shape, lambda i: (0, 0)),
                  pl.BlockSpec(bias.shape, lambda i: (0, 0))],
        out_specs=[kspec, kspec, kspec, pl.BlockSpec((N_EXPERTS, 1), lambda i: (0, 0)),
                   pl.BlockSpec((8, N_EXPERTS), lambda i: (0, 0))],
        out_shape=[jax.ShapeDtypeStruct((TOP_K, n), jnp.int32),
                   jax.ShapeDtypeStruct((TOP_K, n), jnp.int32),
                   jax.ShapeDtypeStruct((TOP_K, n), F32),
                   jax.ShapeDtypeStruct((N_EXPERTS, 1), F32),
                   jax.ShapeDtypeStruct((8, N_EXPERTS), F32)],
        scratch_shapes=[pltpu.VMEM((N_EXPERTS, 1), F32), pltpu.VMEM((8, N_EXPERTS), F32)],
        compiler_params=_params(1),
        name="router",
    )(x2, wr_t, bias)


def _slots_kernel(e_ref, rank_ref, cnt_ref, cntrow_ref, dest_ref, blk_e_ref, valid_ref, *, blk, n_blocks):
    pad = lambda c: jnp.ceil(c / blk) * blk
    cnt = cnt_ref[...]
    padded = pad(cnt)
    padded_row = pad(cntrow_ref[0:1, :])
    r_i = lax.broadcasted_iota(jnp.int32, (N_EXPERTS, N_EXPERTS), 0)
    c_i = lax.broadcasted_iota(jnp.int32, (N_EXPERTS, N_EXPERTS), 1)
    start = jnp.sum(jnp.where(c_i < r_i, padded_row, 0.0), axis=1, keepdims=True)
    end = start + padded
    e = e_ref[...]
    dest = rank_ref[...]
    for ex in range(N_EXPERTS):
        dest = dest + jnp.where(e == ex, start[ex:ex + 1, :].astype(jnp.int32), 0)
    dest_ref[...] = dest
    bstart = (lax.broadcasted_iota(jnp.int32, (1, n_blocks), 1) * blk).astype(F32)
    owner = jnp.sum(jnp.where(end <= bstart, 1.0, 0.0), axis=0, keepdims=True)
    blk_e_ref[...] = jnp.minimum(owner, N_EXPERTS - 1.0).astype(jnp.int32)
    inside = (start <= bstart) & (bstart < end)
    real = jnp.clip(start + cnt - bstart, 0.0, float(blk))
    valid_ref[...] = jnp.sum(jnp.where(inside, real, 0.0), axis=0, keepdims=True).astype(jnp.int32)


def _slots(e_k, rank_k, counts, counts_row, blk, n_blocks):
    n = e_k.shape[1]
    full = lambda shape: pl.BlockSpec(shape, lambda: (0,) * len(shape))
    return pl.pallas_call(
        functools.partial(_slots_kernel, blk=blk, n_blocks=n_blocks),
        in_specs=[full((TOP_K, n)), full((TOP_K, n)), full((N_EXPERTS, 1)), full((8, N_EXPERTS))],
        out_specs=[full((TOP_K, n)), full((1, n_blocks)), full((1, n_blocks))],
        out_shape=[jax.ShapeDtypeStruct((TOP_K, n), jnp.int32),
                   jax.ShapeDtypeStruct((1, n_blocks), jnp.int32),
                   jax.ShapeDtypeStruct((1, n_blocks), jnp.int32)],
        compiler_params=pltpu.CompilerParams(vmem_limit_bytes=VMEM_LIMIT),
        name="slots",
    )(e_k, rank_k, counts, counts_row)


def _sc_worker_base(per_worker):
    return (lax.axis_index("s") * SC_CORES + lax.axis_index("c")) * per_worker


def _sc_scatter_rows(rows, idx, n_out):
    n, width = rows.shape
    k_lists = idx.shape[0] // n
    workers = SC_CORES * SC_SUBCORES
    per_worker = n // workers
    assert per_worker * workers == n and per_worker % SC_CHUNK == 0
    mesh = plsc.VectorSubcoreMesh(core_axis_name="c", subcore_axis_name="s")

    @functools.partial(
        pl.kernel, mesh=mesh,
        out_type=jax.ShapeDtypeStruct((n_out, width), rows.dtype),
        scratch_types=[pltpu.VMEM((SC_CHUNK, width), rows.dtype)]
                      + [pltpu.VMEM((SC_CHUNK,), jnp.int32)] * k_lists + [pltpu.SemaphoreType.DMA] * 3,
        name="sc_scatter")
    def scatter(rows_hbm, idx_hbm, out_hbm, rows_v, *rest):
        idx_vs = rest[:k_lists]
        sem_rows, sem_idx, sem_out = rest[k_lists:]
        base = _sc_worker_base(per_worker)

        @pl.loop(0, per_worker // SC_CHUNK)
        def _(ci):
            off = pl.multiple_of(base + ci * SC_CHUNK, SC_CHUNK)
            loads = [pltpu.async_copy(rows_hbm.at[pl.ds(off, SC_CHUNK)], rows_v, sem_rows)]
            loads += [pltpu.async_copy(idx_hbm.at[pl.ds(pl.multiple_of(k * n + off, SC_CHUNK), SC_CHUNK)],
                                       idx_vs[k], sem_idx) for k in range(k_lists)]
            for c in loads:
                c.wait()
            copies = [pltpu.async_copy(rows_v, out_hbm.at[idx_vs[k]], sem_out) for k in range(k_lists)]
            for c in copies:
                c.wait()

    return scatter(rows, idx)


def _experts_kernel(blk_e_ref, valid_ref, xs_ref, wg_ref, wu_ref, wd_ref, y_ref):
    del blk_e_ref
    valid = valid_ref[pl.program_id(0)]

    @pl.when(valid > 0)
    def _():
        half = D_MODEL // 2
        row = lax.broadcasted_iota(jnp.int32, (xs_ref.shape[0], 1), 0)
        hi, lo = (v.astype(BF16) for v in _unpack_halves(jnp.where(row < valid, xs_ref[...], jnp.uint32(0))))
        gate = _dot(hi, wg_ref[:half, :]) + _dot(lo, wg_ref[half:, :])
        up = _dot(hi, wu_ref[:half, :]) + _dot(lo, wu_ref[half:, :])
        y_ref[...] = _pack_halves(_dot((jax.nn.silu(gate) * up).astype(BF16), wd_ref[...]))

    @pl.when(valid <= 0)
    def _():
        y_ref[...] = jnp.zeros_like(y_ref)


def _experts(blk_e, valid, xs, w_gate, w_up, w_down, blk):
    cap, width = xs.shape
    wg, wu, wd = (a.astype(BF16) for a in (w_gate, w_up, w_down))
    wspec = lambda a: pl.BlockSpec((None,) + a.shape[1:], lambda i, be, nv: (be[i], 0, 0))
    rows = pl.BlockSpec((blk, width), lambda i, be, nv: (i, 0))
    return pl.pallas_call(
        _experts_kernel,
        grid_spec=pltpu.PrefetchScalarGridSpec(
            num_scalar_prefetch=2,
            grid=(cap // blk,),
            in_specs=[rows, wspec(wg), wspec(wu), wspec(wd)],
            out_specs=rows,
        ),
        out_shape=jax.ShapeDtypeStruct(xs.shape, xs.dtype),
        compiler_params=_params(1),
        name="experts",
    )(blk_e, valid, xs, wg, wu, wd)


def _sc_gather_rows(table, idx):
    b, width = idx.shape[0], table.shape[1]
    workers = SC_CORES * SC_SUBCORES
    per_worker = b // workers
    assert per_worker * workers == b and per_worker % (SC_CHUNK * SC_INFLIGHT) == 0
    mesh = plsc.VectorSubcoreMesh(core_axis_name="c", subcore_axis_name="s")

    @functools.partial(
        pl.kernel, mesh=mesh,
        out_type=jax.ShapeDtypeStruct((b, width), table.dtype),
        scratch_types=[pltpu.VMEM((SC_CHUNK,), jnp.int32)] * SC_INFLIGHT
                      + [pltpu.VMEM((SC_CHUNK, width), table.dtype)] * SC_INFLIGHT
                      + [pltpu.SemaphoreType.DMA] * (1 + 2 * SC_INFLIGHT),
        name="sc_gather")
    def gather(table_hbm, idx_hbm, out_hbm, *scratch):
        idx_vs = scratch[:SC_INFLIGHT]
        rows_vs = scratch[SC_INFLIGHT:2 * SC_INFLIGHT]
        sem_idx = scratch[2 * SC_INFLIGHT]
        sem_rows = scratch[2 * SC_INFLIGHT + 1:3 * SC_INFLIGHT + 1]
        sem_out = scratch[3 * SC_INFLIGHT + 1:]
        base = _sc_worker_base(per_worker)
        lanes = range(SC_INFLIGHT)

        @pl.loop(0, per_worker // (SC_CHUNK * SC_INFLIGHT))
        def _(gi):
            offs = [pl.multiple_of(base + (gi * SC_INFLIGHT + j) * SC_CHUNK, SC_CHUNK) for j in lanes]
            loads = [pltpu.async_copy(idx_hbm.at[pl.ds(offs[j], SC_CHUNK)], idx_vs[j], sem_idx) for j in lanes]
            for c in loads:
                c.wait()
            gathers = [pltpu.async_copy(table_hbm.at[idx_vs[j]], rows_vs[j], sem_rows[j]) for j in lanes]
            writes = []
            for j in lanes:
                gathers[j].wait()
                writes.append(pltpu.async_copy(rows_vs[j], out_hbm.at[pl.ds(offs[j], SC_CHUNK)], sem_out[j]))
            for c in writes:
                c.wait()

    return gather(table, idx)


def _combine_kernel(x_ref, wk_ref, yk_ref, wsg_ref, wsu_ref, wsd_ref, g_ref, b_ref, o_ref):
    x = x_ref[...]
    xb = x.astype(BF16)
    shared = _dot((jax.nn.silu(_dot(xb, wsg_ref[...])) * _dot(xb, wsu_ref[...])).astype(BF16), wsd_ref[...])
    wk = wk_ref[...]
    routed_hi = routed_lo = None
    for kk in range(TOP_K):
        hi, lo = _unpack_halves(yk_ref[kk])
        w = wk[:, kk:kk + 1]
        routed_hi = hi * w if kk == 0 else routed_hi + hi * w
        routed_lo = lo * w if kk == 0 else routed_lo + lo * w
    routed = jnp.concatenate([routed_hi, routed_lo], axis=1)
    o_ref[...] = _layer_norm(DN_ALPHA * x + (routed + shared), g_ref[...], b_ref[...])


def _combine(x2, w_tok, yk, ws_gate, ws_up, ws_down, ln3_g, ln3_b):
    n = x2.shape[0]
    tt = 256 if n % 256 == 0 else n
    ws = [ws_gate.astype(BF16), ws_up.astype(BF16), ws_down.astype(BF16)]
    vecs = [ln3_g.reshape(1, D_MODEL), ln3_b.reshape(1, D_MODEL)]
    full = lambda a: pl.BlockSpec(a.shape, lambda i: (0,) * a.ndim)
    return pl.pallas_call(
        _combine_kernel,
        grid=(n // tt,),
        in_specs=[pl.BlockSpec((tt, D_MODEL), lambda i: (i, 0)),
                  pl.BlockSpec((tt, TOP_K), lambda i: (i, 0)),
                  pl.BlockSpec((TOP_K, tt, D_MODEL // 2), lambda i: (0, i, 0))]
                 + [full(a) for a in ws] + [full(v) for v in vecs],
        out_specs=pl.BlockSpec((tt, D_MODEL), lambda i: (i, 0)),
        out_shape=jax.ShapeDtypeStruct((n, D_MODEL), F32),
        compiler_params=_params(1),
        name="combine",
    )(x2, w_tok, yk, *ws, *vecs)


def _moe_and_norm(x2, x2p, w_router, router_bias, w_gate, w_up, w_down, ws_gate, ws_up, ws_down,
                  ln3_g, ln3_b):
    n = x2.shape[0]
    blk = 512
    cap = n * TOP_K + N_EXPERTS * blk
    n_blocks = cap // blk
    e_k, rank_k, w_k, counts, counts_row = _router(x2, w_router, router_bias)
    dest, blk_e, valid = _slots(e_k, rank_k, counts, counts_row, blk, n_blocks)
    dest = dest.reshape(-1)
    xs = _sc_scatter_rows(x2p, dest, cap)
    y = _experts(blk_e.reshape(-1), valid.reshape(-1), xs, w_gate, w_up, w_down, blk)
    yk = _sc_gather_rows(y, dest).reshape(TOP_K, n, D_MODEL // 2)
    return _combine(x2, w_k.T, yk, ws_gate, ws_up, ws_down, ln3_g, ln3_b)


def _layer(x, mem, positions, w_in, cmp_pe_k, cmp_pe_v, cmp_w1_k, cmp_w2_k, cmp_w1_v, cmp_w2_v,
           w_out, ln1_g, ln1_b, w_xq, w_xkv, w_xo, ln2_g, ln2_b, w_router, router_bias,
           w_gate, w_up, w_down, ws_gate, ws_up, ws_down, ln3_g, ln3_b):
    batch, seq, _ = x.shape
    n = batch * seq
    x2d = x.reshape(n, D_MODEL)
    pos_col = positions.astype(F32).reshape(n, 1)
    (rq, rk, rv, rg, nq, nqr, kc, vc, ks, vs, kw, vw, gates) = _inproj(x2d, pos_col, w_in)
    o_ret = _retention(rq, rk, rv, rg, batch, seq)
    kcmp = _compress(kc, cmp_pe_k, cmp_w1_k, cmp_w2_k, batch, seq)
    vcmp = _compress(vc, cmp_pe_v, cmp_w1_v, cmp_w2_v, batch, seq)
    o_nsa = _nsa(nq, nqr, gates, kcmp, vcmp, ks, vs, kw, vw, batch, seq)
    kvx = _memkv(mem.reshape(batch * MEM_LEN, D_MODEL), w_xkv)
    x2, x2p = _postmix(x2d, o_ret, o_nsa, kvx, w_out, w_xq, w_xo, ln1_g, ln1_b, ln2_g, ln2_b, batch, seq)
    out = _moe_and_norm(x2, x2p, w_router, router_bias, w_gate, w_up, w_down,
                        ws_gate, ws_up, ws_down, ln3_g, ln3_b)
    return out.reshape(batch, seq, D_MODEL)


def kernel(x, mem, positions, w_in, cmp_pe_k, cmp_pe_v, cmp_w1_k, cmp_w2_k, cmp_w1_v, cmp_w2_v, w_out, ln1_g, ln1_b, w_xq, w_xkv, w_xo, ln2_g, ln2_b, w_router, router_bias, w_gate, w_up, w_down, ws_gate, ws_up, ws_down, ln3_g, ln3_b):
    for l in range(DEPTH):
        x = _layer(x, mem, positions, w_in[l], cmp_pe_k[l], cmp_pe_v[l], cmp_w1_k[l], cmp_w2_k[l],
                   cmp_w1_v[l], cmp_w2_v[l], w_out[l], ln1_g[l], ln1_b[l], w_xq[l], w_xkv[l],
                   w_xo[l], ln2_g[l], ln2_b[l], w_router[l], router_bias[l], w_gate[l], w_up[l],
                   w_down[l], ws_gate[l], ws_up[l], ws_down[l], ln3_g[l], ln3_b[l])
    return x
```

```python
import functools

import numpy as np
import jax
import jax.numpy as jnp
from jax import lax
from jax.experimental import pallas as pl
from jax.experimental.pallas import tpu as pltpu
from jax.experimental.pallas import tpu_sc as plsc

D_MODEL = 1024
MEM_LEN = 256
DEPTH = 1
DN_ALPHA = (2 * DEPTH) ** 0.25
LN_EPS = 1e-5
NEG = -1e30
FORCE = 1e9

RET_HEADS = 4
RET_DIM = 128
RET_CHUNK = 128
RET_ROPE_BASE = 10000.0
RET_WIDTH = RET_HEADS * RET_DIM

NSA_HEADS = 8
NSA_KV_GROUPS = 2
NSA_HPG = NSA_HEADS // NSA_KV_GROUPS
NSA_DIM = 64
NSA_WIDTH = NSA_HEADS * NSA_DIM
KV_WIDTH = NSA_KV_GROUPS * NSA_DIM
CMP_LEN = 32
CMP_STRIDE = 16
CMP_HIDDEN = 256
SEL_LEN = 64
SEL_SHIFT = 6
SEL_TOPK = 16
WIN = 512
ROPE_THETA = 500000.0
ROPE_DIMS = NSA_DIM // 4
GATE_LANES = 16
NSA_CHAINS = 2
SUM_ROWS = 16

BATCH_GROUPS = 2

SC_CORES = 2
SC_SUBCORES = 16
SC_CHUNK = 64
SC_INFLIGHT = 2

XATT_HEADS = 4
XATT_DIM = D_MODEL // XATT_HEADS

N_EXPERTS = 64
TOP_K = 8
N_GROUPS = 8
TOPK_GROUPS = 4
EXPERT_FF = 256
SHARED_FF = 256
ROUTED_SCALE = 2.5

LANES = 128
VMEM_LIMIT = 56 * 1024 * 1024

F32 = jnp.float32
BF16 = jnp.bfloat16
NT_DIMS = (((1,), (1,)), ((), ()))


def _params(n_axes):
    return pltpu.CompilerParams(dimension_semantics=("arbitrary",) * n_axes,
                                vmem_limit_bytes=VMEM_LIMIT)


def _dot(a, b):
    return jnp.dot(a, b, preferred_element_type=F32)


def _dot_nt(a, b):
    return lax.dot_general(a, b, NT_DIMS, preferred_element_type=F32)


def _layer_norm(v, g, b):
    mu = jnp.mean(v, axis=-1, keepdims=True)
    d = v - mu
    var = jnp.mean(d * d, axis=-1, keepdims=True)
    return d * lax.rsqrt(var + LN_EPS) * g + b


def _inproj_kernel(x_ref, pos_ref, wret_ref, wnq_ref, wkv_ref, wg_ref, invr_ref, invn_ref,
                   rq_ref, rk_ref, rv_ref, rg_ref, nq_ref, nqr_ref, kc_ref, vc_ref,
                   ks_ref, vs_ref, kw_ref, vw_ref, gate_ref):
    xb = x_ref[...].astype(BF16)
    pos = pos_ref[...]
    lane = lax.broadcasted_iota(jnp.int32, (1, LANES), 1)

    ang = pos * invr_ref[...]
    cos_r = jnp.cos(ang)
    sin_r = jnp.sin(ang)
    sin_r = jnp.where(lane < RET_DIM // 2, -sin_r, sin_r)
    q_all = _dot(xb, wret_ref[:, :RET_WIDTH])
    k_all = _dot(xb, wret_ref[:, RET_WIDTH:2 * RET_WIDTH])
    for h in range(RET_HEADS):
        cols = slice(h * RET_DIM, (h + 1) * RET_DIM)
        q = q_all[:, cols]
        rq_ref[:, cols] = (q * cos_r + pltpu.roll(q, RET_DIM // 2, 1) * sin_r).astype(BF16)
        k = k_all[:, cols]
        k = (k * cos_r + pltpu.roll(k, RET_DIM // 2, 1) * sin_r) * (RET_DIM ** -0.5)
        rk_ref[:, cols] = k.astype(BF16)
    rv_ref[...] = _dot(xb, wret_ref[:, 2 * RET_WIDTH:3 * RET_WIDTH]).astype(BF16)
    rg_ref[...] = _dot(xb, wret_ref[:, 3 * RET_WIDTH:4 * RET_WIDTH]).astype(BF16)

    half = ROPE_DIMS // 2
    j = lane % NSA_DIM
    angn = pos * invn_ref[...]
    cos_n = jnp.cos(angn)
    sin_n = jnp.sin(angn)
    sin_lo = jnp.where(j < half, -sin_n, 0.0)
    sin_hi = jnp.where((j >= half) & (j < 2 * half), sin_n, 0.0)

    def rope_n(v):
        return v * cos_n + pltpu.roll(v, half, 1) * sin_hi + pltpu.roll(v, LANES - half, 1) * sin_lo

    scale = NSA_DIM ** -0.5
    nq_all = _dot(xb, wnq_ref[...])
    for c in range(NSA_WIDTH // LANES):
        cols = slice(c * LANES, (c + 1) * LANES)
        q = nq_all[:, cols]
        nq_ref[:, cols] = (q * scale).astype(BF16)
        nqr_ref[:, cols] = (rope_n(q) * scale).astype(BF16)

    kv_all = _dot(xb, wkv_ref[...])

    def kv(i):
        return kv_all[:, i * KV_WIDTH:(i + 1) * KV_WIDTH]

    def split_groups(ref, v):
        for g in range(NSA_KV_GROUPS):
            ref[g] = v[:, g * NSA_DIM:(g + 1) * NSA_DIM].astype(BF16)

    kc_ref[...] = kv(0)
    vc_ref[...] = kv(1)
    split_groups(ks_ref, rope_n(kv(2)))
    split_groups(vs_ref, kv(3))
    split_groups(kw_ref, rope_n(kv(4)))
    split_groups(vw_ref, kv(5))

    gt = jax.nn.sigmoid(_dot_nt(wg_ref[...], xb))
    for g in range(NSA_KV_GROUPS):
        gate_ref[g] = gt[g * GATE_LANES:(g + 1) * GATE_LANES, :]


def _inproj(x2d, pos_col, w_in):
    n = x2d.shape[0]
    tm = 512 if n % 512 == 0 else n
    off = np.cumsum([0] + [RET_WIDTH] * 4 + [NSA_WIDTH] + [KV_WIDTH] * 6)
    w_ret = w_in[:, :off[4]].astype(BF16)
    w_nq = w_in[:, off[4]:off[5]].astype(BF16)
    w_kv = w_in[:, off[5]:off[11]].astype(BF16)
    wg = w_in[:, off[11]:].reshape(D_MODEL, NSA_KV_GROUPS, NSA_HPG * 3)
    wg = jnp.pad(wg, ((0, 0), (0, 0), (0, GATE_LANES - NSA_HPG * 3)))
    wg = wg.reshape(D_MODEL, NSA_KV_GROUPS * GATE_LANES).T.astype(BF16)

    lane = np.arange(LANES)
    half_r = RET_DIM // 2
    inv_r = (np.float32(RET_ROPE_BASE) ** (-np.arange(half_r, dtype=np.float32) / np.float32(half_r)))
    inv_r = inv_r.astype(np.float32)[lane % half_r][None, :]
    half_n = ROPE_DIMS // 2
    inv_n = (np.float32(ROPE_THETA) ** (-np.arange(half_n, dtype=np.float32) / np.float32(half_n)))
    jn = lane % NSA_DIM
    inv_n = np.where(jn < ROPE_DIMS, inv_n.astype(np.float32)[jn % half_n], np.float32(0.0))[None, :]

    row = lambda w: pl.BlockSpec((tm, w), lambda i: (i, 0))
    full = lambda a: pl.BlockSpec(a.shape, lambda i: (0,) * a.ndim)
    grp = lambda w: pl.BlockSpec((NSA_KV_GROUPS, tm, w), lambda i: (0, i, 0))
    bf = lambda w: jax.ShapeDtypeStruct((n, w), BF16)
    gbf = jax.ShapeDtypeStruct((NSA_KV_GROUPS, n, NSA_DIM), BF16)
    inv_r = jnp.asarray(inv_r, F32)
    inv_n = jnp.asarray(inv_n, F32)
    return pl.pallas_call(
        _inproj_kernel,
        grid=(n // tm,),
        in_specs=[row(D_MODEL), row(1), full(w_ret), full(w_nq), full(w_kv), full(wg),
                  full(inv_r), full(inv_n)],
        out_specs=[row(RET_WIDTH)] * 4 + [row(NSA_WIDTH)] * 2 + [row(KV_WIDTH)] * 2
                  + [grp(NSA_DIM)] * 4
                  + [pl.BlockSpec((NSA_KV_GROUPS, GATE_LANES, tm), lambda i: (0, 0, i))],
        out_shape=[bf(RET_WIDTH)] * 4 + [bf(NSA_WIDTH)] * 2
                  + [jax.ShapeDtypeStruct((n, KV_WIDTH), F32)] * 2 + [gbf] * 4
                  + [jax.ShapeDtypeStruct((NSA_KV_GROUPS, GATE_LANES, n), F32)],
        compiler_params=_params(1),
        name="inproj",
    )(x2d, pos_col, w_ret, w_nq, w_kv, wg, inv_r, inv_n)


def _retention_kernel(q_ref, k_ref, v_ref, g_ref, o_ref, state_ref):
    c = RET_CHUNK

    @pl.when(pl.program_id(1) == 0)
    def _():
        state_ref[...] = jnp.zeros_like(state_ref)

    row = lax.broadcasted_iota(jnp.int32, (c, c), 0)
    col = lax.broadcasted_iota(jnp.int32, (c, c), 1)
    rel = (row - col).astype(F32)
    idx = lax.broadcasted_iota(jnp.int32, (c, 1), 0).astype(F32)
    for h in range(RET_HEADS):
        log_g = float(np.log(np.float32(1.0) - np.float32(2.0) ** np.float32(-5.0 - h)))
        cols = slice(h * RET_DIM, (h + 1) * RET_DIM)
        q = q_ref[:, cols]
        k = k_ref[:, cols]
        v = v_ref[:, cols]
        dmask = jnp.where(rel >= 0, jnp.exp(log_g * jnp.maximum(rel, 0.0)), 0.0)
        scores = _dot_nt(q, k) * dmask
        inner = _dot(scores.astype(BF16), v)
        zeta = jnp.exp(log_g * (c - 1.0 - idx))
        xi = jnp.exp(log_g * (idx + 1.0))
        prev = state_ref[h]
        cross = _dot(q, prev.astype(BF16)) * xi
        kz = (k.astype(F32) * zeta).astype(BF16)
        kv = lax.dot_general(kz, v, (((0,), (0,)), ((), ())), preferred_element_type=F32)
        state_ref[h] = prev * float(np.exp(np.float32(log_g) * np.float32(c))) + kv
        o = inner + cross
        mu = jnp.mean(o, axis=-1, keepdims=True)
        d = o - mu
        var = jnp.mean(d * d, axis=-1, keepdims=True)
        o = d * lax.rsqrt(var + LN_EPS)
        o_ref[:, cols] = (jax.nn.silu(g_ref[:, cols].astype(F32)) * o).astype(BF16)


def _retention(rq, rk, rv, rg, batch, seq):
    nc = seq // RET_CHUNK
    spec = pl.BlockSpec((RET_CHUNK, RET_WIDTH), lambda b, n: (b * nc + n, 0))
    return pl.pallas_call(
        _retention_kernel,
        grid=(batch, nc),
        in_specs=[spec] * 4,
        out_specs=spec,
        out_shape=jax.ShapeDtypeStruct(rq.shape, BF16),
        scratch_shapes=[pltpu.VMEM((RET_HEADS, RET_DIM, RET_DIM), F32)],
        compiler_params=_params(2),
        name="retention",
    )(rq, rk, rv, rg)


def _compress_kernel(a_ref, pe_ref, w1_ref, w2_ref, o_ref, shift_ref, *, n_cmp):
    rows = a_ref.shape[0]
    a = a_ref[...]
    lo = (a + pe_ref[0]).astype(BF16)
    hi = (a + pe_ref[1]).astype(BF16)
    ridx = lax.broadcasted_iota(jnp.int32, (rows, 1), 0)
    shift_ref[rows:rows + 8, :] = jnp.zeros((8, CMP_HIDDEN), F32)
    for g in range(NSA_KV_GROUPS):
        p = _dot(lo, w1_ref[0, g])
        shift_ref[0:rows, :] = _dot(hi, w1_ref[1, g])
        hid = jax.nn.silu(p + shift_ref[pl.ds(1, rows), :])
        out = _dot(hid.astype(BF16), w2_ref[...])
        o_ref[g] = jnp.where(ridx < n_cmp, out, 0.0).astype(BF16)


def _compress(a, pe, w1, w2, batch, seq):
    rows = seq // CMP_STRIDE
    per = CMP_STRIDE * KV_WIDTH
    n_cmp = (seq - CMP_LEN) // CMP_STRIDE + 1
    a2 = a.reshape(batch * rows, per)
    pe2 = jnp.tile(pe.reshape(2, CMP_STRIDE, 1, NSA_DIM), (1, 1, NSA_KV_GROUPS, 1)).reshape(2, 1, per)
    w1r = w1.reshape(2, CMP_STRIDE, 1, NSA_DIM, CMP_HIDDEN)
    eye = jnp.eye(NSA_KV_GROUPS, dtype=w1.dtype).reshape(1, NSA_KV_GROUPS, 1, NSA_KV_GROUPS, 1, 1)
    w1x = (w1r[:, None] * eye).reshape(2, NSA_KV_GROUPS, per, CMP_HIDDEN).astype(BF16)
    w2b = w2.astype(BF16)
    full = lambda arr: pl.BlockSpec(arr.shape, lambda b: (0,) * arr.ndim)
    return pl.pallas_call(
        functools.partial(_compress_kernel, n_cmp=n_cmp),
        grid=(batch,),
        in_specs=[pl.BlockSpec((rows, per), lambda b: (b, 0)), full(pe2), full(w1x), full(w2b)],
        out_specs=pl.BlockSpec((None, NSA_KV_GROUPS, rows, NSA_DIM), lambda b: (b, 0, 0, 0)),
        out_shape=jax.ShapeDtypeStruct((batch, NSA_KV_GROUPS, rows, NSA_DIM), BF16),
        scratch_shapes=[pltpu.VMEM((rows + 8, CMP_HIDDEN), F32)],
        compiler_params=_params(1),
        name="compress",
    )(a2, pe2, w1x, w2b)


def _heads_to_lanes(ref):
    vt = ref[...].astype(F32).T
    return jnp.concatenate([vt[h * NSA_DIM:(h + 1) * NSA_DIM] for h in range(NSA_HPG)], axis=1).astype(BF16)


def _tile_heads(v):
    return jnp.concatenate([v] * NSA_HPG, axis=1)


def _transpose_into(dst_ref, src_ref, chunk):
    def step(c, _):
        c0 = pl.multiple_of(c * chunk, chunk)
        dst_ref[:NSA_DIM, pl.ds(c0, chunk)] = src_ref[pl.ds(c0, chunk), :].astype(F32).T.astype(BF16)
        return 0
    lax.fori_loop(0, src_ref.shape[0] // chunk, step, 0)


def _nsa_kernel(qraw_ref, qrot_ref, gate_ref, kcmp_ref, vcmp_ref, ovt_ref,
                ks_ref, vs_ref, kw_ref, vw_ref, o_ref, vst_ref, vwt_ref, vct_ref, bias_ref, *, tq, tk, seq):
    i = pl.program_id(2)
    t0 = i * tq
    cols = NSA_HPG * tq
    n_sel = seq // SEL_LEN
    n_cmp_rows = seq // CMP_STRIDE

    @pl.when(i == 0)
    def _():
        chunk = min(512, n_cmp_rows)
        _transpose_into(vst_ref, vs_ref, chunk)
        _transpose_into(vwt_ref, vw_ref, chunk)
        _transpose_into(vct_ref, vcmp_ref, chunk)
        vst_ref[NSA_DIM:, :] = jnp.ones((SUM_ROWS, seq), BF16)
        vwt_ref[NSA_DIM:, :] = jnp.ones((SUM_ROWS, seq), BF16)

    def split_sum(acc):
        return acc[:NSA_DIM] / acc[NSA_DIM:NSA_DIM + 1]

    q_raw = _heads_to_lanes(qraw_ref)
    q_rot = _heads_to_lanes(qrot_ref)
    t_row = t0 + lax.broadcasted_iota(jnp.int32, (1, tq), 1)

    chain_w = cols // NSA_CHAINS
    heads_per_chain = chain_w // tq
    chains = [slice(c * chain_w, (c + 1) * chain_w) for c in range(NSA_CHAINS)]
    tile_chain = lambda v: jnp.concatenate([v] * heads_per_chain, axis=1)

    span = WIN + tq
    ws = pl.multiple_of(jnp.maximum(t0 - WIN, 0), tq)
    dist = t_row - (ws + lax.broadcasted_iota(jnp.int32, (span, 1), 0))
    bias_w = tile_chain(jnp.where((dist >= 0) & (dist < WIN), 0.0, NEG))
    k_w = kw_ref[pl.ds(ws, span), :]
    v_w = vwt_ref[:, pl.ds(ws, span)]
    o_w = []
    for c in chains:
        s_w = _dot(k_w, q_rot[:, c]) + bias_w
        p_w = jnp.exp((s_w - jnp.max(s_w, axis=0, keepdims=True)).astype(BF16))
        o_w.append(split_sum(_dot(v_w, p_w)))
    o_w = jnp.concatenate(o_w, axis=1)

    c_idx = lax.broadcasted_iota(jnp.int32, (n_cmp_rows, 1), 0)
    valid = tile_chain(jnp.where(c_idx * CMP_STRIDE + (CMP_LEN - 1) <= t_row, 1.0, 0.0))
    bias_c = (valid - 1.0) * (-NEG)
    o_c = []
    p_sum = None
    for c in chains:
        s_c = _dot(kcmp_ref[...], q_raw[:, c]) + bias_c
        e_c = jnp.exp(s_c - jnp.max(s_c, axis=0, keepdims=True)) * valid
        l_c = jnp.sum(e_c, axis=0, keepdims=True)
        p_c = e_c / jnp.where(l_c > 0.0, l_c, 1.0)
        o_c.append(_dot(vct_ref[...], p_c.astype(BF16)))
        for h in range(heads_per_chain):
            p_h = p_c[:, h * tq:(h + 1) * tq]
            p_sum = p_h if p_sum is None else p_sum + p_h
    o_c = jnp.concatenate(o_c, axis=1)

    p_hi = p_sum.astype(BF16)
    p_lo = (p_sum - p_hi.astype(F32)).astype(BF16)
    ovt = ovt_ref[...]
    imp = _dot(ovt, p_hi) + _dot(ovt, p_lo)
    jb = lax.broadcasted_iota(jnp.int32, (n_sel, tq), 0)
    cur = (t0 + lax.broadcasted_iota(jnp.int32, (n_sel, tq), 1)) >> SEL_SHIFT
    forced = (jb == 0) | (jb == cur) | (jb == cur - 1)
    work = jnp.where(forced, FORCE, imp)
    work = jnp.where(jb <= cur, work, NEG)
    sel_t = jnp.zeros((n_sel, tq), F32)
    for _ in range(min(SEL_TOPK, n_sel)):
        best = jnp.max(work, axis=0, keepdims=True)
        first = jnp.min(jnp.where(work == best, jb, n_sel), axis=0, keepdims=True)
        hit = jb == first
        sel_t = jnp.where(hit, 1.0, sel_t)
        work = jnp.where(hit, -jnp.inf, work)
    bias_ref[...] = jnp.where(sel_t > 0.5, 0.0, NEG)

    blocks_per_tile = tk // SEL_LEN

    def sel_tile(kt, carry, causal):
        k0 = pl.multiple_of(kt * tk, tk)
        bias = jnp.concatenate(
            [jnp.broadcast_to(bias_ref[pl.ds(kt * blocks_per_tile + j, 1), :], (SEL_LEN, tq))
             for j in range(blocks_per_tile)], axis=0)
        if causal:
            kpos = k0 + lax.broadcasted_iota(jnp.int32, (tk, 1), 0)
            bias = jnp.where(kpos <= t_row, bias, NEG)
        bias = tile_chain(bias)
        k_t = ks_ref[pl.ds(k0, tk), :]
        v_t = vst_ref[:, pl.ds(k0, tk)]
        out = []
        scores = [_dot(k_t, q_rot[:, c]) + bias for c in chains]
        for (m, acc), s in zip(carry, scores):
            m_new = jnp.maximum(m, jnp.max(s, axis=0, keepdims=True))
            p = jnp.exp((s - m_new).astype(BF16))
            acc = jnp.exp(m - m_new) * acc + _dot(v_t, p)
            out.append((m_new, acc))
        return tuple(out)

    n_full = t0 // tk
    init = tuple((jnp.full((1, chain_w), NEG, F32), jnp.zeros((NSA_DIM + SUM_ROWS, chain_w), F32))
                 for _ in chains)
    carry = lax.fori_loop(0, n_full, functools.partial(sel_tile, causal=False), init)
    carry = sel_tile(n_full, carry, causal=True)
    o_s = jnp.concatenate([split_sum(acc) for _, acc in carry], axis=1)

    gt = gate_ref[...]
    outs = []
    for h in range(NSA_HPG):
        c = slice(h * tq, (h + 1) * tq)
        outs.append(gt[3 * h:3 * h + 1] * o_c[:, c] + gt[3 * h + 1:3 * h + 2] * o_s[:, c]
                    + gt[3 * h + 2:3 * h + 3] * o_w[:, c])
    o_ref[...] = jnp.concatenate(outs, axis=0).T.astype(BF16)


def _nsa(nq, nqr, gates, kcmp, vcmp, ks, vs, kw, vw, batch, seq):
    n = batch * seq
    tq = 512
    tk = 512 if seq % 512 == 0 else seq
    nqb = seq // tq
    n_sel = seq // SEL_LEN
    rows_c = seq // CMP_STRIDE
    gw = NSA_HPG * NSA_DIM
    cs = np.arange(rows_c)[None, :] * CMP_STRIDE
    ss = np.arange(n_sel)[:, None] * SEL_LEN
    n_cmp = (seq - CMP_LEN) // CMP_STRIDE + 1
    ovt = ((cs < ss + SEL_LEN) & (cs + CMP_LEN > ss) & (np.arange(rows_c)[None, :] < n_cmp))
    ovt = jnp.asarray(ovt.astype(np.float32), BF16)

    qspec = pl.BlockSpec((tq, gw), lambda b, g, i: (b * nqb + i, g))
    cspec = pl.BlockSpec((None, None, rows_c, NSA_DIM), lambda b, g, i: (b, g, 0, 0))
    kspec = pl.BlockSpec((None, seq, NSA_DIM), lambda b, g, i: (g, b, 0))
    return pl.pallas_call(
        functools.partial(_nsa_kernel, tq=tq, tk=tk, seq=seq),
        grid=(batch, NSA_KV_GROUPS, nqb),
        in_specs=[qspec, qspec,
                  pl.BlockSpec((None, GATE_LANES, tq), lambda b, g, i: (g, 0, b * nqb + i)),
                  cspec, cspec, pl.BlockSpec(ovt.shape, lambda b, g, i: (0, 0)),
                  kspec, kspec, kspec, kspec],
        out_specs=qspec,
        out_shape=jax.ShapeDtypeStruct((n, NSA_WIDTH), BF16),
        scratch_shapes=[pltpu.VMEM((NSA_DIM + SUM_ROWS, seq), BF16), pltpu.VMEM((NSA_DIM + SUM_ROWS, seq), BF16),
                        pltpu.VMEM((NSA_DIM, rows_c), BF16), pltpu.VMEM((n_sel, tq), F32)],
        compiler_params=_params(3),
        name="nsa",
    )(nq, nqr, gates, kcmp, vcmp, ovt, ks, vs, kw, vw)


def _memkv_kernel(mem_ref, w_ref, kv_ref):
    kv_ref[...] = _dot(mem_ref[...].astype(BF16), w_ref[...]).astype(BF16)


def _memkv(mem2d, w_xkv):
    n = mem2d.shape[0]
    w = w_xkv.astype(BF16)
    return pl.pallas_call(
        _memkv_kernel,
        grid=(n // MEM_LEN,),
        in_specs=[pl.BlockSpec((MEM_LEN, D_MODEL), lambda i: (i, 0)),
                  pl.BlockSpec(w.shape, lambda i: (0, 0))],
        out_specs=pl.BlockSpec((MEM_LEN, 2 * D_MODEL), lambda i: (i, 0)),
        out_shape=jax.ShapeDtypeStruct((n, 2 * D_MODEL), BF16),
        compiler_params=_params(1),
        name="memkv",
    )(mem2d, w)


def _pack_halves(v):
    half = D_MODEL // 2
    hi = pltpu.bitcast(v[:, :half].astype(BF16).astype(F32), jnp.uint32)
    lo = pltpu.bitcast(v[:, half:].astype(BF16).astype(F32), jnp.uint32)
    return hi | (lo >> 16)


def _unpack_halves(words):
    return pltpu.bitcast(words & jnp.uint32(0xFFFF0000), F32), pltpu.bitcast(words << 16, F32)


def _postmix_kernel(x_ref, oret_ref, onsa_ref, kv_ref, wout_ref, wq_ref, wo_ref,
                    g1_ref, b1_ref, g2_ref, b2_ref, x2_ref, x2p_ref):
    mixed = jnp.concatenate([oret_ref[...], onsa_ref[...]], axis=1)
    x1 = _layer_norm(DN_ALPHA * x_ref[...] + _dot(mixed, wout_ref[...]), g1_ref[...], b1_ref[...])
    q = (_dot(x1.astype(BF16), wq_ref[...]) * (XATT_DIM ** -0.5)).astype(BF16)
    heads = []
    for h in range(XATT_HEADS):
        cols = slice(h * XATT_DIM, (h + 1) * XATT_DIM)
        s = _dot_nt(q[:, cols], kv_ref[:, cols])
        m = jnp.max(s, axis=-1, keepdims=True)
        p = jnp.exp(s - m)
        l = jnp.sum(p, axis=-1, keepdims=True)
        heads.append(_dot(p.astype(BF16), kv_ref[:, D_MODEL + h * XATT_DIM:D_MODEL + (h + 1) * XATT_DIM]) / l)
    att = jnp.concatenate(heads, axis=1).astype(BF16)
    x2 = _layer_norm(DN_ALPHA * x1 + _dot(att, wo_ref[...]), g2_ref[...], b2_ref[...])
    x2_ref[...] = x2
    x2p_ref[...] = _pack_halves(x2)


def _postmix(x2d, o_ret, o_nsa, kvx, w_out, w_xq, w_xo, ln1_g, ln1_b, ln2_g, ln2_b, batch, seq):
    n = x2d.shape[0]
    tm = 256 if seq % 256 == 0 else seq
    per_b = seq // tm
    row = lambda w: pl.BlockSpec((tm, w), lambda b, i: (b * per_b + i, 0))
    full = lambda a: pl.BlockSpec(a.shape, lambda b, i: (0,) * a.ndim)
    ws = [w_out.astype(BF16), w_xq.astype(BF16), w_xo.astype(BF16)]
    vecs = [v.reshape(1, D_MODEL) for v in (ln1_g, ln1_b, ln2_g, ln2_b)]
    return pl.pallas_call(
        _postmix_kernel,
        grid=(batch, per_b),
        in_specs=[row(D_MODEL), row(RET_WIDTH), row(NSA_WIDTH),
                  pl.BlockSpec((MEM_LEN, 2 * D_MODEL), lambda b, i: (b, 0))]
                 + [full(w) for w in ws] + [full(v) for v in vecs],
        out_specs=[row(D_MODEL),
                   row(D_MODEL // 2)],
        out_shape=[jax.ShapeDtypeStruct((n, D_MODEL), F32),
                   jax.ShapeDtypeStruct((n, D_MODEL // 2), jnp.uint32)],
        compiler_params=_params(2),
        name="postmix",
    )(x2d, o_ret, o_nsa, kvx, *ws, *vecs)


def _router_kernel(x_ref, wr_ref, bias_ref, e_ref, rank_ref, w_ref, cnt_ref, cntrow_ref, carry_ref, carryrow_ref):
    tn = x_ref.shape[0]
    per = N_EXPERTS // N_GROUPS

    @pl.when(pl.program_id(0) == 0)
    def _():
        carry_ref[...] = jnp.zeros_like(carry_ref)
        carryrow_ref[...] = jnp.zeros_like(carryrow_ref)

    logits = _dot_nt(wr_ref[...], x_ref[...].astype(BF16))
    scores = jax.nn.sigmoid(logits)
    biased = scores + bias_ref[...]
    b3 = biased.reshape(N_GROUPS, per, tn)
    member = lax.broadcasted_iota(jnp.int32, (N_GROUPS, per, tn), 1)
    top1 = jnp.max(b3, axis=1, keepdims=True)
    first1 = jnp.min(jnp.where(b3 == top1, member, per), axis=1, keepdims=True)
    top2 = jnp.max(jnp.where(member == first1, -jnp.inf, b3), axis=1, keepdims=True)
    gscore = top1 + top2
    gidx = lax.broadcasted_iota(jnp.int32, (N_GROUPS, 1, tn), 0)
    gwork = gscore
    for _ in range(TOPK_GROUPS - 1):
        gbest = jnp.max(gwork, axis=0, keepdims=True)
        gfirst = jnp.min(jnp.where(gwork == gbest, gidx, N_GROUPS), axis=0, keepdims=True)
        gwork = jnp.where(gidx == gfirst, -jnp.inf, gwork)
    kth = jnp.max(gwork, axis=0, keepdims=True)
    work = jnp.where(gscore >= kth, b3, NEG).reshape(N_EXPERTS, tn)
    eidx = lax.broadcasted_iota(jnp.int32, (N_EXPERTS, tn), 0)
    picks = []
    chosen = jnp.zeros((N_EXPERTS, tn), F32)
    for _ in range(TOP_K):
        best = jnp.max(work, axis=0, keepdims=True)
        first = jnp.min(jnp.where(work == best, eidx, N_EXPERTS), axis=0, keepdims=True)
        hit = eidx == first
        picks.append((first, hit))
        chosen = jnp.where(hit, 1.0, chosen)
        work = jnp.where(hit, -jnp.inf, work)

    r_i = lax.broadcasted_iota(jnp.int32, (tn, tn), 0)
    c_i = lax.broadcasted_iota(jnp.int32, (tn, tn), 1)
    before = jnp.where(r_i < c_i, 1.0, 0.0).astype(BF16)
    chosen_b = chosen.astype(BF16)
    rank = _dot(chosen_b, before) + carry_ref[...]
    carry_ref[...] = carry_ref[...] + jnp.sum(chosen, axis=1, keepdims=True)
    carryrow_ref[...] = carryrow_ref[...] + _dot_nt(jnp.ones((8, tn), BF16), chosen_b)
    cnt_ref[...] = carry_ref[...]
    cntrow_ref[...] = carryrow_ref[...]

    wsel = [jnp.sum(jnp.where(hit, scores, 0.0), axis=0, keepdims=True) for _, hit in picks]
    wsum = wsel[0]
    for v in wsel[1:]:
        wsum = wsum + v
    for kk, (first, hit) in enumerate(picks):
        e_ref[kk:kk + 1, :] = first
        rank_ref[kk:kk + 1, :] = jnp.sum(jnp.where(hit, rank, 0.0), axis=0, keepdims=True).astype(jnp.int32)
        w_ref[kk:kk + 1, :] = wsel[kk] / wsum * ROUTED_SCALE


def _router(x2, w_router, router_bias):
    n = x2.shape[0]
    tn = 512 if n % 512 == 0 else n
    wr_t = w_router.T.astype(BF16)
    bias = router_bias.reshape(N_EXPERTS, 1).astype(F32)
    kspec = pl.BlockSpec((TOP_K, tn), lambda i: (0, i))
    return pl.pallas_call(
        _router_kernel,
        grid=(n // tn,),
        in_specs=[pl.BlockSpec((tn, D_MODEL), lambda i: (i, 0)),
                  pl.BlockSpec(wr_t.shape, lambda i: (0, 0)),
                  pl.BlockSpec(bias.shape, lambda i: (0, 0))],
        out_specs=[kspec, kspec, kspec, pl.BlockSpec((N_EXPERTS, 1), lambda i: (0, 0)),
                   pl.BlockSpec((8, N_EXPERTS), lambda i: (0, 0))],
        out_shape=[jax.ShapeDtypeStruct((TOP_K, n), jnp.int32),
                   jax.ShapeDtypeStruct((TOP_K, n), jnp.int32),
                   jax.ShapeDtypeStruct((TOP_K, n), F32),
                   jax.ShapeDtypeStruct((N_EXPERTS, 1), F32),
                   jax.ShapeDtypeStruct((8, N_EXPERTS), F32)],
        scratch_shapes=[pltpu.VMEM((N_EXPERTS, 1), F32), pltpu.VMEM((8, N_EXPERTS), F32)],
        compiler_params=_params(1),
        name="router",
    )(x2, wr_t, bias)


def _slots_kernel(e_ref, rank_ref, cnt_ref, cntrow_ref, dest_ref, blk_e_ref, valid_ref, *, blk, n_blocks):
    pad = lambda c: jnp.ceil(c / blk) * blk
    cnt = cnt_ref[...]
    padded = pad(cnt)
    padded_row = pad(cntrow_ref[0:1, :])
    r_i = lax.broadcasted_iota(jnp.int32, (N_EXPERTS, N_EXPERTS), 0)
    c_i = lax.broadcasted_iota(jnp.int32, (N_EXPERTS, N_EXPERTS), 1)
    start = jnp.sum(jnp.where(c_i < r_i, padded_row, 0.0), axis=1, keepdims=True)
    end = start + padded
    e = e_ref[...]
    dest = rank_ref[...]
    for ex in range(N_EXPERTS):
        dest = dest + jnp.where(e == ex, start[ex:ex + 1, :].astype(jnp.int32), 0)
    dest_ref[...] = dest
    bstart = (lax.broadcasted_iota(jnp.int32, (1, n_blocks), 1) * blk).astype(F32)
    owner = jnp.sum(jnp.where(end <= bstart, 1.0, 0.0), axis=0, keepdims=True)
    blk_e_ref[...] = jnp.minimum(owner, N_EXPERTS - 1.0).astype(jnp.int32)
    inside = (start <= bstart) & (bstart < end)
    real = jnp.clip(start + cnt - bstart, 0.0, float(blk))
    valid_ref[...] = jnp.sum(jnp.where(inside, real, 0.0), axis=0, keepdims=True).astype(jnp.int32)


def _slots(e_k, rank_k, counts, counts_row, blk, n_blocks):
    n = e_k.shape[1]
    full = lambda shape: pl.BlockSpec(shape, lambda: (0,) * len(shape))
    return pl.pallas_call(
        functools.partial(_slots_kernel, blk=blk, n_blocks=n_blocks),
        in_specs=[full((TOP_K, n)), full((TOP_K, n)), full((N_EXPERTS, 1)), full((8, N_EXPERTS))],
        out_specs=[full((TOP_K, n)), full((1, n_blocks)), full((1, n_blocks))],
        out_shape=[jax.ShapeDtypeStruct((TOP_K, n), jnp.int32),
                   jax.ShapeDtypeStruct((1, n_blocks), jnp.int32),
                   jax.ShapeDtypeStruct((1, n_blocks), jnp.int32)],
        compiler_params=pltpu.CompilerParams(vmem_limit_bytes=VMEM_LIMIT),
        name="slots",
    )(e_k, rank_k, counts, counts_row)


def _sc_worker_base(per_worker):
    return (lax.axis_index("s") * SC_CORES + lax.axis_index("c")) * per_worker


def _sc_scatter_rows(rows, idx, n_out):
    n, width = rows.shape
    k_lists = idx.shape[0] // n
    workers = SC_CORES * SC_SUBCORES
    per_worker = n // workers
    assert per_worker * workers == n and per_worker % SC_CHUNK == 0
    mesh = plsc.VectorSubcoreMesh(core_axis_name="c", subcore_axis_name="s")

    @functools.partial(
        pl.kernel, mesh=mesh,
        out_type=jax.ShapeDtypeStruct((n_out, width), rows.dtype),
        scratch_types=[pltpu.VMEM((SC_CHUNK, width), rows.dtype)]
                      + [pltpu.VMEM((SC_CHUNK,), jnp.int32)] * k_lists + [pltpu.SemaphoreType.DMA] * 3,
        name="sc_scatter")
    def scatter(rows_hbm, idx_hbm, out_hbm, rows_v, *rest):
        idx_vs = rest[:k_lists]
        sem_rows, sem_idx, sem_out = rest[k_lists:]
        base = _sc_worker_base(per_worker)

        @pl.loop(0, per_worker // SC_CHUNK)
        def _(ci):
            off = pl.multiple_of(base + ci * SC_CHUNK, SC_CHUNK)
            loads = [pltpu.async_copy(rows_hbm.at[pl.ds(off, SC_CHUNK)], rows_v, sem_rows)]
            loads += [pltpu.async_copy(idx_hbm.at[pl.ds(pl.multiple_of(k * n + off, SC_CHUNK), SC_CHUNK)],
                                       idx_vs[k], sem_idx) for k in range(k_lists)]
            for c in loads:
                c.wait()
            copies = [pltpu.async_copy(rows_v, out_hbm.at[idx_vs[k]], sem_out) for k in range(k_lists)]
            for c in copies:
                c.wait()

    return scatter(rows, idx)


def _experts_kernel(blk_e_ref, valid_ref, xs_ref, wg_ref, wu_ref, wd_ref, y_ref):
    del blk_e_ref
    valid = valid_ref[pl.program_id(0)]

    @pl.when(valid > 0)
    def _():
        half = D_MODEL // 2
        row = lax.broadcasted_iota(jnp.int32, (xs_ref.shape[0], 1), 0)
        hi, lo = (v.astype(BF16) for v in _unpack_halves(jnp.where(row < valid, xs_ref[...], jnp.uint32(0))))
        gate = _dot(hi, wg_ref[:half, :]) + _dot(lo, wg_ref[half:, :])
        up = _dot(hi, wu_ref[:half, :]) + _dot(lo, wu_ref[half:, :])
        y_ref[...] = _pack_halves(_dot((jax.nn.silu(gate) * up).astype(BF16), wd_ref[...]))

    @pl.when(valid <= 0)
    def _():
        y_ref[...] = jnp.zeros_like(y_ref)


def _experts(blk_e, valid, xs, w_gate, w_up, w_down, blk):
    cap, width = xs.shape
    wg, wu, wd = (a.astype(BF16) for a in (w_gate, w_up, w_down))
    wspec = lambda a: pl.BlockSpec((None,) + a.shape[1:], lambda i, be, nv: (be[i], 0, 0))
    rows = pl.BlockSpec((blk, width), lambda i, be, nv: (i, 0))
    return pl.pallas_call(
        _experts_kernel,
        grid_spec=pltpu.PrefetchScalarGridSpec(
            num_scalar_prefetch=2,
            grid=(cap // blk,),
            in_specs=[rows, wspec(wg), wspec(wu), wspec(wd)],
            out_specs=rows,
        ),
        out_shape=jax.ShapeDtypeStruct(xs.shape, xs.dtype),
        compiler_params=_params(1),
        name="experts",
    )(blk_e, valid, xs, wg, wu, wd)


def _sc_gather_rows(table, idx):
    b, width = idx.shape[0], table.shape[1]
    workers = SC_CORES * SC_SUBCORES
    per_worker = b // workers
    assert per_worker * workers == b and per_worker % (SC_CHUNK * SC_INFLIGHT) == 0
    mesh = plsc.VectorSubcoreMesh(core_axis_name="c", subcore_axis_name="s")

    @functools.partial(
        pl.kernel, mesh=mesh,
        out_type=jax.ShapeDtypeStruct((b, width), table.dtype),
        scratch_types=[pltpu.VMEM((SC_CHUNK,), jnp.int32)] * SC_INFLIGHT
                      + [pltpu.VMEM((SC_CHUNK, width), table.dtype)] * SC_INFLIGHT
                      + [pltpu.SemaphoreType.DMA] * (1 + 2 * SC_INFLIGHT),
        name="sc_gather")
    def gather(table_hbm, idx_hbm, out_hbm, *scratch):
        idx_vs = scratch[:SC_INFLIGHT]
        rows_vs = scratch[SC_INFLIGHT:2 * SC_INFLIGHT]
        sem_idx = scratch[2 * SC_INFLIGHT]
        sem_rows = scratch[2 * SC_INFLIGHT + 1:3 * SC_INFLIGHT + 1]
        sem_out = scratch[3 * SC_INFLIGHT + 1:]
        base = _sc_worker_base(per_worker)
        lanes = range(SC_INFLIGHT)

        @pl.loop(0, per_worker // (SC_CHUNK * SC_INFLIGHT))
        def _(gi):
            offs = [pl.multiple_of(base + (gi * SC_INFLIGHT + j) * SC_CHUNK, SC_CHUNK) for j in lanes]
            loads = [pltpu.async_copy(idx_hbm.at[pl.ds(offs[j], SC_CHUNK)], idx_vs[j], sem_idx) for j in lanes]
            for c in loads:
                c.wait()
            gathers = [pltpu.async_copy(table_hbm.at[idx_vs[j]], rows_vs[j], sem_rows[j]) for j in lanes]
            writes = []
            for j in lanes:
                gathers[j].wait()
                writes.append(pltpu.async_copy(rows_vs[j], out_hbm.at[pl.ds(offs[j], SC_CHUNK)], sem_out[j]))
            for c in writes:
                c.wait()

    return gather(table, idx)


def _combine_kernel(x_ref, wk_ref, yk_ref, wsg_ref, wsu_ref, wsd_ref, g_ref, b_ref, o_ref):
    x = x_ref[...]
    xb = x.astype(BF16)
    shared = _dot((jax.nn.silu(_dot(xb, wsg_ref[...])) * _dot(xb, wsu_ref[...])).astype(BF16), wsd_ref[...])
    wk = wk_ref[...]
    routed_hi = routed_lo = None
    for kk in range(TOP_K):
        hi, lo = _unpack_halves(yk_ref[kk])
        w = wk[:, kk:kk + 1]
        routed_hi = hi * w if kk == 0 else routed_hi + hi * w
        routed_lo = lo * w if kk == 0 else routed_lo + lo * w
    routed = jnp.concatenate([routed_hi, routed_lo], axis=1)
    o_ref[...] = _layer_norm(DN_ALPHA * x + (routed + shared), g_ref[...], b_ref[...])


def _combine(x2, w_tok, yk, ws_gate, ws_up, ws_down, ln3_g, ln3_b):
    n = x2.shape[0]
    tt = 256 if n % 256 == 0 else n
    ws = [ws_gate.astype(BF16), ws_up.astype(BF16), ws_down.astype(BF16)]
    vecs = [ln3_g.reshape(1, D_MODEL), ln3_b.reshape(1, D_MODEL)]
    full = lambda a: pl.BlockSpec(a.shape, lambda i: (0,) * a.ndim)
    return pl.pallas_call(
        _combine_kernel,
        grid=(n // tt,),
        in_specs=[pl.BlockSpec((tt, D_MODEL), lambda i: (i, 0)),
                  pl.BlockSpec((tt, TOP_K), lambda i: (i, 0)),
                  pl.BlockSpec((TOP_K, tt, D_MODEL // 2), lambda i: (0, i, 0))]
                 + [full(a) for a in ws] + [full(v) for v in vecs],
        out_specs=pl.BlockSpec((tt, D_MODEL), lambda i: (i, 0)),
        out_shape=jax.ShapeDtypeStruct((n, D_MODEL), F32),
        compiler_params=_params(1),
        name="combine",
    )(x2, w_tok, yk, *ws, *vecs)


def _moe_and_norm(x2, x2p, w_router, router_bias, w_gate, w_up, w_down, ws_gate, ws_up, ws_down,
                  ln3_g, ln3_b):
    n = x2.shape[0]
    blk = 512
    cap = n * TOP_K + N_EXPERTS * blk
    n_blocks = cap // blk
    e_k, rank_k, w_k, counts, counts_row = _router(x2, w_router, router_bias)
    dest, blk_e, valid = _slots(e_k, rank_k, counts, counts_row, blk, n_blocks)
    dest = dest.reshape(-1)
    xs = _sc_scatter_rows(x2p, dest, cap)
    y = _experts(blk_e.reshape(-1), valid.reshape(-1), xs, w_gate, w_up, w_down, blk)
    yk = _sc_gather_rows(y, dest).reshape(TOP_K, n, D_MODEL // 2)
    return _combine(x2, w_k.T, yk, ws_gate, ws_up, ws_down, ln3_g, ln3_b)


def _layer(x, mem, positions, w_in, cmp_pe_k, cmp_pe_v, cmp_w1_k, cmp_w2_k, cmp_w1_v, cmp_w2_v,
           w_out, ln1_g, ln1_b, w_xq, w_xkv, w_xo, ln2_g, ln2_b, w_router, router_bias,
           w_gate, w_up, w_down, ws_gate, ws_up, ws_down, ln3_g, ln3_b):
    batch, seq, _ = x.shape
    n = batch * seq
    x2d = x.reshape(n, D_MODEL)
    pos_col = positions.astype(F32).reshape(n, 1)
    (rq, rk, rv, rg, nq, nqr, kc, vc, ks, vs, kw, vw, gates) = _inproj(x2d, pos_col, w_in)
    o_ret = _retention(rq, rk, rv, rg, batch, seq)
    kcmp = _compress(kc, cmp_pe_k, cmp_w1_k, cmp_w2_k, batch, seq)
    vcmp = _compress(vc, cmp_pe_v, cmp_w1_v, cmp_w2_v, batch, seq)
    o_nsa = _nsa(nq, nqr, gates, kcmp, vcmp, ks, vs, kw, vw, batch, seq)
    kvx = _memkv(mem.reshape(batch * MEM_LEN, D_MODEL), w_xkv)
    x2, x2p = _postmix(x2d, o_ret, o_nsa, kvx, w_out, w_xq, w_xo, ln1_g, ln1_b, ln2_g, ln2_b, batch, seq)
    out = _moe_and_norm(x2, x2p, w_router, router_bias, w_gate, w_up, w_down,
                        ws_gate, ws_up, ws_down, ln3_g, ln3_b)
    return out.reshape(batch, seq, D_MODEL)


def kernel(x, mem, positions, w_in, cmp_pe_k, cmp_pe_v, cmp_w1_k, cmp_w2_k, cmp_w1_v, cmp_w2_v, w_out, ln1_g, ln1_b, w_xq, w_xkv, w_xo, ln2_g, ln2_b, w_router, router_bias, w_gate, w_up, w_down, ws_gate, ws_up, ws_down, ln3_g, ln3_b):
    groups = BATCH_GROUPS if x.shape[0] % BATCH_GROUPS == 0 else 1
    per = x.shape[0] // groups
    outs = []
    for gi in range(groups):
        rows = slice(gi * per, (gi + 1) * per)
        xg = x[rows]
        for l in range(DEPTH):
            xg = _layer(xg, mem[rows], positions[rows], w_in[l], cmp_pe_k[l], cmp_pe_v[l], cmp_w1_k[l],
                        cmp_w2_k[l], cmp_w1_v[l], cmp_w2_v[l], w_out[l], ln1_g[l], ln1_b[l], w_xq[l], w_xkv[l],
                        w_xo[l], ln2_g[l], ln2_b[l], w_router[l], router_bias[l], w_gate[l], w_up[l],
                        w_down[l], ws_gate[l], ws_up[l], ws_down[l], ln3_g[l], ln3_b[l])
        outs.append(xg)
    return jnp.concatenate(outs, axis=0)
```

```python
import functools

import numpy as np
import jax
import jax.numpy as jnp
from jax import lax
from jax.experimental import pallas as pl
from jax.experimental.pallas import tpu as pltpu
from jax.experimental.pallas import tpu_sc as plsc

D_MODEL = 1024
MEM_LEN = 256
DEPTH = 1
DN_ALPHA = (2 * DEPTH) ** 0.25
LN_EPS = 1e-5
NEG = -1e30
FORCE = 1e9

RET_HEADS = 4
RET_DIM = 128
RET_CHUNK = 128
RET_ROPE_BASE = 10000.0
RET_WIDTH = RET_HEADS * RET_DIM

NSA_HEADS = 8
NSA_KV_GROUPS = 2
NSA_HPG = NSA_HEADS // NSA_KV_GROUPS
NSA_DIM = 64
NSA_WIDTH = NSA_HEADS * NSA_DIM
KV_WIDTH = NSA_KV_GROUPS * NSA_DIM
CMP_LEN = 32
CMP_STRIDE = 16
CMP_HIDDEN = 256
SEL_LEN = 64
SEL_SHIFT = 6
SEL_TOPK = 16
WIN = 512
ROPE_THETA = 500000.0
ROPE_DIMS = NSA_DIM // 4
GATE_LANES = 16
NSA_CHAINS = 2
SUM_ROWS = 16

BATCH_GROUPS = 2

SC_CORES = 2
SC_SUBCORES = 16
SC_CHUNK = 64
SC_INFLIGHT = 2

XATT_HEADS = 4
XATT_DIM = D_MODEL // XATT_HEADS

N_EXPERTS = 64
TOP_K = 8
N_GROUPS = 8
TOPK_GROUPS = 4
EXPERT_FF = 256
SHARED_FF = 256
ROUTED_SCALE = 2.5

LANES = 128
VMEM_LIMIT = 56 * 1024 * 1024

F32 = jnp.float32
BF16 = jnp.bfloat16
NT_DIMS = (((1,), (1,)), ((), ()))


def _params(n_axes):
    return pltpu.CompilerParams(dimension_semantics=("arbitrary",) * n_axes,
                                vmem_limit_bytes=VMEM_LIMIT)


def _dot(a, b):
    return jnp.dot(a, b, preferred_element_type=F32)


def _dot_nt(a, b):
    return lax.dot_general(a, b, NT_DIMS, preferred_element_type=F32)


def _layer_norm(v, g, b):
    mu = jnp.mean(v, axis=-1, keepdims=True)
    d = v - mu
    var = jnp.mean(d * d, axis=-1, keepdims=True)
    return d * lax.rsqrt(var + LN_EPS) * g + b


def _inproj_kernel(x_ref, pos_ref, wret_ref, wnq_ref, wkv_ref, wg_ref, invr_ref, invn_ref,
                   rq_ref, rk_ref, rv_ref, rg_ref, nq_ref, nqr_ref, kc_ref, vc_ref,
                   ks_ref, vs_ref, kw_ref, vw_ref, gate_ref):
    xb = x_ref[...].astype(BF16)
    pos = pos_ref[...]
    lane = lax.broadcasted_iota(jnp.int32, (1, LANES), 1)

    ang = pos * invr_ref[...]
    cos_r = jnp.cos(ang)
    sin_r = jnp.sin(ang)
    sin_r = jnp.where(lane < RET_DIM // 2, -sin_r, sin_r)
    q_all = _dot(xb, wret_ref[:, :RET_WIDTH])
    k_all = _dot(xb, wret_ref[:, RET_WIDTH:2 * RET_WIDTH])
    for h in range(RET_HEADS):
        cols = slice(h * RET_DIM, (h + 1) * RET_DIM)
        q = q_all[:, cols]
        rq_ref[:, cols] = (q * cos_r + pltpu.roll(q, RET_DIM // 2, 1) * sin_r).astype(BF16)
        k = k_all[:, cols]
        k = (k * cos_r + pltpu.roll(k, RET_DIM // 2, 1) * sin_r) * (RET_DIM ** -0.5)
        rk_ref[:, cols] = k.astype(BF16)
    rv_ref[...] = _dot(xb, wret_ref[:, 2 * RET_WIDTH:3 * RET_WIDTH]).astype(BF16)
    rg_ref[...] = _dot(xb, wret_ref[:, 3 * RET_WIDTH:4 * RET_WIDTH]).astype(BF16)

    half = ROPE_DIMS // 2
    j = lane % NSA_DIM
    angn = pos * invn_ref[...]
    cos_n = jnp.cos(angn)
    sin_n = jnp.sin(angn)
    sin_lo = jnp.where(j < half, -sin_n, 0.0)
    sin_hi = jnp.where((j >= half) & (j < 2 * half), sin_n, 0.0)

    def rope_n(v):
        return v * cos_n + pltpu.roll(v, half, 1) * sin_hi + pltpu.roll(v, LANES - half, 1) * sin_lo

    scale = NSA_DIM ** -0.5
    nq_all = _dot(xb, wnq_ref[...])
    for c in range(NSA_WIDTH // LANES):
        cols = slice(c * LANES, (c + 1) * LANES)
        q = nq_all[:, cols]
        nq_ref[:, cols] = (q * scale).astype(BF16)
        nqr_ref[:, cols] = (rope_n(q) * scale).astype(BF16)

    kv_all = _dot(xb, wkv_ref[...])

    def kv(i):
        return kv_all[:, i * KV_WIDTH:(i + 1) * KV_WIDTH]

    def split_groups(ref, v):
        for g in range(NSA_KV_GROUPS):
            ref[g] = v[:, g * NSA_DIM:(g + 1) * NSA_DIM].astype(BF16)

    kc_ref[...] = kv(0)
    vc_ref[...] = kv(1)
    split_groups(ks_ref, rope_n(kv(2)))
    split_groups(vs_ref, kv(3))
    split_groups(kw_ref, rope_n(kv(4)))
    split_groups(vw_ref, kv(5))

    gt = jax.nn.sigmoid(_dot_nt(wg_ref[...], xb))
    for g in range(NSA_KV_GROUPS):
        gate_ref[g] = gt[g * GATE_LANES:(g + 1) * GATE_LANES, :]


def _inproj(x2d, pos_col, w_in, row0, n):
    tm = 512 if n % 512 == 0 and row0 % 512 == 0 else n
    blk0 = row0 // tm
    off = np.cumsum([0] + [RET_WIDTH] * 4 + [NSA_WIDTH] + [KV_WIDTH] * 6)
    w_ret = w_in[:, :off[4]].astype(BF16)
    w_nq = w_in[:, off[4]:off[5]].astype(BF16)
    w_kv = w_in[:, off[5]:off[11]].astype(BF16)
    wg = w_in[:, off[11]:].reshape(D_MODEL, NSA_KV_GROUPS, NSA_HPG * 3)
    wg = jnp.pad(wg, ((0, 0), (0, 0), (0, GATE_LANES - NSA_HPG * 3)))
    wg = wg.reshape(D_MODEL, NSA_KV_GROUPS * GATE_LANES).T.astype(BF16)

    lane = np.arange(LANES)
    half_r = RET_DIM // 2
    inv_r = (np.float32(RET_ROPE_BASE) ** (-np.arange(half_r, dtype=np.float32) / np.float32(half_r)))
    inv_r = inv_r.astype(np.float32)[lane % half_r][None, :]
    half_n = ROPE_DIMS // 2
    inv_n = (np.float32(ROPE_THETA) ** (-np.arange(half_n, dtype=np.float32) / np.float32(half_n)))
    jn = lane % NSA_DIM
    inv_n = np.where(jn < ROPE_DIMS, inv_n.astype(np.float32)[jn % half_n], np.float32(0.0))[None, :]

    row = lambda w: pl.BlockSpec((tm, w), lambda i: (i, 0))
    src_row = lambda w: pl.BlockSpec((tm, w), lambda i: (blk0 + i, 0))
    full = lambda a: pl.BlockSpec(a.shape, lambda i: (0,) * a.ndim)
    grp = lambda w: pl.BlockSpec((NSA_KV_GROUPS, tm, w), lambda i: (0, i, 0))
    bf = lambda w: jax.ShapeDtypeStruct((n, w), BF16)
    gbf = jax.ShapeDtypeStruct((NSA_KV_GROUPS, n, NSA_DIM), BF16)
    inv_r = jnp.asarray(inv_r, F32)
    inv_n = jnp.asarray(inv_n, F32)
    return pl.pallas_call(
        _inproj_kernel,
        grid=(n // tm,),
        in_specs=[src_row(D_MODEL), src_row(1), full(w_ret), full(w_nq), full(w_kv), full(wg),
                  full(inv_r), full(inv_n)],
        out_specs=[row(RET_WIDTH)] * 4 + [row(NSA_WIDTH)] * 2 + [row(KV_WIDTH)] * 2
                  + [grp(NSA_DIM)] * 4
                  + [pl.BlockSpec((NSA_KV_GROUPS, GATE_LANES, tm), lambda i: (0, 0, i))],
        out_shape=[bf(RET_WIDTH)] * 4 + [bf(NSA_WIDTH)] * 2
                  + [jax.ShapeDtypeStruct((n, KV_WIDTH), F32)] * 2 + [gbf] * 4
                  + [jax.ShapeDtypeStruct((NSA_KV_GROUPS, GATE_LANES, n), F32)],
        compiler_params=_params(1),
        name="inproj",
    )(x2d, pos_col, w_ret, w_nq, w_kv, wg, inv_r, inv_n)


def _retention_kernel(q_ref, k_ref, v_ref, g_ref, o_ref, state_ref):
    c = RET_CHUNK

    @pl.when(pl.program_id(1) == 0)
    def _():
        state_ref[...] = jnp.zeros_like(state_ref)

    row = lax.broadcasted_iota(jnp.int32, (c, c), 0)
    col = lax.broadcasted_iota(jnp.int32, (c, c), 1)
    rel = (row - col).astype(F32)
    idx = lax.broadcasted_iota(jnp.int32, (c, 1), 0).astype(F32)
    for h in range(RET_HEADS):
        log_g = float(np.log(np.float32(1.0) - np.float32(2.0) ** np.float32(-5.0 - h)))
        cols = slice(h * RET_DIM, (h + 1) * RET_DIM)
        q = q_ref[:, cols]
        k = k_ref[:, cols]
        v = v_ref[:, cols]
        dmask = jnp.where(rel >= 0, jnp.exp(log_g * jnp.maximum(rel, 0.0)), 0.0)
        scores = _dot_nt(q, k) * dmask
        inner = _dot(scores.astype(BF16), v)
        zeta = jnp.exp(log_g * (c - 1.0 - idx))
        xi = jnp.exp(log_g * (idx + 1.0))
        prev = state_ref[h]
        cross = _dot(q, prev.astype(BF16)) * xi
        kz = (k.astype(F32) * zeta).astype(BF16)
        kv = lax.dot_general(kz, v, (((0,), (0,)), ((), ())), preferred_element_type=F32)
        state_ref[h] = prev * float(np.exp(np.float32(log_g) * np.float32(c))) + kv
        o = inner + cross
        mu = jnp.mean(o, axis=-1, keepdims=True)
        d = o - mu
        var = jnp.mean(d * d, axis=-1, keepdims=True)
        o = d * lax.rsqrt(var + LN_EPS)
        o_ref[:, cols] = (jax.nn.silu(g_ref[:, cols].astype(F32)) * o).astype(BF16)


def _retention(rq, rk, rv, rg, batch, seq):
    nc = seq // RET_CHUNK
    spec = pl.BlockSpec((RET_CHUNK, RET_WIDTH), lambda b, n: (b * nc + n, 0))
    return pl.pallas_call(
        _retention_kernel,
        grid=(batch, nc),
        in_specs=[spec] * 4,
        out_specs=spec,
        out_shape=jax.ShapeDtypeStruct(rq.shape, BF16),
        scratch_shapes=[pltpu.VMEM((RET_HEADS, RET_DIM, RET_DIM), F32)],
        compiler_params=_params(2),
        name="retention",
    )(rq, rk, rv, rg)


def _compress_kernel(a_ref, pe_ref, w1_ref, w2_ref, o_ref, shift_ref, *, n_cmp):
    rows = a_ref.shape[0]
    a = a_ref[...]
    lo = (a + pe_ref[0]).astype(BF16)
    hi = (a + pe_ref[1]).astype(BF16)
    ridx = lax.broadcasted_iota(jnp.int32, (rows, 1), 0)
    shift_ref[rows:rows + 8, :] = jnp.zeros((8, CMP_HIDDEN), F32)
    for g in range(NSA_KV_GROUPS):
        p = _dot(lo, w1_ref[0, g])
        shift_ref[0:rows, :] = _dot(hi, w1_ref[1, g])
        hid = jax.nn.silu(p + shift_ref[pl.ds(1, rows), :])
        out = _dot(hid.astype(BF16), w2_ref[...])
        o_ref[g] = jnp.where(ridx < n_cmp, out, 0.0).astype(BF16)


def _compress(a, pe, w1, w2, batch, seq):
    rows = seq // CMP_STRIDE
    per = CMP_STRIDE * KV_WIDTH
    n_cmp = (seq - CMP_LEN) // CMP_STRIDE + 1
    a2 = a.reshape(batch * rows, per)
    pe2 = jnp.tile(pe.reshape(2, CMP_STRIDE, 1, NSA_DIM), (1, 1, NSA_KV_GROUPS, 1)).reshape(2, 1, per)
    w1r = w1.reshape(2, CMP_STRIDE, 1, NSA_DIM, CMP_HIDDEN)
    eye = jnp.eye(NSA_KV_GROUPS, dtype=w1.dtype).reshape(1, NSA_KV_GROUPS, 1, NSA_KV_GROUPS, 1, 1)
    w1x = (w1r[:, None] * eye).reshape(2, NSA_KV_GROUPS, per, CMP_HIDDEN).astype(BF16)
    w2b = w2.astype(BF16)
    full = lambda arr: pl.BlockSpec(arr.shape, lambda b: (0,) * arr.ndim)
    return pl.pallas_call(
        functools.partial(_compress_kernel, n_cmp=n_cmp),
        grid=(batch,),
        in_specs=[pl.BlockSpec((rows, per), lambda b: (b, 0)), full(pe2), full(w1x), full(w2b)],
        out_specs=pl.BlockSpec((None, NSA_KV_GROUPS, rows, NSA_DIM), lambda b: (b, 0, 0, 0)),
        out_shape=jax.ShapeDtypeStruct((batch, NSA_KV_GROUPS, rows, NSA_DIM), BF16),
        scratch_shapes=[pltpu.VMEM((rows + 8, CMP_HIDDEN), F32)],
        compiler_params=_params(1),
        name="compress",
    )(a2, pe2, w1x, w2b)


def _heads_to_lanes(ref):
    vt = ref[...].astype(F32).T
    return jnp.concatenate([vt[h * NSA_DIM:(h + 1) * NSA_DIM] for h in range(NSA_HPG)], axis=1).astype(BF16)


def _tile_heads(v):
    return jnp.concatenate([v] * NSA_HPG, axis=1)


def _transpose_into(dst_ref, src_ref, chunk):
    def step(c, _):
        c0 = pl.multiple_of(c * chunk, chunk)
        dst_ref[:NSA_DIM, pl.ds(c0, chunk)] = src_ref[pl.ds(c0, chunk), :].astype(F32).T.astype(BF16)
        return 0
    lax.fori_loop(0, src_ref.shape[0] // chunk, step, 0)


def _nsa_kernel(qraw_ref, qrot_ref, gate_ref, kcmp_ref, vcmp_ref, ovt_ref,
                ks_ref, vs_ref, kw_ref, vw_ref, o_ref, vst_ref, vwt_ref, vct_ref, bias_ref, *, tq, tk, seq):
    i = pl.program_id(2)
    t0 = i * tq
    cols = NSA_HPG * tq
    n_sel = seq // SEL_LEN
    n_cmp_rows = seq // CMP_STRIDE

    @pl.when(i == 0)
    def _():
        chunk = min(512, n_cmp_rows)
        _transpose_into(vst_ref, vs_ref, chunk)
        _transpose_into(vwt_ref, vw_ref, chunk)
        _transpose_into(vct_ref, vcmp_ref, chunk)
        vst_ref[NSA_DIM:, :] = jnp.ones((SUM_ROWS, seq), BF16)
        vwt_ref[NSA_DIM:, :] = jnp.ones((SUM_ROWS, seq), BF16)

    def split_sum(acc):
        return acc[:NSA_DIM] / acc[NSA_DIM:NSA_DIM + 1]

    q_raw = _heads_to_lanes(qraw_ref)
    q_rot = _heads_to_lanes(qrot_ref)
    t_row = t0 + lax.broadcasted_iota(jnp.int32, (1, tq), 1)

    chain_w = cols // NSA_CHAINS
    heads_per_chain = chain_w // tq
    chains = [slice(c * chain_w, (c + 1) * chain_w) for c in range(NSA_CHAINS)]
    tile_chain = lambda v: jnp.concatenate([v] * heads_per_chain, axis=1)

    span = WIN + tq
    ws = pl.multiple_of(jnp.maximum(t0 - WIN, 0), tq)
    dist = t_row - (ws + lax.broadcasted_iota(jnp.int32, (span, 1), 0))
    bias_w = tile_chain(jnp.where((dist >= 0) & (dist < WIN), 0.0, NEG))
    k_w = kw_ref[pl.ds(ws, span), :]
    v_w = vwt_ref[:, pl.ds(ws, span)]
    o_w = []
    for c in chains:
        s_w = _dot(k_w, q_rot[:, c]) + bias_w
        p_w = jnp.exp((s_w - jnp.max(s_w, axis=0, keepdims=True)).astype(BF16))
        o_w.append(split_sum(_dot(v_w, p_w)))
    o_w = jnp.concatenate(o_w, axis=1)

    c_idx = lax.broadcasted_iota(jnp.int32, (n_cmp_rows, 1), 0)
    valid = tile_chain(jnp.where(c_idx * CMP_STRIDE + (CMP_LEN - 1) <= t_row, 1.0, 0.0))
    bias_c = (valid - 1.0) * (-NEG)
    o_c = []
    p_sum = None
    for c in chains:
        s_c = _dot(kcmp_ref[...], q_raw[:, c]) + bias_c
        e_c = jnp.exp(s_c - jnp.max(s_c, axis=0, keepdims=True)) * valid
        l_c = jnp.sum(e_c, axis=0, keepdims=True)
        p_c = e_c / jnp.where(l_c > 0.0, l_c, 1.0)
        o_c.append(_dot(vct_ref[...], p_c.astype(BF16)))
        for h in range(heads_per_chain):
            p_h = p_c[:, h * tq:(h + 1) * tq]
            p_sum = p_h if p_sum is None else p_sum + p_h
    o_c = jnp.concatenate(o_c, axis=1)

    p_hi = p_sum.astype(BF16)
    p_lo = (p_sum - p_hi.astype(F32)).astype(BF16)
    ovt = ovt_ref[...]
    imp = _dot(ovt, p_hi) + _dot(ovt, p_lo)
    jb = lax.broadcasted_iota(jnp.int32, (n_sel, tq), 0)
    cur = (t0 + lax.broadcasted_iota(jnp.int32, (n_sel, tq), 1)) >> SEL_SHIFT
    forced = (jb == 0) | (jb == cur) | (jb == cur - 1)
    work = jnp.where(forced, FORCE, imp)
    work = jnp.where(jb <= cur, work, NEG)
    sel_t = jnp.zeros((n_sel, tq), F32)
    for _ in range(min(SEL_TOPK, n_sel)):
        best = jnp.max(work, axis=0, keepdims=True)
        first = jnp.min(jnp.where(work == best, jb, n_sel), axis=0, keepdims=True)
        hit = jb == first
        sel_t = jnp.where(hit, 1.0, sel_t)
        work = jnp.where(hit, -jnp.inf, work)
    bias_ref[...] = jnp.where(sel_t > 0.5, 0.0, NEG)

    blocks_per_tile = tk // SEL_LEN

    def sel_tile(kt, carry, causal):
        k0 = pl.multiple_of(kt * tk, tk)
        bias = jnp.concatenate(
            [jnp.broadcast_to(bias_ref[pl.ds(kt * blocks_per_tile + j, 1), :], (SEL_LEN, tq))
             for j in range(blocks_per_tile)], axis=0)
        if causal:
            kpos = k0 + lax.broadcasted_iota(jnp.int32, (tk, 1), 0)
            bias = jnp.where(kpos <= t_row, bias, NEG)
        bias = tile_chain(bias)
        k_t = ks_ref[pl.ds(k0, tk), :]
        v_t = vst_ref[:, pl.ds(k0, tk)]
        out = []
        scores = [_dot(k_t, q_rot[:, c]) + bias for c in chains]
        for (m, acc), s in zip(carry, scores):
            m_new = jnp.maximum(m, jnp.max(s, axis=0, keepdims=True))
            p = jnp.exp((s - m_new).astype(BF16))
            acc = jnp.exp(m - m_new) * acc + _dot(v_t, p)
            out.append((m_new, acc))
        return tuple(out)

    n_full = t0 // tk
    init = tuple((jnp.full((1, chain_w), NEG, F32), jnp.zeros((NSA_DIM + SUM_ROWS, chain_w), F32))
                 for _ in chains)
    carry = lax.fori_loop(0, n_full, functools.partial(sel_tile, causal=False), init)
    carry = sel_tile(n_full, carry, causal=True)
    o_s = jnp.concatenate([split_sum(acc) for _, acc in carry], axis=1)

    gt = gate_ref[...]
    outs = []
    for h in range(NSA_HPG):
        c = slice(h * tq, (h + 1) * tq)
        outs.append(gt[3 * h:3 * h + 1] * o_c[:, c] + gt[3 * h + 1:3 * h + 2] * o_s[:, c]
                    + gt[3 * h + 2:3 * h + 3] * o_w[:, c])
    o_ref[...] = jnp.concatenate(outs, axis=0).T.astype(BF16)


def _nsa(nq, nqr, gates, kcmp, vcmp, ks, vs, kw, vw, batch, seq):
    n = batch * seq
    tq = 512
    tk = 512 if seq % 512 == 0 else seq
    nqb = seq // tq
    n_sel = seq // SEL_LEN
    rows_c = seq // CMP_STRIDE
    gw = NSA_HPG * NSA_DIM
    cs = np.arange(rows_c)[None, :] * CMP_STRIDE
    ss = np.arange(n_sel)[:, None] * SEL_LEN
    n_cmp = (seq - CMP_LEN) // CMP_STRIDE + 1
    ovt = ((cs < ss + SEL_LEN) & (cs + CMP_LEN > ss) & (np.arange(rows_c)[None, :] < n_cmp))
    ovt = jnp.asarray(ovt.astype(np.float32), BF16)

    qspec = pl.BlockSpec((tq, gw), lambda b, g, i: (b * nqb + i, g))
    cspec = pl.BlockSpec((None, None, rows_c, NSA_DIM), lambda b, g, i: (b, g, 0, 0))
    kspec = pl.BlockSpec((None, seq, NSA_DIM), lambda b, g, i: (g, b, 0))
    return pl.pallas_call(
        functools.partial(_nsa_kernel, tq=tq, tk=tk, seq=seq),
        grid=(batch, NSA_KV_GROUPS, nqb),
        in_specs=[qspec, qspec,
                  pl.BlockSpec((None, GATE_LANES, tq), lambda b, g, i: (g, 0, b * nqb + i)),
                  cspec, cspec, pl.BlockSpec(ovt.shape, lambda b, g, i: (0, 0)),
                  kspec, kspec, kspec, kspec],
        out_specs=qspec,
        out_shape=jax.ShapeDtypeStruct((n, NSA_WIDTH), BF16),
        scratch_shapes=[pltpu.VMEM((NSA_DIM + SUM_ROWS, seq), BF16), pltpu.VMEM((NSA_DIM + SUM_ROWS, seq), BF16),
                        pltpu.VMEM((NSA_DIM, rows_c), BF16), pltpu.VMEM((n_sel, tq), F32)],
        compiler_params=_params(3),
        name="nsa",
    )(nq, nqr, gates, kcmp, vcmp, ovt, ks, vs, kw, vw)


def _memkv_kernel(mem_ref, w_ref, kv_ref):
    kv_ref[...] = _dot(mem_ref[...].astype(BF16), w_ref[...]).astype(BF16)


def _memkv(mem2d, w_xkv):
    n = mem2d.shape[0]
    w = w_xkv.astype(BF16)
    return pl.pallas_call(
        _memkv_kernel,
        grid=(n // MEM_LEN,),
        in_specs=[pl.BlockSpec((MEM_LEN, D_MODEL), lambda i: (i, 0)),
                  pl.BlockSpec(w.shape, lambda i: (0, 0))],
        out_specs=pl.BlockSpec((MEM_LEN, 2 * D_MODEL), lambda i: (i, 0)),
        out_shape=jax.ShapeDtypeStruct((n, 2 * D_MODEL), BF16),
        compiler_params=_params(1),
        name="memkv",
    )(mem2d, w)


def _pack_halves(v):
    half = D_MODEL // 2
    hi = pltpu.bitcast(v[:, :half].astype(BF16).astype(F32), jnp.uint32)
    lo = pltpu.bitcast(v[:, half:].astype(BF16).astype(F32), jnp.uint32)
    return hi | (lo >> 16)


def _unpack_halves(words):
    return pltpu.bitcast(words & jnp.uint32(0xFFFF0000), F32), pltpu.bitcast(words << 16, F32)


def _postmix_kernel(x_ref, oret_ref, onsa_ref, kv_ref, wout_ref, wq_ref, wo_ref,
                    g1_ref, b1_ref, g2_ref, b2_ref, x2_ref, x2p_ref):
    mixed = jnp.concatenate([oret_ref[...], onsa_ref[...]], axis=1)
    x1 = _layer_norm(DN_ALPHA * x_ref[...] + _dot(mixed, wout_ref[...]), g1_ref[...], b1_ref[...])
    q = (_dot(x1.astype(BF16), wq_ref[...]) * (XATT_DIM ** -0.5)).astype(BF16)
    heads = []
    for h in range(XATT_HEADS):
        cols = slice(h * XATT_DIM, (h + 1) * XATT_DIM)
        s = _dot_nt(q[:, cols], kv_ref[:, cols])
        m = jnp.max(s, axis=-1, keepdims=True)
        p = jnp.exp(s - m)
        l = jnp.sum(p, axis=-1, keepdims=True)
        heads.append(_dot(p.astype(BF16), kv_ref[:, D_MODEL + h * XATT_DIM:D_MODEL + (h + 1) * XATT_DIM]) / l)
    att = jnp.concatenate(heads, axis=1).astype(BF16)
    x2 = _layer_norm(DN_ALPHA * x1 + _dot(att, wo_ref[...]), g2_ref[...], b2_ref[...])
    x2_ref[...] = x2
    x2p_ref[...] = _pack_halves(x2)


def _postmix(x2d, o_ret, o_nsa, kvx, w_out, w_xq, w_xo, ln1_g, ln1_b, ln2_g, ln2_b, batch0, batch, seq):
    n = batch * seq
    tm = 256 if seq % 256 == 0 else seq
    per_b = seq // tm
    row = lambda w: pl.BlockSpec((tm, w), lambda b, i: (b * per_b + i, 0))
    full = lambda a: pl.BlockSpec(a.shape, lambda b, i: (0,) * a.ndim)
    ws = [w_out.astype(BF16), w_xq.astype(BF16), w_xo.astype(BF16)]
    vecs = [v.reshape(1, D_MODEL) for v in (ln1_g, ln1_b, ln2_g, ln2_b)]
    return pl.pallas_call(
        _postmix_kernel,
        grid=(batch, per_b),
        in_specs=[pl.BlockSpec((tm, D_MODEL), lambda b, i: ((batch0 + b) * per_b + i, 0)),
                  row(RET_WIDTH), row(NSA_WIDTH),
                  pl.BlockSpec((MEM_LEN, 2 * D_MODEL), lambda b, i: (batch0 + b, 0))]
                 + [full(w) for w in ws] + [full(v) for v in vecs],
        out_specs=[row(D_MODEL),
                   row(D_MODEL // 2)],
        out_shape=[jax.ShapeDtypeStruct((n, D_MODEL), F32),
                   jax.ShapeDtypeStruct((n, D_MODEL // 2), jnp.uint32)],
        compiler_params=_params(2),
        name="postmix",
    )(x2d, o_ret, o_nsa, kvx, *ws, *vecs)


def _router_kernel(x_ref, wr_ref, bias_ref, e_ref, rank_ref, w_ref, cnt_ref, cntrow_ref, carry_ref, carryrow_ref):
    tn = x_ref.shape[0]
    per = N_EXPERTS // N_GROUPS

    @pl.when(pl.program_id(0) == 0)
    def _():
        carry_ref[...] = jnp.zeros_like(carry_ref)
        carryrow_ref[...] = jnp.zeros_like(carryrow_ref)

    logits = _dot_nt(wr_ref[...], x_ref[...].astype(BF16))
    scores = jax.nn.sigmoid(logits)
    biased = scores + bias_ref[...]
    b3 = biased.reshape(N_GROUPS, per, tn)
    member = lax.broadcasted_iota(jnp.int32, (N_GROUPS, per, tn), 1)
    top1 = jnp.max(b3, axis=1, keepdims=True)
    first1 = jnp.min(jnp.where(b3 == top1, member, per), axis=1, keepdims=True)
    top2 = jnp.max(jnp.where(member == first1, -jnp.inf, b3), axis=1, keepdims=True)
    gscore = top1 + top2
    gidx = lax.broadcasted_iota(jnp.int32, (N_GROUPS, 1, tn), 0)
    gwork = gscore
    for _ in range(TOPK_GROUPS - 1):
        gbest = jnp.max(gwork, axis=0, keepdims=True)
        gfirst = jnp.min(jnp.where(gwork == gbest, gidx, N_GROUPS), axis=0, keepdims=True)
        gwork = jnp.where(gidx == gfirst, -jnp.inf, gwork)
    kth = jnp.max(gwork, axis=0, keepdims=True)
    work = jnp.where(gscore >= kth, b3, NEG).reshape(N_EXPERTS, tn)
    eidx = lax.broadcasted_iota(jnp.int32, (N_EXPERTS, tn), 0)
    picks = []
    chosen = jnp.zeros((N_EXPERTS, tn), F32)
    for _ in range(TOP_K):
        best = jnp.max(work, axis=0, keepdims=True)
        first = jnp.min(jnp.where(work == best, eidx, N_EXPERTS), axis=0, keepdims=True)
        hit = eidx == first
        picks.append((first, hit))
        chosen = jnp.where(hit, 1.0, chosen)
        work = jnp.where(hit, -jnp.inf, work)

    r_i = lax.broadcasted_iota(jnp.int32, (tn, tn), 0)
    c_i = lax.broadcasted_iota(jnp.int32, (tn, tn), 1)
    before = jnp.where(r_i < c_i, 1.0, 0.0).astype(BF16)
    chosen_b = chosen.astype(BF16)
    rank = _dot(chosen_b, before) + carry_ref[...]
    carry_ref[...] = carry_ref[...] + jnp.sum(chosen, axis=1, keepdims=True)
    carryrow_ref[...] = carryrow_ref[...] + _dot_nt(jnp.ones((8, tn), BF16), chosen_b)
    cnt_ref[...] = carry_ref[...]
    cntrow_ref[...] = carryrow_ref[...]

    wsel = [jnp.sum(jnp.where(hit, scores, 0.0), axis=0, keepdims=True) for _, hit in picks]
    wsum = wsel[0]
    for v in wsel[1:]:
        wsum = wsum + v
    for kk, (first, hit) in enumerate(picks):
        e_ref[kk:kk + 1, :] = first
        rank_ref[kk:kk + 1, :] = jnp.sum(jnp.where(hit, rank, 0.0), axis=0, keepdims=True).astype(jnp.int32)
        w_ref[kk:kk + 1, :] = wsel[kk] / wsum * ROUTED_SCALE


def _router(x2, w_router, router_bias):
    n = x2.shape[0]
    tn = 512 if n % 512 == 0 else n
    wr_t = w_router.T.astype(BF16)
    bias = router_bias.reshape(N_EXPERTS, 1).astype(F32)
    kspec = pl.BlockSpec((TOP_K, tn), lambda i: (0, i))
    return pl.pallas_call(
        _router_kernel,
        grid=(n // tn,),
        in_specs=[pl.BlockSpec((tn, D_MODEL), lambda i: (i, 0)),
                  pl.BlockSpec(wr_t.shape, lambda i: (0, 0)),
                  pl.BlockSpec(bias.shape, lambda i: (0, 0))],
        out_specs=[kspec, kspec, kspec, pl.BlockSpec((N_EXPERTS, 1), lambda i: (0, 0)),
                   pl.BlockSpec((8, N_EXPERTS), lambda i: (0, 0))],
        out_shape=[jax.ShapeDtypeStruct((TOP_K, n), jnp.int32),
                   jax.ShapeDtypeStruct((TOP_K, n), jnp.int32),
                   jax.ShapeDtypeStruct((TOP_K, n), F32),
                   jax.ShapeDtypeStruct((N_EXPERTS, 1), F32),
                   jax.ShapeDtypeStruct((8, N_EXPERTS), F32)],
        scratch_shapes=[pltpu.VMEM((N_EXPERTS, 1), F32), pltpu.VMEM((8, N_EXPERTS), F32)],
        compiler_params=_params(1),
        name="router",
    )(x2, wr_t, bias)


def _slots_kernel(e_ref, rank_ref, cnt_ref, cntrow_ref, dest_ref, blk_e_ref, valid_ref, *, blk, n_blocks):
    pad = lambda c: jnp.ceil(c / blk) * blk
    cnt = cnt_ref[...]
    padded = pad(cnt)
    padded_row = pad(cntrow_ref[0:1, :])
    r_i = lax.broadcasted_iota(jnp.int32, (N_EXPERTS, N_EXPERTS), 0)
    c_i = lax.broadcasted_iota(jnp.int32, (N_EXPERTS, N_EXPERTS), 1)
    start = jnp.sum(jnp.where(c_i < r_i, padded_row, 0.0), axis=1, keepdims=True)
    end = start + padded
    e = e_ref[...]
    dest = rank_ref[...]
    for ex in range(N_EXPERTS):
        dest = dest + jnp.where(e == ex, start[ex:ex + 1, :].astype(jnp.int32), 0)
    dest_ref[...] = dest
    bstart = (lax.broadcasted_iota(jnp.int32, (1, n_blocks), 1) * blk).astype(F32)
    owner = jnp.sum(jnp.where(end <= bstart, 1.0, 0.0), axis=0, keepdims=True)
    blk_e_ref[...] = jnp.minimum(owner, N_EXPERTS - 1.0).astype(jnp.int32)
    inside = (start <= bstart) & (bstart < end)
    real = jnp.clip(start + cnt - bstart, 0.0, float(blk))
    valid_ref[...] = jnp.sum(jnp.where(inside, real, 0.0), axis=0, keepdims=True).astype(jnp.int32)


def _slots(e_k, rank_k, counts, counts_row, blk, n_blocks):
    n = e_k.shape[1]
    full = lambda shape: pl.BlockSpec(shape, lambda: (0,) * len(shape))
    return pl.pallas_call(
        functools.partial(_slots_kernel, blk=blk, n_blocks=n_blocks),
        in_specs=[full((TOP_K, n)), full((TOP_K, n)), full((N_EXPERTS, 1)), full((8, N_EXPERTS))],
        out_specs=[full((TOP_K, n)), full((1, n_blocks)), full((1, n_blocks))],
        out_shape=[jax.ShapeDtypeStruct((TOP_K, n), jnp.int32),
                   jax.ShapeDtypeStruct((1, n_blocks), jnp.int32),
                   jax.ShapeDtypeStruct((1, n_blocks), jnp.int32)],
        compiler_params=pltpu.CompilerParams(vmem_limit_bytes=VMEM_LIMIT),
        name="slots",
    )(e_k, rank_k, counts, counts_row)


def _sc_worker_base(per_worker):
    return (lax.axis_index("s") * SC_CORES + lax.axis_index("c")) * per_worker


def _sc_scatter_rows(rows, idx, n_out):
    n, width = rows.shape
    k_lists = idx.shape[0] // n
    workers = SC_CORES * SC_SUBCORES
    per_worker = n // workers
    assert per_worker * workers == n and per_worker % SC_CHUNK == 0
    mesh = plsc.VectorSubcoreMesh(core_axis_name="c", subcore_axis_name="s")

    @functools.partial(
        pl.kernel, mesh=mesh,
        out_type=jax.ShapeDtypeStruct((n_out, width), rows.dtype),
        scratch_types=[pltpu.VMEM((SC_CHUNK, width), rows.dtype)]
                      + [pltpu.VMEM((SC_CHUNK,), jnp.int32)] * k_lists + [pltpu.SemaphoreType.DMA] * 3,
        name="sc_scatter")
    def scatter(rows_hbm, idx_hbm, out_hbm, rows_v, *rest):
        idx_vs = rest[:k_lists]
        sem_rows, sem_idx, sem_out = rest[k_lists:]
        base = _sc_worker_base(per_worker)

        @pl.loop(0, per_worker // SC_CHUNK)
        def _(ci):
            off = pl.multiple_of(base + ci * SC_CHUNK, SC_CHUNK)
            loads = [pltpu.async_copy(rows_hbm.at[pl.ds(off, SC_CHUNK)], rows_v, sem_rows)]
            loads += [pltpu.async_copy(idx_hbm.at[pl.ds(pl.multiple_of(k * n + off, SC_CHUNK), SC_CHUNK)],
                                       idx_vs[k], sem_idx) for k in range(k_lists)]
            for c in loads:
                c.wait()
            copies = [pltpu.async_copy(rows_v, out_hbm.at[idx_vs[k]], sem_out) for k in range(k_lists)]
            for c in copies:
                c.wait()

    return scatter(rows, idx)


def _experts_kernel(blk_e_ref, valid_ref, xs_ref, wg_ref, wu_ref, wd_ref, y_ref):
    del blk_e_ref
    valid = valid_ref[pl.program_id(0)]

    @pl.when(valid > 0)
    def _():
        half = D_MODEL // 2
        row = lax.broadcasted_iota(jnp.int32, (xs_ref.shape[0], 1), 0)
        hi, lo = (v.astype(BF16) for v in _unpack_halves(jnp.where(row < valid, xs_ref[...], jnp.uint32(0))))
        gate = _dot(hi, wg_ref[:half, :]) + _dot(lo, wg_ref[half:, :])
        up = _dot(hi, wu_ref[:half, :]) + _dot(lo, wu_ref[half:, :])
        y_ref[...] = _pack_halves(_dot((jax.nn.silu(gate) * up).astype(BF16), wd_ref[...]))

    @pl.when(valid <= 0)
    def _():
        y_ref[...] = jnp.zeros_like(y_ref)


def _experts(blk_e, valid, xs, w_gate, w_up, w_down, blk):
    cap, width = xs.shape
    wg, wu, wd = (a.astype(BF16) for a in (w_gate, w_up, w_down))
    wspec = lambda a: pl.BlockSpec((None,) + a.shape[1:], lambda i, be, nv: (be[i], 0, 0))
    rows = pl.BlockSpec((blk, width), lambda i, be, nv: (i, 0))
    return pl.pallas_call(
        _experts_kernel,
        grid_spec=pltpu.PrefetchScalarGridSpec(
            num_scalar_prefetch=2,
            grid=(cap // blk,),
            in_specs=[rows, wspec(wg), wspec(wu), wspec(wd)],
            out_specs=rows,
        ),
        out_shape=jax.ShapeDtypeStruct(xs.shape, xs.dtype),
        compiler_params=_params(1),
        name="experts",
    )(blk_e, valid, xs, wg, wu, wd)


def _sc_gather_rows(table, idx):
    b, width = idx.shape[0], table.shape[1]
    workers = SC_CORES * SC_SUBCORES
    per_worker = b // workers
    assert per_worker * workers == b and per_worker % (SC_CHUNK * SC_INFLIGHT) == 0
    mesh = plsc.VectorSubcoreMesh(core_axis_name="c", subcore_axis_name="s")

    @functools.partial(
        pl.kernel, mesh=mesh,
        out_type=jax.ShapeDtypeStruct((b, width), table.dtype),
        scratch_types=[pltpu.VMEM((SC_CHUNK,), jnp.int32)] * SC_INFLIGHT
                      + [pltpu.VMEM((SC_CHUNK, width), table.dtype)] * SC_INFLIGHT
                      + [pltpu.SemaphoreType.DMA] * (1 + 2 * SC_INFLIGHT),
        name="sc_gather")
    def gather(table_hbm, idx_hbm, out_hbm, *scratch):
        idx_vs = scratch[:SC_INFLIGHT]
        rows_vs = scratch[SC_INFLIGHT:2 * SC_INFLIGHT]
        sem_idx = scratch[2 * SC_INFLIGHT]
        sem_rows = scratch[2 * SC_INFLIGHT + 1:3 * SC_INFLIGHT + 1]
        sem_out = scratch[3 * SC_INFLIGHT + 1:]
        base = _sc_worker_base(per_worker)
        lanes = range(SC_INFLIGHT)

        @pl.loop(0, per_worker // (SC_CHUNK * SC_INFLIGHT))
        def _(gi):
            offs = [pl.multiple_of(base + (gi * SC_INFLIGHT + j) * SC_CHUNK, SC_CHUNK) for j in lanes]
            loads = [pltpu.async_copy(idx_hbm.at[pl.ds(offs[j], SC_CHUNK)], idx_vs[j], sem_idx) for j in lanes]
            for c in loads:
                c.wait()
            gathers = [pltpu.async_copy(table_hbm.at[idx_vs[j]], rows_vs[j], sem_rows[j]) for j in lanes]
            writes = []
            for j in lanes:
                gathers[j].wait()
                writes.append(pltpu.async_copy(rows_vs[j], out_hbm.at[pl.ds(offs[j], SC_CHUNK)], sem_out[j]))
            for c in writes:
                c.wait()

    return gather(table, idx)


def _combine_kernel(x_ref, wk_ref, yk_ref, wsg_ref, wsu_ref, wsd_ref, g_ref, b_ref, *rest):
    o_ref = rest[-1]
    x = x_ref[...]
    xb = x.astype(BF16)
    shared = _dot((jax.nn.silu(_dot(xb, wsg_ref[...])) * _dot(xb, wsu_ref[...])).astype(BF16), wsd_ref[...])
    wk = wk_ref[...]
    routed_hi = routed_lo = None
    for kk in range(TOP_K):
        hi, lo = _unpack_halves(yk_ref[kk])
        w = wk[:, kk:kk + 1]
        routed_hi = hi * w if kk == 0 else routed_hi + hi * w
        routed_lo = lo * w if kk == 0 else routed_lo + lo * w
    routed = jnp.concatenate([routed_hi, routed_lo], axis=1)
    o_ref[...] = _layer_norm(DN_ALPHA * x + (routed + shared), g_ref[...], b_ref[...])


def _combine(x2, w_tok, yk, ws_gate, ws_up, ws_down, ln3_g, ln3_b, row0, n_total, out_prev):
    n = x2.shape[0]
    tt = 256 if n % 256 == 0 and row0 % 256 == 0 else n
    blk0 = row0 // tt
    ws = [ws_gate.astype(BF16), ws_up.astype(BF16), ws_down.astype(BF16)]
    vecs = [ln3_g.reshape(1, D_MODEL), ln3_b.reshape(1, D_MODEL)]
    full = lambda a: pl.BlockSpec(a.shape, lambda i: (0,) * a.ndim)
    args = [x2, w_tok, yk, *ws, *vecs]
    in_specs = ([pl.BlockSpec((tt, D_MODEL), lambda i: (i, 0)),
                 pl.BlockSpec((tt, TOP_K), lambda i: (i, 0)),
                 pl.BlockSpec((TOP_K, tt, D_MODEL // 2), lambda i: (0, i, 0))]
                + [full(a) for a in ws] + [full(v) for v in vecs])
    aliases = {}
    if out_prev is not None:
        aliases = {len(args): 0}
        args.append(out_prev)
        in_specs.append(pl.BlockSpec(memory_space=pl.ANY))
    return pl.pallas_call(
        _combine_kernel,
        grid=(n // tt,),
        in_specs=in_specs,
        out_specs=pl.BlockSpec((tt, D_MODEL), lambda i: (blk0 + i, 0)),
        out_shape=jax.ShapeDtypeStruct((n_total, D_MODEL), F32),
        input_output_aliases=aliases,
        compiler_params=_params(1),
        name="combine",
    )(*args)


def _moe_and_norm(x2, x2p, w_router, router_bias, w_gate, w_up, w_down, ws_gate, ws_up, ws_down,
                  ln3_g, ln3_b, row0, n_total, out_prev):
    n = x2.shape[0]
    blk = 512
    cap = n * TOP_K + N_EXPERTS * blk
    n_blocks = cap // blk
    e_k, rank_k, w_k, counts, counts_row = _router(x2, w_router, router_bias)
    dest, blk_e, valid = _slots(e_k, rank_k, counts, counts_row, blk, n_blocks)
    dest = dest.reshape(-1)
    xs = _sc_scatter_rows(x2p, dest, cap)
    y = _experts(blk_e.reshape(-1), valid.reshape(-1), xs, w_gate, w_up, w_down, blk)
    yk = _sc_gather_rows(y, dest).reshape(TOP_K, n, D_MODEL // 2)
    return _combine(x2, w_k.T, yk, ws_gate, ws_up, ws_down, ln3_g, ln3_b, row0, n_total, out_prev)


def _layer(x, mem, positions, w_in, cmp_pe_k, cmp_pe_v, cmp_w1_k, cmp_w2_k, cmp_w1_v, cmp_w2_v,
           w_out, ln1_g, ln1_b, w_xq, w_xkv, w_xo, ln2_g, ln2_b, w_router, router_bias,
           w_gate, w_up, w_down, ws_gate, ws_up, ws_down, ln3_g, ln3_b):
    batch, seq, _ = x.shape
    n_total = batch * seq
    x2d = x.reshape(n_total, D_MODEL)
    pos_col = positions.astype(F32).reshape(n_total, 1)
    kvx = _memkv(mem.reshape(batch * MEM_LEN, D_MODEL), w_xkv)
    groups = BATCH_GROUPS if batch % BATCH_GROUPS == 0 else 1
    per = batch // groups
    n = per * seq
    out = None
    for gi in range(groups):
        (rq, rk, rv, rg, nq, nqr, kc, vc, ks, vs, kw, vw, gates) = _inproj(x2d, pos_col, w_in, gi * n, n)
        o_ret = _retention(rq, rk, rv, rg, per, seq)
        kcmp = _compress(kc, cmp_pe_k, cmp_w1_k, cmp_w2_k, per, seq)
        vcmp = _compress(vc, cmp_pe_v, cmp_w1_v, cmp_w2_v, per, seq)
        o_nsa = _nsa(nq, nqr, gates, kcmp, vcmp, ks, vs, kw, vw, per, seq)
        x2, x2p = _postmix(x2d, o_ret, o_nsa, kvx, w_out, w_xq, w_xo, ln1_g, ln1_b, ln2_g, ln2_b,
                           gi * per, per, seq)
        out = _moe_and_norm(x2, x2p, w_router, router_bias, w_gate, w_up, w_down,
                            ws_gate, ws_up, ws_down, ln3_g, ln3_b, gi * n, n_total, out)
    return out.reshape(batch, seq, D_MODEL)


def kernel(x, mem, positions, w_in, cmp_pe_k, cmp_pe_v, cmp_w1_k, cmp_w2_k, cmp_w1_v, cmp_w2_v, w_out, ln1_g, ln1_b, w_xq, w_xkv, w_xo, ln2_g, ln2_b, w_router, router_bias, w_gate, w_up, w_down, ws_gate, ws_up, ws_down, ln3_g, ln3_b):
    for l in range(DEPTH):
        x = _layer(x, mem, positions, w_in[l], cmp_pe_k[l], cmp_pe_v[l], cmp_w1_k[l], cmp_w2_k[l],
                   cmp_w1_v[l], cmp_w2_v[l], w_out[l], ln1_g[l], ln1_b[l], w_xq[l], w_xkv[l],
                   w_xo[l], ln2_g[l], ln2_b[l], w_router[l], router_bias[l], w_gate[l], w_up[l],
                   w_down[l], ws_gate[l], ws_up[l], ws_down[l], ln3_g[l], ln3_b[l])
    return x
```

```python
import functools

import numpy as np
import jax
import jax.numpy as jnp
from jax import lax
from jax.experimental import pallas as pl
from jax.experimental.pallas import tpu as pltpu
from jax.experimental.pallas import tpu_sc as plsc

D_MODEL = 1024
MEM_LEN = 256
DEPTH = 1
DN_ALPHA = (2 * DEPTH) ** 0.25
LN_EPS = 1e-5
NEG = -1e30
FORCE = 1e9

RET_HEADS = 4
RET_DIM = 128
RET_CHUNK = 128
RET_ROPE_BASE = 10000.0
RET_WIDTH = RET_HEADS * RET_DIM

NSA_HEADS = 8
NSA_KV_GROUPS = 2
NSA_HPG = NSA_HEADS // NSA_KV_GROUPS
NSA_DIM = 64
NSA_WIDTH = NSA_HEADS * NSA_DIM
KV_WIDTH = NSA_KV_GROUPS * NSA_DIM
CMP_LEN = 32
CMP_STRIDE = 16
CMP_HIDDEN = 256
SEL_LEN = 64
SEL_SHIFT = 6
SEL_TOPK = 16
WIN = 512
ROPE_THETA = 500000.0
ROPE_DIMS = NSA_DIM // 4
GATE_LANES = 16
NSA_CHAINS = 2
SUM_ROWS = 16
WIN_PART = 256

BATCH_GROUPS = 2

SC_CORES = 2
SC_SUBCORES = 16
SC_CHUNK = 64
SC_INFLIGHT = 2

XATT_HEADS = 4
XATT_DIM = D_MODEL // XATT_HEADS

N_EXPERTS = 64
TOP_K = 8
N_GROUPS = 8
TOPK_GROUPS = 4
EXPERT_FF = 256
SHARED_FF = 256
ROUTED_SCALE = 2.5

LANES = 128
VMEM_LIMIT = 56 * 1024 * 1024

F32 = jnp.float32
BF16 = jnp.bfloat16
NT_DIMS = (((1,), (1,)), ((), ()))


def _params(n_axes):
    return pltpu.CompilerParams(dimension_semantics=("arbitrary",) * n_axes,
                                vmem_limit_bytes=VMEM_LIMIT)


def _dot(a, b):
    return jnp.dot(a, b, preferred_element_type=F32)


def _dot_nt(a, b):
    return lax.dot_general(a, b, NT_DIMS, preferred_element_type=F32)


def _layer_norm(v, g, b):
    mu = jnp.mean(v, axis=-1, keepdims=True)
    d = v - mu
    var = jnp.mean(d * d, axis=-1, keepdims=True)
    return d * lax.rsqrt(var + LN_EPS) * g + b


def _inproj_kernel(x_ref, pos_ref, wret_ref, wnq_ref, wkv_ref, wg_ref, invr_ref, invn_ref,
                   rq_ref, rk_ref, rv_ref, rg_ref, nq_ref, nqr_ref, kc_ref, vc_ref,
                   ks_ref, vs_ref, kw_ref, vw_ref, gate_ref):
    xb = x_ref[...].astype(BF16)
    pos = pos_ref[...]
    lane = lax.broadcasted_iota(jnp.int32, (1, LANES), 1)

    ang = pos * invr_ref[...]
    cos_r = jnp.cos(ang)
    sin_r = jnp.sin(ang)
    sin_r = jnp.where(lane < RET_DIM // 2, -sin_r, sin_r)
    q_all = _dot(xb, wret_ref[:, :RET_WIDTH])
    k_all = _dot(xb, wret_ref[:, RET_WIDTH:2 * RET_WIDTH])
    for h in range(RET_HEADS):
        cols = slice(h * RET_DIM, (h + 1) * RET_DIM)
        q = q_all[:, cols]
        rq_ref[:, cols] = (q * cos_r + pltpu.roll(q, RET_DIM // 2, 1) * sin_r).astype(BF16)
        k = k_all[:, cols]
        k = (k * cos_r + pltpu.roll(k, RET_DIM // 2, 1) * sin_r) * (RET_DIM ** -0.5)
        rk_ref[:, cols] = k.astype(BF16)
    rv_ref[...] = _dot(xb, wret_ref[:, 2 * RET_WIDTH:3 * RET_WIDTH]).astype(BF16)
    rg_ref[...] = _dot(xb, wret_ref[:, 3 * RET_WIDTH:4 * RET_WIDTH]).astype(BF16)

    half = ROPE_DIMS // 2
    j = lane % NSA_DIM
    angn = pos * invn_ref[...]
    cos_n = jnp.cos(angn)
    sin_n = jnp.sin(angn)
    sin_lo = jnp.where(j < half, -sin_n, 0.0)
    sin_hi = jnp.where((j >= half) & (j < 2 * half), sin_n, 0.0)

    def rope_n(v):
        return v * cos_n + pltpu.roll(v, half, 1) * sin_hi + pltpu.roll(v, LANES - half, 1) * sin_lo

    scale = NSA_DIM ** -0.5
    nq_all = _dot(xb, wnq_ref[...])
    for c in range(NSA_WIDTH // LANES):
        cols = slice(c * LANES, (c + 1) * LANES)
        q = nq_all[:, cols]
        nq_ref[:, cols] = (q * scale).astype(BF16)
        nqr_ref[:, cols] = (rope_n(q) * scale).astype(BF16)

    kv_all = _dot(xb, wkv_ref[...])

    def kv(i):
        return kv_all[:, i * KV_WIDTH:(i + 1) * KV_WIDTH]

    def split_groups(ref, v):
        for g in range(NSA_KV_GROUPS):
            ref[g] = v[:, g * NSA_DIM:(g + 1) * NSA_DIM].astype(BF16)

    kc_ref[...] = kv(0)
    vc_ref[...] = kv(1)
    split_groups(ks_ref, rope_n(kv(2)))
    split_groups(vs_ref, kv(3))
    split_groups(kw_ref, rope_n(kv(4)))
    split_groups(vw_ref, kv(5))

    gt = jax.nn.sigmoid(_dot_nt(wg_ref[...], xb))
    for g in range(NSA_KV_GROUPS):
        gate_ref[g] = gt[g * GATE_LANES:(g + 1) * GATE_LANES, :]


def _inproj(x2d, pos_col, w_in, row0, n):
    tm = 512 if n % 512 == 0 and row0 % 512 == 0 else n
    blk0 = row0 // tm
    off = np.cumsum([0] + [RET_WIDTH] * 4 + [NSA_WIDTH] + [KV_WIDTH] * 6)
    w_ret = w_in[:, :off[4]].astype(BF16)
    w_nq = w_in[:, off[4]:off[5]].astype(BF16)
    w_kv = w_in[:, off[5]:off[11]].astype(BF16)
    wg = w_in[:, off[11]:].reshape(D_MODEL, NSA_KV_GROUPS, NSA_HPG * 3)
    wg = jnp.pad(wg, ((0, 0), (0, 0), (0, GATE_LANES - NSA_HPG * 3)))
    wg = wg.reshape(D_MODEL, NSA_KV_GROUPS * GATE_LANES).T.astype(BF16)

    lane = np.arange(LANES)
    half_r = RET_DIM // 2
    inv_r = (np.float32(RET_ROPE_BASE) ** (-np.arange(half_r, dtype=np.float32) / np.float32(half_r)))
    inv_r = inv_r.astype(np.float32)[lane % half_r][None, :]
    half_n = ROPE_DIMS // 2
    inv_n = (np.float32(ROPE_THETA) ** (-np.arange(half_n, dtype=np.float32) / np.float32(half_n)))
    jn = lane % NSA_DIM
    inv_n = np.where(jn < ROPE_DIMS, inv_n.astype(np.float32)[jn % half_n], np.float32(0.0))[None, :]

    row = lambda w: pl.BlockSpec((tm, w), lambda i: (i, 0))
    src_row = lambda w: pl.BlockSpec((tm, w), lambda i: (blk0 + i, 0))
    full = lambda a: pl.BlockSpec(a.shape, lambda i: (0,) * a.ndim)
    grp = lambda w: pl.BlockSpec((NSA_KV_GROUPS, tm, w), lambda i: (0, i, 0))
    bf = lambda w: jax.ShapeDtypeStruct((n, w), BF16)
    gbf = jax.ShapeDtypeStruct((NSA_KV_GROUPS, n, NSA_DIM), BF16)
    inv_r = jnp.asarray(inv_r, F32)
    inv_n = jnp.asarray(inv_n, F32)
    return pl.pallas_call(
        _inproj_kernel,
        grid=(n // tm,),
        in_specs=[src_row(D_MODEL), src_row(1), full(w_ret), full(w_nq), full(w_kv), full(wg),
                  full(inv_r), full(inv_n)],
        out_specs=[row(RET_WIDTH)] * 4 + [row(NSA_WIDTH)] * 2 + [row(KV_WIDTH)] * 2
                  + [grp(NSA_DIM)] * 4
                  + [pl.BlockSpec((NSA_KV_GROUPS, GATE_LANES, tm), lambda i: (0, 0, i))],
        out_shape=[bf(RET_WIDTH)] * 4 + [bf(NSA_WIDTH)] * 2
                  + [jax.ShapeDtypeStruct((n, KV_WIDTH), F32)] * 2 + [gbf] * 4
                  + [jax.ShapeDtypeStruct((NSA_KV_GROUPS, GATE_LANES, n), F32)],
        compiler_params=_params(1),
        name="inproj",
    )(x2d, pos_col, w_ret, w_nq, w_kv, wg, inv_r, inv_n)


def _retention_kernel(q_ref, k_ref, v_ref, g_ref, o_ref, state_ref):
    c = RET_CHUNK

    @pl.when(pl.program_id(1) == 0)
    def _():
        state_ref[...] = jnp.zeros_like(state_ref)

    row = lax.broadcasted_iota(jnp.int32, (c, c), 0)
    col = lax.broadcasted_iota(jnp.int32, (c, c), 1)
    rel = (row - col).astype(F32)
    idx = lax.broadcasted_iota(jnp.int32, (c, 1), 0).astype(F32)
    for h in range(RET_HEADS):
        log_g = float(np.log(np.float32(1.0) - np.float32(2.0) ** np.float32(-5.0 - h)))
        cols = slice(h * RET_DIM, (h + 1) * RET_DIM)
        q = q_ref[:, cols]
        k = k_ref[:, cols]
        v = v_ref[:, cols]
        dmask = jnp.where(rel >= 0, jnp.exp(log_g * jnp.maximum(rel, 0.0)), 0.0)
        scores = _dot_nt(q, k) * dmask
        inner = _dot(scores.astype(BF16), v)
        zeta = jnp.exp(log_g * (c - 1.0 - idx))
        xi = jnp.exp(log_g * (idx + 1.0))
        prev = state_ref[h]
        cross = _dot(q, prev.astype(BF16)) * xi
        kz = (k.astype(F32) * zeta).astype(BF16)
        kv = lax.dot_general(kz, v, (((0,), (0,)), ((), ())), preferred_element_type=F32)
        state_ref[h] = prev * float(np.exp(np.float32(log_g) * np.float32(c))) + kv
        o = inner + cross
        mu = jnp.mean(o, axis=-1, keepdims=True)
        d = o - mu
        var = jnp.mean(d * d, axis=-1, keepdims=True)
        o = d * lax.rsqrt(var + LN_EPS)
        o_ref[:, cols] = (jax.nn.silu(g_ref[:, cols].astype(F32)) * o).astype(BF16)


def _retention(rq, rk, rv, rg, batch, seq):
    nc = seq // RET_CHUNK
    spec = pl.BlockSpec((RET_CHUNK, RET_WIDTH), lambda b, n: (b * nc + n, 0))
    return pl.pallas_call(
        _retention_kernel,
        grid=(batch, nc),
        in_specs=[spec] * 4,
        out_specs=spec,
        out_shape=jax.ShapeDtypeStruct(rq.shape, BF16),
        scratch_shapes=[pltpu.VMEM((RET_HEADS, RET_DIM, RET_DIM), F32)],
        compiler_params=_params(2),
        name="retention",
    )(rq, rk, rv, rg)


def _compress_kernel(a_ref, pe_ref, w1_ref, w2_ref, o_ref, shift_ref, *, n_cmp):
    rows = a_ref.shape[0]
    a = a_ref[...]
    lo = (a + pe_ref[0]).astype(BF16)
    hi = (a + pe_ref[1]).astype(BF16)
    ridx = lax.broadcasted_iota(jnp.int32, (rows, 1), 0)
    shift_ref[rows:rows + 8, :] = jnp.zeros((8, CMP_HIDDEN), F32)
    for g in range(NSA_KV_GROUPS):
        p = _dot(lo, w1_ref[0, g])
        shift_ref[0:rows, :] = _dot(hi, w1_ref[1, g])
        hid = jax.nn.silu(p + shift_ref[pl.ds(1, rows), :])
        out = _dot(hid.astype(BF16), w2_ref[...])
        o_ref[g] = jnp.where(ridx < n_cmp, out, 0.0).astype(BF16)


def _compress(a, pe, w1, w2, batch, seq):
    rows = seq // CMP_STRIDE
    per = CMP_STRIDE * KV_WIDTH
    n_cmp = (seq - CMP_LEN) // CMP_STRIDE + 1
    a2 = a.reshape(batch * rows, per)
    pe2 = jnp.tile(pe.reshape(2, CMP_STRIDE, 1, NSA_DIM), (1, 1, NSA_KV_GROUPS, 1)).reshape(2, 1, per)
    w1r = w1.reshape(2, CMP_STRIDE, 1, NSA_DIM, CMP_HIDDEN)
    eye = jnp.eye(NSA_KV_GROUPS, dtype=w1.dtype).reshape(1, NSA_KV_GROUPS, 1, NSA_KV_GROUPS, 1, 1)
    w1x = (w1r[:, None] * eye).reshape(2, NSA_KV_GROUPS, per, CMP_HIDDEN).astype(BF16)
    w2b = w2.astype(BF16)
    full = lambda arr: pl.BlockSpec(arr.shape, lambda b: (0,) * arr.ndim)
    return pl.pallas_call(
        functools.partial(_compress_kernel, n_cmp=n_cmp),
        grid=(batch,),
        in_specs=[pl.BlockSpec((rows, per), lambda b: (b, 0)), full(pe2), full(w1x), full(w2b)],
        out_specs=pl.BlockSpec((None, NSA_KV_GROUPS, rows, NSA_DIM), lambda b: (b, 0, 0, 0)),
        out_shape=jax.ShapeDtypeStruct((batch, NSA_KV_GROUPS, rows, NSA_DIM), BF16),
        scratch_shapes=[pltpu.VMEM((rows + 8, CMP_HIDDEN), F32)],
        compiler_params=_params(1),
        name="compress",
    )(a2, pe2, w1x, w2b)


def _heads_to_lanes(ref):
    vt = ref[...].astype(F32).T
    return jnp.concatenate([vt[h * NSA_DIM:(h + 1) * NSA_DIM] for h in range(NSA_HPG)], axis=1).astype(BF16)


def _tile_heads(v):
    return jnp.concatenate([v] * NSA_HPG, axis=1)


def _transpose_into(dst_ref, src_ref, chunk):
    def step(c, _):
        c0 = pl.multiple_of(c * chunk, chunk)
        dst_ref[:NSA_DIM, pl.ds(c0, chunk)] = src_ref[pl.ds(c0, chunk), :].astype(F32).T.astype(BF16)
        return 0
    lax.fori_loop(0, src_ref.shape[0] // chunk, step, 0)


def _nsa_kernel(qraw_ref, qrot_ref, gate_ref, kcmp_ref, vcmp_ref, ovt_ref,
                ks_ref, vs_ref, kw_ref, vw_ref, o_ref, vst_ref, vwt_ref, vct_ref, bias_ref, *, tq, tk, seq):
    i = pl.program_id(2)
    t0 = i * tq
    cols = NSA_HPG * tq
    n_sel = seq // SEL_LEN
    n_cmp_rows = seq // CMP_STRIDE
    blocks_per_tile = tk // SEL_LEN

    @pl.when(i == 0)
    def _():
        chunk = min(512, n_cmp_rows)
        _transpose_into(vst_ref, vs_ref, chunk)
        _transpose_into(vwt_ref, vw_ref, chunk)
        _transpose_into(vct_ref, vcmp_ref, chunk)
        vst_ref[NSA_DIM:, :] = jnp.ones((SUM_ROWS, seq), BF16)
        vwt_ref[NSA_DIM:, :] = jnp.ones((SUM_ROWS, seq), BF16)

    def split_sum(acc):
        return acc[:NSA_DIM] / acc[NSA_DIM:NSA_DIM + 1]

    q_raw = _heads_to_lanes(qraw_ref)
    q_rot = _heads_to_lanes(qrot_ref)
    t_row = t0 + lax.broadcasted_iota(jnp.int32, (1, tq), 1)

    chain_w = cols // NSA_CHAINS
    heads_per_chain = chain_w // tq
    chains = [slice(c * chain_w, (c + 1) * chain_w) for c in range(NSA_CHAINS)]
    tile_chain = lambda v: jnp.concatenate([v] * heads_per_chain, axis=1)

    pw = min(tq, WIN_PART)
    parts = []
    for u in range(tq // pw):
        span = WIN + pw
        ws = pl.multiple_of(jnp.maximum(t0 + u * pw - WIN, 0), pw)
        dist = t_row[:, u * pw:(u + 1) * pw] - (ws + lax.broadcasted_iota(jnp.int32, (span, 1), 0))
        bias_w = jnp.concatenate([jnp.where((dist >= 0) & (dist < WIN), 0.0, NEG)] * NSA_HPG, axis=1)
        q_part = jnp.concatenate([q_rot[:, h * tq + u * pw:h * tq + (u + 1) * pw] for h in range(NSA_HPG)],
                                 axis=1)
        s_w = _dot(kw_ref[pl.ds(ws, span), :], q_part) + bias_w
        p_w = jnp.exp((s_w - jnp.max(s_w, axis=0, keepdims=True)).astype(BF16))
        parts.append(split_sum(_dot(vwt_ref[:, pl.ds(ws, span)], p_w)))
    o_w = jnp.concatenate([parts[u][:, h * pw:(h + 1) * pw]
                           for h in range(NSA_HPG) for u in range(tq // pw)], axis=1)

    c_idx = lax.broadcasted_iota(jnp.int32, (n_cmp_rows, 1), 0)
    valid = tile_chain(jnp.where(c_idx * CMP_STRIDE + (CMP_LEN - 1) <= t_row, 1.0, 0.0))
    bias_c = (valid - 1.0) * (-NEG)
    o_c = []
    p_sum = None
    for c in chains:
        s_c = _dot(kcmp_ref[...], q_raw[:, c]) + bias_c
        e_c = jnp.exp(s_c - jnp.max(s_c, axis=0, keepdims=True)) * valid
        l_c = jnp.sum(e_c, axis=0, keepdims=True)
        p_c = e_c / jnp.where(l_c > 0.0, l_c, 1.0)
        o_c.append(_dot(vct_ref[...], p_c.astype(BF16)))
        for h in range(heads_per_chain):
            p_h = p_c[:, h * tq:(h + 1) * tq]
            p_sum = p_h if p_sum is None else p_sum + p_h
    o_c = jnp.concatenate(o_c, axis=1)

    p_hi = p_sum.astype(BF16)
    p_lo = (p_sum - p_hi.astype(F32)).astype(BF16)
    ovt = ovt_ref[...]
    imp = _dot(ovt, p_hi) + _dot(ovt, p_lo)
    jb = lax.broadcasted_iota(jnp.int32, (n_sel, tq), 0)
    cur = (t0 + lax.broadcasted_iota(jnp.int32, (n_sel, tq), 1)) >> SEL_SHIFT
    forced = (jb == 0) | (jb == cur) | (jb == cur - 1)
    work = jnp.where(forced, FORCE, imp)
    work = jnp.where(jb <= cur, work, NEG)
    sel_t = jnp.zeros((n_sel, tq), F32)
    for _ in range(min(SEL_TOPK, n_sel)):
        best = jnp.max(work, axis=0, keepdims=True)
        first = jnp.min(jnp.where(work == best, jb, n_sel), axis=0, keepdims=True)
        hit = jb == first
        sel_t = jnp.where(hit, 1.0, sel_t)
        work = jnp.where(hit, -jnp.inf, work)
    bias_ref[...] = jnp.where(sel_t > 0.5, 0.0, NEG)

    def sel_tile(kt, carry, causal):
        k0 = pl.multiple_of(kt * tk, tk)
        bias = jnp.concatenate(
            [jnp.broadcast_to(bias_ref[pl.ds(kt * blocks_per_tile + j, 1), :], (SEL_LEN, tq))
             for j in range(blocks_per_tile)], axis=0)
        if causal:
            kpos = k0 + lax.broadcasted_iota(jnp.int32, (tk, 1), 0)
            bias = jnp.where(kpos <= t_row, bias, NEG)
        bias = tile_chain(bias)
        k_t = ks_ref[pl.ds(k0, tk), :]
        v_t = vst_ref[:, pl.ds(k0, tk)]
        out = []
        scores = [_dot(k_t, q_rot[:, c]) + bias for c in chains]
        for (m, acc), s in zip(carry, scores):
            m_new = jnp.maximum(m, jnp.max(s, axis=0, keepdims=True))
            p = jnp.exp((s - m_new).astype(BF16))
            acc = jnp.exp(m - m_new) * acc + _dot(v_t, p)
            out.append((m_new, acc))
        return tuple(out)

    n_full = t0 // tk
    init = tuple((jnp.full((1, chain_w), NEG, F32), jnp.zeros((NSA_DIM + SUM_ROWS, chain_w), F32))
                 for _ in chains)
    carry = lax.fori_loop(0, n_full, functools.partial(sel_tile, causal=False), init)
    carry = sel_tile(n_full, carry, causal=True)
    o_s = jnp.concatenate([split_sum(acc) for _, acc in carry], axis=1)

    gt = gate_ref[...]
    outs = []
    for h in range(NSA_HPG):
        c = slice(h * tq, (h + 1) * tq)
        outs.append(gt[3 * h:3 * h + 1] * o_c[:, c] + gt[3 * h + 1:3 * h + 2] * o_s[:, c]
                    + gt[3 * h + 2:3 * h + 3] * o_w[:, c])
    o_ref[...] = jnp.concatenate(outs, axis=0).T.astype(BF16)


def _nsa(nq, nqr, gates, kcmp, vcmp, ks, vs, kw, vw, batch, seq):
    n = batch * seq
    tq = 512
    tk = 512 if seq % 512 == 0 else seq
    nqb = seq // tq
    n_sel = seq // SEL_LEN
    rows_c = seq // CMP_STRIDE
    gw = NSA_HPG * NSA_DIM
    cs = np.arange(rows_c)[None, :] * CMP_STRIDE
    ss = np.arange(n_sel)[:, None] * SEL_LEN
    n_cmp = (seq - CMP_LEN) // CMP_STRIDE + 1
    ovt = ((cs < ss + SEL_LEN) & (cs + CMP_LEN > ss) & (np.arange(rows_c)[None, :] < n_cmp))
    ovt = jnp.asarray(ovt.astype(np.float32), BF16)

    qspec = pl.BlockSpec((tq, gw), lambda b, g, i: (b * nqb + i, g))
    cspec = pl.BlockSpec((None, None, rows_c, NSA_DIM), lambda b, g, i: (b, g, 0, 0))
    kspec = pl.BlockSpec((None, seq, NSA_DIM), lambda b, g, i: (g, b, 0))
    return pl.pallas_call(
        functools.partial(_nsa_kernel, tq=tq, tk=tk, seq=seq),
        grid=(batch, NSA_KV_GROUPS, nqb),
        in_specs=[qspec, qspec,
                  pl.BlockSpec((None, GATE_LANES, tq), lambda b, g, i: (g, 0, b * nqb + i)),
                  cspec, cspec, pl.BlockSpec(ovt.shape, lambda b, g, i: (0, 0)),
                  kspec, kspec, kspec, kspec],
        out_specs=qspec,
        out_shape=jax.ShapeDtypeStruct((n, NSA_WIDTH), BF16),
        scratch_shapes=[pltpu.VMEM((NSA_DIM + SUM_ROWS, seq), BF16), pltpu.VMEM((NSA_DIM + SUM_ROWS, seq), BF16),
                        pltpu.VMEM((NSA_DIM, rows_c), BF16), pltpu.VMEM((n_sel, tq), F32)],
        compiler_params=_params(3),
        name="nsa",
    )(nq, nqr, gates, kcmp, vcmp, ovt, ks, vs, kw, vw)


def _memkv_kernel(mem_ref, w_ref, kv_ref):
    kv_ref[...] = _dot(mem_ref[...].astype(BF16), w_ref[...]).astype(BF16)


def _memkv(mem2d, w_xkv):
    n = mem2d.shape[0]
    w = w_xkv.astype(BF16)
    return pl.pallas_call(
        _memkv_kernel,
        grid=(n // MEM_LEN,),
        in_specs=[pl.BlockSpec((MEM_LEN, D_MODEL), lambda i: (i, 0)),
                  pl.BlockSpec(w.shape, lambda i: (0, 0))],
        out_specs=pl.BlockSpec((MEM_LEN, 2 * D_MODEL), lambda i: (i, 0)),
        out_shape=jax.ShapeDtypeStruct((n, 2 * D_MODEL), BF16),
        compiler_params=_params(1),
        name="memkv",
    )(mem2d, w)


def _pack_halves(v):
    half = D_MODEL // 2
    hi = pltpu.bitcast(v[:, :half].astype(BF16).astype(F32), jnp.uint32)
    lo = pltpu.bitcast(v[:, half:].astype(BF16).astype(F32), jnp.uint32)
    return hi | (lo >> 16)


def _unpack_halves(words):
    return pltpu.bitcast(words & jnp.uint32(0xFFFF0000), F32), pltpu.bitcast(words << 16, F32)


def _postmix_kernel(x_ref, oret_ref, onsa_ref, kv_ref, wout_ref, wq_ref, wo_ref,
                    g1_ref, b1_ref, g2_ref, b2_ref, x2_ref, x2p_ref):
    mixed = jnp.concatenate([oret_ref[...], onsa_ref[...]], axis=1)
    x1 = _layer_norm(DN_ALPHA * x_ref[...] + _dot(mixed, wout_ref[...]), g1_ref[...], b1_ref[...])
    q = (_dot(x1.astype(BF16), wq_ref[...]) * (XATT_DIM ** -0.5)).astype(BF16)
    heads = []
    for h in range(XATT_HEADS):
        cols = slice(h * XATT_DIM, (h + 1) * XATT_DIM)
        s = _dot_nt(q[:, cols], kv_ref[:, cols])
        m = jnp.max(s, axis=-1, keepdims=True)
        p = jnp.exp(s - m)
        l = jnp.sum(p, axis=-1, keepdims=True)
        heads.append(_dot(p.astype(BF16), kv_ref[:, D_MODEL + h * XATT_DIM:D_MODEL + (h + 1) * XATT_DIM]) / l)
    att = jnp.concatenate(heads, axis=1).astype(BF16)
    x2 = _layer_norm(DN_ALPHA * x1 + _dot(att, wo_ref[...]), g2_ref[...], b2_ref[...])
    x2_ref[...] = x2
    x2p_ref[...] = _pack_halves(x2)


def _postmix(x2d, o_ret, o_nsa, kvx, w_out, w_xq, w_xo, ln1_g, ln1_b, ln2_g, ln2_b, batch0, batch, seq):
    n = batch * seq
    tm = 256 if seq % 256 == 0 else seq
    per_b = seq // tm
    row = lambda w: pl.BlockSpec((tm, w), lambda b, i: (b * per_b + i, 0))
    full = lambda a: pl.BlockSpec(a.shape, lambda b, i: (0,) * a.ndim)
    ws = [w_out.astype(BF16), w_xq.astype(BF16), w_xo.astype(BF16)]
    vecs = [v.reshape(1, D_MODEL) for v in (ln1_g, ln1_b, ln2_g, ln2_b)]
    return pl.pallas_call(
        _postmix_kernel,
        grid=(batch, per_b),
        in_specs=[pl.BlockSpec((tm, D_MODEL), lambda b, i: ((batch0 + b) * per_b + i, 0)),
                  row(RET_WIDTH), row(NSA_WIDTH),
                  pl.BlockSpec((MEM_LEN, 2 * D_MODEL), lambda b, i: (batch0 + b, 0))]
                 + [full(w) for w in ws] + [full(v) for v in vecs],
        out_specs=[row(D_MODEL),
                   row(D_MODEL // 2)],
        out_shape=[jax.ShapeDtypeStruct((n, D_MODEL), F32),
                   jax.ShapeDtypeStruct((n, D_MODEL // 2), jnp.uint32)],
        compiler_params=_params(2),
        name="postmix",
    )(x2d, o_ret, o_nsa, kvx, *ws, *vecs)


def _router_kernel(x_ref, wr_ref, bias_ref, e_ref, rank_ref, w_ref, cnt_ref, cntrow_ref, carry_ref, carryrow_ref):
    tn = x_ref.shape[0]
    per = N_EXPERTS // N_GROUPS

    @pl.when(pl.program_id(0) == 0)
    def _():
        carry_ref[...] = jnp.zeros_like(carry_ref)
        carryrow_ref[...] = jnp.zeros_like(carryrow_ref)

    logits = _dot_nt(wr_ref[...], x_ref[...].astype(BF16))
    scores = jax.nn.sigmoid(logits)
    biased = scores + bias_ref[...]
    b3 = biased.reshape(N_GROUPS, per, tn)
    member = lax.broadcasted_iota(jnp.int32, (N_GROUPS, per, tn), 1)
    top1 = jnp.max(b3, axis=1, keepdims=True)
    first1 = jnp.min(jnp.where(b3 == top1, member, per), axis=1, keepdims=True)
    top2 = jnp.max(jnp.where(member == first1, -jnp.inf, b3), axis=1, keepdims=True)
    gscore = top1 + top2
    gidx = lax.broadcasted_iota(jnp.int32, (N_GROUPS, 1, tn), 0)
    gwork = gscore
    for _ in range(TOPK_GROUPS - 1):
        gbest = jnp.max(gwork, axis=0, keepdims=True)
        gfirst = jnp.min(jnp.where(gwork == gbest, gidx, N_GROUPS), axis=0, keepdims=True)
        gwork = jnp.where(gidx == gfirst, -jnp.inf, gwork)
    kth = jnp.max(gwork, axis=0, keepdims=True)
    work = jnp.where(gscore >= kth, b3, NEG).reshape(N_EXPERTS, tn)
    eidx = lax.broadcasted_iota(jnp.int32, (N_EXPERTS, tn), 0)
    picks = []
    chosen = jnp.zeros((N_EXPERTS, tn), F32)
    for _ in range(TOP_K):
        best = jnp.max(work, axis=0, keepdims=True)
        first = jnp.min(jnp.where(work == best, eidx, N_EXPERTS), axis=0, keepdims=True)
        hit = eidx == first
        picks.append((first, hit))
        chosen = jnp.where(hit, 1.0, chosen)
        work = jnp.where(hit, -jnp.inf, work)

    r_i = lax.broadcasted_iota(jnp.int32, (tn, tn), 0)
    c_i = lax.broadcasted_iota(jnp.int32, (tn, tn), 1)
    before = jnp.where(r_i < c_i, 1.0, 0.0).astype(BF16)
    chosen_b = chosen.astype(BF16)
    rank = _dot(chosen_b, before) + carry_ref[...]
    carry_ref[...] = carry_ref[...] + jnp.sum(chosen, axis=1, keepdims=True)
    carryrow_ref[...] = carryrow_ref[...] + _dot_nt(jnp.ones((8, tn), BF16), chosen_b)
    cnt_ref[...] = carry_ref[...]
    cntrow_ref[...] = carryrow_ref[...]

    wsel = [jnp.sum(jnp.where(hit, scores, 0.0), axis=0, keepdims=True) for _, hit in picks]
    wsum = wsel[0]
    for v in wsel[1:]:
        wsum = wsum + v
    for kk, (first, hit) in enumerate(picks):
        e_ref[kk:kk + 1, :] = first
        rank_ref[kk:kk + 1, :] = jnp.sum(jnp.where(hit, rank, 0.0), axis=0, keepdims=True).astype(jnp.int32)
        w_ref[kk:kk + 1, :] = wsel[kk] / wsum * ROUTED_SCALE


def _router(x2, w_router, router_bias):
    n = x2.shape[0]
    tn = 512 if n % 512 == 0 else n
    wr_t = w_router.T.astype(BF16)
    bias = router_bias.reshape(N_EXPERTS, 1).astype(F32)
    kspec = pl.BlockSpec((TOP_K, tn), lambda i: (0, i))
    return pl.pallas_call(
        _router_kernel,
        grid=(n // tn,),
        in_specs=[pl.BlockSpec((tn, D_MODEL), lambda i: (i, 0)),
                  pl.BlockSpec(wr_t.shape, lambda i: (0, 0)),
                  pl.BlockSpec(bias.shape, lambda i: (0, 0))],
        out_specs=[kspec, kspec, kspec, pl.BlockSpec((N_EXPERTS, 1), lambda i: (0, 0)),
                   pl.BlockSpec((8, N_EXPERTS), lambda i: (0, 0))],
        out_shape=[jax.ShapeDtypeStruct((TOP_K, n), jnp.int32),
                   jax.ShapeDtypeStruct((TOP_K, n), jnp.int32),
                   jax.ShapeDtypeStruct((TOP_K, n), F32),
                   jax.ShapeDtypeStruct((N_EXPERTS, 1), F32),
                   jax.ShapeDtypeStruct((8, N_EXPERTS), F32)],
        scratch_shapes=[pltpu.VMEM((N_EXPERTS, 1), F32), pltpu.VMEM((8, N_EXPERTS), F32)],
        compiler_params=_params(1),
        name="router",
    )(x2, wr_t, bias)


def _slots_kernel(e_ref, rank_ref, cnt_ref, cntrow_ref, dest_ref, blk_e_ref, valid_ref, *, blk, n_blocks):
    pad = lambda c: jnp.ceil(c / blk) * blk
    cnt = cnt_ref[...]
    padded = pad(cnt)
    padded_row = pad(cntrow_ref[0:1, :])
    r_i = lax.broadcasted_iota(jnp.int32, (N_EXPERTS, N_EXPERTS), 0)
    c_i = lax.broadcasted_iota(jnp.int32, (N_EXPERTS, N_EXPERTS), 1)
    start = jnp.sum(jnp.where(c_i < r_i, padded_row, 0.0), axis=1, keepdims=True)
    end = start + padded
    e = e_ref[...]
    dest = rank_ref[...]
    for ex in range(N_EXPERTS):
        dest = dest + jnp.where(e == ex, start[ex:ex + 1, :].astype(jnp.int32), 0)
    dest_ref[...] = dest
    bstart = (lax.broadcasted_iota(jnp.int32, (1, n_blocks), 1) * blk).astype(F32)
    owner = jnp.sum(jnp.where(end <= bstart, 1.0, 0.0), axis=0, keepdims=True)
    blk_e_ref[...] = jnp.minimum(owner, N_EXPERTS - 1.0).astype(jnp.int32)
    inside = (start <= bstart) & (bstart < end)
    real = jnp.clip(start + cnt - bstart, 0.0, float(blk))
    valid_ref[...] = jnp.sum(jnp.where(inside, real, 0.0), axis=0, keepdims=True).astype(jnp.int32)


def _slots(e_k, rank_k, counts, counts_row, blk, n_blocks):
    n = e_k.shape[1]
    full = lambda shape: pl.BlockSpec(shape, lambda: (0,) * len(shape))
    return pl.pallas_call(
        functools.partial(_slots_kernel, blk=blk, n_blocks=n_blocks),
        in_specs=[full((TOP_K, n)), full((TOP_K, n)), full((N_EXPERTS, 1)), full((8, N_EXPERTS))],
        out_specs=[full((TOP_K, n)), full((1, n_blocks)), full((1, n_blocks))],
        out_shape=[jax.ShapeDtypeStruct((TOP_K, n), jnp.int32),
                   jax.ShapeDtypeStruct((1, n_blocks), jnp.int32),
                   jax.ShapeDtypeStruct((1, n_blocks), jnp.int32)],
        compiler_params=pltpu.CompilerParams(vmem_limit_bytes=VMEM_LIMIT),
        name="slots",
    )(e_k, rank_k, counts, counts_row)


def _sc_worker_base(per_worker):
    return (lax.axis_index("s") * SC_CORES + lax.axis_index("c")) * per_worker


def _sc_scatter_rows(rows, idx, n_out):
    n, width = rows.shape
    k_lists = idx.shape[0] // n
    workers = SC_CORES * SC_SUBCORES
    per_worker = n // workers
    assert per_worker * workers == n and per_worker % SC_CHUNK == 0
    mesh = plsc.VectorSubcoreMesh(core_axis_name="c", subcore_axis_name="s")

    @functools.partial(
        pl.kernel, mesh=mesh,
        out_type=jax.ShapeDtypeStruct((n_out, width), rows.dtype),
        scratch_types=[pltpu.VMEM((SC_CHUNK, width), rows.dtype)]
                      + [pltpu.VMEM((SC_CHUNK,), jnp.int32)] * k_lists + [pltpu.SemaphoreType.DMA] * 3,
        name="sc_scatter")
    def scatter(rows_hbm, idx_hbm, out_hbm, rows_v, *rest):
        idx_vs = rest[:k_lists]
        sem_rows, sem_idx, sem_out = rest[k_lists:]
        base = _sc_worker_base(per_worker)

        @pl.loop(0, per_worker // SC_CHUNK)
        def _(ci):
            off = pl.multiple_of(base + ci * SC_CHUNK, SC_CHUNK)
            loads = [pltpu.async_copy(rows_hbm.at[pl.ds(off, SC_CHUNK)], rows_v, sem_rows)]
            loads += [pltpu.async_copy(idx_hbm.at[pl.ds(pl.multiple_of(k * n + off, SC_CHUNK), SC_CHUNK)],
                                       idx_vs[k], sem_idx) for k in range(k_lists)]
            for c in loads:
                c.wait()
            copies = [pltpu.async_copy(rows_v, out_hbm.at[idx_vs[k]], sem_out) for k in range(k_lists)]
            for c in copies:
                c.wait()

    return scatter(rows, idx)


def _experts_kernel(blk_e_ref, valid_ref, xs_ref, wg_ref, wu_ref, wd_ref, y_ref, wg_b, wu_b, wd_b):
    i = pl.program_id(0)
    valid = valid_ref[i]

    @pl.when((i == 0) | (blk_e_ref[i] != blk_e_ref[jnp.maximum(i - 1, 0)]))
    def _():
        wg_b[...] = wg_ref[...].astype(BF16)
        wu_b[...] = wu_ref[...].astype(BF16)
        wd_b[...] = wd_ref[...].astype(BF16)

    @pl.when(valid > 0)
    def _():
        half = D_MODEL // 2
        row = lax.broadcasted_iota(jnp.int32, (xs_ref.shape[0], 1), 0)
        hi, lo = (v.astype(BF16) for v in _unpack_halves(jnp.where(row < valid, xs_ref[...], jnp.uint32(0))))
        gate = _dot(hi, wg_b[:half, :]) + _dot(lo, wg_b[half:, :])
        up = _dot(hi, wu_b[:half, :]) + _dot(lo, wu_b[half:, :])
        y_ref[...] = _pack_halves(_dot((jax.nn.silu(gate) * up).astype(BF16), wd_b[...]))

    @pl.when(valid <= 0)
    def _():
        y_ref[...] = jnp.zeros_like(y_ref)


def _experts(blk_e, valid, xs, w_gate, w_up, w_down, blk):
    cap, width = xs.shape
    wspec = lambda a: pl.BlockSpec((None,) + a.shape[1:], lambda i, be, nv: (be[i], 0, 0))
    rows = pl.BlockSpec((blk, width), lambda i, be, nv: (i, 0))
    return pl.pallas_call(
        _experts_kernel,
        grid_spec=pltpu.PrefetchScalarGridSpec(
            num_scalar_prefetch=2,
            grid=(cap // blk,),
            in_specs=[rows, wspec(w_gate), wspec(w_up), wspec(w_down)],
            out_specs=rows,
            scratch_shapes=[pltpu.VMEM(w.shape[1:], BF16) for w in (w_gate, w_up, w_down)],
        ),
        out_shape=jax.ShapeDtypeStruct(xs.shape, xs.dtype),
        compiler_params=_params(1),
        name="experts",
    )(blk_e, valid, xs, w_gate, w_up, w_down)


def _sc_gather_rows(table, idx):
    b, width = idx.shape[0], table.shape[1]
    workers = SC_CORES * SC_SUBCORES
    per_worker = b // workers
    assert per_worker * workers == b and per_worker % (SC_CHUNK * SC_INFLIGHT) == 0
    mesh = plsc.VectorSubcoreMesh(core_axis_name="c", subcore_axis_name="s")

    @functools.partial(
        pl.kernel, mesh=mesh,
        out_type=jax.ShapeDtypeStruct((b, width), table.dtype),
        scratch_types=[pltpu.VMEM((SC_CHUNK,), jnp.int32)] * SC_INFLIGHT
                      + [pltpu.VMEM((SC_CHUNK, width), table.dtype)] * SC_INFLIGHT
                      + [pltpu.SemaphoreType.DMA] * (1 + 2 * SC_INFLIGHT),
        name="sc_gather")
    def gather(table_hbm, idx_hbm, out_hbm, *scratch):
        idx_vs = scratch[:SC_INFLIGHT]
        rows_vs = scratch[SC_INFLIGHT:2 * SC_INFLIGHT]
        sem_idx = scratch[2 * SC_INFLIGHT]
        sem_rows = scratch[2 * SC_INFLIGHT + 1:3 * SC_INFLIGHT + 1]
        sem_out = scratch[3 * SC_INFLIGHT + 1:]
        base = _sc_worker_base(per_worker)
        lanes = range(SC_INFLIGHT)

        @pl.loop(0, per_worker // (SC_CHUNK * SC_INFLIGHT))
        def _(gi):
            offs = [pl.multiple_of(base + (gi * SC_INFLIGHT + j) * SC_CHUNK, SC_CHUNK) for j in lanes]
            loads = [pltpu.async_copy(idx_hbm.at[pl.ds(offs[j], SC_CHUNK)], idx_vs[j], sem_idx) for j in lanes]
            for c in loads:
                c.wait()
            gathers = [pltpu.async_copy(table_hbm.at[idx_vs[j]], rows_vs[j], sem_rows[j]) for j in lanes]
            writes = []
            for j in lanes:
                gathers[j].wait()
                writes.append(pltpu.async_copy(rows_vs[j], out_hbm.at[pl.ds(offs[j], SC_CHUNK)], sem_out[j]))
            for c in writes:
                c.wait()

    return gather(table, idx)


def _combine_kernel(x_ref, wk_ref, yk_ref, wsg_ref, wsu_ref, wsd_ref, g_ref, b_ref, *rest):
    o_ref = rest[-1]
    x = x_ref[...]
    xb = x.astype(BF16)
    shared = _dot((jax.nn.silu(_dot(xb, wsg_ref[...])) * _dot(xb, wsu_ref[...])).astype(BF16), wsd_ref[...])
    wk = wk_ref[...]
    routed_hi = routed_lo = None
    for kk in range(TOP_K):
        hi, lo = _unpack_halves(yk_ref[kk])
        w = wk[:, kk:kk + 1]
        routed_hi = hi * w if kk == 0 else routed_hi + hi * w
        routed_lo = lo * w if kk == 0 else routed_lo + lo * w
    routed = jnp.concatenate([routed_hi, routed_lo], axis=1)
    o_ref[...] = _layer_norm(DN_ALPHA * x + (routed + shared), g_ref[...], b_ref[...])


def _combine(x2, w_tok, yk, ws_gate, ws_up, ws_down, ln3_g, ln3_b, row0, n_total, out_prev):
    n = x2.shape[0]
    tt = 256 if n % 256 == 0 and row0 % 256 == 0 else n
    blk0 = row0 // tt
    ws = [ws_gate.astype(BF16), ws_up.astype(BF16), ws_down.astype(BF16)]
    vecs = [ln3_g.reshape(1, D_MODEL), ln3_b.reshape(1, D_MODEL)]
    full = lambda a: pl.BlockSpec(a.shape, lambda i: (0,) * a.ndim)
    args = [x2, w_tok, yk, *ws, *vecs]
    in_specs = ([pl.BlockSpec((tt, D_MODEL), lambda i: (i, 0)),
                 pl.BlockSpec((tt, TOP_K), lambda i: (i, 0)),
                 pl.BlockSpec((TOP_K, tt, D_MODEL // 2), lambda i: (0, i, 0))]
                + [full(a) for a in ws] + [full(v) for v in vecs])
    aliases = {}
    if out_prev is not None:
        aliases = {len(args): 0}
        args.append(out_prev)
        in_specs.append(pl.BlockSpec(memory_space=pl.ANY))
    return pl.pallas_call(
        _combine_kernel,
        grid=(n // tt,),
        in_specs=in_specs,
        out_specs=pl.BlockSpec((tt, D_MODEL), lambda i: (blk0 + i, 0)),
        out_shape=jax.ShapeDtypeStruct((n_total, D_MODEL), F32),
        input_output_aliases=aliases,
        compiler_params=_params(1),
        name="combine",
    )(*args)


def _moe_and_norm(x2, x2p, w_router, router_bias, w_gate, w_up, w_down, ws_gate, ws_up, ws_down,
                  ln3_g, ln3_b, row0, n_total, out_prev):
    n = x2.shape[0]
    blk = 512
    cap = n * TOP_K + N_EXPERTS * blk
    n_blocks = cap // blk
    e_k, rank_k, w_k, counts, counts_row = _router(x2, w_router, router_bias)
    dest, blk_e, valid = _slots(e_k, rank_k, counts, counts_row, blk, n_blocks)
    dest = dest.reshape(-1)
    xs = _sc_scatter_rows(x2p, dest, cap)
    y = _experts(blk_e.reshape(-1), valid.reshape(-1), xs, w_gate, w_up, w_down, blk)
    yk = _sc_gather_rows(y, dest).reshape(TOP_K, n, D_MODEL // 2)
    return _combine(x2, w_k.T, yk, ws_gate, ws_up, ws_down, ln3_g, ln3_b, row0, n_total, out_prev)


def _layer(x, mem, positions, w_in, cmp_pe_k, cmp_pe_v, cmp_w1_k, cmp_w2_k, cmp_w1_v, cmp_w2_v,
           w_out, ln1_g, ln1_b, w_xq, w_xkv, w_xo, ln2_g, ln2_b, w_router, router_bias,
           w_gate, w_up, w_down, ws_gate, ws_up, ws_down, ln3_g, ln3_b):
    batch, seq, _ = x.shape
    n_total = batch * seq
    x2d = x.reshape(n_total, D_MODEL)
    pos_col = positions.astype(F32).reshape(n_total, 1)
    kvx = _memkv(mem.reshape(batch * MEM_LEN, D_MODEL), w_xkv)
    groups = BATCH_GROUPS if batch % BATCH_GROUPS == 0 else 1
    per = batch // groups
    n = per * seq
    out = None
    for gi in range(groups):
        (rq, rk, rv, rg, nq, nqr, kc, vc, ks, vs, kw, vw, gates) = _inproj(x2d, pos_col, w_in, gi * n, n)
        o_ret = _retention(rq, rk, rv, rg, per, seq)
        kcmp = _compress(kc, cmp_pe_k, cmp_w1_k, cmp_w2_k, per, seq)
        vcmp = _compress(vc, cmp_pe_v, cmp_w1_v, cmp_w2_v, per, seq)
        o_nsa = _nsa(nq, nqr, gates, kcmp, vcmp, ks, vs, kw, vw, per, seq)
        x2, x2p = _postmix(x2d, o_ret, o_nsa, kvx, w_out, w_xq, w_xo, ln1_g, ln1_b, ln2_g, ln2_b,
                           gi * per, per, seq)
        out = _moe_and_norm(x2, x2p, w_router, router_bias, w_gate, w_up, w_down,
                            ws_gate, ws_up, ws_down, ln3_g, ln3_b, gi * n, n_total, out)
    return out.reshape(batch, seq, D_MODEL)


def kernel(x, mem, positions, w_in, cmp_pe_k, cmp_pe_v, cmp_w1_k, cmp_w2_k, cmp_w1_v, cmp_w2_v, w_out, ln1_g, ln1_b, w_xq, w_xkv, w_xo, ln2_g, ln2_b, w_router, router_bias, w_gate, w_up, w_down, ws_gate, ws_up, ws_down, ln3_g, ln3_b):
    for l in range(DEPTH):
        x = _layer(x, mem, positions, w_in[l], cmp_pe_k[l], cmp_pe_v[l], cmp_w1_k[l], cmp_w2_k[l],
                   cmp_w1_v[l], cmp_w2_v[l], w_out[l], ln1_g[l], ln1_b[l], w_xq[l], w_xkv[l],
                   w_xo[l], ln2_g[l], ln2_b[l], w_router[l], router_bias[l], w_gate[l], w_up[l],
                   w_down[l], ws_gate[l], ws_up[l], ws_down[l], ln3_g[l], ln3_b[l])
    return x
```

```python
import functools

import numpy as np
import jax
import jax.numpy as jnp
from jax import lax
from jax.experimental import pallas as pl
from jax.experimental.pallas import tpu as pltpu
from jax.experimental.pallas import tpu_sc as plsc

D_MODEL = 1024
MEM_LEN = 256
DEPTH = 1
DN_ALPHA = (2 * DEPTH) ** 0.25
LN_EPS = 1e-5
NEG = -1e30
FORCE = 1e9

RET_HEADS = 4
RET_DIM = 128
RET_CHUNK = 128
RET_ROPE_BASE = 10000.0
RET_STEP_CHUNKS = 4
RET_WIDTH = RET_HEADS * RET_DIM

NSA_HEADS = 8
NSA_KV_GROUPS = 2
NSA_HPG = NSA_HEADS // NSA_KV_GROUPS
NSA_DIM = 64
NSA_WIDTH = NSA_HEADS * NSA_DIM
KV_WIDTH = NSA_KV_GROUPS * NSA_DIM
CMP_LEN = 32
CMP_STRIDE = 16
CMP_HIDDEN = 256
SEL_LEN = 64
SEL_SHIFT = 6
SEL_TOPK = 16
WIN = 512
ROPE_THETA = 500000.0
ROPE_DIMS = NSA_DIM // 4
GATE_LANES = 16
NSA_CHAINS = 1
SUM_ROWS = 16
WIN_PART = 256

BATCH_GROUPS = 2

SC_CORES = 2
SC_SUBCORES = 16
SC_CHUNK = 64
SC_INFLIGHT = 2

XATT_HEADS = 4
XATT_DIM = D_MODEL // XATT_HEADS

N_EXPERTS = 64
TOP_K = 8
N_GROUPS = 8
TOPK_GROUPS = 4
EXPERT_FF = 256
SHARED_FF = 256
ROUTED_SCALE = 2.5

LANES = 128
VMEM_LIMIT = 56 * 1024 * 1024

F32 = jnp.float32
BF16 = jnp.bfloat16
NT_DIMS = (((1,), (1,)), ((), ()))


def _params(n_axes):
    return pltpu.CompilerParams(dimension_semantics=("arbitrary",) * n_axes,
                                vmem_limit_bytes=VMEM_LIMIT)


def _dot(a, b):
    return jnp.dot(a, b, preferred_element_type=F32)


def _dot_nt(a, b):
    return lax.dot_general(a, b, NT_DIMS, preferred_element_type=F32)


def _layer_norm(v, g, b):
    mu = jnp.mean(v, axis=-1, keepdims=True)
    d = v - mu
    var = jnp.mean(d * d, axis=-1, keepdims=True)
    return d * lax.rsqrt(var + LN_EPS) * g + b


def _inproj_kernel(x_ref, pos_ref, wret_ref, wnq_ref, wkv_ref, wg_ref, invr_ref, invn_ref,
                   rq_ref, rk_ref, rv_ref, rg_ref, nq_ref, nqr_ref, kc_ref, vc_ref,
                   ks_ref, vs_ref, kw_ref, vw_ref, gate_ref):
    xb = x_ref[...].astype(BF16)
    pos = pos_ref[...]
    lane = lax.broadcasted_iota(jnp.int32, (1, LANES), 1)

    ang = pos * invr_ref[...]
    cos_r = jnp.cos(ang)
    sin_r = jnp.sin(ang)
    sin_r = jnp.where(lane < RET_DIM // 2, -sin_r, sin_r)
    q_all = _dot(xb, wret_ref[:, :RET_WIDTH])
    k_all = _dot(xb, wret_ref[:, RET_WIDTH:2 * RET_WIDTH])
    for h in range(RET_HEADS):
        cols = slice(h * RET_DIM, (h + 1) * RET_DIM)
        q = q_all[:, cols]
        rq_ref[:, cols] = (q * cos_r + pltpu.roll(q, RET_DIM // 2, 1) * sin_r).astype(BF16)
        k = k_all[:, cols]
        k = (k * cos_r + pltpu.roll(k, RET_DIM // 2, 1) * sin_r) * (RET_DIM ** -0.5)
        rk_ref[:, cols] = k.astype(BF16)
    rv_ref[...] = _dot(xb, wret_ref[:, 2 * RET_WIDTH:3 * RET_WIDTH]).astype(BF16)
    rg_ref[...] = _dot(xb, wret_ref[:, 3 * RET_WIDTH:4 * RET_WIDTH]).astype(BF16)

    half = ROPE_DIMS // 2
    j = lane % NSA_DIM
    angn = pos * invn_ref[...]
    cos_n = jnp.cos(angn)
    sin_n = jnp.sin(angn)
    sin_lo = jnp.where(j < half, -sin_n, 0.0)
    sin_hi = jnp.where((j >= half) & (j < 2 * half), sin_n, 0.0)

    def rope_n(v):
        return v * cos_n + pltpu.roll(v, half, 1) * sin_hi + pltpu.roll(v, LANES - half, 1) * sin_lo

    scale = NSA_DIM ** -0.5
    nq_all = _dot(xb, wnq_ref[...])
    for c in range(NSA_WIDTH // LANES):
        cols = slice(c * LANES, (c + 1) * LANES)
        q = nq_all[:, cols]
        nq_ref[:, cols] = (q * scale).astype(BF16)
        nqr_ref[:, cols] = (rope_n(q) * scale).astype(BF16)

    kv_all = _dot(xb, wkv_ref[...])

    def kv(i):
        return kv_all[:, i * KV_WIDTH:(i + 1) * KV_WIDTH]

    def split_groups(ref, v):
        for g in range(NSA_KV_GROUPS):
            ref[g] = v[:, g * NSA_DIM:(g + 1) * NSA_DIM].astype(BF16)

    kc_ref[...] = kv(0)
    vc_ref[...] = kv(1)
    split_groups(ks_ref, rope_n(kv(2)))
    split_groups(vs_ref, kv(3))
    split_groups(kw_ref, rope_n(kv(4)))
    split_groups(vw_ref, kv(5))

    gt = jax.nn.sigmoid(_dot_nt(wg_ref[...], xb))
    for g in range(NSA_KV_GROUPS):
        gate_ref[g] = gt[g * GATE_LANES:(g + 1) * GATE_LANES, :]


def _inproj(x2d, pos_col, w_in, row0, n):
    tm = 512 if n % 512 == 0 and row0 % 512 == 0 else n
    blk0 = row0 // tm
    off = np.cumsum([0] + [RET_WIDTH] * 4 + [NSA_WIDTH] + [KV_WIDTH] * 6)
    w_ret = w_in[:, :off[4]].astype(BF16)
    w_nq = w_in[:, off[4]:off[5]].astype(BF16)
    w_kv = w_in[:, off[5]:off[11]].astype(BF16)
    wg = w_in[:, off[11]:].reshape(D_MODEL, NSA_KV_GROUPS, NSA_HPG * 3)
    wg = jnp.pad(wg, ((0, 0), (0, 0), (0, GATE_LANES - NSA_HPG * 3)))
    wg = wg.reshape(D_MODEL, NSA_KV_GROUPS * GATE_LANES).T.astype(BF16)

    lane = np.arange(LANES)
    half_r = RET_DIM // 2
    inv_r = (np.float32(RET_ROPE_BASE) ** (-np.arange(half_r, dtype=np.float32) / np.float32(half_r)))
    inv_r = inv_r.astype(np.float32)[lane % half_r][None, :]
    half_n = ROPE_DIMS // 2
    inv_n = (np.float32(ROPE_THETA) ** (-np.arange(half_n, dtype=np.float32) / np.float32(half_n)))
    jn = lane % NSA_DIM
    inv_n = np.where(jn < ROPE_DIMS, inv_n.astype(np.float32)[jn % half_n], np.float32(0.0))[None, :]

    row = lambda w: pl.BlockSpec((tm, w), lambda i: (i, 0))
    src_row = lambda w: pl.BlockSpec((tm, w), lambda i: (blk0 + i, 0))
    full = lambda a: pl.BlockSpec(a.shape, lambda i: (0,) * a.ndim)
    grp = lambda w: pl.BlockSpec((NSA_KV_GROUPS, tm, w), lambda i: (0, i, 0))
    bf = lambda w: jax.ShapeDtypeStruct((n, w), BF16)
    gbf = jax.ShapeDtypeStruct((NSA_KV_GROUPS, n, NSA_DIM), BF16)
    inv_r = jnp.asarray(inv_r, F32)
    inv_n = jnp.asarray(inv_n, F32)
    return pl.pallas_call(
        _inproj_kernel,
        grid=(n // tm,),
        in_specs=[src_row(D_MODEL), src_row(1), full(w_ret), full(w_nq), full(w_kv), full(wg),
                  full(inv_r), full(inv_n)],
        out_specs=[row(RET_WIDTH)] * 4 + [row(NSA_WIDTH)] * 2 + [row(KV_WIDTH)] * 2
                  + [grp(NSA_DIM)] * 4
                  + [pl.BlockSpec((NSA_KV_GROUPS, GATE_LANES, tm), lambda i: (0, 0, i))],
        out_shape=[bf(RET_WIDTH)] * 4 + [bf(NSA_WIDTH)] * 2
                  + [jax.ShapeDtypeStruct((n, KV_WIDTH), F32)] * 2 + [gbf] * 4
                  + [jax.ShapeDtypeStruct((NSA_KV_GROUPS, GATE_LANES, n), F32)],
        compiler_params=_params(1),
        name="inproj",
    )(x2d, pos_col, w_ret, w_nq, w_kv, wg, inv_r, inv_n)


def _retention_kernel(q_ref, k_ref, v_ref, g_ref, o_ref, state_ref):
    c = RET_CHUNK

    @pl.when(pl.program_id(1) == 0)
    def _():
        state_ref[...] = jnp.zeros_like(state_ref)

    row = lax.broadcasted_iota(jnp.int32, (c, c), 0)
    col = lax.broadcasted_iota(jnp.int32, (c, c), 1)
    rel = (row - col).astype(F32)
    idx = lax.broadcasted_iota(jnp.int32, (c, 1), 0).astype(F32)
    for h in range(RET_HEADS):
        log_g = float(np.log(np.float32(1.0) - np.float32(2.0) ** np.float32(-5.0 - h)))
        cols = slice(h * RET_DIM, (h + 1) * RET_DIM)
        dmask = jnp.where(rel >= 0, jnp.exp(log_g * jnp.maximum(rel, 0.0)), 0.0)
        zeta = jnp.exp(log_g * (c - 1.0 - idx))
        xi = jnp.exp(log_g * (idx + 1.0))
        for j in range(q_ref.shape[0] // c):
            rows = slice(j * c, (j + 1) * c)
            q = q_ref[rows, cols]
            k = k_ref[rows, cols]
            v = v_ref[rows, cols]
            scores = _dot_nt(q, k) * dmask
            inner = _dot(scores.astype(BF16), v)
            prev = state_ref[h]
            cross = _dot(q, prev.astype(BF16)) * xi
            kz = (k.astype(F32) * zeta).astype(BF16)
            kv = lax.dot_general(kz, v, (((0,), (0,)), ((), ())), preferred_element_type=F32)
            state_ref[h] = prev * float(np.exp(np.float32(log_g) * np.float32(c))) + kv
            o = inner + cross
            mu = jnp.mean(o, axis=-1, keepdims=True)
            d = o - mu
            var = jnp.mean(d * d, axis=-1, keepdims=True)
            o = d * lax.rsqrt(var + LN_EPS)
            o_ref[rows, cols] = (jax.nn.silu(g_ref[rows, cols].astype(F32)) * o).astype(BF16)


def _retention(rq, rk, rv, rg, batch, seq):
    per_step = RET_STEP_CHUNKS if (seq // RET_CHUNK) % RET_STEP_CHUNKS == 0 else 1
    nc = seq // (RET_CHUNK * per_step)
    spec = pl.BlockSpec((RET_CHUNK * per_step, RET_WIDTH), lambda b, n: (b * nc + n, 0))
    return pl.pallas_call(
        _retention_kernel,
        grid=(batch, nc),
        in_specs=[spec] * 4,
        out_specs=spec,
        out_shape=jax.ShapeDtypeStruct(rq.shape, BF16),
        scratch_shapes=[pltpu.VMEM((RET_HEADS, RET_DIM, RET_DIM), F32)],
        compiler_params=_params(2),
        name="retention",
    )(rq, rk, rv, rg)


def _compress_kernel(a_ref, pe_ref, w1_ref, w2_ref, o_ref, shift_ref, *, n_cmp):
    rows = a_ref.shape[0]
    a = a_ref[...]
    lo = (a + pe_ref[0]).astype(BF16)
    hi = (a + pe_ref[1]).astype(BF16)
    ridx = lax.broadcasted_iota(jnp.int32, (rows, 1), 0)
    shift_ref[rows:rows + 8, :] = jnp.zeros((8, CMP_HIDDEN), F32)
    for g in range(NSA_KV_GROUPS):
        p = _dot(lo, w1_ref[0, g])
        shift_ref[0:rows, :] = _dot(hi, w1_ref[1, g])
        hid = jax.nn.silu(p + shift_ref[pl.ds(1, rows), :])
        out = _dot(hid.astype(BF16), w2_ref[...])
        o_ref[g] = jnp.where(ridx < n_cmp, out, 0.0).astype(BF16)


def _compress(a, pe, w1, w2, batch, seq):
    rows = seq // CMP_STRIDE
    per = CMP_STRIDE * KV_WIDTH
    n_cmp = (seq - CMP_LEN) // CMP_STRIDE + 1
    a2 = a.reshape(batch * rows, per)
    pe2 = jnp.tile(pe.reshape(2, CMP_STRIDE, 1, NSA_DIM), (1, 1, NSA_KV_GROUPS, 1)).reshape(2, 1, per)
    w1r = w1.reshape(2, CMP_STRIDE, 1, NSA_DIM, CMP_HIDDEN)
    eye = jnp.eye(NSA_KV_GROUPS, dtype=w1.dtype).reshape(1, NSA_KV_GROUPS, 1, NSA_KV_GROUPS, 1, 1)
    w1x = (w1r[:, None] * eye).reshape(2, NSA_KV_GROUPS, per, CMP_HIDDEN).astype(BF16)
    w2b = w2.astype(BF16)
    full = lambda arr: pl.BlockSpec(arr.shape, lambda b: (0,) * arr.ndim)
    return pl.pallas_call(
        functools.partial(_compress_kernel, n_cmp=n_cmp),
        grid=(batch,),
        in_specs=[pl.BlockSpec((rows, per), lambda b: (b, 0)), full(pe2), full(w1x), full(w2b)],
        out_specs=pl.BlockSpec((None, NSA_KV_GROUPS, rows, NSA_DIM), lambda b: (b, 0, 0, 0)),
        out_shape=jax.ShapeDtypeStruct((batch, NSA_KV_GROUPS, rows, NSA_DIM), BF16),
        scratch_shapes=[pltpu.VMEM((rows + 8, CMP_HIDDEN), F32)],
        compiler_params=_params(1),
        name="compress",
    )(a2, pe2, w1x, w2b)


def _heads_to_lanes(ref):
    vt = ref[...].astype(F32).T
    return jnp.concatenate([vt[h * NSA_DIM:(h + 1) * NSA_DIM] for h in range(NSA_HPG)], axis=1).astype(BF16)


def _tile_heads(v):
    return jnp.concatenate([v] * NSA_HPG, axis=1)


def _transpose_into(dst_ref, src_ref, chunk):
    def step(c, _):
        c0 = pl.multiple_of(c * chunk, chunk)
        dst_ref[:NSA_DIM, pl.ds(c0, chunk)] = src_ref[pl.ds(c0, chunk), :].astype(F32).T.astype(BF16)
        return 0
    lax.fori_loop(0, src_ref.shape[0] // chunk, step, 0)


def _nsa_kernel(qraw_ref, qrot_ref, gate_ref, kcmp_ref, vcmp_ref, ovt_ref,
                ks_ref, vs_ref, kw_ref, vw_ref, o_ref, vst_ref, vwt_ref, vct_ref, bias_ref, *, tq, tk, seq):
    i = pl.program_id(2)
    t0 = i * tq
    cols = NSA_HPG * tq
    n_sel = seq // SEL_LEN
    n_cmp_rows = seq // CMP_STRIDE
    blocks_per_tile = tk // SEL_LEN

    @pl.when(i == 0)
    def _():
        chunk = min(512, n_cmp_rows)
        _transpose_into(vst_ref, vs_ref, chunk)
        _transpose_into(vwt_ref, vw_ref, chunk)
        _transpose_into(vct_ref, vcmp_ref, chunk)
        vst_ref[NSA_DIM:, :] = jnp.ones((SUM_ROWS, seq), BF16)
        vwt_ref[NSA_DIM:, :] = jnp.ones((SUM_ROWS, seq), BF16)

    def split_sum(acc):
        return acc[:NSA_DIM] / acc[NSA_DIM:NSA_DIM + 1]

    q_raw = _heads_to_lanes(qraw_ref)
    q_rot = _heads_to_lanes(qrot_ref)
    t_row = t0 + lax.broadcasted_iota(jnp.int32, (1, tq), 1)

    chain_w = cols // NSA_CHAINS
    heads_per_chain = chain_w // tq
    chains = [slice(c * chain_w, (c + 1) * chain_w) for c in range(NSA_CHAINS)]
    tile_chain = lambda v: jnp.concatenate([v] * heads_per_chain, axis=1)

    pw = min(tq, WIN_PART)
    parts = []
    for u in range(tq // pw):
        span = WIN + pw
        ws = pl.multiple_of(jnp.maximum(t0 + u * pw - WIN, 0), pw)
        dist = t_row[:, u * pw:(u + 1) * pw] - (ws + lax.broadcasted_iota(jnp.int32, (span, 1), 0))
        bias_w = jnp.concatenate([jnp.where((dist >= 0) & (dist < WIN), 0.0, NEG)] * NSA_HPG, axis=1)
        q_part = jnp.concatenate([q_rot[:, h * tq + u * pw:h * tq + (u + 1) * pw] for h in range(NSA_HPG)],
                                 axis=1)
        s_w = _dot(kw_ref[pl.ds(ws, span), :], q_part) + bias_w
        p_w = jnp.exp((s_w - jnp.max(s_w, axis=0, keepdims=True)).astype(BF16))
        parts.append(split_sum(_dot(vwt_ref[:, pl.ds(ws, span)], p_w)))
    o_w = jnp.concatenate([parts[u][:, h * pw:(h + 1) * pw]
                           for h in range(NSA_HPG) for u in range(tq // pw)], axis=1)

    c_idx = lax.broadcasted_iota(jnp.int32, (n_cmp_rows, 1), 0)
    valid = tile_chain(jnp.where(c_idx * CMP_STRIDE + (CMP_LEN - 1) <= t_row, 1.0, 0.0))
    bias_c = (valid - 1.0) * (-NEG)
    o_c = []
    p_sum = None
    for c in chains:
        s_c = _dot(kcmp_ref[...], q_raw[:, c]) + bias_c
        e_c = jnp.exp(s_c - jnp.max(s_c, axis=0, keepdims=True)) * valid
        l_c = jnp.sum(e_c, axis=0, keepdims=True)
        p_c = e_c / jnp.where(l_c > 0.0, l_c, 1.0)
        o_c.append(_dot(vct_ref[...], p_c.astype(BF16)))
        for h in range(heads_per_chain):
            p_h = p_c[:, h * tq:(h + 1) * tq]
            p_sum = p_h if p_sum is None else p_sum + p_h
    o_c = jnp.concatenate(o_c, axis=1)

    p_hi = p_sum.astype(BF16)
    p_lo = (p_sum - p_hi.astype(F32)).astype(BF16)
    ovt = ovt_ref[...]
    imp = _dot(ovt, p_hi) + _dot(ovt, p_lo)
    jb = lax.broadcasted_iota(jnp.int32, (n_sel, tq), 0)
    cur = (t0 + lax.broadcasted_iota(jnp.int32, (n_sel, tq), 1)) >> SEL_SHIFT
    forced = (jb == 0) | (jb == cur) | (jb == cur - 1)
    work = jnp.where(forced, FORCE, imp)
    work = jnp.where(jb <= cur, work, NEG)
    sel_t = jnp.zeros((n_sel, tq), F32)
    for _ in range(min(SEL_TOPK, n_sel)):
        best = jnp.max(work, axis=0, keepdims=True)
        first = jnp.min(jnp.where(work == best, jb, n_sel), axis=0, keepdims=True)
        hit = jb == first
        sel_t = jnp.where(hit, 1.0, sel_t)
        work = jnp.where(hit, -jnp.inf, work)
    bias_ref[...] = jnp.where(sel_t > 0.5, 0.0, NEG)

    def sel_tile(kt, carry, causal):
        k0 = pl.multiple_of(kt * tk, tk)
        bias = jnp.concatenate(
            [jnp.broadcast_to(bias_ref[pl.ds(kt * blocks_per_tile + j, 1), :], (SEL_LEN, tq))
             for j in range(blocks_per_tile)], axis=0)
        if causal:
            kpos = k0 + lax.broadcasted_iota(jnp.int32, (tk, 1), 0)
            bias = jnp.where(kpos <= t_row, bias, NEG)
        bias = tile_chain(bias)
        k_t = ks_ref[pl.ds(k0, tk), :]
        v_t = vst_ref[:, pl.ds(k0, tk)]
        out = []
        scores = [_dot(k_t, q_rot[:, c]) + bias for c in chains]
        for (m, acc), s in zip(carry, scores):
            m_new = jnp.maximum(m, jnp.max(s, axis=0, keepdims=True))
            p = jnp.exp((s - m_new).astype(BF16))
            acc = jnp.exp(m - m_new) * acc + _dot(v_t, p)
            out.append((m_new, acc))
        return tuple(out)

    n_full = t0 // tk
    init = tuple((jnp.full((1, chain_w), NEG, F32), jnp.zeros((NSA_DIM + SUM_ROWS, chain_w), F32))
                 for _ in chains)
    carry = lax.fori_loop(0, n_full, functools.partial(sel_tile, causal=False), init)
    carry = sel_tile(n_full, carry, causal=True)
    o_s = jnp.concatenate([split_sum(acc) for _, acc in carry], axis=1)

    gt = gate_ref[...]
    outs = []
    for h in range(NSA_HPG):
        c = slice(h * tq, (h + 1) * tq)
        outs.append(gt[3 * h:3 * h + 1] * o_c[:, c] + gt[3 * h + 1:3 * h + 2] * o_s[:, c]
                    + gt[3 * h + 2:3 * h + 3] * o_w[:, c])
    o_ref[...] = jnp.concatenate(outs, axis=0).T.astype(BF16)


def _nsa(nq, nqr, gates, kcmp, vcmp, ks, vs, kw, vw, batch, seq):
    n = batch * seq
    tq = 512
    tk = 512 if seq % 512 == 0 else seq
    nqb = seq // tq
    n_sel = seq // SEL_LEN
    rows_c = seq // CMP_STRIDE
    gw = NSA_HPG * NSA_DIM
    cs = np.arange(rows_c)[None, :] * CMP_STRIDE
    ss = np.arange(n_sel)[:, None] * SEL_LEN
    n_cmp = (seq - CMP_LEN) // CMP_STRIDE + 1
    ovt = ((cs < ss + SEL_LEN) & (cs + CMP_LEN > ss) & (np.arange(rows_c)[None, :] < n_cmp))
    ovt = jnp.asarray(ovt.astype(np.float32), BF16)

    qspec = pl.BlockSpec((tq, gw), lambda b, g, i: (b * nqb + i, g))
    cspec = pl.BlockSpec((None, None, rows_c, NSA_DIM), lambda b, g, i: (b, g, 0, 0))
    kspec = pl.BlockSpec((None, seq, NSA_DIM), lambda b, g, i: (g, b, 0))
    return pl.pallas_call(
        functools.partial(_nsa_kernel, tq=tq, tk=tk, seq=seq),
        grid=(batch, NSA_KV_GROUPS, nqb),
        in_specs=[qspec, qspec,
                  pl.BlockSpec((None, GATE_LANES, tq), lambda b, g, i: (g, 0, b * nqb + i)),
                  cspec, cspec, pl.BlockSpec(ovt.shape, lambda b, g, i: (0, 0)),
                  kspec, kspec, kspec, kspec],
        out_specs=qspec,
        out_shape=jax.ShapeDtypeStruct((n, NSA_WIDTH), BF16),
        scratch_shapes=[pltpu.VMEM((NSA_DIM + SUM_ROWS, seq), BF16), pltpu.VMEM((NSA_DIM + SUM_ROWS, seq), BF16),
                        pltpu.VMEM((NSA_DIM, rows_c), BF16), pltpu.VMEM((n_sel, tq), F32)],
        compiler_params=_params(3),
        name="nsa",
    )(nq, nqr, gates, kcmp, vcmp, ovt, ks, vs, kw, vw)


def _memkv_kernel(mem_ref, w_ref, kv_ref):
    kv_ref[...] = _dot(mem_ref[...].astype(BF16), w_ref[...]).astype(BF16)


def _memkv(mem2d, w_xkv):
    n = mem2d.shape[0]
    w = w_xkv.astype(BF16)
    return pl.pallas_call(
        _memkv_kernel,
        grid=(n // MEM_LEN,),
        in_specs=[pl.BlockSpec((MEM_LEN, D_MODEL), lambda i: (i, 0)),
                  pl.BlockSpec(w.shape, lambda i: (0, 0))],
        out_specs=pl.BlockSpec((MEM_LEN, 2 * D_MODEL), lambda i: (i, 0)),
        out_shape=jax.ShapeDtypeStruct((n, 2 * D_MODEL), BF16),
        compiler_params=_params(1),
        name="memkv",
    )(mem2d, w)


def _pack_halves(v):
    half = D_MODEL // 2
    hi = pltpu.bitcast(v[:, :half].astype(BF16).astype(F32), jnp.uint32)
    lo = pltpu.bitcast(v[:, half:].astype(BF16).astype(F32), jnp.uint32)
    return hi | (lo >> 16)


def _unpack_halves(words):
    return pltpu.bitcast(words & jnp.uint32(0xFFFF0000), F32), pltpu.bitcast(words << 16, F32)


def _postmix_kernel(x_ref, oret_ref, onsa_ref, kv_ref, wout_ref, wq_ref, wo_ref,
                    g1_ref, b1_ref, g2_ref, b2_ref, x2_ref, x2p_ref):
    mixed = jnp.concatenate([oret_ref[...], onsa_ref[...]], axis=1)
    x1 = _layer_norm(DN_ALPHA * x_ref[...] + _dot(mixed, wout_ref[...]), g1_ref[...], b1_ref[...])
    q = (_dot(x1.astype(BF16), wq_ref[...]) * (XATT_DIM ** -0.5)).astype(BF16)
    heads = []
    for h in range(XATT_HEADS):
        cols = slice(h * XATT_DIM, (h + 1) * XATT_DIM)
        s = _dot_nt(q[:, cols], kv_ref[:, cols])
        m = jnp.max(s, axis=-1, keepdims=True)
        p = jnp.exp(s - m)
        l = jnp.sum(p, axis=-1, keepdims=True)
        heads.append(_dot(p.astype(BF16), kv_ref[:, D_MODEL + h * XATT_DIM:D_MODEL + (h + 1) * XATT_DIM]) / l)
    att = jnp.concatenate(heads, axis=1).astype(BF16)
    x2 = _layer_norm(DN_ALPHA * x1 + _dot(att, wo_ref[...]), g2_ref[...], b2_ref[...])
    x2_ref[...] = x2
    x2p_ref[...] = _pack_halves(x2)


def _postmix(x2d, o_ret, o_nsa, kvx, w_out, w_xq, w_xo, ln1_g, ln1_b, ln2_g, ln2_b, batch0, batch, seq):
    n = batch * seq
    tm = 512 if seq % 512 == 0 else seq
    per_b = seq // tm
    row = lambda w: pl.BlockSpec((tm, w), lambda b, i: (b * per_b + i, 0))
    full = lambda a: pl.BlockSpec(a.shape, lambda b, i: (0,) * a.ndim)
    ws = [w_out.astype(BF16), w_xq.astype(BF16), w_xo.astype(BF16)]
    vecs = [v.reshape(1, D_MODEL) for v in (ln1_g, ln1_b, ln2_g, ln2_b)]
    return pl.pallas_call(
        _postmix_kernel,
        grid=(batch, per_b),
        in_specs=[pl.BlockSpec((tm, D_MODEL), lambda b, i: ((batch0 + b) * per_b + i, 0)),
                  row(RET_WIDTH), row(NSA_WIDTH),
                  pl.BlockSpec((MEM_LEN, 2 * D_MODEL), lambda b, i: (batch0 + b, 0))]
                 + [full(w) for w in ws] + [full(v) for v in vecs],
        out_specs=[row(D_MODEL),
                   row(D_MODEL // 2)],
        out_shape=[jax.ShapeDtypeStruct((n, D_MODEL), F32),
                   jax.ShapeDtypeStruct((n, D_MODEL // 2), jnp.uint32)],
        compiler_params=_params(2),
        name="postmix",
    )(x2d, o_ret, o_nsa, kvx, *ws, *vecs)


def _router_kernel(x_ref, wr_ref, bias_ref, e_ref, rank_ref, w_ref, cnt_ref, cntrow_ref, carry_ref, carryrow_ref):
    tn = x_ref.shape[0]
    per = N_EXPERTS // N_GROUPS

    @pl.when(pl.program_id(0) == 0)
    def _():
        carry_ref[...] = jnp.zeros_like(carry_ref)
        carryrow_ref[...] = jnp.zeros_like(carryrow_ref)

    logits = _dot_nt(wr_ref[...], x_ref[...].astype(BF16))
    scores = jax.nn.sigmoid(logits)
    biased = scores + bias_ref[...]
    b3 = biased.reshape(N_GROUPS, per, tn)
    member = lax.broadcasted_iota(jnp.int32, (N_GROUPS, per, tn), 1)
    top1 = jnp.max(b3, axis=1, keepdims=True)
    first1 = jnp.min(jnp.where(b3 == top1, member, per), axis=1, keepdims=True)
    top2 = jnp.max(jnp.where(member == first1, -jnp.inf, b3), axis=1, keepdims=True)
    gscore = top1 + top2
    gidx = lax.broadcasted_iota(jnp.int32, (N_GROUPS, 1, tn), 0)
    gwork = gscore
    for _ in range(TOPK_GROUPS - 1):
        gbest = jnp.max(gwork, axis=0, keepdims=True)
        gfirst = jnp.min(jnp.where(gwork == gbest, gidx, N_GROUPS), axis=0, keepdims=True)
        gwork = jnp.where(gidx == gfirst, -jnp.inf, gwork)
    kth = jnp.max(gwork, axis=0, keepdims=True)
    work = jnp.where(gscore >= kth, b3, NEG).reshape(N_EXPERTS, tn)
    eidx = lax.broadcasted_iota(jnp.int32, (N_EXPERTS, tn), 0)
    picks = []
    chosen = jnp.zeros((N_EXPERTS, tn), F32)
    for _ in range(TOP_K):
        best = jnp.max(work, axis=0, keepdims=True)
        first = jnp.min(jnp.where(work == best, eidx, N_EXPERTS), axis=0, keepdims=True)
        hit = eidx == first
        picks.append((first, hit))
        chosen = jnp.where(hit, 1.0, chosen)
        work = jnp.where(hit, -jnp.inf, work)

    r_i = lax.broadcasted_iota(jnp.int32, (tn, tn), 0)
    c_i = lax.broadcasted_iota(jnp.int32, (tn, tn), 1)
    before = jnp.where(r_i < c_i, 1.0, 0.0).astype(BF16)
    chosen_b = chosen.astype(BF16)
    rank = _dot(chosen_b, before) + carry_ref[...]
    carry_ref[...] = carry_ref[...] + jnp.sum(chosen, axis=1, keepdims=True)
    carryrow_ref[...] = carryrow_ref[...] + _dot_nt(jnp.ones((8, tn), BF16), chosen_b)
    cnt_ref[...] = carry_ref[...]
    cntrow_ref[...] = carryrow_ref[...]

    wsel = [jnp.sum(jnp.where(hit, scores, 0.0), axis=0, keepdims=True) for _, hit in picks]
    wsum = wsel[0]
    for v in wsel[1:]:
        wsum = wsum + v
    for kk, (first, hit) in enumerate(picks):
        e_ref[kk:kk + 1, :] = first
        rank_ref[kk:kk + 1, :] = jnp.sum(jnp.where(hit, rank, 0.0), axis=0, keepdims=True).astype(jnp.int32)
        w_ref[kk:kk + 1, :] = wsel[kk] / wsum * ROUTED_SCALE


def _router(x2, w_router, router_bias):
    n = x2.shape[0]
    tn = 512 if n % 512 == 0 else n
    wr_t = w_router.T.astype(BF16)
    bias = router_bias.reshape(N_EXPERTS, 1).astype(F32)
    kspec = pl.BlockSpec((TOP_K, tn), lambda i: (0, i))
    return pl.pallas_call(
        _router_kernel,
        grid=(n // tn,),
        in_specs=[pl.BlockSpec((tn, D_MODEL), lambda i: (i, 0)),
                  pl.BlockSpec(wr_t.shape, lambda i: (0, 0)),
                  pl.BlockSpec(bias.shape, lambda i: (0, 0))],
        out_specs=[kspec, kspec, kspec, pl.BlockSpec((N_EXPERTS, 1), lambda i: (0, 0)),
                   pl.BlockSpec((8, N_EXPERTS), lambda i: (0, 0))],
        out_shape=[jax.ShapeDtypeStruct((TOP_K, n), jnp.int32),
                   jax.ShapeDtypeStruct((TOP_K, n), jnp.int32),
                   jax.ShapeDtypeStruct((TOP_K, n), F32),
                   jax.ShapeDtypeStruct((N_EXPERTS, 1), F32),
                   jax.ShapeDtypeStruct((8, N_EXPERTS), F32)],
        scratch_shapes=[pltpu.VMEM((N_EXPERTS, 1), F32), pltpu.VMEM((8, N_EXPERTS), F32)],
        compiler_params=_params(1),
        name="router",
    )(x2, wr_t, bias)


def _slots_kernel(e_ref, rank_ref, cnt_ref, cntrow_ref, dest_ref, blk_e_ref, valid_ref, *, blk, n_blocks):
    pad = lambda c: jnp.ceil(c / blk) * blk
    cnt = cnt_ref[...]
    padded = pad(cnt)
    padded_row = pad(cntrow_ref[0:1, :])
    r_i = lax.broadcasted_iota(jnp.int32, (N_EXPERTS, N_EXPERTS), 0)
    c_i = lax.broadcasted_iota(jnp.int32, (N_EXPERTS, N_EXPERTS), 1)
    start = jnp.sum(jnp.where(c_i < r_i, padded_row, 0.0), axis=1, keepdims=True)
    end = start + padded
    e = e_ref[...]
    dest = rank_ref[...]
    for ex in range(N_EXPERTS):
        dest = dest + jnp.where(e == ex, start[ex:ex + 1, :].astype(jnp.int32), 0)
    dest_ref[...] = dest
    bstart = (lax.broadcasted_iota(jnp.int32, (1, n_blocks), 1) * blk).astype(F32)
    owner = jnp.sum(jnp.where(end <= bstart, 1.0, 0.0), axis=0, keepdims=True)
    blk_e_ref[...] = jnp.minimum(owner, N_EXPERTS - 1.0).astype(jnp.int32)
    inside = (start <= bstart) & (bstart < end)
    real = jnp.clip(start + cnt - bstart, 0.0, float(blk))
    valid_ref[...] = jnp.sum(jnp.where(inside, real, 0.0), axis=0, keepdims=True).astype(jnp.int32)


def _slots(e_k, rank_k, counts, counts_row, blk, n_blocks):
    n = e_k.shape[1]
    full = lambda shape: pl.BlockSpec(shape, lambda: (0,) * len(shape))
    return pl.pallas_call(
        functools.partial(_slots_kernel, blk=blk, n_blocks=n_blocks),
        in_specs=[full((TOP_K, n)), full((TOP_K, n)), full((N_EXPERTS, 1)), full((8, N_EXPERTS))],
        out_specs=[full((TOP_K, n)), full((1, n_blocks)), full((1, n_blocks))],
        out_shape=[jax.ShapeDtypeStruct((TOP_K, n), jnp.int32),
                   jax.ShapeDtypeStruct((1, n_blocks), jnp.int32),
                   jax.ShapeDtypeStruct((1, n_blocks), jnp.int32)],
        compiler_params=pltpu.CompilerParams(vmem_limit_bytes=VMEM_LIMIT),
        name="slots",
    )(e_k, rank_k, counts, counts_row)


def _sc_worker_base(per_worker):
    return (lax.axis_index("s") * SC_CORES + lax.axis_index("c")) * per_worker


def _sc_scatter_rows(rows, idx, n_out):
    n, width = rows.shape
    k_lists = idx.shape[0] // n
    workers = SC_CORES * SC_SUBCORES
    per_worker = n // workers
    assert per_worker * workers == n and per_worker % SC_CHUNK == 0
    mesh = plsc.VectorSubcoreMesh(core_axis_name="c", subcore_axis_name="s")

    @functools.partial(
        pl.kernel, mesh=mesh,
        out_type=jax.ShapeDtypeStruct((n_out, width), rows.dtype),
        scratch_types=[pltpu.VMEM((SC_CHUNK, width), rows.dtype)]
                      + [pltpu.VMEM((SC_CHUNK,), jnp.int32)] * k_lists + [pltpu.SemaphoreType.DMA] * 3,
        name="sc_scatter")
    def scatter(rows_hbm, idx_hbm, out_hbm, rows_v, *rest):
        idx_vs = rest[:k_lists]
        sem_rows, sem_idx, sem_out = rest[k_lists:]
        base = _sc_worker_base(per_worker)

        @pl.loop(0, per_worker // SC_CHUNK)
        def _(ci):
            off = pl.multiple_of(base + ci * SC_CHUNK, SC_CHUNK)
            loads = [pltpu.async_copy(rows_hbm.at[pl.ds(off, SC_CHUNK)], rows_v, sem_rows)]
            loads += [pltpu.async_copy(idx_hbm.at[pl.ds(pl.multiple_of(k * n + off, SC_CHUNK), SC_CHUNK)],
                                       idx_vs[k], sem_idx) for k in range(k_lists)]
            for c in loads:
                c.wait()
            copies = [pltpu.async_copy(rows_v, out_hbm.at[idx_vs[k]], sem_out) for k in range(k_lists)]
            for c in copies:
                c.wait()

    return scatter(rows, idx)


def _experts_kernel(blk_e_ref, valid_ref, xs_ref, wg_ref, wu_ref, wd_ref, y_ref, wg_b, wu_b, wd_b):
    i = pl.program_id(0)
    valid = valid_ref[i]

    @pl.when((i == 0) | (blk_e_ref[i] != blk_e_ref[jnp.maximum(i - 1, 0)]))
    def _():
        wg_b[...] = wg_ref[...].astype(BF16)
        wu_b[...] = wu_ref[...].astype(BF16)
        wd_b[...] = wd_ref[...].astype(BF16)

    @pl.when(valid > 0)
    def _():
        half = D_MODEL // 2
        row = lax.broadcasted_iota(jnp.int32, (xs_ref.shape[0], 1), 0)
        hi, lo = (v.astype(BF16) for v in _unpack_halves(jnp.where(row < valid, xs_ref[...], jnp.uint32(0))))
        gate = _dot(hi, wg_b[:half, :]) + _dot(lo, wg_b[half:, :])
        up = _dot(hi, wu_b[:half, :]) + _dot(lo, wu_b[half:, :])
        y_ref[...] = _pack_halves(_dot((jax.nn.silu(gate) * up).astype(BF16), wd_b[...]))

    @pl.when(valid <= 0)
    def _():
        y_ref[...] = jnp.zeros_like(y_ref)


def _experts(blk_e, valid, xs, w_gate, w_up, w_down, blk):
    cap, width = xs.shape
    wspec = lambda a: pl.BlockSpec((None,) + a.shape[1:], lambda i, be, nv: (be[i], 0, 0))
    rows = pl.BlockSpec((blk, width), lambda i, be, nv: (i, 0))
    return pl.pallas_call(
        _experts_kernel,
        grid_spec=pltpu.PrefetchScalarGridSpec(
            num_scalar_prefetch=2,
            grid=(cap // blk,),
            in_specs=[rows, wspec(w_gate), wspec(w_up), wspec(w_down)],
            out_specs=rows,
            scratch_shapes=[pltpu.VMEM(w.shape[1:], BF16) for w in (w_gate, w_up, w_down)],
        ),
        out_shape=jax.ShapeDtypeStruct(xs.shape, xs.dtype),
        compiler_params=_params(1),
        name="experts",
    )(blk_e, valid, xs, w_gate, w_up, w_down)


def _sc_gather_rows(table, idx):
    b, width = idx.shape[0], table.shape[1]
    workers = SC_CORES * SC_SUBCORES
    per_worker = b // workers
    assert per_worker * workers == b and per_worker % (SC_CHUNK * SC_INFLIGHT) == 0
    mesh = plsc.VectorSubcoreMesh(core_axis_name="c", subcore_axis_name="s")

    @functools.partial(
        pl.kernel, mesh=mesh,
        out_type=jax.ShapeDtypeStruct((b, width), table.dtype),
        scratch_types=[pltpu.VMEM((SC_CHUNK,), jnp.int32)] * SC_INFLIGHT
                      + [pltpu.VMEM((SC_CHUNK, width), table.dtype)] * SC_INFLIGHT
                      + [pltpu.SemaphoreType.DMA] * (1 + 2 * SC_INFLIGHT),
        name="sc_gather")
    def gather(table_hbm, idx_hbm, out_hbm, *scratch):
        idx_vs = scratch[:SC_INFLIGHT]
        rows_vs = scratch[SC_INFLIGHT:2 * SC_INFLIGHT]
        sem_idx = scratch[2 * SC_INFLIGHT]
        sem_rows = scratch[2 * SC_INFLIGHT + 1:3 * SC_INFLIGHT + 1]
        sem_out = scratch[3 * SC_INFLIGHT + 1:]
        base = _sc_worker_base(per_worker)
        lanes = range(SC_INFLIGHT)

        @pl.loop(0, per_worker // (SC_CHUNK * SC_INFLIGHT))
        def _(gi):
            offs = [pl.multiple_of(base + (gi * SC_INFLIGHT + j) * SC_CHUNK, SC_CHUNK) for j in lanes]
            loads = [pltpu.async_copy(idx_hbm.at[pl.ds(offs[j], SC_CHUNK)], idx_vs[j], sem_idx) for j in lanes]
            for c in loads:
                c.wait()
            gathers = [pltpu.async_copy(table_hbm.at[idx_vs[j]], rows_vs[j], sem_rows[j]) for j in lanes]
            writes = []
            for j in lanes:
                gathers[j].wait()
                writes.append(pltpu.async_copy(rows_vs[j], out_hbm.at[pl.ds(offs[j], SC_CHUNK)], sem_out[j]))
            for c in writes:
                c.wait()

    return gather(table, idx)


def _combine_kernel(x_ref, wk_ref, yk_ref, wsg_ref, wsu_ref, wsd_ref, g_ref, b_ref, *rest):
    o_ref = rest[-1]
    x = x_ref[...]
    xb = x.astype(BF16)
    shared = _dot((jax.nn.silu(_dot(xb, wsg_ref[...])) * _dot(xb, wsu_ref[...])).astype(BF16), wsd_ref[...])
    wk = wk_ref[...]
    routed_hi = routed_lo = None
    for kk in range(TOP_K):
        hi, lo = _unpack_halves(yk_ref[kk])
        w = wk[:, kk:kk + 1]
        routed_hi = hi * w if kk == 0 else routed_hi + hi * w
        routed_lo = lo * w if kk == 0 else routed_lo + lo * w
    routed = jnp.concatenate([routed_hi, routed_lo], axis=1)
    o_ref[...] = _layer_norm(DN_ALPHA * x + (routed + shared), g_ref[...], b_ref[...])


def _combine(x2, w_tok, yk, ws_gate, ws_up, ws_down, ln3_g, ln3_b, row0, n_total, out_prev):
    n = x2.shape[0]
    tt = 512 if n % 512 == 0 and row0 % 512 == 0 else n
    blk0 = row0 // tt
    ws = [ws_gate.astype(BF16), ws_up.astype(BF16), ws_down.astype(BF16)]
    vecs = [ln3_g.reshape(1, D_MODEL), ln3_b.reshape(1, D_MODEL)]
    full = lambda a: pl.BlockSpec(a.shape, lambda i: (0,) * a.ndim)
    args = [x2, w_tok, yk, *ws, *vecs]
    in_specs = ([pl.BlockSpec((tt, D_MODEL), lambda i: (i, 0)),
                 pl.BlockSpec((tt, TOP_K), lambda i: (i, 0)),
                 pl.BlockSpec((TOP_K, tt, D_MODEL // 2), lambda i: (0, i, 0))]
                + [full(a) for a in ws] + [full(v) for v in vecs])
    aliases = {}
    if out_prev is not None:
        aliases = {len(args): 0}
        args.append(out_prev)
        in_specs.append(pl.BlockSpec(memory_space=pl.ANY))
    return pl.pallas_call(
        _combine_kernel,
        grid=(n // tt,),
        in_specs=in_specs,
        out_specs=pl.BlockSpec((tt, D_MODEL), lambda i: (blk0 + i, 0)),
        out_shape=jax.ShapeDtypeStruct((n_total, D_MODEL), F32),
        input_output_aliases=aliases,
        compiler_params=_params(1),
        name="combine",
    )(*args)


def _moe_and_norm(x2, x2p, w_router, router_bias, w_gate, w_up, w_down, ws_gate, ws_up, ws_down,
                  ln3_g, ln3_b, row0, n_total, out_prev):
    n = x2.shape[0]
    blk = 512
    cap = n * TOP_K + N_EXPERTS * blk
    n_blocks = cap // blk
    e_k, rank_k, w_k, counts, counts_row = _router(x2, w_router, router_bias)
    dest, blk_e, valid = _slots(e_k, rank_k, counts, counts_row, blk, n_blocks)
    dest = dest.reshape(-1)
    xs = _sc_scatter_rows(x2p, dest, cap)
    y = _experts(blk_e.reshape(-1), valid.reshape(-1), xs, w_gate, w_up, w_down, blk)
    yk = _sc_gather_rows(y, dest).reshape(TOP_K, n, D_MODEL // 2)
    return _combine(x2, w_k.T, yk, ws_gate, ws_up, ws_down, ln3_g, ln3_b, row0, n_total, out_prev)


def _layer(x, mem, positions, w_in, cmp_pe_k, cmp_pe_v, cmp_w1_k, cmp_w2_k, cmp_w1_v, cmp_w2_v,
           w_out, ln1_g, ln1_b, w_xq, w_xkv, w_xo, ln2_g, ln2_b, w_router, router_bias,
           w_gate, w_up, w_down, ws_gate, ws_up, ws_down, ln3_g, ln3_b):
    batch, seq, _ = x.shape
    n_total = batch * seq
    x2d = x.reshape(n_total, D_MODEL)
    pos_col = positions.astype(F32).reshape(n_total, 1)
    kvx = _memkv(mem.reshape(batch * MEM_LEN, D_MODEL), w_xkv)
    groups = BATCH_GROUPS if batch % BATCH_GROUPS == 0 else 1
    per = batch // groups
    n = per * seq
    out = None
    for gi in range(groups):
        (rq, rk, rv, rg, nq, nqr, kc, vc, ks, vs, kw, vw, gates) = _inproj(x2d, pos_col, w_in, gi * n, n)
        o_ret = _retention(rq, rk, rv, rg, per, seq)
        kcmp = _compress(kc, cmp_pe_k, cmp_w1_k, cmp_w2_k, per, seq)
        vcmp = _compress(vc, cmp_pe_v, cmp_w1_v, cmp_w2_v, per, seq)
        o_nsa = _nsa(nq, nqr, gates, kcmp, vcmp, ks, vs, kw, vw, per, seq)
        x2, x2p = _postmix(x2d, o_ret, o_nsa, kvx, w_out, w_xq, w_xo, ln1_g, ln1_b, ln2_g, ln2_b,
                           gi * per, per, seq)
        out = _moe_and_norm(x2, x2p, w_router, router_bias, w_gate, w_up, w_down,
                            ws_gate, ws_up, ws_down, ln3_g, ln3_b, gi * n, n_total, out)
    return out.reshape(batch, seq, D_MODEL)


def kernel(x, mem, positions, w_in, cmp_pe_k, cmp_pe_v, cmp_w1_k, cmp_w2_k, cmp_w1_v, cmp_w2_v, w_out, ln1_g, ln1_b, w_xq, w_xkv, w_xo, ln2_g, ln2_b, w_router, router_bias, w_gate, w_up, w_down, ws_gate, ws_up, ws_down, ln3_g, ln3_b):
    for l in range(DEPTH):
        x = _layer(x, mem, positions, w_in[l], cmp_pe_k[l], cmp_pe_v[l], cmp_w1_k[l], cmp_w2_k[l],
                   cmp_w1_v[l], cmp_w2_v[l], w_out[l], ln1_g[l], ln1_b[l], w_xq[l], w_xkv[l],
                   w_xo[l], ln2_g[l], ln2_b[l], w_router[l], router_bias[l], w_gate[l], w_up[l],
                   w_down[l], ws_gate[l], ws_up[l], ws_down[l], ln3_g[l], ln3_b[l])
    return x
```

```python
import functools

import numpy as np
import jax
import jax.numpy as jnp
from jax import lax
from jax.experimental import pallas as pl
from jax.experimental.pallas import tpu as pltpu
from jax.experimental.pallas import tpu_sc as plsc

D_MODEL = 1024
MEM_LEN = 256
DEPTH = 1
DN_ALPHA = (2 * DEPTH) ** 0.25
LN_EPS = 1e-5
NEG = -1e30
FORCE = 1e9

RET_HEADS = 4
RET_DIM = 128
RET_CHUNK = 128
RET_ROPE_BASE = 10000.0
RET_STEP_CHUNKS = 4
RET_WIDTH = RET_HEADS * RET_DIM

NSA_HEADS = 8
NSA_KV_GROUPS = 2
NSA_HPG = NSA_HEADS // NSA_KV_GROUPS
NSA_DIM = 64
NSA_WIDTH = NSA_HEADS * NSA_DIM
KV_WIDTH = NSA_KV_GROUPS * NSA_DIM
CMP_LEN = 32
CMP_STRIDE = 16
CMP_HIDDEN = 256
SEL_LEN = 64
SEL_SHIFT = 6
SEL_TOPK = 16
WIN = 512
ROPE_THETA = 500000.0
ROPE_DIMS = NSA_DIM // 4
GATE_LANES = 16
NSA_CHAINS = 1
SUM_ROWS = 16
WIN_PART = 256

BATCH_GROUPS = 2

SC_CORES = 2
SC_SUBCORES = 16
SC_CHUNK = 64
SC_INFLIGHT = 2

XATT_HEADS = 4
XATT_DIM = D_MODEL // XATT_HEADS

N_EXPERTS = 64
TOP_K = 8
N_GROUPS = 8
TOPK_GROUPS = 4
EXPERT_FF = 256
SHARED_FF = 256
ROUTED_SCALE = 2.5

LANES = 128
VMEM_LIMIT = 56 * 1024 * 1024

F32 = jnp.float32
BF16 = jnp.bfloat16
NT_DIMS = (((1,), (1,)), ((), ()))


def _params(n_axes):
    return pltpu.CompilerParams(dimension_semantics=("arbitrary",) * n_axes,
                                vmem_limit_bytes=VMEM_LIMIT)


def _dot(a, b):
    return jnp.dot(a, b, preferred_element_type=F32)


def _dot_nt(a, b):
    return lax.dot_general(a, b, NT_DIMS, preferred_element_type=F32)


def _layer_norm(v, g, b):
    mu = jnp.mean(v, axis=-1, keepdims=True)
    d = v - mu
    var = jnp.mean(d * d, axis=-1, keepdims=True)
    return d * lax.rsqrt(var + LN_EPS) * g + b


def _inproj_kernel(x_ref, pos_ref, wret_ref, wnq_ref, wkv_ref, wg_ref, invr_ref, invn_ref,
                   rq_ref, rk_ref, rv_ref, rg_ref, nq_ref, nqr_ref, kc_ref, vc_ref,
                   ks_ref, vs_ref, kw_ref, vw_ref, gate_ref):
    xb = x_ref[...].astype(BF16)
    pos = pos_ref[...]
    lane = lax.broadcasted_iota(jnp.int32, (1, LANES), 1)

    ang = pos * invr_ref[...]
    cos_r = jnp.cos(ang)
    sin_r = jnp.sin(ang)
    sin_r = jnp.where(lane < RET_DIM // 2, -sin_r, sin_r)
    q_all = _dot(xb, wret_ref[:, :RET_WIDTH])
    k_all = _dot(xb, wret_ref[:, RET_WIDTH:2 * RET_WIDTH])
    for h in range(RET_HEADS):
        cols = slice(h * RET_DIM, (h + 1) * RET_DIM)
        q = q_all[:, cols]
        rq_ref[:, cols] = (q * cos_r + pltpu.roll(q, RET_DIM // 2, 1) * sin_r).astype(BF16)
        k = k_all[:, cols]
        k = (k * cos_r + pltpu.roll(k, RET_DIM // 2, 1) * sin_r) * (RET_DIM ** -0.5)
        rk_ref[:, cols] = k.astype(BF16)
    rv_ref[...] = _dot(xb, wret_ref[:, 2 * RET_WIDTH:3 * RET_WIDTH]).astype(BF16)
    rg_ref[...] = _dot(xb, wret_ref[:, 3 * RET_WIDTH:4 * RET_WIDTH]).astype(BF16)

    half = ROPE_DIMS // 2
    j = lane % NSA_DIM
    angn = pos * invn_ref[...]
    cos_n = jnp.cos(angn)
    sin_n = jnp.sin(angn)
    sin_lo = jnp.where(j < half, -sin_n, 0.0)
    sin_hi = jnp.where((j >= half) & (j < 2 * half), sin_n, 0.0)

    def rope_n(v):
        return v * cos_n + pltpu.roll(v, half, 1) * sin_hi + pltpu.roll(v, LANES - half, 1) * sin_lo

    scale = NSA_DIM ** -0.5
    nq_all = _dot(xb, wnq_ref[...])
    for c in range(NSA_WIDTH // LANES):
        cols = slice(c * LANES, (c + 1) * LANES)
        q = nq_all[:, cols]
        nq_ref[:, cols] = (q * scale).astype(BF16)
        nqr_ref[:, cols] = (rope_n(q) * scale).astype(BF16)

    kv_all = _dot(xb, wkv_ref[...])

    def kv(i):
        return kv_all[:, i * KV_WIDTH:(i + 1) * KV_WIDTH]

    def split_groups(ref, v):
        for g in range(NSA_KV_GROUPS):
            ref[g] = v[:, g * NSA_DIM:(g + 1) * NSA_DIM].astype(BF16)

    kc_ref[...] = kv(0)
    vc_ref[...] = kv(1)
    split_groups(ks_ref, rope_n(kv(2)))
    split_groups(vs_ref, kv(3))
    split_groups(kw_ref, rope_n(kv(4)))
    split_groups(vw_ref, kv(5))

    gt = jax.nn.sigmoid(_dot_nt(wg_ref[...], xb))
    for g in range(NSA_KV_GROUPS):
        gate_ref[g] = gt[g * GATE_LANES:(g + 1) * GATE_LANES, :]


def _inproj(x2d, pos_col, w_in, row0, n):
    tm = 512 if n % 512 == 0 and row0 % 512 == 0 else n
    blk0 = row0 // tm
    off = np.cumsum([0] + [RET_WIDTH] * 4 + [NSA_WIDTH] + [KV_WIDTH] * 6)
    w_ret = w_in[:, :off[4]].astype(BF16)
    w_nq = w_in[:, off[4]:off[5]].astype(BF16)
    w_kv = w_in[:, off[5]:off[11]].astype(BF16)
    wg = w_in[:, off[11]:].reshape(D_MODEL, NSA_KV_GROUPS, NSA_HPG * 3)
    wg = jnp.pad(wg, ((0, 0), (0, 0), (0, GATE_LANES - NSA_HPG * 3)))
    wg = wg.reshape(D_MODEL, NSA_KV_GROUPS * GATE_LANES).T.astype(BF16)

    lane = np.arange(LANES)
    half_r = RET_DIM // 2
    inv_r = (np.float32(RET_ROPE_BASE) ** (-np.arange(half_r, dtype=np.float32) / np.float32(half_r)))
    inv_r = inv_r.astype(np.float32)[lane % half_r][None, :]
    half_n = ROPE_DIMS // 2
    inv_n = (np.float32(ROPE_THETA) ** (-np.arange(half_n, dtype=np.float32) / np.float32(half_n)))
    jn = lane % NSA_DIM
    inv_n = np.where(jn < ROPE_DIMS, inv_n.astype(np.float32)[jn % half_n], np.float32(0.0))[None, :]

    row = lambda w: pl.BlockSpec((tm, w), lambda i: (i, 0))
    src_row = lambda w: pl.BlockSpec((tm, w), lambda i: (blk0 + i, 0))
    full = lambda a: pl.BlockSpec(a.shape, lambda i: (0,) * a.ndim)
    grp = lambda w: pl.BlockSpec((NSA_KV_GROUPS, tm, w), lambda i: (0, i, 0))
    bf = lambda w: jax.ShapeDtypeStruct((n, w), BF16)
    gbf = jax.ShapeDtypeStruct((NSA_KV_GROUPS, n, NSA_DIM), BF16)
    inv_r = jnp.asarray(inv_r, F32)
    inv_n = jnp.asarray(inv_n, F32)
    return pl.pallas_call(
        _inproj_kernel,
        grid=(n // tm,),
        in_specs=[src_row(D_MODEL), src_row(1), full(w_ret), full(w_nq), full(w_kv), full(wg),
                  full(inv_r), full(inv_n)],
        out_specs=[row(RET_WIDTH)] * 4 + [row(NSA_WIDTH)] * 2 + [row(KV_WIDTH)] * 2
                  + [grp(NSA_DIM)] * 4
                  + [pl.BlockSpec((NSA_KV_GROUPS, GATE_LANES, tm), lambda i: (0, 0, i))],
        out_shape=[bf(RET_WIDTH)] * 4 + [bf(NSA_WIDTH)] * 2
                  + [jax.ShapeDtypeStruct((n, KV_WIDTH), F32)] * 2 + [gbf] * 4
                  + [jax.ShapeDtypeStruct((NSA_KV_GROUPS, GATE_LANES, n), F32)],
        compiler_params=_params(1),
        name="inproj",
    )(x2d, pos_col, w_ret, w_nq, w_kv, wg, inv_r, inv_n)


def _retention_kernel(q_ref, k_ref, v_ref, g_ref, o_ref, state_ref):
    c = RET_CHUNK

    @pl.when(pl.program_id(1) == 0)
    def _():
        state_ref[...] = jnp.zeros_like(state_ref)

    row = lax.broadcasted_iota(jnp.int32, (c, c), 0)
    col = lax.broadcasted_iota(jnp.int32, (c, c), 1)
    rel = (row - col).astype(F32)
    idx = lax.broadcasted_iota(jnp.int32, (c, 1), 0).astype(F32)
    for h in range(RET_HEADS):
        log_g = float(np.log(np.float32(1.0) - np.float32(2.0) ** np.float32(-5.0 - h)))
        cols = slice(h * RET_DIM, (h + 1) * RET_DIM)
        dmask = jnp.where(rel >= 0, jnp.exp(log_g * jnp.maximum(rel, 0.0)), 0.0)
        zeta = jnp.exp(log_g * (c - 1.0 - idx))
        xi = jnp.exp(log_g * (idx + 1.0))
        for j in range(q_ref.shape[0] // c):
            rows = slice(j * c, (j + 1) * c)
            q = q_ref[rows, cols]
            k = k_ref[rows, cols]
            v = v_ref[rows, cols]
            scores = _dot_nt(q, k) * dmask
            inner = _dot(scores.astype(BF16), v)
            prev = state_ref[h]
            cross = _dot(q, prev.astype(BF16)) * xi
            kz = (k.astype(F32) * zeta).astype(BF16)
            kv = lax.dot_general(kz, v, (((0,), (0,)), ((), ())), preferred_element_type=F32)
            state_ref[h] = prev * float(np.exp(np.float32(log_g) * np.float32(c))) + kv
            o = inner + cross
            mu = jnp.mean(o, axis=-1, keepdims=True)
            d = o - mu
            var = jnp.mean(d * d, axis=-1, keepdims=True)
            o = d * lax.rsqrt(var + LN_EPS)
            o_ref[rows, cols] = (jax.nn.silu(g_ref[rows, cols].astype(F32)) * o).astype(BF16)


def _retention(rq, rk, rv, rg, batch, seq):
    per_step = RET_STEP_CHUNKS if (seq // RET_CHUNK) % RET_STEP_CHUNKS == 0 else 1
    nc = seq // (RET_CHUNK * per_step)
    spec = pl.BlockSpec((RET_CHUNK * per_step, RET_WIDTH), lambda b, n: (b * nc + n, 0))
    return pl.pallas_call(
        _retention_kernel,
        grid=(batch, nc),
        in_specs=[spec] * 4,
        out_specs=spec,
        out_shape=jax.ShapeDtypeStruct(rq.shape, BF16),
        scratch_shapes=[pltpu.VMEM((RET_HEADS, RET_DIM, RET_DIM), F32)],
        compiler_params=_params(2),
        name="retention",
    )(rq, rk, rv, rg)


def _compress_kernel(a_ref, pe_ref, w1_ref, w2_ref, o_ref, shift_ref, *, n_cmp):
    rows = a_ref.shape[0]
    a = a_ref[...]
    lo = (a + pe_ref[0]).astype(BF16)
    hi = (a + pe_ref[1]).astype(BF16)
    ridx = lax.broadcasted_iota(jnp.int32, (rows, 1), 0)
    shift_ref[rows:rows + 8, :] = jnp.zeros((8, CMP_HIDDEN), F32)
    for g in range(NSA_KV_GROUPS):
        p = _dot(lo, w1_ref[0, g])
        shift_ref[0:rows, :] = _dot(hi, w1_ref[1, g])
        hid = jax.nn.silu(p + shift_ref[pl.ds(1, rows), :])
        out = _dot(hid.astype(BF16), w2_ref[...])
        o_ref[g] = jnp.where(ridx < n_cmp, out, 0.0).astype(BF16)


def _compress(a, pe, w1, w2, batch, seq):
    rows = seq // CMP_STRIDE
    per = CMP_STRIDE * KV_WIDTH
    n_cmp = (seq - CMP_LEN) // CMP_STRIDE + 1
    a2 = a.reshape(batch * rows, per)
    pe2 = jnp.tile(pe.reshape(2, CMP_STRIDE, 1, NSA_DIM), (1, 1, NSA_KV_GROUPS, 1)).reshape(2, 1, per)
    w1r = w1.reshape(2, CMP_STRIDE, 1, NSA_DIM, CMP_HIDDEN)
    eye = jnp.eye(NSA_KV_GROUPS, dtype=w1.dtype).reshape(1, NSA_KV_GROUPS, 1, NSA_KV_GROUPS, 1, 1)
    w1x = (w1r[:, None] * eye).reshape(2, NSA_KV_GROUPS, per, CMP_HIDDEN).astype(BF16)
    w2b = w2.astype(BF16)
    full = lambda arr: pl.BlockSpec(arr.shape, lambda b: (0,) * arr.ndim)
    return pl.pallas_call(
        functools.partial(_compress_kernel, n_cmp=n_cmp),
        grid=(batch,),
        in_specs=[pl.BlockSpec((rows, per), lambda b: (b, 0)), full(pe2), full(w1x), full(w2b)],
        out_specs=pl.BlockSpec((None, NSA_KV_GROUPS, rows, NSA_DIM), lambda b: (b, 0, 0, 0)),
        out_shape=jax.ShapeDtypeStruct((batch, NSA_KV_GROUPS, rows, NSA_DIM), BF16),
        scratch_shapes=[pltpu.VMEM((rows + 8, CMP_HIDDEN), F32)],
        compiler_params=_params(1),
        name="compress",
    )(a2, pe2, w1x, w2b)


def _heads_to_lanes(ref):
    vt = ref[...].astype(F32).T
    return jnp.concatenate([vt[h * NSA_DIM:(h + 1) * NSA_DIM] for h in range(NSA_HPG)], axis=1).astype(BF16)


def _tile_heads(v):
    return jnp.concatenate([v] * NSA_HPG, axis=1)


def _transpose_into(dst_ref, src_ref, chunk):
    def step(c, _):
        c0 = pl.multiple_of(c * chunk, chunk)
        dst_ref[:NSA_DIM, pl.ds(c0, chunk)] = src_ref[pl.ds(c0, chunk), :].astype(F32).T.astype(BF16)
        return 0
    lax.fori_loop(0, src_ref.shape[0] // chunk, step, 0)


def _nsa_kernel(qraw_ref, qrot_ref, gate_ref, kcmp_ref, vcmp_ref, ovt_ref,
                ks_ref, vs_ref, kw_ref, vw_ref, o_ref, vst_ref, vwt_ref, vct_ref, bias_ref, *, tq, tk, seq):
    i = pl.program_id(2)
    t0 = i * tq
    cols = NSA_HPG * tq
    n_sel = seq // SEL_LEN
    n_cmp_rows = seq // CMP_STRIDE
    blocks_per_tile = tk // SEL_LEN

    @pl.when(i == 0)
    def _():
        chunk = min(512, n_cmp_rows)
        _transpose_into(vst_ref, vs_ref, chunk)
        _transpose_into(vwt_ref, vw_ref, chunk)
        _transpose_into(vct_ref, vcmp_ref, chunk)
        vst_ref[NSA_DIM:, :] = jnp.ones((SUM_ROWS, seq), BF16)
        vwt_ref[NSA_DIM:, :] = jnp.ones((SUM_ROWS, seq), BF16)

    def split_sum(acc):
        return acc[:NSA_DIM] / acc[NSA_DIM:NSA_DIM + 1]

    q_raw = _heads_to_lanes(qraw_ref)
    q_rot = _heads_to_lanes(qrot_ref)
    t_row = t0 + lax.broadcasted_iota(jnp.int32, (1, tq), 1)

    chain_w = cols // NSA_CHAINS
    heads_per_chain = chain_w // tq
    chains = [slice(c * chain_w, (c + 1) * chain_w) for c in range(NSA_CHAINS)]
    tile_chain = lambda v: jnp.concatenate([v] * heads_per_chain, axis=1)

    pw = min(tq, WIN_PART)
    parts = []
    for u in range(tq // pw):
        span = WIN + pw
        ws = pl.multiple_of(jnp.maximum(t0 + u * pw - WIN, 0), pw)
        dist = t_row[:, u * pw:(u + 1) * pw] - (ws + lax.broadcasted_iota(jnp.int32, (span, 1), 0))
        bias_w = jnp.concatenate([jnp.where((dist >= 0) & (dist < WIN), 0.0, NEG)] * NSA_HPG, axis=1)
        q_part = jnp.concatenate([q_rot[:, h * tq + u * pw:h * tq + (u + 1) * pw] for h in range(NSA_HPG)],
                                 axis=1)
        s_w = _dot(kw_ref[pl.ds(ws, span), :], q_part) + bias_w
        p_w = jnp.exp((s_w - jnp.max(s_w, axis=0, keepdims=True)).astype(BF16))
        parts.append(split_sum(_dot(vwt_ref[:, pl.ds(ws, span)], p_w)))
    o_w = jnp.concatenate([parts[u][:, h * pw:(h + 1) * pw]
                           for h in range(NSA_HPG) for u in range(tq // pw)], axis=1)

    c_idx = lax.broadcasted_iota(jnp.int32, (n_cmp_rows, 1), 0)
    valid = tile_chain(jnp.where(c_idx * CMP_STRIDE + (CMP_LEN - 1) <= t_row, 1.0, 0.0))
    bias_c = (valid - 1.0) * (-NEG)
    o_c = []
    p_sum = None
    for c in chains:
        s_c = _dot(kcmp_ref[...], q_raw[:, c]) + bias_c
        e_c = jnp.exp(s_c - jnp.max(s_c, axis=0, keepdims=True)) * valid
        l_c = jnp.sum(e_c, axis=0, keepdims=True)
        p_c = e_c / jnp.where(l_c > 0.0, l_c, 1.0)
        o_c.append(_dot(vct_ref[...], p_c.astype(BF16)))
        for h in range(heads_per_chain):
            p_h = p_c[:, h * tq:(h + 1) * tq]
            p_sum = p_h if p_sum is None else p_sum + p_h
    o_c = jnp.concatenate(o_c, axis=1)

    p_hi = p_sum.astype(BF16)
    p_lo = (p_sum - p_hi.astype(F32)).astype(BF16)
    ovt = ovt_ref[...]
    imp = _dot(ovt, p_hi) + _dot(ovt, p_lo)
    jb = lax.broadcasted_iota(jnp.int32, (n_sel, tq), 0)
    cur = (t0 + lax.broadcasted_iota(jnp.int32, (n_sel, tq), 1)) >> SEL_SHIFT
    forced = (jb == 0) | (jb == cur) | (jb == cur - 1)
    work = jnp.where(forced, FORCE, imp)
    work = jnp.where(jb <= cur, work, NEG)
    sel_t = jnp.zeros((n_sel, tq), F32)
    for _ in range(min(SEL_TOPK, n_sel)):
        best = jnp.max(work, axis=0, keepdims=True)
        first = jnp.min(jnp.where(work == best, jb, n_sel), axis=0, keepdims=True)
        hit = jb == first
        sel_t = jnp.where(hit, 1.0, sel_t)
        work = jnp.where(hit, -jnp.inf, work)
    bias_ref[...] = jnp.where(sel_t > 0.5, 0.0, NEG)

    def sel_tile(kt, carry, causal):
        k0 = pl.multiple_of(kt * tk, tk)
        bias = jnp.concatenate(
            [jnp.broadcast_to(bias_ref[pl.ds(kt * blocks_per_tile + j, 1), :], (SEL_LEN, tq))
             for j in range(blocks_per_tile)], axis=0)
        if causal:
            kpos = k0 + lax.broadcasted_iota(jnp.int32, (tk, 1), 0)
            bias = jnp.where(kpos <= t_row, bias, NEG)
        bias = tile_chain(bias)
        k_t = ks_ref[pl.ds(k0, tk), :]
        v_t = vst_ref[:, pl.ds(k0, tk)]
        out = []
        scores = [_dot(k_t, q_rot[:, c]) + bias for c in chains]
        for (m, acc), s in zip(carry, scores):
            m_new = jnp.maximum(m, jnp.max(s, axis=0, keepdims=True))
            p = jnp.exp((s - m_new).astype(BF16))
            acc = jnp.exp(m - m_new) * acc + _dot(v_t, p)
            out.append((m_new, acc))
        return tuple(out)

    n_full = t0 // tk
    init = tuple((jnp.full((1, chain_w), NEG, F32), jnp.zeros((NSA_DIM + SUM_ROWS, chain_w), F32))
                 for _ in chains)
    carry = lax.fori_loop(0, n_full, functools.partial(sel_tile, causal=False), init)
    for d in range(max(tq // tk, 1)):
        carry = sel_tile(n_full + d, carry, causal=True)
    o_s = jnp.concatenate([split_sum(acc) for _, acc in carry], axis=1)

    gt = gate_ref[...]
    outs = []
    for h in range(NSA_HPG):
        c = slice(h * tq, (h + 1) * tq)
        outs.append(gt[3 * h:3 * h + 1] * o_c[:, c] + gt[3 * h + 1:3 * h + 2] * o_s[:, c]
                    + gt[3 * h + 2:3 * h + 3] * o_w[:, c])
    o_ref[...] = jnp.concatenate(outs, axis=0).T.astype(BF16)


def _nsa_after_kernel(*refs, n_inputs, **static):
    _nsa_kernel(*refs[:n_inputs], *refs[n_inputs + 1:], **static)


def _nsa(nq, nqr, gates, kcmp, vcmp, ks, vs, kw, vw, batch, seq, after=None):
    n = batch * seq
    tq = 512
    tk = 512 if seq % 512 == 0 else seq
    nqb = seq // tq
    n_sel = seq // SEL_LEN
    rows_c = seq // CMP_STRIDE
    gw = NSA_HPG * NSA_DIM
    cs = np.arange(rows_c)[None, :] * CMP_STRIDE
    ss = np.arange(n_sel)[:, None] * SEL_LEN
    n_cmp = (seq - CMP_LEN) // CMP_STRIDE + 1
    ovt = ((cs < ss + SEL_LEN) & (cs + CMP_LEN > ss) & (np.arange(rows_c)[None, :] < n_cmp))
    ovt = jnp.asarray(ovt.astype(np.float32), BF16)

    qspec = pl.BlockSpec((tq, gw), lambda b, g, i: (b * nqb + i, g))
    cspec = pl.BlockSpec((None, None, rows_c, NSA_DIM), lambda b, g, i: (b, g, 0, 0))
    kspec = pl.BlockSpec((None, seq, NSA_DIM), lambda b, g, i: (g, b, 0))
    args = [nq, nqr, gates, kcmp, vcmp, ovt, ks, vs, kw, vw]
    in_specs = [qspec, qspec,
                pl.BlockSpec((None, GATE_LANES, tq), lambda b, g, i: (g, 0, b * nqb + i)),
                cspec, cspec, pl.BlockSpec(ovt.shape, lambda b, g, i: (0, 0)),
                kspec, kspec, kspec, kspec]
    body = functools.partial(_nsa_kernel, tq=tq, tk=tk, seq=seq)
    if after is not None:
        body = functools.partial(_nsa_after_kernel, n_inputs=len(args), tq=tq, tk=tk, seq=seq)
        args.append(after)
        in_specs.append(pl.BlockSpec(memory_space=pl.ANY))
    return pl.pallas_call(
        body,
        grid=(batch, NSA_KV_GROUPS, nqb),
        in_specs=in_specs,
        out_specs=qspec,
        out_shape=jax.ShapeDtypeStruct((n, NSA_WIDTH), BF16),
        scratch_shapes=[pltpu.VMEM((NSA_DIM + SUM_ROWS, seq), BF16), pltpu.VMEM((NSA_DIM + SUM_ROWS, seq), BF16),
                        pltpu.VMEM((NSA_DIM, rows_c), BF16), pltpu.VMEM((n_sel, tq), F32)],
        compiler_params=_params(3),
        name="nsa",
    )(*args)


def _memkv_kernel(mem_ref, w_ref, kv_ref):
    kv_ref[...] = _dot(mem_ref[...].astype(BF16), w_ref[...]).astype(BF16)


def _memkv(mem2d, w_xkv):
    n = mem2d.shape[0]
    w = w_xkv.astype(BF16)
    return pl.pallas_call(
        _memkv_kernel,
        grid=(n // MEM_LEN,),
        in_specs=[pl.BlockSpec((MEM_LEN, D_MODEL), lambda i: (i, 0)),
                  pl.BlockSpec(w.shape, lambda i: (0, 0))],
        out_specs=pl.BlockSpec((MEM_LEN, 2 * D_MODEL), lambda i: (i, 0)),
        out_shape=jax.ShapeDtypeStruct((n, 2 * D_MODEL), BF16),
        compiler_params=_params(1),
        name="memkv",
    )(mem2d, w)


def _pack_halves(v):
    half = D_MODEL // 2
    hi = pltpu.bitcast(v[:, :half].astype(BF16).astype(F32), jnp.uint32)
    lo = pltpu.bitcast(v[:, half:].astype(BF16).astype(F32), jnp.uint32)
    return hi | (lo >> 16)


def _unpack_halves(words):
    return pltpu.bitcast(words & jnp.uint32(0xFFFF0000), F32), pltpu.bitcast(words << 16, F32)


def _postmix_kernel(x_ref, oret_ref, onsa_ref, kv_ref, wout_ref, wq_ref, wo_ref,
                    g1_ref, b1_ref, g2_ref, b2_ref, x2_ref, x2p_ref):
    mixed = jnp.concatenate([oret_ref[...], onsa_ref[...]], axis=1)
    x1 = _layer_norm(DN_ALPHA * x_ref[...] + _dot(mixed, wout_ref[...]), g1_ref[...], b1_ref[...])
    q = (_dot(x1.astype(BF16), wq_ref[...]) * (XATT_DIM ** -0.5)).astype(BF16)
    heads = []
    for h in range(XATT_HEADS):
        cols = slice(h * XATT_DIM, (h + 1) * XATT_DIM)
        s = _dot_nt(q[:, cols], kv_ref[:, cols])
        m = jnp.max(s, axis=-1, keepdims=True)
        p = jnp.exp(s - m)
        l = jnp.sum(p, axis=-1, keepdims=True)
        heads.append(_dot(p.astype(BF16), kv_ref[:, D_MODEL + h * XATT_DIM:D_MODEL + (h + 1) * XATT_DIM]) / l)
    att = jnp.concatenate(heads, axis=1).astype(BF16)
    x2 = _layer_norm(DN_ALPHA * x1 + _dot(att, wo_ref[...]), g2_ref[...], b2_ref[...])
    x2_ref[...] = x2
    x2p_ref[...] = _pack_halves(x2)


def _postmix(x2d, o_ret, o_nsa, kvx, w_out, w_xq, w_xo, ln1_g, ln1_b, ln2_g, ln2_b, batch0, batch, seq):
    n = batch * seq
    tm = 512 if seq % 512 == 0 else seq
    per_b = seq // tm
    row = lambda w: pl.BlockSpec((tm, w), lambda b, i: (b * per_b + i, 0))
    full = lambda a: pl.BlockSpec(a.shape, lambda b, i: (0,) * a.ndim)
    ws = [w_out.astype(BF16), w_xq.astype(BF16), w_xo.astype(BF16)]
    vecs = [v.reshape(1, D_MODEL) for v in (ln1_g, ln1_b, ln2_g, ln2_b)]
    return pl.pallas_call(
        _postmix_kernel,
        grid=(batch, per_b),
        in_specs=[pl.BlockSpec((tm, D_MODEL), lambda b, i: ((batch0 + b) * per_b + i, 0)),
                  row(RET_WIDTH), row(NSA_WIDTH),
                  pl.BlockSpec((MEM_LEN, 2 * D_MODEL), lambda b, i: (batch0 + b, 0))]
                 + [full(w) for w in ws] + [full(v) for v in vecs],
        out_specs=[row(D_MODEL),
                   row(D_MODEL // 2)],
        out_shape=[jax.ShapeDtypeStruct((n, D_MODEL), F32),
                   jax.ShapeDtypeStruct((n, D_MODEL // 2), jnp.uint32)],
        compiler_params=_params(2),
        name="postmix",
    )(x2d, o_ret, o_nsa, kvx, *ws, *vecs)


def _router_kernel(x_ref, wr_ref, bias_ref, e_ref, rank_ref, w_ref, cnt_ref, cntrow_ref, carry_ref, carryrow_ref):
    tn = x_ref.shape[0]
    per = N_EXPERTS // N_GROUPS

    @pl.when(pl.program_id(0) == 0)
    def _():
        carry_ref[...] = jnp.zeros_like(carry_ref)
        carryrow_ref[...] = jnp.zeros_like(carryrow_ref)

    logits = _dot_nt(wr_ref[...], x_ref[...].astype(BF16))
    scores = jax.nn.sigmoid(logits)
    biased = scores + bias_ref[...]
    b3 = biased.reshape(N_GROUPS, per, tn)
    member = lax.broadcasted_iota(jnp.int32, (N_GROUPS, per, tn), 1)
    top1 = jnp.max(b3, axis=1, keepdims=True)
    first1 = jnp.min(jnp.where(b3 == top1, member, per), axis=1, keepdims=True)
    top2 = jnp.max(jnp.where(member == first1, -jnp.inf, b3), axis=1, keepdims=True)
    gscore = top1 + top2
    gidx = lax.broadcasted_iota(jnp.int32, (N_GROUPS, 1, tn), 0)
    gwork = gscore
    for _ in range(TOPK_GROUPS - 1):
        gbest = jnp.max(gwork, axis=0, keepdims=True)
        gfirst = jnp.min(jnp.where(gwork == gbest, gidx, N_GROUPS), axis=0, keepdims=True)
        gwork = jnp.where(gidx == gfirst, -jnp.inf, gwork)
    kth = jnp.max(gwork, axis=0, keepdims=True)
    work = jnp.where(gscore >= kth, b3, NEG).reshape(N_EXPERTS, tn)
    eidx = lax.broadcasted_iota(jnp.int32, (N_EXPERTS, tn), 0)
    picks = []
    chosen = jnp.zeros((N_EXPERTS, tn), F32)
    for _ in range(TOP_K):
        best = jnp.max(work, axis=0, keepdims=True)
        first = jnp.min(jnp.where(work == best, eidx, N_EXPERTS), axis=0, keepdims=True)
        hit = eidx == first
        picks.append((first, hit))
        chosen = jnp.where(hit, 1.0, chosen)
        work = jnp.where(hit, -jnp.inf, work)

    r_i = lax.broadcasted_iota(jnp.int32, (tn, tn), 0)
    c_i = lax.broadcasted_iota(jnp.int32, (tn, tn), 1)
    before = jnp.where(r_i < c_i, 1.0, 0.0).astype(BF16)
    chosen_b = chosen.astype(BF16)
    rank = _dot(chosen_b, before) + carry_ref[...]
    carry_ref[...] = carry_ref[...] + jnp.sum(chosen, axis=1, keepdims=True)
    carryrow_ref[...] = carryrow_ref[...] + _dot_nt(jnp.ones((8, tn), BF16), chosen_b)
    cnt_ref[...] = carry_ref[...]
    cntrow_ref[...] = carryrow_ref[...]

    wsel = [jnp.sum(jnp.where(hit, scores, 0.0), axis=0, keepdims=True) for _, hit in picks]
    wsum = wsel[0]
    for v in wsel[1:]:
        wsum = wsum + v
    for kk, (first, hit) in enumerate(picks):
        e_ref[kk:kk + 1, :] = first
        rank_ref[kk:kk + 1, :] = jnp.sum(jnp.where(hit, rank, 0.0), axis=0, keepdims=True).astype(jnp.int32)
        w_ref[kk:kk + 1, :] = wsel[kk] / wsum * ROUTED_SCALE


def _router(x2, w_router, router_bias):
    n = x2.shape[0]
    tn = 512 if n % 512 == 0 else n
    wr_t = w_router.T.astype(BF16)
    bias = router_bias.reshape(N_EXPERTS, 1).astype(F32)
    kspec = pl.BlockSpec((TOP_K, tn), lambda i: (0, i))
    return pl.pallas_call(
        _router_kernel,
        grid=(n // tn,),
        in_specs=[pl.BlockSpec((tn, D_MODEL), lambda i: (i, 0)),
                  pl.BlockSpec(wr_t.shape, lambda i: (0, 0)),
                  pl.BlockSpec(bias.shape, lambda i: (0, 0))],
        out_specs=[kspec, kspec, kspec, pl.BlockSpec((N_EXPERTS, 1), lambda i: (0, 0)),
                   pl.BlockSpec((8, N_EXPERTS), lambda i: (0, 0))],
        out_shape=[jax.ShapeDtypeStruct((TOP_K, n), jnp.int32),
                   jax.ShapeDtypeStruct((TOP_K, n), jnp.int32),
                   jax.ShapeDtypeStruct((TOP_K, n), F32),
                   jax.ShapeDtypeStruct((N_EXPERTS, 1), F32),
                   jax.ShapeDtypeStruct((8, N_EXPERTS), F32)],
        scratch_shapes=[pltpu.VMEM((N_EXPERTS, 1), F32), pltpu.VMEM((8, N_EXPERTS), F32)],
        compiler_params=_params(1),
        name="router",
    )(x2, wr_t, bias)


def _slots_kernel(e_ref, rank_ref, cnt_ref, cntrow_ref, dest_ref, blk_e_ref, valid_ref, *, blk, n_blocks):
    pad = lambda c: jnp.ceil(c / blk) * blk
    cnt = cnt_ref[...]
    padded = pad(cnt)
    padded_row = pad(cntrow_ref[0:1, :])
    r_i = lax.broadcasted_iota(jnp.int32, (N_EXPERTS, N_EXPERTS), 0)
    c_i = lax.broadcasted_iota(jnp.int32, (N_EXPERTS, N_EXPERTS), 1)
    start = jnp.sum(jnp.where(c_i < r_i, padded_row, 0.0), axis=1, keepdims=True)
    end = start + padded
    e = e_ref[...]
    dest = rank_ref[...]
    for ex in range(N_EXPERTS):
        dest = dest + jnp.where(e == ex, start[ex:ex + 1, :].astype(jnp.int32), 0)
    dest_ref[...] = dest
    bstart = (lax.broadcasted_iota(jnp.int32, (1, n_blocks), 1) * blk).astype(F32)
    owner = jnp.sum(jnp.where(end <= bstart, 1.0, 0.0), axis=0, keepdims=True)
    blk_e_ref[...] = jnp.minimum(owner, N_EXPERTS - 1.0).astype(jnp.int32)
    inside = (start <= bstart) & (bstart < end)
    real = jnp.clip(start + cnt - bstart, 0.0, float(blk))
    valid_ref[...] = jnp.sum(jnp.where(inside, real, 0.0), axis=0, keepdims=True).astype(jnp.int32)


def _slots(e_k, rank_k, counts, counts_row, blk, n_blocks):
    n = e_k.shape[1]
    full = lambda shape: pl.BlockSpec(shape, lambda: (0,) * len(shape))
    return pl.pallas_call(
        functools.partial(_slots_kernel, blk=blk, n_blocks=n_blocks),
        in_specs=[full((TOP_K, n)), full((TOP_K, n)), full((N_EXPERTS, 1)), full((8, N_EXPERTS))],
        out_specs=[full((TOP_K, n)), full((1, n_blocks)), full((1, n_blocks))],
        out_shape=[jax.ShapeDtypeStruct((TOP_K, n), jnp.int32),
                   jax.ShapeDtypeStruct((1, n_blocks), jnp.int32),
                   jax.ShapeDtypeStruct((1, n_blocks), jnp.int32)],
        compiler_params=pltpu.CompilerParams(vmem_limit_bytes=VMEM_LIMIT),
        name="slots",
    )(e_k, rank_k, counts, counts_row)


def _sc_worker_base(per_worker):
    return (lax.axis_index("s") * SC_CORES + lax.axis_index("c")) * per_worker


def _sc_scatter_rows(rows, idx, n_out):
    n, width = rows.shape
    k_lists = idx.shape[0] // n
    workers = SC_CORES * SC_SUBCORES
    per_worker = n // workers
    assert per_worker * workers == n and per_worker % SC_CHUNK == 0
    mesh = plsc.VectorSubcoreMesh(core_axis_name="c", subcore_axis_name="s")

    @functools.partial(
        pl.kernel, mesh=mesh,
        out_type=jax.ShapeDtypeStruct((n_out, width), rows.dtype),
        scratch_types=[pltpu.VMEM((SC_CHUNK, width), rows.dtype)]
                      + [pltpu.VMEM((SC_CHUNK,), jnp.int32)] * k_lists + [pltpu.SemaphoreType.DMA] * 3,
        name="sc_scatter")
    def scatter(rows_hbm, idx_hbm, out_hbm, rows_v, *rest):
        idx_vs = rest[:k_lists]
        sem_rows, sem_idx, sem_out = rest[k_lists:]
        base = _sc_worker_base(per_worker)

        @pl.loop(0, per_worker // SC_CHUNK)
        def _(ci):
            off = pl.multiple_of(base + ci * SC_CHUNK, SC_CHUNK)
            loads = [pltpu.async_copy(rows_hbm.at[pl.ds(off, SC_CHUNK)], rows_v, sem_rows)]
            loads += [pltpu.async_copy(idx_hbm.at[pl.ds(pl.multiple_of(k * n + off, SC_CHUNK), SC_CHUNK)],
                                       idx_vs[k], sem_idx) for k in range(k_lists)]
            for c in loads:
                c.wait()
            copies = [pltpu.async_copy(rows_v, out_hbm.at[idx_vs[k]], sem_out) for k in range(k_lists)]
            for c in copies:
                c.wait()

    return scatter(rows, idx)


def _experts_kernel(blk_e_ref, valid_ref, xs_ref, wg_ref, wu_ref, wd_ref, y_ref, wg_b, wu_b, wd_b):
    i = pl.program_id(0)
    valid = valid_ref[i]

    @pl.when((i == 0) | (blk_e_ref[i] != blk_e_ref[jnp.maximum(i - 1, 0)]))
    def _():
        wg_b[...] = wg_ref[...].astype(BF16)
        wu_b[...] = wu_ref[...].astype(BF16)
        wd_b[...] = wd_ref[...].astype(BF16)

    @pl.when(valid > 0)
    def _():
        half = D_MODEL // 2
        row = lax.broadcasted_iota(jnp.int32, (xs_ref.shape[0], 1), 0)
        hi, lo = (v.astype(BF16) for v in _unpack_halves(jnp.where(row < valid, xs_ref[...], jnp.uint32(0))))
        gate = _dot(hi, wg_b[:half, :]) + _dot(lo, wg_b[half:, :])
        up = _dot(hi, wu_b[:half, :]) + _dot(lo, wu_b[half:, :])
        y_ref[...] = _pack_halves(_dot((jax.nn.silu(gate) * up).astype(BF16), wd_b[...]))

    @pl.when(valid <= 0)
    def _():
        y_ref[...] = jnp.zeros_like(y_ref)


def _experts(blk_e, valid, xs, w_gate, w_up, w_down, blk):
    cap, width = xs.shape
    wspec = lambda a: pl.BlockSpec((None,) + a.shape[1:], lambda i, be, nv: (be[i], 0, 0))
    rows = pl.BlockSpec((blk, width), lambda i, be, nv: (i, 0))
    return pl.pallas_call(
        _experts_kernel,
        grid_spec=pltpu.PrefetchScalarGridSpec(
            num_scalar_prefetch=2,
            grid=(cap // blk,),
            in_specs=[rows, wspec(w_gate), wspec(w_up), wspec(w_down)],
            out_specs=rows,
            scratch_shapes=[pltpu.VMEM(w.shape[1:], BF16) for w in (w_gate, w_up, w_down)],
        ),
        out_shape=jax.ShapeDtypeStruct(xs.shape, xs.dtype),
        compiler_params=_params(1),
        name="experts",
    )(blk_e, valid, xs, w_gate, w_up, w_down)


def _sc_gather_rows(table, idx):
    b, width = idx.shape[0], table.shape[1]
    workers = SC_CORES * SC_SUBCORES
    per_worker = b // workers
    assert per_worker * workers == b and per_worker % (SC_CHUNK * SC_INFLIGHT) == 0
    mesh = plsc.VectorSubcoreMesh(core_axis_name="c", subcore_axis_name="s")

    @functools.partial(
        pl.kernel, mesh=mesh,
        out_type=jax.ShapeDtypeStruct((b, width), table.dtype),
        scratch_types=[pltpu.VMEM((SC_CHUNK,), jnp.int32)] * SC_INFLIGHT
                      + [pltpu.VMEM((SC_CHUNK, width), table.dtype)] * SC_INFLIGHT
                      + [pltpu.SemaphoreType.DMA] * (1 + 2 * SC_INFLIGHT),
        name="sc_gather")
    def gather(table_hbm, idx_hbm, out_hbm, *scratch):
        idx_vs = scratch[:SC_INFLIGHT]
        rows_vs = scratch[SC_INFLIGHT:2 * SC_INFLIGHT]
        sem_idx = scratch[2 * SC_INFLIGHT]
        sem_rows = scratch[2 * SC_INFLIGHT + 1:3 * SC_INFLIGHT + 1]
        sem_out = scratch[3 * SC_INFLIGHT + 1:]
        base = _sc_worker_base(per_worker)
        lanes = range(SC_INFLIGHT)

        @pl.loop(0, per_worker // (SC_CHUNK * SC_INFLIGHT))
        def _(gi):
            offs = [pl.multiple_of(base + (gi * SC_INFLIGHT + j) * SC_CHUNK, SC_CHUNK) for j in lanes]
            loads = [pltpu.async_copy(idx_hbm.at[pl.ds(offs[j], SC_CHUNK)], idx_vs[j], sem_idx) for j in lanes]
            for c in loads:
                c.wait()
            gathers = [pltpu.async_copy(table_hbm.at[idx_vs[j]], rows_vs[j], sem_rows[j]) for j in lanes]
            writes = []
            for j in lanes:
                gathers[j].wait()
                writes.append(pltpu.async_copy(rows_vs[j], out_hbm.at[pl.ds(offs[j], SC_CHUNK)], sem_out[j]))
            for c in writes:
                c.wait()

    return gather(table, idx)


def _combine_kernel(x_ref, wk_ref, yk_ref, wsg_ref, wsu_ref, wsd_ref, g_ref, b_ref, *rest):
    o_ref = rest[-1]
    x = x_ref[...]
    xb = x.astype(BF16)
    shared = _dot((jax.nn.silu(_dot(xb, wsg_ref[...])) * _dot(xb, wsu_ref[...])).astype(BF16), wsd_ref[...])
    wk = wk_ref[...]
    routed_hi = routed_lo = None
    for kk in range(TOP_K):
        hi, lo = _unpack_halves(yk_ref[kk])
        w = wk[:, kk:kk + 1]
        routed_hi = hi * w if kk == 0 else routed_hi + hi * w
        routed_lo = lo * w if kk == 0 else routed_lo + lo * w
    routed = jnp.concatenate([routed_hi, routed_lo], axis=1)
    o_ref[...] = _layer_norm(DN_ALPHA * x + (routed + shared), g_ref[...], b_ref[...])


def _combine(x2, w_tok, yk, ws_gate, ws_up, ws_down, ln3_g, ln3_b, row0, n_total, out_prev):
    n = x2.shape[0]
    tt = 512 if n % 512 == 0 and row0 % 512 == 0 else n
    blk0 = row0 // tt
    ws = [ws_gate.astype(BF16), ws_up.astype(BF16), ws_down.astype(BF16)]
    vecs = [ln3_g.reshape(1, D_MODEL), ln3_b.reshape(1, D_MODEL)]
    full = lambda a: pl.BlockSpec(a.shape, lambda i: (0,) * a.ndim)
    args = [x2, w_tok, yk, *ws, *vecs]
    in_specs = ([pl.BlockSpec((tt, D_MODEL), lambda i: (i, 0)),
                 pl.BlockSpec((tt, TOP_K), lambda i: (i, 0)),
                 pl.BlockSpec((TOP_K, tt, D_MODEL // 2), lambda i: (0, i, 0))]
                + [full(a) for a in ws] + [full(v) for v in vecs])
    aliases = {}
    if out_prev is not None:
        aliases = {len(args): 0}
        args.append(out_prev)
        in_specs.append(pl.BlockSpec(memory_space=pl.ANY))
    return pl.pallas_call(
        _combine_kernel,
        grid=(n // tt,),
        in_specs=in_specs,
        out_specs=pl.BlockSpec((tt, D_MODEL), lambda i: (blk0 + i, 0)),
        out_shape=jax.ShapeDtypeStruct((n_total, D_MODEL), F32),
        input_output_aliases=aliases,
        compiler_params=_params(1),
        name="combine",
    )(*args)


def _moe_and_norm(x2, x2p, w_router, router_bias, w_gate, w_up, w_down, ws_gate, ws_up, ws_down,
                  ln3_g, ln3_b, row0, n_total, out_prev):
    n = x2.shape[0]
    blk = 512
    cap = n * TOP_K + N_EXPERTS * blk
    n_blocks = cap // blk
    e_k, rank_k, w_k, counts, counts_row = _router(x2, w_router, router_bias)
    dest, blk_e, valid = _slots(e_k, rank_k, counts, counts_row, blk, n_blocks)
    dest = dest.reshape(-1)
    xs = _sc_scatter_rows(x2p, dest, cap)
    y = _experts(blk_e.reshape(-1), valid.reshape(-1), xs, w_gate, w_up, w_down, blk)
    yk = _sc_gather_rows(y, dest).reshape(TOP_K, n, D_MODEL // 2)
    return _combine(x2, w_k.T, yk, ws_gate, ws_up, ws_down, ln3_g, ln3_b, row0, n_total, out_prev), y


def _layer(x, mem, positions, w_in, cmp_pe_k, cmp_pe_v, cmp_w1_k, cmp_w2_k, cmp_w1_v, cmp_w2_v,
           w_out, ln1_g, ln1_b, w_xq, w_xkv, w_xo, ln2_g, ln2_b, w_router, router_bias,
           w_gate, w_up, w_down, ws_gate, ws_up, ws_down, ln3_g, ln3_b):
    batch, seq, _ = x.shape
    n_total = batch * seq
    x2d = x.reshape(n_total, D_MODEL)
    pos_col = positions.astype(F32).reshape(n_total, 1)
    kvx = _memkv(mem.reshape(batch * MEM_LEN, D_MODEL), w_xkv)
    groups = BATCH_GROUPS if batch % BATCH_GROUPS == 0 else 1
    per = batch // groups
    n = per * seq
    out = y_prev = None
    for gi in range(groups):
        (rq, rk, rv, rg, nq, nqr, kc, vc, ks, vs, kw, vw, gates) = _inproj(x2d, pos_col, w_in, gi * n, n)
        o_ret = _retention(rq, rk, rv, rg, per, seq)
        kcmp = _compress(kc, cmp_pe_k, cmp_w1_k, cmp_w2_k, per, seq)
        vcmp = _compress(vc, cmp_pe_v, cmp_w1_v, cmp_w2_v, per, seq)
        o_nsa = _nsa(nq, nqr, gates, kcmp, vcmp, ks, vs, kw, vw, per, seq, after=y_prev)
        x2, x2p = _postmix(x2d, o_ret, o_nsa, kvx, w_out, w_xq, w_xo, ln1_g, ln1_b, ln2_g, ln2_b,
                           gi * per, per, seq)
        out, y_prev = _moe_and_norm(x2, x2p, w_router, router_bias, w_gate, w_up, w_down,
                                    ws_gate, ws_up, ws_down, ln3_g, ln3_b, gi * n, n_total, out)
    return out.reshape(batch, seq, D_MODEL)


def kernel(x, mem, positions, w_in, cmp_pe_k, cmp_pe_v, cmp_w1_k, cmp_w2_k, cmp_w1_v, cmp_w2_v, w_out, ln1_g, ln1_b, w_xq, w_xkv, w_xo, ln2_g, ln2_b, w_router, router_bias, w_gate, w_up, w_down, ws_gate, ws_up, ws_down, ln3_g, ln3_b):
    for l in range(DEPTH):
        x = _layer(x, mem, positions, w_in[l], cmp_pe_k[l], cmp_pe_v[l], cmp_w1_k[l], cmp_w2_k[l],
                   cmp_w1_v[l], cmp_w2_v[l], w_out[l], ln1_g[l], ln1_b[l], w_xq[l], w_xkv[l],
                   w_xo[l], ln2_g[l], ln2_b[l], w_router[l], router_bias[l], w_gate[l], w_up[l],
                   w_down[l], ws_gate[l], ws_up[l], ws_down[l], ln3_g[l], ln3_b[l])
    return x
```

```python
import functools

import numpy as np
import jax
import jax.numpy as jnp
from jax import lax
from jax.experimental import pallas as pl
from jax.experimental.pallas import tpu as pltpu
from jax.experimental.pallas import tpu_sc as plsc

D_MODEL = 1024
MEM_LEN = 256
DEPTH = 1
DN_ALPHA = (2 * DEPTH) ** 0.25
LN_EPS = 1e-5
NEG = -1e30
FORCE = 1e9

RET_HEADS = 4
RET_DIM = 128
RET_CHUNK = 128
RET_ROPE_BASE = 10000.0
RET_STEP_CHUNKS = 4
RET_WIDTH = RET_HEADS * RET_DIM

NSA_HEADS = 8
NSA_KV_GROUPS = 2
NSA_HPG = NSA_HEADS // NSA_KV_GROUPS
NSA_DIM = 64
NSA_WIDTH = NSA_HEADS * NSA_DIM
KV_WIDTH = NSA_KV_GROUPS * NSA_DIM
CMP_LEN = 32
CMP_STRIDE = 16
CMP_HIDDEN = 256
SEL_LEN = 64
SEL_SHIFT = 6
SEL_TOPK = 16
WIN = 512
ROPE_THETA = 500000.0
ROPE_DIMS = NSA_DIM // 4
GATE_LANES = 16
NSA_CHAINS = 1
SUM_ROWS = 16
WIN_PART = 256

BATCH_GROUPS = 2

SC_CORES = 2
SC_SUBCORES = 16
SC_CHUNK = 64
SC_INFLIGHT = 2

XATT_HEADS = 4
XATT_DIM = D_MODEL // XATT_HEADS

N_EXPERTS = 64
TOP_K = 8
N_GROUPS = 8
TOPK_GROUPS = 4
EXPERT_FF = 256
SHARED_FF = 256
ROUTED_SCALE = 2.5

LANES = 128
VMEM_LIMIT = 56 * 1024 * 1024

F32 = jnp.float32
BF16 = jnp.bfloat16
NT_DIMS = (((1,), (1,)), ((), ()))


def _params(n_axes):
    return pltpu.CompilerParams(dimension_semantics=("arbitrary",) * n_axes,
                                vmem_limit_bytes=VMEM_LIMIT)


def _dot(a, b):
    return jnp.dot(a, b, preferred_element_type=F32)


def _dot_nt(a, b):
    return lax.dot_general(a, b, NT_DIMS, preferred_element_type=F32)


def _layer_norm(v, g, b):
    mu = jnp.mean(v, axis=-1, keepdims=True)
    d = v - mu
    var = jnp.mean(d * d, axis=-1, keepdims=True)
    return d * lax.rsqrt(var + LN_EPS) * g + b


def _inproj_kernel(x_ref, pos_ref, wret_ref, wnq_ref, wkv_ref, wg_ref, invr_ref, invn_ref,
                   rq_ref, rk_ref, rv_ref, rg_ref, nq_ref, nqr_ref, kc_ref, vc_ref,
                   ks_ref, vs_ref, kw_ref, vw_ref, gate_ref):
    xb = x_ref[...].astype(BF16)
    pos = pos_ref[...]
    lane = lax.broadcasted_iota(jnp.int32, (1, LANES), 1)

    ang = pos * invr_ref[...]
    cos_r = jnp.cos(ang)
    sin_r = jnp.sin(ang)
    sin_r = jnp.where(lane < RET_DIM // 2, -sin_r, sin_r)
    q_all = _dot(xb, wret_ref[:, :RET_WIDTH])
    k_all = _dot(xb, wret_ref[:, RET_WIDTH:2 * RET_WIDTH])
    for h in range(RET_HEADS):
        cols = slice(h * RET_DIM, (h + 1) * RET_DIM)
        q = q_all[:, cols]
        rq_ref[:, cols] = (q * cos_r + pltpu.roll(q, RET_DIM // 2, 1) * sin_r).astype(BF16)
        k = k_all[:, cols]
        k = (k * cos_r + pltpu.roll(k, RET_DIM // 2, 1) * sin_r) * (RET_DIM ** -0.5)
        rk_ref[:, cols] = k.astype(BF16)
    rv_ref[...] = _dot(xb, wret_ref[:, 2 * RET_WIDTH:3 * RET_WIDTH]).astype(BF16)
    rg_ref[...] = _dot(xb, wret_ref[:, 3 * RET_WIDTH:4 * RET_WIDTH]).astype(BF16)

    half = ROPE_DIMS // 2
    j = lane % NSA_DIM
    angn = pos * invn_ref[...]
    cos_n = jnp.cos(angn)
    sin_n = jnp.sin(angn)
    sin_lo = jnp.where(j < half, -sin_n, 0.0)
    sin_hi = jnp.where((j >= half) & (j < 2 * half), sin_n, 0.0)

    def rope_n(v):
        return v * cos_n + pltpu.roll(v, half, 1) * sin_hi + pltpu.roll(v, LANES - half, 1) * sin_lo

    scale = NSA_DIM ** -0.5
    nq_all = _dot(xb, wnq_ref[...])
    for c in range(NSA_WIDTH // LANES):
        cols = slice(c * LANES, (c + 1) * LANES)
        q = nq_all[:, cols]
        nq_ref[:, cols] = (q * scale).astype(BF16)
        nqr_ref[:, cols] = (rope_n(q) * scale).astype(BF16)

    kv_all = _dot(xb, wkv_ref[...])

    def kv(i):
        return kv_all[:, i * KV_WIDTH:(i + 1) * KV_WIDTH]

    def split_groups(ref, v):
        for g in range(NSA_KV_GROUPS):
            ref[g] = v[:, g * NSA_DIM:(g + 1) * NSA_DIM].astype(BF16)

    kc_ref[...] = kv(0)
    vc_ref[...] = kv(1)
    split_groups(ks_ref, rope_n(kv(2)))
    split_groups(vs_ref, kv(3))
    split_groups(kw_ref, rope_n(kv(4)))
    split_groups(vw_ref, kv(5))

    gt = jax.nn.sigmoid(_dot_nt(wg_ref[...], xb))
    for g in range(NSA_KV_GROUPS):
        gate_ref[g] = gt[g * GATE_LANES:(g + 1) * GATE_LANES, :]


def _inproj(x2d, pos_col, w_in, row0, n):
    tm = 512 if n % 512 == 0 and row0 % 512 == 0 else n
    blk0 = row0 // tm
    off = np.cumsum([0] + [RET_WIDTH] * 4 + [NSA_WIDTH] + [KV_WIDTH] * 6)
    w_ret = w_in[:, :off[4]].astype(BF16)
    w_nq = w_in[:, off[4]:off[5]].astype(BF16)
    w_kv = w_in[:, off[5]:off[11]].astype(BF16)
    wg = w_in[:, off[11]:].reshape(D_MODEL, NSA_KV_GROUPS, NSA_HPG * 3)
    wg = jnp.pad(wg, ((0, 0), (0, 0), (0, GATE_LANES - NSA_HPG * 3)))
    wg = wg.reshape(D_MODEL, NSA_KV_GROUPS * GATE_LANES).T.astype(BF16)

    lane = np.arange(LANES)
    half_r = RET_DIM // 2
    inv_r = (np.float32(RET_ROPE_BASE) ** (-np.arange(half_r, dtype=np.float32) / np.float32(half_r)))
    inv_r = inv_r.astype(np.float32)[lane % half_r][None, :]
    half_n = ROPE_DIMS // 2
    inv_n = (np.float32(ROPE_THETA) ** (-np.arange(half_n, dtype=np.float32) / np.float32(half_n)))
    jn = lane % NSA_DIM
    inv_n = np.where(jn < ROPE_DIMS, inv_n.astype(np.float32)[jn % half_n], np.float32(0.0))[None, :]

    row = lambda w: pl.BlockSpec((tm, w), lambda i: (i, 0))
    src_row = lambda w: pl.BlockSpec((tm, w), lambda i: (blk0 + i, 0))
    full = lambda a: pl.BlockSpec(a.shape, lambda i: (0,) * a.ndim)
    grp = lambda w: pl.BlockSpec((NSA_KV_GROUPS, tm, w), lambda i: (0, i, 0))
    bf = lambda w: jax.ShapeDtypeStruct((n, w), BF16)
    gbf = jax.ShapeDtypeStruct((NSA_KV_GROUPS, n, NSA_DIM), BF16)
    inv_r = jnp.asarray(inv_r, F32)
    inv_n = jnp.asarray(inv_n, F32)
    return pl.pallas_call(
        _inproj_kernel,
        grid=(n // tm,),
        in_specs=[src_row(D_MODEL), src_row(1), full(w_ret), full(w_nq), full(w_kv), full(wg),
                  full(inv_r), full(inv_n)],
        out_specs=[row(RET_WIDTH)] * 4 + [row(NSA_WIDTH)] * 2 + [row(KV_WIDTH)] * 2
                  + [grp(NSA_DIM)] * 4
                  + [pl.BlockSpec((NSA_KV_GROUPS, GATE_LANES, tm), lambda i: (0, 0, i))],
        out_shape=[bf(RET_WIDTH)] * 4 + [bf(NSA_WIDTH)] * 2
                  + [jax.ShapeDtypeStruct((n, KV_WIDTH), F32)] * 2 + [gbf] * 4
                  + [jax.ShapeDtypeStruct((NSA_KV_GROUPS, GATE_LANES, n), F32)],
        compiler_params=_params(1),
        name="inproj",
    )(x2d, pos_col, w_ret, w_nq, w_kv, wg, inv_r, inv_n)


def _retention_kernel(q_ref, k_ref, v_ref, g_ref, o_ref, state_ref):
    c = RET_CHUNK

    @pl.when(pl.program_id(1) == 0)
    def _():
        state_ref[...] = jnp.zeros_like(state_ref)

    row = lax.broadcasted_iota(jnp.int32, (c, c), 0)
    col = lax.broadcasted_iota(jnp.int32, (c, c), 1)
    rel = (row - col).astype(F32)
    idx = lax.broadcasted_iota(jnp.int32, (c, 1), 0).astype(F32)
    for h in range(RET_HEADS):
        log_g = float(np.log(np.float32(1.0) - np.float32(2.0) ** np.float32(-5.0 - h)))
        cols = slice(h * RET_DIM, (h + 1) * RET_DIM)
        dmask = jnp.where(rel >= 0, jnp.exp(log_g * jnp.maximum(rel, 0.0)), 0.0)
        zeta = jnp.exp(log_g * (c - 1.0 - idx))
        xi = jnp.exp(log_g * (idx + 1.0))
        for j in range(q_ref.shape[0] // c):
            rows = slice(j * c, (j + 1) * c)
            q = q_ref[rows, cols]
            k = k_ref[rows, cols]
            v = v_ref[rows, cols]
            scores = _dot_nt(q, k) * dmask
            inner = _dot(scores.astype(BF16), v)
            prev = state_ref[h]
            cross = _dot(q, prev.astype(BF16)) * xi
            kz = (k.astype(F32) * zeta).astype(BF16)
            kv = lax.dot_general(kz, v, (((0,), (0,)), ((), ())), preferred_element_type=F32)
            state_ref[h] = prev * float(np.exp(np.float32(log_g) * np.float32(c))) + kv
            o = inner + cross
            mu = jnp.mean(o, axis=-1, keepdims=True)
            d = o - mu
            var = jnp.mean(d * d, axis=-1, keepdims=True)
            o = d * lax.rsqrt(var + LN_EPS)
            o_ref[rows, cols] = (jax.nn.silu(g_ref[rows, cols].astype(F32)) * o).astype(BF16)


def _retention(rq, rk, rv, rg, batch, seq):
    per_step = RET_STEP_CHUNKS if (seq // RET_CHUNK) % RET_STEP_CHUNKS == 0 else 1
    nc = seq // (RET_CHUNK * per_step)
    spec = pl.BlockSpec((RET_CHUNK * per_step, RET_WIDTH), lambda b, n: (b * nc + n, 0))
    return pl.pallas_call(
        _retention_kernel,
        grid=(batch, nc),
        in_specs=[spec] * 4,
        out_specs=spec,
        out_shape=jax.ShapeDtypeStruct(rq.shape, BF16),
        scratch_shapes=[pltpu.VMEM((RET_HEADS, RET_DIM, RET_DIM), F32)],
        compiler_params=_params(2),
        name="retention",
    )(rq, rk, rv, rg)


def _compress_kernel(a_ref, pe_ref, w1_ref, w2_ref, o_ref, shift_ref, *, n_cmp):
    rows = a_ref.shape[0]
    a = a_ref[...]
    lo = (a + pe_ref[0]).astype(BF16)
    hi = (a + pe_ref[1]).astype(BF16)
    ridx = lax.broadcasted_iota(jnp.int32, (rows, 1), 0)
    shift_ref[rows:rows + 8, :] = jnp.zeros((8, CMP_HIDDEN), F32)
    for g in range(NSA_KV_GROUPS):
        p = _dot(lo, w1_ref[0, g])
        shift_ref[0:rows, :] = _dot(hi, w1_ref[1, g])
        hid = jax.nn.silu(p + shift_ref[pl.ds(1, rows), :])
        out = _dot(hid.astype(BF16), w2_ref[...])
        o_ref[g] = jnp.where(ridx < n_cmp, out, 0.0).astype(BF16)


def _compress(a, pe, w1, w2, batch, seq):
    rows = seq // CMP_STRIDE
    per = CMP_STRIDE * KV_WIDTH
    n_cmp = (seq - CMP_LEN) // CMP_STRIDE + 1
    a2 = a.reshape(batch * rows, per)
    pe2 = jnp.tile(pe.reshape(2, CMP_STRIDE, 1, NSA_DIM), (1, 1, NSA_KV_GROUPS, 1)).reshape(2, 1, per)
    w1r = w1.reshape(2, CMP_STRIDE, 1, NSA_DIM, CMP_HIDDEN)
    eye = jnp.eye(NSA_KV_GROUPS, dtype=w1.dtype).reshape(1, NSA_KV_GROUPS, 1, NSA_KV_GROUPS, 1, 1)
    w1x = (w1r[:, None] * eye).reshape(2, NSA_KV_GROUPS, per, CMP_HIDDEN).astype(BF16)
    w2b = w2.astype(BF16)
    full = lambda arr: pl.BlockSpec(arr.shape, lambda b: (0,) * arr.ndim)
    return pl.pallas_call(
        functools.partial(_compress_kernel, n_cmp=n_cmp),
        grid=(batch,),
        in_specs=[pl.BlockSpec((rows, per), lambda b: (b, 0)), full(pe2), full(w1x), full(w2b)],
        out_specs=pl.BlockSpec((None, NSA_KV_GROUPS, rows, NSA_DIM), lambda b: (b, 0, 0, 0)),
        out_shape=jax.ShapeDtypeStruct((batch, NSA_KV_GROUPS, rows, NSA_DIM), BF16),
        scratch_shapes=[pltpu.VMEM((rows + 8, CMP_HIDDEN), F32)],
        compiler_params=_params(1),
        name="compress",
    )(a2, pe2, w1x, w2b)


def _heads_to_lanes(ref):
    vt = ref[...].astype(F32).T
    return jnp.concatenate([vt[h * NSA_DIM:(h + 1) * NSA_DIM] for h in range(NSA_HPG)], axis=1).astype(BF16)


def _tile_heads(v):
    return jnp.concatenate([v] * NSA_HPG, axis=1)


def _transpose_into(dst_ref, src_ref, chunk):
    def step(c, _):
        c0 = pl.multiple_of(c * chunk, chunk)
        dst_ref[:NSA_DIM, pl.ds(c0, chunk)] = src_ref[pl.ds(c0, chunk), :].astype(F32).T.astype(BF16)
        return 0
    lax.fori_loop(0, src_ref.shape[0] // chunk, step, 0)


def _nsa_kernel(qraw_ref, qrot_ref, gate_ref, kcmp_ref, vcmp_ref, ovt_ref,
                ks_ref, vs_ref, kw_ref, vw_ref, o_ref, vst_ref, vwt_ref, vct_ref, bias_ref, *, tq, tk, seq):
    i = pl.program_id(2)
    t0 = i * tq
    cols = NSA_HPG * tq
    n_sel = seq // SEL_LEN
    n_cmp_rows = seq // CMP_STRIDE
    blocks_per_tile = tk // SEL_LEN

    @pl.when(i == 0)
    def _():
        chunk = min(512, n_cmp_rows)
        _transpose_into(vst_ref, vs_ref, chunk)
        _transpose_into(vwt_ref, vw_ref, chunk)
        _transpose_into(vct_ref, vcmp_ref, chunk)
        vst_ref[NSA_DIM:, :] = jnp.ones((SUM_ROWS, seq), BF16)
        vwt_ref[NSA_DIM:, :] = jnp.ones((SUM_ROWS, seq), BF16)

    def split_sum(acc):
        return acc[:NSA_DIM] / acc[NSA_DIM:NSA_DIM + 1]

    q_raw = _heads_to_lanes(qraw_ref)
    q_rot = _heads_to_lanes(qrot_ref)
    t_row = t0 + lax.broadcasted_iota(jnp.int32, (1, tq), 1)

    chain_w = cols // NSA_CHAINS
    heads_per_chain = chain_w // tq
    chains = [slice(c * chain_w, (c + 1) * chain_w) for c in range(NSA_CHAINS)]
    tile_chain = lambda v: jnp.concatenate([v] * heads_per_chain, axis=1)

    pw = min(tq, WIN_PART)
    parts = []
    for u in range(tq // pw):
        span = WIN + pw
        ws = pl.multiple_of(jnp.maximum(t0 + u * pw - WIN, 0), pw)
        dist = t_row[:, u * pw:(u + 1) * pw] - (ws + lax.broadcasted_iota(jnp.int32, (span, 1), 0))
        bias_w = jnp.concatenate([jnp.where((dist >= 0) & (dist < WIN), 0.0, NEG)] * NSA_HPG, axis=1)
        q_part = jnp.concatenate([q_rot[:, h * tq + u * pw:h * tq + (u + 1) * pw] for h in range(NSA_HPG)],
                                 axis=1)
        s_w = _dot(kw_ref[pl.ds(ws, span), :], q_part) + bias_w
        p_w = jnp.exp((s_w - jnp.max(s_w, axis=0, keepdims=True)).astype(BF16))
        parts.append(split_sum(_dot(vwt_ref[:, pl.ds(ws, span)], p_w)))
    o_w = jnp.concatenate([parts[u][:, h * pw:(h + 1) * pw]
                           for h in range(NSA_HPG) for u in range(tq // pw)], axis=1)

    c_idx = lax.broadcasted_iota(jnp.int32, (n_cmp_rows, 1), 0)
    valid = tile_chain(jnp.where(c_idx * CMP_STRIDE + (CMP_LEN - 1) <= t_row, 1.0, 0.0))
    bias_c = (valid - 1.0) * (-NEG)
    o_c = []
    p_sum = None
    for c in chains:
        s_c = _dot(kcmp_ref[...], q_raw[:, c]) + bias_c
        e_c = jnp.exp(s_c - jnp.max(s_c, axis=0, keepdims=True)) * valid
        l_c = jnp.sum(e_c, axis=0, keepdims=True)
        p_c = e_c / jnp.where(l_c > 0.0, l_c, 1.0)
        o_c.append(_dot(vct_ref[...], p_c.astype(BF16)))
        for h in range(heads_per_chain):
            p_h = p_c[:, h * tq:(h + 1) * tq]
            p_sum = p_h if p_sum is None else p_sum + p_h
    o_c = jnp.concatenate(o_c, axis=1)

    p_hi = p_sum.astype(BF16)
    p_lo = (p_sum - p_hi.astype(F32)).astype(BF16)
    ovt = ovt_ref[...]
    imp = _dot(ovt, p_hi) + _dot(ovt, p_lo)
    jb = lax.broadcasted_iota(jnp.int32, (n_sel, tq), 0)
    cur = (t0 + lax.broadcasted_iota(jnp.int32, (n_sel, tq), 1)) >> SEL_SHIFT
    forced = (jb == 0) | (jb == cur) | (jb == cur - 1)
    work = jnp.where(forced, FORCE, imp)
    work = jnp.where(jb <= cur, work, NEG)
    sel_t = jnp.zeros((n_sel, tq), F32)
    for _ in range(min(SEL_TOPK, n_sel)):
        best = jnp.max(work, axis=0, keepdims=True)
        first = jnp.min(jnp.where(work == best, jb, n_sel), axis=0, keepdims=True)
        hit = jb == first
        sel_t = jnp.where(hit, 1.0, sel_t)
        work = jnp.where(hit, -jnp.inf, work)
    bias_ref[...] = jnp.where(sel_t > 0.5, 0.0, NEG)

    def sel_tile(kt, carry, causal):
        k0 = pl.multiple_of(kt * tk, tk)
        bias = jnp.concatenate(
            [jnp.broadcast_to(bias_ref[pl.ds(kt * blocks_per_tile + j, 1), :], (SEL_LEN, tq))
             for j in range(blocks_per_tile)], axis=0)
        if causal:
            kpos = k0 + lax.broadcasted_iota(jnp.int32, (tk, 1), 0)
            bias = jnp.where(kpos <= t_row, bias, NEG)
        bias = tile_chain(bias)
        k_t = ks_ref[pl.ds(k0, tk), :]
        v_t = vst_ref[:, pl.ds(k0, tk)]
        out = []
        scores = [_dot(k_t, q_rot[:, c]) + bias for c in chains]
        for (m, acc), s in zip(carry, scores):
            m_new = jnp.maximum(m, jnp.max(s, axis=0, keepdims=True))
            p = jnp.exp((s - m_new).astype(BF16))
            acc = jnp.exp(m - m_new) * acc + _dot(v_t, p)
            out.append((m_new, acc))
        return tuple(out)

    n_full = t0 // tk
    init = tuple((jnp.full((1, chain_w), NEG, F32), jnp.zeros((NSA_DIM + SUM_ROWS, chain_w), F32))
                 for _ in chains)
    carry = lax.fori_loop(0, n_full, functools.partial(sel_tile, causal=False), init)
    for d in range(max(tq // tk, 1)):
        carry = sel_tile(n_full + d, carry, causal=True)
    o_s = jnp.concatenate([split_sum(acc) for _, acc in carry], axis=1)

    gt = gate_ref[...]
    outs = []
    for h in range(NSA_HPG):
        c = slice(h * tq, (h + 1) * tq)
        outs.append(gt[3 * h:3 * h + 1] * o_c[:, c] + gt[3 * h + 1:3 * h + 2] * o_s[:, c]
                    + gt[3 * h + 2:3 * h + 3] * o_w[:, c])
    o_ref[...] = jnp.concatenate(outs, axis=0).T.astype(BF16)


def _anchored(kernel_fn, n_inputs):
    def body(*refs, **static):
        kernel_fn(*refs[:n_inputs], *refs[n_inputs + 1:], **static)
    return body


def _nsa(nq, nqr, gates, kcmp, vcmp, ks, vs, kw, vw, batch, seq, after=None):
    n = batch * seq
    tq = 512
    tk = 512 if seq % 512 == 0 else seq
    nqb = seq // tq
    n_sel = seq // SEL_LEN
    rows_c = seq // CMP_STRIDE
    gw = NSA_HPG * NSA_DIM
    cs = np.arange(rows_c)[None, :] * CMP_STRIDE
    ss = np.arange(n_sel)[:, None] * SEL_LEN
    n_cmp = (seq - CMP_LEN) // CMP_STRIDE + 1
    ovt = ((cs < ss + SEL_LEN) & (cs + CMP_LEN > ss) & (np.arange(rows_c)[None, :] < n_cmp))
    ovt = jnp.asarray(ovt.astype(np.float32), BF16)

    qspec = pl.BlockSpec((tq, gw), lambda b, g, i: (b * nqb + i, g))
    cspec = pl.BlockSpec((None, None, rows_c, NSA_DIM), lambda b, g, i: (b, g, 0, 0))
    kspec = pl.BlockSpec((None, seq, NSA_DIM), lambda b, g, i: (g, b, 0))
    args = [nq, nqr, gates, kcmp, vcmp, ovt, ks, vs, kw, vw]
    in_specs = [qspec, qspec,
                pl.BlockSpec((None, GATE_LANES, tq), lambda b, g, i: (g, 0, b * nqb + i)),
                cspec, cspec, pl.BlockSpec(ovt.shape, lambda b, g, i: (0, 0)),
                kspec, kspec, kspec, kspec]
    body = _nsa_kernel
    if after is not None:
        body = _anchored(_nsa_kernel, len(args))
        args.append(after)
        in_specs.append(pl.BlockSpec(memory_space=pl.ANY))
    body = functools.partial(body, tq=tq, tk=tk, seq=seq)
    return pl.pallas_call(
        body,
        grid=(batch, NSA_KV_GROUPS, nqb),
        in_specs=in_specs,
        out_specs=qspec,
        out_shape=jax.ShapeDtypeStruct((n, NSA_WIDTH), BF16),
        scratch_shapes=[pltpu.VMEM((NSA_DIM + SUM_ROWS, seq), BF16), pltpu.VMEM((NSA_DIM + SUM_ROWS, seq), BF16),
                        pltpu.VMEM((NSA_DIM, rows_c), BF16), pltpu.VMEM((n_sel, tq), F32)],
        compiler_params=_params(3),
        name="nsa",
    )(*args)


def _memkv_kernel(mem_ref, w_ref, kv_ref):
    kv_ref[...] = _dot(mem_ref[...].astype(BF16), w_ref[...]).astype(BF16)


def _memkv(mem2d, w_xkv):
    n = mem2d.shape[0]
    w = w_xkv.astype(BF16)
    return pl.pallas_call(
        _memkv_kernel,
        grid=(n // MEM_LEN,),
        in_specs=[pl.BlockSpec((MEM_LEN, D_MODEL), lambda i: (i, 0)),
                  pl.BlockSpec(w.shape, lambda i: (0, 0))],
        out_specs=pl.BlockSpec((MEM_LEN, 2 * D_MODEL), lambda i: (i, 0)),
        out_shape=jax.ShapeDtypeStruct((n, 2 * D_MODEL), BF16),
        compiler_params=_params(1),
        name="memkv",
    )(mem2d, w)


def _pack_halves(v):
    half = D_MODEL // 2
    hi = pltpu.bitcast(v[:, :half].astype(BF16).astype(F32), jnp.uint32)
    lo = pltpu.bitcast(v[:, half:].astype(BF16).astype(F32), jnp.uint32)
    return hi | (lo >> 16)


def _unpack_halves(words):
    return pltpu.bitcast(words & jnp.uint32(0xFFFF0000), F32), pltpu.bitcast(words << 16, F32)


def _postmix_kernel(x_ref, oret_ref, onsa_ref, kv_ref, wout_ref, wq_ref, wo_ref,
                    g1_ref, b1_ref, g2_ref, b2_ref, x2_ref, x2p_ref):
    mixed = jnp.concatenate([oret_ref[...], onsa_ref[...]], axis=1)
    x1 = _layer_norm(DN_ALPHA * x_ref[...] + _dot(mixed, wout_ref[...]), g1_ref[...], b1_ref[...])
    q = (_dot(x1.astype(BF16), wq_ref[...]) * (XATT_DIM ** -0.5)).astype(BF16)
    heads = []
    for h in range(XATT_HEADS):
        cols = slice(h * XATT_DIM, (h + 1) * XATT_DIM)
        s = _dot_nt(q[:, cols], kv_ref[:, cols])
        m = jnp.max(s, axis=-1, keepdims=True)
        p = jnp.exp(s - m)
        l = jnp.sum(p, axis=-1, keepdims=True)
        heads.append(_dot(p.astype(BF16), kv_ref[:, D_MODEL + h * XATT_DIM:D_MODEL + (h + 1) * XATT_DIM]) / l)
    att = jnp.concatenate(heads, axis=1).astype(BF16)
    x2 = _layer_norm(DN_ALPHA * x1 + _dot(att, wo_ref[...]), g2_ref[...], b2_ref[...])
    x2_ref[...] = x2
    x2p_ref[...] = _pack_halves(x2)


def _postmix(x2d, o_ret, o_nsa, kvx, w_out, w_xq, w_xo, ln1_g, ln1_b, ln2_g, ln2_b, batch0, batch, seq,
             after=None):
    n = batch * seq
    tm = 512 if seq % 512 == 0 else seq
    per_b = seq // tm
    row = lambda w: pl.BlockSpec((tm, w), lambda b, i: (b * per_b + i, 0))
    full = lambda a: pl.BlockSpec(a.shape, lambda b, i: (0,) * a.ndim)
    ws = [w_out.astype(BF16), w_xq.astype(BF16), w_xo.astype(BF16)]
    vecs = [v.reshape(1, D_MODEL) for v in (ln1_g, ln1_b, ln2_g, ln2_b)]
    args = [x2d, o_ret, o_nsa, kvx, *ws, *vecs]
    in_specs = ([pl.BlockSpec((tm, D_MODEL), lambda b, i: ((batch0 + b) * per_b + i, 0)),
                 row(RET_WIDTH), row(NSA_WIDTH),
                 pl.BlockSpec((MEM_LEN, 2 * D_MODEL), lambda b, i: (batch0 + b, 0))]
                + [full(w) for w in ws] + [full(v) for v in vecs])
    body = _postmix_kernel
    if after is not None:
        body = _anchored(_postmix_kernel, len(args))
        args.append(after)
        in_specs.append(pl.BlockSpec(memory_space=pl.ANY))
    return pl.pallas_call(
        body,
        grid=(batch, per_b),
        in_specs=in_specs,
        out_specs=[row(D_MODEL),
                   row(D_MODEL // 2)],
        out_shape=[jax.ShapeDtypeStruct((n, D_MODEL), F32),
                   jax.ShapeDtypeStruct((n, D_MODEL // 2), jnp.uint32)],
        compiler_params=_params(2),
        name="postmix",
    )(*args)


def _router_kernel(x_ref, wr_ref, bias_ref, e_ref, rank_ref, w_ref, cnt_ref, cntrow_ref, carry_ref, carryrow_ref):
    tn = x_ref.shape[0]
    per = N_EXPERTS // N_GROUPS

    @pl.when(pl.program_id(0) == 0)
    def _():
        carry_ref[...] = jnp.zeros_like(carry_ref)
        carryrow_ref[...] = jnp.zeros_like(carryrow_ref)

    logits = _dot_nt(wr_ref[...], x_ref[...].astype(BF16))
    scores = jax.nn.sigmoid(logits)
    biased = scores + bias_ref[...]
    b3 = biased.reshape(N_GROUPS, per, tn)
    member = lax.broadcasted_iota(jnp.int32, (N_GROUPS, per, tn), 1)
    top1 = jnp.max(b3, axis=1, keepdims=True)
    first1 = jnp.min(jnp.where(b3 == top1, member, per), axis=1, keepdims=True)
    top2 = jnp.max(jnp.where(member == first1, -jnp.inf, b3), axis=1, keepdims=True)
    gscore = top1 + top2
    gidx = lax.broadcasted_iota(jnp.int32, (N_GROUPS, 1, tn), 0)
    gwork = gscore
    for _ in range(TOPK_GROUPS - 1):
        gbest = jnp.max(gwork, axis=0, keepdims=True)
        gfirst = jnp.min(jnp.where(gwork == gbest, gidx, N_GROUPS), axis=0, keepdims=True)
        gwork = jnp.where(gidx == gfirst, -jnp.inf, gwork)
    kth = jnp.max(gwork, axis=0, keepdims=True)
    work = jnp.where(gscore >= kth, b3, NEG).reshape(N_EXPERTS, tn)
    eidx = lax.broadcasted_iota(jnp.int32, (N_EXPERTS, tn), 0)
    picks = []
    chosen = jnp.zeros((N_EXPERTS, tn), F32)
    for _ in range(TOP_K):
        best = jnp.max(work, axis=0, keepdims=True)
        first = jnp.min(jnp.where(work == best, eidx, N_EXPERTS), axis=0, keepdims=True)
        hit = eidx == first
        picks.append((first, hit))
        chosen = jnp.where(hit, 1.0, chosen)
        work = jnp.where(hit, -jnp.inf, work)

    r_i = lax.broadcasted_iota(jnp.int32, (tn, tn), 0)
    c_i = lax.broadcasted_iota(jnp.int32, (tn, tn), 1)
    before = jnp.where(r_i < c_i, 1.0, 0.0).astype(BF16)
    chosen_b = chosen.astype(BF16)
    rank = _dot(chosen_b, before) + carry_ref[...]
    carry_ref[...] = carry_ref[...] + jnp.sum(chosen, axis=1, keepdims=True)
    carryrow_ref[...] = carryrow_ref[...] + _dot_nt(jnp.ones((8, tn), BF16), chosen_b)
    cnt_ref[...] = carry_ref[...]
    cntrow_ref[...] = carryrow_ref[...]

    wsel = [jnp.sum(jnp.where(hit, scores, 0.0), axis=0, keepdims=True) for _, hit in picks]
    wsum = wsel[0]
    for v in wsel[1:]:
        wsum = wsum + v
    for kk, (first, hit) in enumerate(picks):
        e_ref[kk:kk + 1, :] = first
        rank_ref[kk:kk + 1, :] = jnp.sum(jnp.where(hit, rank, 0.0), axis=0, keepdims=True).astype(jnp.int32)
        w_ref[kk:kk + 1, :] = wsel[kk] / wsum * ROUTED_SCALE


def _router(x2, w_router, router_bias):
    n = x2.shape[0]
    tn = 512 if n % 512 == 0 else n
    wr_t = w_router.T.astype(BF16)
    bias = router_bias.reshape(N_EXPERTS, 1).astype(F32)
    kspec = pl.BlockSpec((TOP_K, tn), lambda i: (0, i))
    return pl.pallas_call(
        _router_kernel,
        grid=(n // tn,),
        in_specs=[pl.BlockSpec((tn, D_MODEL), lambda i: (i, 0)),
                  pl.BlockSpec(wr_t.shape, lambda i: (0, 0)),
                  pl.BlockSpec(bias.shape, lambda i: (0, 0))],
        out_specs=[kspec, kspec, kspec, pl.BlockSpec((N_EXPERTS, 1), lambda i: (0, 0)),
                   pl.BlockSpec((8, N_EXPERTS), lambda i: (0, 0))],
        out_shape=[jax.ShapeDtypeStruct((TOP_K, n), jnp.int32),
                   jax.ShapeDtypeStruct((TOP_K, n), jnp.int32),
                   jax.ShapeDtypeStruct((TOP_K, n), F32),
                   jax.ShapeDtypeStruct((N_EXPERTS, 1), F32),
                   jax.ShapeDtypeStruct((8, N_EXPERTS), F32)],
        scratch_shapes=[pltpu.VMEM((N_EXPERTS, 1), F32), pltpu.VMEM((8, N_EXPERTS), F32)],
        compiler_params=_params(1),
        name="router",
    )(x2, wr_t, bias)


def _slots_kernel(e_ref, rank_ref, cnt_ref, cntrow_ref, dest_ref, blk_e_ref, valid_ref, *, blk, n_blocks):
    pad = lambda c: jnp.ceil(c / blk) * blk
    cnt = cnt_ref[...]
    padded = pad(cnt)
    padded_row = pad(cntrow_ref[0:1, :])
    r_i = lax.broadcasted_iota(jnp.int32, (N_EXPERTS, N_EXPERTS), 0)
    c_i = lax.broadcasted_iota(jnp.int32, (N_EXPERTS, N_EXPERTS), 1)
    start = jnp.sum(jnp.where(c_i < r_i, padded_row, 0.0), axis=1, keepdims=True)
    end = start + padded
    e = e_ref[...]
    dest = rank_ref[...]
    for ex in range(N_EXPERTS):
        dest = dest + jnp.where(e == ex, start[ex:ex + 1, :].astype(jnp.int32), 0)
    dest_ref[...] = dest
    bstart = (lax.broadcasted_iota(jnp.int32, (1, n_blocks), 1) * blk).astype(F32)
    owner = jnp.sum(jnp.where(end <= bstart, 1.0, 0.0), axis=0, keepdims=True)
    blk_e_ref[...] = jnp.minimum(owner, N_EXPERTS - 1.0).astype(jnp.int32)
    inside = (start <= bstart) & (bstart < end)
    real = jnp.clip(start + cnt - bstart, 0.0, float(blk))
    valid_ref[...] = jnp.sum(jnp.where(inside, real, 0.0), axis=0, keepdims=True).astype(jnp.int32)


def _slots(e_k, rank_k, counts, counts_row, blk, n_blocks):
    n = e_k.shape[1]
    full = lambda shape: pl.BlockSpec(shape, lambda: (0,) * len(shape))
    return pl.pallas_call(
        functools.partial(_slots_kernel, blk=blk, n_blocks=n_blocks),
        in_specs=[full((TOP_K, n)), full((TOP_K, n)), full((N_EXPERTS, 1)), full((8, N_EXPERTS))],
        out_specs=[full((TOP_K, n)), full((1, n_blocks)), full((1, n_blocks))],
        out_shape=[jax.ShapeDtypeStruct((TOP_K, n), jnp.int32),
                   jax.ShapeDtypeStruct((1, n_blocks), jnp.int32),
                   jax.ShapeDtypeStruct((1, n_blocks), jnp.int32)],
        compiler_params=pltpu.CompilerParams(vmem_limit_bytes=VMEM_LIMIT),
        name="slots",
    )(e_k, rank_k, counts, counts_row)


def _sc_worker_base(per_worker):
    return (lax.axis_index("s") * SC_CORES + lax.axis_index("c")) * per_worker


def _sc_scatter_rows(rows, idx, n_out):
    n, width = rows.shape
    k_lists = idx.shape[0] // n
    workers = SC_CORES * SC_SUBCORES
    per_worker = n // workers
    assert per_worker * workers == n and per_worker % SC_CHUNK == 0
    mesh = plsc.VectorSubcoreMesh(core_axis_name="c", subcore_axis_name="s")

    @functools.partial(
        pl.kernel, mesh=mesh,
        out_type=jax.ShapeDtypeStruct((n_out, width), rows.dtype),
        scratch_types=[pltpu.VMEM((SC_CHUNK, width), rows.dtype)]
                      + [pltpu.VMEM((SC_CHUNK,), jnp.int32)] * k_lists + [pltpu.SemaphoreType.DMA] * 3,
        name="sc_scatter")
    def scatter(rows_hbm, idx_hbm, out_hbm, rows_v, *rest):
        idx_vs = rest[:k_lists]
        sem_rows, sem_idx, sem_out = rest[k_lists:]
        base = _sc_worker_base(per_worker)

        @pl.loop(0, per_worker // SC_CHUNK)
        def _(ci):
            off = pl.multiple_of(base + ci * SC_CHUNK, SC_CHUNK)
            loads = [pltpu.async_copy(rows_hbm.at[pl.ds(off, SC_CHUNK)], rows_v, sem_rows)]
            loads += [pltpu.async_copy(idx_hbm.at[pl.ds(pl.multiple_of(k * n + off, SC_CHUNK), SC_CHUNK)],
                                       idx_vs[k], sem_idx) for k in range(k_lists)]
            for c in loads:
                c.wait()
            copies = [pltpu.async_copy(rows_v, out_hbm.at[idx_vs[k]], sem_out) for k in range(k_lists)]
            for c in copies:
                c.wait()

    return scatter(rows, idx)


def _experts_kernel(blk_e_ref, valid_ref, xs_ref, wg_ref, wu_ref, wd_ref, y_ref, wg_b, wu_b, wd_b):
    i = pl.program_id(0)
    valid = valid_ref[i]

    @pl.when((i == 0) | (blk_e_ref[i] != blk_e_ref[jnp.maximum(i - 1, 0)]))
    def _():
        wg_b[...] = wg_ref[...].astype(BF16)
        wu_b[...] = wu_ref[...].astype(BF16)
        wd_b[...] = wd_ref[...].astype(BF16)

    @pl.when(valid > 0)
    def _():
        half = D_MODEL // 2
        row = lax.broadcasted_iota(jnp.int32, (xs_ref.shape[0], 1), 0)
        hi, lo = (v.astype(BF16) for v in _unpack_halves(jnp.where(row < valid, xs_ref[...], jnp.uint32(0))))
        gate = _dot(hi, wg_b[:half, :]) + _dot(lo, wg_b[half:, :])
        up = _dot(hi, wu_b[:half, :]) + _dot(lo, wu_b[half:, :])
        y_ref[...] = _pack_halves(_dot((jax.nn.silu(gate) * up).astype(BF16), wd_b[...]))

    @pl.when(valid <= 0)
    def _():
        y_ref[...] = jnp.zeros_like(y_ref)


def _experts(blk_e, valid, xs, w_gate, w_up, w_down, blk):
    cap, width = xs.shape
    wspec = lambda a: pl.BlockSpec((None,) + a.shape[1:], lambda i, be, nv: (be[i], 0, 0))
    rows = pl.BlockSpec((blk, width), lambda i, be, nv: (i, 0))
    return pl.pallas_call(
        _experts_kernel,
        grid_spec=pltpu.PrefetchScalarGridSpec(
            num_scalar_prefetch=2,
            grid=(cap // blk,),
            in_specs=[rows, wspec(w_gate), wspec(w_up), wspec(w_down)],
            out_specs=rows,
            scratch_shapes=[pltpu.VMEM(w.shape[1:], BF16) for w in (w_gate, w_up, w_down)],
        ),
        out_shape=jax.ShapeDtypeStruct(xs.shape, xs.dtype),
        compiler_params=_params(1),
        name="experts",
    )(blk_e, valid, xs, w_gate, w_up, w_down)


def _sc_gather_rows(table, idx):
    b, width = idx.shape[0], table.shape[1]
    workers = SC_CORES * SC_SUBCORES
    per_worker = b // workers
    assert per_worker * workers == b and per_worker % (SC_CHUNK * SC_INFLIGHT) == 0
    mesh = plsc.VectorSubcoreMesh(core_axis_name="c", subcore_axis_name="s")

    @functools.partial(
        pl.kernel, mesh=mesh,
        out_type=jax.ShapeDtypeStruct((b, width), table.dtype),
        scratch_types=[pltpu.VMEM((SC_CHUNK,), jnp.int32)] * SC_INFLIGHT
                      + [pltpu.VMEM((SC_CHUNK, width), table.dtype)] * SC_INFLIGHT
                      + [pltpu.SemaphoreType.DMA] * (1 + 2 * SC_INFLIGHT),
        name="sc_gather")
    def gather(table_hbm, idx_hbm, out_hbm, *scratch):
        idx_vs = scratch[:SC_INFLIGHT]
        rows_vs = scratch[SC_INFLIGHT:2 * SC_INFLIGHT]
        sem_idx = scratch[2 * SC_INFLIGHT]
        sem_rows = scratch[2 * SC_INFLIGHT + 1:3 * SC_INFLIGHT + 1]
        sem_out = scratch[3 * SC_INFLIGHT + 1:]
        base = _sc_worker_base(per_worker)
        lanes = range(SC_INFLIGHT)

        @pl.loop(0, per_worker // (SC_CHUNK * SC_INFLIGHT))
        def _(gi):
            offs = [pl.multiple_of(base + (gi * SC_INFLIGHT + j) * SC_CHUNK, SC_CHUNK) for j in lanes]
            loads = [pltpu.async_copy(idx_hbm.at[pl.ds(offs[j], SC_CHUNK)], idx_vs[j], sem_idx) for j in lanes]
            for c in loads:
                c.wait()
            gathers = [pltpu.async_copy(table_hbm.at[idx_vs[j]], rows_vs[j], sem_rows[j]) for j in lanes]
            writes = []
            for j in lanes:
                gathers[j].wait()
                writes.append(pltpu.async_copy(rows_vs[j], out_hbm.at[pl.ds(offs[j], SC_CHUNK)], sem_out[j]))
            for c in writes:
                c.wait()

    return gather(table, idx)


def _combine_kernel(x_ref, wk_ref, yk_ref, wsg_ref, wsu_ref, wsd_ref, g_ref, b_ref, *rest):
    o_ref = rest[-1]
    x = x_ref[...]
    xb = x.astype(BF16)
    shared = _dot((jax.nn.silu(_dot(xb, wsg_ref[...])) * _dot(xb, wsu_ref[...])).astype(BF16), wsd_ref[...])
    wk = wk_ref[...]
    routed_hi = routed_lo = None
    for kk in range(TOP_K):
        hi, lo = _unpack_halves(yk_ref[kk])
        w = wk[:, kk:kk + 1]
        routed_hi = hi * w if kk == 0 else routed_hi + hi * w
        routed_lo = lo * w if kk == 0 else routed_lo + lo * w
    routed = jnp.concatenate([routed_hi, routed_lo], axis=1)
    o_ref[...] = _layer_norm(DN_ALPHA * x + (routed + shared), g_ref[...], b_ref[...])


def _combine(x2, w_tok, yk, ws_gate, ws_up, ws_down, ln3_g, ln3_b, row0, n_total, out_prev, after=None):
    n = x2.shape[0]
    tt = 512 if n % 512 == 0 and row0 % 512 == 0 else n
    blk0 = row0 // tt
    ws = [ws_gate.astype(BF16), ws_up.astype(BF16), ws_down.astype(BF16)]
    vecs = [ln3_g.reshape(1, D_MODEL), ln3_b.reshape(1, D_MODEL)]
    full = lambda a: pl.BlockSpec(a.shape, lambda i: (0,) * a.ndim)
    args = [x2, w_tok, yk, *ws, *vecs]
    in_specs = ([pl.BlockSpec((tt, D_MODEL), lambda i: (i, 0)),
                 pl.BlockSpec((tt, TOP_K), lambda i: (i, 0)),
                 pl.BlockSpec((TOP_K, tt, D_MODEL // 2), lambda i: (0, i, 0))]
                + [full(a) for a in ws] + [full(v) for v in vecs])
    aliases = {}
    if after is not None:
        args.append(after)
        in_specs.append(pl.BlockSpec(memory_space=pl.ANY))
    if out_prev is not None:
        aliases = {len(args): 0}
        args.append(out_prev)
        in_specs.append(pl.BlockSpec(memory_space=pl.ANY))
    return pl.pallas_call(
        _combine_kernel,
        grid=(n // tt,),
        in_specs=in_specs,
        out_specs=pl.BlockSpec((tt, D_MODEL), lambda i: (blk0 + i, 0)),
        out_shape=jax.ShapeDtypeStruct((n_total, D_MODEL), F32),
        input_output_aliases=aliases,
        compiler_params=_params(1),
        name="combine",
    )(*args)


def _moe_rows(x2, x2p, w_router, router_bias, w_gate, w_up, w_down):
    n = x2.shape[0]
    blk = 512
    cap = n * TOP_K + N_EXPERTS * blk
    n_blocks = cap // blk
    e_k, rank_k, w_k, counts, counts_row = _router(x2, w_router, router_bias)
    dest, blk_e, valid = _slots(e_k, rank_k, counts, counts_row, blk, n_blocks)
    dest = dest.reshape(-1)
    xs = _sc_scatter_rows(x2p, dest, cap)
    y = _experts(blk_e.reshape(-1), valid.reshape(-1), xs, w_gate, w_up, w_down, blk)
    yk = _sc_gather_rows(y, dest).reshape(TOP_K, n, D_MODEL // 2)
    return w_k.T, yk, y


def _layer(x, mem, positions, w_in, cmp_pe_k, cmp_pe_v, cmp_w1_k, cmp_w2_k, cmp_w1_v, cmp_w2_v,
           w_out, ln1_g, ln1_b, w_xq, w_xkv, w_xo, ln2_g, ln2_b, w_router, router_bias,
           w_gate, w_up, w_down, ws_gate, ws_up, ws_down, ln3_g, ln3_b):
    batch, seq, _ = x.shape
    n_total = batch * seq
    x2d = x.reshape(n_total, D_MODEL)
    pos_col = positions.astype(F32).reshape(n_total, 1)
    kvx = _memkv(mem.reshape(batch * MEM_LEN, D_MODEL), w_xkv)
    groups = BATCH_GROUPS if batch % BATCH_GROUPS == 0 else 1
    per = batch // groups
    n = per * seq
    combine = functools.partial(_combine, ws_gate=ws_gate, ws_up=ws_up, ws_down=ws_down, ln3_g=ln3_g,
                                ln3_b=ln3_b, n_total=n_total)
    out = y_prev = yk_prev = pending = None
    for gi in range(groups):
        (rq, rk, rv, rg, nq, nqr, kc, vc, ks, vs, kw, vw, gates) = _inproj(x2d, pos_col, w_in, gi * n, n)
        o_ret = _retention(rq, rk, rv, rg, per, seq)
        kcmp = _compress(kc, cmp_pe_k, cmp_w1_k, cmp_w2_k, per, seq)
        vcmp = _compress(vc, cmp_pe_v, cmp_w1_v, cmp_w2_v, per, seq)
        o_nsa = _nsa(nq, nqr, gates, kcmp, vcmp, ks, vs, kw, vw, per, seq, after=y_prev)
        x2, x2p = _postmix(x2d, o_ret, o_nsa, kvx, w_out, w_xq, w_xo, ln1_g, ln1_b, ln2_g, ln2_b,
                           gi * per, per, seq, after=yk_prev)
        w_tok, yk, y = _moe_rows(x2, x2p, w_router, router_bias, w_gate, w_up, w_down)
        if pending is not None:
            out = combine(**pending, out_prev=out, after=y)
        pending = dict(x2=x2, w_tok=w_tok, yk=yk, row0=gi * n)
        y_prev, yk_prev = y, yk
    out = combine(**pending, out_prev=out)
    return out.reshape(batch, seq, D_MODEL)


def kernel(x, mem, positions, w_in, cmp_pe_k, cmp_pe_v, cmp_w1_k, cmp_w2_k, cmp_w1_v, cmp_w2_v, w_out, ln1_g, ln1_b, w_xq, w_xkv, w_xo, ln2_g, ln2_b, w_router, router_bias, w_gate, w_up, w_down, ws_gate, ws_up, ws_down, ln3_g, ln3_b):
    for l in range(DEPTH):
        x = _layer(x, mem, positions, w_in[l], cmp_pe_k[l], cmp_pe_v[l], cmp_w1_k[l], cmp_w2_k[l],
                   cmp_w1_v[l], cmp_w2_v[l], w_out[l], ln1_g[l], ln1_b[l], w_xq[l], w_xkv[l],
                   w_xo[l], ln2_g[l], ln2_b[l], w_router[l], router_bias[l], w_gate[l], w_up[l],
                   w_down[l], ws_gate[l], ws_up[l], ws_down[l], ln3_g[l], ln3_b[l])
    return x
```

```python
import functools

import numpy as np
import jax
import jax.numpy as jnp
from jax import lax
from jax.experimental import pallas as pl
from jax.experimental.pallas import tpu as pltpu
from jax.experimental.pallas import tpu_sc as plsc

D_MODEL = 1024
MEM_LEN = 256
DEPTH = 1
DN_ALPHA = (2 * DEPTH) ** 0.25
LN_EPS = 1e-5
NEG = -1e30
FORCE = 1e9

RET_HEADS = 4
RET_DIM = 128
RET_CHUNK = 128
RET_ROPE_BASE = 10000.0
RET_STEP_CHUNKS = 4
RET_WIDTH = RET_HEADS * RET_DIM

NSA_HEADS = 8
NSA_KV_GROUPS = 2
NSA_HPG = NSA_HEADS // NSA_KV_GROUPS
NSA_DIM = 64
NSA_WIDTH = NSA_HEADS * NSA_DIM
KV_WIDTH = NSA_KV_GROUPS * NSA_DIM
CMP_LEN = 32
CMP_STRIDE = 16
CMP_HIDDEN = 256
SEL_LEN = 64
SEL_SHIFT = 6
SEL_TOPK = 16
WIN = 512
ROPE_THETA = 500000.0
ROPE_DIMS = NSA_DIM // 4
GATE_LANES = 16
NSA_CHAINS = 1
SUM_ROWS = 16
WIN_PART = 256

BATCH_GROUPS = 2

SC_CORES = 2
SC_SUBCORES = 16
SC_CHUNK = 64
SC_INFLIGHT = 2

XATT_HEADS = 4
XATT_DIM = D_MODEL // XATT_HEADS

N_EXPERTS = 64
TOP_K = 8
N_GROUPS = 8
TOPK_GROUPS = 4
EXPERT_FF = 256
SHARED_FF = 256
ROUTED_SCALE = 2.5

LANES = 128
VMEM_LIMIT = 56 * 1024 * 1024

F32 = jnp.float32
BF16 = jnp.bfloat16
NT_DIMS = (((1,), (1,)), ((), ()))


def _params(n_axes):
    return pltpu.CompilerParams(dimension_semantics=("arbitrary",) * n_axes,
                                vmem_limit_bytes=VMEM_LIMIT)


def _dot(a, b):
    return jnp.dot(a, b, preferred_element_type=F32)


def _dot_nt(a, b):
    return lax.dot_general(a, b, NT_DIMS, preferred_element_type=F32)


def _layer_norm(v, g, b):
    mu = jnp.mean(v, axis=-1, keepdims=True)
    d = v - mu
    var = jnp.mean(d * d, axis=-1, keepdims=True)
    return d * lax.rsqrt(var + LN_EPS) * g + b


def _inproj_kernel(x_ref, pos_ref, wret_ref, wnq_ref, wkv_ref, wg_ref, invr_ref, invn_ref,
                   rq_ref, rk_ref, rv_ref, rg_ref, nq_ref, nqr_ref, kc_ref, vc_ref,
                   ks_ref, vs_ref, kw_ref, vw_ref, gate_ref):
    xb = x_ref[...].astype(BF16)
    pos = pos_ref[...]
    lane = lax.broadcasted_iota(jnp.int32, (1, LANES), 1)

    ang = pos * invr_ref[...]
    cos_r = jnp.cos(ang)
    sin_r = jnp.sin(ang)
    sin_r = jnp.where(lane < RET_DIM // 2, -sin_r, sin_r)
    q_all = _dot(xb, wret_ref[:, :RET_WIDTH])
    k_all = _dot(xb, wret_ref[:, RET_WIDTH:2 * RET_WIDTH])
    for h in range(RET_HEADS):
        cols = slice(h * RET_DIM, (h + 1) * RET_DIM)
        q = q_all[:, cols]
        rq_ref[:, cols] = (q * cos_r + pltpu.roll(q, RET_DIM // 2, 1) * sin_r).astype(BF16)
        k = k_all[:, cols]
        k = (k * cos_r + pltpu.roll(k, RET_DIM // 2, 1) * sin_r) * (RET_DIM ** -0.5)
        rk_ref[:, cols] = k.astype(BF16)
    rv_ref[...] = _dot(xb, wret_ref[:, 2 * RET_WIDTH:3 * RET_WIDTH]).astype(BF16)
    rg_ref[...] = _dot(xb, wret_ref[:, 3 * RET_WIDTH:4 * RET_WIDTH]).astype(BF16)

    half = ROPE_DIMS // 2
    j = lane % NSA_DIM
    angn = pos * invn_ref[...]
    cos_n = jnp.cos(angn)
    sin_n = jnp.sin(angn)
    sin_lo = jnp.where(j < half, -sin_n, 0.0)
    sin_hi = jnp.where((j >= half) & (j < 2 * half), sin_n, 0.0)

    def rope_n(v):
        return v * cos_n + pltpu.roll(v, half, 1) * sin_hi + pltpu.roll(v, LANES - half, 1) * sin_lo

    scale = NSA_DIM ** -0.5
    nq_all = _dot(xb, wnq_ref[...])
    for c in range(NSA_WIDTH // LANES):
        cols = slice(c * LANES, (c + 1) * LANES)
        q = nq_all[:, cols]
        nq_ref[:, cols] = (q * scale).astype(BF16)
        nqr_ref[:, cols] = (rope_n(q) * scale).astype(BF16)

    kv_all = _dot(xb, wkv_ref[...])

    def kv(i):
        return kv_all[:, i * KV_WIDTH:(i + 1) * KV_WIDTH]

    def split_groups(ref, v):
        for g in range(NSA_KV_GROUPS):
            ref[g] = v[:, g * NSA_DIM:(g + 1) * NSA_DIM].astype(BF16)

    kc_ref[...] = kv(0)
    vc_ref[...] = kv(1)
    split_groups(ks_ref, rope_n(kv(2)))
    split_groups(vs_ref, kv(3))
    split_groups(kw_ref, rope_n(kv(4)))
    split_groups(vw_ref, kv(5))

    gt = jax.nn.sigmoid(_dot_nt(wg_ref[...], xb))
    for g in range(NSA_KV_GROUPS):
        gate_ref[g] = gt[g * GATE_LANES:(g + 1) * GATE_LANES, :]


def _inproj(x2d, pos_col, w_in, row0, n):
    tm = 512 if n % 512 == 0 and row0 % 512 == 0 else n
    blk0 = row0 // tm
    off = np.cumsum([0] + [RET_WIDTH] * 4 + [NSA_WIDTH] + [KV_WIDTH] * 6)
    w_ret = w_in[:, :off[4]].astype(BF16)
    w_nq = w_in[:, off[4]:off[5]].astype(BF16)
    w_kv = w_in[:, off[5]:off[11]].astype(BF16)
    wg = w_in[:, off[11]:].reshape(D_MODEL, NSA_KV_GROUPS, NSA_HPG * 3)
    wg = jnp.pad(wg, ((0, 0), (0, 0), (0, GATE_LANES - NSA_HPG * 3)))
    wg = wg.reshape(D_MODEL, NSA_KV_GROUPS * GATE_LANES).T.astype(BF16)

    lane = np.arange(LANES)
    half_r = RET_DIM // 2
    inv_r = (np.float32(RET_ROPE_BASE) ** (-np.arange(half_r, dtype=np.float32) / np.float32(half_r)))
    inv_r = inv_r.astype(np.float32)[lane % half_r][None, :]
    half_n = ROPE_DIMS // 2
    inv_n = (np.float32(ROPE_THETA) ** (-np.arange(half_n, dtype=np.float32) / np.float32(half_n)))
    jn = lane % NSA_DIM
    inv_n = np.where(jn < ROPE_DIMS, inv_n.astype(np.float32)[jn % half_n], np.float32(0.0))[None, :]

    row = lambda w: pl.BlockSpec((tm, w), lambda i: (i, 0))
    src_row = lambda w: pl.BlockSpec((tm, w), lambda i: (blk0 + i, 0))
    full = lambda a: pl.BlockSpec(a.shape, lambda i: (0,) * a.ndim)
    grp = lambda w: pl.BlockSpec((NSA_KV_GROUPS, tm, w), lambda i: (0, i, 0))
    bf = lambda w: jax.ShapeDtypeStruct((n, w), BF16)
    gbf = jax.ShapeDtypeStruct((NSA_KV_GROUPS, n, NSA_DIM), BF16)
    inv_r = jnp.asarray(inv_r, F32)
    inv_n = jnp.asarray(inv_n, F32)
    return pl.pallas_call(
        _inproj_kernel,
        grid=(n // tm,),
        in_specs=[src_row(D_MODEL), src_row(1), full(w_ret), full(w_nq), full(w_kv), full(wg),
                  full(inv_r), full(inv_n)],
        out_specs=[row(RET_WIDTH)] * 4 + [row(NSA_WIDTH)] * 2 + [row(KV_WIDTH)] * 2
                  + [grp(NSA_DIM)] * 4
                  + [pl.BlockSpec((NSA_KV_GROUPS, GATE_LANES, tm), lambda i: (0, 0, i))],
        out_shape=[bf(RET_WIDTH)] * 4 + [bf(NSA_WIDTH)] * 2
                  + [jax.ShapeDtypeStruct((n, KV_WIDTH), F32)] * 2 + [gbf] * 4
                  + [jax.ShapeDtypeStruct((NSA_KV_GROUPS, GATE_LANES, n), F32)],
        compiler_params=_params(1),
        name="inproj",
    )(x2d, pos_col, w_ret, w_nq, w_kv, wg, inv_r, inv_n)


def _retention_kernel(q_ref, k_ref, v_ref, g_ref, o_ref, state_ref):
    c = RET_CHUNK

    @pl.when(pl.program_id(1) == 0)
    def _():
        state_ref[...] = jnp.zeros_like(state_ref)

    row = lax.broadcasted_iota(jnp.int32, (c, c), 0)
    col = lax.broadcasted_iota(jnp.int32, (c, c), 1)
    rel = (row - col).astype(F32)
    idx = lax.broadcasted_iota(jnp.int32, (c, 1), 0).astype(F32)
    for h in range(RET_HEADS):
        log_g = float(np.log(np.float32(1.0) - np.float32(2.0) ** np.float32(-5.0 - h)))
        cols = slice(h * RET_DIM, (h + 1) * RET_DIM)
        dmask = jnp.where(rel >= 0, jnp.exp(log_g * jnp.maximum(rel, 0.0)), 0.0)
        zeta = jnp.exp(log_g * (c - 1.0 - idx))
        xi = jnp.exp(log_g * (idx + 1.0))
        for j in range(q_ref.shape[0] // c):
            rows = slice(j * c, (j + 1) * c)
            q = q_ref[rows, cols]
            k = k_ref[rows, cols]
            v = v_ref[rows, cols]
            scores = _dot_nt(q, k) * dmask
            inner = _dot(scores.astype(BF16), v)
            prev = state_ref[h]
            cross = _dot(q, prev.astype(BF16)) * xi
            kz = (k.astype(F32) * zeta).astype(BF16)
            kv = lax.dot_general(kz, v, (((0,), (0,)), ((), ())), preferred_element_type=F32)
            state_ref[h] = prev * float(np.exp(np.float32(log_g) * np.float32(c))) + kv
            o = inner + cross
            mu = jnp.mean(o, axis=-1, keepdims=True)
            d = o - mu
            var = jnp.mean(d * d, axis=-1, keepdims=True)
            o = d * lax.rsqrt(var + LN_EPS)
            o_ref[rows, cols] = (jax.nn.silu(g_ref[rows, cols].astype(F32)) * o).astype(BF16)


def _retention(rq, rk, rv, rg, batch, seq):
    per_step = RET_STEP_CHUNKS if (seq // RET_CHUNK) % RET_STEP_CHUNKS == 0 else 1
    nc = seq // (RET_CHUNK * per_step)
    spec = pl.BlockSpec((RET_CHUNK * per_step, RET_WIDTH), lambda b, n: (b * nc + n, 0))
    return pl.pallas_call(
        _retention_kernel,
        grid=(batch, nc),
        in_specs=[spec] * 4,
        out_specs=spec,
        out_shape=jax.ShapeDtypeStruct(rq.shape, BF16),
        scratch_shapes=[pltpu.VMEM((RET_HEADS, RET_DIM, RET_DIM), F32)],
        compiler_params=_params(2),
        name="retention",
    )(rq, rk, rv, rg)


def _compress_kernel(a_ref, pe_ref, w1_ref, w2_ref, o_ref, shift_ref, *, n_cmp):
    rows = a_ref.shape[0]
    a = a_ref[...]
    lo = (a + pe_ref[0]).astype(BF16)
    hi = (a + pe_ref[1]).astype(BF16)
    ridx = lax.broadcasted_iota(jnp.int32, (rows, 1), 0)
    shift_ref[rows:rows + 8, :] = jnp.zeros((8, CMP_HIDDEN), F32)
    for g in range(NSA_KV_GROUPS):
        p = _dot(lo, w1_ref[0, g])
        shift_ref[0:rows, :] = _dot(hi, w1_ref[1, g])
        hid = jax.nn.silu(p + shift_ref[pl.ds(1, rows), :])
        out = _dot(hid.astype(BF16), w2_ref[...])
        o_ref[g] = jnp.where(ridx < n_cmp, out, 0.0).astype(BF16)


def _compress(a, pe, w1, w2, batch, seq):
    rows = seq // CMP_STRIDE
    per = CMP_STRIDE * KV_WIDTH
    n_cmp = (seq - CMP_LEN) // CMP_STRIDE + 1
    a2 = a.reshape(batch * rows, per)
    pe2 = jnp.tile(pe.reshape(2, CMP_STRIDE, 1, NSA_DIM), (1, 1, NSA_KV_GROUPS, 1)).reshape(2, 1, per)
    w1r = w1.reshape(2, CMP_STRIDE, 1, NSA_DIM, CMP_HIDDEN)
    eye = jnp.eye(NSA_KV_GROUPS, dtype=w1.dtype).reshape(1, NSA_KV_GROUPS, 1, NSA_KV_GROUPS, 1, 1)
    w1x = (w1r[:, None] * eye).reshape(2, NSA_KV_GROUPS, per, CMP_HIDDEN).astype(BF16)
    w2b = w2.astype(BF16)
    full = lambda arr: pl.BlockSpec(arr.shape, lambda b: (0,) * arr.ndim)
    return pl.pallas_call(
        functools.partial(_compress_kernel, n_cmp=n_cmp),
        grid=(batch,),
        in_specs=[pl.BlockSpec((rows, per), lambda b: (b, 0)), full(pe2), full(w1x), full(w2b)],
        out_specs=pl.BlockSpec((None, NSA_KV_GROUPS, rows, NSA_DIM), lambda b: (b, 0, 0, 0)),
        out_shape=jax.ShapeDtypeStruct((batch, NSA_KV_GROUPS, rows, NSA_DIM), BF16),
        scratch_shapes=[pltpu.VMEM((rows + 8, CMP_HIDDEN), F32)],
        compiler_params=_params(1),
        name="compress",
    )(a2, pe2, w1x, w2b)


def _heads_to_lanes(ref):
    vt = ref[...].astype(F32).T
    return jnp.concatenate([vt[h * NSA_DIM:(h + 1) * NSA_DIM] for h in range(NSA_HPG)], axis=1).astype(BF16)


def _tile_heads(v):
    return jnp.concatenate([v] * NSA_HPG, axis=1)


def _transpose_into(dst_ref, src_ref, chunk):
    def step(c, _):
        c0 = pl.multiple_of(c * chunk, chunk)
        dst_ref[:NSA_DIM, pl.ds(c0, chunk)] = src_ref[pl.ds(c0, chunk), :].astype(F32).T.astype(BF16)
        return 0
    lax.fori_loop(0, src_ref.shape[0] // chunk, step, 0)


def _nsa_kernel(qraw_ref, qrot_ref, gate_ref, kcmp_ref, vcmp_ref, ovt_ref,
                ks_ref, vs_ref, kw_ref, vw_ref, o_ref, vst_ref, vwt_ref, vct_ref, bias_ref, *, tq, tk, seq):
    i = pl.program_id(2)
    t0 = i * tq
    cols = NSA_HPG * tq
    n_sel = seq // SEL_LEN
    n_cmp_rows = seq // CMP_STRIDE
    blocks_per_tile = tk // SEL_LEN

    @pl.when(i == 0)
    def _():
        chunk = min(512, n_cmp_rows)
        _transpose_into(vst_ref, vs_ref, chunk)
        _transpose_into(vwt_ref, vw_ref, chunk)
        _transpose_into(vct_ref, vcmp_ref, chunk)
        vst_ref[NSA_DIM:, :] = jnp.ones((SUM_ROWS, seq), BF16)
        vwt_ref[NSA_DIM:, :] = jnp.ones((SUM_ROWS, seq), BF16)

    def split_sum(acc):
        return acc[:NSA_DIM] / acc[NSA_DIM:NSA_DIM + 1]

    q_raw = _heads_to_lanes(qraw_ref)
    q_rot = _heads_to_lanes(qrot_ref)
    t_row = t0 + lax.broadcasted_iota(jnp.int32, (1, tq), 1)

    chain_w = cols // NSA_CHAINS
    heads_per_chain = chain_w // tq
    chains = [slice(c * chain_w, (c + 1) * chain_w) for c in range(NSA_CHAINS)]
    tile_chain = lambda v: jnp.concatenate([v] * heads_per_chain, axis=1)

    pw = min(tq, WIN_PART)
    parts = []
    for u in range(tq // pw):
        span = WIN + pw
        ws = pl.multiple_of(jnp.maximum(t0 + u * pw - WIN, 0), pw)
        dist = t_row[:, u * pw:(u + 1) * pw] - (ws + lax.broadcasted_iota(jnp.int32, (span, 1), 0))
        bias_w = jnp.concatenate([jnp.where((dist >= 0) & (dist < WIN), 0.0, NEG)] * NSA_HPG, axis=1)
        q_part = jnp.concatenate([q_rot[:, h * tq + u * pw:h * tq + (u + 1) * pw] for h in range(NSA_HPG)],
                                 axis=1)
        s_w = _dot(kw_ref[pl.ds(ws, span), :], q_part) + bias_w
        p_w = jnp.exp((s_w - jnp.max(s_w, axis=0, keepdims=True)).astype(BF16))
        parts.append(split_sum(_dot(vwt_ref[:, pl.ds(ws, span)], p_w)))
    o_w = jnp.concatenate([parts[u][:, h * pw:(h + 1) * pw]
                           for h in range(NSA_HPG) for u in range(tq // pw)], axis=1)

    c_idx = lax.broadcasted_iota(jnp.int32, (n_cmp_rows, 1), 0)
    valid = tile_chain(jnp.where(c_idx * CMP_STRIDE + (CMP_LEN - 1) <= t_row, 1.0, 0.0))
    bias_c = (valid - 1.0) * (-NEG)
    o_c = []
    p_sum = None
    for c in chains:
        s_c = _dot(kcmp_ref[...], q_raw[:, c]) + bias_c
        e_c = jnp.exp(s_c - jnp.max(s_c, axis=0, keepdims=True)) * valid
        l_c = jnp.sum(e_c, axis=0, keepdims=True)
        p_c = e_c / jnp.where(l_c > 0.0, l_c, 1.0)
        o_c.append(_dot(vct_ref[...], p_c.astype(BF16)))
        for h in range(heads_per_chain):
            p_h = p_c[:, h * tq:(h + 1) * tq]
            p_sum = p_h if p_sum is None else p_sum + p_h
    o_c = jnp.concatenate(o_c, axis=1)

    p_hi = p_sum.astype(BF16)
    p_lo = (p_sum - p_hi.astype(F32)).astype(BF16)
    ovt = ovt_ref[...]
    imp = _dot(ovt, p_hi) + _dot(ovt, p_lo)
    jb = lax.broadcasted_iota(jnp.int32, (n_sel, tq), 0)
    cur = (t0 + lax.broadcasted_iota(jnp.int32, (n_sel, tq), 1)) >> SEL_SHIFT
    forced = (jb == 0) | (jb == cur) | (jb == cur - 1)
    work = jnp.where(forced, FORCE, imp)
    work = jnp.where(jb <= cur, work, NEG)
    sel_t = jnp.zeros((n_sel, tq), F32)
    for _ in range(min(SEL_TOPK, n_sel)):
        best = jnp.max(work, axis=0, keepdims=True)
        first = jnp.min(jnp.where(work == best, jb, n_sel), axis=0, keepdims=True)
        hit = jb == first
        sel_t = jnp.where(hit, 1.0, sel_t)
        work = jnp.where(hit, -jnp.inf, work)
    bias_ref[...] = jnp.where(sel_t > 0.5, 0.0, NEG)

    def sel_tile(kt, carry, causal):
        k0 = pl.multiple_of(kt * tk, tk)
        bias = jnp.concatenate(
            [jnp.broadcast_to(bias_ref[pl.ds(kt * blocks_per_tile + j, 1), :], (SEL_LEN, tq))
             for j in range(blocks_per_tile)], axis=0)
        if causal:
            kpos = k0 + lax.broadcasted_iota(jnp.int32, (tk, 1), 0)
            bias = jnp.where(kpos <= t_row, bias, NEG)
        bias = tile_chain(bias)
        k_t = ks_ref[pl.ds(k0, tk), :]
        v_t = vst_ref[:, pl.ds(k0, tk)]
        out = []
        scores = [_dot(k_t, q_rot[:, c]) + bias for c in chains]
        for (m, acc), s in zip(carry, scores):
            m_new = jnp.maximum(m, jnp.max(s, axis=0, keepdims=True))
            p = jnp.exp((s - m_new).astype(BF16))
            acc = jnp.exp(m - m_new) * acc + _dot(v_t, p)
            out.append((m_new, acc))
        return tuple(out)

    n_full = t0 // tk
    init = tuple((jnp.full((1, chain_w), NEG, F32), jnp.zeros((NSA_DIM + SUM_ROWS, chain_w), F32))
                 for _ in chains)
    carry = lax.fori_loop(0, n_full, functools.partial(sel_tile, causal=False), init)
    for d in range(max(tq // tk, 1)):
        carry = sel_tile(n_full + d, carry, causal=True)
    o_s = jnp.concatenate([split_sum(acc) for _, acc in carry], axis=1)

    gt = gate_ref[...]
    outs = []
    for h in range(NSA_HPG):
        c = slice(h * tq, (h + 1) * tq)
        outs.append(gt[3 * h:3 * h + 1] * o_c[:, c] + gt[3 * h + 1:3 * h + 2] * o_s[:, c]
                    + gt[3 * h + 2:3 * h + 3] * o_w[:, c])
    o_ref[...] = jnp.concatenate(outs, axis=0).T.astype(BF16)


def _anchored(kernel_fn, n_inputs):
    def body(*refs, **static):
        kernel_fn(*refs[:n_inputs], *refs[n_inputs + 1:], **static)
    return body


def _nsa(nq, nqr, gates, kcmp, vcmp, ks, vs, kw, vw, batch, seq, after=None):
    n = batch * seq
    tq = 512
    tk = 512 if seq % 512 == 0 else seq
    nqb = seq // tq
    n_sel = seq // SEL_LEN
    rows_c = seq // CMP_STRIDE
    gw = NSA_HPG * NSA_DIM
    cs = np.arange(rows_c)[None, :] * CMP_STRIDE
    ss = np.arange(n_sel)[:, None] * SEL_LEN
    n_cmp = (seq - CMP_LEN) // CMP_STRIDE + 1
    ovt = ((cs < ss + SEL_LEN) & (cs + CMP_LEN > ss) & (np.arange(rows_c)[None, :] < n_cmp))
    ovt = jnp.asarray(ovt.astype(np.float32), BF16)

    qspec = pl.BlockSpec((tq, gw), lambda b, g, i: (b * nqb + i, g))
    cspec = pl.BlockSpec((None, None, rows_c, NSA_DIM), lambda b, g, i: (b, g, 0, 0))
    kspec = pl.BlockSpec((None, seq, NSA_DIM), lambda b, g, i: (g, b, 0))
    args = [nq, nqr, gates, kcmp, vcmp, ovt, ks, vs, kw, vw]
    in_specs = [qspec, qspec,
                pl.BlockSpec((None, GATE_LANES, tq), lambda b, g, i: (g, 0, b * nqb + i)),
                cspec, cspec, pl.BlockSpec(ovt.shape, lambda b, g, i: (0, 0)),
                kspec, kspec, kspec, kspec]
    body = _nsa_kernel
    if after is not None:
        body = _anchored(_nsa_kernel, len(args))
        args.append(after)
        in_specs.append(pl.BlockSpec(memory_space=pl.ANY))
    body = functools.partial(body, tq=tq, tk=tk, seq=seq)
    return pl.pallas_call(
        body,
        grid=(batch, NSA_KV_GROUPS, nqb),
        in_specs=in_specs,
        out_specs=qspec,
        out_shape=jax.ShapeDtypeStruct((n, NSA_WIDTH), BF16),
        scratch_shapes=[pltpu.VMEM((NSA_DIM + SUM_ROWS, seq), BF16), pltpu.VMEM((NSA_DIM + SUM_ROWS, seq), BF16),
                        pltpu.VMEM((NSA_DIM, rows_c), BF16), pltpu.VMEM((n_sel, tq), F32)],
        compiler_params=_params(3),
        name="nsa",
    )(*args)


def _memkv_kernel(mem_ref, w_ref, kv_ref):
    kv_ref[...] = _dot(mem_ref[...].astype(BF16), w_ref[...]).astype(BF16)


def _memkv(mem2d, w_xkv):
    n = mem2d.shape[0]
    w = w_xkv.astype(BF16)
    return pl.pallas_call(
        _memkv_kernel,
        grid=(n // MEM_LEN,),
        in_specs=[pl.BlockSpec((MEM_LEN, D_MODEL), lambda i: (i, 0)),
                  pl.BlockSpec(w.shape, lambda i: (0, 0))],
        out_specs=pl.BlockSpec((MEM_LEN, 2 * D_MODEL), lambda i: (i, 0)),
        out_shape=jax.ShapeDtypeStruct((n, 2 * D_MODEL), BF16),
        compiler_params=_params(1),
        name="memkv",
    )(mem2d, w)


def _pack_halves(v):
    half = D_MODEL // 2
    hi = pltpu.bitcast(v[:, :half].astype(BF16).astype(F32), jnp.uint32)
    lo = pltpu.bitcast(v[:, half:].astype(BF16).astype(F32), jnp.uint32)
    return hi | (lo >> 16)


def _unpack_halves(words):
    return pltpu.bitcast(words & jnp.uint32(0xFFFF0000), F32), pltpu.bitcast(words << 16, F32)


def _postmix_kernel(x_ref, oret_ref, onsa_ref, kv_ref, wout_ref, wq_ref, wo_ref,
                    g1_ref, b1_ref, g2_ref, b2_ref, x2_ref, x2p_ref):
    mixed = jnp.concatenate([oret_ref[...], onsa_ref[...]], axis=1)
    x1 = _layer_norm(DN_ALPHA * x_ref[...] + _dot(mixed, wout_ref[...]), g1_ref[...], b1_ref[...])
    q = (_dot(x1.astype(BF16), wq_ref[...]) * (XATT_DIM ** -0.5)).astype(BF16)
    heads = []
    for h in range(XATT_HEADS):
        cols = slice(h * XATT_DIM, (h + 1) * XATT_DIM)
        s = _dot_nt(q[:, cols], kv_ref[:, cols])
        m = jnp.max(s, axis=-1, keepdims=True)
        p = jnp.exp(s - m)
        l = jnp.sum(p, axis=-1, keepdims=True)
        heads.append(_dot(p.astype(BF16), kv_ref[:, D_MODEL + h * XATT_DIM:D_MODEL + (h + 1) * XATT_DIM]) / l)
    att = jnp.concatenate(heads, axis=1).astype(BF16)
    x2 = _layer_norm(DN_ALPHA * x1 + _dot(att, wo_ref[...]), g2_ref[...], b2_ref[...])
    x2_ref[...] = x2
    x2p_ref[...] = _pack_halves(x2)


def _postmix(x2d, o_ret, o_nsa, kvx, w_out, w_xq, w_xo, ln1_g, ln1_b, ln2_g, ln2_b, batch0, batch, seq,
             after=None):
    n = batch * seq
    tm = 512 if seq % 512 == 0 else seq
    per_b = seq // tm
    row = lambda w: pl.BlockSpec((tm, w), lambda b, i: (b * per_b + i, 0))
    full = lambda a: pl.BlockSpec(a.shape, lambda b, i: (0,) * a.ndim)
    ws = [w_out.astype(BF16), w_xq.astype(BF16), w_xo.astype(BF16)]
    vecs = [v.reshape(1, D_MODEL) for v in (ln1_g, ln1_b, ln2_g, ln2_b)]
    args = [x2d, o_ret, o_nsa, kvx, *ws, *vecs]
    in_specs = ([pl.BlockSpec((tm, D_MODEL), lambda b, i: ((batch0 + b) * per_b + i, 0)),
                 row(RET_WIDTH), row(NSA_WIDTH),
                 pl.BlockSpec((MEM_LEN, 2 * D_MODEL), lambda b, i: (batch0 + b, 0))]
                + [full(w) for w in ws] + [full(v) for v in vecs])
    body = _postmix_kernel
    if after is not None:
        body = _anchored(_postmix_kernel, len(args))
        args.append(after)
        in_specs.append(pl.BlockSpec(memory_space=pl.ANY))
    return pl.pallas_call(
        body,
        grid=(batch, per_b),
        in_specs=in_specs,
        out_specs=[row(D_MODEL),
                   row(D_MODEL // 2)],
        out_shape=[jax.ShapeDtypeStruct((n, D_MODEL), F32),
                   jax.ShapeDtypeStruct((n, D_MODEL // 2), jnp.uint32)],
        compiler_params=_params(2),
        name="postmix",
    )(*args)


def _router_kernel(x_ref, wr_ref, bias_ref, e_ref, rank_ref, w_ref, cnt_ref, cntrow_ref, carry_ref, carryrow_ref):
    tn = x_ref.shape[0]
    per = N_EXPERTS // N_GROUPS

    @pl.when(pl.program_id(0) == 0)
    def _():
        carry_ref[...] = jnp.zeros_like(carry_ref)
        carryrow_ref[...] = jnp.zeros_like(carryrow_ref)

    logits = _dot_nt(wr_ref[...], x_ref[...].astype(BF16))
    scores = jax.nn.sigmoid(logits)
    biased = scores + bias_ref[...]
    b3 = biased.reshape(N_GROUPS, per, tn)
    member = lax.broadcasted_iota(jnp.int32, (N_GROUPS, per, tn), 1)
    top1 = jnp.max(b3, axis=1, keepdims=True)
    first1 = jnp.min(jnp.where(b3 == top1, member, per), axis=1, keepdims=True)
    top2 = jnp.max(jnp.where(member == first1, -jnp.inf, b3), axis=1, keepdims=True)
    gscore = top1 + top2
    gidx = lax.broadcasted_iota(jnp.int32, (N_GROUPS, 1, tn), 0)
    gwork = gscore
    for _ in range(TOPK_GROUPS - 1):
        gbest = jnp.max(gwork, axis=0, keepdims=True)
        gfirst = jnp.min(jnp.where(gwork == gbest, gidx, N_GROUPS), axis=0, keepdims=True)
        gwork = jnp.where(gidx == gfirst, -jnp.inf, gwork)
    kth = jnp.max(gwork, axis=0, keepdims=True)
    work = jnp.where(gscore >= kth, b3, NEG).reshape(N_EXPERTS, tn)
    eidx = lax.broadcasted_iota(jnp.int32, (N_EXPERTS, tn), 0)
    picks = []
    chosen = jnp.zeros((N_EXPERTS, tn), F32)
    for _ in range(TOP_K):
        best = jnp.max(work, axis=0, keepdims=True)
        first = jnp.min(jnp.where(work == best, eidx, N_EXPERTS), axis=0, keepdims=True)
        hit = eidx == first
        picks.append((first, hit))
        chosen = jnp.where(hit, 1.0, chosen)
        work = jnp.where(hit, -jnp.inf, work)

    r_i = lax.broadcasted_iota(jnp.int32, (tn, tn), 0)
    c_i = lax.broadcasted_iota(jnp.int32, (tn, tn), 1)
    before = jnp.where(r_i < c_i, 1.0, 0.0).astype(BF16)
    chosen_b = chosen.astype(BF16)
    rank = _dot(chosen_b, before) + carry_ref[...]
    carry_ref[...] = carry_ref[...] + jnp.sum(chosen, axis=1, keepdims=True)
    carryrow_ref[...] = carryrow_ref[...] + _dot_nt(jnp.ones((8, tn), BF16), chosen_b)
    cnt_ref[...] = carry_ref[...]
    cntrow_ref[...] = carryrow_ref[...]

    wsel = [jnp.sum(jnp.where(hit, scores, 0.0), axis=0, keepdims=True) for _, hit in picks]
    wsum = wsel[0]
    for v in wsel[1:]:
        wsum = wsum + v
    for kk, (first, hit) in enumerate(picks):
        e_ref[kk:kk + 1, :] = first
        rank_ref[kk:kk + 1, :] = jnp.sum(jnp.where(hit, rank, 0.0), axis=0, keepdims=True).astype(jnp.int32)
        w_ref[kk:kk + 1, :] = wsel[kk] / wsum * ROUTED_SCALE


def _router(x2, w_router, router_bias):
    n = x2.shape[0]
    tn = 512 if n % 512 == 0 else n
    wr_t = w_router.T.astype(BF16)
    bias = router_bias.reshape(N_EXPERTS, 1).astype(F32)
    kspec = pl.BlockSpec((TOP_K, tn), lambda i: (0, i))
    return pl.pallas_call(
        _router_kernel,
        grid=(n // tn,),
        in_specs=[pl.BlockSpec((tn, D_MODEL), lambda i: (i, 0)),
                  pl.BlockSpec(wr_t.shape, lambda i: (0, 0)),
                  pl.BlockSpec(bias.shape, lambda i: (0, 0))],
        out_specs=[kspec, kspec, kspec, pl.BlockSpec((N_EXPERTS, 1), lambda i: (0, 0)),
                   pl.BlockSpec((8, N_EXPERTS), lambda i: (0, 0))],
        out_shape=[jax.ShapeDtypeStruct((TOP_K, n), jnp.int32),
                   jax.ShapeDtypeStruct((TOP_K, n), jnp.int32),
                   jax.ShapeDtypeStruct((TOP_K, n), F32),
                   jax.ShapeDtypeStruct((N_EXPERTS, 1), F32),
                   jax.ShapeDtypeStruct((8, N_EXPERTS), F32)],
        scratch_shapes=[pltpu.VMEM((N_EXPERTS, 1), F32), pltpu.VMEM((8, N_EXPERTS), F32)],
        compiler_params=_params(1),
        name="router",
    )(x2, wr_t, bias)


def _slots_kernel(e_ref, rank_ref, cnt_ref, cntrow_ref, dest_ref, blk_e_ref, valid_ref, *, blk, n_blocks):
    pad = lambda c: jnp.ceil(c / blk) * blk
    cnt = cnt_ref[...]
    padded = pad(cnt)
    padded_row = pad(cntrow_ref[0:1, :])
    r_i = lax.broadcasted_iota(jnp.int32, (N_EXPERTS, N_EXPERTS), 0)
    c_i = lax.broadcasted_iota(jnp.int32, (N_EXPERTS, N_EXPERTS), 1)
    start = jnp.sum(jnp.where(c_i < r_i, padded_row, 0.0), axis=1, keepdims=True)
    end = start + padded
    e = e_ref[...]
    dest = rank_ref[...]
    for ex in range(N_EXPERTS):
        dest = dest + jnp.where(e == ex, start[ex:ex + 1, :].astype(jnp.int32), 0)
    dest_ref[...] = dest
    bstart = (lax.broadcasted_iota(jnp.int32, (1, n_blocks), 1) * blk).astype(F32)
    owner = jnp.sum(jnp.where(end <= bstart, 1.0, 0.0), axis=0, keepdims=True)
    blk_e_ref[...] = jnp.minimum(owner, N_EXPERTS - 1.0).astype(jnp.int32)
    inside = (start <= bstart) & (bstart < end)
    real = jnp.clip(start + cnt - bstart, 0.0, float(blk))
    valid_ref[...] = jnp.sum(jnp.where(inside, real, 0.0), axis=0, keepdims=True).astype(jnp.int32)


def _slots(e_k, rank_k, counts, counts_row, blk, n_blocks):
    n = e_k.shape[1]
    full = lambda shape: pl.BlockSpec(shape, lambda: (0,) * len(shape))
    return pl.pallas_call(
        functools.partial(_slots_kernel, blk=blk, n_blocks=n_blocks),
        in_specs=[full((TOP_K, n)), full((TOP_K, n)), full((N_EXPERTS, 1)), full((8, N_EXPERTS))],
        out_specs=[full((TOP_K, n)), full((1, n_blocks)), full((1, n_blocks))],
        out_shape=[jax.ShapeDtypeStruct((TOP_K, n), jnp.int32),
                   jax.ShapeDtypeStruct((1, n_blocks), jnp.int32),
                   jax.ShapeDtypeStruct((1, n_blocks), jnp.int32)],
        compiler_params=pltpu.CompilerParams(vmem_limit_bytes=VMEM_LIMIT),
        name="slots",
    )(e_k, rank_k, counts, counts_row)


def _sc_worker_base(per_worker):
    return (lax.axis_index("s") * SC_CORES + lax.axis_index("c")) * per_worker


def _sc_scatter_rows(rows, idx, n_out):
    n, width = rows.shape
    k_lists = idx.shape[0] // n
    workers = SC_CORES * SC_SUBCORES
    per_worker = n // workers
    assert per_worker * workers == n and per_worker % SC_CHUNK == 0
    mesh = plsc.VectorSubcoreMesh(core_axis_name="c", subcore_axis_name="s")

    @functools.partial(
        pl.kernel, mesh=mesh,
        out_type=jax.ShapeDtypeStruct((n_out, width), rows.dtype),
        scratch_types=[pltpu.VMEM((SC_CHUNK, width), rows.dtype)]
                      + [pltpu.VMEM((SC_CHUNK,), jnp.int32)] * k_lists + [pltpu.SemaphoreType.DMA] * 3,
        name="sc_scatter")
    def scatter(rows_hbm, idx_hbm, out_hbm, rows_v, *rest):
        idx_vs = rest[:k_lists]
        sem_rows, sem_idx, sem_out = rest[k_lists:]
        base = _sc_worker_base(per_worker)

        @pl.loop(0, per_worker // SC_CHUNK)
        def _(ci):
            off = pl.multiple_of(base + ci * SC_CHUNK, SC_CHUNK)
            loads = [pltpu.async_copy(rows_hbm.at[pl.ds(off, SC_CHUNK)], rows_v, sem_rows)]
            loads += [pltpu.async_copy(idx_hbm.at[pl.ds(pl.multiple_of(k * n + off, SC_CHUNK), SC_CHUNK)],
                                       idx_vs[k], sem_idx) for k in range(k_lists)]
            for c in loads:
                c.wait()
            copies = [pltpu.async_copy(rows_v, out_hbm.at[idx_vs[k]], sem_out) for k in range(k_lists)]
            for c in copies:
                c.wait()

    return scatter(rows, idx)


def _experts_kernel(blk_e_ref, valid_ref, xs_ref, wg_ref, wu_ref, wd_ref, y_ref, wg_b, wu_b, wd_b):
    i = pl.program_id(0)
    valid = valid_ref[i]

    @pl.when((i == 0) | (blk_e_ref[i] != blk_e_ref[jnp.maximum(i - 1, 0)]))
    def _():
        wg_b[...] = wg_ref[...].astype(BF16)
        wu_b[...] = wu_ref[...].astype(BF16)
        wd_b[...] = wd_ref[...].astype(BF16)

    @pl.when(valid > 0)
    def _():
        half = D_MODEL // 2
        row = lax.broadcasted_iota(jnp.int32, (xs_ref.shape[0], 1), 0)
        hi, lo = (v.astype(BF16) for v in _unpack_halves(jnp.where(row < valid, xs_ref[...], jnp.uint32(0))))
        gate = _dot(hi, wg_b[:half, :]) + _dot(lo, wg_b[half:, :])
        up = _dot(hi, wu_b[:half, :]) + _dot(lo, wu_b[half:, :])
        y_ref[...] = _pack_halves(_dot((jax.nn.silu(gate) * up).astype(BF16), wd_b[...]))

    @pl.when(valid <= 0)
    def _():
        y_ref[...] = jnp.zeros_like(y_ref)


def _experts(blk_e, valid, xs, w_gate, w_up, w_down, blk, after=None):
    cap, width = xs.shape
    wspec = lambda a: pl.BlockSpec((None,) + a.shape[1:], lambda i, be, nv: (be[i], 0, 0))
    rows = pl.BlockSpec((blk, width), lambda i, be, nv: (i, 0))
    args = [blk_e, valid, xs, w_gate, w_up, w_down]
    in_specs = [rows, wspec(w_gate), wspec(w_up), wspec(w_down)]
    body = _experts_kernel
    if after is not None:
        body = _anchored(_experts_kernel, len(args))
        args.append(after)
        in_specs.append(pl.BlockSpec(memory_space=pl.ANY))
    return pl.pallas_call(
        body,
        grid_spec=pltpu.PrefetchScalarGridSpec(
            num_scalar_prefetch=2,
            grid=(cap // blk,),
            in_specs=in_specs,
            out_specs=rows,
            scratch_shapes=[pltpu.VMEM(w.shape[1:], BF16) for w in (w_gate, w_up, w_down)],
        ),
        out_shape=jax.ShapeDtypeStruct(xs.shape, xs.dtype),
        compiler_params=_params(1),
        name="experts",
    )(*args)


def _sc_gather_rows(table, idx):
    b, width = idx.shape[0], table.shape[1]
    workers = SC_CORES * SC_SUBCORES
    per_worker = b // workers
    assert per_worker * workers == b and per_worker % (SC_CHUNK * SC_INFLIGHT) == 0
    mesh = plsc.VectorSubcoreMesh(core_axis_name="c", subcore_axis_name="s")

    @functools.partial(
        pl.kernel, mesh=mesh,
        out_type=jax.ShapeDtypeStruct((b, width), table.dtype),
        scratch_types=[pltpu.VMEM((SC_CHUNK,), jnp.int32)] * SC_INFLIGHT
                      + [pltpu.VMEM((SC_CHUNK, width), table.dtype)] * SC_INFLIGHT
                      + [pltpu.SemaphoreType.DMA] * (1 + 2 * SC_INFLIGHT),
        name="sc_gather")
    def gather(table_hbm, idx_hbm, out_hbm, *scratch):
        idx_vs = scratch[:SC_INFLIGHT]
        rows_vs = scratch[SC_INFLIGHT:2 * SC_INFLIGHT]
        sem_idx = scratch[2 * SC_INFLIGHT]
        sem_rows = scratch[2 * SC_INFLIGHT + 1:3 * SC_INFLIGHT + 1]
        sem_out = scratch[3 * SC_INFLIGHT + 1:]
        base = _sc_worker_base(per_worker)
        lanes = range(SC_INFLIGHT)

        @pl.loop(0, per_worker // (SC_CHUNK * SC_INFLIGHT))
        def _(gi):
            offs = [pl.multiple_of(base + (gi * SC_INFLIGHT + j) * SC_CHUNK, SC_CHUNK) for j in lanes]
            loads = [pltpu.async_copy(idx_hbm.at[pl.ds(offs[j], SC_CHUNK)], idx_vs[j], sem_idx) for j in lanes]
            for c in loads:
                c.wait()
            gathers = [pltpu.async_copy(table_hbm.at[idx_vs[j]], rows_vs[j], sem_rows[j]) for j in lanes]
            writes = []
            for j in lanes:
                gathers[j].wait()
                writes.append(pltpu.async_copy(rows_vs[j], out_hbm.at[pl.ds(offs[j], SC_CHUNK)], sem_out[j]))
            for c in writes:
                c.wait()

    return gather(table, idx)


def _combine_kernel(x_ref, wk_ref, yk_ref, wsg_ref, wsu_ref, wsd_ref, g_ref, b_ref, *rest):
    o_ref = rest[-1]
    x = x_ref[...]
    xb = x.astype(BF16)
    shared = _dot((jax.nn.silu(_dot(xb, wsg_ref[...])) * _dot(xb, wsu_ref[...])).astype(BF16), wsd_ref[...])
    wk = wk_ref[...]
    routed_hi = routed_lo = None
    for kk in range(TOP_K):
        hi, lo = _unpack_halves(yk_ref[kk])
        w = wk[:, kk:kk + 1]
        routed_hi = hi * w if kk == 0 else routed_hi + hi * w
        routed_lo = lo * w if kk == 0 else routed_lo + lo * w
    routed = jnp.concatenate([routed_hi, routed_lo], axis=1)
    o_ref[...] = _layer_norm(DN_ALPHA * x + (routed + shared), g_ref[...], b_ref[...])


def _combine(x2, w_tok, yk, ws_gate, ws_up, ws_down, ln3_g, ln3_b, row0, n_total, out_prev, after=None):
    n = x2.shape[0]
    tt = 512 if n % 512 == 0 and row0 % 512 == 0 else n
    blk0 = row0 // tt
    ws = [ws_gate.astype(BF16), ws_up.astype(BF16), ws_down.astype(BF16)]
    vecs = [ln3_g.reshape(1, D_MODEL), ln3_b.reshape(1, D_MODEL)]
    full = lambda a: pl.BlockSpec(a.shape, lambda i: (0,) * a.ndim)
    args = [x2, w_tok, yk, *ws, *vecs]
    in_specs = ([pl.BlockSpec((tt, D_MODEL), lambda i: (i, 0)),
                 pl.BlockSpec((tt, TOP_K), lambda i: (i, 0)),
                 pl.BlockSpec((TOP_K, tt, D_MODEL // 2), lambda i: (0, i, 0))]
                + [full(a) for a in ws] + [full(v) for v in vecs])
    aliases = {}
    if after is not None:
        args.append(after)
        in_specs.append(pl.BlockSpec(memory_space=pl.ANY))
    if out_prev is not None:
        aliases = {len(args): 0}
        args.append(out_prev)
        in_specs.append(pl.BlockSpec(memory_space=pl.ANY))
    return pl.pallas_call(
        _combine_kernel,
        grid=(n // tt,),
        in_specs=in_specs,
        out_specs=pl.BlockSpec((tt, D_MODEL), lambda i: (blk0 + i, 0)),
        out_shape=jax.ShapeDtypeStruct((n_total, D_MODEL), F32),
        input_output_aliases=aliases,
        compiler_params=_params(1),
        name="combine",
    )(*args)


EXPERT_BLOCK = 512


def _moe_dispatch(x2, x2p, w_router, router_bias):
    n = x2.shape[0]
    cap = n * TOP_K + N_EXPERTS * EXPERT_BLOCK
    e_k, rank_k, w_k, counts, counts_row = _router(x2, w_router, router_bias)
    dest, blk_e, valid = _slots(e_k, rank_k, counts, counts_row, EXPERT_BLOCK, cap // EXPERT_BLOCK)
    dest = dest.reshape(-1)
    return dict(w_tok=w_k.T, dest=dest, blk_e=blk_e.reshape(-1), valid=valid.reshape(-1),
                xs=_sc_scatter_rows(x2p, dest, cap))


def _moe_experts(routed, w_gate, w_up, w_down, after):
    y = _experts(routed["blk_e"], routed["valid"], routed["xs"], w_gate, w_up, w_down, EXPERT_BLOCK, after=after)
    n = routed["dest"].shape[0] // TOP_K
    return y, _sc_gather_rows(y, routed["dest"]).reshape(TOP_K, n, D_MODEL // 2)


def _layer(x, mem, positions, w_in, cmp_pe_k, cmp_pe_v, cmp_w1_k, cmp_w2_k, cmp_w1_v, cmp_w2_v,
           w_out, ln1_g, ln1_b, w_xq, w_xkv, w_xo, ln2_g, ln2_b, w_router, router_bias,
           w_gate, w_up, w_down, ws_gate, ws_up, ws_down, ln3_g, ln3_b):
    batch, seq, _ = x.shape
    n_total = batch * seq
    x2d = x.reshape(n_total, D_MODEL)
    pos_col = positions.astype(F32).reshape(n_total, 1)
    kvx = _memkv(mem.reshape(batch * MEM_LEN, D_MODEL), w_xkv)
    groups = BATCH_GROUPS if batch % BATCH_GROUPS == 0 else 1
    per = batch // groups
    n = per * seq

    def mixers_in(gi):
        (rq, rk, rv, rg, nq, nqr, kc, vc, ks, vs, kw, vw, gates) = _inproj(x2d, pos_col, w_in, gi * n, n)
        o_ret = _retention(rq, rk, rv, rg, per, seq)
        kcmp = _compress(kc, cmp_pe_k, cmp_w1_k, cmp_w2_k, per, seq)
        vcmp = _compress(vc, cmp_pe_v, cmp_w1_v, cmp_w2_v, per, seq)
        return o_ret, (nq, nqr, gates, kcmp, vcmp, ks, vs, kw, vw)

    def attend_and_route(gi, o_ret, nsa_args, after):
        o_nsa = _nsa(*nsa_args, per, seq, after=after)
        x2, x2p = _postmix(x2d, o_ret, o_nsa, kvx, w_out, w_xq, w_xo, ln1_g, ln1_b, ln2_g, ln2_b,
                           gi * per, per, seq)
        return x2, _moe_dispatch(x2, x2p, w_router, router_bias)

    def combine(gi, x2, routed, yk, out_prev, after):
        return _combine(x2, routed["w_tok"], yk, ws_gate, ws_up, ws_down, ln3_g, ln3_b, gi * n, n_total,
                        out_prev, after=after)

    out = None
    o_ret, nsa_args = mixers_in(0)
    x2, routed = attend_and_route(0, o_ret, nsa_args, None)
    for gi in range(1, groups):
        o_ret, nsa_args = mixers_in(gi)
        y, yk = _moe_experts(routed, w_gate, w_up, w_down, after=o_ret)
        x2_next, routed_next = attend_and_route(gi, o_ret, nsa_args, y)
        out = combine(gi - 1, x2, routed, yk, out, routed_next["dest"])
        x2, routed = x2_next, routed_next
    y, yk = _moe_experts(routed, w_gate, w_up, w_down, after=None)
    out = combine(groups - 1, x2, routed, yk, out, None)
    return out.reshape(batch, seq, D_MODEL)


def kernel(x, mem, positions, w_in, cmp_pe_k, cmp_pe_v, cmp_w1_k, cmp_w2_k, cmp_w1_v, cmp_w2_v, w_out, ln1_g, ln1_b, w_xq, w_xkv, w_xo, ln2_g, ln2_b, w_router, router_bias, w_gate, w_up, w_down, ws_gate, ws_up, ws_down, ln3_g, ln3_b):
    for l in range(DEPTH):
        x = _layer(x, mem, positions, w_in[l], cmp_pe_k[l], cmp_pe_v[l], cmp_w1_k[l], cmp_w2_k[l],
                   cmp_w1_v[l], cmp_w2_v[l], w_out[l], ln1_g[l], ln1_b[l], w_xq[l], w_xkv[l],
                   w_xo[l], ln2_g[l], ln2_b[l], w_router[l], router_bias[l], w_gate[l], w_up[l],
                   w_down[l], ws_gate[l], ws_up[l], ws_down[l], ln3_g[l], ln3_b[l])
    return x
```

```python
import functools

import numpy as np
import jax
import jax.numpy as jnp
from jax import lax
from jax.experimental import pallas as pl
from jax.experimental.pallas import tpu as pltpu
from jax.experimental.pallas import tpu_sc as plsc

D_MODEL = 1024
MEM_LEN = 256
DEPTH = 1
DN_ALPHA = (2 * DEPTH) ** 0.25
LN_EPS = 1e-5
NEG = -1e30
FORCE = 1e9

RET_HEADS = 4
RET_DIM = 128
RET_CHUNK = 128
RET_ROPE_BASE = 10000.0
RET_STEP_CHUNKS = 4
RET_WIDTH = RET_HEADS * RET_DIM

NSA_HEADS = 8
NSA_KV_GROUPS = 2
NSA_HPG = NSA_HEADS // NSA_KV_GROUPS
NSA_DIM = 64
NSA_WIDTH = NSA_HEADS * NSA_DIM
KV_WIDTH = NSA_KV_GROUPS * NSA_DIM
CMP_LEN = 32
CMP_STRIDE = 16
CMP_HIDDEN = 256
SEL_LEN = 64
SEL_SHIFT = 6
SEL_TOPK = 16
WIN = 512
ROPE_THETA = 500000.0
ROPE_DIMS = NSA_DIM // 4
GATE_LANES = 16
NSA_CHAINS = 1
SUM_ROWS = 16
WIN_PART = 256

BATCH_GROUPS = 2

SC_CORES = 2
SC_SUBCORES = 16
SC_CHUNK = 64
SC_INFLIGHT = 2

XATT_HEADS = 4
XATT_DIM = D_MODEL // XATT_HEADS

N_EXPERTS = 64
TOP_K = 8
N_GROUPS = 8
TOPK_GROUPS = 4
EXPERT_FF = 256
SHARED_FF = 256
ROUTED_SCALE = 2.5

LANES = 128
VMEM_LIMIT = 56 * 1024 * 1024

F32 = jnp.float32
BF16 = jnp.bfloat16
NT_DIMS = (((1,), (1,)), ((), ()))


def _params(n_axes):
    return pltpu.CompilerParams(dimension_semantics=("arbitrary",) * n_axes,
                                vmem_limit_bytes=VMEM_LIMIT)


def _dot(a, b):
    return jnp.dot(a, b, preferred_element_type=F32)


def _dot_nt(a, b):
    return lax.dot_general(a, b, NT_DIMS, preferred_element_type=F32)


def _layer_norm(v, g, b):
    mu = jnp.mean(v, axis=-1, keepdims=True)
    d = v - mu
    var = jnp.mean(d * d, axis=-1, keepdims=True)
    return d * lax.rsqrt(var + LN_EPS) * g + b


def _inproj_kernel(x_ref, pos_ref, wret_ref, wnq_ref, wkv_ref, wg_ref, invr_ref, invn_ref,
                   rq_ref, rk_ref, rv_ref, rg_ref, nq_ref, nqr_ref, kc_ref, vc_ref,
                   ks_ref, vs_ref, kw_ref, vw_ref, gate_ref):
    xb = x_ref[...].astype(BF16)
    pos = pos_ref[...]
    lane = lax.broadcasted_iota(jnp.int32, (1, LANES), 1)

    ang = pos * invr_ref[...]
    cos_r = jnp.cos(ang)
    sin_r = jnp.sin(ang)
    sin_r = jnp.where(lane < RET_DIM // 2, -sin_r, sin_r)
    q_all = _dot(xb, wret_ref[:, :RET_WIDTH])
    k_all = _dot(xb, wret_ref[:, RET_WIDTH:2 * RET_WIDTH])
    for h in range(RET_HEADS):
        cols = slice(h * RET_DIM, (h + 1) * RET_DIM)
        q = q_all[:, cols]
        rq_ref[:, cols] = (q * cos_r + pltpu.roll(q, RET_DIM // 2, 1) * sin_r).astype(BF16)
        k = k_all[:, cols]
        k = (k * cos_r + pltpu.roll(k, RET_DIM // 2, 1) * sin_r) * (RET_DIM ** -0.5)
        rk_ref[:, cols] = k.astype(BF16)
    rv_ref[...] = _dot(xb, wret_ref[:, 2 * RET_WIDTH:3 * RET_WIDTH]).astype(BF16)
    rg_ref[...] = _dot(xb, wret_ref[:, 3 * RET_WIDTH:4 * RET_WIDTH]).astype(BF16)

    half = ROPE_DIMS // 2
    j = lane % NSA_DIM
    angn = pos * invn_ref[...]
    cos_n = jnp.cos(angn)
    sin_n = jnp.sin(angn)
    sin_lo = jnp.where(j < half, -sin_n, 0.0)
    sin_hi = jnp.where((j >= half) & (j < 2 * half), sin_n, 0.0)

    def rope_n(v):
        return v * cos_n + pltpu.roll(v, half, 1) * sin_hi + pltpu.roll(v, LANES - half, 1) * sin_lo

    scale = NSA_DIM ** -0.5
    nq_all = _dot(xb, wnq_ref[...])
    for c in range(NSA_WIDTH // LANES):
        cols = slice(c * LANES, (c + 1) * LANES)
        q = nq_all[:, cols]
        nq_ref[:, cols] = (q * scale).astype(BF16)
        nqr_ref[:, cols] = (rope_n(q) * scale).astype(BF16)

    kv_all = _dot(xb, wkv_ref[...])

    def kv(i):
        return kv_all[:, i * KV_WIDTH:(i + 1) * KV_WIDTH]

    def split_groups(ref, v):
        for g in range(NSA_KV_GROUPS):
            ref[g] = v[:, g * NSA_DIM:(g + 1) * NSA_DIM].astype(BF16)

    kc_ref[...] = kv(0)
    vc_ref[...] = kv(1)
    split_groups(ks_ref, rope_n(kv(2)))
    split_groups(vs_ref, kv(3))
    split_groups(kw_ref, rope_n(kv(4)))
    split_groups(vw_ref, kv(5))

    gt = jax.nn.sigmoid(_dot_nt(wg_ref[...], xb))
    for g in range(NSA_KV_GROUPS):
        gate_ref[g] = gt[g * GATE_LANES:(g + 1) * GATE_LANES, :]


def _inproj(x2d, pos_col, w_in, row0, n):
    tm = 512 if n % 512 == 0 and row0 % 512 == 0 else n
    blk0 = row0 // tm
    off = np.cumsum([0] + [RET_WIDTH] * 4 + [NSA_WIDTH] + [KV_WIDTH] * 6)
    w_ret = w_in[:, :off[4]].astype(BF16)
    w_nq = w_in[:, off[4]:off[5]].astype(BF16)
    w_kv = w_in[:, off[5]:off[11]].astype(BF16)
    wg = w_in[:, off[11]:].reshape(D_MODEL, NSA_KV_GROUPS, NSA_HPG * 3)
    wg = jnp.pad(wg, ((0, 0), (0, 0), (0, GATE_LANES - NSA_HPG * 3)))
    wg = wg.reshape(D_MODEL, NSA_KV_GROUPS * GATE_LANES).T.astype(BF16)

    lane = np.arange(LANES)
    half_r = RET_DIM // 2
    inv_r = (np.float32(RET_ROPE_BASE) ** (-np.arange(half_r, dtype=np.float32) / np.float32(half_r)))
    inv_r = inv_r.astype(np.float32)[lane % half_r][None, :]
    half_n = ROPE_DIMS // 2
    inv_n = (np.float32(ROPE_THETA) ** (-np.arange(half_n, dtype=np.float32) / np.float32(half_n)))
    jn = lane % NSA_DIM
    inv_n = np.where(jn < ROPE_DIMS, inv_n.astype(np.float32)[jn % half_n], np.float32(0.0))[None, :]

    row = lambda w: pl.BlockSpec((tm, w), lambda i: (i, 0))
    src_row = lambda w: pl.BlockSpec((tm, w), lambda i: (blk0 + i, 0))
    full = lambda a: pl.BlockSpec(a.shape, lambda i: (0,) * a.ndim)
    grp = lambda w: pl.BlockSpec((NSA_KV_GROUPS, tm, w), lambda i: (0, i, 0))
    bf = lambda w: jax.ShapeDtypeStruct((n, w), BF16)
    gbf = jax.ShapeDtypeStruct((NSA_KV_GROUPS, n, NSA_DIM), BF16)
    inv_r = jnp.asarray(inv_r, F32)
    inv_n = jnp.asarray(inv_n, F32)
    return pl.pallas_call(
        _inproj_kernel,
        grid=(n // tm,),
        in_specs=[src_row(D_MODEL), src_row(1), full(w_ret), full(w_nq), full(w_kv), full(wg),
                  full(inv_r), full(inv_n)],
        out_specs=[row(RET_WIDTH)] * 4 + [row(NSA_WIDTH)] * 2 + [row(KV_WIDTH)] * 2
                  + [grp(NSA_DIM)] * 4
                  + [pl.BlockSpec((NSA_KV_GROUPS, GATE_LANES, tm), lambda i: (0, 0, i))],
        out_shape=[bf(RET_WIDTH)] * 4 + [bf(NSA_WIDTH)] * 2
                  + [jax.ShapeDtypeStruct((n, KV_WIDTH), F32)] * 2 + [gbf] * 4
                  + [jax.ShapeDtypeStruct((NSA_KV_GROUPS, GATE_LANES, n), F32)],
        compiler_params=_params(1),
        name="inproj",
    )(x2d, pos_col, w_ret, w_nq, w_kv, wg, inv_r, inv_n)


def _retention_kernel(q_ref, k_ref, v_ref, g_ref, o_ref, state_ref):
    c = RET_CHUNK

    @pl.when(pl.program_id(1) == 0)
    def _():
        state_ref[...] = jnp.zeros_like(state_ref)

    row = lax.broadcasted_iota(jnp.int32, (c, c), 0)
    col = lax.broadcasted_iota(jnp.int32, (c, c), 1)
    rel = (row - col).astype(F32)
    idx = lax.broadcasted_iota(jnp.int32, (c, 1), 0).astype(F32)
    for h in range(RET_HEADS):
        log_g = float(np.log(np.float32(1.0) - np.float32(2.0) ** np.float32(-5.0 - h)))
        cols = slice(h * RET_DIM, (h + 1) * RET_DIM)
        dmask = jnp.where(rel >= 0, jnp.exp(log_g * jnp.maximum(rel, 0.0)), 0.0)
        zeta = jnp.exp(log_g * (c - 1.0 - idx))
        xi = jnp.exp(log_g * (idx + 1.0))
        for j in range(q_ref.shape[0] // c):
            rows = slice(j * c, (j + 1) * c)
            q = q_ref[rows, cols]
            k = k_ref[rows, cols]
            v = v_ref[rows, cols]
            scores = _dot_nt(q, k) * dmask
            inner = _dot(scores.astype(BF16), v)
            prev = state_ref[h]
            cross = _dot(q, prev.astype(BF16)) * xi
            kz = (k.astype(F32) * zeta).astype(BF16)
            kv = lax.dot_general(kz, v, (((0,), (0,)), ((), ())), preferred_element_type=F32)
            state_ref[h] = prev * float(np.exp(np.float32(log_g) * np.float32(c))) + kv
            o = inner + cross
            mu = jnp.mean(o, axis=-1, keepdims=True)
            d = o - mu
            var = jnp.mean(d * d, axis=-1, keepdims=True)
            o = d * lax.rsqrt(var + LN_EPS)
            o_ref[rows, cols] = (jax.nn.silu(g_ref[rows, cols].astype(F32)) * o).astype(BF16)


def _retention(rq, rk, rv, rg, batch, seq):
    per_step = RET_STEP_CHUNKS if (seq // RET_CHUNK) % RET_STEP_CHUNKS == 0 else 1
    nc = seq // (RET_CHUNK * per_step)
    spec = pl.BlockSpec((RET_CHUNK * per_step, RET_WIDTH), lambda b, n: (b * nc + n, 0))
    return pl.pallas_call(
        _retention_kernel,
        grid=(batch, nc),
        in_specs=[spec] * 4,
        out_specs=spec,
        out_shape=jax.ShapeDtypeStruct(rq.shape, BF16),
        scratch_shapes=[pltpu.VMEM((RET_HEADS, RET_DIM, RET_DIM), F32)],
        compiler_params=_params(2),
        name="retention",
    )(rq, rk, rv, rg)


def _compress_kernel(a_ref, pe_ref, w1_ref, w2_ref, o_ref, shift_ref, *, n_cmp):
    rows = a_ref.shape[0]
    a = a_ref[...]
    lo = (a + pe_ref[0]).astype(BF16)
    hi = (a + pe_ref[1]).astype(BF16)
    ridx = lax.broadcasted_iota(jnp.int32, (rows, 1), 0)
    shift_ref[rows:rows + 8, :] = jnp.zeros((8, CMP_HIDDEN), F32)
    for g in range(NSA_KV_GROUPS):
        p = _dot(lo, w1_ref[0, g])
        shift_ref[0:rows, :] = _dot(hi, w1_ref[1, g])
        hid = jax.nn.silu(p + shift_ref[pl.ds(1, rows), :])
        out = _dot(hid.astype(BF16), w2_ref[...])
        o_ref[g] = jnp.where(ridx < n_cmp, out, 0.0).astype(BF16)


def _compress(a, pe, w1, w2, batch, seq):
    rows = seq // CMP_STRIDE
    per = CMP_STRIDE * KV_WIDTH
    n_cmp = (seq - CMP_LEN) // CMP_STRIDE + 1
    a2 = a.reshape(batch * rows, per)
    pe2 = jnp.tile(pe.reshape(2, CMP_STRIDE, 1, NSA_DIM), (1, 1, NSA_KV_GROUPS, 1)).reshape(2, 1, per)
    w1r = w1.reshape(2, CMP_STRIDE, 1, NSA_DIM, CMP_HIDDEN)
    eye = jnp.eye(NSA_KV_GROUPS, dtype=w1.dtype).reshape(1, NSA_KV_GROUPS, 1, NSA_KV_GROUPS, 1, 1)
    w1x = (w1r[:, None] * eye).reshape(2, NSA_KV_GROUPS, per, CMP_HIDDEN).astype(BF16)
    w2b = w2.astype(BF16)
    full = lambda arr: pl.BlockSpec(arr.shape, lambda b: (0,) * arr.ndim)
    return pl.pallas_call(
        functools.partial(_compress_kernel, n_cmp=n_cmp),
        grid=(batch,),
        in_specs=[pl.BlockSpec((rows, per), lambda b: (b, 0)), full(pe2), full(w1x), full(w2b)],
        out_specs=pl.BlockSpec((None, NSA_KV_GROUPS, rows, NSA_DIM), lambda b: (b, 0, 0, 0)),
        out_shape=jax.ShapeDtypeStruct((batch, NSA_KV_GROUPS, rows, NSA_DIM), BF16),
        scratch_shapes=[pltpu.VMEM((rows + 8, CMP_HIDDEN), F32)],
        compiler_params=_params(1),
        name="compress",
    )(a2, pe2, w1x, w2b)


def _heads_to_lanes(ref):
    vt = ref[...].astype(F32).T
    return jnp.concatenate([vt[h * NSA_DIM:(h + 1) * NSA_DIM] for h in range(NSA_HPG)], axis=1).astype(BF16)


def _tile_heads(v):
    return jnp.concatenate([v] * NSA_HPG, axis=1)


def _transpose_into(dst_ref, src_ref, chunk):
    def step(c, _):
        c0 = pl.multiple_of(c * chunk, chunk)
        dst_ref[:NSA_DIM, pl.ds(c0, chunk)] = src_ref[pl.ds(c0, chunk), :].astype(F32).T.astype(BF16)
        return 0
    lax.fori_loop(0, src_ref.shape[0] // chunk, step, 0)


def _nsa_kernel(qraw_ref, qrot_ref, gate_ref, kcmp_ref, vcmp_ref, ovt_ref,
                ks_ref, vs_ref, kw_ref, vw_ref, o_ref, vst_ref, vwt_ref, vct_ref, bias_ref, *, tq, tk, seq):
    i = pl.program_id(2)
    t0 = i * tq
    cols = NSA_HPG * tq
    n_sel = seq // SEL_LEN
    n_cmp_rows = seq // CMP_STRIDE
    blocks_per_tile = tk // SEL_LEN

    @pl.when(i == 0)
    def _():
        chunk = min(512, n_cmp_rows)
        _transpose_into(vst_ref, vs_ref, chunk)
        _transpose_into(vwt_ref, vw_ref, chunk)
        _transpose_into(vct_ref, vcmp_ref, chunk)
        vst_ref[NSA_DIM:, :] = jnp.ones((SUM_ROWS, seq), BF16)
        vwt_ref[NSA_DIM:, :] = jnp.ones((SUM_ROWS, seq), BF16)

    def split_sum(acc):
        return acc[:NSA_DIM] / acc[NSA_DIM:NSA_DIM + 1]

    q_raw = _heads_to_lanes(qraw_ref)
    q_rot = _heads_to_lanes(qrot_ref)
    t_row = t0 + lax.broadcasted_iota(jnp.int32, (1, tq), 1)

    chain_w = cols // NSA_CHAINS
    heads_per_chain = chain_w // tq
    chains = [slice(c * chain_w, (c + 1) * chain_w) for c in range(NSA_CHAINS)]
    tile_chain = lambda v: jnp.concatenate([v] * heads_per_chain, axis=1)

    pw = min(tq, WIN_PART)
    parts = []
    for u in range(tq // pw):
        span = WIN + pw
        ws = pl.multiple_of(jnp.maximum(t0 + u * pw - WIN, 0), pw)
        dist = t_row[:, u * pw:(u + 1) * pw] - (ws + lax.broadcasted_iota(jnp.int32, (span, 1), 0))
        bias_w = jnp.concatenate([jnp.where((dist >= 0) & (dist < WIN), 0.0, NEG)] * NSA_HPG, axis=1)
        q_part = jnp.concatenate([q_rot[:, h * tq + u * pw:h * tq + (u + 1) * pw] for h in range(NSA_HPG)],
                                 axis=1)
        s_w = _dot(kw_ref[pl.ds(ws, span), :], q_part) + bias_w
        p_w = jnp.exp((s_w - jnp.max(s_w, axis=0, keepdims=True)).astype(BF16))
        parts.append(split_sum(_dot(vwt_ref[:, pl.ds(ws, span)], p_w)))
    o_w = jnp.concatenate([parts[u][:, h * pw:(h + 1) * pw]
                           for h in range(NSA_HPG) for u in range(tq // pw)], axis=1)

    c_idx = lax.broadcasted_iota(jnp.int32, (n_cmp_rows, 1), 0)
    valid = tile_chain(jnp.where(c_idx * CMP_STRIDE + (CMP_LEN - 1) <= t_row, 1.0, 0.0))
    bias_c = (valid - 1.0) * (-NEG)
    o_c = []
    p_sum = None
    for c in chains:
        s_c = _dot(kcmp_ref[...], q_raw[:, c]) + bias_c
        e_c = jnp.exp(s_c - jnp.max(s_c, axis=0, keepdims=True)) * valid
        l_c = jnp.sum(e_c, axis=0, keepdims=True)
        p_c = e_c / jnp.where(l_c > 0.0, l_c, 1.0)
        o_c.append(_dot(vct_ref[...], p_c.astype(BF16)))
        for h in range(heads_per_chain):
            p_h = p_c[:, h * tq:(h + 1) * tq]
            p_sum = p_h if p_sum is None else p_sum + p_h
    o_c = jnp.concatenate(o_c, axis=1)

    p_hi = p_sum.astype(BF16)
    p_lo = (p_sum - p_hi.astype(F32)).astype(BF16)
    ovt = ovt_ref[...]
    imp = _dot(ovt, p_hi) + _dot(ovt, p_lo)
    jb = lax.broadcasted_iota(jnp.int32, (n_sel, tq), 0)
    cur = (t0 + lax.broadcasted_iota(jnp.int32, (n_sel, tq), 1)) >> SEL_SHIFT
    forced = (jb == 0) | (jb == cur) | (jb == cur - 1)
    work = jnp.where(forced, FORCE, imp)
    work = jnp.where(jb <= cur, work, NEG)
    sel_t = jnp.zeros((n_sel, tq), F32)
    for _ in range(min(SEL_TOPK, n_sel)):
        best = jnp.max(work, axis=0, keepdims=True)
        first = jnp.min(jnp.where(work == best, jb, n_sel), axis=0, keepdims=True)
        hit = jb == first
        sel_t = jnp.where(hit, 1.0, sel_t)
        work = jnp.where(hit, -jnp.inf, work)
    bias_ref[...] = jnp.where(sel_t > 0.5, 0.0, NEG)

    def sel_tile(kt, carry, causal):
        k0 = pl.multiple_of(kt * tk, tk)
        bias = jnp.concatenate(
            [jnp.broadcast_to(bias_ref[pl.ds(kt * blocks_per_tile + j, 1), :], (SEL_LEN, tq))
             for j in range(blocks_per_tile)], axis=0)
        if causal:
            kpos = k0 + lax.broadcasted_iota(jnp.int32, (tk, 1), 0)
            bias = jnp.where(kpos <= t_row, bias, NEG)
        bias = tile_chain(bias)
        k_t = ks_ref[pl.ds(k0, tk), :]
        v_t = vst_ref[:, pl.ds(k0, tk)]
        out = []
        scores = [_dot(k_t, q_rot[:, c]) + bias for c in chains]
        for (m, acc), s in zip(carry, scores):
            m_new = jnp.maximum(m, jnp.max(s, axis=0, keepdims=True))
            p = jnp.exp((s - m_new).astype(BF16))
            acc = jnp.exp(m - m_new) * acc + _dot(v_t, p)
            out.append((m_new, acc))
        return tuple(out)

    n_full = t0 // tk
    init = tuple((jnp.full((1, chain_w), NEG, F32), jnp.zeros((NSA_DIM + SUM_ROWS, chain_w), F32))
                 for _ in chains)
    carry = lax.fori_loop(0, n_full, functools.partial(sel_tile, causal=False), init)
    for d in range(max(tq // tk, 1)):
        carry = sel_tile(n_full + d, carry, causal=True)
    o_s = jnp.concatenate([split_sum(acc) for _, acc in carry], axis=1)

    gt = gate_ref[...]
    outs = []
    for h in range(NSA_HPG):
        c = slice(h * tq, (h + 1) * tq)
        outs.append(gt[3 * h:3 * h + 1] * o_c[:, c] + gt[3 * h + 1:3 * h + 2] * o_s[:, c]
                    + gt[3 * h + 2:3 * h + 3] * o_w[:, c])
    o_ref[...] = jnp.concatenate(outs, axis=0).T.astype(BF16)


def _anchored(kernel_fn, n_inputs):
    def body(*refs, **static):
        kernel_fn(*refs[:n_inputs], *refs[n_inputs + 1:], **static)
    return body


def _nsa(nq, nqr, gates, kcmp, vcmp, ks, vs, kw, vw, batch, seq, after=None):
    n = batch * seq
    tq = 512
    tk = 512 if seq % 512 == 0 else seq
    nqb = seq // tq
    n_sel = seq // SEL_LEN
    rows_c = seq // CMP_STRIDE
    gw = NSA_HPG * NSA_DIM
    cs = np.arange(rows_c)[None, :] * CMP_STRIDE
    ss = np.arange(n_sel)[:, None] * SEL_LEN
    n_cmp = (seq - CMP_LEN) // CMP_STRIDE + 1
    ovt = ((cs < ss + SEL_LEN) & (cs + CMP_LEN > ss) & (np.arange(rows_c)[None, :] < n_cmp))
    ovt = jnp.asarray(ovt.astype(np.float32), BF16)

    qspec = pl.BlockSpec((tq, gw), lambda b, g, i: (b * nqb + i, g))
    cspec = pl.BlockSpec((None, None, rows_c, NSA_DIM), lambda b, g, i: (b, g, 0, 0))
    kspec = pl.BlockSpec((None, seq, NSA_DIM), lambda b, g, i: (g, b, 0))
    args = [nq, nqr, gates, kcmp, vcmp, ovt, ks, vs, kw, vw]
    in_specs = [qspec, qspec,
                pl.BlockSpec((None, GATE_LANES, tq), lambda b, g, i: (g, 0, b * nqb + i)),
                cspec, cspec, pl.BlockSpec(ovt.shape, lambda b, g, i: (0, 0)),
                kspec, kspec, kspec, kspec]
    body = _nsa_kernel
    if after is not None:
        body = _anchored(_nsa_kernel, len(args))
        args.append(after)
        in_specs.append(pl.BlockSpec(memory_space=pl.ANY))
    body = functools.partial(body, tq=tq, tk=tk, seq=seq)
    return pl.pallas_call(
        body,
        grid=(batch, NSA_KV_GROUPS, nqb),
        in_specs=in_specs,
        out_specs=qspec,
        out_shape=jax.ShapeDtypeStruct((n, NSA_WIDTH), BF16),
        scratch_shapes=[pltpu.VMEM((NSA_DIM + SUM_ROWS, seq), BF16), pltpu.VMEM((NSA_DIM + SUM_ROWS, seq), BF16),
                        pltpu.VMEM((NSA_DIM, rows_c), BF16), pltpu.VMEM((n_sel, tq), F32)],
        compiler_params=_params(3),
        name="nsa",
    )(*args)


def _memkv_kernel(mem_ref, w_ref, kv_ref):
    kv_ref[...] = _dot(mem_ref[...].astype(BF16), w_ref[...]).astype(BF16)


def _memkv(mem2d, w_xkv):
    n = mem2d.shape[0]
    w = w_xkv.astype(BF16)
    return pl.pallas_call(
        _memkv_kernel,
        grid=(n // MEM_LEN,),
        in_specs=[pl.BlockSpec((MEM_LEN, D_MODEL), lambda i: (i, 0)),
                  pl.BlockSpec(w.shape, lambda i: (0, 0))],
        out_specs=pl.BlockSpec((MEM_LEN, 2 * D_MODEL), lambda i: (i, 0)),
        out_shape=jax.ShapeDtypeStruct((n, 2 * D_MODEL), BF16),
        compiler_params=_params(1),
        name="memkv",
    )(mem2d, w)


def _pack_halves(v):
    half = D_MODEL // 2
    hi = pltpu.bitcast(v[:, :half].astype(BF16).astype(F32), jnp.uint32)
    lo = pltpu.bitcast(v[:, half:].astype(BF16).astype(F32), jnp.uint32)
    return hi | (lo >> 16)


def _unpack_halves(words):
    return pltpu.bitcast(words & jnp.uint32(0xFFFF0000), F32), pltpu.bitcast(words << 16, F32)


def _postmix_kernel(x_ref, oret_ref, onsa_ref, kv_ref, wout_ref, wq_ref, wo_ref,
                    g1_ref, b1_ref, g2_ref, b2_ref, x2_ref, x2p_ref):
    mixed = jnp.concatenate([oret_ref[...], onsa_ref[...]], axis=1)
    x1 = _layer_norm(DN_ALPHA * x_ref[...] + _dot(mixed, wout_ref[...]), g1_ref[...], b1_ref[...])
    q = (_dot(x1.astype(BF16), wq_ref[...]) * (XATT_DIM ** -0.5)).astype(BF16)
    heads = []
    for h in range(XATT_HEADS):
        cols = slice(h * XATT_DIM, (h + 1) * XATT_DIM)
        s = _dot_nt(q[:, cols], kv_ref[:, cols])
        m = jnp.max(s, axis=-1, keepdims=True)
        p = jnp.exp(s - m)
        l = jnp.sum(p, axis=-1, keepdims=True)
        heads.append(_dot(p.astype(BF16), kv_ref[:, D_MODEL + h * XATT_DIM:D_MODEL + (h + 1) * XATT_DIM]) / l)
    att = jnp.concatenate(heads, axis=1).astype(BF16)
    x2 = _layer_norm(DN_ALPHA * x1 + _dot(att, wo_ref[...]), g2_ref[...], b2_ref[...])
    x2_ref[...] = x2
    x2p_ref[...] = _pack_halves(x2)


def _postmix(x2d, o_ret, o_nsa, kvx, w_out, w_xq, w_xo, ln1_g, ln1_b, ln2_g, ln2_b, batch0, batch, seq,
             after=None):
    n = batch * seq
    tm = 512 if seq % 512 == 0 else seq
    per_b = seq // tm
    row = lambda w: pl.BlockSpec((tm, w), lambda b, i: (b * per_b + i, 0))
    full = lambda a: pl.BlockSpec(a.shape, lambda b, i: (0,) * a.ndim)
    ws = [w_out.astype(BF16), w_xq.astype(BF16), w_xo.astype(BF16)]
    vecs = [v.reshape(1, D_MODEL) for v in (ln1_g, ln1_b, ln2_g, ln2_b)]
    args = [x2d, o_ret, o_nsa, kvx, *ws, *vecs]
    in_specs = ([pl.BlockSpec((tm, D_MODEL), lambda b, i: ((batch0 + b) * per_b + i, 0)),
                 row(RET_WIDTH), row(NSA_WIDTH),
                 pl.BlockSpec((MEM_LEN, 2 * D_MODEL), lambda b, i: (batch0 + b, 0))]
                + [full(w) for w in ws] + [full(v) for v in vecs])
    body = _postmix_kernel
    if after is not None:
        body = _anchored(_postmix_kernel, len(args))
        args.append(after)
        in_specs.append(pl.BlockSpec(memory_space=pl.ANY))
    return pl.pallas_call(
        body,
        grid=(batch, per_b),
        in_specs=in_specs,
        out_specs=[row(D_MODEL),
                   row(D_MODEL // 2)],
        out_shape=[jax.ShapeDtypeStruct((n, D_MODEL), F32),
                   jax.ShapeDtypeStruct((n, D_MODEL // 2), jnp.uint32)],
        compiler_params=_params(2),
        name="postmix",
    )(*args)


def _router_kernel(x_ref, wr_ref, bias_ref, e_ref, rank_ref, w_ref, cnt_ref, cntrow_ref, carry_ref, carryrow_ref):
    tn = x_ref.shape[0]
    per = N_EXPERTS // N_GROUPS

    @pl.when(pl.program_id(0) == 0)
    def _():
        carry_ref[...] = jnp.zeros_like(carry_ref)
        carryrow_ref[...] = jnp.zeros_like(carryrow_ref)

    logits = _dot_nt(wr_ref[...], x_ref[...].astype(BF16))
    scores = jax.nn.sigmoid(logits)
    biased = scores + bias_ref[...]
    b3 = biased.reshape(N_GROUPS, per, tn)
    member = lax.broadcasted_iota(jnp.int32, (N_GROUPS, per, tn), 1)
    top1 = jnp.max(b3, axis=1, keepdims=True)
    first1 = jnp.min(jnp.where(b3 == top1, member, per), axis=1, keepdims=True)
    top2 = jnp.max(jnp.where(member == first1, -jnp.inf, b3), axis=1, keepdims=True)
    gscore = top1 + top2
    gidx = lax.broadcasted_iota(jnp.int32, (N_GROUPS, 1, tn), 0)
    gwork = gscore
    for _ in range(TOPK_GROUPS - 1):
        gbest = jnp.max(gwork, axis=0, keepdims=True)
        gfirst = jnp.min(jnp.where(gwork == gbest, gidx, N_GROUPS), axis=0, keepdims=True)
        gwork = jnp.where(gidx == gfirst, -jnp.inf, gwork)
    kth = jnp.max(gwork, axis=0, keepdims=True)
    work = jnp.where(gscore >= kth, b3, NEG).reshape(N_EXPERTS, tn)
    eidx = lax.broadcasted_iota(jnp.int32, (N_EXPERTS, tn), 0)
    picks = []
    chosen = jnp.zeros((N_EXPERTS, tn), F32)
    for _ in range(TOP_K):
        best = jnp.max(work, axis=0, keepdims=True)
        first = jnp.min(jnp.where(work == best, eidx, N_EXPERTS), axis=0, keepdims=True)
        hit = eidx == first
        picks.append((first, hit))
        chosen = jnp.where(hit, 1.0, chosen)
        work = jnp.where(hit, -jnp.inf, work)

    r_i = lax.broadcasted_iota(jnp.int32, (tn, tn), 0)
    c_i = lax.broadcasted_iota(jnp.int32, (tn, tn), 1)
    before = jnp.where(r_i < c_i, 1.0, 0.0).astype(BF16)
    chosen_b = chosen.astype(BF16)
    rank = _dot(chosen_b, before) + carry_ref[...]
    carry_ref[...] = carry_ref[...] + jnp.sum(chosen, axis=1, keepdims=True)
    carryrow_ref[...] = carryrow_ref[...] + _dot_nt(jnp.ones((8, tn), BF16), chosen_b)
    cnt_ref[...] = carry_ref[...]
    cntrow_ref[...] = carryrow_ref[...]

    wsel = [jnp.sum(jnp.where(hit, scores, 0.0), axis=0, keepdims=True) for _, hit in picks]
    wsum = wsel[0]
    for v in wsel[1:]:
        wsum = wsum + v
    for kk, (first, hit) in enumerate(picks):
        e_ref[kk:kk + 1, :] = first
        rank_ref[kk:kk + 1, :] = jnp.sum(jnp.where(hit, rank, 0.0), axis=0, keepdims=True).astype(jnp.int32)
        w_ref[kk:kk + 1, :] = wsel[kk] / wsum * ROUTED_SCALE


def _router(x2, w_router, router_bias):
    n = x2.shape[0]
    tn = 512 if n % 512 == 0 else n
    wr_t = w_router.T.astype(BF16)
    bias = router_bias.reshape(N_EXPERTS, 1).astype(F32)
    kspec = pl.BlockSpec((TOP_K, tn), lambda i: (0, i))
    return pl.pallas_call(
        _router_kernel,
        grid=(n // tn,),
        in_specs=[pl.BlockSpec((tn, D_MODEL), lambda i: (i, 0)),
                  pl.BlockSpec(wr_t.shape, lambda i: (0, 0)),
                  pl.BlockSpec(bias.shape, lambda i: (0, 0))],
        out_specs=[kspec, kspec, kspec, pl.BlockSpec((N_EXPERTS, 1), lambda i: (0, 0)),
                   pl.BlockSpec((8, N_EXPERTS), lambda i: (0, 0))],
        out_shape=[jax.ShapeDtypeStruct((TOP_K, n), jnp.int32),
                   jax.ShapeDtypeStruct((TOP_K, n), jnp.int32),
                   jax.ShapeDtypeStruct((TOP_K, n), F32),
                   jax.ShapeDtypeStruct((N_EXPERTS, 1), F32),
                   jax.ShapeDtypeStruct((8, N_EXPERTS), F32)],
        scratch_shapes=[pltpu.VMEM((N_EXPERTS, 1), F32), pltpu.VMEM((8, N_EXPERTS), F32)],
        compiler_params=_params(1),
        name="router",
    )(x2, wr_t, bias)


def _slots_kernel(e_ref, rank_ref, cnt_ref, cntrow_ref, dest_ref, blk_e_ref, valid_ref, *, blk, n_blocks):
    pad = lambda c: jnp.ceil(c / blk) * blk
    cnt = cnt_ref[...]
    padded = pad(cnt)
    padded_row = pad(cntrow_ref[0:1, :])
    r_i = lax.broadcasted_iota(jnp.int32, (N_EXPERTS, N_EXPERTS), 0)
    c_i = lax.broadcasted_iota(jnp.int32, (N_EXPERTS, N_EXPERTS), 1)
    start = jnp.sum(jnp.where(c_i < r_i, padded_row, 0.0), axis=1, keepdims=True)
    end = start + padded
    e = e_ref[...]
    dest = rank_ref[...]
    for ex in range(N_EXPERTS):
        dest = dest + jnp.where(e == ex, start[ex:ex + 1, :].astype(jnp.int32), 0)
    dest_ref[...] = dest
    bstart = (lax.broadcasted_iota(jnp.int32, (1, n_blocks), 1) * blk).astype(F32)
    owner = jnp.sum(jnp.where(end <= bstart, 1.0, 0.0), axis=0, keepdims=True)
    blk_e_ref[...] = jnp.minimum(owner, N_EXPERTS - 1.0).astype(jnp.int32)
    inside = (start <= bstart) & (bstart < end)
    real = jnp.clip(start + cnt - bstart, 0.0, float(blk))
    valid_ref[...] = jnp.sum(jnp.where(inside, real, 0.0), axis=0, keepdims=True).astype(jnp.int32)


def _slots(e_k, rank_k, counts, counts_row, blk, n_blocks):
    n = e_k.shape[1]
    full = lambda shape: pl.BlockSpec(shape, lambda: (0,) * len(shape))
    return pl.pallas_call(
        functools.partial(_slots_kernel, blk=blk, n_blocks=n_blocks),
        in_specs=[full((TOP_K, n)), full((TOP_K, n)), full((N_EXPERTS, 1)), full((8, N_EXPERTS))],
        out_specs=[full((TOP_K, n)), full((1, n_blocks)), full((1, n_blocks))],
        out_shape=[jax.ShapeDtypeStruct((TOP_K, n), jnp.int32),
                   jax.ShapeDtypeStruct((1, n_blocks), jnp.int32),
                   jax.ShapeDtypeStruct((1, n_blocks), jnp.int32)],
        compiler_params=pltpu.CompilerParams(vmem_limit_bytes=VMEM_LIMIT),
        name="slots",
    )(e_k, rank_k, counts, counts_row)


def _sc_worker_base(per_worker):
    return (lax.axis_index("s") * SC_CORES + lax.axis_index("c")) * per_worker


def _sc_scatter_rows(rows, idx, n_out):
    n, width = rows.shape
    k_lists = idx.shape[0] // n
    workers = SC_CORES * SC_SUBCORES
    per_worker = n // workers
    assert per_worker * workers == n and per_worker % SC_CHUNK == 0
    mesh = plsc.VectorSubcoreMesh(core_axis_name="c", subcore_axis_name="s")

    @functools.partial(
        pl.kernel, mesh=mesh,
        out_type=jax.ShapeDtypeStruct((n_out, width), rows.dtype),
        scratch_types=[pltpu.VMEM((SC_CHUNK, width), rows.dtype)]
                      + [pltpu.VMEM((SC_CHUNK,), jnp.int32)] * k_lists + [pltpu.SemaphoreType.DMA] * 3,
        name="sc_scatter")
    def scatter(rows_hbm, idx_hbm, out_hbm, rows_v, *rest):
        idx_vs = rest[:k_lists]
        sem_rows, sem_idx, sem_out = rest[k_lists:]
        base = _sc_worker_base(per_worker)

        @pl.loop(0, per_worker // SC_CHUNK)
        def _(ci):
            off = pl.multiple_of(base + ci * SC_CHUNK, SC_CHUNK)
            loads = [pltpu.async_copy(rows_hbm.at[pl.ds(off, SC_CHUNK)], rows_v, sem_rows)]
            loads += [pltpu.async_copy(idx_hbm.at[pl.ds(pl.multiple_of(k * n + off, SC_CHUNK), SC_CHUNK)],
                                       idx_vs[k], sem_idx) for k in range(k_lists)]
            for c in loads:
                c.wait()
            copies = [pltpu.async_copy(rows_v, out_hbm.at[idx_vs[k]], sem_out) for k in range(k_lists)]
            for c in copies:
                c.wait()

    return scatter(rows, idx)


def _experts_kernel(blk_e_ref, valid_ref, xs_ref, wg_ref, wu_ref, wd_ref, y_ref, wg_b, wu_b, wd_b):
    i = pl.program_id(0)
    valid = valid_ref[i]

    @pl.when((i == 0) | (blk_e_ref[i] != blk_e_ref[jnp.maximum(i - 1, 0)]))
    def _():
        wg_b[...] = wg_ref[...].astype(BF16)
        wu_b[...] = wu_ref[...].astype(BF16)
        wd_b[...] = wd_ref[...].astype(BF16)

    @pl.when(valid > 0)
    def _():
        half = D_MODEL // 2
        row = lax.broadcasted_iota(jnp.int32, (xs_ref.shape[0], 1), 0)
        hi, lo = (v.astype(BF16) for v in _unpack_halves(jnp.where(row < valid, xs_ref[...], jnp.uint32(0))))
        gate = _dot(hi, wg_b[:half, :]) + _dot(lo, wg_b[half:, :])
        up = _dot(hi, wu_b[:half, :]) + _dot(lo, wu_b[half:, :])
        y_ref[...] = _pack_halves(_dot((jax.nn.silu(gate) * up).astype(BF16), wd_b[...]))

    @pl.when(valid <= 0)
    def _():
        y_ref[...] = jnp.zeros_like(y_ref)


def _experts(blk_e, valid, xs, w_gate, w_up, w_down, blk, after=None):
    cap, width = xs.shape
    wspec = lambda a: pl.BlockSpec((None,) + a.shape[1:], lambda i, be, nv: (be[i], 0, 0))
    rows = pl.BlockSpec((blk, width), lambda i, be, nv: (i, 0))
    args = [blk_e, valid, xs, w_gate, w_up, w_down]
    in_specs = [rows, wspec(w_gate), wspec(w_up), wspec(w_down)]
    body = _experts_kernel
    if after is not None:
        body = _anchored(_experts_kernel, len(args))
        args.append(after)
        in_specs.append(pl.BlockSpec(memory_space=pl.ANY))
    return pl.pallas_call(
        body,
        grid_spec=pltpu.PrefetchScalarGridSpec(
            num_scalar_prefetch=2,
            grid=(cap // blk,),
            in_specs=in_specs,
            out_specs=rows,
            scratch_shapes=[pltpu.VMEM(w.shape[1:], BF16) for w in (w_gate, w_up, w_down)],
        ),
        out_shape=jax.ShapeDtypeStruct(xs.shape, xs.dtype),
        compiler_params=_params(1),
        name="experts",
    )(*args)


def _sc_gather_rows(table, idx):
    b, width = idx.shape[0], table.shape[1]
    workers = SC_CORES * SC_SUBCORES
    per_worker = b // workers
    assert per_worker * workers == b and per_worker % (SC_CHUNK * SC_INFLIGHT) == 0
    mesh = plsc.VectorSubcoreMesh(core_axis_name="c", subcore_axis_name="s")

    @functools.partial(
        pl.kernel, mesh=mesh,
        out_type=jax.ShapeDtypeStruct((b, width), table.dtype),
        scratch_types=[pltpu.VMEM((SC_CHUNK,), jnp.int32)] * SC_INFLIGHT
                      + [pltpu.VMEM((SC_CHUNK, width), table.dtype)] * SC_INFLIGHT
                      + [pltpu.SemaphoreType.DMA] * (1 + 2 * SC_INFLIGHT),
        name="sc_gather")
    def gather(table_hbm, idx_hbm, out_hbm, *scratch):
        idx_vs = scratch[:SC_INFLIGHT]
        rows_vs = scratch[SC_INFLIGHT:2 * SC_INFLIGHT]
        sem_idx = scratch[2 * SC_INFLIGHT]
        sem_rows = scratch[2 * SC_INFLIGHT + 1:3 * SC_INFLIGHT + 1]
        sem_out = scratch[3 * SC_INFLIGHT + 1:]
        base = _sc_worker_base(per_worker)
        lanes = range(SC_INFLIGHT)

        @pl.loop(0, per_worker // (SC_CHUNK * SC_INFLIGHT))
        def _(gi):
            offs = [pl.multiple_of(base + (gi * SC_INFLIGHT + j) * SC_CHUNK, SC_CHUNK) for j in lanes]
            loads = [pltpu.async_copy(idx_hbm.at[pl.ds(offs[j], SC_CHUNK)], idx_vs[j], sem_idx) for j in lanes]
            for c in loads:
                c.wait()
            gathers = [pltpu.async_copy(table_hbm.at[idx_vs[j]], rows_vs[j], sem_rows[j]) for j in lanes]
            writes = []
            for j in lanes:
                gathers[j].wait()
                writes.append(pltpu.async_copy(rows_vs[j], out_hbm.at[pl.ds(offs[j], SC_CHUNK)], sem_out[j]))
            for c in writes:
                c.wait()

    return gather(table, idx)


def _combine_kernel(x_ref, wk_ref, yk_ref, wsg_ref, wsu_ref, wsd_ref, g_ref, b_ref, *rest):
    o_ref = rest[-1]
    x = x_ref[...]
    xb = x.astype(BF16)
    shared = _dot((jax.nn.silu(_dot(xb, wsg_ref[...])) * _dot(xb, wsu_ref[...])).astype(BF16), wsd_ref[...])
    wk = wk_ref[...]
    routed_hi = routed_lo = None
    for kk in range(TOP_K):
        hi, lo = _unpack_halves(yk_ref[kk])
        w = wk[:, kk:kk + 1]
        routed_hi = hi * w if kk == 0 else routed_hi + hi * w
        routed_lo = lo * w if kk == 0 else routed_lo + lo * w
    routed = jnp.concatenate([routed_hi, routed_lo], axis=1)
    o_ref[...] = _layer_norm(DN_ALPHA * x + (routed + shared), g_ref[...], b_ref[...])


def _combine(x2, w_tok, yk, ws_gate, ws_up, ws_down, ln3_g, ln3_b, row0, n_total, out_prev, after=None):
    n = x2.shape[0]
    tt = 512 if n % 512 == 0 and row0 % 512 == 0 else n
    blk0 = row0 // tt
    ws = [ws_gate.astype(BF16), ws_up.astype(BF16), ws_down.astype(BF16)]
    vecs = [ln3_g.reshape(1, D_MODEL), ln3_b.reshape(1, D_MODEL)]
    full = lambda a: pl.BlockSpec(a.shape, lambda i: (0,) * a.ndim)
    args = [x2, w_tok, yk, *ws, *vecs]
    in_specs = ([pl.BlockSpec((tt, D_MODEL), lambda i: (i, 0)),
                 pl.BlockSpec((tt, TOP_K), lambda i: (i, 0)),
                 pl.BlockSpec((TOP_K, tt, D_MODEL // 2), lambda i: (0, i, 0))]
                + [full(a) for a in ws] + [full(v) for v in vecs])
    aliases = {}
    if after is not None:
        args.append(after)
        in_specs.append(pl.BlockSpec(memory_space=pl.ANY))
    if out_prev is not None:
        aliases = {len(args): 0}
        args.append(out_prev)
        in_specs.append(pl.BlockSpec(memory_space=pl.ANY))
    return pl.pallas_call(
        _combine_kernel,
        grid=(n // tt,),
        in_specs=in_specs,
        out_specs=pl.BlockSpec((tt, D_MODEL), lambda i: (blk0 + i, 0)),
        out_shape=jax.ShapeDtypeStruct((n_total, D_MODEL), F32),
        input_output_aliases=aliases,
        compiler_params=_params(1),
        name="combine",
    )(*args)


EXPERT_BLOCK = 512


def _moe_dispatch(x2, x2p, w_router, router_bias):
    n = x2.shape[0]
    cap = n * TOP_K + N_EXPERTS * EXPERT_BLOCK
    e_k, rank_k, w_k, counts, counts_row = _router(x2, w_router, router_bias)
    dest, blk_e, valid = _slots(e_k, rank_k, counts, counts_row, EXPERT_BLOCK, cap // EXPERT_BLOCK)
    dest = dest.reshape(-1)
    return dict(w_tok=w_k.T, dest=dest, blk_e=blk_e.reshape(-1), valid=valid.reshape(-1),
                xs=_sc_scatter_rows(x2p, dest, cap))


def _moe_experts(routed, w_gate, w_up, w_down, after):
    y = _experts(routed["blk_e"], routed["valid"], routed["xs"], w_gate, w_up, w_down, EXPERT_BLOCK, after=after)
    n = routed["dest"].shape[0] // TOP_K
    return y, _sc_gather_rows(y, routed["dest"]).reshape(TOP_K, n, D_MODEL // 2)


def _layer(x, mem, positions, w_in, cmp_pe_k, cmp_pe_v, cmp_w1_k, cmp_w2_k, cmp_w1_v, cmp_w2_v,
           w_out, ln1_g, ln1_b, w_xq, w_xkv, w_xo, ln2_g, ln2_b, w_router, router_bias,
           w_gate, w_up, w_down, ws_gate, ws_up, ws_down, ln3_g, ln3_b):
    batch, seq, _ = x.shape
    n_total = batch * seq
    x2d = x.reshape(n_total, D_MODEL)
    pos_col = positions.astype(F32).reshape(n_total, 1)
    kvx = _memkv(mem.reshape(batch * MEM_LEN, D_MODEL), w_xkv)
    groups = BATCH_GROUPS if batch % BATCH_GROUPS == 0 else 1
    per = batch // groups
    n = per * seq

    def mixers_in(gi):
        (rq, rk, rv, rg, nq, nqr, kc, vc, ks, vs, kw, vw, gates) = _inproj(x2d, pos_col, w_in, gi * n, n)
        o_ret = _retention(rq, rk, rv, rg, per, seq)
        kcmp = _compress(kc, cmp_pe_k, cmp_w1_k, cmp_w2_k, per, seq)
        vcmp = _compress(vc, cmp_pe_v, cmp_w1_v, cmp_w2_v, per, seq)
        return o_ret, (nq, nqr, gates, kcmp, vcmp, ks, vs, kw, vw)

    def mix_and_route(gi, o_ret, o_nsa, after):
        x2, x2p = _postmix(x2d, o_ret, o_nsa, kvx, w_out, w_xq, w_xo, ln1_g, ln1_b, ln2_g, ln2_b,
                           gi * per, per, seq, after=after)
        return x2, _moe_dispatch(x2, x2p, w_router, router_bias)

    def combine(gi, x2, routed, yk, out_prev):
        return _combine(x2, routed["w_tok"], yk, ws_gate, ws_up, ws_down, ln3_g, ln3_b, gi * n, n_total,
                        out_prev)

    out = None
    o_ret, nsa_args = mixers_in(0)
    x2, routed = mix_and_route(0, o_ret, _nsa(*nsa_args, per, seq), None)
    for gi in range(1, groups):
        o_ret, nsa_args = mixers_in(gi)
        y, yk = _moe_experts(routed, w_gate, w_up, w_down, after=o_ret)
        o_nsa = _nsa(*nsa_args, per, seq, after=y)
        out = combine(gi - 1, x2, routed, yk, out)
        x2, routed = mix_and_route(gi, o_ret, o_nsa, out)
    y, yk = _moe_experts(routed, w_gate, w_up, w_down, after=None)
    out = combine(groups - 1, x2, routed, yk, out)
    return out.reshape(batch, seq, D_MODEL)


def kernel(x, mem, positions, w_in, cmp_pe_k, cmp_pe_v, cmp_w1_k, cmp_w2_k, cmp_w1_v, cmp_w2_v, w_out, ln1_g, ln1_b, w_xq, w_xkv, w_xo, ln2_g, ln2_b, w_router, router_bias, w_gate, w_up, w_down, ws_gate, ws_up, ws_down, ln3_g, ln3_b):
    for l in range(DEPTH):
        x = _layer(x, mem, positions, w_in[l], cmp_pe_k[l], cmp_pe_v[l], cmp_w1_k[l], cmp_w2_k[l],
                   cmp_w1_v[l], cmp_w2_v[l], w_out[l], ln1_g[l], ln1_b[l], w_xq[l], w_xkv[l],
                   w_xo[l], ln2_g[l], ln2_b[l], w_router[l], router_bias[l], w_gate[l], w_up[l],
                   w_down[l], ws_gate[l], ws_up[l], ws_down[l], ln3_g[l], ln3_b[l])
    return x
```

```python
import functools

import numpy as np
import jax
import jax.numpy as jnp
from jax import lax
from jax.experimental import pallas as pl
from jax.experimental.pallas import tpu as pltpu
from jax.experimental.pallas import tpu_sc as plsc

D_MODEL = 1024
MEM_LEN = 256
DEPTH = 1
DN_ALPHA = (2 * DEPTH) ** 0.25
LN_EPS = 1e-5
NEG = -1e30
FORCE = 1e9

RET_HEADS = 4
RET_DIM = 128
RET_CHUNK = 128
RET_ROPE_BASE = 10000.0
RET_STEP_CHUNKS = 4
RET_WIDTH = RET_HEADS * RET_DIM

NSA_HEADS = 8
NSA_KV_GROUPS = 2
NSA_HPG = NSA_HEADS // NSA_KV_GROUPS
NSA_DIM = 64
NSA_WIDTH = NSA_HEADS * NSA_DIM
KV_WIDTH = NSA_KV_GROUPS * NSA_DIM
CMP_LEN = 32
CMP_STRIDE = 16
CMP_HIDDEN = 256
SEL_LEN = 64
SEL_SHIFT = 6
SEL_TOPK = 16
WIN = 512
ROPE_THETA = 500000.0
ROPE_DIMS = NSA_DIM // 4
GATE_LANES = 16
NSA_CHAINS = 1
SUM_ROWS = 16
WIN_PART = 256

SC_CORES = 2
SC_SUBCORES = 16
SC_CHUNK = 64
SC_INFLIGHT = 2

XATT_HEADS = 4
XATT_DIM = D_MODEL // XATT_HEADS

N_EXPERTS = 64
TOP_K = 8
N_GROUPS = 8
TOPK_GROUPS = 4
EXPERT_FF = 256
SHARED_FF = 256
ROUTED_SCALE = 2.5

LANES = 128
VMEM_LIMIT = 56 * 1024 * 1024

F32 = jnp.float32
BF16 = jnp.bfloat16
NT_DIMS = (((1,), (1,)), ((), ()))


def _params(n_axes):
    return pltpu.CompilerParams(dimension_semantics=("arbitrary",) * n_axes,
                                vmem_limit_bytes=VMEM_LIMIT)


def _dot(a, b):
    return jnp.dot(a, b, preferred_element_type=F32)


def _dot_nt(a, b):
    return lax.dot_general(a, b, NT_DIMS, preferred_element_type=F32)


def _layer_norm(v, g, b):
    mu = jnp.mean(v, axis=-1, keepdims=True)
    d = v - mu
    var = jnp.mean(d * d, axis=-1, keepdims=True)
    return d * lax.rsqrt(var + LN_EPS) * g + b


def _inproj_kernel(x_ref, pos_ref, wret_ref, wnq_ref, wkv_ref, wg_ref, invr_ref, invn_ref,
                   rq_ref, rk_ref, rv_ref, rg_ref, nq_ref, nqr_ref, kc_ref, vc_ref,
                   ks_ref, vs_ref, kw_ref, vw_ref, gate_ref):
    xb = x_ref[...].astype(BF16)
    pos = pos_ref[...]
    lane = lax.broadcasted_iota(jnp.int32, (1, LANES), 1)

    ang = pos * invr_ref[...]
    cos_r = jnp.cos(ang)
    sin_r = jnp.sin(ang)
    sin_r = jnp.where(lane < RET_DIM // 2, -sin_r, sin_r)
    q_all = _dot(xb, wret_ref[:, :RET_WIDTH])
    k_all = _dot(xb, wret_ref[:, RET_WIDTH:2 * RET_WIDTH])
    for h in range(RET_HEADS):
        cols = slice(h * RET_DIM, (h + 1) * RET_DIM)
        q = q_all[:, cols]
        rq_ref[:, cols] = (q * cos_r + pltpu.roll(q, RET_DIM // 2, 1) * sin_r).astype(BF16)
        k = k_all[:, cols]
        k = (k * cos_r + pltpu.roll(k, RET_DIM // 2, 1) * sin_r) * (RET_DIM ** -0.5)
        rk_ref[:, cols] = k.astype(BF16)
    rv_ref[...] = _dot(xb, wret_ref[:, 2 * RET_WIDTH:3 * RET_WIDTH]).astype(BF16)
    rg_ref[...] = _dot(xb, wret_ref[:, 3 * RET_WIDTH:4 * RET_WIDTH]).astype(BF16)

    half = ROPE_DIMS // 2
    j = lane % NSA_DIM
    angn = pos * invn_ref[...]
    cos_n = jnp.cos(angn)
    sin_n = jnp.sin(angn)
    sin_lo = jnp.where(j < half, -sin_n, 0.0)
    sin_hi = jnp.where((j >= half) & (j < 2 * half), sin_n, 0.0)

    def rope_n(v):
        return v * cos_n + pltpu.roll(v, half, 1) * sin_hi + pltpu.roll(v, LANES - half, 1) * sin_lo

    scale = NSA_DIM ** -0.5
    nq_all = _dot(xb, wnq_ref[...])
    for c in range(NSA_WIDTH // LANES):
        cols = slice(c * LANES, (c + 1) * LANES)
        q = nq_all[:, cols]
        nq_ref[:, cols] = (q * scale).astype(BF16)
        nqr_ref[:, cols] = (rope_n(q) * scale).astype(BF16)

    kv_all = _dot(xb, wkv_ref[...])

    def kv(i):
        return kv_all[:, i * KV_WIDTH:(i + 1) * KV_WIDTH]

    def split_groups(ref, v):
        for g in range(NSA_KV_GROUPS):
            ref[g] = v[:, g * NSA_DIM:(g + 1) * NSA_DIM].astype(BF16)

    kc_ref[...] = kv(0)
    vc_ref[...] = kv(1)
    split_groups(ks_ref, rope_n(kv(2)))
    split_groups(vs_ref, kv(3))
    split_groups(kw_ref, rope_n(kv(4)))
    split_groups(vw_ref, kv(5))

    gt = jax.nn.sigmoid(_dot_nt(wg_ref[...], xb))
    for g in range(NSA_KV_GROUPS):
        gate_ref[g] = gt[g * GATE_LANES:(g + 1) * GATE_LANES, :]


def _inproj(x2d, pos_col, w_in, row0, n):
    tm = 512 if n % 512 == 0 and row0 % 512 == 0 else n
    blk0 = row0 // tm
    off = np.cumsum([0] + [RET_WIDTH] * 4 + [NSA_WIDTH] + [KV_WIDTH] * 6)
    w_ret = w_in[:, :off[4]].astype(BF16)
    w_nq = w_in[:, off[4]:off[5]].astype(BF16)
    w_kv = w_in[:, off[5]:off[11]].astype(BF16)
    wg = w_in[:, off[11]:].reshape(D_MODEL, NSA_KV_GROUPS, NSA_HPG * 3)
    wg = jnp.pad(wg, ((0, 0), (0, 0), (0, GATE_LANES - NSA_HPG * 3)))
    wg = wg.reshape(D_MODEL, NSA_KV_GROUPS * GATE_LANES).T.astype(BF16)

    lane = np.arange(LANES)
    half_r = RET_DIM // 2
    inv_r = (np.float32(RET_ROPE_BASE) ** (-np.arange(half_r, dtype=np.float32) / np.float32(half_r)))
    inv_r = inv_r.astype(np.float32)[lane % half_r][None, :]
    half_n = ROPE_DIMS // 2
    inv_n = (np.float32(ROPE_THETA) ** (-np.arange(half_n, dtype=np.float32) / np.float32(half_n)))
    jn = lane % NSA_DIM
    inv_n = np.where(jn < ROPE_DIMS, inv_n.astype(np.float32)[jn % half_n], np.float32(0.0))[None, :]

    row = lambda w: pl.BlockSpec((tm, w), lambda i: (i, 0))
    src_row = lambda w: pl.BlockSpec((tm, w), lambda i: (blk0 + i, 0))
    full = lambda a: pl.BlockSpec(a.shape, lambda i: (0,) * a.ndim)
    grp = lambda w: pl.BlockSpec((NSA_KV_GROUPS, tm, w), lambda i: (0, i, 0))
    bf = lambda w: jax.ShapeDtypeStruct((n, w), BF16)
    gbf = jax.ShapeDtypeStruct((NSA_KV_GROUPS, n, NSA_DIM), BF16)
    inv_r = jnp.asarray(inv_r, F32)
    inv_n = jnp.asarray(inv_n, F32)
    return pl.pallas_call(
        _inproj_kernel,
        grid=(n // tm,),
        in_specs=[src_row(D_MODEL), src_row(1), full(w_ret), full(w_nq), full(w_kv), full(wg),
                  full(inv_r), full(inv_n)],
        out_specs=[row(RET_WIDTH)] * 4 + [row(NSA_WIDTH)] * 2 + [row(KV_WIDTH)] * 2
                  + [grp(NSA_DIM)] * 4
                  + [pl.BlockSpec((NSA_KV_GROUPS, GATE_LANES, tm), lambda i: (0, 0, i))],
        out_shape=[bf(RET_WIDTH)] * 4 + [bf(NSA_WIDTH)] * 2
                  + [jax.ShapeDtypeStruct((n, KV_WIDTH), F32)] * 2 + [gbf] * 4
                  + [jax.ShapeDtypeStruct((NSA_KV_GROUPS, GATE_LANES, n), F32)],
        compiler_params=_params(1),
        name="inproj",
    )(x2d, pos_col, w_ret, w_nq, w_kv, wg, inv_r, inv_n)


def _retention_kernel(q_ref, k_ref, v_ref, g_ref, o_ref, state_ref):
    c = RET_CHUNK

    @pl.when(pl.program_id(1) == 0)
    def _():
        state_ref[...] = jnp.zeros_like(state_ref)

    row = lax.broadcasted_iota(jnp.int32, (c, c), 0)
    col = lax.broadcasted_iota(jnp.int32, (c, c), 1)
    rel = (row - col).astype(F32)
    idx = lax.broadcasted_iota(jnp.int32, (c, 1), 0).astype(F32)
    for h in range(RET_HEADS):
        log_g = float(np.log(np.float32(1.0) - np.float32(2.0) ** np.float32(-5.0 - h)))
        cols = slice(h * RET_DIM, (h + 1) * RET_DIM)
        dmask = jnp.where(rel >= 0, jnp.exp(log_g * jnp.maximum(rel, 0.0)), 0.0)
        zeta = jnp.exp(log_g * (c - 1.0 - idx))
        xi = jnp.exp(log_g * (idx + 1.0))
        for j in range(q_ref.shape[0] // c):
            rows = slice(j * c, (j + 1) * c)
            q = q_ref[rows, cols]
            k = k_ref[rows, cols]
            v = v_ref[rows, cols]
            scores = _dot_nt(q, k) * dmask
            inner = _dot(scores.astype(BF16), v)
            prev = state_ref[h]
            cross = _dot(q, prev.astype(BF16)) * xi
            kz = (k.astype(F32) * zeta).astype(BF16)
            kv = lax.dot_general(kz, v, (((0,), (0,)), ((), ())), preferred_element_type=F32)
            state_ref[h] = prev * float(np.exp(np.float32(log_g) * np.float32(c))) + kv
            o = inner + cross
            mu = jnp.mean(o, axis=-1, keepdims=True)
            d = o - mu
            var = jnp.mean(d * d, axis=-1, keepdims=True)
            o = d * lax.rsqrt(var + LN_EPS)
            o_ref[rows, cols] = (jax.nn.silu(g_ref[rows, cols].astype(F32)) * o).astype(BF16)


def _retention(rq, rk, rv, rg, batch, seq):
    per_step = RET_STEP_CHUNKS if (seq // RET_CHUNK) % RET_STEP_CHUNKS == 0 else 1
    nc = seq // (RET_CHUNK * per_step)
    spec = pl.BlockSpec((RET_CHUNK * per_step, RET_WIDTH), lambda b, n: (b * nc + n, 0))
    return pl.pallas_call(
        _retention_kernel,
        grid=(batch, nc),
        in_specs=[spec] * 4,
        out_specs=spec,
        out_shape=jax.ShapeDtypeStruct(rq.shape, BF16),
        scratch_shapes=[pltpu.VMEM((RET_HEADS, RET_DIM, RET_DIM), F32)],
        compiler_params=_params(2),
        name="retention",
    )(rq, rk, rv, rg)


def _compress_kernel(a_ref, pe_ref, w1_ref, w2_ref, o_ref, shift_ref, *, n_cmp):
    rows = a_ref.shape[0]
    a = a_ref[...]
    lo = (a + pe_ref[0]).astype(BF16)
    hi = (a + pe_ref[1]).astype(BF16)
    ridx = lax.broadcasted_iota(jnp.int32, (rows, 1), 0)
    shift_ref[rows:rows + 8, :] = jnp.zeros((8, CMP_HIDDEN), F32)
    for g in range(NSA_KV_GROUPS):
        p = _dot(lo, w1_ref[0, g])
        shift_ref[0:rows, :] = _dot(hi, w1_ref[1, g])
        hid = jax.nn.silu(p + shift_ref[pl.ds(1, rows), :])
        out = _dot(hid.astype(BF16), w2_ref[...])
        o_ref[g] = jnp.where(ridx < n_cmp, out, 0.0).astype(BF16)


def _compress(a, pe, w1, w2, batch, seq):
    rows = seq // CMP_STRIDE
    per = CMP_STRIDE * KV_WIDTH
    n_cmp = (seq - CMP_LEN) // CMP_STRIDE + 1
    a2 = a.reshape(batch * rows, per)
    pe2 = jnp.tile(pe.reshape(2, CMP_STRIDE, 1, NSA_DIM), (1, 1, NSA_KV_GROUPS, 1)).reshape(2, 1, per)
    w1r = w1.reshape(2, CMP_STRIDE, 1, NSA_DIM, CMP_HIDDEN)
    eye = jnp.eye(NSA_KV_GROUPS, dtype=w1.dtype).reshape(1, NSA_KV_GROUPS, 1, NSA_KV_GROUPS, 1, 1)
    w1x = (w1r[:, None] * eye).reshape(2, NSA_KV_GROUPS, per, CMP_HIDDEN).astype(BF16)
    w2b = w2.astype(BF16)
    full = lambda arr: pl.BlockSpec(arr.shape, lambda b: (0,) * arr.ndim)
    return pl.pallas_call(
        functools.partial(_compress_kernel, n_cmp=n_cmp),
        grid=(batch,),
        in_specs=[pl.BlockSpec((rows, per), lambda b: (b, 0)), full(pe2), full(w1x), full(w2b)],
        out_specs=pl.BlockSpec((None, NSA_KV_GROUPS, rows, NSA_DIM), lambda b: (b, 0, 0, 0)),
        out_shape=jax.ShapeDtypeStruct((batch, NSA_KV_GROUPS, rows, NSA_DIM), BF16),
        scratch_shapes=[pltpu.VMEM((rows + 8, CMP_HIDDEN), F32)],
        compiler_params=_params(1),
        name="compress",
    )(a2, pe2, w1x, w2b)


def _heads_to_lanes(ref):
    vt = ref[...].astype(F32).T
    return jnp.concatenate([vt[h * NSA_DIM:(h + 1) * NSA_DIM] for h in range(NSA_HPG)], axis=1).astype(BF16)


def _tile_heads(v):
    return jnp.concatenate([v] * NSA_HPG, axis=1)


def _transpose_into(dst_ref, src_ref, chunk):
    def step(c, _):
        c0 = pl.multiple_of(c * chunk, chunk)
        dst_ref[:NSA_DIM, pl.ds(c0, chunk)] = src_ref[pl.ds(c0, chunk), :].astype(F32).T.astype(BF16)
        return 0
    lax.fori_loop(0, src_ref.shape[0] // chunk, step, 0)


def _nsa_kernel(qraw_ref, qrot_ref, gate_ref, kcmp_ref, vcmp_ref, ovt_ref,
                ks_ref, vs_ref, kw_ref, vw_ref, o_ref, vst_ref, vwt_ref, vct_ref, bias_ref, *, tq, tk, seq):
    i = pl.program_id(2)
    t0 = i * tq
    cols = NSA_HPG * tq
    n_sel = seq // SEL_LEN
    n_cmp_rows = seq // CMP_STRIDE
    blocks_per_tile = tk // SEL_LEN

    @pl.when(i == 0)
    def _():
        chunk = min(512, n_cmp_rows)
        _transpose_into(vst_ref, vs_ref, chunk)
        _transpose_into(vwt_ref, vw_ref, chunk)
        _transpose_into(vct_ref, vcmp_ref, chunk)
        vst_ref[NSA_DIM:, :] = jnp.ones((SUM_ROWS, seq), BF16)
        vwt_ref[NSA_DIM:, :] = jnp.ones((SUM_ROWS, seq), BF16)

    def split_sum(acc):
        return acc[:NSA_DIM] / acc[NSA_DIM:NSA_DIM + 1]

    q_raw = _heads_to_lanes(qraw_ref)
    q_rot = _heads_to_lanes(qrot_ref)
    t_row = t0 + lax.broadcasted_iota(jnp.int32, (1, tq), 1)

    chain_w = cols // NSA_CHAINS
    heads_per_chain = chain_w // tq
    chains = [slice(c * chain_w, (c + 1) * chain_w) for c in range(NSA_CHAINS)]
    tile_chain = lambda v: jnp.concatenate([v] * heads_per_chain, axis=1)

    pw = min(tq, WIN_PART)
    parts = []
    for u in range(tq // pw):
        span = WIN + pw
        ws = pl.multiple_of(jnp.maximum(t0 + u * pw - WIN, 0), pw)
        dist = t_row[:, u * pw:(u + 1) * pw] - (ws + lax.broadcasted_iota(jnp.int32, (span, 1), 0))
        bias_w = jnp.concatenate([jnp.where((dist >= 0) & (dist < WIN), 0.0, NEG)] * NSA_HPG, axis=1)
        q_part = jnp.concatenate([q_rot[:, h * tq + u * pw:h * tq + (u + 1) * pw] for h in range(NSA_HPG)],
                                 axis=1)
        s_w = _dot(kw_ref[pl.ds(ws, span), :], q_part) + bias_w
        p_w = jnp.exp((s_w - jnp.max(s_w, axis=0, keepdims=True)).astype(BF16))
        parts.append(split_sum(_dot(vwt_ref[:, pl.ds(ws, span)], p_w)))
    o_w = jnp.concatenate([parts[u][:, h * pw:(h + 1) * pw]
                           for h in range(NSA_HPG) for u in range(tq // pw)], axis=1)

    c_idx = lax.broadcasted_iota(jnp.int32, (n_cmp_rows, 1), 0)
    valid = tile_chain(jnp.where(c_idx * CMP_STRIDE + (CMP_LEN - 1) <= t_row, 1.0, 0.0))
    bias_c = (valid - 1.0) * (-NEG)
    o_c = []
    p_sum = None
    for c in chains:
        s_c = _dot(kcmp_ref[...], q_raw[:, c]) + bias_c
        e_c = jnp.exp(s_c - jnp.max(s_c, axis=0, keepdims=True)) * valid
        l_c = jnp.sum(e_c, axis=0, keepdims=True)
        p_c = e_c / jnp.where(l_c > 0.0, l_c, 1.0)
        o_c.append(_dot(vct_ref[...], p_c.astype(BF16)))
        for h in range(heads_per_chain):
            p_h = p_c[:, h * tq:(h + 1) * tq]
            p_sum = p_h if p_sum is None else p_sum + p_h
    o_c = jnp.concatenate(o_c, axis=1)

    p_hi = p_sum.astype(BF16)
    p_lo = (p_sum - p_hi.astype(F32)).astype(BF16)
    ovt = ovt_ref[...]
    imp = _dot(ovt, p_hi) + _dot(ovt, p_lo)
    jb = lax.broadcasted_iota(jnp.int32, (n_sel, tq), 0)
    cur = (t0 + lax.broadcasted_iota(jnp.int32, (n_sel, tq), 1)) >> SEL_SHIFT
    forced = (jb == 0) | (jb == cur) | (jb == cur - 1)
    work = jnp.where(forced, FORCE, imp)
    work = jnp.where(jb <= cur, work, NEG)
    sel_t = jnp.zeros((n_sel, tq), F32)
    for _ in range(min(SEL_TOPK, n_sel)):
        best = jnp.max(work, axis=0, keepdims=True)
        first = jnp.min(jnp.where(work == best, jb, n_sel), axis=0, keepdims=True)
        hit = jb == first
        sel_t = jnp.where(hit, 1.0, sel_t)
        work = jnp.where(hit, -jnp.inf, work)
    bias_ref[...] = jnp.where(sel_t > 0.5, 0.0, NEG)

    def sel_tile(kt, carry, causal):
        k0 = pl.multiple_of(kt * tk, tk)
        bias = jnp.concatenate(
            [jnp.broadcast_to(bias_ref[pl.ds(kt * blocks_per_tile + j, 1), :], (SEL_LEN, tq))
             for j in range(blocks_per_tile)], axis=0)
        if causal:
            kpos = k0 + lax.broadcasted_iota(jnp.int32, (tk, 1), 0)
            bias = jnp.where(kpos <= t_row, bias, NEG)
        bias = tile_chain(bias)
        k_t = ks_ref[pl.ds(k0, tk), :]
        v_t = vst_ref[:, pl.ds(k0, tk)]
        out = []
        scores = [_dot(k_t, q_rot[:, c]) + bias for c in chains]
        for (m, acc), s in zip(carry, scores):
            m_new = jnp.maximum(m, jnp.max(s, axis=0, keepdims=True))
            p = jnp.exp((s - m_new).astype(BF16))
            acc = jnp.exp(m - m_new) * acc + _dot(v_t, p)
            out.append((m_new, acc))
        return tuple(out)

    n_full = t0 // tk
    init = tuple((jnp.full((1, chain_w), NEG, F32), jnp.zeros((NSA_DIM + SUM_ROWS, chain_w), F32))
                 for _ in chains)
    carry = lax.fori_loop(0, n_full, functools.partial(sel_tile, causal=False), init)
    for d in range(max(tq // tk, 1)):
        carry = sel_tile(n_full + d, carry, causal=True)
    o_s = jnp.concatenate([split_sum(acc) for _, acc in carry], axis=1)

    gt = gate_ref[...]
    outs = []
    for h in range(NSA_HPG):
        c = slice(h * tq, (h + 1) * tq)
        outs.append(gt[3 * h:3 * h + 1] * o_c[:, c] + gt[3 * h + 1:3 * h + 2] * o_s[:, c]
                    + gt[3 * h + 2:3 * h + 3] * o_w[:, c])
    o_ref[...] = jnp.concatenate(outs, axis=0).T.astype(BF16)


def _anchored(kernel_fn, n_inputs, n_anchors):
    def body(*refs, **static):
        kernel_fn(*refs[:n_inputs], *refs[n_inputs + n_anchors:], **static)
    return body


def _add_anchors(kernel_fn, args, in_specs, after):
    after = [a for a in (after or ()) if a is not None]
    if not after:
        return kernel_fn
    body = _anchored(kernel_fn, len(args), len(after))
    args.extend(after)
    in_specs.extend([pl.BlockSpec(memory_space=pl.ANY)] * len(after))
    return body


def _nsa(nq, nqr, gates, kcmp, vcmp, ks, vs, kw, vw, batch, seq, after=None):
    n = batch * seq
    tq = 512
    tk = 512 if seq % 512 == 0 else seq
    nqb = seq // tq
    n_sel = seq // SEL_LEN
    rows_c = seq // CMP_STRIDE
    gw = NSA_HPG * NSA_DIM
    cs = np.arange(rows_c)[None, :] * CMP_STRIDE
    ss = np.arange(n_sel)[:, None] * SEL_LEN
    n_cmp = (seq - CMP_LEN) // CMP_STRIDE + 1
    ovt = ((cs < ss + SEL_LEN) & (cs + CMP_LEN > ss) & (np.arange(rows_c)[None, :] < n_cmp))
    ovt = jnp.asarray(ovt.astype(np.float32), BF16)

    qspec = pl.BlockSpec((tq, gw), lambda b, g, i: (b * nqb + i, g))
    cspec = pl.BlockSpec((None, None, rows_c, NSA_DIM), lambda b, g, i: (b, g, 0, 0))
    kspec = pl.BlockSpec((None, seq, NSA_DIM), lambda b, g, i: (g, b, 0))
    args = [nq, nqr, gates, kcmp, vcmp, ovt, ks, vs, kw, vw]
    in_specs = [qspec, qspec,
                pl.BlockSpec((None, GATE_LANES, tq), lambda b, g, i: (g, 0, b * nqb + i)),
                cspec, cspec, pl.BlockSpec(ovt.shape, lambda b, g, i: (0, 0)),
                kspec, kspec, kspec, kspec]
    body = functools.partial(_add_anchors(_nsa_kernel, args, in_specs, after), tq=tq, tk=tk, seq=seq)
    return pl.pallas_call(
        body,
        grid=(batch, NSA_KV_GROUPS, nqb),
        in_specs=in_specs,
        out_specs=qspec,
        out_shape=jax.ShapeDtypeStruct((n, NSA_WIDTH), BF16),
        scratch_shapes=[pltpu.VMEM((NSA_DIM + SUM_ROWS, seq), BF16), pltpu.VMEM((NSA_DIM + SUM_ROWS, seq), BF16),
                        pltpu.VMEM((NSA_DIM, rows_c), BF16), pltpu.VMEM((n_sel, tq), F32)],
        compiler_params=_params(3),
        name="nsa",
    )(*args)


def _memkv_kernel(mem_ref, w_ref, kv_ref):
    kv_ref[...] = _dot(mem_ref[...].astype(BF16), w_ref[...]).astype(BF16)


def _memkv(mem2d, w_xkv):
    n = mem2d.shape[0]
    w = w_xkv.astype(BF16)
    return pl.pallas_call(
        _memkv_kernel,
        grid=(n // MEM_LEN,),
        in_specs=[pl.BlockSpec((MEM_LEN, D_MODEL), lambda i: (i, 0)),
                  pl.BlockSpec(w.shape, lambda i: (0, 0))],
        out_specs=pl.BlockSpec((MEM_LEN, 2 * D_MODEL), lambda i: (i, 0)),
        out_shape=jax.ShapeDtypeStruct((n, 2 * D_MODEL), BF16),
        compiler_params=_params(1),
        name="memkv",
    )(mem2d, w)


def _pack_halves(v):
    half = D_MODEL // 2
    hi = pltpu.bitcast(v[:, :half].astype(BF16).astype(F32), jnp.uint32)
    lo = pltpu.bitcast(v[:, half:].astype(BF16).astype(F32), jnp.uint32)
    return hi | (lo >> 16)


def _unpack_halves(words):
    return pltpu.bitcast(words & jnp.uint32(0xFFFF0000), F32), pltpu.bitcast(words << 16, F32)


def _postmix_kernel(x_ref, oret_ref, onsa_ref, kv_ref, wout_ref, wq_ref, wo_ref,
                    g1_ref, b1_ref, g2_ref, b2_ref, x2_ref, x2p_ref):
    mixed = jnp.concatenate([oret_ref[...], onsa_ref[...]], axis=1)
    x1 = _layer_norm(DN_ALPHA * x_ref[...] + _dot(mixed, wout_ref[...]), g1_ref[...], b1_ref[...])
    q = (_dot(x1.astype(BF16), wq_ref[...]) * (XATT_DIM ** -0.5)).astype(BF16)
    heads = []
    for h in range(XATT_HEADS):
        cols = slice(h * XATT_DIM, (h + 1) * XATT_DIM)
        s = _dot_nt(q[:, cols], kv_ref[:, cols])
        m = jnp.max(s, axis=-1, keepdims=True)
        p = jnp.exp(s - m)
        l = jnp.sum(p, axis=-1, keepdims=True)
        heads.append(_dot(p.astype(BF16), kv_ref[:, D_MODEL + h * XATT_DIM:D_MODEL + (h + 1) * XATT_DIM]) / l)
    att = jnp.concatenate(heads, axis=1).astype(BF16)
    x2 = _layer_norm(DN_ALPHA * x1 + _dot(att, wo_ref[...]), g2_ref[...], b2_ref[...])
    x2_ref[...] = x2
    x2p_ref[...] = _pack_halves(x2)


def _postmix(x2d, o_ret, o_nsa, kvx, w_out, w_xq, w_xo, ln1_g, ln1_b, ln2_g, ln2_b, batch0, batch, seq,
             after=None):
    n = batch * seq
    tm = 512 if seq % 512 == 0 else seq
    per_b = seq // tm
    row = lambda w: pl.BlockSpec((tm, w), lambda b, i: (b * per_b + i, 0))
    full = lambda a: pl.BlockSpec(a.shape, lambda b, i: (0,) * a.ndim)
    ws = [w_out.astype(BF16), w_xq.astype(BF16), w_xo.astype(BF16)]
    vecs = [v.reshape(1, D_MODEL) for v in (ln1_g, ln1_b, ln2_g, ln2_b)]
    args = [x2d, o_ret, o_nsa, kvx, *ws, *vecs]
    in_specs = ([pl.BlockSpec((tm, D_MODEL), lambda b, i: ((batch0 + b) * per_b + i, 0)),
                 row(RET_WIDTH), row(NSA_WIDTH),
                 pl.BlockSpec((MEM_LEN, 2 * D_MODEL), lambda b, i: (batch0 + b, 0))]
                + [full(w) for w in ws] + [full(v) for v in vecs])
    return pl.pallas_call(
        _add_anchors(_postmix_kernel, args, in_specs, after),
        grid=(batch, per_b),
        in_specs=in_specs,
        out_specs=[row(D_MODEL),
                   row(D_MODEL // 2)],
        out_shape=[jax.ShapeDtypeStruct((n, D_MODEL), F32),
                   jax.ShapeDtypeStruct((n, D_MODEL // 2), jnp.uint32)],
        compiler_params=_params(2),
        name="postmix",
    )(*args)


def _router_kernel(x_ref, wr_ref, bias_ref, e_ref, rank_ref, w_ref, cnt_ref, cntrow_ref, carry_ref, carryrow_ref):
    tn = x_ref.shape[0]
    per = N_EXPERTS // N_GROUPS

    @pl.when(pl.program_id(0) == 0)
    def _():
        carry_ref[...] = jnp.zeros_like(carry_ref)
        carryrow_ref[...] = jnp.zeros_like(carryrow_ref)

    logits = _dot_nt(wr_ref[...], x_ref[...].astype(BF16))
    scores = jax.nn.sigmoid(logits)
    biased = scores + bias_ref[...]
    b3 = biased.reshape(N_GROUPS, per, tn)
    member = lax.broadcasted_iota(jnp.int32, (N_GROUPS, per, tn), 1)
    top1 = jnp.max(b3, axis=1, keepdims=True)
    first1 = jnp.min(jnp.where(b3 == top1, member, per), axis=1, keepdims=True)
    top2 = jnp.max(jnp.where(member == first1, -jnp.inf, b3), axis=1, keepdims=True)
    gscore = top1 + top2
    gidx = lax.broadcasted_iota(jnp.int32, (N_GROUPS, 1, tn), 0)
    gwork = gscore
    for _ in range(TOPK_GROUPS - 1):
        gbest = jnp.max(gwork, axis=0, keepdims=True)
        gfirst = jnp.min(jnp.where(gwork == gbest, gidx, N_GROUPS), axis=0, keepdims=True)
        gwork = jnp.where(gidx == gfirst, -jnp.inf, gwork)
    kth = jnp.max(gwork, axis=0, keepdims=True)
    work = jnp.where(gscore >= kth, b3, NEG).reshape(N_EXPERTS, tn)
    eidx = lax.broadcasted_iota(jnp.int32, (N_EXPERTS, tn), 0)
    picks = []
    chosen = jnp.zeros((N_EXPERTS, tn), F32)
    for _ in range(TOP_K):
        best = jnp.max(work, axis=0, keepdims=True)
        first = jnp.min(jnp.where(work == best, eidx, N_EXPERTS), axis=0, keepdims=True)
        hit = eidx == first
        picks.append((first, hit))
        chosen = jnp.where(hit, 1.0, chosen)
        work = jnp.where(hit, -jnp.inf, work)

    r_i = lax.broadcasted_iota(jnp.int32, (tn, tn), 0)
    c_i = lax.broadcasted_iota(jnp.int32, (tn, tn), 1)
    before = jnp.where(r_i < c_i, 1.0, 0.0).astype(BF16)
    chosen_b = chosen.astype(BF16)
    rank = _dot(chosen_b, before) + carry_ref[...]
    carry_ref[...] = carry_ref[...] + jnp.sum(chosen, axis=1, keepdims=True)
    carryrow_ref[...] = carryrow_ref[...] + _dot_nt(jnp.ones((8, tn), BF16), chosen_b)
    cnt_ref[...] = carry_ref[...]
    cntrow_ref[...] = carryrow_ref[...]

    wsel = [jnp.sum(jnp.where(hit, scores, 0.0), axis=0, keepdims=True) for _, hit in picks]
    wsum = wsel[0]
    for v in wsel[1:]:
        wsum = wsum + v
    for kk, (first, hit) in enumerate(picks):
        e_ref[kk:kk + 1, :] = first
        rank_ref[kk:kk + 1, :] = jnp.sum(jnp.where(hit, rank, 0.0), axis=0, keepdims=True).astype(jnp.int32)
        w_ref[kk:kk + 1, :] = wsel[kk] / wsum * ROUTED_SCALE


def _router(x2, w_router, router_bias):
    n = x2.shape[0]
    tn = 512 if n % 512 == 0 else n
    wr_t = w_router.T.astype(BF16)
    bias = router_bias.reshape(N_EXPERTS, 1).astype(F32)
    kspec = pl.BlockSpec((TOP_K, tn), lambda i: (0, i))
    return pl.pallas_call(
        _router_kernel,
        grid=(n // tn,),
        in_specs=[pl.BlockSpec((tn, D_MODEL), lambda i: (i, 0)),
                  pl.BlockSpec(wr_t.shape, lambda i: (0, 0)),
                  pl.BlockSpec(bias.shape, lambda i: (0, 0))],
        out_specs=[kspec, kspec, kspec, pl.BlockSpec((N_EXPERTS, 1), lambda i: (0, 0)),
                   pl.BlockSpec((8, N_EXPERTS), lambda i: (0, 0))],
        out_shape=[jax.ShapeDtypeStruct((TOP_K, n), jnp.int32),
                   jax.ShapeDtypeStruct((TOP_K, n), jnp.int32),
                   jax.ShapeDtypeStruct((TOP_K, n), F32),
                   jax.ShapeDtypeStruct((N_EXPERTS, 1), F32),
                   jax.ShapeDtypeStruct((8, N_EXPERTS), F32)],
        scratch_shapes=[pltpu.VMEM((N_EXPERTS, 1), F32), pltpu.VMEM((8, N_EXPERTS), F32)],
        compiler_params=_params(1),
        name="router",
    )(x2, wr_t, bias)


def _slots_kernel(e_ref, rank_ref, cnt_ref, cntrow_ref, dest_ref, blk_e_ref, valid_ref, *, blk, n_blocks):
    pad = lambda c: jnp.ceil(c / blk) * blk
    cnt = cnt_ref[...]
    padded = pad(cnt)
    padded_row = pad(cntrow_ref[0:1, :])
    r_i = lax.broadcasted_iota(jnp.int32, (N_EXPERTS, N_EXPERTS), 0)
    c_i = lax.broadcasted_iota(jnp.int32, (N_EXPERTS, N_EXPERTS), 1)
    start = jnp.sum(jnp.where(c_i < r_i, padded_row, 0.0), axis=1, keepdims=True)
    end = start + padded
    e = e_ref[...]
    dest = rank_ref[...]
    for ex in range(N_EXPERTS):
        dest = dest + jnp.where(e == ex, start[ex:ex + 1, :].astype(jnp.int32), 0)
    dest_ref[...] = dest
    bstart = (lax.broadcasted_iota(jnp.int32, (1, n_blocks), 1) * blk).astype(F32)
    owner = jnp.sum(jnp.where(end <= bstart, 1.0, 0.0), axis=0, keepdims=True)
    blk_e_ref[...] = jnp.minimum(owner, N_EXPERTS - 1.0).astype(jnp.int32)
    inside = (start <= bstart) & (bstart < end)
    real = jnp.clip(start + cnt - bstart, 0.0, float(blk))
    valid_ref[...] = jnp.sum(jnp.where(inside, real, 0.0), axis=0, keepdims=True).astype(jnp.int32)


def _slots(e_k, rank_k, counts, counts_row, blk, n_blocks):
    n = e_k.shape[1]
    full = lambda shape: pl.BlockSpec(shape, lambda: (0,) * len(shape))
    return pl.pallas_call(
        functools.partial(_slots_kernel, blk=blk, n_blocks=n_blocks),
        in_specs=[full((TOP_K, n)), full((TOP_K, n)), full((N_EXPERTS, 1)), full((8, N_EXPERTS))],
        out_specs=[full((TOP_K, n)), full((1, n_blocks)), full((1, n_blocks))],
        out_shape=[jax.ShapeDtypeStruct((TOP_K, n), jnp.int32),
                   jax.ShapeDtypeStruct((1, n_blocks), jnp.int32),
                   jax.ShapeDtypeStruct((1, n_blocks), jnp.int32)],
        compiler_params=pltpu.CompilerParams(vmem_limit_bytes=VMEM_LIMIT),
        name="slots",
    )(e_k, rank_k, counts, counts_row)


def _sc_worker_base(per_worker):
    return (lax.axis_index("s") * SC_CORES + lax.axis_index("c")) * per_worker


def _sc_scatter_rows(rows, idx, n_out):
    n, width = rows.shape
    k_lists = idx.shape[0] // n
    workers = SC_CORES * SC_SUBCORES
    per_worker = n // workers
    assert per_worker * workers == n and per_worker % SC_CHUNK == 0
    mesh = plsc.VectorSubcoreMesh(core_axis_name="c", subcore_axis_name="s")

    @functools.partial(
        pl.kernel, mesh=mesh,
        out_type=jax.ShapeDtypeStruct((n_out, width), rows.dtype),
        scratch_types=[pltpu.VMEM((SC_CHUNK, width), rows.dtype)]
                      + [pltpu.VMEM((SC_CHUNK,), jnp.int32)] * k_lists + [pltpu.SemaphoreType.DMA] * 3,
        name="sc_scatter")
    def scatter(rows_hbm, idx_hbm, out_hbm, rows_v, *rest):
        idx_vs = rest[:k_lists]
        sem_rows, sem_idx, sem_out = rest[k_lists:]
        base = _sc_worker_base(per_worker)

        @pl.loop(0, per_worker // SC_CHUNK)
        def _(ci):
            off = pl.multiple_of(base + ci * SC_CHUNK, SC_CHUNK)
            loads = [pltpu.async_copy(rows_hbm.at[pl.ds(off, SC_CHUNK)], rows_v, sem_rows)]
            loads += [pltpu.async_copy(idx_hbm.at[pl.ds(pl.multiple_of(k * n + off, SC_CHUNK), SC_CHUNK)],
                                       idx_vs[k], sem_idx) for k in range(k_lists)]
            for c in loads:
                c.wait()
            copies = [pltpu.async_copy(rows_v, out_hbm.at[idx_vs[k]], sem_out) for k in range(k_lists)]
            for c in copies:
                c.wait()

    return scatter(rows, idx)


def _experts_kernel(blk_e_ref, valid_ref, xs_ref, wg_ref, wu_ref, wd_ref, y_ref, wg_b, wu_b, wd_b):
    i = pl.program_id(0)
    valid = valid_ref[i]

    @pl.when((i == 0) | (blk_e_ref[i] != blk_e_ref[jnp.maximum(i - 1, 0)]))
    def _():
        wg_b[...] = wg_ref[...].astype(BF16)
        wu_b[...] = wu_ref[...].astype(BF16)
        wd_b[...] = wd_ref[...].astype(BF16)

    @pl.when(valid > 0)
    def _():
        half = D_MODEL // 2
        row = lax.broadcasted_iota(jnp.int32, (xs_ref.shape[0], 1), 0)
        hi, lo = (v.astype(BF16) for v in _unpack_halves(jnp.where(row < valid, xs_ref[...], jnp.uint32(0))))
        gate = _dot(hi, wg_b[:half, :]) + _dot(lo, wg_b[half:, :])
        up = _dot(hi, wu_b[:half, :]) + _dot(lo, wu_b[half:, :])
        y_ref[...] = _pack_halves(_dot((jax.nn.silu(gate) * up).astype(BF16), wd_b[...]))

    @pl.when(valid <= 0)
    def _():
        y_ref[...] = jnp.zeros_like(y_ref)


def _experts(blk_e, valid, xs, w_gate, w_up, w_down, blk, after=None):
    cap, width = xs.shape
    wspec = lambda a: pl.BlockSpec((None,) + a.shape[1:], lambda i, be, nv: (be[i], 0, 0))
    rows = pl.BlockSpec((blk, width), lambda i, be, nv: (i, 0))
    args = [blk_e, valid, xs, w_gate, w_up, w_down]
    in_specs = [rows, wspec(w_gate), wspec(w_up), wspec(w_down)]
    body = _add_anchors(_experts_kernel, args, in_specs, after)
    return pl.pallas_call(
        body,
        grid_spec=pltpu.PrefetchScalarGridSpec(
            num_scalar_prefetch=2,
            grid=(cap // blk,),
            in_specs=in_specs,
            out_specs=rows,
            scratch_shapes=[pltpu.VMEM(w.shape[1:], BF16) for w in (w_gate, w_up, w_down)],
        ),
        out_shape=jax.ShapeDtypeStruct(xs.shape, xs.dtype),
        compiler_params=_params(1),
        name="experts",
    )(*args)


def _sc_gather_rows(table, idx):
    b, width = idx.shape[0], table.shape[1]
    workers = SC_CORES * SC_SUBCORES
    per_worker = b // workers
    assert per_worker * workers == b and per_worker % (SC_CHUNK * SC_INFLIGHT) == 0
    mesh = plsc.VectorSubcoreMesh(core_axis_name="c", subcore_axis_name="s")

    @functools.partial(
        pl.kernel, mesh=mesh,
        out_type=jax.ShapeDtypeStruct((b, width), table.dtype),
        scratch_types=[pltpu.VMEM((SC_CHUNK,), jnp.int32)] * SC_INFLIGHT
                      + [pltpu.VMEM((SC_CHUNK, width), table.dtype)] * SC_INFLIGHT
                      + [pltpu.SemaphoreType.DMA] * (1 + 2 * SC_INFLIGHT),
        name="sc_gather")
    def gather(table_hbm, idx_hbm, out_hbm, *scratch):
        idx_vs = scratch[:SC_INFLIGHT]
        rows_vs = scratch[SC_INFLIGHT:2 * SC_INFLIGHT]
        sem_idx = scratch[2 * SC_INFLIGHT]
        sem_rows = scratch[2 * SC_INFLIGHT + 1:3 * SC_INFLIGHT + 1]
        sem_out = scratch[3 * SC_INFLIGHT + 1:]
        base = _sc_worker_base(per_worker)
        lanes = range(SC_INFLIGHT)

        @pl.loop(0, per_worker // (SC_CHUNK * SC_INFLIGHT))
        def _(gi):
            offs = [pl.multiple_of(base + (gi * SC_INFLIGHT + j) * SC_CHUNK, SC_CHUNK) for j in lanes]
            loads = [pltpu.async_copy(idx_hbm.at[pl.ds(offs[j], SC_CHUNK)], idx_vs[j], sem_idx) for j in lanes]
            for c in loads:
                c.wait()
            gathers = [pltpu.async_copy(table_hbm.at[idx_vs[j]], rows_vs[j], sem_rows[j]) for j in lanes]
            writes = []
            for j in lanes:
                gathers[j].wait()
                writes.append(pltpu.async_copy(rows_vs[j], out_hbm.at[pl.ds(offs[j], SC_CHUNK)], sem_out[j]))
            for c in writes:
                c.wait()

    return gather(table, idx)


def _combine_kernel(x_ref, wk_ref, yk_ref, wsg_ref, wsu_ref, wsd_ref, g_ref, b_ref, *rest):
    o_ref = rest[-1]
    x = x_ref[...]
    xb = x.astype(BF16)
    shared = _dot((jax.nn.silu(_dot(xb, wsg_ref[...])) * _dot(xb, wsu_ref[...])).astype(BF16), wsd_ref[...])
    wk = wk_ref[...]
    routed_hi = routed_lo = None
    for kk in range(TOP_K):
        hi, lo = _unpack_halves(yk_ref[kk])
        w = wk[:, kk:kk + 1]
        routed_hi = hi * w if kk == 0 else routed_hi + hi * w
        routed_lo = lo * w if kk == 0 else routed_lo + lo * w
    routed = jnp.concatenate([routed_hi, routed_lo], axis=1)
    o_ref[...] = _layer_norm(DN_ALPHA * x + (routed + shared), g_ref[...], b_ref[...])


def _combine(x2, w_tok, yk, ws_gate, ws_up, ws_down, ln3_g, ln3_b, row0, n_total, out_prev, after=None):
    n = x2.shape[0]
    tt = 512 if n % 512 == 0 and row0 % 512 == 0 else n
    blk0 = row0 // tt
    ws = [ws_gate.astype(BF16), ws_up.astype(BF16), ws_down.astype(BF16)]
    vecs = [ln3_g.reshape(1, D_MODEL), ln3_b.reshape(1, D_MODEL)]
    full = lambda a: pl.BlockSpec(a.shape, lambda i: (0,) * a.ndim)
    args = [x2, w_tok, yk, *ws, *vecs]
    in_specs = ([pl.BlockSpec((tt, D_MODEL), lambda i: (i, 0)),
                 pl.BlockSpec((tt, TOP_K), lambda i: (i, 0)),
                 pl.BlockSpec((TOP_K, tt, D_MODEL // 2), lambda i: (0, i, 0))]
                + [full(a) for a in ws] + [full(v) for v in vecs])
    aliases = {}
    for anchor in (a for a in (after or ()) if a is not None):
        args.append(anchor)
        in_specs.append(pl.BlockSpec(memory_space=pl.ANY))
    if out_prev is not None:
        aliases = {len(args): 0}
        args.append(out_prev)
        in_specs.append(pl.BlockSpec(memory_space=pl.ANY))
    return pl.pallas_call(
        _combine_kernel,
        grid=(n // tt,),
        in_specs=in_specs,
        out_specs=pl.BlockSpec((tt, D_MODEL), lambda i: (blk0 + i, 0)),
        out_shape=jax.ShapeDtypeStruct((n_total, D_MODEL), F32),
        input_output_aliases=aliases,
        compiler_params=_params(1),
        name="combine",
    )(*args)


EXPERT_BLOCK = 512


def _moe_dispatch(x2, x2p, w_router, router_bias):
    n = x2.shape[0]
    cap = n * TOP_K + N_EXPERTS * EXPERT_BLOCK
    e_k, rank_k, w_k, counts, counts_row = _router(x2, w_router, router_bias)
    dest, blk_e, valid = _slots(e_k, rank_k, counts, counts_row, EXPERT_BLOCK, cap // EXPERT_BLOCK)
    dest = dest.reshape(-1)
    return dict(w_tok=w_k.T, dest=dest, blk_e=blk_e.reshape(-1), valid=valid.reshape(-1),
                xs=_sc_scatter_rows(x2p, dest, cap))


def _moe_experts(routed, w_gate, w_up, w_down, after):
    y = _experts(routed["blk_e"], routed["valid"], routed["xs"], w_gate, w_up, w_down, EXPERT_BLOCK, after=after)
    n = routed["dest"].shape[0] // TOP_K
    return y, _sc_gather_rows(y, routed["dest"]).reshape(TOP_K, n, D_MODEL // 2)


def _layer(x, mem, positions, w_in, cmp_pe_k, cmp_pe_v, cmp_w1_k, cmp_w2_k, cmp_w1_v, cmp_w2_v,
           w_out, ln1_g, ln1_b, w_xq, w_xkv, w_xo, ln2_g, ln2_b, w_router, router_bias,
           w_gate, w_up, w_down, ws_gate, ws_up, ws_down, ln3_g, ln3_b):
    batch, seq, _ = x.shape
    n_total = batch * seq
    x2d = x.reshape(n_total, D_MODEL)
    pos_col = positions.astype(F32).reshape(n_total, 1)
    kvx = _memkv(mem.reshape(batch * MEM_LEN, D_MODEL), w_xkv)
    last = max(1, batch // 4)
    sizes = [batch - last, last] if batch > 1 else [batch]
    starts = [sum(sizes[:g]) for g in range(len(sizes))]

    def mixers_in(g):
        nb, row0 = sizes[g], starts[g] * seq
        (rq, rk, rv, rg, nq, nqr, kc, vc, ks, vs, kw, vw, gates) = _inproj(x2d, pos_col, w_in, row0, nb * seq)
        o_ret = _retention(rq, rk, rv, rg, nb, seq)
        kcmp = _compress(kc, cmp_pe_k, cmp_w1_k, cmp_w2_k, nb, seq)
        vcmp = _compress(vc, cmp_pe_v, cmp_w1_v, cmp_w2_v, nb, seq)
        return o_ret, (nq, nqr, gates, kcmp, vcmp, ks, vs, kw, vw)

    def attend(g, nsa_args, after):
        return _nsa(*nsa_args, sizes[g], seq, after=after)

    def mix_and_route(g, o_ret, o_nsa, after):
        x2, x2p = _postmix(x2d, o_ret, o_nsa, kvx, w_out, w_xq, w_xo, ln1_g, ln1_b, ln2_g, ln2_b,
                           starts[g], sizes[g], seq, after=after)
        return x2, _moe_dispatch(x2, x2p, w_router, router_bias)

    def combine(g, x2, routed, yk, out_prev, after):
        return _combine(x2, routed["w_tok"], yk, ws_gate, ws_up, ws_down, ln3_g, ln3_b, starts[g] * seq,
                        n_total, out_prev, after=after)

    out = None
    o_ret, nsa_args = mixers_in(0)
    x2, routed = mix_and_route(0, o_ret, attend(0, nsa_args, None), None)
    for g in range(1, len(sizes)):
        o_ret, nsa_args = mixers_in(g)
        y, yk = _moe_experts(routed, w_gate, w_up, w_down, after=[o_ret, out])
        o_nsa = attend(g, nsa_args, [y])
        x2_next, routed_next = mix_and_route(g, o_ret, o_nsa, [yk])
        out = combine(g - 1, x2, routed, yk, out, [routed_next["dest"]])
        x2, routed = x2_next, routed_next
    y, yk = _moe_experts(routed, w_gate, w_up, w_down, after=[out])
    out = combine(len(sizes) - 1, x2, routed, yk, out, None)
    return out.reshape(batch, seq, D_MODEL)


def kernel(x, mem, positions, w_in, cmp_pe_k, cmp_pe_v, cmp_w1_k, cmp_w2_k, cmp_w1_v, cmp_w2_v, w_out, ln1_g, ln1_b, w_xq, w_xkv, w_xo, ln2_g, ln2_b, w_router, router_bias, w_gate, w_up, w_down, ws_gate, ws_up, ws_down, ln3_g, ln3_b):
    for l in range(DEPTH):
        x = _layer(x, mem, positions, w_in[l], cmp_pe_k[l], cmp_pe_v[l], cmp_w1_k[l], cmp_w2_k[l],
                   cmp_w1_v[l], cmp_w2_v[l], w_out[l], ln1_g[l], ln1_b[l], w_xq[l], w_xkv[l],
                   w_xo[l], ln2_g[l], ln2_b[l], w_router[l], router_bias[l], w_gate[l], w_up[l],
                   w_down[l], ws_gate[l], ws_up[l], ws_down[l], ln3_g[l], ln3_b[l])
    return x
```

```python
import functools

import numpy as np
import jax
import jax.numpy as jnp
from jax import lax
from jax.experimental import pallas as pl
from jax.experimental.pallas import tpu as pltpu
from jax.experimental.pallas import tpu_sc as plsc

D_MODEL = 1024
MEM_LEN = 256
DEPTH = 1
DN_ALPHA = (2 * DEPTH) ** 0.25
LN_EPS = 1e-5
NEG = -1e30
FORCE = 1e9

RET_HEADS = 4
RET_DIM = 128
RET_CHUNK = 128
RET_ROPE_BASE = 10000.0
RET_STEP_CHUNKS = 4
RET_WIDTH = RET_HEADS * RET_DIM

NSA_HEADS = 8
NSA_KV_GROUPS = 2
NSA_HPG = NSA_HEADS // NSA_KV_GROUPS
NSA_DIM = 64
NSA_WIDTH = NSA_HEADS * NSA_DIM
KV_WIDTH = NSA_KV_GROUPS * NSA_DIM
CMP_LEN = 32
CMP_STRIDE = 16
CMP_HIDDEN = 256
SEL_LEN = 64
SEL_SHIFT = 6
SEL_TOPK = 16
WIN = 512
ROPE_THETA = 500000.0
ROPE_DIMS = NSA_DIM // 4
GATE_LANES = 16
NSA_CHAINS = 1
SUM_ROWS = 16
WIN_PART = 256

SC_CORES = 2
SC_SUBCORES = 16
SC_CHUNK = 64
SC_INFLIGHT = 2

XATT_HEADS = 4
XATT_DIM = D_MODEL // XATT_HEADS

N_EXPERTS = 64
TOP_K = 8
N_GROUPS = 8
TOPK_GROUPS = 4
EXPERT_FF = 256
SHARED_FF = 256
ROUTED_SCALE = 2.5

LANES = 128
VMEM_LIMIT = 56 * 1024 * 1024

F32 = jnp.float32
BF16 = jnp.bfloat16
NT_DIMS = (((1,), (1,)), ((), ()))


def _params(n_axes):
    return pltpu.CompilerParams(dimension_semantics=("arbitrary",) * n_axes,
                                vmem_limit_bytes=VMEM_LIMIT)


def _dot(a, b):
    return jnp.dot(a, b, preferred_element_type=F32)


def _dot_nt(a, b):
    return lax.dot_general(a, b, NT_DIMS, preferred_element_type=F32)


def _layer_norm(v, g, b):
    mu = jnp.mean(v, axis=-1, keepdims=True)
    d = v - mu
    var = jnp.mean(d * d, axis=-1, keepdims=True)
    return d * lax.rsqrt(var + LN_EPS) * g + b


def _inproj_kernel(x_ref, pos_ref, wret_ref, wnq_ref, wkv_ref, wg_ref, invr_ref, invn_ref,
                   rq_ref, rk_ref, rv_ref, rg_ref, nq_ref, nqr_ref, kc_ref, vc_ref,
                   ks_ref, vs_ref, kw_ref, vw_ref, gate_ref):
    xb = x_ref[...].astype(BF16)
    pos = pos_ref[...]
    lane = lax.broadcasted_iota(jnp.int32, (1, LANES), 1)

    ang = pos * invr_ref[...]
    cos_r = jnp.cos(ang)
    sin_r = jnp.sin(ang)
    sin_r = jnp.where(lane < RET_DIM // 2, -sin_r, sin_r)
    q_all = _dot(xb, wret_ref[:, :RET_WIDTH])
    k_all = _dot(xb, wret_ref[:, RET_WIDTH:2 * RET_WIDTH])
    for h in range(RET_HEADS):
        cols = slice(h * RET_DIM, (h + 1) * RET_DIM)
        q = q_all[:, cols]
        rq_ref[:, cols] = (q * cos_r + pltpu.roll(q, RET_DIM // 2, 1) * sin_r).astype(BF16)
        k = k_all[:, cols]
        k = (k * cos_r + pltpu.roll(k, RET_DIM // 2, 1) * sin_r) * (RET_DIM ** -0.5)
        rk_ref[:, cols] = k.astype(BF16)
    rv_ref[...] = _dot(xb, wret_ref[:, 2 * RET_WIDTH:3 * RET_WIDTH]).astype(BF16)
    rg_ref[...] = _dot(xb, wret_ref[:, 3 * RET_WIDTH:4 * RET_WIDTH]).astype(BF16)

    half = ROPE_DIMS // 2
    j = lane % NSA_DIM
    angn = pos * invn_ref[...]
    cos_n = jnp.cos(angn)
    sin_n = jnp.sin(angn)
    sin_lo = jnp.where(j < half, -sin_n, 0.0)
    sin_hi = jnp.where((j >= half) & (j < 2 * half), sin_n, 0.0)

    def rope_n(v):
        return v * cos_n + pltpu.roll(v, half, 1) * sin_hi + pltpu.roll(v, LANES - half, 1) * sin_lo

    scale = NSA_DIM ** -0.5
    nq_all = _dot(xb, wnq_ref[...])
    for c in range(NSA_WIDTH // LANES):
        cols = slice(c * LANES, (c + 1) * LANES)
        q = nq_all[:, cols]
        nq_ref[:, cols] = (q * scale).astype(BF16)
        nqr_ref[:, cols] = (rope_n(q) * scale).astype(BF16)

    kv_all = _dot(xb, wkv_ref[...])

    def kv(i):
        return kv_all[:, i * KV_WIDTH:(i + 1) * KV_WIDTH]

    def split_groups(ref, v):
        for g in range(NSA_KV_GROUPS):
            ref[g] = v[:, g * NSA_DIM:(g + 1) * NSA_DIM].astype(BF16)

    kc_ref[...] = kv(0)
    vc_ref[...] = kv(1)
    split_groups(ks_ref, rope_n(kv(2)))
    split_groups(vs_ref, kv(3))
    split_groups(kw_ref, rope_n(kv(4)))
    split_groups(vw_ref, kv(5))

    gt = jax.nn.sigmoid(_dot_nt(wg_ref[...], xb))
    for g in range(NSA_KV_GROUPS):
        gate_ref[g] = gt[g * GATE_LANES:(g + 1) * GATE_LANES, :]


def _inproj(x2d, pos_col, w_in, row0, n):
    tm = 512 if n % 512 == 0 and row0 % 512 == 0 else n
    blk0 = row0 // tm
    off = np.cumsum([0] + [RET_WIDTH] * 4 + [NSA_WIDTH] + [KV_WIDTH] * 6)
    w_ret = w_in[:, :off[4]].astype(BF16)
    w_nq = w_in[:, off[4]:off[5]].astype(BF16)
    w_kv = w_in[:, off[5]:off[11]].astype(BF16)
    wg = w_in[:, off[11]:].reshape(D_MODEL, NSA_KV_GROUPS, NSA_HPG * 3)
    wg = jnp.pad(wg, ((0, 0), (0, 0), (0, GATE_LANES - NSA_HPG * 3)))
    wg = wg.reshape(D_MODEL, NSA_KV_GROUPS * GATE_LANES).T.astype(BF16)

    lane = np.arange(LANES)
    half_r = RET_DIM // 2
    inv_r = (np.float32(RET_ROPE_BASE) ** (-np.arange(half_r, dtype=np.float32) / np.float32(half_r)))
    inv_r = inv_r.astype(np.float32)[lane % half_r][None, :]
    half_n = ROPE_DIMS // 2
    inv_n = (np.float32(ROPE_THETA) ** (-np.arange(half_n, dtype=np.float32) / np.float32(half_n)))
    jn = lane % NSA_DIM
    inv_n = np.where(jn < ROPE_DIMS, inv_n.astype(np.float32)[jn % half_n], np.float32(0.0))[None, :]

    row = lambda w: pl.BlockSpec((tm, w), lambda i: (i, 0))
    src_row = lambda w: pl.BlockSpec((tm, w), lambda i: (blk0 + i, 0))
    full = lambda a: pl.BlockSpec(a.shape, lambda i: (0,) * a.ndim)
    grp = lambda w: pl.BlockSpec((NSA_KV_GROUPS, tm, w), lambda i: (0, i, 0))
    bf = lambda w: jax.ShapeDtypeStruct((n, w), BF16)
    gbf = jax.ShapeDtypeStruct((NSA_KV_GROUPS, n, NSA_DIM), BF16)
    inv_r = jnp.asarray(inv_r, F32)
    inv_n = jnp.asarray(inv_n, F32)
    return pl.pallas_call(
        _inproj_kernel,
        grid=(n // tm,),
        in_specs=[src_row(D_MODEL), src_row(1), full(w_ret), full(w_nq), full(w_kv), full(wg),
                  full(inv_r), full(inv_n)],
        out_specs=[row(RET_WIDTH)] * 4 + [row(NSA_WIDTH)] * 2 + [row(KV_WIDTH)] * 2
                  + [grp(NSA_DIM)] * 4
                  + [pl.BlockSpec((NSA_KV_GROUPS, GATE_LANES, tm), lambda i: (0, 0, i))],
        out_shape=[bf(RET_WIDTH)] * 4 + [bf(NSA_WIDTH)] * 2
                  + [jax.ShapeDtypeStruct((n, KV_WIDTH), F32)] * 2 + [gbf] * 4
                  + [jax.ShapeDtypeStruct((NSA_KV_GROUPS, GATE_LANES, n), F32)],
        compiler_params=_params(1),
        name="inproj",
    )(x2d, pos_col, w_ret, w_nq, w_kv, wg, inv_r, inv_n)


def _retention_kernel(q_ref, k_ref, v_ref, g_ref, o_ref, state_ref):
    c = RET_CHUNK

    @pl.when(pl.program_id(1) == 0)
    def _():
        state_ref[...] = jnp.zeros_like(state_ref)

    row = lax.broadcasted_iota(jnp.int32, (c, c), 0)
    col = lax.broadcasted_iota(jnp.int32, (c, c), 1)
    rel = (row - col).astype(F32)
    idx = lax.broadcasted_iota(jnp.int32, (c, 1), 0).astype(F32)
    for h in range(RET_HEADS):
        log_g = float(np.log(np.float32(1.0) - np.float32(2.0) ** np.float32(-5.0 - h)))
        cols = slice(h * RET_DIM, (h + 1) * RET_DIM)
        dmask = jnp.where(rel >= 0, jnp.exp(log_g * jnp.maximum(rel, 0.0)), 0.0)
        zeta = jnp.exp(log_g * (c - 1.0 - idx))
        xi = jnp.exp(log_g * (idx + 1.0))
        for j in range(q_ref.shape[0] // c):
            rows = slice(j * c, (j + 1) * c)
            q = q_ref[rows, cols]
            k = k_ref[rows, cols]
            v = v_ref[rows, cols]
            scores = _dot_nt(q, k) * dmask
            inner = _dot(scores.astype(BF16), v)
            prev = state_ref[h]
            cross = _dot(q, prev.astype(BF16)) * xi
            kz = (k.astype(F32) * zeta).astype(BF16)
            kv = lax.dot_general(kz, v, (((0,), (0,)), ((), ())), preferred_element_type=F32)
            state_ref[h] = prev * float(np.exp(np.float32(log_g) * np.float32(c))) + kv
            o = inner + cross
            mu = jnp.mean(o, axis=-1, keepdims=True)
            d = o - mu
            var = jnp.mean(d * d, axis=-1, keepdims=True)
            o = d * lax.rsqrt(var + LN_EPS)
            o_ref[rows, cols] = (jax.nn.silu(g_ref[rows, cols].astype(F32)) * o).astype(BF16)


def _retention(rq, rk, rv, rg, batch, seq):
    per_step = RET_STEP_CHUNKS if (seq // RET_CHUNK) % RET_STEP_CHUNKS == 0 else 1
    nc = seq // (RET_CHUNK * per_step)
    spec = pl.BlockSpec((RET_CHUNK * per_step, RET_WIDTH), lambda b, n: (b * nc + n, 0))
    return pl.pallas_call(
        _retention_kernel,
        grid=(batch, nc),
        in_specs=[spec] * 4,
        out_specs=spec,
        out_shape=jax.ShapeDtypeStruct(rq.shape, BF16),
        scratch_shapes=[pltpu.VMEM((RET_HEADS, RET_DIM, RET_DIM), F32)],
        compiler_params=_params(2),
        name="retention",
    )(rq, rk, rv, rg)


def _compress_kernel(a_ref, pe_ref, w1_ref, w2_ref, o_ref, shift_ref, *, n_cmp):
    rows = a_ref.shape[0]
    a = a_ref[...]
    lo = (a + pe_ref[0]).astype(BF16)
    hi = (a + pe_ref[1]).astype(BF16)
    ridx = lax.broadcasted_iota(jnp.int32, (rows, 1), 0)
    shift_ref[rows:rows + 8, :] = jnp.zeros((8, CMP_HIDDEN), F32)
    for g in range(NSA_KV_GROUPS):
        p = _dot(lo, w1_ref[0, g])
        shift_ref[0:rows, :] = _dot(hi, w1_ref[1, g])
        hid = jax.nn.silu(p + shift_ref[pl.ds(1, rows), :])
        out = _dot(hid.astype(BF16), w2_ref[...])
        o_ref[g] = jnp.where(ridx < n_cmp, out, 0.0).astype(BF16)


def _compress(a, pe, w1, w2, batch, seq):
    rows = seq // CMP_STRIDE
    per = CMP_STRIDE * KV_WIDTH
    n_cmp = (seq - CMP_LEN) // CMP_STRIDE + 1
    a2 = a.reshape(batch * rows, per)
    pe2 = jnp.tile(pe.reshape(2, CMP_STRIDE, 1, NSA_DIM), (1, 1, NSA_KV_GROUPS, 1)).reshape(2, 1, per)
    w1r = w1.reshape(2, CMP_STRIDE, 1, NSA_DIM, CMP_HIDDEN)
    eye = jnp.eye(NSA_KV_GROUPS, dtype=w1.dtype).reshape(1, NSA_KV_GROUPS, 1, NSA_KV_GROUPS, 1, 1)
    w1x = (w1r[:, None] * eye).reshape(2, NSA_KV_GROUPS, per, CMP_HIDDEN).astype(BF16)
    w2b = w2.astype(BF16)
    full = lambda arr: pl.BlockSpec(arr.shape, lambda b: (0,) * arr.ndim)
    return pl.pallas_call(
        functools.partial(_compress_kernel, n_cmp=n_cmp),
        grid=(batch,),
        in_specs=[pl.BlockSpec((rows, per), lambda b: (b, 0)), full(pe2), full(w1x), full(w2b)],
        out_specs=pl.BlockSpec((None, NSA_KV_GROUPS, rows, NSA_DIM), lambda b: (b, 0, 0, 0)),
        out_shape=jax.ShapeDtypeStruct((batch, NSA_KV_GROUPS, rows, NSA_DIM), BF16),
        scratch_shapes=[pltpu.VMEM((rows + 8, CMP_HIDDEN), F32)],
        compiler_params=_params(1),
        name="compress",
    )(a2, pe2, w1x, w2b)


def _heads_to_lanes(ref):
    vt = ref[...].astype(F32).T
    return jnp.concatenate([vt[h * NSA_DIM:(h + 1) * NSA_DIM] for h in range(NSA_HPG)], axis=1).astype(BF16)


def _tile_heads(v):
    return jnp.concatenate([v] * NSA_HPG, axis=1)


def _transpose_into(dst_ref, src_ref, chunk):
    def step(c, _):
        c0 = pl.multiple_of(c * chunk, chunk)
        dst_ref[:NSA_DIM, pl.ds(c0, chunk)] = src_ref[pl.ds(c0, chunk), :].astype(F32).T.astype(BF16)
        return 0
    lax.fori_loop(0, src_ref.shape[0] // chunk, step, 0)


def _nsa_kernel(qraw_ref, qrot_ref, gate_ref, kcmp_ref, vcmp_ref, ovt_ref,
                ks_ref, vs_ref, kw_ref, vw_ref, o_ref, vst_ref, vwt_ref, vct_ref, bias_ref, *, tq, tk, seq):
    i = pl.program_id(2)
    t0 = i * tq
    cols = NSA_HPG * tq
    n_sel = seq // SEL_LEN
    n_cmp_rows = seq // CMP_STRIDE
    blocks_per_tile = tk // SEL_LEN

    @pl.when(i == 0)
    def _():
        chunk = min(512, n_cmp_rows)
        _transpose_into(vst_ref, vs_ref, chunk)
        _transpose_into(vwt_ref, vw_ref, chunk)
        _transpose_into(vct_ref, vcmp_ref, chunk)
        vst_ref[NSA_DIM:, :] = jnp.ones((SUM_ROWS, seq), BF16)
        vwt_ref[NSA_DIM:, :] = jnp.ones((SUM_ROWS, seq), BF16)

    def split_sum(acc):
        return acc[:NSA_DIM] / acc[NSA_DIM:NSA_DIM + 1]

    q_raw = _heads_to_lanes(qraw_ref)
    q_rot = _heads_to_lanes(qrot_ref)
    t_row = t0 + lax.broadcasted_iota(jnp.int32, (1, tq), 1)

    chain_w = cols // NSA_CHAINS
    heads_per_chain = chain_w // tq
    chains = [slice(c * chain_w, (c + 1) * chain_w) for c in range(NSA_CHAINS)]
    tile_chain = lambda v: jnp.concatenate([v] * heads_per_chain, axis=1)

    pw = min(tq, WIN_PART)
    parts = []
    for u in range(tq // pw):
        span = WIN + pw
        ws = pl.multiple_of(jnp.maximum(t0 + u * pw - WIN, 0), pw)
        dist = t_row[:, u * pw:(u + 1) * pw] - (ws + lax.broadcasted_iota(jnp.int32, (span, 1), 0))
        bias_w = jnp.concatenate([jnp.where((dist >= 0) & (dist < WIN), 0.0, NEG)] * NSA_HPG, axis=1)
        q_part = jnp.concatenate([q_rot[:, h * tq + u * pw:h * tq + (u + 1) * pw] for h in range(NSA_HPG)],
                                 axis=1)
        s_w = _dot(kw_ref[pl.ds(ws, span), :], q_part) + bias_w
        p_w = jnp.exp((s_w - jnp.max(s_w, axis=0, keepdims=True)).astype(BF16))
        parts.append(split_sum(_dot(vwt_ref[:, pl.ds(ws, span)], p_w)))
    o_w = jnp.concatenate([parts[u][:, h * pw:(h + 1) * pw]
                           for h in range(NSA_HPG) for u in range(tq // pw)], axis=1)

    c_idx = lax.broadcasted_iota(jnp.int32, (n_cmp_rows, 1), 0)
    valid = tile_chain(jnp.where(c_idx * CMP_STRIDE + (CMP_LEN - 1) <= t_row, 1.0, 0.0))
    bias_c = (valid - 1.0) * (-NEG)
    o_c = []
    p_sum = None
    for c in chains:
        s_c = _dot(kcmp_ref[...], q_raw[:, c]) + bias_c
        e_c = jnp.exp(s_c - jnp.max(s_c, axis=0, keepdims=True)) * valid
        l_c = jnp.sum(e_c, axis=0, keepdims=True)
        p_c = e_c / jnp.where(l_c > 0.0, l_c, 1.0)
        o_c.append(_dot(vct_ref[...], p_c.astype(BF16)))
        for h in range(heads_per_chain):
            p_h = p_c[:, h * tq:(h + 1) * tq]
            p_sum = p_h if p_sum is None else p_sum + p_h
    o_c = jnp.concatenate(o_c, axis=1)

    p_hi = p_sum.astype(BF16)
    p_lo = (p_sum - p_hi.astype(F32)).astype(BF16)
    ovt = ovt_ref[...]
    imp = _dot(ovt, p_hi) + _dot(ovt, p_lo)
    jb = lax.broadcasted_iota(jnp.int32, (n_sel, tq), 0)
    cur = (t0 + lax.broadcasted_iota(jnp.int32, (n_sel, tq), 1)) >> SEL_SHIFT
    forced = (jb == 0) | (jb == cur) | (jb == cur - 1)
    work = jnp.where(forced, FORCE, imp)
    work = jnp.where(jb <= cur, work, NEG)
    sel_t = jnp.zeros((n_sel, tq), F32)
    for _ in range(min(SEL_TOPK, n_sel)):
        best = jnp.max(work, axis=0, keepdims=True)
        first = jnp.min(jnp.where(work == best, jb, n_sel), axis=0, keepdims=True)
        hit = jb == first
        sel_t = jnp.where(hit, 1.0, sel_t)
        work = jnp.where(hit, -jnp.inf, work)
    bias_ref[...] = jnp.where(sel_t > 0.5, 0.0, NEG)

    def sel_tile(kt, carry, causal):
        k0 = pl.multiple_of(kt * tk, tk)
        bias = jnp.concatenate(
            [jnp.broadcast_to(bias_ref[pl.ds(kt * blocks_per_tile + j, 1), :], (SEL_LEN, tq))
             for j in range(blocks_per_tile)], axis=0)
        if causal:
            kpos = k0 + lax.broadcasted_iota(jnp.int32, (tk, 1), 0)
            bias = jnp.where(kpos <= t_row, bias, NEG)
        bias = tile_chain(bias)
        k_t = ks_ref[pl.ds(k0, tk), :]
        v_t = vst_ref[:, pl.ds(k0, tk)]
        out = []
        scores = [_dot(k_t, q_rot[:, c]) + bias for c in chains]
        for (m, acc), s in zip(carry, scores):
            m_new = jnp.maximum(m, jnp.max(s, axis=0, keepdims=True))
            p = jnp.exp((s - m_new).astype(BF16))
            acc = jnp.exp(m - m_new) * acc + _dot(v_t, p)
            out.append((m_new, acc))
        return tuple(out)

    n_full = t0 // tk
    init = tuple((jnp.full((1, chain_w), NEG, F32), jnp.zeros((NSA_DIM + SUM_ROWS, chain_w), F32))
                 for _ in chains)
    carry = lax.fori_loop(0, n_full, functools.partial(sel_tile, causal=False), init)
    for d in range(max(tq // tk, 1)):
        carry = sel_tile(n_full + d, carry, causal=True)
    o_s = jnp.concatenate([split_sum(acc) for _, acc in carry], axis=1)

    gt = gate_ref[...]
    outs = []
    for h in range(NSA_HPG):
        c = slice(h * tq, (h + 1) * tq)
        outs.append(gt[3 * h:3 * h + 1] * o_c[:, c] + gt[3 * h + 1:3 * h + 2] * o_s[:, c]
                    + gt[3 * h + 2:3 * h + 3] * o_w[:, c])
    o_ref[...] = jnp.concatenate(outs, axis=0).T.astype(BF16)


def _anchored(kernel_fn, n_inputs, n_anchors):
    def body(*refs, **static):
        kernel_fn(*refs[:n_inputs], *refs[n_inputs + n_anchors:], **static)
    return body


def _add_anchors(kernel_fn, args, in_specs, after):
    after = [a for a in (after or ()) if a is not None]
    if not after:
        return kernel_fn
    body = _anchored(kernel_fn, len(args), len(after))
    args.extend(after)
    in_specs.extend([pl.BlockSpec(memory_space=pl.ANY)] * len(after))
    return body


def _nsa(nq, nqr, gates, kcmp, vcmp, ks, vs, kw, vw, batch, seq, after=None):
    n = batch * seq
    tq = 512
    tk = 512 if seq % 512 == 0 else seq
    nqb = seq // tq
    n_sel = seq // SEL_LEN
    rows_c = seq // CMP_STRIDE
    gw = NSA_HPG * NSA_DIM
    cs = np.arange(rows_c)[None, :] * CMP_STRIDE
    ss = np.arange(n_sel)[:, None] * SEL_LEN
    n_cmp = (seq - CMP_LEN) // CMP_STRIDE + 1
    ovt = ((cs < ss + SEL_LEN) & (cs + CMP_LEN > ss) & (np.arange(rows_c)[None, :] < n_cmp))
    ovt = jnp.asarray(ovt.astype(np.float32), BF16)

    qspec = pl.BlockSpec((tq, gw), lambda b, g, i: (b * nqb + i, g))
    cspec = pl.BlockSpec((None, None, rows_c, NSA_DIM), lambda b, g, i: (b, g, 0, 0))
    kspec = pl.BlockSpec((None, seq, NSA_DIM), lambda b, g, i: (g, b, 0))
    args = [nq, nqr, gates, kcmp, vcmp, ovt, ks, vs, kw, vw]
    in_specs = [qspec, qspec,
                pl.BlockSpec((None, GATE_LANES, tq), lambda b, g, i: (g, 0, b * nqb + i)),
                cspec, cspec, pl.BlockSpec(ovt.shape, lambda b, g, i: (0, 0)),
                kspec, kspec, kspec, kspec]
    body = functools.partial(_add_anchors(_nsa_kernel, args, in_specs, after), tq=tq, tk=tk, seq=seq)
    return pl.pallas_call(
        body,
        grid=(batch, NSA_KV_GROUPS, nqb),
        in_specs=in_specs,
        out_specs=qspec,
        out_shape=jax.ShapeDtypeStruct((n, NSA_WIDTH), BF16),
        scratch_shapes=[pltpu.VMEM((NSA_DIM + SUM_ROWS, seq), BF16), pltpu.VMEM((NSA_DIM + SUM_ROWS, seq), BF16),
                        pltpu.VMEM((NSA_DIM, rows_c), BF16), pltpu.VMEM((n_sel, tq), F32)],
        compiler_params=_params(3),
        name="nsa",
    )(*args)


def _memkv_kernel(mem_ref, w_ref, kv_ref):
    kv_ref[...] = _dot(mem_ref[...].astype(BF16), w_ref[...]).astype(BF16)


def _memkv(mem2d, w_xkv):
    n = mem2d.shape[0]
    w = w_xkv.astype(BF16)
    return pl.pallas_call(
        _memkv_kernel,
        grid=(n // MEM_LEN,),
        in_specs=[pl.BlockSpec((MEM_LEN, D_MODEL), lambda i: (i, 0)),
                  pl.BlockSpec(w.shape, lambda i: (0, 0))],
        out_specs=pl.BlockSpec((MEM_LEN, 2 * D_MODEL), lambda i: (i, 0)),
        out_shape=jax.ShapeDtypeStruct((n, 2 * D_MODEL), BF16),
        compiler_params=_params(1),
        name="memkv",
    )(mem2d, w)


def _pack_halves(v):
    half = D_MODEL // 2
    hi = pltpu.bitcast(v[:, :half].astype(BF16).astype(F32), jnp.uint32)
    lo = pltpu.bitcast(v[:, half:].astype(BF16).astype(F32), jnp.uint32)
    return hi | (lo >> 16)


def _unpack_halves(words):
    return pltpu.bitcast(words & jnp.uint32(0xFFFF0000), F32), pltpu.bitcast(words << 16, F32)


def _postmix_kernel(x_ref, oret_ref, onsa_ref, kv_ref, wout_ref, wq_ref, wo_ref,
                    g1_ref, b1_ref, g2_ref, b2_ref, x2_ref, x2p_ref):
    mixed = jnp.concatenate([oret_ref[...], onsa_ref[...]], axis=1)
    x1 = _layer_norm(DN_ALPHA * x_ref[...] + _dot(mixed, wout_ref[...]), g1_ref[...], b1_ref[...])
    q = (_dot(x1.astype(BF16), wq_ref[...]) * (XATT_DIM ** -0.5)).astype(BF16)
    heads = []
    for h in range(XATT_HEADS):
        cols = slice(h * XATT_DIM, (h + 1) * XATT_DIM)
        s = _dot_nt(q[:, cols], kv_ref[:, cols])
        m = jnp.max(s, axis=-1, keepdims=True)
        p = jnp.exp(s - m)
        l = jnp.sum(p, axis=-1, keepdims=True)
        heads.append(_dot(p.astype(BF16), kv_ref[:, D_MODEL + h * XATT_DIM:D_MODEL + (h + 1) * XATT_DIM]) / l)
    att = jnp.concatenate(heads, axis=1).astype(BF16)
    x2 = _layer_norm(DN_ALPHA * x1 + _dot(att, wo_ref[...]), g2_ref[...], b2_ref[...])
    x2_ref[...] = x2
    x2p_ref[...] = _pack_halves(x2)


def _postmix(x2d, o_ret, o_nsa, kvx, w_out, w_xq, w_xo, ln1_g, ln1_b, ln2_g, ln2_b, batch0, batch, seq,
             after=None):
    n = batch * seq
    tm = 512 if seq % 512 == 0 else seq
    per_b = seq // tm
    row = lambda w: pl.BlockSpec((tm, w), lambda b, i: (b * per_b + i, 0))
    full = lambda a: pl.BlockSpec(a.shape, lambda b, i: (0,) * a.ndim)
    ws = [w_out.astype(BF16), w_xq.astype(BF16), w_xo.astype(BF16)]
    vecs = [v.reshape(1, D_MODEL) for v in (ln1_g, ln1_b, ln2_g, ln2_b)]
    args = [x2d, o_ret, o_nsa, kvx, *ws, *vecs]
    in_specs = ([pl.BlockSpec((tm, D_MODEL), lambda b, i: ((batch0 + b) * per_b + i, 0)),
                 row(RET_WIDTH), row(NSA_WIDTH),
                 pl.BlockSpec((MEM_LEN, 2 * D_MODEL), lambda b, i: (batch0 + b, 0))]
                + [full(w) for w in ws] + [full(v) for v in vecs])
    return pl.pallas_call(
        _add_anchors(_postmix_kernel, args, in_specs, after),
        grid=(batch, per_b),
        in_specs=in_specs,
        out_specs=[row(D_MODEL),
                   row(D_MODEL // 2)],
        out_shape=[jax.ShapeDtypeStruct((n, D_MODEL), F32),
                   jax.ShapeDtypeStruct((n, D_MODEL // 2), jnp.uint32)],
        compiler_params=_params(2),
        name="postmix",
    )(*args)


def _router_kernel(x_ref, wr_ref, bias_ref, e_ref, rank_ref, w_ref, cnt_ref, cntrow_ref, carry_ref, carryrow_ref):
    tn = x_ref.shape[0]
    per = N_EXPERTS // N_GROUPS

    @pl.when(pl.program_id(0) == 0)
    def _():
        carry_ref[...] = jnp.zeros_like(carry_ref)
        carryrow_ref[...] = jnp.zeros_like(carryrow_ref)

    logits = _dot_nt(wr_ref[...], x_ref[...].astype(BF16))
    scores = jax.nn.sigmoid(logits)
    biased = scores + bias_ref[...]
    b3 = biased.reshape(N_GROUPS, per, tn)
    member = lax.broadcasted_iota(jnp.int32, (N_GROUPS, per, tn), 1)
    top1 = jnp.max(b3, axis=1, keepdims=True)
    first1 = jnp.min(jnp.where(b3 == top1, member, per), axis=1, keepdims=True)
    top2 = jnp.max(jnp.where(member == first1, -jnp.inf, b3), axis=1, keepdims=True)
    gscore = top1 + top2
    gidx = lax.broadcasted_iota(jnp.int32, (N_GROUPS, 1, tn), 0)
    gwork = gscore
    for _ in range(TOPK_GROUPS - 1):
        gbest = jnp.max(gwork, axis=0, keepdims=True)
        gfirst = jnp.min(jnp.where(gwork == gbest, gidx, N_GROUPS), axis=0, keepdims=True)
        gwork = jnp.where(gidx == gfirst, -jnp.inf, gwork)
    kth = jnp.max(gwork, axis=0, keepdims=True)
    work = jnp.where(gscore >= kth, b3, NEG).reshape(N_EXPERTS, tn)
    eidx = lax.broadcasted_iota(jnp.int32, (N_EXPERTS, tn), 0)
    picks = []
    chosen = jnp.zeros((N_EXPERTS, tn), F32)
    for _ in range(TOP_K):
        best = jnp.max(work, axis=0, keepdims=True)
        first = jnp.min(jnp.where(work == best, eidx, N_EXPERTS), axis=0, keepdims=True)
        hit = eidx == first
        picks.append((first, hit))
        chosen = jnp.where(hit, 1.0, chosen)
        work = jnp.where(hit, -jnp.inf, work)

    r_i = lax.broadcasted_iota(jnp.int32, (tn, tn), 0)
    c_i = lax.broadcasted_iota(jnp.int32, (tn, tn), 1)
    before = jnp.where(r_i < c_i, 1.0, 0.0).astype(BF16)
    chosen_b = chosen.astype(BF16)
    rank = _dot(chosen_b, before) + carry_ref[...]
    carry_ref[...] = carry_ref[...] + jnp.sum(chosen, axis=1, keepdims=True)
    carryrow_ref[...] = carryrow_ref[...] + _dot_nt(jnp.ones((8, tn), BF16), chosen_b)
    cnt_ref[...] = carry_ref[...]
    cntrow_ref[...] = carryrow_ref[...]

    wsel = [jnp.sum(jnp.where(hit, scores, 0.0), axis=0, keepdims=True) for _, hit in picks]
    wsum = wsel[0]
    for v in wsel[1:]:
        wsum = wsum + v
    for kk, (first, hit) in enumerate(picks):
        e_ref[kk:kk + 1, :] = first
        rank_ref[kk:kk + 1, :] = jnp.sum(jnp.where(hit, rank, 0.0), axis=0, keepdims=True).astype(jnp.int32)
        w_ref[kk:kk + 1, :] = wsel[kk] / wsum * ROUTED_SCALE


def _router(x2, w_router, router_bias):
    n = x2.shape[0]
    tn = 512 if n % 512 == 0 else n
    wr_t = w_router.T.astype(BF16)
    bias = router_bias.reshape(N_EXPERTS, 1).astype(F32)
    kspec = pl.BlockSpec((TOP_K, tn), lambda i: (0, i))
    return pl.pallas_call(
        _router_kernel,
        grid=(n // tn,),
        in_specs=[pl.BlockSpec((tn, D_MODEL), lambda i: (i, 0)),
                  pl.BlockSpec(wr_t.shape, lambda i: (0, 0)),
                  pl.BlockSpec(bias.shape, lambda i: (0, 0))],
        out_specs=[kspec, kspec, kspec, pl.BlockSpec((N_EXPERTS, 1), lambda i: (0, 0)),
                   pl.BlockSpec((8, N_EXPERTS), lambda i: (0, 0))],
        out_shape=[jax.ShapeDtypeStruct((TOP_K, n), jnp.int32),
                   jax.ShapeDtypeStruct((TOP_K, n), jnp.int32),
                   jax.ShapeDtypeStruct((TOP_K, n), F32),
                   jax.ShapeDtypeStruct((N_EXPERTS, 1), F32),
                   jax.ShapeDtypeStruct((8, N_EXPERTS), F32)],
        scratch_shapes=[pltpu.VMEM((N_EXPERTS, 1), F32), pltpu.VMEM((8, N_EXPERTS), F32)],
        compiler_params=_params(1),
        name="router",
    )(x2, wr_t, bias)


def _slots_kernel(e_ref, rank_ref, cnt_ref, cntrow_ref, dest_ref, blk_e_ref, valid_ref, *, blk, n_blocks):
    pad = lambda c: jnp.ceil(c / blk) * blk
    cnt = cnt_ref[...]
    padded = pad(cnt)
    padded_row = pad(cntrow_ref[0:1, :])
    r_i = lax.broadcasted_iota(jnp.int32, (N_EXPERTS, N_EXPERTS), 0)
    c_i = lax.broadcasted_iota(jnp.int32, (N_EXPERTS, N_EXPERTS), 1)
    start = jnp.sum(jnp.where(c_i < r_i, padded_row, 0.0), axis=1, keepdims=True)
    end = start + padded
    e = e_ref[...]
    dest = rank_ref[...]
    for ex in range(N_EXPERTS):
        dest = dest + jnp.where(e == ex, start[ex:ex + 1, :].astype(jnp.int32), 0)
    dest_ref[...] = dest
    bstart = (lax.broadcasted_iota(jnp.int32, (1, n_blocks), 1) * blk).astype(F32)
    owner = jnp.sum(jnp.where(end <= bstart, 1.0, 0.0), axis=0, keepdims=True)
    blk_e_ref[...] = jnp.minimum(owner, N_EXPERTS - 1.0).astype(jnp.int32)
    inside = (start <= bstart) & (bstart < end)
    real = jnp.clip(start + cnt - bstart, 0.0, float(blk))
    valid_ref[...] = jnp.sum(jnp.where(inside, real, 0.0), axis=0, keepdims=True).astype(jnp.int32)


def _slots(e_k, rank_k, counts, counts_row, blk, n_blocks):
    n = e_k.shape[1]
    full = lambda shape: pl.BlockSpec(shape, lambda: (0,) * len(shape))
    return pl.pallas_call(
        functools.partial(_slots_kernel, blk=blk, n_blocks=n_blocks),
        in_specs=[full((TOP_K, n)), full((TOP_K, n)), full((N_EXPERTS, 1)), full((8, N_EXPERTS))],
        out_specs=[full((TOP_K, n)), full((1, n_blocks)), full((1, n_blocks))],
        out_shape=[jax.ShapeDtypeStruct((TOP_K, n), jnp.int32),
                   jax.ShapeDtypeStruct((1, n_blocks), jnp.int32),
                   jax.ShapeDtypeStruct((1, n_blocks), jnp.int32)],
        compiler_params=pltpu.CompilerParams(vmem_limit_bytes=VMEM_LIMIT),
        name="slots",
    )(e_k, rank_k, counts, counts_row)


def _sc_worker_base(per_worker):
    return (lax.axis_index("s") * SC_CORES + lax.axis_index("c")) * per_worker


def _sc_scatter_rows(rows, idx, n_out):
    n, width = rows.shape
    k_lists = idx.shape[0] // n
    workers = SC_CORES * SC_SUBCORES
    per_worker = n // workers
    assert per_worker * workers == n and per_worker % SC_CHUNK == 0
    mesh = plsc.VectorSubcoreMesh(core_axis_name="c", subcore_axis_name="s")

    @functools.partial(
        pl.kernel, mesh=mesh,
        out_type=jax.ShapeDtypeStruct((n_out, width), rows.dtype),
        scratch_types=[pltpu.VMEM((SC_CHUNK, width), rows.dtype)]
                      + [pltpu.VMEM((SC_CHUNK,), jnp.int32)] * k_lists + [pltpu.SemaphoreType.DMA] * 3,
        name="sc_scatter")
    def scatter(rows_hbm, idx_hbm, out_hbm, rows_v, *rest):
        idx_vs = rest[:k_lists]
        sem_rows, sem_idx, sem_out = rest[k_lists:]
        base = _sc_worker_base(per_worker)

        @pl.loop(0, per_worker // SC_CHUNK)
        def _(ci):
            off = pl.multiple_of(base + ci * SC_CHUNK, SC_CHUNK)
            loads = [pltpu.async_copy(rows_hbm.at[pl.ds(off, SC_CHUNK)], rows_v, sem_rows)]
            loads += [pltpu.async_copy(idx_hbm.at[pl.ds(pl.multiple_of(k * n + off, SC_CHUNK), SC_CHUNK)],
                                       idx_vs[k], sem_idx) for k in range(k_lists)]
            for c in loads:
                c.wait()
            copies = [pltpu.async_copy(rows_v, out_hbm.at[idx_vs[k]], sem_out) for k in range(k_lists)]
            for c in copies:
                c.wait()

    return scatter(rows, idx)


def _experts_kernel(blk_e_ref, valid_ref, xs_ref, wg_ref, wu_ref, wd_ref, y_ref, wg_b, wu_b, wd_b):
    i = pl.program_id(0)
    valid = valid_ref[i]

    @pl.when((i == 0) | (blk_e_ref[i] != blk_e_ref[jnp.maximum(i - 1, 0)]))
    def _():
        wg_b[...] = wg_ref[...].astype(BF16)
        wu_b[...] = wu_ref[...].astype(BF16)
        wd_b[...] = wd_ref[...].astype(BF16)

    @pl.when(valid > 0)
    def _():
        half = D_MODEL // 2
        row = lax.broadcasted_iota(jnp.int32, (xs_ref.shape[0], 1), 0)
        hi, lo = (v.astype(BF16) for v in _unpack_halves(jnp.where(row < valid, xs_ref[...], jnp.uint32(0))))
        gate = _dot(hi, wg_b[:half, :]) + _dot(lo, wg_b[half:, :])
        up = _dot(hi, wu_b[:half, :]) + _dot(lo, wu_b[half:, :])
        y_ref[...] = _pack_halves(_dot((jax.nn.silu(gate) * up).astype(BF16), wd_b[...]))

    @pl.when(valid <= 0)
    def _():
        y_ref[...] = jnp.zeros_like(y_ref)


def _experts(blk_e, valid, xs, w_gate, w_up, w_down, blk, after=None):
    cap, width = xs.shape
    wspec = lambda a: pl.BlockSpec((None,) + a.shape[1:], lambda i, be, nv: (be[i], 0, 0))
    rows = pl.BlockSpec((blk, width), lambda i, be, nv: (i, 0))
    args = [blk_e, valid, xs, w_gate, w_up, w_down]
    in_specs = [rows, wspec(w_gate), wspec(w_up), wspec(w_down)]
    body = _add_anchors(_experts_kernel, args, in_specs, after)
    return pl.pallas_call(
        body,
        grid_spec=pltpu.PrefetchScalarGridSpec(
            num_scalar_prefetch=2,
            grid=(cap // blk,),
            in_specs=in_specs,
            out_specs=rows,
            scratch_shapes=[pltpu.VMEM(w.shape[1:], BF16) for w in (w_gate, w_up, w_down)],
        ),
        out_shape=jax.ShapeDtypeStruct(xs.shape, xs.dtype),
        compiler_params=_params(1),
        name="experts",
    )(*args)


def _sc_gather_rows(table, idx):
    b, width = idx.shape[0], table.shape[1]
    workers = SC_CORES * SC_SUBCORES
    per_worker = b // workers
    assert per_worker * workers == b and per_worker % (SC_CHUNK * SC_INFLIGHT) == 0
    mesh = plsc.VectorSubcoreMesh(core_axis_name="c", subcore_axis_name="s")

    @functools.partial(
        pl.kernel, mesh=mesh,
        out_type=jax.ShapeDtypeStruct((b, width), table.dtype),
        scratch_types=[pltpu.VMEM((SC_CHUNK,), jnp.int32)] * SC_INFLIGHT
                      + [pltpu.VMEM((SC_CHUNK, width), table.dtype)] * SC_INFLIGHT
                      + [pltpu.SemaphoreType.DMA] * (1 + 2 * SC_INFLIGHT),
        name="sc_gather")
    def gather(table_hbm, idx_hbm, out_hbm, *scratch):
        idx_vs = scratch[:SC_INFLIGHT]
        rows_vs = scratch[SC_INFLIGHT:2 * SC_INFLIGHT]
        sem_idx = scratch[2 * SC_INFLIGHT]
        sem_rows = scratch[2 * SC_INFLIGHT + 1:3 * SC_INFLIGHT + 1]
        sem_out = scratch[3 * SC_INFLIGHT + 1:]
        base = _sc_worker_base(per_worker)
        lanes = range(SC_INFLIGHT)

        @pl.loop(0, per_worker // (SC_CHUNK * SC_INFLIGHT))
        def _(gi):
            offs = [pl.multiple_of(base + (gi * SC_INFLIGHT + j) * SC_CHUNK, SC_CHUNK) for j in lanes]
            loads = [pltpu.async_copy(idx_hbm.at[pl.ds(offs[j], SC_CHUNK)], idx_vs[j], sem_idx) for j in lanes]
            for c in loads:
                c.wait()
            gathers = [pltpu.async_copy(table_hbm.at[idx_vs[j]], rows_vs[j], sem_rows[j]) for j in lanes]
            writes = []
            for j in lanes:
                gathers[j].wait()
                writes.append(pltpu.async_copy(rows_vs[j], out_hbm.at[pl.ds(offs[j], SC_CHUNK)], sem_out[j]))
            for c in writes:
                c.wait()

    return gather(table, idx)


def _combine_kernel(x_ref, wk_ref, yk_ref, wsg_ref, wsu_ref, wsd_ref, g_ref, b_ref, *rest):
    o_ref = rest[-1]
    x = x_ref[...]
    xb = x.astype(BF16)
    shared = _dot((jax.nn.silu(_dot(xb, wsg_ref[...])) * _dot(xb, wsu_ref[...])).astype(BF16), wsd_ref[...])
    wk = wk_ref[...]
    routed_hi = routed_lo = None
    for kk in range(TOP_K):
        hi, lo = _unpack_halves(yk_ref[kk])
        w = wk[:, kk:kk + 1]
        routed_hi = hi * w if kk == 0 else routed_hi + hi * w
        routed_lo = lo * w if kk == 0 else routed_lo + lo * w
    routed = jnp.concatenate([routed_hi, routed_lo], axis=1)
    o_ref[...] = _layer_norm(DN_ALPHA * x + (routed + shared), g_ref[...], b_ref[...])


def _combine(x2, w_tok, yk, ws_gate, ws_up, ws_down, ln3_g, ln3_b, row0, n_total, out_prev, after=None,
             part=(0, 1)):
    n = x2.shape[0] // part[1]
    sub0 = part[0] * n
    tt = 512 if n % 512 == 0 and row0 % 512 == 0 else n
    blk0 = (row0 + sub0) // tt
    sub_blk = sub0 // tt
    ws = [ws_gate.astype(BF16), ws_up.astype(BF16), ws_down.astype(BF16)]
    vecs = [ln3_g.reshape(1, D_MODEL), ln3_b.reshape(1, D_MODEL)]
    full = lambda a: pl.BlockSpec(a.shape, lambda i: (0,) * a.ndim)
    args = [x2, w_tok, yk, *ws, *vecs]
    in_specs = ([pl.BlockSpec((tt, D_MODEL), lambda i: (sub_blk + i, 0)),
                 pl.BlockSpec((tt, TOP_K), lambda i: (sub_blk + i, 0)),
                 pl.BlockSpec((TOP_K, tt, D_MODEL // 2), lambda i: (0, sub_blk + i, 0))]
                + [full(a) for a in ws] + [full(v) for v in vecs])
    aliases = {}
    for anchor in (a for a in (after or ()) if a is not None):
        args.append(anchor)
        in_specs.append(pl.BlockSpec(memory_space=pl.ANY))
    if out_prev is not None:
        aliases = {len(args): 0}
        args.append(out_prev)
        in_specs.append(pl.BlockSpec(memory_space=pl.ANY))
    return pl.pallas_call(
        _combine_kernel,
        grid=(n // tt,),
        in_specs=in_specs,
        out_specs=pl.BlockSpec((tt, D_MODEL), lambda i: (blk0 + i, 0)),
        out_shape=jax.ShapeDtypeStruct((n_total, D_MODEL), F32),
        input_output_aliases=aliases,
        compiler_params=_params(1),
        name="combine",
    )(*args)


EXPERT_BLOCK = 512


def _moe_dispatch(x2, x2p, w_router, router_bias):
    n = x2.shape[0]
    cap = n * TOP_K + N_EXPERTS * EXPERT_BLOCK
    e_k, rank_k, w_k, counts, counts_row = _router(x2, w_router, router_bias)
    dest, blk_e, valid = _slots(e_k, rank_k, counts, counts_row, EXPERT_BLOCK, cap // EXPERT_BLOCK)
    dest = dest.reshape(-1)
    return dict(w_tok=w_k.T, dest=dest, blk_e=blk_e.reshape(-1), valid=valid.reshape(-1),
                xs=_sc_scatter_rows(x2p, dest, cap))


def _moe_experts(routed, w_gate, w_up, w_down, after):
    y = _experts(routed["blk_e"], routed["valid"], routed["xs"], w_gate, w_up, w_down, EXPERT_BLOCK, after=after)
    n = routed["dest"].shape[0] // TOP_K
    return y, _sc_gather_rows(y, routed["dest"]).reshape(TOP_K, n, D_MODEL // 2)


def _layer(x, mem, positions, w_in, cmp_pe_k, cmp_pe_v, cmp_w1_k, cmp_w2_k, cmp_w1_v, cmp_w2_v,
           w_out, ln1_g, ln1_b, w_xq, w_xkv, w_xo, ln2_g, ln2_b, w_router, router_bias,
           w_gate, w_up, w_down, ws_gate, ws_up, ws_down, ln3_g, ln3_b):
    batch, seq, _ = x.shape
    n_total = batch * seq
    x2d = x.reshape(n_total, D_MODEL)
    pos_col = positions.astype(F32).reshape(n_total, 1)
    kvx = _memkv(mem.reshape(batch * MEM_LEN, D_MODEL), w_xkv)
    last = max(1, batch // 4)
    sizes = [batch - last, last] if batch > 1 else [batch]
    starts = [sum(sizes[:g]) for g in range(len(sizes))]

    def mixers_in(g):
        nb, row0 = sizes[g], starts[g] * seq
        (rq, rk, rv, rg, nq, nqr, kc, vc, ks, vs, kw, vw, gates) = _inproj(x2d, pos_col, w_in, row0, nb * seq)
        o_ret = _retention(rq, rk, rv, rg, nb, seq)
        kcmp = _compress(kc, cmp_pe_k, cmp_w1_k, cmp_w2_k, nb, seq)
        vcmp = _compress(vc, cmp_pe_v, cmp_w1_v, cmp_w2_v, nb, seq)
        return o_ret, (nq, nqr, gates, kcmp, vcmp, ks, vs, kw, vw)

    def attend(g, nsa_args, after):
        return _nsa(*nsa_args, sizes[g], seq, after=after)

    def mix_and_route(g, o_ret, o_nsa, after):
        x2, x2p = _postmix(x2d, o_ret, o_nsa, kvx, w_out, w_xq, w_xo, ln1_g, ln1_b, ln2_g, ln2_b,
                           starts[g], sizes[g], seq, after=after)
        return x2, _moe_dispatch(x2, x2p, w_router, router_bias)

    def combine(g, x2, routed, yk, out_prev, after=None, part=(0, 1)):
        return _combine(x2, routed["w_tok"], yk, ws_gate, ws_up, ws_down, ln3_g, ln3_b, starts[g] * seq,
                        n_total, out_prev, after=after, part=part)

    o_ret, nsa_args = mixers_in(0)
    x2, routed = mix_and_route(0, o_ret, attend(0, nsa_args, None), None)
    if len(sizes) == 1:
        y, yk = _moe_experts(routed, w_gate, w_up, w_down, after=None)
        return combine(0, x2, routed, yk, None).reshape(batch, seq, D_MODEL)
    o_ret1, nsa_args1 = mixers_in(1)
    y, yk = _moe_experts(routed, w_gate, w_up, w_down, after=[o_ret1])
    o_nsa1 = attend(1, nsa_args1, [y])
    x2_1, routed1 = mix_and_route(1, o_ret1, o_nsa1, [yk])
    out = combine(0, x2, routed, yk, None, [routed1["dest"]], part=(0, 2))
    y1, yk1 = _moe_experts(routed1, w_gate, w_up, w_down, after=[out])
    out = combine(0, x2, routed, yk, out, [y1], part=(1, 2))
    out = combine(1, x2_1, routed1, yk1, out)
    return out.reshape(batch, seq, D_MODEL)


def kernel(x, mem, positions, w_in, cmp_pe_k, cmp_pe_v, cmp_w1_k, cmp_w2_k, cmp_w1_v, cmp_w2_v, w_out, ln1_g, ln1_b, w_xq, w_xkv, w_xo, ln2_g, ln2_b, w_router, router_bias, w_gate, w_up, w_down, ws_gate, ws_up, ws_down, ln3_g, ln3_b):
    for l in range(DEPTH):
        x = _layer(x, mem, positions, w_in[l], cmp_pe_k[l], cmp_pe_v[l], cmp_w1_k[l], cmp_w2_k[l],
                   cmp_w1_v[l], cmp_w2_v[l], w_out[l], ln1_g[l], ln1_b[l], w_xq[l], w_xkv[l],
                   w_xo[l], ln2_g[l], ln2_b[l], w_router[l], router_bias[l], w_gate[l], w_up[l],
                   w_down[l], ws_gate[l], ws_up[l], ws_down[l], ln3_g[l], ln3_b[l])
    return x
```

```python
import functools

import numpy as np
import jax
import jax.numpy as jnp
from jax import lax
from jax.experimental import pallas as pl
from jax.experimental.pallas import tpu as pltpu
from jax.experimental.pallas import tpu_sc as plsc

D_MODEL = 1024
MEM_LEN = 256
DEPTH = 1
DN_ALPHA = (2 * DEPTH) ** 0.25
LN_EPS = 1e-5
NEG = -1e30
FORCE = 1e9

RET_HEADS = 4
RET_DIM = 128
RET_CHUNK = 128
RET_ROPE_BASE = 10000.0
RET_STEP_CHUNKS = 4
RET_WIDTH = RET_HEADS * RET_DIM

NSA_HEADS = 8
NSA_KV_GROUPS = 2
NSA_HPG = NSA_HEADS // NSA_KV_GROUPS
NSA_DIM = 64
NSA_WIDTH = NSA_HEADS * NSA_DIM
KV_WIDTH = NSA_KV_GROUPS * NSA_DIM
CMP_LEN = 32
CMP_STRIDE = 16
CMP_HIDDEN = 256
SEL_LEN = 64
SEL_SHIFT = 6
SEL_TOPK = 16
N_FORCED = 3
WIN = 512
ROPE_THETA = 500000.0
ROPE_DIMS = NSA_DIM // 4
GATE_LANES = 16
NSA_CHAINS = 1
SUM_ROWS = 16
WIN_PART = 256

SC_CORES = 2
SC_SUBCORES = 16
SC_CHUNK = 64
SC_INFLIGHT = 2

XATT_HEADS = 4
XATT_DIM = D_MODEL // XATT_HEADS

N_EXPERTS = 64
TOP_K = 8
N_GROUPS = 8
TOPK_GROUPS = 4
EXPERT_FF = 256
SHARED_FF = 256
ROUTED_SCALE = 2.5

LANES = 128
VMEM_LIMIT = 56 * 1024 * 1024

F32 = jnp.float32
BF16 = jnp.bfloat16
NT_DIMS = (((1,), (1,)), ((), ()))


def _params(n_axes):
    return pltpu.CompilerParams(dimension_semantics=("arbitrary",) * n_axes,
                                vmem_limit_bytes=VMEM_LIMIT)


def _dot(a, b):
    return jnp.dot(a, b, preferred_element_type=F32)


def _dot_nt(a, b):
    return lax.dot_general(a, b, NT_DIMS, preferred_element_type=F32)


def _layer_norm(v, g, b):
    mu = jnp.mean(v, axis=-1, keepdims=True)
    d = v - mu
    var = jnp.mean(d * d, axis=-1, keepdims=True)
    return d * lax.rsqrt(var + LN_EPS) * g + b


def _inproj_kernel(x_ref, pos_ref, wret_ref, wnq_ref, wkv_ref, wg_ref, invr_ref, invn_ref,
                   rq_ref, rk_ref, rv_ref, rg_ref, nq_ref, nqr_ref, kc_ref, vc_ref,
                   ks_ref, vs_ref, kw_ref, vw_ref, gate_ref):
    xb = x_ref[...].astype(BF16)
    pos = pos_ref[...]
    lane = lax.broadcasted_iota(jnp.int32, (1, LANES), 1)

    ang = pos * invr_ref[...]
    cos_r = jnp.cos(ang)
    sin_r = jnp.sin(ang)
    sin_r = jnp.where(lane < RET_DIM // 2, -sin_r, sin_r)
    q_all = _dot(xb, wret_ref[:, :RET_WIDTH])
    k_all = _dot(xb, wret_ref[:, RET_WIDTH:2 * RET_WIDTH])
    for h in range(RET_HEADS):
        cols = slice(h * RET_DIM, (h + 1) * RET_DIM)
        q = q_all[:, cols]
        rq_ref[:, cols] = (q * cos_r + pltpu.roll(q, RET_DIM // 2, 1) * sin_r).astype(BF16)
        k = k_all[:, cols]
        k = (k * cos_r + pltpu.roll(k, RET_DIM // 2, 1) * sin_r) * (RET_DIM ** -0.5)
        rk_ref[:, cols] = k.astype(BF16)
    rv_ref[...] = _dot(xb, wret_ref[:, 2 * RET_WIDTH:3 * RET_WIDTH]).astype(BF16)
    rg_ref[...] = _dot(xb, wret_ref[:, 3 * RET_WIDTH:4 * RET_WIDTH]).astype(BF16)

    half = ROPE_DIMS // 2
    j = lane % NSA_DIM
    angn = pos * invn_ref[...]
    cos_n = jnp.cos(angn)
    sin_n = jnp.sin(angn)
    sin_lo = jnp.where(j < half, -sin_n, 0.0)
    sin_hi = jnp.where((j >= half) & (j < 2 * half), sin_n, 0.0)

    def rope_n(v):
        return v * cos_n + pltpu.roll(v, half, 1) * sin_hi + pltpu.roll(v, LANES - half, 1) * sin_lo

    scale = NSA_DIM ** -0.5
    nq_all = _dot(xb, wnq_ref[...])
    for c in range(NSA_WIDTH // LANES):
        cols = slice(c * LANES, (c + 1) * LANES)
        q = nq_all[:, cols]
        nq_ref[:, cols] = (q * scale).astype(BF16)
        nqr_ref[:, cols] = (rope_n(q) * scale).astype(BF16)

    kv_all = _dot(xb, wkv_ref[...])

    def kv(i):
        return kv_all[:, i * KV_WIDTH:(i + 1) * KV_WIDTH]

    def split_groups(ref, v):
        for g in range(NSA_KV_GROUPS):
            ref[g] = v[:, g * NSA_DIM:(g + 1) * NSA_DIM].astype(BF16)

    kc_ref[...] = kv(0)
    vc_ref[...] = kv(1)
    split_groups(ks_ref, rope_n(kv(2)))
    split_groups(vs_ref, kv(3))
    split_groups(kw_ref, rope_n(kv(4)))
    split_groups(vw_ref, kv(5))

    gt = jax.nn.sigmoid(_dot_nt(wg_ref[...], xb))
    for g in range(NSA_KV_GROUPS):
        gate_ref[g] = gt[g * GATE_LANES:(g + 1) * GATE_LANES, :]


def _inproj(x2d, pos_col, w_in, row0, n):
    tm = 1024 if n % 1024 == 0 and row0 % 1024 == 0 else n
    blk0 = row0 // tm
    off = np.cumsum([0] + [RET_WIDTH] * 4 + [NSA_WIDTH] + [KV_WIDTH] * 6)
    w_ret = w_in[:, :off[4]].astype(BF16)
    w_nq = w_in[:, off[4]:off[5]].astype(BF16)
    w_kv = w_in[:, off[5]:off[11]].astype(BF16)
    wg = w_in[:, off[11]:].reshape(D_MODEL, NSA_KV_GROUPS, NSA_HPG * 3)
    wg = jnp.pad(wg, ((0, 0), (0, 0), (0, GATE_LANES - NSA_HPG * 3)))
    wg = wg.reshape(D_MODEL, NSA_KV_GROUPS * GATE_LANES).T.astype(BF16)

    lane = np.arange(LANES)
    half_r = RET_DIM // 2
    inv_r = (np.float32(RET_ROPE_BASE) ** (-np.arange(half_r, dtype=np.float32) / np.float32(half_r)))
    inv_r = inv_r.astype(np.float32)[lane % half_r][None, :]
    half_n = ROPE_DIMS // 2
    inv_n = (np.float32(ROPE_THETA) ** (-np.arange(half_n, dtype=np.float32) / np.float32(half_n)))
    jn = lane % NSA_DIM
    inv_n = np.where(jn < ROPE_DIMS, inv_n.astype(np.float32)[jn % half_n], np.float32(0.0))[None, :]

    row = lambda w: pl.BlockSpec((tm, w), lambda i: (i, 0))
    src_row = lambda w: pl.BlockSpec((tm, w), lambda i: (blk0 + i, 0))
    full = lambda a: pl.BlockSpec(a.shape, lambda i: (0,) * a.ndim)
    grp = lambda w: pl.BlockSpec((NSA_KV_GROUPS, tm, w), lambda i: (0, i, 0))
    bf = lambda w: jax.ShapeDtypeStruct((n, w), BF16)
    gbf = jax.ShapeDtypeStruct((NSA_KV_GROUPS, n, NSA_DIM), BF16)
    inv_r = jnp.asarray(inv_r, F32)
    inv_n = jnp.asarray(inv_n, F32)
    return pl.pallas_call(
        _inproj_kernel,
        grid=(n // tm,),
        in_specs=[src_row(D_MODEL), src_row(1), full(w_ret), full(w_nq), full(w_kv), full(wg),
                  full(inv_r), full(inv_n)],
        out_specs=[row(RET_WIDTH)] * 4 + [row(NSA_WIDTH)] * 2 + [row(KV_WIDTH)] * 2
                  + [grp(NSA_DIM)] * 4
                  + [pl.BlockSpec((NSA_KV_GROUPS, GATE_LANES, tm), lambda i: (0, 0, i))],
        out_shape=[bf(RET_WIDTH)] * 4 + [bf(NSA_WIDTH)] * 2
                  + [jax.ShapeDtypeStruct((n, KV_WIDTH), F32)] * 2 + [gbf] * 4
                  + [jax.ShapeDtypeStruct((NSA_KV_GROUPS, GATE_LANES, n), F32)],
        compiler_params=_params(1),
        name="inproj",
    )(x2d, pos_col, w_ret, w_nq, w_kv, wg, inv_r, inv_n)


def _retention_kernel(q_ref, k_ref, v_ref, g_ref, o_ref, state_ref):
    c = RET_CHUNK

    @pl.when(pl.program_id(1) == 0)
    def _():
        state_ref[...] = jnp.zeros_like(state_ref)

    row = lax.broadcasted_iota(jnp.int32, (c, c), 0)
    col = lax.broadcasted_iota(jnp.int32, (c, c), 1)
    rel = (row - col).astype(F32)
    idx = lax.broadcasted_iota(jnp.int32, (c, 1), 0).astype(F32)
    for h in range(RET_HEADS):
        log_g = float(np.log(np.float32(1.0) - np.float32(2.0) ** np.float32(-5.0 - h)))
        cols = slice(h * RET_DIM, (h + 1) * RET_DIM)
        dmask = jnp.where(rel >= 0, jnp.exp(log_g * jnp.maximum(rel, 0.0)), 0.0)
        zeta = jnp.exp(log_g * (c - 1.0 - idx))
        xi = jnp.exp(log_g * (idx + 1.0))
        for j in range(q_ref.shape[0] // c):
            rows = slice(j * c, (j + 1) * c)
            q = q_ref[rows, cols]
            k = k_ref[rows, cols]
            v = v_ref[rows, cols]
            scores = _dot_nt(q, k) * dmask
            inner = _dot(scores.astype(BF16), v)
            prev = state_ref[h]
            cross = _dot(q, prev.astype(BF16)) * xi
            kz = (k.astype(F32) * zeta).astype(BF16)
            kv = lax.dot_general(kz, v, (((0,), (0,)), ((), ())), preferred_element_type=F32)
            state_ref[h] = prev * float(np.exp(np.float32(log_g) * np.float32(c))) + kv
            o = inner + cross
            mu = jnp.mean(o, axis=-1, keepdims=True)
            d = o - mu
            var = jnp.mean(d * d, axis=-1, keepdims=True)
            o = d * lax.rsqrt(var + LN_EPS)
            o_ref[rows, cols] = (jax.nn.silu(g_ref[rows, cols].astype(F32)) * o).astype(BF16)


def _retention(rq, rk, rv, rg, batch, seq):
    per_step = RET_STEP_CHUNKS if (seq // RET_CHUNK) % RET_STEP_CHUNKS == 0 else 1
    nc = seq // (RET_CHUNK * per_step)
    spec = pl.BlockSpec((RET_CHUNK * per_step, RET_WIDTH), lambda b, n: (b * nc + n, 0))
    return pl.pallas_call(
        _retention_kernel,
        grid=(batch, nc),
        in_specs=[spec] * 4,
        out_specs=spec,
        out_shape=jax.ShapeDtypeStruct(rq.shape, BF16),
        scratch_shapes=[pltpu.VMEM((RET_HEADS, RET_DIM, RET_DIM), F32)],
        compiler_params=_params(2),
        name="retention",
    )(rq, rk, rv, rg)


def _compress_kernel(a_ref, pe_ref, w1_ref, w2_ref, o_ref, shift_ref, *, n_cmp):
    rows = a_ref.shape[0]
    a = a_ref[...]
    lo = (a + pe_ref[0]).astype(BF16)
    hi = (a + pe_ref[1]).astype(BF16)
    ridx = lax.broadcasted_iota(jnp.int32, (rows, 1), 0)
    shift_ref[rows:rows + 8, :] = jnp.zeros((8, CMP_HIDDEN), F32)
    for g in range(NSA_KV_GROUPS):
        p = _dot(lo, w1_ref[0, g])
        shift_ref[0:rows, :] = _dot(hi, w1_ref[1, g])
        hid = jax.nn.silu(p + shift_ref[pl.ds(1, rows), :])
        out = _dot(hid.astype(BF16), w2_ref[...])
        o_ref[g] = jnp.where(ridx < n_cmp, out, 0.0).astype(BF16)


def _compress(a, pe, w1, w2, batch, seq):
    rows = seq // CMP_STRIDE
    per = CMP_STRIDE * KV_WIDTH
    n_cmp = (seq - CMP_LEN) // CMP_STRIDE + 1
    a2 = a.reshape(batch * rows, per)
    pe2 = jnp.tile(pe.reshape(2, CMP_STRIDE, 1, NSA_DIM), (1, 1, NSA_KV_GROUPS, 1)).reshape(2, 1, per)
    w1r = w1.reshape(2, CMP_STRIDE, 1, NSA_DIM, CMP_HIDDEN)
    eye = jnp.eye(NSA_KV_GROUPS, dtype=w1.dtype).reshape(1, NSA_KV_GROUPS, 1, NSA_KV_GROUPS, 1, 1)
    w1x = (w1r[:, None] * eye).reshape(2, NSA_KV_GROUPS, per, CMP_HIDDEN).astype(BF16)
    w2b = w2.astype(BF16)
    full = lambda arr: pl.BlockSpec(arr.shape, lambda b: (0,) * arr.ndim)
    return pl.pallas_call(
        functools.partial(_compress_kernel, n_cmp=n_cmp),
        grid=(batch,),
        in_specs=[pl.BlockSpec((rows, per), lambda b: (b, 0)), full(pe2), full(w1x), full(w2b)],
        out_specs=pl.BlockSpec((None, NSA_KV_GROUPS, rows, NSA_DIM), lambda b: (b, 0, 0, 0)),
        out_shape=jax.ShapeDtypeStruct((batch, NSA_KV_GROUPS, rows, NSA_DIM), BF16),
        scratch_shapes=[pltpu.VMEM((rows + 8, CMP_HIDDEN), F32)],
        compiler_params=_params(1),
        name="compress",
    )(a2, pe2, w1x, w2b)


def _heads_to_lanes(ref):
    vt = ref[...].astype(F32).T
    return jnp.concatenate([vt[h * NSA_DIM:(h + 1) * NSA_DIM] for h in range(NSA_HPG)], axis=1).astype(BF16)


def _tile_heads(v):
    return jnp.concatenate([v] * NSA_HPG, axis=1)


def _transpose_into(dst_ref, src_ref, chunk):
    def step(c, _):
        c0 = pl.multiple_of(c * chunk, chunk)
        dst_ref[:NSA_DIM, pl.ds(c0, chunk)] = src_ref[pl.ds(c0, chunk), :].astype(F32).T.astype(BF16)
        return 0
    lax.fori_loop(0, src_ref.shape[0] // chunk, step, 0)


def _nsa_kernel(qraw_ref, qrot_ref, gate_ref, kcmp_ref, vcmp_ref, ovt_ref,
                ks_ref, vs_ref, kw_ref, vw_ref, o_ref, vst_ref, vwt_ref, vct_ref, bias_ref, *, tq, tk, seq):
    i = pl.program_id(2)
    t0 = i * tq
    cols = NSA_HPG * tq
    n_sel = seq // SEL_LEN
    n_cmp_rows = seq // CMP_STRIDE
    blocks_per_tile = tk // SEL_LEN

    @pl.when(i == 0)
    def _():
        chunk = min(512, n_cmp_rows)
        _transpose_into(vst_ref, vs_ref, chunk)
        _transpose_into(vwt_ref, vw_ref, chunk)
        _transpose_into(vct_ref, vcmp_ref, chunk)
        vst_ref[NSA_DIM:, :] = jnp.ones((SUM_ROWS, seq), BF16)
        vwt_ref[NSA_DIM:, :] = jnp.ones((SUM_ROWS, seq), BF16)

    def split_sum(acc):
        return acc[:NSA_DIM] / acc[NSA_DIM:NSA_DIM + 1]

    q_raw = _heads_to_lanes(qraw_ref)
    q_rot = _heads_to_lanes(qrot_ref)
    t_row = t0 + lax.broadcasted_iota(jnp.int32, (1, tq), 1)

    chain_w = cols // NSA_CHAINS
    heads_per_chain = chain_w // tq
    chains = [slice(c * chain_w, (c + 1) * chain_w) for c in range(NSA_CHAINS)]
    tile_chain = lambda v: jnp.concatenate([v] * heads_per_chain, axis=1)

    pw = min(tq, WIN_PART)
    parts = []
    for u in range(tq // pw):
        span = WIN + pw
        ws = pl.multiple_of(jnp.maximum(t0 + u * pw - WIN, 0), pw)
        dist = t_row[:, u * pw:(u + 1) * pw] - (ws + lax.broadcasted_iota(jnp.int32, (span, 1), 0))
        bias_w = jnp.concatenate([jnp.where((dist >= 0) & (dist < WIN), 0.0, NEG)] * NSA_HPG, axis=1)
        q_part = jnp.concatenate([q_rot[:, h * tq + u * pw:h * tq + (u + 1) * pw] for h in range(NSA_HPG)],
                                 axis=1)
        s_w = _dot(kw_ref[pl.ds(ws, span), :], q_part) + bias_w
        p_w = jnp.exp((s_w - jnp.max(s_w, axis=0, keepdims=True)).astype(BF16))
        parts.append(split_sum(_dot(vwt_ref[:, pl.ds(ws, span)], p_w)))
    o_w = jnp.concatenate([parts[u][:, h * pw:(h + 1) * pw]
                           for h in range(NSA_HPG) for u in range(tq // pw)], axis=1)

    c_idx = lax.broadcasted_iota(jnp.int32, (n_cmp_rows, 1), 0)
    valid = tile_chain(jnp.where(c_idx * CMP_STRIDE + (CMP_LEN - 1) <= t_row, 1.0, 0.0))
    bias_c = (valid - 1.0) * (-NEG)
    o_c = []
    p_sum = None
    for c in chains:
        s_c = _dot(kcmp_ref[...], q_raw[:, c]) + bias_c
        e_c = jnp.exp(s_c - jnp.max(s_c, axis=0, keepdims=True)) * valid
        l_c = jnp.sum(e_c, axis=0, keepdims=True)
        p_c = e_c / jnp.where(l_c > 0.0, l_c, 1.0)
        o_c.append(_dot(vct_ref[...], p_c.astype(BF16)))
        for h in range(heads_per_chain):
            p_h = p_c[:, h * tq:(h + 1) * tq]
            p_sum = p_h if p_sum is None else p_sum + p_h
    o_c = jnp.concatenate(o_c, axis=1)

    p_hi = p_sum.astype(BF16)
    p_lo = (p_sum - p_hi.astype(F32)).astype(BF16)
    ovt = ovt_ref[...]
    imp = _dot(ovt, p_hi) + _dot(ovt, p_lo)
    jb = lax.broadcasted_iota(jnp.int32, (n_sel, tq), 0)
    cur = (t0 + lax.broadcasted_iota(jnp.int32, (n_sel, tq), 1)) >> SEL_SHIFT
    forced = (jb == 0) | (jb == cur) | (jb == cur - 1)
    work = jnp.where(forced, -jnp.inf, imp)
    work = jnp.where(jb <= cur, work, NEG)
    sel_t = jnp.where(forced, 1.0, 0.0)
    for _ in range(max(min(SEL_TOPK, n_sel) - N_FORCED, 0)):
        best = jnp.max(work, axis=0, keepdims=True)
        first = jnp.min(jnp.where(work == best, jb, n_sel), axis=0, keepdims=True)
        hit = jb == first
        sel_t = jnp.where(hit, 1.0, sel_t)
        work = jnp.where(hit, -jnp.inf, work)
    bias_ref[...] = jnp.where(sel_t > 0.5, 0.0, NEG)

    def sel_tile(kt, carry, causal):
        k0 = pl.multiple_of(kt * tk, tk)
        bias = jnp.concatenate(
            [jnp.broadcast_to(bias_ref[pl.ds(kt * blocks_per_tile + j, 1), :], (SEL_LEN, tq))
             for j in range(blocks_per_tile)], axis=0)
        if causal:
            kpos = k0 + lax.broadcasted_iota(jnp.int32, (tk, 1), 0)
            bias = jnp.where(kpos <= t_row, bias, NEG)
        bias = tile_chain(bias)
        k_t = ks_ref[pl.ds(k0, tk), :]
        v_t = vst_ref[:, pl.ds(k0, tk)]
        out = []
        scores = [_dot(k_t, q_rot[:, c]) + bias for c in chains]
        for (m, acc), s in zip(carry, scores):
            m_new = jnp.maximum(m, jnp.max(s, axis=0, keepdims=True))
            p = jnp.exp((s - m_new).astype(BF16))
            acc = jnp.exp(m - m_new) * acc + _dot(v_t, p)
            out.append((m_new, acc))
        return tuple(out)

    n_full = t0 // tk
    init = tuple((jnp.full((1, chain_w), NEG, F32), jnp.zeros((NSA_DIM + SUM_ROWS, chain_w), F32))
                 for _ in chains)
    carry = lax.fori_loop(0, n_full, functools.partial(sel_tile, causal=False), init)
    for d in range(max(tq // tk, 1)):
        carry = sel_tile(n_full + d, carry, causal=True)
    o_s = jnp.concatenate([split_sum(acc) for _, acc in carry], axis=1)

    gt = gate_ref[...]
    outs = []
    for h in range(NSA_HPG):
        c = slice(h * tq, (h + 1) * tq)
        outs.append(gt[3 * h:3 * h + 1] * o_c[:, c] + gt[3 * h + 1:3 * h + 2] * o_s[:, c]
                    + gt[3 * h + 2:3 * h + 3] * o_w[:, c])
    o_ref[...] = jnp.concatenate(outs, axis=0).T.astype(BF16)


def _anchored(kernel_fn, n_inputs, n_anchors):
    def body(*refs, **static):
        kernel_fn(*refs[:n_inputs], *refs[n_inputs + n_anchors:], **static)
    return body


def _add_anchors(kernel_fn, args, in_specs, after):
    after = [a for a in (after or ()) if a is not None]
    if not after:
        return kernel_fn
    body = _anchored(kernel_fn, len(args), len(after))
    args.extend(after)
    in_specs.extend([pl.BlockSpec(memory_space=pl.ANY)] * len(after))
    return body


def _nsa(nq, nqr, gates, kcmp, vcmp, ks, vs, kw, vw, batch, seq, after=None):
    n = batch * seq
    tq = 512
    tk = 512 if seq % 512 == 0 else seq
    nqb = seq // tq
    n_sel = seq // SEL_LEN
    rows_c = seq // CMP_STRIDE
    gw = NSA_HPG * NSA_DIM
    cs = np.arange(rows_c)[None, :] * CMP_STRIDE
    ss = np.arange(n_sel)[:, None] * SEL_LEN
    n_cmp = (seq - CMP_LEN) // CMP_STRIDE + 1
    ovt = ((cs < ss + SEL_LEN) & (cs + CMP_LEN > ss) & (np.arange(rows_c)[None, :] < n_cmp))
    ovt = jnp.asarray(ovt.astype(np.float32), BF16)

    qspec = pl.BlockSpec((tq, gw), lambda b, g, i: (b * nqb + i, g))
    cspec = pl.BlockSpec((None, None, rows_c, NSA_DIM), lambda b, g, i: (b, g, 0, 0))
    kspec = pl.BlockSpec((None, seq, NSA_DIM), lambda b, g, i: (g, b, 0))
    args = [nq, nqr, gates, kcmp, vcmp, ovt, ks, vs, kw, vw]
    in_specs = [qspec, qspec,
                pl.BlockSpec((None, GATE_LANES, tq), lambda b, g, i: (g, 0, b * nqb + i)),
                cspec, cspec, pl.BlockSpec(ovt.shape, lambda b, g, i: (0, 0)),
                kspec, kspec, kspec, kspec]
    body = functools.partial(_add_anchors(_nsa_kernel, args, in_specs, after), tq=tq, tk=tk, seq=seq)
    return pl.pallas_call(
        body,
        grid=(batch, NSA_KV_GROUPS, nqb),
        in_specs=in_specs,
        out_specs=qspec,
        out_shape=jax.ShapeDtypeStruct((n, NSA_WIDTH), BF16),
        scratch_shapes=[pltpu.VMEM((NSA_DIM + SUM_ROWS, seq), BF16), pltpu.VMEM((NSA_DIM + SUM_ROWS, seq), BF16),
                        pltpu.VMEM((NSA_DIM, rows_c), BF16), pltpu.VMEM((n_sel, tq), F32)],
        compiler_params=_params(3),
        name="nsa",
    )(*args)


def _memkv_kernel(mem_ref, w_ref, kv_ref):
    kv_ref[...] = _dot(mem_ref[...].astype(BF16), w_ref[...]).astype(BF16)


def _memkv(mem2d, w_xkv):
    n = mem2d.shape[0]
    w = w_xkv.astype(BF16)
    return pl.pallas_call(
        _memkv_kernel,
        grid=(n // MEM_LEN,),
        in_specs=[pl.BlockSpec((MEM_LEN, D_MODEL), lambda i: (i, 0)),
                  pl.BlockSpec(w.shape, lambda i: (0, 0))],
        out_specs=pl.BlockSpec((MEM_LEN, 2 * D_MODEL), lambda i: (i, 0)),
        out_shape=jax.ShapeDtypeStruct((n, 2 * D_MODEL), BF16),
        compiler_params=_params(1),
        name="memkv",
    )(mem2d, w)


def _pack_halves(v):
    half = D_MODEL // 2
    hi = pltpu.bitcast(v[:, :half].astype(BF16).astype(F32), jnp.uint32)
    lo = pltpu.bitcast(v[:, half:].astype(BF16).astype(F32), jnp.uint32)
    return hi | (lo >> 16)


def _unpack_halves(words):
    return pltpu.bitcast(words & jnp.uint32(0xFFFF0000), F32), pltpu.bitcast(words << 16, F32)


def _postmix_kernel(x_ref, oret_ref, onsa_ref, kv_ref, wout_ref, wq_ref, wo_ref,
                    g1_ref, b1_ref, g2_ref, b2_ref, x2_ref, x2p_ref):
    mixed = jnp.concatenate([oret_ref[...], onsa_ref[...]], axis=1)
    x1 = _layer_norm(DN_ALPHA * x_ref[...] + _dot(mixed, wout_ref[...]), g1_ref[...], b1_ref[...])
    q = (_dot(x1.astype(BF16), wq_ref[...]) * (XATT_DIM ** -0.5)).astype(BF16)
    heads = []
    for h in range(XATT_HEADS):
        cols = slice(h * XATT_DIM, (h + 1) * XATT_DIM)
        s = _dot_nt(q[:, cols], kv_ref[:, cols])
        m = jnp.max(s, axis=-1, keepdims=True)
        p = jnp.exp(s - m)
        l = jnp.sum(p, axis=-1, keepdims=True)
        heads.append(_dot(p.astype(BF16), kv_ref[:, D_MODEL + h * XATT_DIM:D_MODEL + (h + 1) * XATT_DIM]) / l)
    att = jnp.concatenate(heads, axis=1).astype(BF16)
    x2 = _layer_norm(DN_ALPHA * x1 + _dot(att, wo_ref[...]), g2_ref[...], b2_ref[...])
    x2_ref[...] = x2
    x2p_ref[...] = _pack_halves(x2)


def _postmix(x2d, o_ret, o_nsa, kvx, w_out, w_xq, w_xo, ln1_g, ln1_b, ln2_g, ln2_b, batch0, batch, seq,
             after=None):
    n = batch * seq
    tm = 512 if seq % 512 == 0 else seq
    per_b = seq // tm
    row = lambda w: pl.BlockSpec((tm, w), lambda b, i: (b * per_b + i, 0))
    full = lambda a: pl.BlockSpec(a.shape, lambda b, i: (0,) * a.ndim)
    ws = [w_out.astype(BF16), w_xq.astype(BF16), w_xo.astype(BF16)]
    vecs = [v.reshape(1, D_MODEL) for v in (ln1_g, ln1_b, ln2_g, ln2_b)]
    args = [x2d, o_ret, o_nsa, kvx, *ws, *vecs]
    in_specs = ([pl.BlockSpec((tm, D_MODEL), lambda b, i: ((batch0 + b) * per_b + i, 0)),
                 row(RET_WIDTH), row(NSA_WIDTH),
                 pl.BlockSpec((MEM_LEN, 2 * D_MODEL), lambda b, i: (batch0 + b, 0))]
                + [full(w) for w in ws] + [full(v) for v in vecs])
    return pl.pallas_call(
        _add_anchors(_postmix_kernel, args, in_specs, after),
        grid=(batch, per_b),
        in_specs=in_specs,
        out_specs=[row(D_MODEL),
                   row(D_MODEL // 2)],
        out_shape=[jax.ShapeDtypeStruct((n, D_MODEL), F32),
                   jax.ShapeDtypeStruct((n, D_MODEL // 2), jnp.uint32)],
        compiler_params=_params(2),
        name="postmix",
    )(*args)


def _router_kernel(x_ref, wr_ref, bias_ref, e_ref, rank_ref, w_ref, cnt_ref, cntrow_ref, carry_ref, carryrow_ref):
    tn = x_ref.shape[0]
    per = N_EXPERTS // N_GROUPS

    @pl.when(pl.program_id(0) == 0)
    def _():
        carry_ref[...] = jnp.zeros_like(carry_ref)
        carryrow_ref[...] = jnp.zeros_like(carryrow_ref)

    logits = _dot_nt(wr_ref[...], x_ref[...].astype(BF16))
    scores = jax.nn.sigmoid(logits)
    biased = scores + bias_ref[...]
    b3 = biased.reshape(N_GROUPS, per, tn)
    member = lax.broadcasted_iota(jnp.int32, (N_GROUPS, per, tn), 1)
    top1 = jnp.max(b3, axis=1, keepdims=True)
    first1 = jnp.min(jnp.where(b3 == top1, member, per), axis=1, keepdims=True)
    top2 = jnp.max(jnp.where(member == first1, -jnp.inf, b3), axis=1, keepdims=True)
    gscore = top1 + top2
    gidx = lax.broadcasted_iota(jnp.int32, (N_GROUPS, 1, tn), 0)
    gwork = gscore
    for _ in range(TOPK_GROUPS - 1):
        gbest = jnp.max(gwork, axis=0, keepdims=True)
        gfirst = jnp.min(jnp.where(gwork == gbest, gidx, N_GROUPS), axis=0, keepdims=True)
        gwork = jnp.where(gidx == gfirst, -jnp.inf, gwork)
    kth = jnp.max(gwork, axis=0, keepdims=True)
    work = jnp.where(gscore >= kth, b3, NEG).reshape(N_EXPERTS, tn)
    eidx = lax.broadcasted_iota(jnp.int32, (N_EXPERTS, tn), 0)
    picks = []
    chosen = jnp.zeros((N_EXPERTS, tn), F32)
    for _ in range(TOP_K):
        best = jnp.max(work, axis=0, keepdims=True)
        first = jnp.min(jnp.where(work == best, eidx, N_EXPERTS), axis=0, keepdims=True)
        hit = eidx == first
        picks.append((first, hit))
        chosen = jnp.where(hit, 1.0, chosen)
        work = jnp.where(hit, -jnp.inf, work)

    r_i = lax.broadcasted_iota(jnp.int32, (tn, tn), 0)
    c_i = lax.broadcasted_iota(jnp.int32, (tn, tn), 1)
    before = jnp.where(r_i < c_i, 1.0, 0.0).astype(BF16)
    chosen_b = chosen.astype(BF16)
    rank = _dot(chosen_b, before) + carry_ref[...]
    carry_ref[...] = carry_ref[...] + jnp.sum(chosen, axis=1, keepdims=True)
    carryrow_ref[...] = carryrow_ref[...] + _dot_nt(jnp.ones((8, tn), BF16), chosen_b)
    cnt_ref[...] = carry_ref[...]
    cntrow_ref[...] = carryrow_ref[...]

    wsel = [jnp.sum(jnp.where(hit, scores, 0.0), axis=0, keepdims=True) for _, hit in picks]
    wsum = wsel[0]
    for v in wsel[1:]:
        wsum = wsum + v
    for kk, (first, hit) in enumerate(picks):
        e_ref[kk:kk + 1, :] = first
        rank_ref[kk:kk + 1, :] = jnp.sum(jnp.where(hit, rank, 0.0), axis=0, keepdims=True).astype(jnp.int32)
        w_ref[kk:kk + 1, :] = wsel[kk] / wsum * ROUTED_SCALE


def _router(x2, w_router, router_bias):
    n = x2.shape[0]
    tn = 512 if n % 512 == 0 else n
    wr_t = w_router.T.astype(BF16)
    bias = router_bias.reshape(N_EXPERTS, 1).astype(F32)
    kspec = pl.BlockSpec((TOP_K, tn), lambda i: (0, i))
    return pl.pallas_call(
        _router_kernel,
        grid=(n // tn,),
        in_specs=[pl.BlockSpec((tn, D_MODEL), lambda i: (i, 0)),
                  pl.BlockSpec(wr_t.shape, lambda i: (0, 0)),
                  pl.BlockSpec(bias.shape, lambda i: (0, 0))],
        out_specs=[kspec, kspec, kspec, pl.BlockSpec((N_EXPERTS, 1), lambda i: (0, 0)),
                   pl.BlockSpec((8, N_EXPERTS), lambda i: (0, 0))],
        out_shape=[jax.ShapeDtypeStruct((TOP_K, n), jnp.int32),
                   jax.ShapeDtypeStruct((TOP_K, n), jnp.int32),
                   jax.ShapeDtypeStruct((TOP_K, n), F32),
                   jax.ShapeDtypeStruct((N_EXPERTS, 1), F32),
                   jax.ShapeDtypeStruct((8, N_EXPERTS), F32)],
        scratch_shapes=[pltpu.VMEM((N_EXPERTS, 1), F32), pltpu.VMEM((8, N_EXPERTS), F32)],
        compiler_params=_params(1),
        name="router",
    )(x2, wr_t, bias)


def _slots_kernel(e_ref, rank_ref, cnt_ref, cntrow_ref, dest_ref, blk_e_ref, valid_ref, *, blk, n_blocks):
    pad = lambda c: jnp.ceil(c / blk) * blk
    cnt = cnt_ref[...]
    padded = pad(cnt)
    padded_row = pad(cntrow_ref[0:1, :])
    r_i = lax.broadcasted_iota(jnp.int32, (N_EXPERTS, N_EXPERTS), 0)
    c_i = lax.broadcasted_iota(jnp.int32, (N_EXPERTS, N_EXPERTS), 1)
    start = jnp.sum(jnp.where(c_i < r_i, padded_row, 0.0), axis=1, keepdims=True)
    end = start + padded
    e = e_ref[...]
    dest = rank_ref[...]
    for ex in range(N_EXPERTS):
        dest = dest + jnp.where(e == ex, start[ex:ex + 1, :].astype(jnp.int32), 0)
    dest_ref[...] = dest
    bstart = (lax.broadcasted_iota(jnp.int32, (1, n_blocks), 1) * blk).astype(F32)
    owner = jnp.sum(jnp.where(end <= bstart, 1.0, 0.0), axis=0, keepdims=True)
    blk_e_ref[...] = jnp.minimum(owner, N_EXPERTS - 1.0).astype(jnp.int32)
    inside = (start <= bstart) & (bstart < end)
    real = jnp.clip(start + cnt - bstart, 0.0, float(blk))
    valid_ref[...] = jnp.sum(jnp.where(inside, real, 0.0), axis=0, keepdims=True).astype(jnp.int32)


def _slots(e_k, rank_k, counts, counts_row, blk, n_blocks):
    n = e_k.shape[1]
    full = lambda shape: pl.BlockSpec(shape, lambda: (0,) * len(shape))
    return pl.pallas_call(
        functools.partial(_slots_kernel, blk=blk, n_blocks=n_blocks),
        in_specs=[full((TOP_K, n)), full((TOP_K, n)), full((N_EXPERTS, 1)), full((8, N_EXPERTS))],
        out_specs=[full((TOP_K, n)), full((1, n_blocks)), full((1, n_blocks))],
        out_shape=[jax.ShapeDtypeStruct((TOP_K, n), jnp.int32),
                   jax.ShapeDtypeStruct((1, n_blocks), jnp.int32),
                   jax.ShapeDtypeStruct((1, n_blocks), jnp.int32)],
        compiler_params=pltpu.CompilerParams(vmem_limit_bytes=VMEM_LIMIT),
        name="slots",
    )(e_k, rank_k, counts, counts_row)


def _sc_worker_base(per_worker):
    return (lax.axis_index("s") * SC_CORES + lax.axis_index("c")) * per_worker


def _sc_scatter_rows(rows, idx, n_out):
    n, width = rows.shape
    k_lists = idx.shape[0] // n
    workers = SC_CORES * SC_SUBCORES
    per_worker = n // workers
    assert per_worker * workers == n and per_worker % SC_CHUNK == 0
    mesh = plsc.VectorSubcoreMesh(core_axis_name="c", subcore_axis_name="s")

    @functools.partial(
        pl.kernel, mesh=mesh,
        out_type=jax.ShapeDtypeStruct((n_out, width), rows.dtype),
        scratch_types=[pltpu.VMEM((SC_CHUNK, width), rows.dtype)]
                      + [pltpu.VMEM((SC_CHUNK,), jnp.int32)] * k_lists + [pltpu.SemaphoreType.DMA] * 3,
        name="sc_scatter")
    def scatter(rows_hbm, idx_hbm, out_hbm, rows_v, *rest):
        idx_vs = rest[:k_lists]
        sem_rows, sem_idx, sem_out = rest[k_lists:]
        base = _sc_worker_base(per_worker)

        @pl.loop(0, per_worker // SC_CHUNK)
        def _(ci):
            off = pl.multiple_of(base + ci * SC_CHUNK, SC_CHUNK)
            loads = [pltpu.async_copy(rows_hbm.at[pl.ds(off, SC_CHUNK)], rows_v, sem_rows)]
            loads += [pltpu.async_copy(idx_hbm.at[pl.ds(pl.multiple_of(k * n + off, SC_CHUNK), SC_CHUNK)],
                                       idx_vs[k], sem_idx) for k in range(k_lists)]
            for c in loads:
                c.wait()
            copies = [pltpu.async_copy(rows_v, out_hbm.at[idx_vs[k]], sem_out) for k in range(k_lists)]
            for c in copies:
                c.wait()

    return scatter(rows, idx)


def _experts_kernel(blk_e_ref, valid_ref, xs_ref, wg_ref, wu_ref, wd_ref, y_ref, wg_b, wu_b, wd_b):
    i = pl.program_id(0)
    valid = valid_ref[i]

    @pl.when((i == 0) | (blk_e_ref[i] != blk_e_ref[jnp.maximum(i - 1, 0)]))
    def _():
        wg_b[...] = wg_ref[...].astype(BF16)
        wu_b[...] = wu_ref[...].astype(BF16)
        wd_b[...] = wd_ref[...].astype(BF16)

    @pl.when(valid > 0)
    def _():
        half = D_MODEL // 2
        row = lax.broadcasted_iota(jnp.int32, (xs_ref.shape[0], 1), 0)
        hi, lo = (v.astype(BF16) for v in _unpack_halves(jnp.where(row < valid, xs_ref[...], jnp.uint32(0))))
        gate = _dot(hi, wg_b[:half, :]) + _dot(lo, wg_b[half:, :])
        up = _dot(hi, wu_b[:half, :]) + _dot(lo, wu_b[half:, :])
        y_ref[...] = _pack_halves(_dot((jax.nn.silu(gate) * up).astype(BF16), wd_b[...]))

    @pl.when(valid <= 0)
    def _():
        y_ref[...] = jnp.zeros_like(y_ref)


def _experts(blk_e, valid, xs, w_gate, w_up, w_down, blk, after=None):
    cap, width = xs.shape
    wspec = lambda a: pl.BlockSpec((None,) + a.shape[1:], lambda i, be, nv: (be[i], 0, 0))
    rows = pl.BlockSpec((blk, width), lambda i, be, nv: (i, 0))
    args = [blk_e, valid, xs, w_gate, w_up, w_down]
    in_specs = [rows, wspec(w_gate), wspec(w_up), wspec(w_down)]
    body = _add_anchors(_experts_kernel, args, in_specs, after)
    return pl.pallas_call(
        body,
        grid_spec=pltpu.PrefetchScalarGridSpec(
            num_scalar_prefetch=2,
            grid=(cap // blk,),
            in_specs=in_specs,
            out_specs=rows,
            scratch_shapes=[pltpu.VMEM(w.shape[1:], BF16) for w in (w_gate, w_up, w_down)],
        ),
        out_shape=jax.ShapeDtypeStruct(xs.shape, xs.dtype),
        compiler_params=_params(1),
        name="experts",
    )(*args)


def _sc_gather_rows(table, idx):
    b, width = idx.shape[0], table.shape[1]
    workers = SC_CORES * SC_SUBCORES
    per_worker = b // workers
    assert per_worker * workers == b and per_worker % (SC_CHUNK * SC_INFLIGHT) == 0
    mesh = plsc.VectorSubcoreMesh(core_axis_name="c", subcore_axis_name="s")

    @functools.partial(
        pl.kernel, mesh=mesh,
        out_type=jax.ShapeDtypeStruct((b, width), table.dtype),
        scratch_types=[pltpu.VMEM((SC_CHUNK,), jnp.int32)] * SC_INFLIGHT
                      + [pltpu.VMEM((SC_CHUNK, width), table.dtype)] * SC_INFLIGHT
                      + [pltpu.SemaphoreType.DMA] * (1 + 2 * SC_INFLIGHT),
        name="sc_gather")
    def gather(table_hbm, idx_hbm, out_hbm, *scratch):
        idx_vs = scratch[:SC_INFLIGHT]
        rows_vs = scratch[SC_INFLIGHT:2 * SC_INFLIGHT]
        sem_idx = scratch[2 * SC_INFLIGHT]
        sem_rows = scratch[2 * SC_INFLIGHT + 1:3 * SC_INFLIGHT + 1]
        sem_out = scratch[3 * SC_INFLIGHT + 1:]
        base = _sc_worker_base(per_worker)
        lanes = range(SC_INFLIGHT)

        @pl.loop(0, per_worker // (SC_CHUNK * SC_INFLIGHT))
        def _(gi):
            offs = [pl.multiple_of(base + (gi * SC_INFLIGHT + j) * SC_CHUNK, SC_CHUNK) for j in lanes]
            loads = [pltpu.async_copy(idx_hbm.at[pl.ds(offs[j], SC_CHUNK)], idx_vs[j], sem_idx) for j in lanes]
            for c in loads:
                c.wait()
            gathers = [pltpu.async_copy(table_hbm.at[idx_vs[j]], rows_vs[j], sem_rows[j]) for j in lanes]
            writes = []
            for j in lanes:
                gathers[j].wait()
                writes.append(pltpu.async_copy(rows_vs[j], out_hbm.at[pl.ds(offs[j], SC_CHUNK)], sem_out[j]))
            for c in writes:
                c.wait()

    return gather(table, idx)


def _combine_kernel(x_ref, wk_ref, yk_ref, wsg_ref, wsu_ref, wsd_ref, g_ref, b_ref, *rest):
    o_ref = rest[-1]
    x = x_ref[...]
    xb = x.astype(BF16)
    shared = _dot((jax.nn.silu(_dot(xb, wsg_ref[...])) * _dot(xb, wsu_ref[...])).astype(BF16), wsd_ref[...])
    wk = wk_ref[...]
    routed_hi = routed_lo = None
    for kk in range(TOP_K):
        hi, lo = _unpack_halves(yk_ref[kk])
        w = wk[:, kk:kk + 1]
        routed_hi = hi * w if kk == 0 else routed_hi + hi * w
        routed_lo = lo * w if kk == 0 else routed_lo + lo * w
    routed = jnp.concatenate([routed_hi, routed_lo], axis=1)
    o_ref[...] = _layer_norm(DN_ALPHA * x + (routed + shared), g_ref[...], b_ref[...])


def _combine(x2, w_tok, yk, ws_gate, ws_up, ws_down, ln3_g, ln3_b, row0, n_total, out_prev, after=None,
             part=(0, 1)):
    n = x2.shape[0] // part[1]
    sub0 = part[0] * n
    tt = 512 if n % 512 == 0 and row0 % 512 == 0 else n
    blk0 = (row0 + sub0) // tt
    sub_blk = sub0 // tt
    ws = [ws_gate.astype(BF16), ws_up.astype(BF16), ws_down.astype(BF16)]
    vecs = [ln3_g.reshape(1, D_MODEL), ln3_b.reshape(1, D_MODEL)]
    full = lambda a: pl.BlockSpec(a.shape, lambda i: (0,) * a.ndim)
    args = [x2, w_tok, yk, *ws, *vecs]
    in_specs = ([pl.BlockSpec((tt, D_MODEL), lambda i: (sub_blk + i, 0)),
                 pl.BlockSpec((tt, TOP_K), lambda i: (sub_blk + i, 0)),
                 pl.BlockSpec((TOP_K, tt, D_MODEL // 2), lambda i: (0, sub_blk + i, 0))]
                + [full(a) for a in ws] + [full(v) for v in vecs])
    aliases = {}
    for anchor in (a for a in (after or ()) if a is not None):
        args.append(anchor)
        in_specs.append(pl.BlockSpec(memory_space=pl.ANY))
    if out_prev is not None:
        aliases = {len(args): 0}
        args.append(out_prev)
        in_specs.append(pl.BlockSpec(memory_space=pl.ANY))
    return pl.pallas_call(
        _combine_kernel,
        grid=(n // tt,),
        in_specs=in_specs,
        out_specs=pl.BlockSpec((tt, D_MODEL), lambda i: (blk0 + i, 0)),
        out_shape=jax.ShapeDtypeStruct((n_total, D_MODEL), F32),
        input_output_aliases=aliases,
        compiler_params=_params(1),
        name="combine",
    )(*args)


EXPERT_BLOCK = 512


def _moe_dispatch(x2, x2p, w_router, router_bias):
    n = x2.shape[0]
    cap = n * TOP_K + N_EXPERTS * EXPERT_BLOCK
    e_k, rank_k, w_k, counts, counts_row = _router(x2, w_router, router_bias)
    dest, blk_e, valid = _slots(e_k, rank_k, counts, counts_row, EXPERT_BLOCK, cap // EXPERT_BLOCK)
    dest = dest.reshape(-1)
    return dict(w_tok=w_k.T, dest=dest, blk_e=blk_e.reshape(-1), valid=valid.reshape(-1),
                xs=_sc_scatter_rows(x2p, dest, cap))


def _moe_experts(routed, w_gate, w_up, w_down, after):
    y = _experts(routed["blk_e"], routed["valid"], routed["xs"], w_gate, w_up, w_down, EXPERT_BLOCK, after=after)
    n = routed["dest"].shape[0] // TOP_K
    return y, _sc_gather_rows(y, routed["dest"]).reshape(TOP_K, n, D_MODEL // 2)


def _layer(x, mem, positions, w_in, cmp_pe_k, cmp_pe_v, cmp_w1_k, cmp_w2_k, cmp_w1_v, cmp_w2_v,
           w_out, ln1_g, ln1_b, w_xq, w_xkv, w_xo, ln2_g, ln2_b, w_router, router_bias,
           w_gate, w_up, w_down, ws_gate, ws_up, ws_down, ln3_g, ln3_b):
    batch, seq, _ = x.shape
    n_total = batch * seq
    x2d = x.reshape(n_total, D_MODEL)
    pos_col = positions.astype(F32).reshape(n_total, 1)
    kvx = _memkv(mem.reshape(batch * MEM_LEN, D_MODEL), w_xkv)
    last = max(1, batch // 4)
    sizes = [batch - last, last] if batch > 1 else [batch]
    starts = [sum(sizes[:g]) for g in range(len(sizes))]

    def mixers_in(g):
        nb, row0 = sizes[g], starts[g] * seq
        (rq, rk, rv, rg, nq, nqr, kc, vc, ks, vs, kw, vw, gates) = _inproj(x2d, pos_col, w_in, row0, nb * seq)
        o_ret = _retention(rq, rk, rv, rg, nb, seq)
        kcmp = _compress(kc, cmp_pe_k, cmp_w1_k, cmp_w2_k, nb, seq)
        vcmp = _compress(vc, cmp_pe_v, cmp_w1_v, cmp_w2_v, nb, seq)
        return o_ret, (nq, nqr, gates, kcmp, vcmp, ks, vs, kw, vw)

    def attend(g, nsa_args, after):
        return _nsa(*nsa_args, sizes[g], seq, after=after)

    def mix_and_route(g, o_ret, o_nsa, after):
        x2, x2p = _postmix(x2d, o_ret, o_nsa, kvx, w_out, w_xq, w_xo, ln1_g, ln1_b, ln2_g, ln2_b,
                           starts[g], sizes[g], seq, after=after)
        return x2, _moe_dispatch(x2, x2p, w_router, router_bias)

    def combine(g, x2, routed, yk, out_prev, after=None, part=(0, 1)):
        return _combine(x2, routed["w_tok"], yk, ws_gate, ws_up, ws_down, ln3_g, ln3_b, starts[g] * seq,
                        n_total, out_prev, after=after, part=part)

    o_ret, nsa_args = mixers_in(0)
    x2, routed = mix_and_route(0, o_ret, attend(0, nsa_args, None), None)
    if len(sizes) == 1:
        y, yk = _moe_experts(routed, w_gate, w_up, w_down, after=None)
        return combine(0, x2, routed, yk, None).reshape(batch, seq, D_MODEL)
    o_ret1, nsa_args1 = mixers_in(1)
    y, yk = _moe_experts(routed, w_gate, w_up, w_down, after=[o_ret1])
    o_nsa1 = attend(1, nsa_args1, [y])
    x2_1, routed1 = mix_and_route(1, o_ret1, o_nsa1, [yk])
    out = combine(0, x2, routed, yk, None, [routed1["dest"]], part=(0, 2))
    y1, yk1 = _moe_experts(routed1, w_gate, w_up, w_down, after=[out])
    out = combine(0, x2, routed, yk, out, [y1], part=(1, 2))
    out = combine(1, x2_1, routed1, yk1, out)
    return out.reshape(batch, seq, D_MODEL)


def kernel(x, mem, positions, w_in, cmp_pe_k, cmp_pe_v, cmp_w1_k, cmp_w2_k, cmp_w1_v, cmp_w2_v, w_out, ln1_g, ln1_b, w_xq, w_xkv, w_xo, ln2_g, ln2_b, w_router, router_bias, w_gate, w_up, w_down, ws_gate, ws_up, ws_down, ln3_g, ln3_b):
    for l in range(DEPTH):
        x = _layer(x, mem, positions, w_in[l], cmp_pe_k[l], cmp_pe_v[l], cmp_w1_k[l], cmp_w2_k[l],
                   cmp_w1_v[l], cmp_w2_v[l], w_out[l], ln1_g[l], ln1_b[l], w_xq[l], w_xkv[l],
                   w_xo[l], ln2_g[l], ln2_b[l], w_router[l], router_bias[l], w_gate[l], w_up[l],
                   w_down[l], ws_gate[l], ws_up[l], ws_down[l], ln3_g[l], ln3_b[l])
    return x
```

```python
import functools

import numpy as np
import jax
import jax.numpy as jnp
from jax import lax
from jax.experimental import pallas as pl
from jax.experimental.pallas import tpu as pltpu
from jax.experimental.pallas import tpu_sc as plsc

D_MODEL = 1024
MEM_LEN = 256
DEPTH = 1
DN_ALPHA = (2 * DEPTH) ** 0.25
LN_EPS = 1e-5
NEG = -1e30

RET_HEADS = 4
RET_DIM = 128
RET_CHUNK = 128
RET_ROPE_BASE = 10000.0
RET_STEP_CHUNKS = 4
RET_WIDTH = RET_HEADS * RET_DIM

NSA_HEADS = 8
NSA_KV_GROUPS = 2
NSA_HPG = NSA_HEADS // NSA_KV_GROUPS
NSA_DIM = 64
NSA_WIDTH = NSA_HEADS * NSA_DIM
KV_WIDTH = NSA_KV_GROUPS * NSA_DIM
CMP_LEN = 32
CMP_STRIDE = 16
CMP_HIDDEN = 256
SEL_LEN = 64
SEL_SHIFT = 6
SEL_TOPK = 16
N_FORCED = 3
WIN = 512
ROPE_THETA = 500000.0
ROPE_DIMS = NSA_DIM // 4
GATE_LANES = 16
NSA_CHAINS = 1
SUM_ROWS = 16
WIN_PART = 256

SC_CORES = 2
SC_SUBCORES = 16
SC_CHUNK = 64
SC_INFLIGHT = 2

XATT_HEADS = 4
XATT_DIM = D_MODEL // XATT_HEADS

N_EXPERTS = 64
TOP_K = 8
N_GROUPS = 8
TOPK_GROUPS = 4
EXPERT_FF = 256
SHARED_FF = 256
ROUTED_SCALE = 2.5

LANES = 128
VMEM_LIMIT = 56 * 1024 * 1024

F32 = jnp.float32
BF16 = jnp.bfloat16
NT_DIMS = (((1,), (1,)), ((), ()))


def _params(n_axes):
    return pltpu.CompilerParams(dimension_semantics=("arbitrary",) * n_axes,
                                vmem_limit_bytes=VMEM_LIMIT)


def _dot(a, b):
    return jnp.dot(a, b, preferred_element_type=F32)


def _dot_nt(a, b):
    return lax.dot_general(a, b, NT_DIMS, preferred_element_type=F32)


def _layer_norm(v, g, b):
    mu = jnp.mean(v, axis=-1, keepdims=True)
    d = v - mu
    var = jnp.mean(d * d, axis=-1, keepdims=True)
    return d * lax.rsqrt(var + LN_EPS) * g + b


def _inproj_kernel(x_ref, pos_ref, wret_ref, wnq_ref, wkv_ref, wg_ref, invr_ref, invn_ref,
                   rq_ref, rk_ref, rv_ref, rg_ref, nq_ref, nqr_ref, kc_ref, vc_ref,
                   ks_ref, vs_ref, kw_ref, vw_ref, gate_ref):
    xb = x_ref[...].astype(BF16)
    pos = pos_ref[...]
    lane = lax.broadcasted_iota(jnp.int32, (1, LANES), 1)

    ang = pos * invr_ref[...]
    cos_r = jnp.cos(ang)
    sin_r = jnp.sin(ang)
    sin_r = jnp.where(lane < RET_DIM // 2, -sin_r, sin_r)
    q_all = _dot(xb, wret_ref[:, :RET_WIDTH])
    k_all = _dot(xb, wret_ref[:, RET_WIDTH:2 * RET_WIDTH])
    for h in range(RET_HEADS):
        cols = slice(h * RET_DIM, (h + 1) * RET_DIM)
        q = q_all[:, cols]
        rq_ref[:, cols] = (q * cos_r + pltpu.roll(q, RET_DIM // 2, 1) * sin_r).astype(BF16)
        k = k_all[:, cols]
        k = (k * cos_r + pltpu.roll(k, RET_DIM // 2, 1) * sin_r) * (RET_DIM ** -0.5)
        rk_ref[:, cols] = k.astype(BF16)
    rv_ref[...] = _dot(xb, wret_ref[:, 2 * RET_WIDTH:3 * RET_WIDTH]).astype(BF16)
    rg_ref[...] = _dot(xb, wret_ref[:, 3 * RET_WIDTH:4 * RET_WIDTH]).astype(BF16)

    half = ROPE_DIMS // 2
    j = lane % NSA_DIM
    angn = pos * invn_ref[...]
    cos_n = jnp.cos(angn)
    sin_n = jnp.sin(angn)
    sin_lo = jnp.where(j < half, -sin_n, 0.0)
    sin_hi = jnp.where((j >= half) & (j < 2 * half), sin_n, 0.0)

    def rope_n(v):
        return v * cos_n + pltpu.roll(v, half, 1) * sin_hi + pltpu.roll(v, LANES - half, 1) * sin_lo

    scale = NSA_DIM ** -0.5
    nq_all = _dot(xb, wnq_ref[...])
    for c in range(NSA_WIDTH // LANES):
        cols = slice(c * LANES, (c + 1) * LANES)
        q = nq_all[:, cols]
        nq_ref[:, cols] = (q * scale).astype(BF16)
        nqr_ref[:, cols] = (rope_n(q) * scale).astype(BF16)

    kv_all = _dot(xb, wkv_ref[...])

    def kv(i):
        return kv_all[:, i * KV_WIDTH:(i + 1) * KV_WIDTH]

    def split_groups(ref, v):
        for g in range(NSA_KV_GROUPS):
            ref[g] = v[:, g * NSA_DIM:(g + 1) * NSA_DIM].astype(BF16)

    kc_ref[...] = kv(0)
    vc_ref[...] = kv(1)
    split_groups(ks_ref, rope_n(kv(2)))
    split_groups(vs_ref, kv(3))
    split_groups(kw_ref, rope_n(kv(4)))
    split_groups(vw_ref, kv(5))

    gt = jax.nn.sigmoid(_dot_nt(wg_ref[...], xb))
    for g in range(NSA_KV_GROUPS):
        gate_ref[g] = gt[g * GATE_LANES:(g + 1) * GATE_LANES, :]


def _inproj(x2d, pos_col, w_in, row0, n):
    tm = 1024 if n % 1024 == 0 and row0 % 1024 == 0 else n
    blk0 = row0 // tm
    off = np.cumsum([0] + [RET_WIDTH] * 4 + [NSA_WIDTH] + [KV_WIDTH] * 6)
    w_ret = w_in[:, :off[4]].astype(BF16)
    w_nq = w_in[:, off[4]:off[5]].astype(BF16)
    w_kv = w_in[:, off[5]:off[11]].astype(BF16)
    wg = w_in[:, off[11]:].reshape(D_MODEL, NSA_KV_GROUPS, NSA_HPG * 3)
    wg = jnp.pad(wg, ((0, 0), (0, 0), (0, GATE_LANES - NSA_HPG * 3)))
    wg = wg.reshape(D_MODEL, NSA_KV_GROUPS * GATE_LANES).T.astype(BF16)

    lane = np.arange(LANES)
    half_r = RET_DIM // 2
    inv_r = (np.float32(RET_ROPE_BASE) ** (-np.arange(half_r, dtype=np.float32) / np.float32(half_r)))
    inv_r = inv_r.astype(np.float32)[lane % half_r][None, :]
    half_n = ROPE_DIMS // 2
    inv_n = (np.float32(ROPE_THETA) ** (-np.arange(half_n, dtype=np.float32) / np.float32(half_n)))
    jn = lane % NSA_DIM
    inv_n = np.where(jn < ROPE_DIMS, inv_n.astype(np.float32)[jn % half_n], np.float32(0.0))[None, :]

    row = lambda w: pl.BlockSpec((tm, w), lambda i: (i, 0))
    src_row = lambda w: pl.BlockSpec((tm, w), lambda i: (blk0 + i, 0))
    full = lambda a: pl.BlockSpec(a.shape, lambda i: (0,) * a.ndim)
    grp = lambda w: pl.BlockSpec((NSA_KV_GROUPS, tm, w), lambda i: (0, i, 0))
    bf = lambda w: jax.ShapeDtypeStruct((n, w), BF16)
    gbf = jax.ShapeDtypeStruct((NSA_KV_GROUPS, n, NSA_DIM), BF16)
    inv_r = jnp.asarray(inv_r, F32)
    inv_n = jnp.asarray(inv_n, F32)
    return pl.pallas_call(
        _inproj_kernel,
        grid=(n // tm,),
        in_specs=[src_row(D_MODEL), src_row(1), full(w_ret), full(w_nq), full(w_kv), full(wg),
                  full(inv_r), full(inv_n)],
        out_specs=[row(RET_WIDTH)] * 4 + [row(NSA_WIDTH)] * 2 + [row(KV_WIDTH)] * 2
                  + [grp(NSA_DIM)] * 4
                  + [pl.BlockSpec((NSA_KV_GROUPS, GATE_LANES, tm), lambda i: (0, 0, i))],
        out_shape=[bf(RET_WIDTH)] * 4 + [bf(NSA_WIDTH)] * 2
                  + [jax.ShapeDtypeStruct((n, KV_WIDTH), F32)] * 2 + [gbf] * 4
                  + [jax.ShapeDtypeStruct((NSA_KV_GROUPS, GATE_LANES, n), F32)],
        compiler_params=_params(1),
        name="inproj",
    )(x2d, pos_col, w_ret, w_nq, w_kv, wg, inv_r, inv_n)


def _retention_kernel(q_ref, k_ref, v_ref, g_ref, o_ref, state_ref):
    c = RET_CHUNK

    @pl.when(pl.program_id(1) == 0)
    def _():
        state_ref[...] = jnp.zeros_like(state_ref)

    row = lax.broadcasted_iota(jnp.int32, (c, c), 0)
    col = lax.broadcasted_iota(jnp.int32, (c, c), 1)
    rel = (row - col).astype(F32)
    idx = lax.broadcasted_iota(jnp.int32, (c, 1), 0).astype(F32)
    for h in range(RET_HEADS):
        log_g = float(np.log(np.float32(1.0) - np.float32(2.0) ** np.float32(-5.0 - h)))
        cols = slice(h * RET_DIM, (h + 1) * RET_DIM)
        dmask = jnp.where(rel >= 0, jnp.exp(log_g * jnp.maximum(rel, 0.0)), 0.0)
        zeta = jnp.exp(log_g * (c - 1.0 - idx))
        xi = jnp.exp(log_g * (idx + 1.0))
        for j in range(q_ref.shape[0] // c):
            rows = slice(j * c, (j + 1) * c)
            q = q_ref[rows, cols]
            k = k_ref[rows, cols]
            v = v_ref[rows, cols]
            scores = _dot_nt(q, k) * dmask
            inner = _dot(scores.astype(BF16), v)
            prev = state_ref[h]
            cross = _dot(q, prev.astype(BF16)) * xi
            kz = (k.astype(F32) * zeta).astype(BF16)
            kv = lax.dot_general(kz, v, (((0,), (0,)), ((), ())), preferred_element_type=F32)
            state_ref[h] = prev * float(np.exp(np.float32(log_g) * np.float32(c))) + kv
            o = inner + cross
            mu = jnp.mean(o, axis=-1, keepdims=True)
            d = o - mu
            var = jnp.mean(d * d, axis=-1, keepdims=True)
            o = d * lax.rsqrt(var + LN_EPS)
            o_ref[rows, cols] = (jax.nn.silu(g_ref[rows, cols].astype(F32)) * o).astype(BF16)


def _retention(rq, rk, rv, rg, batch, seq):
    per_step = RET_STEP_CHUNKS if (seq // RET_CHUNK) % RET_STEP_CHUNKS == 0 else 1
    nc = seq // (RET_CHUNK * per_step)
    spec = pl.BlockSpec((RET_CHUNK * per_step, RET_WIDTH), lambda b, n: (b * nc + n, 0))
    return pl.pallas_call(
        _retention_kernel,
        grid=(batch, nc),
        in_specs=[spec] * 4,
        out_specs=spec,
        out_shape=jax.ShapeDtypeStruct(rq.shape, BF16),
        scratch_shapes=[pltpu.VMEM((RET_HEADS, RET_DIM, RET_DIM), F32)],
        compiler_params=_params(2),
        name="retention",
    )(rq, rk, rv, rg)


def _compress_kernel(a_ref, pe_ref, w1_ref, w2_ref, o_ref, shift_ref, *, n_cmp):
    rows = a_ref.shape[0]
    a = a_ref[...]
    lo = (a + pe_ref[0]).astype(BF16)
    hi = (a + pe_ref[1]).astype(BF16)
    ridx = lax.broadcasted_iota(jnp.int32, (rows, 1), 0)
    shift_ref[rows:rows + 8, :] = jnp.zeros((8, CMP_HIDDEN), F32)
    for g in range(NSA_KV_GROUPS):
        p = _dot(lo, w1_ref[0, g])
        shift_ref[0:rows, :] = _dot(hi, w1_ref[1, g])
        hid = jax.nn.silu(p + shift_ref[pl.ds(1, rows), :])
        out = _dot(hid.astype(BF16), w2_ref[...])
        o_ref[g] = jnp.where(ridx < n_cmp, out, 0.0).astype(BF16)


def _compress(a, pe, w1, w2, batch, seq):
    rows = seq // CMP_STRIDE
    per = CMP_STRIDE * KV_WIDTH
    n_cmp = (seq - CMP_LEN) // CMP_STRIDE + 1
    a2 = a.reshape(batch * rows, per)
    pe2 = jnp.tile(pe.reshape(2, CMP_STRIDE, 1, NSA_DIM), (1, 1, NSA_KV_GROUPS, 1)).reshape(2, 1, per)
    w1r = w1.reshape(2, CMP_STRIDE, 1, NSA_DIM, CMP_HIDDEN)
    eye = jnp.eye(NSA_KV_GROUPS, dtype=w1.dtype).reshape(1, NSA_KV_GROUPS, 1, NSA_KV_GROUPS, 1, 1)
    w1x = (w1r[:, None] * eye).reshape(2, NSA_KV_GROUPS, per, CMP_HIDDEN).astype(BF16)
    w2b = w2.astype(BF16)
    full = lambda arr: pl.BlockSpec(arr.shape, lambda b: (0,) * arr.ndim)
    return pl.pallas_call(
        functools.partial(_compress_kernel, n_cmp=n_cmp),
        grid=(batch,),
        in_specs=[pl.BlockSpec((rows, per), lambda b: (b, 0)), full(pe2), full(w1x), full(w2b)],
        out_specs=pl.BlockSpec((None, NSA_KV_GROUPS, rows, NSA_DIM), lambda b: (b, 0, 0, 0)),
        out_shape=jax.ShapeDtypeStruct((batch, NSA_KV_GROUPS, rows, NSA_DIM), BF16),
        scratch_shapes=[pltpu.VMEM((rows + 8, CMP_HIDDEN), F32)],
        compiler_params=_params(1),
        name="compress",
    )(a2, pe2, w1x, w2b)


def _heads_to_lanes(ref):
    vt = ref[...].astype(F32).T
    return jnp.concatenate([vt[h * NSA_DIM:(h + 1) * NSA_DIM] for h in range(NSA_HPG)], axis=1).astype(BF16)


def _tile_heads(v):
    return jnp.concatenate([v] * NSA_HPG, axis=1)


def _transpose_into(dst_ref, src_ref, chunk):
    def step(c, _):
        c0 = pl.multiple_of(c * chunk, chunk)
        dst_ref[:NSA_DIM, pl.ds(c0, chunk)] = src_ref[pl.ds(c0, chunk), :].astype(F32).T.astype(BF16)
        return 0
    lax.fori_loop(0, src_ref.shape[0] // chunk, step, 0)


def _nsa_kernel(qraw_ref, qrot_ref, gate_ref, kcmp_ref, vcmp_ref, ovt_ref,
                ks_ref, vs_ref, kw_ref, vw_ref, o_ref, vst_ref, vwt_ref, vct_ref, bias_ref, *, tq, tk, seq):
    i = pl.program_id(2)
    t0 = i * tq
    cols = NSA_HPG * tq
    n_sel = seq // SEL_LEN
    n_cmp_rows = seq // CMP_STRIDE
    blocks_per_tile = tk // SEL_LEN

    @pl.when(i == 0)
    def _():
        chunk = min(512, n_cmp_rows)
        _transpose_into(vst_ref, vs_ref, chunk)
        _transpose_into(vwt_ref, vw_ref, chunk)
        _transpose_into(vct_ref, vcmp_ref, chunk)
        vst_ref[NSA_DIM:, :] = jnp.ones((SUM_ROWS, seq), BF16)
        vwt_ref[NSA_DIM:, :] = jnp.ones((SUM_ROWS, seq), BF16)

    def split_sum(acc):
        return acc[:NSA_DIM] / acc[NSA_DIM:NSA_DIM + 1]

    q_raw = _heads_to_lanes(qraw_ref)
    q_rot = _heads_to_lanes(qrot_ref)
    t_row = t0 + lax.broadcasted_iota(jnp.int32, (1, tq), 1)

    chain_w = cols // NSA_CHAINS
    heads_per_chain = chain_w // tq
    chains = [slice(c * chain_w, (c + 1) * chain_w) for c in range(NSA_CHAINS)]
    tile_chain = lambda v: jnp.concatenate([v] * heads_per_chain, axis=1)

    pw = min(tq, WIN_PART)
    parts = []
    for u in range(tq // pw):
        span = WIN + pw
        ws = pl.multiple_of(jnp.maximum(t0 + u * pw - WIN, 0), pw)
        dist = t_row[:, u * pw:(u + 1) * pw] - (ws + lax.broadcasted_iota(jnp.int32, (span, 1), 0))
        bias_w = jnp.concatenate([jnp.where((dist >= 0) & (dist < WIN), 0.0, NEG)] * NSA_HPG, axis=1)
        q_part = jnp.concatenate([q_rot[:, h * tq + u * pw:h * tq + (u + 1) * pw] for h in range(NSA_HPG)],
                                 axis=1)
        s_w = _dot(kw_ref[pl.ds(ws, span), :], q_part) + bias_w
        p_w = jnp.exp((s_w - jnp.max(s_w, axis=0, keepdims=True)).astype(BF16))
        parts.append(split_sum(_dot(vwt_ref[:, pl.ds(ws, span)], p_w)))
    o_w = jnp.concatenate([parts[u][:, h * pw:(h + 1) * pw]
                           for h in range(NSA_HPG) for u in range(tq // pw)], axis=1)

    c_idx = lax.broadcasted_iota(jnp.int32, (n_cmp_rows, 1), 0)
    valid = tile_chain(jnp.where(c_idx * CMP_STRIDE + (CMP_LEN - 1) <= t_row, 1.0, 0.0))
    bias_c = (valid - 1.0) * (-NEG)
    o_c = []
    p_sum = None
    for c in chains:
        s_c = _dot(kcmp_ref[...], q_raw[:, c]) + bias_c
        e_c = jnp.exp(s_c - jnp.max(s_c, axis=0, keepdims=True)) * valid
        l_c = jnp.sum(e_c, axis=0, keepdims=True)
        p_c = e_c / jnp.where(l_c > 0.0, l_c, 1.0)
        o_c.append(_dot(vct_ref[...], p_c.astype(BF16)))
        for h in range(heads_per_chain):
            p_h = p_c[:, h * tq:(h + 1) * tq]
            p_sum = p_h if p_sum is None else p_sum + p_h
    o_c = jnp.concatenate(o_c, axis=1)

    p_hi = p_sum.astype(BF16)
    p_lo = (p_sum - p_hi.astype(F32)).astype(BF16)
    ovt = ovt_ref[...]
    imp = _dot(ovt, p_hi) + _dot(ovt, p_lo)
    jb = lax.broadcasted_iota(jnp.int32, (n_sel, tq), 0)
    cur = (t0 + lax.broadcasted_iota(jnp.int32, (n_sel, tq), 1)) >> SEL_SHIFT
    forced = (jb == 0) | (jb == cur) | (jb == cur - 1)
    work = jnp.where(forced, -jnp.inf, imp)
    work = jnp.where(jb <= cur, work, NEG)
    sel_t = jnp.where(forced, 1.0, 0.0)
    for _ in range(max(min(SEL_TOPK, n_sel) - N_FORCED, 0)):
        best = jnp.max(work, axis=0, keepdims=True)
        first = jnp.min(jnp.where(work == best, jb, n_sel), axis=0, keepdims=True)
        hit = jb == first
        sel_t = jnp.where(hit, 1.0, sel_t)
        work = jnp.where(hit, -jnp.inf, work)
    bias_ref[...] = jnp.where(sel_t > 0.5, 0.0, NEG)

    def sel_tile(kt, carry, causal):
        k0 = pl.multiple_of(kt * tk, tk)
        bias = jnp.concatenate(
            [jnp.broadcast_to(bias_ref[pl.ds(kt * blocks_per_tile + j, 1), :], (SEL_LEN, tq))
             for j in range(blocks_per_tile)], axis=0)
        if causal:
            kpos = k0 + lax.broadcasted_iota(jnp.int32, (tk, 1), 0)
            bias = jnp.where(kpos <= t_row, bias, NEG)
        bias = tile_chain(bias)
        k_t = ks_ref[pl.ds(k0, tk), :]
        v_t = vst_ref[:, pl.ds(k0, tk)]
        out = []
        scores = [_dot(k_t, q_rot[:, c]) + bias for c in chains]
        for (m, acc), s in zip(carry, scores):
            m_new = jnp.maximum(m, jnp.max(s, axis=0, keepdims=True))
            p = jnp.exp((s - m_new).astype(BF16))
            acc = jnp.exp(m - m_new) * acc + _dot(v_t, p)
            out.append((m_new, acc))
        return tuple(out)

    n_full = t0 // tk
    init = tuple((jnp.full((1, chain_w), NEG, F32), jnp.zeros((NSA_DIM + SUM_ROWS, chain_w), F32))
                 for _ in chains)
    carry = lax.fori_loop(0, n_full, functools.partial(sel_tile, causal=False), init)
    for d in range(max(tq // tk, 1)):
        carry = sel_tile(n_full + d, carry, causal=True)
    o_s = jnp.concatenate([split_sum(acc) for _, acc in carry], axis=1)

    gt = gate_ref[...]
    outs = []
    for h in range(NSA_HPG):
        c = slice(h * tq, (h + 1) * tq)
        outs.append(gt[3 * h:3 * h + 1] * o_c[:, c] + gt[3 * h + 1:3 * h + 2] * o_s[:, c]
                    + gt[3 * h + 2:3 * h + 3] * o_w[:, c])
    o_ref[...] = jnp.concatenate(outs, axis=0).T.astype(BF16)


def _anchored(kernel_fn, n_inputs, n_anchors):
    def body(*refs, **static):
        kernel_fn(*refs[:n_inputs], *refs[n_inputs + n_anchors:], **static)
    return body


def _add_anchors(kernel_fn, args, in_specs, after):
    after = [a for a in (after or ()) if a is not None]
    if not after:
        return kernel_fn
    body = _anchored(kernel_fn, len(args), len(after))
    args.extend(after)
    in_specs.extend([pl.BlockSpec(memory_space=pl.ANY)] * len(after))
    return body


def _nsa(nq, nqr, gates, kcmp, vcmp, ks, vs, kw, vw, batch, seq, after=None):
    n = batch * seq
    tq = 512
    tk = 512 if seq % 512 == 0 else seq
    nqb = seq // tq
    n_sel = seq // SEL_LEN
    rows_c = seq // CMP_STRIDE
    gw = NSA_HPG * NSA_DIM
    cs = np.arange(rows_c)[None, :] * CMP_STRIDE
    ss = np.arange(n_sel)[:, None] * SEL_LEN
    n_cmp = (seq - CMP_LEN) // CMP_STRIDE + 1
    ovt = ((cs < ss + SEL_LEN) & (cs + CMP_LEN > ss) & (np.arange(rows_c)[None, :] < n_cmp))
    ovt = jnp.asarray(ovt.astype(np.float32), BF16)

    qspec = pl.BlockSpec((tq, gw), lambda b, g, i: (b * nqb + i, g))
    cspec = pl.BlockSpec((None, None, rows_c, NSA_DIM), lambda b, g, i: (b, g, 0, 0))
    kspec = pl.BlockSpec((None, seq, NSA_DIM), lambda b, g, i: (g, b, 0))
    args = [nq, nqr, gates, kcmp, vcmp, ovt, ks, vs, kw, vw]
    in_specs = [qspec, qspec,
                pl.BlockSpec((None, GATE_LANES, tq), lambda b, g, i: (g, 0, b * nqb + i)),
                cspec, cspec, pl.BlockSpec(ovt.shape, lambda b, g, i: (0, 0)),
                kspec, kspec, kspec, kspec]
    body = functools.partial(_add_anchors(_nsa_kernel, args, in_specs, after), tq=tq, tk=tk, seq=seq)
    return pl.pallas_call(
        body,
        grid=(batch, NSA_KV_GROUPS, nqb),
        in_specs=in_specs,
        out_specs=qspec,
        out_shape=jax.ShapeDtypeStruct((n, NSA_WIDTH), BF16),
        scratch_shapes=[pltpu.VMEM((NSA_DIM + SUM_ROWS, seq), BF16), pltpu.VMEM((NSA_DIM + SUM_ROWS, seq), BF16),
                        pltpu.VMEM((NSA_DIM, rows_c), BF16), pltpu.VMEM((n_sel, tq), F32)],
        compiler_params=_params(3),
        name="nsa",
    )(*args)


def _memkv_kernel(mem_ref, w_ref, kv_ref):
    kv_ref[...] = _dot(mem_ref[...].astype(BF16), w_ref[...]).astype(BF16)


def _memkv(mem2d, w_xkv):
    n = mem2d.shape[0]
    w = w_xkv.astype(BF16)
    return pl.pallas_call(
        _memkv_kernel,
        grid=(n // MEM_LEN,),
        in_specs=[pl.BlockSpec((MEM_LEN, D_MODEL), lambda i: (i, 0)),
                  pl.BlockSpec(w.shape, lambda i: (0, 0))],
        out_specs=pl.BlockSpec((MEM_LEN, 2 * D_MODEL), lambda i: (i, 0)),
        out_shape=jax.ShapeDtypeStruct((n, 2 * D_MODEL), BF16),
        compiler_params=_params(1),
        name="memkv",
    )(mem2d, w)


def _pack_halves(v):
    half = D_MODEL // 2
    hi = pltpu.bitcast(v[:, :half].astype(BF16).astype(F32), jnp.uint32)
    lo = pltpu.bitcast(v[:, half:].astype(BF16).astype(F32), jnp.uint32)
    return hi | (lo >> 16)


def _unpack_halves(words):
    return pltpu.bitcast(words & jnp.uint32(0xFFFF0000), F32), pltpu.bitcast(words << 16, F32)


def _postmix_kernel(x_ref, oret_ref, onsa_ref, kv_ref, wout_ref, wq_ref, wo_ref,
                    g1_ref, b1_ref, g2_ref, b2_ref, x2_ref, x2p_ref):
    mixed = jnp.concatenate([oret_ref[...], onsa_ref[...]], axis=1)
    x1 = _layer_norm(DN_ALPHA * x_ref[...] + _dot(mixed, wout_ref[...]), g1_ref[...], b1_ref[...])
    q = (_dot(x1.astype(BF16), wq_ref[...]) * (XATT_DIM ** -0.5)).astype(BF16)
    heads = []
    for h in range(XATT_HEADS):
        cols = slice(h * XATT_DIM, (h + 1) * XATT_DIM)
        s = _dot_nt(q[:, cols], kv_ref[:, cols])
        m = jnp.max(s, axis=-1, keepdims=True)
        p = jnp.exp(s - m)
        l = jnp.sum(p, axis=-1, keepdims=True)
        heads.append(_dot(p.astype(BF16), kv_ref[:, D_MODEL + h * XATT_DIM:D_MODEL + (h + 1) * XATT_DIM]) / l)
    att = jnp.concatenate(heads, axis=1).astype(BF16)
    x2 = _layer_norm(DN_ALPHA * x1 + _dot(att, wo_ref[...]), g2_ref[...], b2_ref[...])
    x2_ref[...] = x2
    x2p_ref[...] = _pack_halves(x2)


def _postmix(x2d, o_ret, o_nsa, kvx, w_out, w_xq, w_xo, ln1_g, ln1_b, ln2_g, ln2_b, batch0, batch, seq,
             after=None):
    n = batch * seq
    tm = 512 if seq % 512 == 0 else seq
    per_b = seq // tm
    row = lambda w: pl.BlockSpec((tm, w), lambda b, i: (b * per_b + i, 0))
    full = lambda a: pl.BlockSpec(a.shape, lambda b, i: (0,) * a.ndim)
    ws = [w_out.astype(BF16), w_xq.astype(BF16), w_xo.astype(BF16)]
    vecs = [v.reshape(1, D_MODEL) for v in (ln1_g, ln1_b, ln2_g, ln2_b)]
    args = [x2d, o_ret, o_nsa, kvx, *ws, *vecs]
    in_specs = ([pl.BlockSpec((tm, D_MODEL), lambda b, i: ((batch0 + b) * per_b + i, 0)),
                 row(RET_WIDTH), row(NSA_WIDTH),
                 pl.BlockSpec((MEM_LEN, 2 * D_MODEL), lambda b, i: (batch0 + b, 0))]
                + [full(w) for w in ws] + [full(v) for v in vecs])
    return pl.pallas_call(
        _add_anchors(_postmix_kernel, args, in_specs, after),
        grid=(batch, per_b),
        in_specs=in_specs,
        out_specs=[row(D_MODEL),
                   row(D_MODEL // 2)],
        out_shape=[jax.ShapeDtypeStruct((n, D_MODEL), F32),
                   jax.ShapeDtypeStruct((n, D_MODEL // 2), jnp.uint32)],
        compiler_params=_params(2),
        name="postmix",
    )(*args)


def _router_kernel(x_ref, wr_ref, bias_ref, e_ref, rank_ref, w_ref, cnt_ref, cntrow_ref, carry_ref, carryrow_ref):
    tn = x_ref.shape[0]
    per = N_EXPERTS // N_GROUPS

    @pl.when(pl.program_id(0) == 0)
    def _():
        carry_ref[...] = jnp.zeros_like(carry_ref)
        carryrow_ref[...] = jnp.zeros_like(carryrow_ref)

    logits = _dot_nt(wr_ref[...], x_ref[...].astype(BF16))
    scores = jax.nn.sigmoid(logits)
    biased = scores + bias_ref[...]
    b3 = biased.reshape(N_GROUPS, per, tn)
    member = lax.broadcasted_iota(jnp.int32, (N_GROUPS, per, tn), 1)
    top1 = jnp.max(b3, axis=1, keepdims=True)
    first1 = jnp.min(jnp.where(b3 == top1, member, per), axis=1, keepdims=True)
    top2 = jnp.max(jnp.where(member == first1, -jnp.inf, b3), axis=1, keepdims=True)
    gscore = top1 + top2
    gidx = lax.broadcasted_iota(jnp.int32, (N_GROUPS, 1, tn), 0)
    gwork = gscore
    for _ in range(TOPK_GROUPS - 1):
        gbest = jnp.max(gwork, axis=0, keepdims=True)
        gfirst = jnp.min(jnp.where(gwork == gbest, gidx, N_GROUPS), axis=0, keepdims=True)
        gwork = jnp.where(gidx == gfirst, -jnp.inf, gwork)
    kth = jnp.max(gwork, axis=0, keepdims=True)
    work = jnp.where(gscore >= kth, b3, NEG).reshape(N_EXPERTS, tn)
    eidx = lax.broadcasted_iota(jnp.int32, (N_EXPERTS, tn), 0)
    picks = []
    chosen = jnp.zeros((N_EXPERTS, tn), F32)
    for _ in range(TOP_K):
        best = jnp.max(work, axis=0, keepdims=True)
        first = jnp.min(jnp.where(work == best, eidx, N_EXPERTS), axis=0, keepdims=True)
        hit = eidx == first
        picks.append((first, hit))
        chosen = jnp.where(hit, 1.0, chosen)
        work = jnp.where(hit, -jnp.inf, work)

    r_i = lax.broadcasted_iota(jnp.int32, (tn, tn), 0)
    c_i = lax.broadcasted_iota(jnp.int32, (tn, tn), 1)
    before = jnp.where(r_i < c_i, 1.0, 0.0).astype(BF16)
    chosen_b = chosen.astype(BF16)
    rank = _dot(chosen_b, before) + carry_ref[...]
    carry_ref[...] = carry_ref[...] + jnp.sum(chosen, axis=1, keepdims=True)
    carryrow_ref[...] = carryrow_ref[...] + _dot_nt(jnp.ones((8, tn), BF16), chosen_b)
    cnt_ref[...] = carry_ref[...]
    cntrow_ref[...] = carryrow_ref[...]

    wsel = [jnp.sum(jnp.where(hit, scores, 0.0), axis=0, keepdims=True) for _, hit in picks]
    wsum = wsel[0]
    for v in wsel[1:]:
        wsum = wsum + v
    for kk, (first, hit) in enumerate(picks):
        e_ref[kk:kk + 1, :] = first
        rank_ref[kk:kk + 1, :] = jnp.sum(jnp.where(hit, rank, 0.0), axis=0, keepdims=True).astype(jnp.int32)
        w_ref[kk:kk + 1, :] = wsel[kk] / wsum * ROUTED_SCALE


def _router(x2, w_router, router_bias):
    n = x2.shape[0]
    tn = 512 if n % 512 == 0 else n
    wr_t = w_router.T.astype(BF16)
    bias = router_bias.reshape(N_EXPERTS, 1).astype(F32)
    kspec = pl.BlockSpec((TOP_K, tn), lambda i: (0, i))
    return pl.pallas_call(
        _router_kernel,
        grid=(n // tn,),
        in_specs=[pl.BlockSpec((tn, D_MODEL), lambda i: (i, 0)),
                  pl.BlockSpec(wr_t.shape, lambda i: (0, 0)),
                  pl.BlockSpec(bias.shape, lambda i: (0, 0))],
        out_specs=[kspec, kspec, kspec, pl.BlockSpec((N_EXPERTS, 1), lambda i: (0, 0)),
                   pl.BlockSpec((8, N_EXPERTS), lambda i: (0, 0))],
        out_shape=[jax.ShapeDtypeStruct((TOP_K, n), jnp.int32),
                   jax.ShapeDtypeStruct((TOP_K, n), jnp.int32),
                   jax.ShapeDtypeStruct((TOP_K, n), F32),
                   jax.ShapeDtypeStruct((N_EXPERTS, 1), F32),
                   jax.ShapeDtypeStruct((8, N_EXPERTS), F32)],
        scratch_shapes=[pltpu.VMEM((N_EXPERTS, 1), F32), pltpu.VMEM((8, N_EXPERTS), F32)],
        compiler_params=_params(1),
        name="router",
    )(x2, wr_t, bias)


def _slots_kernel(e_ref, rank_ref, cnt_ref, cntrow_ref, dest_ref, blk_e_ref, valid_ref, used_ref,
                  *, blk, n_blocks):
    pad = lambda c: jnp.ceil(c / blk) * blk
    cnt = cnt_ref[...]
    padded = pad(cnt)
    padded_row = pad(cntrow_ref[0:1, :])
    r_i = lax.broadcasted_iota(jnp.int32, (N_EXPERTS, N_EXPERTS), 0)
    c_i = lax.broadcasted_iota(jnp.int32, (N_EXPERTS, N_EXPERTS), 1)
    start = jnp.sum(jnp.where(c_i < r_i, padded_row, 0.0), axis=1, keepdims=True)
    end = start + padded
    e = e_ref[...]
    dest = rank_ref[...]
    for ex in range(N_EXPERTS):
        dest = dest + jnp.where(e == ex, start[ex:ex + 1, :].astype(jnp.int32), 0)
    dest_ref[...] = dest
    bstart = (lax.broadcasted_iota(jnp.int32, (1, n_blocks), 1) * blk).astype(F32)
    owner = jnp.sum(jnp.where(end <= bstart, 1.0, 0.0), axis=0, keepdims=True)
    blk_e_ref[...] = jnp.minimum(owner, N_EXPERTS - 1.0).astype(jnp.int32)
    inside = (start <= bstart) & (bstart < end)
    real = jnp.clip(start + cnt - bstart, 0.0, float(blk))
    valid = jnp.sum(jnp.where(inside, real, 0.0), axis=0, keepdims=True)
    valid_ref[...] = valid.astype(jnp.int32)
    used_ref[...] = jnp.sum(jnp.where(valid > 0.0, 1.0, 0.0), axis=1, keepdims=True).astype(jnp.int32)


def _slots(e_k, rank_k, counts, counts_row, blk, n_blocks):
    n = e_k.shape[1]
    full = lambda shape: pl.BlockSpec(shape, lambda: (0,) * len(shape))
    return pl.pallas_call(
        functools.partial(_slots_kernel, blk=blk, n_blocks=n_blocks),
        in_specs=[full((TOP_K, n)), full((TOP_K, n)), full((N_EXPERTS, 1)), full((8, N_EXPERTS))],
        out_specs=[full((TOP_K, n)), full((1, n_blocks)), full((1, n_blocks)), full((1, 1))],
        out_shape=[jax.ShapeDtypeStruct((TOP_K, n), jnp.int32),
                   jax.ShapeDtypeStruct((1, n_blocks), jnp.int32),
                   jax.ShapeDtypeStruct((1, n_blocks), jnp.int32),
                   jax.ShapeDtypeStruct((1, 1), jnp.int32)],
        compiler_params=pltpu.CompilerParams(vmem_limit_bytes=VMEM_LIMIT),
        name="slots",
    )(e_k, rank_k, counts, counts_row)


def _sc_worker_base(per_worker):
    return (lax.axis_index("s") * SC_CORES + lax.axis_index("c")) * per_worker


def _sc_scatter_rows(rows, idx, n_out):
    n, width = rows.shape
    k_lists = idx.shape[0] // n
    workers = SC_CORES * SC_SUBCORES
    per_worker = n // workers
    assert per_worker * workers == n and per_worker % SC_CHUNK == 0
    mesh = plsc.VectorSubcoreMesh(core_axis_name="c", subcore_axis_name="s")

    @functools.partial(
        pl.kernel, mesh=mesh,
        out_type=jax.ShapeDtypeStruct((n_out, width), rows.dtype),
        scratch_types=[pltpu.VMEM((SC_CHUNK, width), rows.dtype)]
                      + [pltpu.VMEM((SC_CHUNK,), jnp.int32)] * k_lists + [pltpu.SemaphoreType.DMA] * 3,
        name="sc_scatter")
    def scatter(rows_hbm, idx_hbm, out_hbm, rows_v, *rest):
        idx_vs = rest[:k_lists]
        sem_rows, sem_idx, sem_out = rest[k_lists:]
        base = _sc_worker_base(per_worker)

        @pl.loop(0, per_worker // SC_CHUNK)
        def _(ci):
            off = pl.multiple_of(base + ci * SC_CHUNK, SC_CHUNK)
            loads = [pltpu.async_copy(rows_hbm.at[pl.ds(off, SC_CHUNK)], rows_v, sem_rows)]
            loads += [pltpu.async_copy(idx_hbm.at[pl.ds(pl.multiple_of(k * n + off, SC_CHUNK), SC_CHUNK)],
                                       idx_vs[k], sem_idx) for k in range(k_lists)]
            for c in loads:
                c.wait()
            copies = [pltpu.async_copy(rows_v, out_hbm.at[idx_vs[k]], sem_out) for k in range(k_lists)]
            for c in copies:
                c.wait()

    return scatter(rows, idx)


def _experts_kernel(blk_e_ref, valid_ref, used_ref, xs_ref, wg_ref, wu_ref, wd_ref, y_ref, wg_b, wu_b, wd_b):
    del used_ref
    i = pl.program_id(0)
    valid = valid_ref[i]

    @pl.when((i == 0) | (blk_e_ref[i] != blk_e_ref[jnp.maximum(i - 1, 0)]))
    def _():
        wg_b[...] = wg_ref[...].astype(BF16)
        wu_b[...] = wu_ref[...].astype(BF16)
        wd_b[...] = wd_ref[...].astype(BF16)

    @pl.when(valid > 0)
    def _():
        half = D_MODEL // 2
        row = lax.broadcasted_iota(jnp.int32, (xs_ref.shape[0], 1), 0)
        hi, lo = (v.astype(BF16) for v in _unpack_halves(jnp.where(row < valid, xs_ref[...], jnp.uint32(0))))
        gate = _dot(hi, wg_b[:half, :]) + _dot(lo, wg_b[half:, :])
        up = _dot(hi, wu_b[:half, :]) + _dot(lo, wu_b[half:, :])
        y_ref[...] = _pack_halves(_dot((jax.nn.silu(gate) * up).astype(BF16), wd_b[...]))

    @pl.when(valid <= 0)
    def _():
        y_ref[...] = jnp.zeros_like(y_ref)


def _experts(blk_e, valid, n_used, xs, w_gate, w_up, w_down, blk, after=None):
    cap, width = xs.shape
    wspec = lambda a: pl.BlockSpec((None,) + a.shape[1:], lambda i, be, nv, nu: (be[i], 0, 0))
    rows = pl.BlockSpec((blk, width), lambda i, be, nv, nu: (jnp.minimum(i, nu[0]), 0))
    args = [blk_e, valid, n_used, xs, w_gate, w_up, w_down]
    in_specs = [rows, wspec(w_gate), wspec(w_up), wspec(w_down)]
    body = _add_anchors(_experts_kernel, args, in_specs, after)
    return pl.pallas_call(
        body,
        grid_spec=pltpu.PrefetchScalarGridSpec(
            num_scalar_prefetch=3,
            grid=(cap // blk,),
            in_specs=in_specs,
            out_specs=rows,
            scratch_shapes=[pltpu.VMEM(w.shape[1:], BF16) for w in (w_gate, w_up, w_down)],
        ),
        out_shape=jax.ShapeDtypeStruct(xs.shape, xs.dtype),
        compiler_params=_params(1),
        name="experts",
    )(*args)


def _sc_gather_rows(table, idx):
    b, width = idx.shape[0], table.shape[1]
    workers = SC_CORES * SC_SUBCORES
    per_worker = b // workers
    assert per_worker * workers == b and per_worker % (SC_CHUNK * SC_INFLIGHT) == 0
    mesh = plsc.VectorSubcoreMesh(core_axis_name="c", subcore_axis_name="s")

    @functools.partial(
        pl.kernel, mesh=mesh,
        out_type=jax.ShapeDtypeStruct((b, width), table.dtype),
        scratch_types=[pltpu.VMEM((SC_CHUNK,), jnp.int32)] * SC_INFLIGHT
                      + [pltpu.VMEM((SC_CHUNK, width), table.dtype)] * SC_INFLIGHT
                      + [pltpu.SemaphoreType.DMA] * (1 + 2 * SC_INFLIGHT),
        name="sc_gather")
    def gather(table_hbm, idx_hbm, out_hbm, *scratch):
        idx_vs = scratch[:SC_INFLIGHT]
        rows_vs = scratch[SC_INFLIGHT:2 * SC_INFLIGHT]
        sem_idx = scratch[2 * SC_INFLIGHT]
        sem_rows = scratch[2 * SC_INFLIGHT + 1:3 * SC_INFLIGHT + 1]
        sem_out = scratch[3 * SC_INFLIGHT + 1:]
        base = _sc_worker_base(per_worker)
        lanes = range(SC_INFLIGHT)

        @pl.loop(0, per_worker // (SC_CHUNK * SC_INFLIGHT))
        def _(gi):
            offs = [pl.multiple_of(base + (gi * SC_INFLIGHT + j) * SC_CHUNK, SC_CHUNK) for j in lanes]
            loads = [pltpu.async_copy(idx_hbm.at[pl.ds(offs[j], SC_CHUNK)], idx_vs[j], sem_idx) for j in lanes]
            for c in loads:
                c.wait()
            gathers = [pltpu.async_copy(table_hbm.at[idx_vs[j]], rows_vs[j], sem_rows[j]) for j in lanes]
            writes = []
            for j in lanes:
                gathers[j].wait()
                writes.append(pltpu.async_copy(rows_vs[j], out_hbm.at[pl.ds(offs[j], SC_CHUNK)], sem_out[j]))
            for c in writes:
                c.wait()

    return gather(table, idx)


def _combine_kernel(x_ref, wk_ref, yk_ref, wsg_ref, wsu_ref, wsd_ref, g_ref, b_ref, *rest):
    o_ref = rest[-1]
    x = x_ref[...]
    xb = x.astype(BF16)
    shared = _dot((jax.nn.silu(_dot(xb, wsg_ref[...])) * _dot(xb, wsu_ref[...])).astype(BF16), wsd_ref[...])
    wk = wk_ref[...]
    routed_hi = routed_lo = None
    for kk in range(TOP_K):
        hi, lo = _unpack_halves(yk_ref[kk])
        w = wk[:, kk:kk + 1]
        routed_hi = hi * w if kk == 0 else routed_hi + hi * w
        routed_lo = lo * w if kk == 0 else routed_lo + lo * w
    routed = jnp.concatenate([routed_hi, routed_lo], axis=1)
    o_ref[...] = _layer_norm(DN_ALPHA * x + (routed + shared), g_ref[...], b_ref[...])


def _combine(x2, w_tok, yk, ws_gate, ws_up, ws_down, ln3_g, ln3_b, row0, n_total, out_prev, after=None,
             part=(0, 1)):
    n = x2.shape[0] // part[1]
    sub0 = part[0] * n
    tt = 512 if n % 512 == 0 and row0 % 512 == 0 else n
    blk0 = (row0 + sub0) // tt
    sub_blk = sub0 // tt
    ws = [ws_gate.astype(BF16), ws_up.astype(BF16), ws_down.astype(BF16)]
    vecs = [ln3_g.reshape(1, D_MODEL), ln3_b.reshape(1, D_MODEL)]
    full = lambda a: pl.BlockSpec(a.shape, lambda i: (0,) * a.ndim)
    args = [x2, w_tok, yk, *ws, *vecs]
    in_specs = ([pl.BlockSpec((tt, D_MODEL), lambda i: (sub_blk + i, 0)),
                 pl.BlockSpec((tt, TOP_K), lambda i: (sub_blk + i, 0)),
                 pl.BlockSpec((TOP_K, tt, D_MODEL // 2), lambda i: (0, sub_blk + i, 0))]
                + [full(a) for a in ws] + [full(v) for v in vecs])
    aliases = {}
    for anchor in (a for a in (after or ()) if a is not None):
        args.append(anchor)
        in_specs.append(pl.BlockSpec(memory_space=pl.ANY))
    if out_prev is not None:
        aliases = {len(args): 0}
        args.append(out_prev)
        in_specs.append(pl.BlockSpec(memory_space=pl.ANY))
    return pl.pallas_call(
        _combine_kernel,
        grid=(n // tt,),
        in_specs=in_specs,
        out_specs=pl.BlockSpec((tt, D_MODEL), lambda i: (blk0 + i, 0)),
        out_shape=jax.ShapeDtypeStruct((n_total, D_MODEL), F32),
        input_output_aliases=aliases,
        compiler_params=_params(1),
        name="combine",
    )(*args)


EXPERT_BLOCK = 512


def _moe_dispatch(x2, x2p, w_router, router_bias):
    n = x2.shape[0]
    cap = n * TOP_K + N_EXPERTS * EXPERT_BLOCK
    e_k, rank_k, w_k, counts, counts_row = _router(x2, w_router, router_bias)
    dest, blk_e, valid, n_used = _slots(e_k, rank_k, counts, counts_row, EXPERT_BLOCK, cap // EXPERT_BLOCK)
    dest = dest.reshape(-1)
    return dict(w_tok=w_k.T, dest=dest, blk_e=blk_e.reshape(-1), valid=valid.reshape(-1),
                n_used=n_used.reshape(-1), xs=_sc_scatter_rows(x2p, dest, cap))


def _moe_experts(routed, w_gate, w_up, w_down, after):
    y = _experts(routed["blk_e"], routed["valid"], routed["n_used"], routed["xs"], w_gate, w_up, w_down,
                 EXPERT_BLOCK, after=after)
    n = routed["dest"].shape[0] // TOP_K
    return y, _sc_gather_rows(y, routed["dest"]).reshape(TOP_K, n, D_MODEL // 2)


def _layer(x, mem, positions, w_in, cmp_pe_k, cmp_pe_v, cmp_w1_k, cmp_w2_k, cmp_w1_v, cmp_w2_v,
           w_out, ln1_g, ln1_b, w_xq, w_xkv, w_xo, ln2_g, ln2_b, w_router, router_bias,
           w_gate, w_up, w_down, ws_gate, ws_up, ws_down, ln3_g, ln3_b):
    batch, seq, _ = x.shape
    n_total = batch * seq
    x2d = x.reshape(n_total, D_MODEL)
    pos_col = positions.astype(F32).reshape(n_total, 1)
    kvx = _memkv(mem.reshape(batch * MEM_LEN, D_MODEL), w_xkv)
    last = max(1, batch // 4)
    sizes = [batch - last, last] if batch > 1 else [batch]
    starts = [sum(sizes[:g]) for g in range(len(sizes))]

    def mixers_in(g):
        nb, row0 = sizes[g], starts[g] * seq
        (rq, rk, rv, rg, nq, nqr, kc, vc, ks, vs, kw, vw, gates) = _inproj(x2d, pos_col, w_in, row0, nb * seq)
        o_ret = _retention(rq, rk, rv, rg, nb, seq)
        kcmp = _compress(kc, cmp_pe_k, cmp_w1_k, cmp_w2_k, nb, seq)
        vcmp = _compress(vc, cmp_pe_v, cmp_w1_v, cmp_w2_v, nb, seq)
        return o_ret, (nq, nqr, gates, kcmp, vcmp, ks, vs, kw, vw)

    def attend(g, nsa_args, after):
        return _nsa(*nsa_args, sizes[g], seq, after=after)

    def mix_and_route(g, o_ret, o_nsa, after):
        x2, x2p = _postmix(x2d, o_ret, o_nsa, kvx, w_out, w_xq, w_xo, ln1_g, ln1_b, ln2_g, ln2_b,
                           starts[g], sizes[g], seq, after=after)
        return x2, _moe_dispatch(x2, x2p, w_router, router_bias)

    def combine(g, x2, routed, yk, out_prev, after=None, part=(0, 1)):
        return _combine(x2, routed["w_tok"], yk, ws_gate, ws_up, ws_down, ln3_g, ln3_b, starts[g] * seq,
                        n_total, out_prev, after=after, part=part)

    o_ret, nsa_args = mixers_in(0)
    x2, routed = mix_and_route(0, o_ret, attend(0, nsa_args, None), None)
    if len(sizes) == 1:
        y, yk = _moe_experts(routed, w_gate, w_up, w_down, after=None)
        return combine(0, x2, routed, yk, None).reshape(batch, seq, D_MODEL)
    o_ret1, nsa_args1 = mixers_in(1)
    y, yk = _moe_experts(routed, w_gate, w_up, w_down, after=[o_ret1])
    o_nsa1 = attend(1, nsa_args1, [y])
    x2_1, routed1 = mix_and_route(1, o_ret1, o_nsa1, [yk])
    out = combine(0, x2, routed, yk, None, [routed1["dest"]], part=(0, 2))
    y1, yk1 = _moe_experts(routed1, w_gate, w_up, w_down, after=[out])
    out = combine(0, x2, routed, yk, out, [y1], part=(1, 2))
    out = combine(1, x2_1, routed1, yk1, out)
    return out.reshape(batch, seq, D_MODEL)


def kernel(x, mem, positions, w_in, cmp_pe_k, cmp_pe_v, cmp_w1_k, cmp_w2_k, cmp_w1_v, cmp_w2_v, w_out, ln1_g, ln1_b, w_xq, w_xkv, w_xo, ln2_g, ln2_b, w_router, router_bias, w_gate, w_up, w_down, ws_gate, ws_up, ws_down, ln3_g, ln3_b):
    for l in range(DEPTH):
        x = _layer(x, mem, positions, w_in[l], cmp_pe_k[l], cmp_pe_v[l], cmp_w1_k[l], cmp_w2_k[l],
                   cmp_w1_v[l], cmp_w2_v[l], w_out[l], ln1_g[l], ln1_b[l], w_xq[l], w_xkv[l],
                   w_xo[l], ln2_g[l], ln2_b[l], w_router[l], router_bias[l], w_gate[l], w_up[l],
                   w_down[l], ws_gate[l], ws_up[l], ws_down[l], ln3_g[l], ln3_b[l])
    return x
```

```python
import functools

import numpy as np
import jax
import jax.numpy as jnp
from jax import lax
from jax.experimental import pallas as pl
from jax.experimental.pallas import tpu as pltpu
from jax.experimental.pallas import tpu_sc as plsc

D_MODEL = 1024
MEM_LEN = 256
DEPTH = 1
DN_ALPHA = (2 * DEPTH) ** 0.25
LN_EPS = 1e-5
NEG = -1e30

RET_HEADS = 4
RET_DIM = 128
RET_CHUNK = 128
RET_ROPE_BASE = 10000.0
RET_STEP_CHUNKS = 4
RET_WIDTH = RET_HEADS * RET_DIM

NSA_HEADS = 8
NSA_KV_GROUPS = 2
NSA_HPG = NSA_HEADS // NSA_KV_GROUPS
NSA_DIM = 64
NSA_WIDTH = NSA_HEADS * NSA_DIM
KV_WIDTH = NSA_KV_GROUPS * NSA_DIM
CMP_LEN = 32
CMP_STRIDE = 16
CMP_HIDDEN = 256
SEL_LEN = 64
SEL_SHIFT = 6
SEL_TOPK = 16
N_FORCED = 3
WIN = 512
ROPE_THETA = 500000.0
ROPE_DIMS = NSA_DIM // 4
GATE_LANES = 16
NSA_CHAINS = 1
SUM_ROWS = 16
WIN_PART = 256

SC_CORES = 2
SC_SUBCORES = 16
SC_CHUNK = 64
SC_INFLIGHT = 2

XATT_HEADS = 4
XATT_DIM = D_MODEL // XATT_HEADS

N_EXPERTS = 64
TOP_K = 8
N_GROUPS = 8
TOPK_GROUPS = 4
EXPERT_FF = 256
SHARED_FF = 256
ROUTED_SCALE = 2.5

LANES = 128
VMEM_LIMIT = 56 * 1024 * 1024

F32 = jnp.float32
BF16 = jnp.bfloat16
NT_DIMS = (((1,), (1,)), ((), ()))


def _params(n_axes):
    return pltpu.CompilerParams(dimension_semantics=("arbitrary",) * n_axes,
                                vmem_limit_bytes=VMEM_LIMIT)


def _dot(a, b):
    return jnp.dot(a, b, preferred_element_type=F32)


def _dot_nt(a, b):
    return lax.dot_general(a, b, NT_DIMS, preferred_element_type=F32)


def _layer_norm(v, g, b):
    mu = jnp.mean(v, axis=-1, keepdims=True)
    d = v - mu
    var = jnp.mean(d * d, axis=-1, keepdims=True)
    return d * lax.rsqrt(var + LN_EPS) * g + b


def _inproj_kernel(x_ref, pos_ref, wret_ref, wnq_ref, wkv_ref, wg_ref, invr_ref, invn_ref,
                   rq_ref, rk_ref, rv_ref, rg_ref, nq_ref, nqr_ref, kc_ref, vc_ref,
                   ks_ref, vs_ref, kw_ref, vw_ref, gate_ref):
    xb = x_ref[...].astype(BF16)
    pos = pos_ref[...]
    lane = lax.broadcasted_iota(jnp.int32, (1, LANES), 1)

    ang = pos * invr_ref[...]
    cos_r = jnp.cos(ang)
    sin_r = jnp.sin(ang)
    sin_r = jnp.where(lane < RET_DIM // 2, -sin_r, sin_r)
    q_all = _dot(xb, wret_ref[:, :RET_WIDTH])
    k_all = _dot(xb, wret_ref[:, RET_WIDTH:2 * RET_WIDTH])
    for h in range(RET_HEADS):
        cols = slice(h * RET_DIM, (h + 1) * RET_DIM)
        q = q_all[:, cols]
        rq_ref[:, cols] = (q * cos_r + pltpu.roll(q, RET_DIM // 2, 1) * sin_r).astype(BF16)
        k = k_all[:, cols]
        k = (k * cos_r + pltpu.roll(k, RET_DIM // 2, 1) * sin_r) * (RET_DIM ** -0.5)
        rk_ref[:, cols] = k.astype(BF16)
    rv_ref[...] = _dot(xb, wret_ref[:, 2 * RET_WIDTH:3 * RET_WIDTH]).astype(BF16)
    rg_ref[...] = _dot(xb, wret_ref[:, 3 * RET_WIDTH:4 * RET_WIDTH]).astype(BF16)

    half = ROPE_DIMS // 2
    j = lane % NSA_DIM
    angn = pos * invn_ref[...]
    cos_n = jnp.cos(angn)
    sin_n = jnp.sin(angn)
    sin_lo = jnp.where(j < half, -sin_n, 0.0)
    sin_hi = jnp.where((j >= half) & (j < 2 * half), sin_n, 0.0)

    def rope_n(v):
        return v * cos_n + pltpu.roll(v, half, 1) * sin_hi + pltpu.roll(v, LANES - half, 1) * sin_lo

    scale = NSA_DIM ** -0.5
    nq_all = _dot(xb, wnq_ref[...])
    for c in range(NSA_WIDTH // LANES):
        cols = slice(c * LANES, (c + 1) * LANES)
        q = nq_all[:, cols]
        nq_ref[:, cols] = (q * scale).astype(BF16)
        nqr_ref[:, cols] = (rope_n(q) * scale).astype(BF16)

    kv_all = _dot(xb, wkv_ref[...])

    def kv(i):
        return kv_all[:, i * KV_WIDTH:(i + 1) * KV_WIDTH]

    def split_groups(ref, v):
        for g in range(NSA_KV_GROUPS):
            ref[g] = v[:, g * NSA_DIM:(g + 1) * NSA_DIM].astype(BF16)

    kc_ref[...] = kv(0)
    vc_ref[...] = kv(1)
    split_groups(ks_ref, rope_n(kv(2)))
    split_groups(vs_ref, kv(3))
    split_groups(kw_ref, rope_n(kv(4)))
    split_groups(vw_ref, kv(5))

    gt = jax.nn.sigmoid(_dot_nt(wg_ref[...], xb))
    for g in range(NSA_KV_GROUPS):
        gate_ref[g] = gt[g * GATE_LANES:(g + 1) * GATE_LANES, :]


def _inproj(x2d, pos_col, w_in, row0, n):
    tm = 1024 if n % 1024 == 0 and row0 % 1024 == 0 else n
    blk0 = row0 // tm
    off = np.cumsum([0] + [RET_WIDTH] * 4 + [NSA_WIDTH] + [KV_WIDTH] * 6)
    w_ret = w_in[:, :off[4]].astype(BF16)
    w_nq = w_in[:, off[4]:off[5]].astype(BF16)
    w_kv = w_in[:, off[5]:off[11]].astype(BF16)
    wg = w_in[:, off[11]:].reshape(D_MODEL, NSA_KV_GROUPS, NSA_HPG * 3)
    wg = jnp.pad(wg, ((0, 0), (0, 0), (0, GATE_LANES - NSA_HPG * 3)))
    wg = wg.reshape(D_MODEL, NSA_KV_GROUPS * GATE_LANES).T.astype(BF16)

    lane = np.arange(LANES)
    half_r = RET_DIM // 2
    inv_r = (np.float32(RET_ROPE_BASE) ** (-np.arange(half_r, dtype=np.float32) / np.float32(half_r)))
    inv_r = inv_r.astype(np.float32)[lane % half_r][None, :]
    half_n = ROPE_DIMS // 2
    inv_n = (np.float32(ROPE_THETA) ** (-np.arange(half_n, dtype=np.float32) / np.float32(half_n)))
    jn = lane % NSA_DIM
    inv_n = np.where(jn < ROPE_DIMS, inv_n.astype(np.float32)[jn % half_n], np.float32(0.0))[None, :]

    row = lambda w: pl.BlockSpec((tm, w), lambda i: (i, 0))
    src_row = lambda w: pl.BlockSpec((tm, w), lambda i: (blk0 + i, 0))
    full = lambda a: pl.BlockSpec(a.shape, lambda i: (0,) * a.ndim)
    grp = lambda w: pl.BlockSpec((NSA_KV_GROUPS, tm, w), lambda i: (0, i, 0))
    bf = lambda w: jax.ShapeDtypeStruct((n, w), BF16)
    gbf = jax.ShapeDtypeStruct((NSA_KV_GROUPS, n, NSA_DIM), BF16)
    inv_r = jnp.asarray(inv_r, F32)
    inv_n = jnp.asarray(inv_n, F32)
    return pl.pallas_call(
        _inproj_kernel,
        grid=(n // tm,),
        in_specs=[src_row(D_MODEL), src_row(1), full(w_ret), full(w_nq), full(w_kv), full(wg),
                  full(inv_r), full(inv_n)],
        out_specs=[row(RET_WIDTH)] * 4 + [row(NSA_WIDTH)] * 2 + [row(KV_WIDTH)] * 2
                  + [grp(NSA_DIM)] * 4
                  + [pl.BlockSpec((NSA_KV_GROUPS, GATE_LANES, tm), lambda i: (0, 0, i))],
        out_shape=[bf(RET_WIDTH)] * 4 + [bf(NSA_WIDTH)] * 2
                  + [jax.ShapeDtypeStruct((n, KV_WIDTH), F32)] * 2 + [gbf] * 4
                  + [jax.ShapeDtypeStruct((NSA_KV_GROUPS, GATE_LANES, n), F32)],
        compiler_params=_params(1),
        name="inproj",
    )(x2d, pos_col, w_ret, w_nq, w_kv, wg, inv_r, inv_n)


def _retention_kernel(q_ref, k_ref, v_ref, g_ref, o_ref, state_ref):
    c = RET_CHUNK

    @pl.when(pl.program_id(1) == 0)
    def _():
        state_ref[...] = jnp.zeros_like(state_ref)

    row = lax.broadcasted_iota(jnp.int32, (c, c), 0)
    col = lax.broadcasted_iota(jnp.int32, (c, c), 1)
    rel = (row - col).astype(F32)
    idx = lax.broadcasted_iota(jnp.int32, (c, 1), 0).astype(F32)
    for h in range(RET_HEADS):
        log_g = float(np.log(np.float32(1.0) - np.float32(2.0) ** np.float32(-5.0 - h)))
        cols = slice(h * RET_DIM, (h + 1) * RET_DIM)
        dmask = jnp.where(rel >= 0, jnp.exp(log_g * jnp.maximum(rel, 0.0)), 0.0)
        zeta = jnp.exp(log_g * (c - 1.0 - idx))
        xi = jnp.exp(log_g * (idx + 1.0))
        for j in range(q_ref.shape[0] // c):
            rows = slice(j * c, (j + 1) * c)
            q = q_ref[rows, cols]
            k = k_ref[rows, cols]
            v = v_ref[rows, cols]
            scores = _dot_nt(q, k) * dmask
            inner = _dot(scores.astype(BF16), v)
            prev = state_ref[h]
            cross = _dot(q, prev.astype(BF16)) * xi
            kz = (k.astype(F32) * zeta).astype(BF16)
            kv = lax.dot_general(kz, v, (((0,), (0,)), ((), ())), preferred_element_type=F32)
            state_ref[h] = prev * float(np.exp(np.float32(log_g) * np.float32(c))) + kv
            o = inner + cross
            mu = jnp.mean(o, axis=-1, keepdims=True)
            d = o - mu
            var = jnp.mean(d * d, axis=-1, keepdims=True)
            o = d * lax.rsqrt(var + LN_EPS)
            o_ref[rows, cols] = (jax.nn.silu(g_ref[rows, cols].astype(F32)) * o).astype(BF16)


def _retention(rq, rk, rv, rg, batch, seq):
    per_step = RET_STEP_CHUNKS if (seq // RET_CHUNK) % RET_STEP_CHUNKS == 0 else 1
    nc = seq // (RET_CHUNK * per_step)
    spec = pl.BlockSpec((RET_CHUNK * per_step, RET_WIDTH), lambda b, n: (b * nc + n, 0))
    return pl.pallas_call(
        _retention_kernel,
        grid=(batch, nc),
        in_specs=[spec] * 4,
        out_specs=spec,
        out_shape=jax.ShapeDtypeStruct(rq.shape, BF16),
        scratch_shapes=[pltpu.VMEM((RET_HEADS, RET_DIM, RET_DIM), F32)],
        compiler_params=_params(2),
        name="retention",
    )(rq, rk, rv, rg)


def _compress_kernel(a_ref, pe_ref, w1_ref, w2_ref, o_ref, shift_ref, *, n_cmp):
    rows = a_ref.shape[0]
    a = a_ref[...]
    lo = (a + pe_ref[0]).astype(BF16)
    hi = (a + pe_ref[1]).astype(BF16)
    ridx = lax.broadcasted_iota(jnp.int32, (rows, 1), 0)
    shift_ref[rows:rows + 8, :] = jnp.zeros((8, CMP_HIDDEN), F32)
    for g in range(NSA_KV_GROUPS):
        p = _dot(lo, w1_ref[0, g])
        shift_ref[0:rows, :] = _dot(hi, w1_ref[1, g])
        hid = jax.nn.silu(p + shift_ref[pl.ds(1, rows), :])
        out = _dot(hid.astype(BF16), w2_ref[...])
        o_ref[g] = jnp.where(ridx < n_cmp, out, 0.0).astype(BF16)


def _compress(a, pe, w1, w2, batch, seq):
    rows = seq // CMP_STRIDE
    per = CMP_STRIDE * KV_WIDTH
    n_cmp = (seq - CMP_LEN) // CMP_STRIDE + 1
    a2 = a.reshape(batch * rows, per)
    pe2 = jnp.tile(pe.reshape(2, CMP_STRIDE, 1, NSA_DIM), (1, 1, NSA_KV_GROUPS, 1)).reshape(2, 1, per)
    w1r = w1.reshape(2, CMP_STRIDE, 1, NSA_DIM, CMP_HIDDEN)
    eye = jnp.eye(NSA_KV_GROUPS, dtype=w1.dtype).reshape(1, NSA_KV_GROUPS, 1, NSA_KV_GROUPS, 1, 1)
    w1x = (w1r[:, None] * eye).reshape(2, NSA_KV_GROUPS, per, CMP_HIDDEN).astype(BF16)
    w2b = w2.astype(BF16)
    full = lambda arr: pl.BlockSpec(arr.shape, lambda b: (0,) * arr.ndim)
    return pl.pallas_call(
        functools.partial(_compress_kernel, n_cmp=n_cmp),
        grid=(batch,),
        in_specs=[pl.BlockSpec((rows, per), lambda b: (b, 0)), full(pe2), full(w1x), full(w2b)],
        out_specs=pl.BlockSpec((None, NSA_KV_GROUPS, rows, NSA_DIM), lambda b: (b, 0, 0, 0)),
        out_shape=jax.ShapeDtypeStruct((batch, NSA_KV_GROUPS, rows, NSA_DIM), BF16),
        scratch_shapes=[pltpu.VMEM((rows + 8, CMP_HIDDEN), F32)],
        compiler_params=_params(1),
        name="compress",
    )(a2, pe2, w1x, w2b)


def _heads_to_lanes(ref):
    vt = ref[...].astype(F32).T
    return jnp.concatenate([vt[h * NSA_DIM:(h + 1) * NSA_DIM] for h in range(NSA_HPG)], axis=1).astype(BF16)


def _tile_heads(v):
    return jnp.concatenate([v] * NSA_HPG, axis=1)


def _transpose_into(dst_ref, src_ref, chunk):
    def step(c, _):
        c0 = pl.multiple_of(c * chunk, chunk)
        dst_ref[:NSA_DIM, pl.ds(c0, chunk)] = src_ref[pl.ds(c0, chunk), :].astype(F32).T.astype(BF16)
        return 0
    lax.fori_loop(0, src_ref.shape[0] // chunk, step, 0)


def _nsa_kernel(qraw_ref, qrot_ref, gate_ref, kcmp_ref, vcmp_ref, ovt_ref,
                ks_ref, vs_ref, kw_ref, vw_ref, o_ref, vst_ref, vwt_ref, vct_ref, bias_ref, *, tq, tk, seq):
    i = pl.program_id(2)
    t0 = i * tq
    cols = NSA_HPG * tq
    n_sel = seq // SEL_LEN
    n_cmp_rows = seq // CMP_STRIDE
    blocks_per_tile = tk // SEL_LEN

    @pl.when(i == 0)
    def _():
        chunk = min(512, n_cmp_rows)
        _transpose_into(vst_ref, vs_ref, chunk)
        _transpose_into(vwt_ref, vw_ref, chunk)
        _transpose_into(vct_ref, vcmp_ref, chunk)
        vst_ref[NSA_DIM:, :] = jnp.ones((SUM_ROWS, seq), BF16)
        vwt_ref[NSA_DIM:, :] = jnp.ones((SUM_ROWS, seq), BF16)

    def split_sum(acc):
        return acc[:NSA_DIM] / acc[NSA_DIM:NSA_DIM + 1]

    q_raw = _heads_to_lanes(qraw_ref)
    q_rot = _heads_to_lanes(qrot_ref)
    t_row = t0 + lax.broadcasted_iota(jnp.int32, (1, tq), 1)

    chain_w = cols // NSA_CHAINS
    heads_per_chain = chain_w // tq
    chains = [slice(c * chain_w, (c + 1) * chain_w) for c in range(NSA_CHAINS)]
    tile_chain = lambda v: jnp.concatenate([v] * heads_per_chain, axis=1)

    pw = min(tq, WIN_PART)
    parts = []
    for u in range(tq // pw):
        span = WIN + pw
        ws = pl.multiple_of(jnp.maximum(t0 + u * pw - WIN, 0), pw)
        dist = t_row[:, u * pw:(u + 1) * pw] - (ws + lax.broadcasted_iota(jnp.int32, (span, 1), 0))
        bias_w = jnp.concatenate([jnp.where((dist >= 0) & (dist < WIN), 0.0, NEG)] * NSA_HPG, axis=1)
        q_part = jnp.concatenate([q_rot[:, h * tq + u * pw:h * tq + (u + 1) * pw] for h in range(NSA_HPG)],
                                 axis=1)
        s_w = _dot(kw_ref[pl.ds(ws, span), :], q_part) + bias_w
        p_w = jnp.exp((s_w - jnp.max(s_w, axis=0, keepdims=True)).astype(BF16))
        parts.append(split_sum(_dot(vwt_ref[:, pl.ds(ws, span)], p_w)))
    o_w = jnp.concatenate([parts[u][:, h * pw:(h + 1) * pw]
                           for h in range(NSA_HPG) for u in range(tq // pw)], axis=1)

    c_idx = lax.broadcasted_iota(jnp.int32, (n_cmp_rows, 1), 0)
    valid = tile_chain(jnp.where(c_idx * CMP_STRIDE + (CMP_LEN - 1) <= t_row, 1.0, 0.0))
    bias_c = (valid - 1.0) * (-NEG)
    o_c = []
    p_sum = None
    for c in chains:
        s_c = _dot(kcmp_ref[...], q_raw[:, c]) + bias_c
        e_c = jnp.exp(s_c - jnp.max(s_c, axis=0, keepdims=True)) * valid
        l_c = jnp.sum(e_c, axis=0, keepdims=True)
        p_c = e_c / jnp.where(l_c > 0.0, l_c, 1.0)
        o_c.append(_dot(vct_ref[...], p_c.astype(BF16)))
        for h in range(heads_per_chain):
            p_h = p_c[:, h * tq:(h + 1) * tq]
            p_sum = p_h if p_sum is None else p_sum + p_h
    o_c = jnp.concatenate(o_c, axis=1)

    p_hi = p_sum.astype(BF16)
    p_lo = (p_sum - p_hi.astype(F32)).astype(BF16)
    ovt = ovt_ref[...]
    imp = _dot(ovt, p_hi) + _dot(ovt, p_lo)
    jb = lax.broadcasted_iota(jnp.int32, (n_sel, tq), 0)
    cur = (t0 + lax.broadcasted_iota(jnp.int32, (n_sel, tq), 1)) >> SEL_SHIFT
    forced = (jb == 0) | (jb == cur) | (jb == cur - 1)
    work = jnp.where(forced, -jnp.inf, imp)
    work = jnp.where(jb <= cur, work, NEG)
    sel_t = jnp.where(forced, 1.0, 0.0)
    for _ in range(max(min(SEL_TOPK, n_sel) - N_FORCED, 0)):
        best = jnp.max(work, axis=0, keepdims=True)
        first = jnp.min(jnp.where(work == best, jb, n_sel), axis=0, keepdims=True)
        hit = jb == first
        sel_t = jnp.where(hit, 1.0, sel_t)
        work = jnp.where(hit, -jnp.inf, work)
    bias_ref[...] = jnp.where(sel_t > 0.5, 0.0, NEG)

    def sel_tile(kt, carry, causal):
        k0 = pl.multiple_of(kt * tk, tk)
        bias = jnp.concatenate(
            [jnp.broadcast_to(bias_ref[pl.ds(kt * blocks_per_tile + j, 1), :], (SEL_LEN, tq))
             for j in range(blocks_per_tile)], axis=0)
        if causal:
            kpos = k0 + lax.broadcasted_iota(jnp.int32, (tk, 1), 0)
            bias = jnp.where(kpos <= t_row, bias, NEG)
        bias = tile_chain(bias)
        k_t = ks_ref[pl.ds(k0, tk), :]
        v_t = vst_ref[:, pl.ds(k0, tk)]
        out = []
        scores = [_dot(k_t, q_rot[:, c]) + bias for c in chains]
        for (m, acc), s in zip(carry, scores):
            m_new = jnp.maximum(m, jnp.max(s, axis=0, keepdims=True))
            p = jnp.exp((s - m_new).astype(BF16))
            acc = jnp.exp(m - m_new) * acc + _dot(v_t, p)
            out.append((m_new, acc))
        return tuple(out)

    n_full = t0 // tk
    init = tuple((jnp.full((1, chain_w), NEG, F32), jnp.zeros((NSA_DIM + SUM_ROWS, chain_w), F32))
                 for _ in chains)
    carry = lax.fori_loop(0, n_full, functools.partial(sel_tile, causal=False), init)
    for d in range(max(tq // tk, 1)):
        carry = sel_tile(n_full + d, carry, causal=True)
    o_s = jnp.concatenate([split_sum(acc) for _, acc in carry], axis=1)

    gt = gate_ref[...]
    outs = []
    for h in range(NSA_HPG):
        c = slice(h * tq, (h + 1) * tq)
        outs.append(gt[3 * h:3 * h + 1] * o_c[:, c] + gt[3 * h + 1:3 * h + 2] * o_s[:, c]
                    + gt[3 * h + 2:3 * h + 3] * o_w[:, c])
    o_ref[...] = jnp.concatenate(outs, axis=0).T.astype(BF16)


def _anchored(kernel_fn, n_inputs, n_anchors):
    def body(*refs, **static):
        kernel_fn(*refs[:n_inputs], *refs[n_inputs + n_anchors:], **static)
    return body


def _add_anchors(kernel_fn, args, in_specs, after):
    after = [a for a in (after or ()) if a is not None]
    if not after:
        return kernel_fn
    body = _anchored(kernel_fn, len(args), len(after))
    args.extend(after)
    in_specs.extend([pl.BlockSpec(memory_space=pl.ANY)] * len(after))
    return body


def _nsa(nq, nqr, gates, kcmp, vcmp, ks, vs, kw, vw, batch, seq, after=None):
    n = batch * seq
    tq = 512
    tk = 512 if seq % 512 == 0 else seq
    nqb = seq // tq
    n_sel = seq // SEL_LEN
    rows_c = seq // CMP_STRIDE
    gw = NSA_HPG * NSA_DIM
    cs = np.arange(rows_c)[None, :] * CMP_STRIDE
    ss = np.arange(n_sel)[:, None] * SEL_LEN
    n_cmp = (seq - CMP_LEN) // CMP_STRIDE + 1
    ovt = ((cs < ss + SEL_LEN) & (cs + CMP_LEN > ss) & (np.arange(rows_c)[None, :] < n_cmp))
    ovt = jnp.asarray(ovt.astype(np.float32), BF16)

    qspec = pl.BlockSpec((tq, gw), lambda b, g, i: (b * nqb + i, g))
    cspec = pl.BlockSpec((None, None, rows_c, NSA_DIM), lambda b, g, i: (b, g, 0, 0))
    kspec = pl.BlockSpec((None, seq, NSA_DIM), lambda b, g, i: (g, b, 0))
    args = [nq, nqr, gates, kcmp, vcmp, ovt, ks, vs, kw, vw]
    in_specs = [qspec, qspec,
                pl.BlockSpec((None, GATE_LANES, tq), lambda b, g, i: (g, 0, b * nqb + i)),
                cspec, cspec, pl.BlockSpec(ovt.shape, lambda b, g, i: (0, 0)),
                kspec, kspec, kspec, kspec]
    body = functools.partial(_add_anchors(_nsa_kernel, args, in_specs, after), tq=tq, tk=tk, seq=seq)
    return pl.pallas_call(
        body,
        grid=(batch, NSA_KV_GROUPS, nqb),
        in_specs=in_specs,
        out_specs=qspec,
        out_shape=jax.ShapeDtypeStruct((n, NSA_WIDTH), BF16),
        scratch_shapes=[pltpu.VMEM((NSA_DIM + SUM_ROWS, seq), BF16), pltpu.VMEM((NSA_DIM + SUM_ROWS, seq), BF16),
                        pltpu.VMEM((NSA_DIM, rows_c), BF16), pltpu.VMEM((n_sel, tq), F32)],
        compiler_params=_params(3),
        name="nsa",
    )(*args)


def _memkv_kernel(mem_ref, w_ref, kv_ref):
    kv_ref[...] = _dot(mem_ref[...].astype(BF16), w_ref[...]).astype(BF16)


def _memkv(mem2d, w_xkv):
    n = mem2d.shape[0]
    w = w_xkv.astype(BF16)
    return pl.pallas_call(
        _memkv_kernel,
        grid=(n // MEM_LEN,),
        in_specs=[pl.BlockSpec((MEM_LEN, D_MODEL), lambda i: (i, 0)),
                  pl.BlockSpec(w.shape, lambda i: (0, 0))],
        out_specs=pl.BlockSpec((MEM_LEN, 2 * D_MODEL), lambda i: (i, 0)),
        out_shape=jax.ShapeDtypeStruct((n, 2 * D_MODEL), BF16),
        compiler_params=_params(1),
        name="memkv",
    )(mem2d, w)


def _pack_halves(v):
    half = D_MODEL // 2
    hi = pltpu.bitcast(v[:, :half].astype(BF16).astype(F32), jnp.uint32)
    lo = pltpu.bitcast(v[:, half:].astype(BF16).astype(F32), jnp.uint32)
    return hi | (lo >> 16)


def _unpack_halves(words):
    return pltpu.bitcast(words & jnp.uint32(0xFFFF0000), F32), pltpu.bitcast(words << 16, F32)


def _postmix_kernel(x_ref, oret_ref, onsa_ref, kv_ref, wout_ref, wq_ref, wo_ref,
                    g1_ref, b1_ref, g2_ref, b2_ref, x2_ref, x2p_ref):
    mixed = jnp.concatenate([oret_ref[...], onsa_ref[...]], axis=1)
    x1 = _layer_norm(DN_ALPHA * x_ref[...] + _dot(mixed, wout_ref[...]), g1_ref[...], b1_ref[...])
    q = (_dot(x1.astype(BF16), wq_ref[...]) * (XATT_DIM ** -0.5)).astype(BF16)
    heads = []
    for h in range(XATT_HEADS):
        cols = slice(h * XATT_DIM, (h + 1) * XATT_DIM)
        s = _dot_nt(q[:, cols], kv_ref[:, cols])
        m = jnp.max(s, axis=-1, keepdims=True)
        p = jnp.exp(s - m)
        l = jnp.sum(p, axis=-1, keepdims=True)
        heads.append(_dot(p.astype(BF16), kv_ref[:, D_MODEL + h * XATT_DIM:D_MODEL + (h + 1) * XATT_DIM]) / l)
    att = jnp.concatenate(heads, axis=1).astype(BF16)
    x2 = _layer_norm(DN_ALPHA * x1 + _dot(att, wo_ref[...]), g2_ref[...], b2_ref[...])
    x2_ref[...] = x2
    x2p_ref[...] = _pack_halves(x2)


def _postmix(x2d, o_ret, o_nsa, kvx, w_out, w_xq, w_xo, ln1_g, ln1_b, ln2_g, ln2_b, batch0, batch, seq,
             after=None):
    n = batch * seq
    tm = 512 if seq % 512 == 0 else seq
    per_b = seq // tm
    row = lambda w: pl.BlockSpec((tm, w), lambda b, i: (b * per_b + i, 0))
    full = lambda a: pl.BlockSpec(a.shape, lambda b, i: (0,) * a.ndim)
    ws = [w_out.astype(BF16), w_xq.astype(BF16), w_xo.astype(BF16)]
    vecs = [v.reshape(1, D_MODEL) for v in (ln1_g, ln1_b, ln2_g, ln2_b)]
    args = [x2d, o_ret, o_nsa, kvx, *ws, *vecs]
    in_specs = ([pl.BlockSpec((tm, D_MODEL), lambda b, i: ((batch0 + b) * per_b + i, 0)),
                 row(RET_WIDTH), row(NSA_WIDTH),
                 pl.BlockSpec((MEM_LEN, 2 * D_MODEL), lambda b, i: (batch0 + b, 0))]
                + [full(w) for w in ws] + [full(v) for v in vecs])
    return pl.pallas_call(
        _add_anchors(_postmix_kernel, args, in_specs, after),
        grid=(batch, per_b),
        in_specs=in_specs,
        out_specs=[row(D_MODEL),
                   row(D_MODEL // 2)],
        out_shape=[jax.ShapeDtypeStruct((n, D_MODEL), F32),
                   jax.ShapeDtypeStruct((n, D_MODEL // 2), jnp.uint32)],
        compiler_params=_params(2),
        name="postmix",
    )(*args)


def _router_kernel(x_ref, wr_ref, bias_ref, e_ref, rank_ref, w_ref, cnt_ref, cntrow_ref, carry_ref, carryrow_ref):
    tn = x_ref.shape[0]
    per = N_EXPERTS // N_GROUPS

    @pl.when(pl.program_id(0) == 0)
    def _():
        carry_ref[...] = jnp.zeros_like(carry_ref)
        carryrow_ref[...] = jnp.zeros_like(carryrow_ref)

    logits = _dot_nt(wr_ref[...], x_ref[...].astype(BF16))
    scores = jax.nn.sigmoid(logits)
    biased = scores + bias_ref[...]
    b3 = biased.reshape(N_GROUPS, per, tn)
    member = lax.broadcasted_iota(jnp.int32, (N_GROUPS, per, tn), 1)
    top1 = jnp.max(b3, axis=1, keepdims=True)
    first1 = jnp.min(jnp.where(b3 == top1, member, per), axis=1, keepdims=True)
    top2 = jnp.max(jnp.where(member == first1, -jnp.inf, b3), axis=1, keepdims=True)
    gscore = top1 + top2
    gidx = lax.broadcasted_iota(jnp.int32, (N_GROUPS, 1, tn), 0)
    gwork = gscore
    for _ in range(TOPK_GROUPS - 1):
        gbest = jnp.max(gwork, axis=0, keepdims=True)
        gfirst = jnp.min(jnp.where(gwork == gbest, gidx, N_GROUPS), axis=0, keepdims=True)
        gwork = jnp.where(gidx == gfirst, -jnp.inf, gwork)
    kth = jnp.max(gwork, axis=0, keepdims=True)
    work = jnp.where(gscore >= kth, b3, NEG).reshape(N_EXPERTS, tn)
    eidx = lax.broadcasted_iota(jnp.int32, (N_EXPERTS, tn), 0)
    picks = []
    chosen = jnp.zeros((N_EXPERTS, tn), F32)
    for _ in range(TOP_K):
        best = jnp.max(work, axis=0, keepdims=True)
        first = jnp.min(jnp.where(work == best, eidx, N_EXPERTS), axis=0, keepdims=True)
        hit = eidx == first
        picks.append((first, hit))
        chosen = jnp.where(hit, 1.0, chosen)
        work = jnp.where(hit, -jnp.inf, work)

    r_i = lax.broadcasted_iota(jnp.int32, (tn, tn), 0)
    c_i = lax.broadcasted_iota(jnp.int32, (tn, tn), 1)
    before = jnp.where(r_i < c_i, 1.0, 0.0).astype(BF16)
    chosen_b = chosen.astype(BF16)
    rank = _dot(chosen_b, before) + carry_ref[...]
    carry_ref[...] = carry_ref[...] + jnp.sum(chosen, axis=1, keepdims=True)
    carryrow_ref[...] = carryrow_ref[...] + _dot_nt(jnp.ones((8, tn), BF16), chosen_b)
    cnt_ref[...] = carry_ref[...]
    cntrow_ref[...] = carryrow_ref[...]

    wsel = [jnp.sum(jnp.where(hit, scores, 0.0), axis=0, keepdims=True) for _, hit in picks]
    wsum = wsel[0]
    for v in wsel[1:]:
        wsum = wsum + v
    for kk, (first, hit) in enumerate(picks):
        e_ref[kk:kk + 1, :] = first
        rank_ref[kk:kk + 1, :] = jnp.sum(jnp.where(hit, rank, 0.0), axis=0, keepdims=True).astype(jnp.int32)
        w_ref[kk:kk + 1, :] = wsel[kk] / wsum * ROUTED_SCALE


def _router(x2, w_router, router_bias):
    n = x2.shape[0]
    tn = 512 if n % 512 == 0 else n
    wr_t = w_router.T.astype(BF16)
    bias = router_bias.reshape(N_EXPERTS, 1).astype(F32)
    kspec = pl.BlockSpec((TOP_K, tn), lambda i: (0, i))
    return pl.pallas_call(
        _router_kernel,
        grid=(n // tn,),
        in_specs=[pl.BlockSpec((tn, D_MODEL), lambda i: (i, 0)),
                  pl.BlockSpec(wr_t.shape, lambda i: (0, 0)),
                  pl.BlockSpec(bias.shape, lambda i: (0, 0))],
        out_specs=[kspec, kspec, kspec, pl.BlockSpec((N_EXPERTS, 1), lambda i: (0, 0)),
                   pl.BlockSpec((8, N_EXPERTS), lambda i: (0, 0))],
        out_shape=[jax.ShapeDtypeStruct((TOP_K, n), jnp.int32),
                   jax.ShapeDtypeStruct((TOP_K, n), jnp.int32),
                   jax.ShapeDtypeStruct((TOP_K, n), F32),
                   jax.ShapeDtypeStruct((N_EXPERTS, 1), F32),
                   jax.ShapeDtypeStruct((8, N_EXPERTS), F32)],
        scratch_shapes=[pltpu.VMEM((N_EXPERTS, 1), F32), pltpu.VMEM((8, N_EXPERTS), F32)],
        compiler_params=_params(1),
        name="router",
    )(x2, wr_t, bias)


def _slots_kernel(e_ref, rank_ref, cnt_ref, cntrow_ref, dest_ref, blk_e_ref, valid_ref, used_ref,
                  *, blk, n_blocks):
    pad = lambda c: jnp.ceil(c / blk) * blk
    cnt = cnt_ref[...]
    padded = pad(cnt)
    padded_row = pad(cntrow_ref[0:1, :])
    r_i = lax.broadcasted_iota(jnp.int32, (N_EXPERTS, N_EXPERTS), 0)
    c_i = lax.broadcasted_iota(jnp.int32, (N_EXPERTS, N_EXPERTS), 1)
    start = jnp.sum(jnp.where(c_i < r_i, padded_row, 0.0), axis=1, keepdims=True)
    end = start + padded
    e = e_ref[...]
    dest = rank_ref[...]
    for ex in range(N_EXPERTS):
        dest = dest + jnp.where(e == ex, start[ex:ex + 1, :].astype(jnp.int32), 0)
    dest_ref[...] = dest
    bstart = (lax.broadcasted_iota(jnp.int32, (1, n_blocks), 1) * blk).astype(F32)
    owner = jnp.sum(jnp.where(end <= bstart, 1.0, 0.0), axis=0, keepdims=True)
    blk_e_ref[...] = jnp.minimum(owner, N_EXPERTS - 1.0).astype(jnp.int32)
    inside = (start <= bstart) & (bstart < end)
    real = jnp.clip(start + cnt - bstart, 0.0, float(blk))
    valid = jnp.sum(jnp.where(inside, real, 0.0), axis=0, keepdims=True)
    valid_ref[...] = valid.astype(jnp.int32)
    used_ref[...] = jnp.sum(jnp.where(valid > 0.0, 1.0, 0.0), axis=1, keepdims=True).astype(jnp.int32)


def _slots(e_k, rank_k, counts, counts_row, blk, n_blocks):
    n = e_k.shape[1]
    full = lambda shape: pl.BlockSpec(shape, lambda: (0,) * len(shape))
    return pl.pallas_call(
        functools.partial(_slots_kernel, blk=blk, n_blocks=n_blocks),
        in_specs=[full((TOP_K, n)), full((TOP_K, n)), full((N_EXPERTS, 1)), full((8, N_EXPERTS))],
        out_specs=[full((TOP_K, n)), full((1, n_blocks)), full((1, n_blocks)), full((1, 1))],
        out_shape=[jax.ShapeDtypeStruct((TOP_K, n), jnp.int32),
                   jax.ShapeDtypeStruct((1, n_blocks), jnp.int32),
                   jax.ShapeDtypeStruct((1, n_blocks), jnp.int32),
                   jax.ShapeDtypeStruct((1, 1), jnp.int32)],
        compiler_params=pltpu.CompilerParams(vmem_limit_bytes=VMEM_LIMIT),
        name="slots",
    )(e_k, rank_k, counts, counts_row)


def _sc_worker_base(per_worker):
    return (lax.axis_index("s") * SC_CORES + lax.axis_index("c")) * per_worker


def _sc_scatter_rows(rows, idx, n_out):
    n, width = rows.shape
    k_lists = idx.shape[0] // n
    workers = SC_CORES * SC_SUBCORES
    per_worker = n // workers
    assert per_worker * workers == n and per_worker % SC_CHUNK == 0
    mesh = plsc.VectorSubcoreMesh(core_axis_name="c", subcore_axis_name="s")

    @functools.partial(
        pl.kernel, mesh=mesh,
        out_type=jax.ShapeDtypeStruct((n_out, width), rows.dtype),
        scratch_types=[pltpu.VMEM((SC_CHUNK, width), rows.dtype)]
                      + [pltpu.VMEM((SC_CHUNK,), jnp.int32)] * k_lists + [pltpu.SemaphoreType.DMA] * 3,
        name="sc_scatter")
    def scatter(rows_hbm, idx_hbm, out_hbm, rows_v, *rest):
        idx_vs = rest[:k_lists]
        sem_rows, sem_idx, sem_out = rest[k_lists:]
        base = _sc_worker_base(per_worker)

        @pl.loop(0, per_worker // SC_CHUNK)
        def _(ci):
            off = pl.multiple_of(base + ci * SC_CHUNK, SC_CHUNK)
            loads = [pltpu.async_copy(rows_hbm.at[pl.ds(off, SC_CHUNK)], rows_v, sem_rows)]
            loads += [pltpu.async_copy(idx_hbm.at[pl.ds(pl.multiple_of(k * n + off, SC_CHUNK), SC_CHUNK)],
                                       idx_vs[k], sem_idx) for k in range(k_lists)]
            for c in loads:
                c.wait()
            copies = [pltpu.async_copy(rows_v, out_hbm.at[idx_vs[k]], sem_out) for k in range(k_lists)]
            for c in copies:
                c.wait()

    return scatter(rows, idx)


def _experts_kernel(blk_e_ref, valid_ref, used_ref, xs_ref, wg_ref, wu_ref, wd_ref, y_ref, wg_b, wu_b, wd_b):
    del used_ref
    i = pl.program_id(0)
    valid = valid_ref[i]

    @pl.when((i == 0) | (blk_e_ref[i] != blk_e_ref[jnp.maximum(i - 1, 0)]))
    def _():
        wg_b[...] = wg_ref[...].astype(BF16)
        wu_b[...] = wu_ref[...].astype(BF16)
        wd_b[...] = wd_ref[...].astype(BF16)

    @pl.when(valid > 0)
    def _():
        half = D_MODEL // 2
        row = lax.broadcasted_iota(jnp.int32, (xs_ref.shape[0], 1), 0)
        hi, lo = (v.astype(BF16) for v in _unpack_halves(jnp.where(row < valid, xs_ref[...], jnp.uint32(0))))
        gate = _dot(hi, wg_b[:half, :]) + _dot(lo, wg_b[half:, :])
        up = _dot(hi, wu_b[:half, :]) + _dot(lo, wu_b[half:, :])
        y_ref[...] = _pack_halves(_dot((jax.nn.silu(gate) * up).astype(BF16), wd_b[...]))

    @pl.when(valid <= 0)
    def _():
        y_ref[...] = jnp.zeros_like(y_ref)


def _experts(blk_e, valid, n_used, xs, w_gate, w_up, w_down, blk, after=None):
    cap, width = xs.shape
    wspec = lambda a: pl.BlockSpec((None,) + a.shape[1:], lambda i, be, nv, nu: (be[i], 0, 0))
    rows = pl.BlockSpec((blk, width), lambda i, be, nv, nu: (jnp.minimum(i, nu[0]), 0))
    args = [blk_e, valid, n_used, xs, w_gate, w_up, w_down]
    in_specs = [rows, wspec(w_gate), wspec(w_up), wspec(w_down)]
    body = _add_anchors(_experts_kernel, args, in_specs, after)
    return pl.pallas_call(
        body,
        grid_spec=pltpu.PrefetchScalarGridSpec(
            num_scalar_prefetch=3,
            grid=(cap // blk,),
            in_specs=in_specs,
            out_specs=rows,
            scratch_shapes=[pltpu.VMEM(w.shape[1:], BF16) for w in (w_gate, w_up, w_down)],
        ),
        out_shape=jax.ShapeDtypeStruct(xs.shape, xs.dtype),
        compiler_params=_params(1),
        name="experts",
    )(*args)


def _sc_gather_rows(table, idx):
    b, width = idx.shape[0], table.shape[1]
    workers = SC_CORES * SC_SUBCORES
    per_worker = b // workers
    assert per_worker * workers == b and per_worker % (SC_CHUNK * SC_INFLIGHT) == 0
    mesh = plsc.VectorSubcoreMesh(core_axis_name="c", subcore_axis_name="s")

    @functools.partial(
        pl.kernel, mesh=mesh,
        out_type=jax.ShapeDtypeStruct((b, width), table.dtype),
        scratch_types=[pltpu.VMEM((SC_CHUNK,), jnp.int32)] * SC_INFLIGHT
                      + [pltpu.VMEM((SC_CHUNK, width), table.dtype)] * SC_INFLIGHT
                      + [pltpu.SemaphoreType.DMA] * (1 + 2 * SC_INFLIGHT),
        name="sc_gather")
    def gather(table_hbm, idx_hbm, out_hbm, *scratch):
        idx_vs = scratch[:SC_INFLIGHT]
        rows_vs = scratch[SC_INFLIGHT:2 * SC_INFLIGHT]
        sem_idx = scratch[2 * SC_INFLIGHT]
        sem_rows = scratch[2 * SC_INFLIGHT + 1:3 * SC_INFLIGHT + 1]
        sem_out = scratch[3 * SC_INFLIGHT + 1:]
        base = _sc_worker_base(per_worker)
        lanes = range(SC_INFLIGHT)

        @pl.loop(0, per_worker // (SC_CHUNK * SC_INFLIGHT))
        def _(gi):
            offs = [pl.multiple_of(base + (gi * SC_INFLIGHT + j) * SC_CHUNK, SC_CHUNK) for j in lanes]
            loads = [pltpu.async_copy(idx_hbm.at[pl.ds(offs[j], SC_CHUNK)], idx_vs[j], sem_idx) for j in lanes]
            for c in loads:
                c.wait()
            gathers = [pltpu.async_copy(table_hbm.at[idx_vs[j]], rows_vs[j], sem_rows[j]) for j in lanes]
            writes = []
            for j in lanes:
                gathers[j].wait()
                writes.append(pltpu.async_copy(rows_vs[j], out_hbm.at[pl.ds(offs[j], SC_CHUNK)], sem_out[j]))
            for c in writes:
                c.wait()

    return gather(table, idx)


def _combine_kernel(x_ref, wk_ref, yk_ref, wsg_ref, wsu_ref, wsd_ref, g_ref, b_ref, *rest):
    o_ref = rest[-1]
    x = x_ref[...]
    xb = x.astype(BF16)
    shared = _dot((jax.nn.silu(_dot(xb, wsg_ref[...])) * _dot(xb, wsu_ref[...])).astype(BF16), wsd_ref[...])
    wk = wk_ref[...]
    routed_hi = routed_lo = None
    for kk in range(TOP_K):
        hi, lo = _unpack_halves(yk_ref[kk])
        w = wk[:, kk:kk + 1]
        routed_hi = hi * w if kk == 0 else routed_hi + hi * w
        routed_lo = lo * w if kk == 0 else routed_lo + lo * w
    routed = jnp.concatenate([routed_hi, routed_lo], axis=1)
    o_ref[...] = _layer_norm(DN_ALPHA * x + (routed + shared), g_ref[...], b_ref[...])


def _combine(x2, w_tok, yk, ws_gate, ws_up, ws_down, ln3_g, ln3_b, row0, n_total, out_prev, after=None,
             part=(0, 1)):
    n = x2.shape[0] // part[1]
    sub0 = part[0] * n
    tt = 512 if n % 512 == 0 and row0 % 512 == 0 else n
    blk0 = (row0 + sub0) // tt
    sub_blk = sub0 // tt
    ws = [ws_gate.astype(BF16), ws_up.astype(BF16), ws_down.astype(BF16)]
    vecs = [ln3_g.reshape(1, D_MODEL), ln3_b.reshape(1, D_MODEL)]
    full = lambda a: pl.BlockSpec(a.shape, lambda i: (0,) * a.ndim)
    args = [x2, w_tok, yk, *ws, *vecs]
    in_specs = ([pl.BlockSpec((tt, D_MODEL), lambda i: (sub_blk + i, 0)),
                 pl.BlockSpec((tt, TOP_K), lambda i: (sub_blk + i, 0)),
                 pl.BlockSpec((TOP_K, tt, D_MODEL // 2), lambda i: (0, sub_blk + i, 0))]
                + [full(a) for a in ws] + [full(v) for v in vecs])
    aliases = {}
    for anchor in (a for a in (after or ()) if a is not None):
        args.append(anchor)
        in_specs.append(pl.BlockSpec(memory_space=pl.ANY))
    if out_prev is not None:
        aliases = {len(args): 0}
        args.append(out_prev)
        in_specs.append(pl.BlockSpec(memory_space=pl.ANY))
    return pl.pallas_call(
        _combine_kernel,
        grid=(n // tt,),
        in_specs=in_specs,
        out_specs=pl.BlockSpec((tt, D_MODEL), lambda i: (blk0 + i, 0)),
        out_shape=jax.ShapeDtypeStruct((n_total, D_MODEL), F32),
        input_output_aliases=aliases,
        compiler_params=_params(1),
        name="combine",
    )(*args)


EXPERT_BLOCK = 1024


def _moe_dispatch(x2, x2p, w_router, router_bias):
    n = x2.shape[0]
    cap = n * TOP_K + N_EXPERTS * EXPERT_BLOCK
    e_k, rank_k, w_k, counts, counts_row = _router(x2, w_router, router_bias)
    dest, blk_e, valid, n_used = _slots(e_k, rank_k, counts, counts_row, EXPERT_BLOCK, cap // EXPERT_BLOCK)
    dest = dest.reshape(-1)
    return dict(w_tok=w_k.T, dest=dest, blk_e=blk_e.reshape(-1), valid=valid.reshape(-1),
                n_used=n_used.reshape(-1), xs=_sc_scatter_rows(x2p, dest, cap))


def _moe_experts(routed, w_gate, w_up, w_down, after):
    y = _experts(routed["blk_e"], routed["valid"], routed["n_used"], routed["xs"], w_gate, w_up, w_down,
                 EXPERT_BLOCK, after=after)
    n = routed["dest"].shape[0] // TOP_K
    return y, _sc_gather_rows(y, routed["dest"]).reshape(TOP_K, n, D_MODEL // 2)


def _layer(x, mem, positions, w_in, cmp_pe_k, cmp_pe_v, cmp_w1_k, cmp_w2_k, cmp_w1_v, cmp_w2_v,
           w_out, ln1_g, ln1_b, w_xq, w_xkv, w_xo, ln2_g, ln2_b, w_router, router_bias,
           w_gate, w_up, w_down, ws_gate, ws_up, ws_down, ln3_g, ln3_b):
    batch, seq, _ = x.shape
    n_total = batch * seq
    x2d = x.reshape(n_total, D_MODEL)
    pos_col = positions.astype(F32).reshape(n_total, 1)
    kvx = _memkv(mem.reshape(batch * MEM_LEN, D_MODEL), w_xkv)
    last = max(1, batch // 4)
    sizes = [batch - last, last] if batch > 1 else [batch]
    starts = [sum(sizes[:g]) for g in range(len(sizes))]

    def mixers_in(g):
        nb, row0 = sizes[g], starts[g] * seq
        (rq, rk, rv, rg, nq, nqr, kc, vc, ks, vs, kw, vw, gates) = _inproj(x2d, pos_col, w_in, row0, nb * seq)
        o_ret = _retention(rq, rk, rv, rg, nb, seq)
        kcmp = _compress(kc, cmp_pe_k, cmp_w1_k, cmp_w2_k, nb, seq)
        vcmp = _compress(vc, cmp_pe_v, cmp_w1_v, cmp_w2_v, nb, seq)
        return o_ret, (nq, nqr, gates, kcmp, vcmp, ks, vs, kw, vw)

    def attend(g, nsa_args, after):
        return _nsa(*nsa_args, sizes[g], seq, after=after)

    def mix_and_route(g, o_ret, o_nsa, after):
        x2, x2p = _postmix(x2d, o_ret, o_nsa, kvx, w_out, w_xq, w_xo, ln1_g, ln1_b, ln2_g, ln2_b,
                           starts[g], sizes[g], seq, after=after)
        return x2, _moe_dispatch(x2, x2p, w_router, router_bias)

    def combine(g, x2, routed, yk, out_prev, after=None, part=(0, 1)):
        return _combine(x2, routed["w_tok"], yk, ws_gate, ws_up, ws_down, ln3_g, ln3_b, starts[g] * seq,
                        n_total, out_prev, after=after, part=part)

    o_ret, nsa_args = mixers_in(0)
    x2, routed = mix_and_route(0, o_ret, attend(0, nsa_args, None), None)
    if len(sizes) == 1:
        y, yk = _moe_experts(routed, w_gate, w_up, w_down, after=None)
        return combine(0, x2, routed, yk, None).reshape(batch, seq, D_MODEL)
    o_ret1, nsa_args1 = mixers_in(1)
    y, yk = _moe_experts(routed, w_gate, w_up, w_down, after=[o_ret1])
    o_nsa1 = attend(1, nsa_args1, [y])
    x2_1, routed1 = mix_and_route(1, o_ret1, o_nsa1, [yk])
    out = combine(0, x2, routed, yk, None, [routed1["dest"]], part=(0, 2))
    y1, yk1 = _moe_experts(routed1, w_gate, w_up, w_down, after=[out])
    out = combine(0, x2, routed, yk, out, [y1], part=(1, 2))
    out = combine(1, x2_1, routed1, yk1, out)
    return out.reshape(batch, seq, D_MODEL)


def kernel(x, mem, positions, w_in, cmp_pe_k, cmp_pe_v, cmp_w1_k, cmp_w2_k, cmp_w1_v, cmp_w2_v, w_out, ln1_g, ln1_b, w_xq, w_xkv, w_xo, ln2_g, ln2_b, w_router, router_bias, w_gate, w_up, w_down, ws_gate, ws_up, ws_down, ln3_g, ln3_b):
    for l in range(DEPTH):
        x = _layer(x, mem, positions, w_in[l], cmp_pe_k[l], cmp_pe_v[l], cmp_w1_k[l], cmp_w2_k[l],
                   cmp_w1_v[l], cmp_w2_v[l], w_out[l], ln1_g[l], ln1_b[l], w_xq[l], w_xkv[l],
                   w_xo[l], ln2_g[l], ln2_b[l], w_router[l], router_bias[l], w_gate[l], w_up[l],
                   w_down[l], ws_gate[l], ws_up[l], ws_down[l], ln3_g[l], ln3_b[l])
    return x
```

```python
import functools

import numpy as np
import jax
import jax.numpy as jnp
from jax import lax
from jax.experimental import pallas as pl
from jax.experimental.pallas import tpu as pltpu
from jax.experimental.pallas import tpu_sc as plsc

D_MODEL = 1024
MEM_LEN = 256
DEPTH = 1
DN_ALPHA = (2 * DEPTH) ** 0.25
LN_EPS = 1e-5
NEG = -1e30

RET_HEADS = 4
RET_DIM = 128
RET_CHUNK = 128
RET_ROPE_BASE = 10000.0
RET_STEP_CHUNKS = 4
RET_WIDTH = RET_HEADS * RET_DIM

NSA_HEADS = 8
NSA_KV_GROUPS = 2
NSA_HPG = NSA_HEADS // NSA_KV_GROUPS
NSA_DIM = 64
NSA_WIDTH = NSA_HEADS * NSA_DIM
KV_WIDTH = NSA_KV_GROUPS * NSA_DIM
CMP_LEN = 32
CMP_STRIDE = 16
CMP_HIDDEN = 256
SEL_LEN = 64
SEL_SHIFT = 6
SEL_TOPK = 16
N_FORCED = 3
WIN = 512
ROPE_THETA = 500000.0
ROPE_DIMS = NSA_DIM // 4
GATE_LANES = 16
NSA_CHAINS = 1
SUM_ROWS = 16
WIN_PART = 256

SC_CORES = 2
SC_SUBCORES = 16
SC_CHUNK = 64
SC_INFLIGHT = 2

XATT_HEADS = 4
XATT_DIM = D_MODEL // XATT_HEADS

N_EXPERTS = 64
TOP_K = 8
N_GROUPS = 8
TOPK_GROUPS = 4
EXPERT_FF = 256
SHARED_FF = 256
ROUTED_SCALE = 2.5

LANES = 128
VMEM_LIMIT = 56 * 1024 * 1024

F32 = jnp.float32
BF16 = jnp.bfloat16
NT_DIMS = (((1,), (1,)), ((), ()))


def _params(n_axes):
    return pltpu.CompilerParams(dimension_semantics=("arbitrary",) * n_axes,
                                vmem_limit_bytes=VMEM_LIMIT)


def _dot(a, b):
    return jnp.dot(a, b, preferred_element_type=F32)


def _dot_nt(a, b):
    return lax.dot_general(a, b, NT_DIMS, preferred_element_type=F32)


def _layer_norm(v, g, b):
    mu = jnp.mean(v, axis=-1, keepdims=True)
    d = v - mu
    var = jnp.mean(d * d, axis=-1, keepdims=True)
    return d * lax.rsqrt(var + LN_EPS) * g + b


def _inproj_kernel(x_ref, pos_ref, wret_ref, wnq_ref, wkv_ref, wg_ref, invr_ref, invn_ref,
                   rq_ref, rk_ref, rv_ref, rg_ref, nq_ref, nqr_ref, kc_ref, vc_ref,
                   ks_ref, vs_ref, kw_ref, vw_ref, gate_ref):
    xb = x_ref[...].astype(BF16)
    pos = pos_ref[...]
    lane = lax.broadcasted_iota(jnp.int32, (1, LANES), 1)

    ang = pos * invr_ref[...]
    cos_r = jnp.cos(ang)
    sin_r = jnp.sin(ang)
    sin_r = jnp.where(lane < RET_DIM // 2, -sin_r, sin_r)
    q_all = _dot(xb, wret_ref[:, :RET_WIDTH])
    k_all = _dot(xb, wret_ref[:, RET_WIDTH:2 * RET_WIDTH])
    for h in range(RET_HEADS):
        cols = slice(h * RET_DIM, (h + 1) * RET_DIM)
        q = q_all[:, cols]
        rq_ref[:, cols] = (q * cos_r + pltpu.roll(q, RET_DIM // 2, 1) * sin_r).astype(BF16)
        k = k_all[:, cols]
        k = (k * cos_r + pltpu.roll(k, RET_DIM // 2, 1) * sin_r) * (RET_DIM ** -0.5)
        rk_ref[:, cols] = k.astype(BF16)
    rv_ref[...] = _dot(xb, wret_ref[:, 2 * RET_WIDTH:3 * RET_WIDTH]).astype(BF16)
    rg_ref[...] = _dot(xb, wret_ref[:, 3 * RET_WIDTH:4 * RET_WIDTH]).astype(BF16)

    half = ROPE_DIMS // 2
    j = lane % NSA_DIM
    angn = pos * invn_ref[...]
    cos_n = jnp.cos(angn)
    sin_n = jnp.sin(angn)
    sin_lo = jnp.where(j < half, -sin_n, 0.0)
    sin_hi = jnp.where((j >= half) & (j < 2 * half), sin_n, 0.0)

    def rope_n(v):
        return v * cos_n + pltpu.roll(v, half, 1) * sin_hi + pltpu.roll(v, LANES - half, 1) * sin_lo

    scale = NSA_DIM ** -0.5
    nq_all = _dot(xb, wnq_ref[...])
    for c in range(NSA_WIDTH // LANES):
        cols = slice(c * LANES, (c + 1) * LANES)
        q = nq_all[:, cols]
        nq_ref[:, cols] = (q * scale).astype(BF16)
        nqr_ref[:, cols] = (rope_n(q) * scale).astype(BF16)

    kv_all = _dot(xb, wkv_ref[...])

    def kv(i):
        return kv_all[:, i * KV_WIDTH:(i + 1) * KV_WIDTH]

    def split_groups(ref, v):
        for g in range(NSA_KV_GROUPS):
            ref[g] = v[:, g * NSA_DIM:(g + 1) * NSA_DIM].astype(BF16)

    kc_ref[...] = kv(0)
    vc_ref[...] = kv(1)
    split_groups(ks_ref, rope_n(kv(2)))
    split_groups(vs_ref, kv(3))
    split_groups(kw_ref, rope_n(kv(4)))
    split_groups(vw_ref, kv(5))

    gt = jax.nn.sigmoid(_dot_nt(wg_ref[...], xb))
    for g in range(NSA_KV_GROUPS):
        gate_ref[g] = gt[g * GATE_LANES:(g + 1) * GATE_LANES, :]


def _inproj(x2d, pos_col, w_in, row0, n):
    tm = 1024 if n % 1024 == 0 and row0 % 1024 == 0 else n
    blk0 = row0 // tm
    off = np.cumsum([0] + [RET_WIDTH] * 4 + [NSA_WIDTH] + [KV_WIDTH] * 6)
    w_ret = w_in[:, :off[4]].astype(BF16)
    w_nq = w_in[:, off[4]:off[5]].astype(BF16)
    w_kv = w_in[:, off[5]:off[11]].astype(BF16)
    wg = w_in[:, off[11]:].reshape(D_MODEL, NSA_KV_GROUPS, NSA_HPG * 3)
    wg = jnp.pad(wg, ((0, 0), (0, 0), (0, GATE_LANES - NSA_HPG * 3)))
    wg = wg.reshape(D_MODEL, NSA_KV_GROUPS * GATE_LANES).T.astype(BF16)

    lane = np.arange(LANES)
    half_r = RET_DIM // 2
    inv_r = (np.float32(RET_ROPE_BASE) ** (-np.arange(half_r, dtype=np.float32) / np.float32(half_r)))
    inv_r = inv_r.astype(np.float32)[lane % half_r][None, :]
    half_n = ROPE_DIMS // 2
    inv_n = (np.float32(ROPE_THETA) ** (-np.arange(half_n, dtype=np.float32) / np.float32(half_n)))
    jn = lane % NSA_DIM
    inv_n = np.where(jn < ROPE_DIMS, inv_n.astype(np.float32)[jn % half_n], np.float32(0.0))[None, :]

    row = lambda w: pl.BlockSpec((tm, w), lambda i: (i, 0))
    src_row = lambda w: pl.BlockSpec((tm, w), lambda i: (blk0 + i, 0))
    full = lambda a: pl.BlockSpec(a.shape, lambda i: (0,) * a.ndim)
    grp = lambda w: pl.BlockSpec((NSA_KV_GROUPS, tm, w), lambda i: (0, i, 0))
    bf = lambda w: jax.ShapeDtypeStruct((n, w), BF16)
    gbf = jax.ShapeDtypeStruct((NSA_KV_GROUPS, n, NSA_DIM), BF16)
    inv_r = jnp.asarray(inv_r, F32)
    inv_n = jnp.asarray(inv_n, F32)
    return pl.pallas_call(
        _inproj_kernel,
        grid=(n // tm,),
        in_specs=[src_row(D_MODEL), src_row(1), full(w_ret), full(w_nq), full(w_kv), full(wg),
                  full(inv_r), full(inv_n)],
        out_specs=[row(RET_WIDTH)] * 4 + [row(NSA_WIDTH)] * 2 + [row(KV_WIDTH)] * 2
                  + [grp(NSA_DIM)] * 4
                  + [pl.BlockSpec((NSA_KV_GROUPS, GATE_LANES, tm), lambda i: (0, 0, i))],
        out_shape=[bf(RET_WIDTH)] * 4 + [bf(NSA_WIDTH)] * 2
                  + [jax.ShapeDtypeStruct((n, KV_WIDTH), F32)] * 2 + [gbf] * 4
                  + [jax.ShapeDtypeStruct((NSA_KV_GROUPS, GATE_LANES, n), F32)],
        compiler_params=_params(1),
        name="inproj",
    )(x2d, pos_col, w_ret, w_nq, w_kv, wg, inv_r, inv_n)


def _retention_kernel(q_ref, k_ref, v_ref, g_ref, o_ref, state_ref):
    c = RET_CHUNK

    @pl.when(pl.program_id(1) == 0)
    def _():
        state_ref[...] = jnp.zeros_like(state_ref)

    row = lax.broadcasted_iota(jnp.int32, (c, c), 0)
    col = lax.broadcasted_iota(jnp.int32, (c, c), 1)
    rel = (row - col).astype(F32)
    idx = lax.broadcasted_iota(jnp.int32, (c, 1), 0).astype(F32)
    for h in range(RET_HEADS):
        log_g = float(np.log(np.float32(1.0) - np.float32(2.0) ** np.float32(-5.0 - h)))
        cols = slice(h * RET_DIM, (h + 1) * RET_DIM)
        dmask = jnp.where(rel >= 0, jnp.exp(log_g * jnp.maximum(rel, 0.0)), 0.0)
        zeta = jnp.exp(log_g * (c - 1.0 - idx))
        xi = jnp.exp(log_g * (idx + 1.0))
        for j in range(q_ref.shape[0] // c):
            rows = slice(j * c, (j + 1) * c)
            q = q_ref[rows, cols]
            k = k_ref[rows, cols]
            v = v_ref[rows, cols]
            scores = _dot_nt(q, k) * dmask
            inner = _dot(scores.astype(BF16), v)
            prev = state_ref[h]
            cross = _dot(q, prev.astype(BF16)) * xi
            kz = (k.astype(F32) * zeta).astype(BF16)
            kv = lax.dot_general(kz, v, (((0,), (0,)), ((), ())), preferred_element_type=F32)
            state_ref[h] = prev * float(np.exp(np.float32(log_g) * np.float32(c))) + kv
            o = inner + cross
            mu = jnp.mean(o, axis=-1, keepdims=True)
            d = o - mu
            var = jnp.mean(d * d, axis=-1, keepdims=True)
            o = d * lax.rsqrt(var + LN_EPS)
            o_ref[rows, cols] = (jax.nn.silu(g_ref[rows, cols].astype(F32)) * o).astype(BF16)


def _retention(rq, rk, rv, rg, batch, seq):
    per_step = RET_STEP_CHUNKS if (seq // RET_CHUNK) % RET_STEP_CHUNKS == 0 else 1
    nc = seq // (RET_CHUNK * per_step)
    spec = pl.BlockSpec((RET_CHUNK * per_step, RET_WIDTH), lambda b, n: (b * nc + n, 0))
    return pl.pallas_call(
        _retention_kernel,
        grid=(batch, nc),
        in_specs=[spec] * 4,
        out_specs=spec,
        out_shape=jax.ShapeDtypeStruct(rq.shape, BF16),
        scratch_shapes=[pltpu.VMEM((RET_HEADS, RET_DIM, RET_DIM), F32)],
        compiler_params=_params(2),
        name="retention",
    )(rq, rk, rv, rg)


def _compress_kernel(a_ref, pe_ref, w1_ref, w2_ref, o_ref, shift_ref, *, n_cmp):
    rows = a_ref.shape[0]
    a = a_ref[...]
    lo = (a + pe_ref[0]).astype(BF16)
    hi = (a + pe_ref[1]).astype(BF16)
    ridx = lax.broadcasted_iota(jnp.int32, (rows, 1), 0)
    shift_ref[rows:rows + 8, :] = jnp.zeros((8, CMP_HIDDEN), F32)
    for g in range(NSA_KV_GROUPS):
        p = _dot(lo, w1_ref[0, g])
        shift_ref[0:rows, :] = _dot(hi, w1_ref[1, g])
        hid = jax.nn.silu(p + shift_ref[pl.ds(1, rows), :])
        out = _dot(hid.astype(BF16), w2_ref[...])
        o_ref[g] = jnp.where(ridx < n_cmp, out, 0.0).astype(BF16)


def _compress(a, pe, w1, w2, batch, seq):
    rows = seq // CMP_STRIDE
    per = CMP_STRIDE * KV_WIDTH
    n_cmp = (seq - CMP_LEN) // CMP_STRIDE + 1
    a2 = a.reshape(batch * rows, per)
    pe2 = jnp.tile(pe.reshape(2, CMP_STRIDE, 1, NSA_DIM), (1, 1, NSA_KV_GROUPS, 1)).reshape(2, 1, per)
    w1r = w1.reshape(2, CMP_STRIDE, 1, NSA_DIM, CMP_HIDDEN)
    eye = jnp.eye(NSA_KV_GROUPS, dtype=w1.dtype).reshape(1, NSA_KV_GROUPS, 1, NSA_KV_GROUPS, 1, 1)
    w1x = (w1r[:, None] * eye).reshape(2, NSA_KV_GROUPS, per, CMP_HIDDEN).astype(BF16)
    w2b = w2.astype(BF16)
    full = lambda arr: pl.BlockSpec(arr.shape, lambda b: (0,) * arr.ndim)
    return pl.pallas_call(
        functools.partial(_compress_kernel, n_cmp=n_cmp),
        grid=(batch,),
        in_specs=[pl.BlockSpec((rows, per), lambda b: (b, 0)), full(pe2), full(w1x), full(w2b)],
        out_specs=pl.BlockSpec((None, NSA_KV_GROUPS, rows, NSA_DIM), lambda b: (b, 0, 0, 0)),
        out_shape=jax.ShapeDtypeStruct((batch, NSA_KV_GROUPS, rows, NSA_DIM), BF16),
        scratch_shapes=[pltpu.VMEM((rows + 8, CMP_HIDDEN), F32)],
        compiler_params=_params(1),
        name="compress",
    )(a2, pe2, w1x, w2b)


def _heads_to_lanes(ref):
    vt = ref[...].astype(F32).T
    return jnp.concatenate([vt[h * NSA_DIM:(h + 1) * NSA_DIM] for h in range(NSA_HPG)], axis=1).astype(BF16)


def _tile_heads(v):
    return jnp.concatenate([v] * NSA_HPG, axis=1)


def _transpose_into(dst_ref, src_ref, chunk):
    def step(c, _):
        c0 = pl.multiple_of(c * chunk, chunk)
        dst_ref[:NSA_DIM, pl.ds(c0, chunk)] = src_ref[pl.ds(c0, chunk), :].astype(F32).T.astype(BF16)
        return 0
    lax.fori_loop(0, src_ref.shape[0] // chunk, step, 0)


def _nsa_kernel(qraw_ref, qrot_ref, gate_ref, kcmp_ref, vcmp_ref, ovt_ref,
                ks_ref, vs_ref, kw_ref, vw_ref, o_ref, vst_ref, vwt_ref, vct_ref, bias_ref, *, tq, tk, seq):
    i = pl.program_id(2)
    t0 = i * tq
    cols = NSA_HPG * tq
    n_sel = seq // SEL_LEN
    n_cmp_rows = seq // CMP_STRIDE
    blocks_per_tile = tk // SEL_LEN

    @pl.when(i == 0)
    def _():
        chunk = min(512, n_cmp_rows)
        _transpose_into(vst_ref, vs_ref, chunk)
        _transpose_into(vwt_ref, vw_ref, chunk)
        _transpose_into(vct_ref, vcmp_ref, chunk)
        vst_ref[NSA_DIM:, :] = jnp.ones((SUM_ROWS, seq), BF16)
        vwt_ref[NSA_DIM:, :] = jnp.ones((SUM_ROWS, seq), BF16)

    def split_sum(acc):
        return acc[:NSA_DIM] / acc[NSA_DIM:NSA_DIM + 1]

    q_raw = _heads_to_lanes(qraw_ref)
    q_rot = _heads_to_lanes(qrot_ref)
    t_row = t0 + lax.broadcasted_iota(jnp.int32, (1, tq), 1)

    chain_w = cols // NSA_CHAINS
    heads_per_chain = chain_w // tq
    chains = [slice(c * chain_w, (c + 1) * chain_w) for c in range(NSA_CHAINS)]
    tile_chain = lambda v: jnp.concatenate([v] * heads_per_chain, axis=1)

    pw = min(tq, WIN_PART)
    parts = []
    for u in range(tq // pw):
        span = WIN + pw
        ws = pl.multiple_of(jnp.maximum(t0 + u * pw - WIN, 0), pw)
        dist = t_row[:, u * pw:(u + 1) * pw] - (ws + lax.broadcasted_iota(jnp.int32, (span, 1), 0))
        bias_w = jnp.concatenate([jnp.where((dist >= 0) & (dist < WIN), 0.0, NEG)] * NSA_HPG, axis=1)
        q_part = jnp.concatenate([q_rot[:, h * tq + u * pw:h * tq + (u + 1) * pw] for h in range(NSA_HPG)],
                                 axis=1)
        s_w = _dot(kw_ref[pl.ds(ws, span), :], q_part) + bias_w
        p_w = jnp.exp((s_w - jnp.max(s_w, axis=0, keepdims=True)).astype(BF16))
        parts.append(split_sum(_dot(vwt_ref[:, pl.ds(ws, span)], p_w)))
    o_w = jnp.concatenate([parts[u][:, h * pw:(h + 1) * pw]
                           for h in range(NSA_HPG) for u in range(tq // pw)], axis=1)

    c_idx = lax.broadcasted_iota(jnp.int32, (n_cmp_rows, 1), 0)
    valid = tile_chain(jnp.where(c_idx * CMP_STRIDE + (CMP_LEN - 1) <= t_row, 1.0, 0.0))
    bias_c = (valid - 1.0) * (-NEG)
    o_c = []
    p_sum = None
    for c in chains:
        s_c = _dot(kcmp_ref[...], q_raw[:, c]) + bias_c
        e_c = jnp.exp(s_c - jnp.max(s_c, axis=0, keepdims=True)) * valid
        l_c = jnp.sum(e_c, axis=0, keepdims=True)
        p_c = e_c / jnp.where(l_c > 0.0, l_c, 1.0)
        o_c.append(_dot(vct_ref[...], p_c.astype(BF16)))
        for h in range(heads_per_chain):
            p_h = p_c[:, h * tq:(h + 1) * tq]
            p_sum = p_h if p_sum is None else p_sum + p_h
    o_c = jnp.concatenate(o_c, axis=1)

    p_hi = p_sum.astype(BF16)
    p_lo = (p_sum - p_hi.astype(F32)).astype(BF16)
    ovt = ovt_ref[...]
    imp = _dot(ovt, p_hi) + _dot(ovt, p_lo)
    jb = lax.broadcasted_iota(jnp.int32, (n_sel, tq), 0)
    cur = (t0 + lax.broadcasted_iota(jnp.int32, (n_sel, tq), 1)) >> SEL_SHIFT
    forced = (jb == 0) | (jb == cur) | (jb == cur - 1)
    work = jnp.where(forced, -jnp.inf, imp)
    work = jnp.where(jb <= cur, work, NEG)
    sel_t = jnp.where(forced, 1.0, 0.0)
    for _ in range(max(min(SEL_TOPK, n_sel) - N_FORCED, 0)):
        best = jnp.max(work, axis=0, keepdims=True)
        first = jnp.min(jnp.where(work == best, jb, n_sel), axis=0, keepdims=True)
        hit = jb == first
        sel_t = jnp.where(hit, 1.0, sel_t)
        work = jnp.where(hit, -jnp.inf, work)
    bias_ref[...] = jnp.where(sel_t > 0.5, 0.0, NEG)

    def sel_tile(kt, carry, causal):
        k0 = pl.multiple_of(kt * tk, tk)
        bias = jnp.concatenate(
            [jnp.broadcast_to(bias_ref[pl.ds(kt * blocks_per_tile + j, 1), :], (SEL_LEN, tq))
             for j in range(blocks_per_tile)], axis=0)
        if causal:
            kpos = k0 + lax.broadcasted_iota(jnp.int32, (tk, 1), 0)
            bias = jnp.where(kpos <= t_row, bias, NEG)
        bias = tile_chain(bias)
        k_t = ks_ref[pl.ds(k0, tk), :]
        v_t = vst_ref[:, pl.ds(k0, tk)]
        out = []
        scores = [_dot(k_t, q_rot[:, c]) + bias for c in chains]
        for (m, acc), s in zip(carry, scores):
            m_new = jnp.maximum(m, jnp.max(s, axis=0, keepdims=True))
            p = jnp.exp((s - m_new).astype(BF16))
            acc = jnp.exp(m - m_new) * acc + _dot(v_t, p)
            out.append((m_new, acc))
        return tuple(out)

    n_full = t0 // tk
    init = tuple((jnp.full((1, chain_w), NEG, F32), jnp.zeros((NSA_DIM + SUM_ROWS, chain_w), F32))
                 for _ in chains)
    carry = lax.fori_loop(0, n_full, functools.partial(sel_tile, causal=False), init)
    for d in range(max(tq // tk, 1)):
        carry = sel_tile(n_full + d, carry, causal=True)
    o_s = jnp.concatenate([split_sum(acc) for _, acc in carry], axis=1)

    gt = gate_ref[...]
    outs = []
    for h in range(NSA_HPG):
        c = slice(h * tq, (h + 1) * tq)
        outs.append(gt[3 * h:3 * h + 1] * o_c[:, c] + gt[3 * h + 1:3 * h + 2] * o_s[:, c]
                    + gt[3 * h + 2:3 * h + 3] * o_w[:, c])
    o_ref[...] = jnp.concatenate(outs, axis=0).T.astype(BF16)


def _anchored(kernel_fn, n_inputs, n_anchors):
    def body(*refs, **static):
        kernel_fn(*refs[:n_inputs], *refs[n_inputs + n_anchors:], **static)
    return body


def _add_anchors(kernel_fn, args, in_specs, after):
    after = [a for a in (after or ()) if a is not None]
    if not after:
        return kernel_fn
    body = _anchored(kernel_fn, len(args), len(after))
    args.extend(after)
    in_specs.extend([pl.BlockSpec(memory_space=pl.ANY)] * len(after))
    return body


def _nsa(nq, nqr, gates, kcmp, vcmp, ks, vs, kw, vw, batch, seq, after=None):
    n = batch * seq
    tq = 512
    tk = 512 if seq % 512 == 0 else seq
    nqb = seq // tq
    n_sel = seq // SEL_LEN
    rows_c = seq // CMP_STRIDE
    gw = NSA_HPG * NSA_DIM
    cs = np.arange(rows_c)[None, :] * CMP_STRIDE
    ss = np.arange(n_sel)[:, None] * SEL_LEN
    n_cmp = (seq - CMP_LEN) // CMP_STRIDE + 1
    ovt = ((cs < ss + SEL_LEN) & (cs + CMP_LEN > ss) & (np.arange(rows_c)[None, :] < n_cmp))
    ovt = jnp.asarray(ovt.astype(np.float32), BF16)

    qspec = pl.BlockSpec((tq, gw), lambda b, g, i: (b * nqb + i, g))
    cspec = pl.BlockSpec((None, None, rows_c, NSA_DIM), lambda b, g, i: (b, g, 0, 0))
    kspec = pl.BlockSpec((None, seq, NSA_DIM), lambda b, g, i: (g, b, 0))
    args = [nq, nqr, gates, kcmp, vcmp, ovt, ks, vs, kw, vw]
    in_specs = [qspec, qspec,
                pl.BlockSpec((None, GATE_LANES, tq), lambda b, g, i: (g, 0, b * nqb + i)),
                cspec, cspec, pl.BlockSpec(ovt.shape, lambda b, g, i: (0, 0)),
                kspec, kspec, kspec, kspec]
    body = functools.partial(_add_anchors(_nsa_kernel, args, in_specs, after), tq=tq, tk=tk, seq=seq)
    return pl.pallas_call(
        body,
        grid=(batch, NSA_KV_GROUPS, nqb),
        in_specs=in_specs,
        out_specs=qspec,
        out_shape=jax.ShapeDtypeStruct((n, NSA_WIDTH), BF16),
        scratch_shapes=[pltpu.VMEM((NSA_DIM + SUM_ROWS, seq), BF16), pltpu.VMEM((NSA_DIM + SUM_ROWS, seq), BF16),
                        pltpu.VMEM((NSA_DIM, rows_c), BF16), pltpu.VMEM((n_sel, tq), F32)],
        compiler_params=_params(3),
        name="nsa",
    )(*args)


def _memkv_kernel(mem_ref, w_ref, kv_ref):
    kv_ref[...] = _dot(mem_ref[...].astype(BF16), w_ref[...]).astype(BF16)


def _memkv(mem2d, w_xkv):
    n = mem2d.shape[0]
    w = w_xkv.astype(BF16)
    return pl.pallas_call(
        _memkv_kernel,
        grid=(n // MEM_LEN,),
        in_specs=[pl.BlockSpec((MEM_LEN, D_MODEL), lambda i: (i, 0)),
                  pl.BlockSpec(w.shape, lambda i: (0, 0))],
        out_specs=pl.BlockSpec((MEM_LEN, 2 * D_MODEL), lambda i: (i, 0)),
        out_shape=jax.ShapeDtypeStruct((n, 2 * D_MODEL), BF16),
        compiler_params=_params(1),
        name="memkv",
    )(mem2d, w)


def _pack_halves(v):
    half = D_MODEL // 2
    hi = pltpu.bitcast(v[:, :half].astype(BF16).astype(F32), jnp.uint32)
    lo = pltpu.bitcast(v[:, half:].astype(BF16).astype(F32), jnp.uint32)
    return hi | (lo >> 16)


def _unpack_halves(words):
    return pltpu.bitcast(words & jnp.uint32(0xFFFF0000), F32), pltpu.bitcast(words << 16, F32)


def _postmix_kernel(x_ref, oret_ref, onsa_ref, kv_ref, wout_ref, wq_ref, wo_ref,
                    g1_ref, b1_ref, g2_ref, b2_ref, x2_ref, x2p_ref):
    mixed = jnp.concatenate([oret_ref[...], onsa_ref[...]], axis=1)
    x1 = _layer_norm(DN_ALPHA * x_ref[...] + _dot(mixed, wout_ref[...]), g1_ref[...], b1_ref[...])
    q = (_dot(x1.astype(BF16), wq_ref[...]) * (XATT_DIM ** -0.5)).astype(BF16)
    heads = []
    for h in range(XATT_HEADS):
        cols = slice(h * XATT_DIM, (h + 1) * XATT_DIM)
        s = _dot_nt(q[:, cols], kv_ref[:, cols])
        m = jnp.max(s, axis=-1, keepdims=True)
        p = jnp.exp(s - m)
        l = jnp.sum(p, axis=-1, keepdims=True)
        heads.append(_dot(p.astype(BF16), kv_ref[:, D_MODEL + h * XATT_DIM:D_MODEL + (h + 1) * XATT_DIM]) / l)
    att = jnp.concatenate(heads, axis=1).astype(BF16)
    x2 = _layer_norm(DN_ALPHA * x1 + _dot(att, wo_ref[...]), g2_ref[...], b2_ref[...])
    x2_ref[...] = x2
    x2p_ref[...] = _pack_halves(x2)


def _postmix(x2d, o_ret, o_nsa, kvx, w_out, w_xq, w_xo, ln1_g, ln1_b, ln2_g, ln2_b, batch0, batch, seq,
             after=None):
    n = batch * seq
    tm = 512 if seq % 512 == 0 else seq
    per_b = seq // tm
    row = lambda w: pl.BlockSpec((tm, w), lambda b, i: (b * per_b + i, 0))
    full = lambda a: pl.BlockSpec(a.shape, lambda b, i: (0,) * a.ndim)
    ws = [w_out.astype(BF16), w_xq.astype(BF16), w_xo.astype(BF16)]
    vecs = [v.reshape(1, D_MODEL) for v in (ln1_g, ln1_b, ln2_g, ln2_b)]
    args = [x2d, o_ret, o_nsa, kvx, *ws, *vecs]
    in_specs = ([pl.BlockSpec((tm, D_MODEL), lambda b, i: ((batch0 + b) * per_b + i, 0)),
                 row(RET_WIDTH), row(NSA_WIDTH),
                 pl.BlockSpec((MEM_LEN, 2 * D_MODEL), lambda b, i: (batch0 + b, 0))]
                + [full(w) for w in ws] + [full(v) for v in vecs])
    return pl.pallas_call(
        _add_anchors(_postmix_kernel, args, in_specs, after),
        grid=(batch, per_b),
        in_specs=in_specs,
        out_specs=[row(D_MODEL),
                   row(D_MODEL // 2)],
        out_shape=[jax.ShapeDtypeStruct((n, D_MODEL), F32),
                   jax.ShapeDtypeStruct((n, D_MODEL // 2), jnp.uint32)],
        compiler_params=_params(2),
        name="postmix",
    )(*args)


def _router_kernel(x_ref, wr_ref, bias_ref, e_ref, rank_ref, w_ref, cnt_ref, cntrow_ref, carry_ref, carryrow_ref):
    tn = x_ref.shape[0]
    per = N_EXPERTS // N_GROUPS

    @pl.when(pl.program_id(0) == 0)
    def _():
        carry_ref[...] = jnp.zeros_like(carry_ref)
        carryrow_ref[...] = jnp.zeros_like(carryrow_ref)

    logits = _dot_nt(wr_ref[...], x_ref[...].astype(BF16))
    scores = jax.nn.sigmoid(logits)
    biased = scores + bias_ref[...]
    b3 = biased.reshape(N_GROUPS, per, tn)
    member = lax.broadcasted_iota(jnp.int32, (N_GROUPS, per, tn), 1)
    top1 = jnp.max(b3, axis=1, keepdims=True)
    first1 = jnp.min(jnp.where(b3 == top1, member, per), axis=1, keepdims=True)
    top2 = jnp.max(jnp.where(member == first1, -jnp.inf, b3), axis=1, keepdims=True)
    gscore = top1 + top2
    gidx = lax.broadcasted_iota(jnp.int32, (N_GROUPS, 1, tn), 0)
    gwork = gscore
    for _ in range(TOPK_GROUPS - 1):
        gbest = jnp.max(gwork, axis=0, keepdims=True)
        gfirst = jnp.min(jnp.where(gwork == gbest, gidx, N_GROUPS), axis=0, keepdims=True)
        gwork = jnp.where(gidx == gfirst, -jnp.inf, gwork)
    kth = jnp.max(gwork, axis=0, keepdims=True)
    work = jnp.where(gscore >= kth, b3, NEG).reshape(N_EXPERTS, tn)
    eidx = lax.broadcasted_iota(jnp.int32, (N_EXPERTS, tn), 0)
    picks = []
    chosen = jnp.zeros((N_EXPERTS, tn), F32)
    for _ in range(TOP_K):
        best = jnp.max(work, axis=0, keepdims=True)
        first = jnp.min(jnp.where(work == best, eidx, N_EXPERTS), axis=0, keepdims=True)
        hit = eidx == first
        picks.append((first, hit))
        chosen = jnp.where(hit, 1.0, chosen)
        work = jnp.where(hit, -jnp.inf, work)

    r_i = lax.broadcasted_iota(jnp.int32, (tn, tn), 0)
    c_i = lax.broadcasted_iota(jnp.int32, (tn, tn), 1)
    before = jnp.where(r_i < c_i, 1.0, 0.0).astype(BF16)
    chosen_b = chosen.astype(BF16)
    rank = _dot(chosen_b, before) + carry_ref[...]
    carry_ref[...] = carry_ref[...] + jnp.sum(chosen, axis=1, keepdims=True)
    carryrow_ref[...] = carryrow_ref[...] + _dot_nt(jnp.ones((8, tn), BF16), chosen_b)
    cnt_ref[...] = carry_ref[...]
    cntrow_ref[...] = carryrow_ref[...]

    wsel = [jnp.sum(jnp.where(hit, scores, 0.0), axis=0, keepdims=True) for _, hit in picks]
    wsum = wsel[0]
    for v in wsel[1:]:
        wsum = wsum + v
    for kk, (first, hit) in enumerate(picks):
        e_ref[kk:kk + 1, :] = first
        rank_ref[kk:kk + 1, :] = jnp.sum(jnp.where(hit, rank, 0.0), axis=0, keepdims=True).astype(jnp.int32)
        w_ref[kk:kk + 1, :] = wsel[kk] / wsum * ROUTED_SCALE


def _router(x2, w_router, router_bias):
    n = x2.shape[0]
    tn = 512 if n % 512 == 0 else n
    wr_t = w_router.T.astype(BF16)
    bias = router_bias.reshape(N_EXPERTS, 1).astype(F32)
    kspec = pl.BlockSpec((TOP_K, tn), lambda i: (0, i))
    return pl.pallas_call(
        _router_kernel,
        grid=(n // tn,),
        in_specs=[pl.BlockSpec((tn, D_MODEL), lambda i: (i, 0)),
                  pl.BlockSpec(wr_t.shape, lambda i: (0, 0)),
                  pl.BlockSpec(bias.shape, lambda i: (0, 0))],
        out_specs=[kspec, kspec, kspec, pl.BlockSpec((N_EXPERTS, 1), lambda i: (0, 0)),
                   pl.BlockSpec((8, N_EXPERTS), lambda i: (0, 0))],
        out_shape=[jax.ShapeDtypeStruct((TOP_K, n), jnp.int32),
                   jax.ShapeDtypeStruct((TOP_K, n), jnp.int32),
                   jax.ShapeDtypeStruct((TOP_K, n), F32),
                   jax.ShapeDtypeStruct((N_EXPERTS, 1), F32),
                   jax.ShapeDtypeStruct((8, N_EXPERTS), F32)],
        scratch_shapes=[pltpu.VMEM((N_EXPERTS, 1), F32), pltpu.VMEM((8, N_EXPERTS), F32)],
        compiler_params=_params(1),
        name="router",
    )(x2, wr_t, bias)


def _slots_kernel(e_ref, rank_ref, cnt_ref, cntrow_ref, dest_ref, blk_e_ref, valid_ref, used_ref,
                  *, blk, n_blocks):
    pad = lambda c: jnp.ceil(c / blk) * blk
    cnt = cnt_ref[...]
    padded = pad(cnt)
    padded_row = pad(cntrow_ref[0:1, :])
    r_i = lax.broadcasted_iota(jnp.int32, (N_EXPERTS, N_EXPERTS), 0)
    c_i = lax.broadcasted_iota(jnp.int32, (N_EXPERTS, N_EXPERTS), 1)
    start = jnp.sum(jnp.where(c_i < r_i, padded_row, 0.0), axis=1, keepdims=True)
    end = start + padded
    e = e_ref[...]
    dest = rank_ref[...]
    for ex in range(N_EXPERTS):
        dest = dest + jnp.where(e == ex, start[ex:ex + 1, :].astype(jnp.int32), 0)
    dest_ref[...] = dest
    bstart = (lax.broadcasted_iota(jnp.int32, (1, n_blocks), 1) * blk).astype(F32)
    owner = jnp.sum(jnp.where(end <= bstart, 1.0, 0.0), axis=0, keepdims=True)
    blk_e_ref[...] = jnp.minimum(owner, N_EXPERTS - 1.0).astype(jnp.int32)
    inside = (start <= bstart) & (bstart < end)
    real = jnp.clip(start + cnt - bstart, 0.0, float(blk))
    valid = jnp.sum(jnp.where(inside, real, 0.0), axis=0, keepdims=True)
    valid_ref[...] = valid.astype(jnp.int32)
    used_ref[...] = jnp.sum(jnp.where(valid > 0.0, 1.0, 0.0), axis=1, keepdims=True).astype(jnp.int32)


def _slots(e_k, rank_k, counts, counts_row, blk, n_blocks):
    n = e_k.shape[1]
    full = lambda shape: pl.BlockSpec(shape, lambda: (0,) * len(shape))
    return pl.pallas_call(
        functools.partial(_slots_kernel, blk=blk, n_blocks=n_blocks),
        in_specs=[full((TOP_K, n)), full((TOP_K, n)), full((N_EXPERTS, 1)), full((8, N_EXPERTS))],
        out_specs=[full((TOP_K, n)), full((1, n_blocks)), full((1, n_blocks)), full((1, 1))],
        out_shape=[jax.ShapeDtypeStruct((TOP_K, n), jnp.int32),
                   jax.ShapeDtypeStruct((1, n_blocks), jnp.int32),
                   jax.ShapeDtypeStruct((1, n_blocks), jnp.int32),
                   jax.ShapeDtypeStruct((1, 1), jnp.int32)],
        compiler_params=pltpu.CompilerParams(vmem_limit_bytes=VMEM_LIMIT),
        name="slots",
    )(e_k, rank_k, counts, counts_row)


def _sc_worker_base(per_worker):
    return (lax.axis_index("s") * SC_CORES + lax.axis_index("c")) * per_worker


def _sc_scatter_rows(rows, idx, n_out):
    n, width = rows.shape
    k_lists = idx.shape[0] // n
    workers = SC_CORES * SC_SUBCORES
    per_worker = n // workers
    assert per_worker * workers == n and per_worker % (SC_CHUNK * SC_INFLIGHT) == 0
    mesh = plsc.VectorSubcoreMesh(core_axis_name="c", subcore_axis_name="s")
    lanes = range(SC_INFLIGHT)

    @functools.partial(
        pl.kernel, mesh=mesh,
        out_type=jax.ShapeDtypeStruct((n_out, width), rows.dtype),
        scratch_types=[pltpu.VMEM((SC_CHUNK, width), rows.dtype)] * SC_INFLIGHT
                      + [pltpu.VMEM((SC_CHUNK,), jnp.int32)] * (k_lists * SC_INFLIGHT)
                      + [pltpu.SemaphoreType.DMA] * (2 * SC_INFLIGHT),
        name="sc_scatter")
    def scatter(rows_hbm, idx_hbm, out_hbm, *scratch):
        rows_vs = scratch[:SC_INFLIGHT]
        idx_vs = [scratch[SC_INFLIGHT + j * k_lists:SC_INFLIGHT + (j + 1) * k_lists] for j in lanes]
        sems = scratch[SC_INFLIGHT * (1 + k_lists):]
        sem_in, sem_out = sems[:SC_INFLIGHT], sems[SC_INFLIGHT:]
        base = _sc_worker_base(per_worker)

        @pl.loop(0, per_worker // (SC_CHUNK * SC_INFLIGHT))
        def _(gi):
            offs = [pl.multiple_of(base + (gi * SC_INFLIGHT + j) * SC_CHUNK, SC_CHUNK) for j in lanes]
            loads = []
            for j in lanes:
                loads.append([pltpu.async_copy(rows_hbm.at[pl.ds(offs[j], SC_CHUNK)], rows_vs[j], sem_in[j])]
                             + [pltpu.async_copy(
                                 idx_hbm.at[pl.ds(pl.multiple_of(k * n + offs[j], SC_CHUNK), SC_CHUNK)],
                                 idx_vs[j][k], sem_in[j]) for k in range(k_lists)])
            copies = []
            for j in lanes:
                for c in loads[j]:
                    c.wait()
                copies.append([pltpu.async_copy(rows_vs[j], out_hbm.at[idx_vs[j][k]], sem_out[j])
                               for k in range(k_lists)])
            for j in lanes:
                for c in copies[j]:
                    c.wait()

    return scatter(rows, idx)


def _experts_kernel(blk_e_ref, valid_ref, used_ref, xs_ref, wg_ref, wu_ref, wd_ref, y_ref, wg_b, wu_b, wd_b):
    del used_ref
    i = pl.program_id(0)
    valid = valid_ref[i]

    @pl.when((i == 0) | (blk_e_ref[i] != blk_e_ref[jnp.maximum(i - 1, 0)]))
    def _():
        wg_b[...] = wg_ref[...].astype(BF16)
        wu_b[...] = wu_ref[...].astype(BF16)
        wd_b[...] = wd_ref[...].astype(BF16)

    @pl.when(valid > 0)
    def _():
        half = D_MODEL // 2
        row = lax.broadcasted_iota(jnp.int32, (xs_ref.shape[0], 1), 0)
        hi, lo = (v.astype(BF16) for v in _unpack_halves(jnp.where(row < valid, xs_ref[...], jnp.uint32(0))))
        gate = _dot(hi, wg_b[:half, :]) + _dot(lo, wg_b[half:, :])
        up = _dot(hi, wu_b[:half, :]) + _dot(lo, wu_b[half:, :])
        y_ref[...] = _pack_halves(_dot((jax.nn.silu(gate) * up).astype(BF16), wd_b[...]))

    @pl.when(valid <= 0)
    def _():
        y_ref[...] = jnp.zeros_like(y_ref)


def _experts(blk_e, valid, n_used, xs, w_gate, w_up, w_down, blk, after=None):
    cap, width = xs.shape
    wspec = lambda a: pl.BlockSpec((None,) + a.shape[1:], lambda i, be, nv, nu: (be[i], 0, 0))
    rows = pl.BlockSpec((blk, width), lambda i, be, nv, nu: (jnp.minimum(i, nu[0]), 0))
    args = [blk_e, valid, n_used, xs, w_gate, w_up, w_down]
    in_specs = [rows, wspec(w_gate), wspec(w_up), wspec(w_down)]
    body = _add_anchors(_experts_kernel, args, in_specs, after)
    return pl.pallas_call(
        body,
        grid_spec=pltpu.PrefetchScalarGridSpec(
            num_scalar_prefetch=3,
            grid=(cap // blk,),
            in_specs=in_specs,
            out_specs=rows,
            scratch_shapes=[pltpu.VMEM(w.shape[1:], BF16) for w in (w_gate, w_up, w_down)],
        ),
        out_shape=jax.ShapeDtypeStruct(xs.shape, xs.dtype),
        compiler_params=_params(1),
        name="experts",
    )(*args)


def _sc_gather_rows(table, idx):
    b, width = idx.shape[0], table.shape[1]
    workers = SC_CORES * SC_SUBCORES
    per_worker = b // workers
    assert per_worker * workers == b and per_worker % (SC_CHUNK * SC_INFLIGHT) == 0
    mesh = plsc.VectorSubcoreMesh(core_axis_name="c", subcore_axis_name="s")

    @functools.partial(
        pl.kernel, mesh=mesh,
        out_type=jax.ShapeDtypeStruct((b, width), table.dtype),
        scratch_types=[pltpu.VMEM((SC_CHUNK,), jnp.int32)] * SC_INFLIGHT
                      + [pltpu.VMEM((SC_CHUNK, width), table.dtype)] * SC_INFLIGHT
                      + [pltpu.SemaphoreType.DMA] * (1 + 2 * SC_INFLIGHT),
        name="sc_gather")
    def gather(table_hbm, idx_hbm, out_hbm, *scratch):
        idx_vs = scratch[:SC_INFLIGHT]
        rows_vs = scratch[SC_INFLIGHT:2 * SC_INFLIGHT]
        sem_idx = scratch[2 * SC_INFLIGHT]
        sem_rows = scratch[2 * SC_INFLIGHT + 1:3 * SC_INFLIGHT + 1]
        sem_out = scratch[3 * SC_INFLIGHT + 1:]
        base = _sc_worker_base(per_worker)
        lanes = range(SC_INFLIGHT)

        @pl.loop(0, per_worker // (SC_CHUNK * SC_INFLIGHT))
        def _(gi):
            offs = [pl.multiple_of(base + (gi * SC_INFLIGHT + j) * SC_CHUNK, SC_CHUNK) for j in lanes]
            loads = [pltpu.async_copy(idx_hbm.at[pl.ds(offs[j], SC_CHUNK)], idx_vs[j], sem_idx) for j in lanes]
            for c in loads:
                c.wait()
            gathers = [pltpu.async_copy(table_hbm.at[idx_vs[j]], rows_vs[j], sem_rows[j]) for j in lanes]
            writes = []
            for j in lanes:
                gathers[j].wait()
                writes.append(pltpu.async_copy(rows_vs[j], out_hbm.at[pl.ds(offs[j], SC_CHUNK)], sem_out[j]))
            for c in writes:
                c.wait()

    return gather(table, idx)


def _combine_kernel(x_ref, wk_ref, yk_ref, wsg_ref, wsu_ref, wsd_ref, g_ref, b_ref, *rest):
    o_ref = rest[-1]
    x = x_ref[...]
    xb = x.astype(BF16)
    shared = _dot((jax.nn.silu(_dot(xb, wsg_ref[...])) * _dot(xb, wsu_ref[...])).astype(BF16), wsd_ref[...])
    wk = wk_ref[...]
    routed_hi = routed_lo = None
    for kk in range(TOP_K):
        hi, lo = _unpack_halves(yk_ref[kk])
        w = wk[:, kk:kk + 1]
        routed_hi = hi * w if kk == 0 else routed_hi + hi * w
        routed_lo = lo * w if kk == 0 else routed_lo + lo * w
    routed = jnp.concatenate([routed_hi, routed_lo], axis=1)
    o_ref[...] = _layer_norm(DN_ALPHA * x + (routed + shared), g_ref[...], b_ref[...])


def _combine(x2, w_tok, yk, ws_gate, ws_up, ws_down, ln3_g, ln3_b, row0, n_total, out_prev, after=None,
             part=(0, 1)):
    n = x2.shape[0] // part[1]
    sub0 = part[0] * n
    tt = 512 if n % 512 == 0 and row0 % 512 == 0 else n
    blk0 = (row0 + sub0) // tt
    sub_blk = sub0 // tt
    ws = [ws_gate.astype(BF16), ws_up.astype(BF16), ws_down.astype(BF16)]
    vecs = [ln3_g.reshape(1, D_MODEL), ln3_b.reshape(1, D_MODEL)]
    full = lambda a: pl.BlockSpec(a.shape, lambda i: (0,) * a.ndim)
    args = [x2, w_tok, yk, *ws, *vecs]
    in_specs = ([pl.BlockSpec((tt, D_MODEL), lambda i: (sub_blk + i, 0)),
                 pl.BlockSpec((tt, TOP_K), lambda i: (sub_blk + i, 0)),
                 pl.BlockSpec((TOP_K, tt, D_MODEL // 2), lambda i: (0, sub_blk + i, 0))]
                + [full(a) for a in ws] + [full(v) for v in vecs])
    aliases = {}
    for anchor in (a for a in (after or ()) if a is not None):
        args.append(anchor)
        in_specs.append(pl.BlockSpec(memory_space=pl.ANY))
    if out_prev is not None:
        aliases = {len(args): 0}
        args.append(out_prev)
        in_specs.append(pl.BlockSpec(memory_space=pl.ANY))
    return pl.pallas_call(
        _combine_kernel,
        grid=(n // tt,),
        in_specs=in_specs,
        out_specs=pl.BlockSpec((tt, D_MODEL), lambda i: (blk0 + i, 0)),
        out_shape=jax.ShapeDtypeStruct((n_total, D_MODEL), F32),
        input_output_aliases=aliases,
        compiler_params=_params(1),
        name="combine",
    )(*args)


EXPERT_BLOCK = 1024


def _moe_dispatch(x2, x2p, w_router, router_bias):
    n = x2.shape[0]
    cap = n * TOP_K + N_EXPERTS * EXPERT_BLOCK
    e_k, rank_k, w_k, counts, counts_row = _router(x2, w_router, router_bias)
    dest, blk_e, valid, n_used = _slots(e_k, rank_k, counts, counts_row, EXPERT_BLOCK, cap // EXPERT_BLOCK)
    dest = dest.reshape(-1)
    return dict(w_tok=w_k.T, dest=dest, blk_e=blk_e.reshape(-1), valid=valid.reshape(-1),
                n_used=n_used.reshape(-1), xs=_sc_scatter_rows(x2p, dest, cap))


def _moe_experts(routed, w_gate, w_up, w_down, after):
    y = _experts(routed["blk_e"], routed["valid"], routed["n_used"], routed["xs"], w_gate, w_up, w_down,
                 EXPERT_BLOCK, after=after)
    n = routed["dest"].shape[0] // TOP_K
    return y, _sc_gather_rows(y, routed["dest"]).reshape(TOP_K, n, D_MODEL // 2)


def _layer(x, mem, positions, w_in, cmp_pe_k, cmp_pe_v, cmp_w1_k, cmp_w2_k, cmp_w1_v, cmp_w2_v,
           w_out, ln1_g, ln1_b, w_xq, w_xkv, w_xo, ln2_g, ln2_b, w_router, router_bias,
           w_gate, w_up, w_down, ws_gate, ws_up, ws_down, ln3_g, ln3_b):
    batch, seq, _ = x.shape
    n_total = batch * seq
    x2d = x.reshape(n_total, D_MODEL)
    pos_col = positions.astype(F32).reshape(n_total, 1)
    kvx = _memkv(mem.reshape(batch * MEM_LEN, D_MODEL), w_xkv)
    last = max(1, batch // 4)
    sizes = [batch - last, last] if batch > 1 else [batch]
    starts = [sum(sizes[:g]) for g in range(len(sizes))]

    def mixers_in(g):
        nb, row0 = sizes[g], starts[g] * seq
        (rq, rk, rv, rg, nq, nqr, kc, vc, ks, vs, kw, vw, gates) = _inproj(x2d, pos_col, w_in, row0, nb * seq)
        o_ret = _retention(rq, rk, rv, rg, nb, seq)
        kcmp = _compress(kc, cmp_pe_k, cmp_w1_k, cmp_w2_k, nb, seq)
        vcmp = _compress(vc, cmp_pe_v, cmp_w1_v, cmp_w2_v, nb, seq)
        return o_ret, (nq, nqr, gates, kcmp, vcmp, ks, vs, kw, vw)

    def attend(g, nsa_args, after):
        return _nsa(*nsa_args, sizes[g], seq, after=after)

    def mix_and_route(g, o_ret, o_nsa, after):
        x2, x2p = _postmix(x2d, o_ret, o_nsa, kvx, w_out, w_xq, w_xo, ln1_g, ln1_b, ln2_g, ln2_b,
                           starts[g], sizes[g], seq, after=after)
        return x2, _moe_dispatch(x2, x2p, w_router, router_bias)

    def combine(g, x2, routed, yk, out_prev, after=None, part=(0, 1)):
        return _combine(x2, routed["w_tok"], yk, ws_gate, ws_up, ws_down, ln3_g, ln3_b, starts[g] * seq,
                        n_total, out_prev, after=after, part=part)

    o_ret, nsa_args = mixers_in(0)
    x2, routed = mix_and_route(0, o_ret, attend(0, nsa_args, None), None)
    if len(sizes) == 1:
        y, yk = _moe_experts(routed, w_gate, w_up, w_down, after=None)
        return combine(0, x2, routed, yk, None).reshape(batch, seq, D_MODEL)
    o_ret1, nsa_args1 = mixers_in(1)
    y, yk = _moe_experts(routed, w_gate, w_up, w_down, after=[o_ret1])
    o_nsa1 = attend(1, nsa_args1, [y])
    x2_1, routed1 = mix_and_route(1, o_ret1, o_nsa1, [yk])
    out = combine(0, x2, routed, yk, None, [routed1["dest"]], part=(0, 2))
    y1, yk1 = _moe_experts(routed1, w_gate, w_up, w_down, after=[out])
    out = combine(0, x2, routed, yk, out, [y1], part=(1, 2))
    out = combine(1, x2_1, routed1, yk1, out)
    return out.reshape(batch, seq, D_MODEL)


def kernel(x, mem, positions, w_in, cmp_pe_k, cmp_pe_v, cmp_w1_k, cmp_w2_k, cmp_w1_v, cmp_w2_v, w_out, ln1_g, ln1_b, w_xq, w_xkv, w_xo, ln2_g, ln2_b, w_router, router_bias, w_gate, w_up, w_down, ws_gate, ws_up, ws_down, ln3_g, ln3_b):
    for l in range(DEPTH):
        x = _layer(x, mem, positions, w_in[l], cmp_pe_k[l], cmp_pe_v[l], cmp_w1_k[l], cmp_w2_k[l],
                   cmp_w1_v[l], cmp_w2_v[l], w_out[l], ln1_g[l], ln1_b[l], w_xq[l], w_xkv[l],
                   w_xo[l], ln2_g[l], ln2_b[l], w_router[l], router_bias[l], w_gate[l], w_up[l],
                   w_down[l], ws_gate[l], ws_up[l], ws_down[l], ln3_g[l], ln3_b[l])
    return x
```

```python
import functools

import numpy as np
import jax
import jax.numpy as jnp
from jax import lax
from jax.experimental import pallas as pl
from jax.experimental.pallas import tpu as pltpu
from jax.experimental.pallas import tpu_sc as plsc

D_MODEL = 1024
MEM_LEN = 256
DEPTH = 1
DN_ALPHA = (2 * DEPTH) ** 0.25
LN_EPS = 1e-5
NEG = -1e30

RET_HEADS = 4
RET_DIM = 128
RET_CHUNK = 128
RET_ROPE_BASE = 10000.0
RET_STEP_CHUNKS = 8
RET_WIDTH = RET_HEADS * RET_DIM

NSA_HEADS = 8
NSA_KV_GROUPS = 2
NSA_HPG = NSA_HEADS // NSA_KV_GROUPS
NSA_DIM = 64
NSA_WIDTH = NSA_HEADS * NSA_DIM
KV_WIDTH = NSA_KV_GROUPS * NSA_DIM
CMP_LEN = 32
CMP_STRIDE = 16
CMP_HIDDEN = 256
SEL_LEN = 64
SEL_SHIFT = 6
SEL_TOPK = 16
N_FORCED = 3
WIN = 512
ROPE_THETA = 500000.0
ROPE_DIMS = NSA_DIM // 4
GATE_LANES = 16
NSA_CHAINS = 1
SUM_ROWS = 16
WIN_PART = 256

SC_CORES = 2
SC_SUBCORES = 16
SC_CHUNK = 64
SC_INFLIGHT = 2

XATT_HEADS = 4
XATT_DIM = D_MODEL // XATT_HEADS

N_EXPERTS = 64
TOP_K = 8
N_GROUPS = 8
TOPK_GROUPS = 4
EXPERT_FF = 256
SHARED_FF = 256
ROUTED_SCALE = 2.5

LANES = 128
VMEM_LIMIT = 56 * 1024 * 1024

F32 = jnp.float32
BF16 = jnp.bfloat16
NT_DIMS = (((1,), (1,)), ((), ()))


def _params(n_axes):
    return pltpu.CompilerParams(dimension_semantics=("arbitrary",) * n_axes,
                                vmem_limit_bytes=VMEM_LIMIT)


def _dot(a, b):
    return jnp.dot(a, b, preferred_element_type=F32)


def _dot_nt(a, b):
    return lax.dot_general(a, b, NT_DIMS, preferred_element_type=F32)


def _layer_norm(v, g, b):
    mu = jnp.mean(v, axis=-1, keepdims=True)
    d = v - mu
    var = jnp.mean(d * d, axis=-1, keepdims=True)
    return d * lax.rsqrt(var + LN_EPS) * g + b


def _inproj_kernel(x_ref, pos_ref, wret_ref, wnq_ref, wkv_ref, wg_ref, invr_ref, invn_ref,
                   rq_ref, rk_ref, rv_ref, rg_ref, nq_ref, nqr_ref, kc_ref, vc_ref,
                   ks_ref, vs_ref, kw_ref, vw_ref, gate_ref):
    xb = x_ref[...].astype(BF16)
    pos = pos_ref[...]
    lane = lax.broadcasted_iota(jnp.int32, (1, LANES), 1)

    ang = pos * invr_ref[...]
    cos_r = jnp.cos(ang)
    sin_r = jnp.sin(ang)
    sin_r = jnp.where(lane < RET_DIM // 2, -sin_r, sin_r)
    q_all = _dot(xb, wret_ref[:, :RET_WIDTH])
    k_all = _dot(xb, wret_ref[:, RET_WIDTH:2 * RET_WIDTH])
    for h in range(RET_HEADS):
        cols = slice(h * RET_DIM, (h + 1) * RET_DIM)
        q = q_all[:, cols]
        rq_ref[:, cols] = (q * cos_r + pltpu.roll(q, RET_DIM // 2, 1) * sin_r).astype(BF16)
        k = k_all[:, cols]
        k = (k * cos_r + pltpu.roll(k, RET_DIM // 2, 1) * sin_r) * (RET_DIM ** -0.5)
        rk_ref[:, cols] = k.astype(BF16)
    rv_ref[...] = _dot(xb, wret_ref[:, 2 * RET_WIDTH:3 * RET_WIDTH]).astype(BF16)
    rg_ref[...] = _dot(xb, wret_ref[:, 3 * RET_WIDTH:4 * RET_WIDTH]).astype(BF16)

    half = ROPE_DIMS // 2
    j = lane % NSA_DIM
    angn = pos * invn_ref[...]
    cos_n = jnp.cos(angn)
    sin_n = jnp.sin(angn)
    sin_lo = jnp.where(j < half, -sin_n, 0.0)
    sin_hi = jnp.where((j >= half) & (j < 2 * half), sin_n, 0.0)

    def rope_n(v):
        return v * cos_n + pltpu.roll(v, half, 1) * sin_hi + pltpu.roll(v, LANES - half, 1) * sin_lo

    scale = NSA_DIM ** -0.5
    nq_all = _dot(xb, wnq_ref[...])
    for c in range(NSA_WIDTH // LANES):
        cols = slice(c * LANES, (c + 1) * LANES)
        q = nq_all[:, cols]
        nq_ref[:, cols] = (q * scale).astype(BF16)
        nqr_ref[:, cols] = (rope_n(q) * scale).astype(BF16)

    kv_all = _dot(xb, wkv_ref[...])

    def kv(i):
        return kv_all[:, i * KV_WIDTH:(i + 1) * KV_WIDTH]

    def split_groups(ref, v):
        for g in range(NSA_KV_GROUPS):
            ref[g] = v[:, g * NSA_DIM:(g + 1) * NSA_DIM].astype(BF16)

    kc_ref[...] = kv(0)
    vc_ref[...] = kv(1)
    split_groups(ks_ref, rope_n(kv(2)))
    split_groups(vs_ref, kv(3))
    split_groups(kw_ref, rope_n(kv(4)))
    split_groups(vw_ref, kv(5))

    gt = jax.nn.sigmoid(_dot_nt(wg_ref[...], xb))
    for g in range(NSA_KV_GROUPS):
        gate_ref[g] = gt[g * GATE_LANES:(g + 1) * GATE_LANES, :]


def _inproj(x2d, pos_col, w_in, row0, n):
    tm = 1024 if n % 1024 == 0 and row0 % 1024 == 0 else n
    blk0 = row0 // tm
    off = np.cumsum([0] + [RET_WIDTH] * 4 + [NSA_WIDTH] + [KV_WIDTH] * 6)
    w_ret = w_in[:, :off[4]].astype(BF16)
    w_nq = w_in[:, off[4]:off[5]].astype(BF16)
    w_kv = w_in[:, off[5]:off[11]].astype(BF16)
    wg = w_in[:, off[11]:].reshape(D_MODEL, NSA_KV_GROUPS, NSA_HPG * 3)
    wg = jnp.pad(wg, ((0, 0), (0, 0), (0, GATE_LANES - NSA_HPG * 3)))
    wg = wg.reshape(D_MODEL, NSA_KV_GROUPS * GATE_LANES).T.astype(BF16)

    lane = np.arange(LANES)
    half_r = RET_DIM // 2
    inv_r = (np.float32(RET_ROPE_BASE) ** (-np.arange(half_r, dtype=np.float32) / np.float32(half_r)))
    inv_r = inv_r.astype(np.float32)[lane % half_r][None, :]
    half_n = ROPE_DIMS // 2
    inv_n = (np.float32(ROPE_THETA) ** (-np.arange(half_n, dtype=np.float32) / np.float32(half_n)))
    jn = lane % NSA_DIM
    inv_n = np.where(jn < ROPE_DIMS, inv_n.astype(np.float32)[jn % half_n], np.float32(0.0))[None, :]

    row = lambda w: pl.BlockSpec((tm, w), lambda i: (i, 0))
    src_row = lambda w: pl.BlockSpec((tm, w), lambda i: (blk0 + i, 0))
    full = lambda a: pl.BlockSpec(a.shape, lambda i: (0,) * a.ndim)
    grp = lambda w: pl.BlockSpec((NSA_KV_GROUPS, tm, w), lambda i: (0, i, 0))
    bf = lambda w: jax.ShapeDtypeStruct((n, w), BF16)
    gbf = jax.ShapeDtypeStruct((NSA_KV_GROUPS, n, NSA_DIM), BF16)
    inv_r = jnp.asarray(inv_r, F32)
    inv_n = jnp.asarray(inv_n, F32)
    return pl.pallas_call(
        _inproj_kernel,
        grid=(n // tm,),
        in_specs=[src_row(D_MODEL), src_row(1), full(w_ret), full(w_nq), full(w_kv), full(wg),
                  full(inv_r), full(inv_n)],
        out_specs=[row(RET_WIDTH)] * 4 + [row(NSA_WIDTH)] * 2 + [row(KV_WIDTH)] * 2
                  + [grp(NSA_DIM)] * 4
                  + [pl.BlockSpec((NSA_KV_GROUPS, GATE_LANES, tm), lambda i: (0, 0, i))],
        out_shape=[bf(RET_WIDTH)] * 4 + [bf(NSA_WIDTH)] * 2
                  + [jax.ShapeDtypeStruct((n, KV_WIDTH), F32)] * 2 + [gbf] * 4
                  + [jax.ShapeDtypeStruct((NSA_KV_GROUPS, GATE_LANES, n), F32)],
        compiler_params=_params(1),
        name="inproj",
    )(x2d, pos_col, w_ret, w_nq, w_kv, wg, inv_r, inv_n)


def _retention_kernel(q_ref, k_ref, v_ref, g_ref, o_ref, state_ref):
    c = RET_CHUNK

    @pl.when(pl.program_id(1) == 0)
    def _():
        state_ref[...] = jnp.zeros_like(state_ref)

    row = lax.broadcasted_iota(jnp.int32, (c, c), 0)
    col = lax.broadcasted_iota(jnp.int32, (c, c), 1)
    rel = (row - col).astype(F32)
    idx = lax.broadcasted_iota(jnp.int32, (c, 1), 0).astype(F32)
    for h in range(RET_HEADS):
        log_g = float(np.log(np.float32(1.0) - np.float32(2.0) ** np.float32(-5.0 - h)))
        cols = slice(h * RET_DIM, (h + 1) * RET_DIM)
        dmask = jnp.where(rel >= 0, jnp.exp(log_g * jnp.maximum(rel, 0.0)), 0.0)
        zeta = jnp.exp(log_g * (c - 1.0 - idx))
        xi = jnp.exp(log_g * (idx + 1.0))
        for j in range(q_ref.shape[0] // c):
            rows = slice(j * c, (j + 1) * c)
            q = q_ref[rows, cols]
            k = k_ref[rows, cols]
            v = v_ref[rows, cols]
            scores = _dot_nt(q, k) * dmask
            inner = _dot(scores.astype(BF16), v)
            prev = state_ref[h]
            cross = _dot(q, prev.astype(BF16)) * xi
            kz = (k.astype(F32) * zeta).astype(BF16)
            kv = lax.dot_general(kz, v, (((0,), (0,)), ((), ())), preferred_element_type=F32)
            state_ref[h] = prev * float(np.exp(np.float32(log_g) * np.float32(c))) + kv
            o = inner + cross
            mu = jnp.mean(o, axis=-1, keepdims=True)
            d = o - mu
            var = jnp.mean(d * d, axis=-1, keepdims=True)
            o = d * lax.rsqrt(var + LN_EPS)
            o_ref[rows, cols] = (jax.nn.silu(g_ref[rows, cols].astype(F32)) * o).astype(BF16)


def _retention(rq, rk, rv, rg, batch, seq):
    per_step = RET_STEP_CHUNKS if (seq // RET_CHUNK) % RET_STEP_CHUNKS == 0 else 1
    nc = seq // (RET_CHUNK * per_step)
    spec = pl.BlockSpec((RET_CHUNK * per_step, RET_WIDTH), lambda b, n: (b * nc + n, 0))
    return pl.pallas_call(
        _retention_kernel,
        grid=(batch, nc),
        in_specs=[spec] * 4,
        out_specs=spec,
        out_shape=jax.ShapeDtypeStruct(rq.shape, BF16),
        scratch_shapes=[pltpu.VMEM((RET_HEADS, RET_DIM, RET_DIM), F32)],
        compiler_params=_params(2),
        name="retention",
    )(rq, rk, rv, rg)


def _compress_kernel(a_ref, pe_ref, w1_ref, w2_ref, o_ref, shift_ref, *, n_cmp):
    rows = a_ref.shape[0]
    a = a_ref[...]
    lo = (a + pe_ref[0]).astype(BF16)
    hi = (a + pe_ref[1]).astype(BF16)
    ridx = lax.broadcasted_iota(jnp.int32, (rows, 1), 0)
    shift_ref[rows:rows + 8, :] = jnp.zeros((8, CMP_HIDDEN), F32)
    for g in range(NSA_KV_GROUPS):
        p = _dot(lo, w1_ref[0, g])
        shift_ref[0:rows, :] = _dot(hi, w1_ref[1, g])
        hid = jax.nn.silu(p + shift_ref[pl.ds(1, rows), :])
        out = _dot(hid.astype(BF16), w2_ref[...])
        o_ref[g] = jnp.where(ridx < n_cmp, out, 0.0).astype(BF16)


def _compress(a, pe, w1, w2, batch, seq):
    rows = seq // CMP_STRIDE
    per = CMP_STRIDE * KV_WIDTH
    n_cmp = (seq - CMP_LEN) // CMP_STRIDE + 1
    a2 = a.reshape(batch * rows, per)
    pe2 = jnp.tile(pe.reshape(2, CMP_STRIDE, 1, NSA_DIM), (1, 1, NSA_KV_GROUPS, 1)).reshape(2, 1, per)
    w1r = w1.reshape(2, CMP_STRIDE, 1, NSA_DIM, CMP_HIDDEN)
    eye = jnp.eye(NSA_KV_GROUPS, dtype=w1.dtype).reshape(1, NSA_KV_GROUPS, 1, NSA_KV_GROUPS, 1, 1)
    w1x = (w1r[:, None] * eye).reshape(2, NSA_KV_GROUPS, per, CMP_HIDDEN).astype(BF16)
    w2b = w2.astype(BF16)
    full = lambda arr: pl.BlockSpec(arr.shape, lambda b: (0,) * arr.ndim)
    return pl.pallas_call(
        functools.partial(_compress_kernel, n_cmp=n_cmp),
        grid=(batch,),
        in_specs=[pl.BlockSpec((rows, per), lambda b: (b, 0)), full(pe2), full(w1x), full(w2b)],
        out_specs=pl.BlockSpec((None, NSA_KV_GROUPS, rows, NSA_DIM), lambda b: (b, 0, 0, 0)),
        out_shape=jax.ShapeDtypeStruct((batch, NSA_KV_GROUPS, rows, NSA_DIM), BF16),
        scratch_shapes=[pltpu.VMEM((rows + 8, CMP_HIDDEN), F32)],
        compiler_params=_params(1),
        name="compress",
    )(a2, pe2, w1x, w2b)


def _heads_to_lanes(ref):
    vt = ref[...].astype(F32).T
    return jnp.concatenate([vt[h * NSA_DIM:(h + 1) * NSA_DIM] for h in range(NSA_HPG)], axis=1).astype(BF16)


def _tile_heads(v):
    return jnp.concatenate([v] * NSA_HPG, axis=1)


def _transpose_into(dst_ref, src_ref, chunk):
    def step(c, _):
        c0 = pl.multiple_of(c * chunk, chunk)
        dst_ref[:NSA_DIM, pl.ds(c0, chunk)] = src_ref[pl.ds(c0, chunk), :].astype(F32).T.astype(BF16)
        return 0
    lax.fori_loop(0, src_ref.shape[0] // chunk, step, 0)


def _nsa_kernel(qraw_ref, qrot_ref, gate_ref, kcmp_ref, vcmp_ref, ovt_ref,
                ks_ref, vs_ref, kw_ref, vw_ref, o_ref, vst_ref, vwt_ref, vct_ref, bias_ref, *, tq, tk, seq):
    i = pl.program_id(2)
    t0 = i * tq
    cols = NSA_HPG * tq
    n_sel = seq // SEL_LEN
    n_cmp_rows = seq // CMP_STRIDE
    blocks_per_tile = tk // SEL_LEN

    @pl.when(i == 0)
    def _():
        chunk = min(512, n_cmp_rows)
        _transpose_into(vst_ref, vs_ref, chunk)
        _transpose_into(vwt_ref, vw_ref, chunk)
        _transpose_into(vct_ref, vcmp_ref, chunk)
        vst_ref[NSA_DIM:, :] = jnp.ones((SUM_ROWS, seq), BF16)
        vwt_ref[NSA_DIM:, :] = jnp.ones((SUM_ROWS, seq), BF16)

    def split_sum(acc):
        return acc[:NSA_DIM] / acc[NSA_DIM:NSA_DIM + 1]

    q_raw = _heads_to_lanes(qraw_ref)
    q_rot = _heads_to_lanes(qrot_ref)
    t_row = t0 + lax.broadcasted_iota(jnp.int32, (1, tq), 1)

    chain_w = cols // NSA_CHAINS
    heads_per_chain = chain_w // tq
    chains = [slice(c * chain_w, (c + 1) * chain_w) for c in range(NSA_CHAINS)]
    tile_chain = lambda v: jnp.concatenate([v] * heads_per_chain, axis=1)

    pw = min(tq, WIN_PART)
    parts = []
    for u in range(tq // pw):
        span = WIN + pw
        ws = pl.multiple_of(jnp.maximum(t0 + u * pw - WIN, 0), pw)
        dist = t_row[:, u * pw:(u + 1) * pw] - (ws + lax.broadcasted_iota(jnp.int32, (span, 1), 0))
        bias_w = jnp.concatenate([jnp.where((dist >= 0) & (dist < WIN), 0.0, NEG)] * NSA_HPG, axis=1)
        q_part = jnp.concatenate([q_rot[:, h * tq + u * pw:h * tq + (u + 1) * pw] for h in range(NSA_HPG)],
                                 axis=1)
        s_w = _dot(kw_ref[pl.ds(ws, span), :], q_part) + bias_w
        p_w = jnp.exp((s_w - jnp.max(s_w, axis=0, keepdims=True)).astype(BF16))
        parts.append(split_sum(_dot(vwt_ref[:, pl.ds(ws, span)], p_w)))
    o_w = jnp.concatenate([parts[u][:, h * pw:(h + 1) * pw]
                           for h in range(NSA_HPG) for u in range(tq // pw)], axis=1)

    c_idx = lax.broadcasted_iota(jnp.int32, (n_cmp_rows, 1), 0)
    bias_c = tile_chain(jnp.where(c_idx * CMP_STRIDE + (CMP_LEN - 1) <= t_row, 0.0, NEG))
    sees_any = tile_chain(jnp.where(t_row >= CMP_LEN - 1, 1.0, 0.0))
    o_c = []
    p_sum = None
    for c in chains:
        s_c = _dot(kcmp_ref[...], q_raw[:, c]) + bias_c
        e_c = jnp.exp(s_c - jnp.max(s_c, axis=0, keepdims=True))
        p_c = e_c * (sees_any / jnp.sum(e_c, axis=0, keepdims=True))
        o_c.append(_dot(vct_ref[...], p_c.astype(BF16)))
        for h in range(heads_per_chain):
            p_h = p_c[:, h * tq:(h + 1) * tq]
            p_sum = p_h if p_sum is None else p_sum + p_h
    o_c = jnp.concatenate(o_c, axis=1)

    p_hi = p_sum.astype(BF16)
    p_lo = (p_sum - p_hi.astype(F32)).astype(BF16)
    ovt = ovt_ref[...]
    imp = _dot(ovt, p_hi) + _dot(ovt, p_lo)
    jb = lax.broadcasted_iota(jnp.int32, (n_sel, tq), 0)
    cur = (t0 + lax.broadcasted_iota(jnp.int32, (n_sel, tq), 1)) >> SEL_SHIFT
    forced = (jb == 0) | (jb == cur) | (jb == cur - 1)
    work = jnp.where(forced, -jnp.inf, imp)
    work = jnp.where(jb <= cur, work, NEG)
    sel_t = jnp.where(forced, 1.0, 0.0)
    for _ in range(max(min(SEL_TOPK, n_sel) - N_FORCED, 0)):
        best = jnp.max(work, axis=0, keepdims=True)
        first = jnp.min(jnp.where(work == best, jb, n_sel), axis=0, keepdims=True)
        hit = jb == first
        sel_t = jnp.where(hit, 1.0, sel_t)
        work = jnp.where(hit, -jnp.inf, work)
    bias_ref[...] = jnp.where(sel_t > 0.5, 0.0, NEG)

    def sel_tile(kt, carry, causal):
        k0 = pl.multiple_of(kt * tk, tk)
        bias = jnp.concatenate(
            [jnp.broadcast_to(bias_ref[pl.ds(kt * blocks_per_tile + j, 1), :], (SEL_LEN, tq))
             for j in range(blocks_per_tile)], axis=0)
        if causal:
            kpos = k0 + lax.broadcasted_iota(jnp.int32, (tk, 1), 0)
            bias = jnp.where(kpos <= t_row, bias, NEG)
        bias = tile_chain(bias)
        k_t = ks_ref[pl.ds(k0, tk), :]
        v_t = vst_ref[:, pl.ds(k0, tk)]
        out = []
        scores = [_dot(k_t, q_rot[:, c]) + bias for c in chains]
        for (m, acc), s in zip(carry, scores):
            m_new = jnp.maximum(m, jnp.max(s, axis=0, keepdims=True))
            p = jnp.exp((s - m_new).astype(BF16))
            acc = jnp.exp(m - m_new) * acc + _dot(v_t, p)
            out.append((m_new, acc))
        return tuple(out)

    n_full = t0 // tk
    init = tuple((jnp.full((1, chain_w), NEG, F32), jnp.zeros((NSA_DIM + SUM_ROWS, chain_w), F32))
                 for _ in chains)
    carry = lax.fori_loop(0, n_full, functools.partial(sel_tile, causal=False), init)
    for d in range(max(tq // tk, 1)):
        carry = sel_tile(n_full + d, carry, causal=True)
    o_s = jnp.concatenate([split_sum(acc) for _, acc in carry], axis=1)

    gt = gate_ref[...]
    outs = []
    for h in range(NSA_HPG):
        c = slice(h * tq, (h + 1) * tq)
        outs.append(gt[3 * h:3 * h + 1] * o_c[:, c] + gt[3 * h + 1:3 * h + 2] * o_s[:, c]
                    + gt[3 * h + 2:3 * h + 3] * o_w[:, c])
    o_ref[...] = jnp.concatenate(outs, axis=0).T.astype(BF16)


def _anchored(kernel_fn, n_inputs, n_anchors):
    def body(*refs, **static):
        kernel_fn(*refs[:n_inputs], *refs[n_inputs + n_anchors:], **static)
    return body


def _add_anchors(kernel_fn, args, in_specs, after):
    after = [a for a in (after or ()) if a is not None]
    if not after:
        return kernel_fn
    body = _anchored(kernel_fn, len(args), len(after))
    args.extend(after)
    in_specs.extend([pl.BlockSpec(memory_space=pl.ANY)] * len(after))
    return body


def _nsa(nq, nqr, gates, kcmp, vcmp, ks, vs, kw, vw, batch, seq, after=None):
    n = batch * seq
    tq = 512
    tk = 512 if seq % 512 == 0 else seq
    nqb = seq // tq
    n_sel = seq // SEL_LEN
    rows_c = seq // CMP_STRIDE
    gw = NSA_HPG * NSA_DIM
    cs = np.arange(rows_c)[None, :] * CMP_STRIDE
    ss = np.arange(n_sel)[:, None] * SEL_LEN
    n_cmp = (seq - CMP_LEN) // CMP_STRIDE + 1
    ovt = ((cs < ss + SEL_LEN) & (cs + CMP_LEN > ss) & (np.arange(rows_c)[None, :] < n_cmp))
    ovt = jnp.asarray(ovt.astype(np.float32), BF16)

    qspec = pl.BlockSpec((tq, gw), lambda b, g, i: (b * nqb + i, g))
    cspec = pl.BlockSpec((None, None, rows_c, NSA_DIM), lambda b, g, i: (b, g, 0, 0))
    kspec = pl.BlockSpec((None, seq, NSA_DIM), lambda b, g, i: (g, b, 0))
    args = [nq, nqr, gates, kcmp, vcmp, ovt, ks, vs, kw, vw]
    in_specs = [qspec, qspec,
                pl.BlockSpec((None, GATE_LANES, tq), lambda b, g, i: (g, 0, b * nqb + i)),
                cspec, cspec, pl.BlockSpec(ovt.shape, lambda b, g, i: (0, 0)),
                kspec, kspec, kspec, kspec]
    body = functools.partial(_add_anchors(_nsa_kernel, args, in_specs, after), tq=tq, tk=tk, seq=seq)
    return pl.pallas_call(
        body,
        grid=(batch, NSA_KV_GROUPS, nqb),
        in_specs=in_specs,
        out_specs=qspec,
        out_shape=jax.ShapeDtypeStruct((n, NSA_WIDTH), BF16),
        scratch_shapes=[pltpu.VMEM((NSA_DIM + SUM_ROWS, seq), BF16), pltpu.VMEM((NSA_DIM + SUM_ROWS, seq), BF16),
                        pltpu.VMEM((NSA_DIM, rows_c), BF16), pltpu.VMEM((n_sel, tq), F32)],
        compiler_params=_params(3),
        name="nsa",
    )(*args)


def _memkv_kernel(mem_ref, w_ref, kv_ref):
    kv_ref[...] = _dot(mem_ref[...].astype(BF16), w_ref[...]).astype(BF16)


def _memkv(mem2d, w_xkv):
    n = mem2d.shape[0]
    w = w_xkv.astype(BF16)
    return pl.pallas_call(
        _memkv_kernel,
        grid=(n // MEM_LEN,),
        in_specs=[pl.BlockSpec((MEM_LEN, D_MODEL), lambda i: (i, 0)),
                  pl.BlockSpec(w.shape, lambda i: (0, 0))],
        out_specs=pl.BlockSpec((MEM_LEN, 2 * D_MODEL), lambda i: (i, 0)),
        out_shape=jax.ShapeDtypeStruct((n, 2 * D_MODEL), BF16),
        compiler_params=_params(1),
        name="memkv",
    )(mem2d, w)


def _pack_halves(v):
    half = D_MODEL // 2
    hi = pltpu.bitcast(v[:, :half].astype(BF16).astype(F32), jnp.uint32)
    lo = pltpu.bitcast(v[:, half:].astype(BF16).astype(F32), jnp.uint32)
    return hi | (lo >> 16)


def _unpack_halves(words):
    return pltpu.bitcast(words & jnp.uint32(0xFFFF0000), F32), pltpu.bitcast(words << 16, F32)


def _postmix_kernel(x_ref, oret_ref, onsa_ref, kv_ref, wout_ref, wq_ref, wo_ref,
                    g1_ref, b1_ref, g2_ref, b2_ref, x2_ref, x2p_ref):
    mixed = jnp.concatenate([oret_ref[...], onsa_ref[...]], axis=1)
    x1 = _layer_norm(DN_ALPHA * x_ref[...] + _dot(mixed, wout_ref[...]), g1_ref[...], b1_ref[...])
    q = (_dot(x1.astype(BF16), wq_ref[...]) * (XATT_DIM ** -0.5)).astype(BF16)
    heads = []
    for h in range(XATT_HEADS):
        cols = slice(h * XATT_DIM, (h + 1) * XATT_DIM)
        s = _dot_nt(q[:, cols], kv_ref[:, cols])
        m = jnp.max(s, axis=-1, keepdims=True)
        p = jnp.exp(s - m)
        l = jnp.sum(p, axis=-1, keepdims=True)
        heads.append(_dot(p.astype(BF16), kv_ref[:, D_MODEL + h * XATT_DIM:D_MODEL + (h + 1) * XATT_DIM]) / l)
    att = jnp.concatenate(heads, axis=1).astype(BF16)
    x2 = _layer_norm(DN_ALPHA * x1 + _dot(att, wo_ref[...]), g2_ref[...], b2_ref[...])
    x2_ref[...] = x2
    x2p_ref[...] = _pack_halves(x2)


def _postmix(x2d, o_ret, o_nsa, kvx, w_out, w_xq, w_xo, ln1_g, ln1_b, ln2_g, ln2_b, batch0, batch, seq,
             after=None):
    n = batch * seq
    tm = 512 if seq % 512 == 0 else seq
    per_b = seq // tm
    row = lambda w: pl.BlockSpec((tm, w), lambda b, i: (b * per_b + i, 0))
    full = lambda a: pl.BlockSpec(a.shape, lambda b, i: (0,) * a.ndim)
    ws = [w_out.astype(BF16), w_xq.astype(BF16), w_xo.astype(BF16)]
    vecs = [v.reshape(1, D_MODEL) for v in (ln1_g, ln1_b, ln2_g, ln2_b)]
    args = [x2d, o_ret, o_nsa, kvx, *ws, *vecs]
    in_specs = ([pl.BlockSpec((tm, D_MODEL), lambda b, i: ((batch0 + b) * per_b + i, 0)),
                 row(RET_WIDTH), row(NSA_WIDTH),
                 pl.BlockSpec((MEM_LEN, 2 * D_MODEL), lambda b, i: (batch0 + b, 0))]
                + [full(w) for w in ws] + [full(v) for v in vecs])
    return pl.pallas_call(
        _add_anchors(_postmix_kernel, args, in_specs, after),
        grid=(batch, per_b),
        in_specs=in_specs,
        out_specs=[row(D_MODEL),
                   row(D_MODEL // 2)],
        out_shape=[jax.ShapeDtypeStruct((n, D_MODEL), F32),
                   jax.ShapeDtypeStruct((n, D_MODEL // 2), jnp.uint32)],
        compiler_params=_params(2),
        name="postmix",
    )(*args)


def _router_kernel(x_ref, wr_ref, bias_ref, e_ref, rank_ref, w_ref, cnt_ref, cntrow_ref, carry_ref, carryrow_ref):
    tn = x_ref.shape[0]
    per = N_EXPERTS // N_GROUPS

    @pl.when(pl.program_id(0) == 0)
    def _():
        carry_ref[...] = jnp.zeros_like(carry_ref)
        carryrow_ref[...] = jnp.zeros_like(carryrow_ref)

    logits = _dot_nt(wr_ref[...], x_ref[...].astype(BF16))
    scores = jax.nn.sigmoid(logits)
    biased = scores + bias_ref[...]
    b3 = biased.reshape(N_GROUPS, per, tn)
    member = lax.broadcasted_iota(jnp.int32, (N_GROUPS, per, tn), 1)
    top1 = jnp.max(b3, axis=1, keepdims=True)
    first1 = jnp.min(jnp.where(b3 == top1, member, per), axis=1, keepdims=True)
    top2 = jnp.max(jnp.where(member == first1, -jnp.inf, b3), axis=1, keepdims=True)
    gscore = top1 + top2
    gidx = lax.broadcasted_iota(jnp.int32, (N_GROUPS, 1, tn), 0)
    gwork = gscore
    for _ in range(TOPK_GROUPS - 1):
        gbest = jnp.max(gwork, axis=0, keepdims=True)
        gfirst = jnp.min(jnp.where(gwork == gbest, gidx, N_GROUPS), axis=0, keepdims=True)
        gwork = jnp.where(gidx == gfirst, -jnp.inf, gwork)
    kth = jnp.max(gwork, axis=0, keepdims=True)
    work = jnp.where(gscore >= kth, b3, NEG).reshape(N_EXPERTS, tn)
    eidx = lax.broadcasted_iota(jnp.int32, (N_EXPERTS, tn), 0)
    picks = []
    chosen = jnp.zeros((N_EXPERTS, tn), F32)
    for _ in range(TOP_K):
        best = jnp.max(work, axis=0, keepdims=True)
        first = jnp.min(jnp.where(work == best, eidx, N_EXPERTS), axis=0, keepdims=True)
        hit = eidx == first
        picks.append((first, hit))
        chosen = jnp.where(hit, 1.0, chosen)
        work = jnp.where(hit, -jnp.inf, work)

    r_i = lax.broadcasted_iota(jnp.int32, (tn, tn), 0)
    c_i = lax.broadcasted_iota(jnp.int32, (tn, tn), 1)
    before = jnp.where(r_i < c_i, 1.0, 0.0).astype(BF16)
    chosen_b = chosen.astype(BF16)
    rank = _dot(chosen_b, before) + carry_ref[...]
    carry_ref[...] = carry_ref[...] + jnp.sum(chosen, axis=1, keepdims=True)
    carryrow_ref[...] = carryrow_ref[...] + _dot_nt(jnp.ones((8, tn), BF16), chosen_b)
    cnt_ref[...] = carry_ref[...]
    cntrow_ref[...] = carryrow_ref[...]

    wsel = [jnp.sum(jnp.where(hit, scores, 0.0), axis=0, keepdims=True) for _, hit in picks]
    wsum = wsel[0]
    for v in wsel[1:]:
        wsum = wsum + v
    for kk, (first, hit) in enumerate(picks):
        e_ref[kk:kk + 1, :] = first
        rank_ref[kk:kk + 1, :] = jnp.sum(jnp.where(hit, rank, 0.0), axis=0, keepdims=True).astype(jnp.int32)
        w_ref[kk:kk + 1, :] = wsel[kk] / wsum * ROUTED_SCALE


def _router(x2, w_router, router_bias):
    n = x2.shape[0]
    tn = 512 if n % 512 == 0 else n
    wr_t = w_router.T.astype(BF16)
    bias = router_bias.reshape(N_EXPERTS, 1).astype(F32)
    kspec = pl.BlockSpec((TOP_K, tn), lambda i: (0, i))
    return pl.pallas_call(
        _router_kernel,
        grid=(n // tn,),
        in_specs=[pl.BlockSpec((tn, D_MODEL), lambda i: (i, 0)),
                  pl.BlockSpec(wr_t.shape, lambda i: (0, 0)),
                  pl.BlockSpec(bias.shape, lambda i: (0, 0))],
        out_specs=[kspec, kspec, kspec, pl.BlockSpec((N_EXPERTS, 1), lambda i: (0, 0)),
                   pl.BlockSpec((8, N_EXPERTS), lambda i: (0, 0))],
        out_shape=[jax.ShapeDtypeStruct((TOP_K, n), jnp.int32),
                   jax.ShapeDtypeStruct((TOP_K, n), jnp.int32),
                   jax.ShapeDtypeStruct((TOP_K, n), F32),
                   jax.ShapeDtypeStruct((N_EXPERTS, 1), F32),
                   jax.ShapeDtypeStruct((8, N_EXPERTS), F32)],
        scratch_shapes=[pltpu.VMEM((N_EXPERTS, 1), F32), pltpu.VMEM((8, N_EXPERTS), F32)],
        compiler_params=_params(1),
        name="router",
    )(x2, wr_t, bias)


def _slots_kernel(e_ref, rank_ref, cnt_ref, cntrow_ref, dest_ref, blk_e_ref, valid_ref, used_ref,
                  *, blk, n_blocks):
    pad = lambda c: jnp.ceil(c / blk) * blk
    cnt = cnt_ref[...]
    padded = pad(cnt)
    padded_row = pad(cntrow_ref[0:1, :])
    r_i = lax.broadcasted_iota(jnp.int32, (N_EXPERTS, N_EXPERTS), 0)
    c_i = lax.broadcasted_iota(jnp.int32, (N_EXPERTS, N_EXPERTS), 1)
    start = jnp.sum(jnp.where(c_i < r_i, padded_row, 0.0), axis=1, keepdims=True)
    end = start + padded
    e = e_ref[...]
    dest = rank_ref[...]
    for ex in range(N_EXPERTS):
        dest = dest + jnp.where(e == ex, start[ex:ex + 1, :].astype(jnp.int32), 0)
    dest_ref[...] = dest
    bstart = (lax.broadcasted_iota(jnp.int32, (1, n_blocks), 1) * blk).astype(F32)
    owner = jnp.sum(jnp.where(end <= bstart, 1.0, 0.0), axis=0, keepdims=True)
    blk_e_ref[...] = jnp.minimum(owner, N_EXPERTS - 1.0).astype(jnp.int32)
    inside = (start <= bstart) & (bstart < end)
    real = jnp.clip(start + cnt - bstart, 0.0, float(blk))
    valid = jnp.sum(jnp.where(inside, real, 0.0), axis=0, keepdims=True)
    valid_ref[...] = valid.astype(jnp.int32)
    used_ref[...] = jnp.sum(jnp.where(valid > 0.0, 1.0, 0.0), axis=1, keepdims=True).astype(jnp.int32)


def _slots(e_k, rank_k, counts, counts_row, blk, n_blocks):
    n = e_k.shape[1]
    full = lambda shape: pl.BlockSpec(shape, lambda: (0,) * len(shape))
    return pl.pallas_call(
        functools.partial(_slots_kernel, blk=blk, n_blocks=n_blocks),
        in_specs=[full((TOP_K, n)), full((TOP_K, n)), full((N_EXPERTS, 1)), full((8, N_EXPERTS))],
        out_specs=[full((TOP_K, n)), full((1, n_blocks)), full((1, n_blocks)), full((1, 1))],
        out_shape=[jax.ShapeDtypeStruct((TOP_K, n), jnp.int32),
                   jax.ShapeDtypeStruct((1, n_blocks), jnp.int32),
                   jax.ShapeDtypeStruct((1, n_blocks), jnp.int32),
                   jax.ShapeDtypeStruct((1, 1), jnp.int32)],
        compiler_params=pltpu.CompilerParams(vmem_limit_bytes=VMEM_LIMIT),
        name="slots",
    )(e_k, rank_k, counts, counts_row)


def _sc_worker_base(per_worker):
    return (lax.axis_index("s") * SC_CORES + lax.axis_index("c")) * per_worker


def _sc_scatter_rows(rows, idx, n_out):
    n, width = rows.shape
    k_lists = idx.shape[0] // n
    workers = SC_CORES * SC_SUBCORES
    per_worker = n // workers
    assert per_worker * workers == n and per_worker % (SC_CHUNK * SC_INFLIGHT) == 0
    mesh = plsc.VectorSubcoreMesh(core_axis_name="c", subcore_axis_name="s")
    lanes = range(SC_INFLIGHT)

    @functools.partial(
        pl.kernel, mesh=mesh,
        out_type=jax.ShapeDtypeStruct((n_out, width), rows.dtype),
        scratch_types=[pltpu.VMEM((SC_CHUNK, width), rows.dtype)] * SC_INFLIGHT
                      + [pltpu.VMEM((SC_CHUNK,), jnp.int32)] * (k_lists * SC_INFLIGHT)
                      + [pltpu.SemaphoreType.DMA] * (2 * SC_INFLIGHT),
        name="sc_scatter")
    def scatter(rows_hbm, idx_hbm, out_hbm, *scratch):
        rows_vs = scratch[:SC_INFLIGHT]
        idx_vs = [scratch[SC_INFLIGHT + j * k_lists:SC_INFLIGHT + (j + 1) * k_lists] for j in lanes]
        sems = scratch[SC_INFLIGHT * (1 + k_lists):]
        sem_in, sem_out = sems[:SC_INFLIGHT], sems[SC_INFLIGHT:]
        base = _sc_worker_base(per_worker)

        @pl.loop(0, per_worker // (SC_CHUNK * SC_INFLIGHT))
        def _(gi):
            offs = [pl.multiple_of(base + (gi * SC_INFLIGHT + j) * SC_CHUNK, SC_CHUNK) for j in lanes]
            loads = []
            for j in lanes:
                loads.append([pltpu.async_copy(rows_hbm.at[pl.ds(offs[j], SC_CHUNK)], rows_vs[j], sem_in[j])]
                             + [pltpu.async_copy(
                                 idx_hbm.at[pl.ds(pl.multiple_of(k * n + offs[j], SC_CHUNK), SC_CHUNK)],
                                 idx_vs[j][k], sem_in[j]) for k in range(k_lists)])
            copies = []
            for j in lanes:
                for c in loads[j]:
                    c.wait()
                copies.append([pltpu.async_copy(rows_vs[j], out_hbm.at[idx_vs[j][k]], sem_out[j])
                               for k in range(k_lists)])
            for j in lanes:
                for c in copies[j]:
                    c.wait()

    return scatter(rows, idx)


def _experts_kernel(blk_e_ref, valid_ref, used_ref, xs_ref, wg_ref, wu_ref, wd_ref, y_ref, wg_b, wu_b, wd_b):
    del used_ref
    i = pl.program_id(0)
    valid = valid_ref[i]

    @pl.when((i == 0) | (blk_e_ref[i] != blk_e_ref[jnp.maximum(i - 1, 0)]))
    def _():
        wg_b[...] = wg_ref[...].astype(BF16)
        wu_b[...] = wu_ref[...].astype(BF16)
        wd_b[...] = wd_ref[...].astype(BF16)

    @pl.when(valid > 0)
    def _():
        half = D_MODEL // 2
        row = lax.broadcasted_iota(jnp.int32, (xs_ref.shape[0], 1), 0)
        hi, lo = (v.astype(BF16) for v in _unpack_halves(jnp.where(row < valid, xs_ref[...], jnp.uint32(0))))
        gate = _dot(hi, wg_b[:half, :]) + _dot(lo, wg_b[half:, :])
        up = _dot(hi, wu_b[:half, :]) + _dot(lo, wu_b[half:, :])
        y_ref[...] = _pack_halves(_dot((jax.nn.silu(gate) * up).astype(BF16), wd_b[...]))

    @pl.when(valid <= 0)
    def _():
        y_ref[...] = jnp.zeros_like(y_ref)


def _experts(blk_e, valid, n_used, xs, w_gate, w_up, w_down, blk, after=None):
    cap, width = xs.shape
    wspec = lambda a: pl.BlockSpec((None,) + a.shape[1:], lambda i, be, nv, nu: (be[i], 0, 0))
    rows = pl.BlockSpec((blk, width), lambda i, be, nv, nu: (jnp.minimum(i, nu[0]), 0))
    args = [blk_e, valid, n_used, xs, w_gate, w_up, w_down]
    in_specs = [rows, wspec(w_gate), wspec(w_up), wspec(w_down)]
    body = _add_anchors(_experts_kernel, args, in_specs, after)
    return pl.pallas_call(
        body,
        grid_spec=pltpu.PrefetchScalarGridSpec(
            num_scalar_prefetch=3,
            grid=(cap // blk,),
            in_specs=in_specs,
            out_specs=rows,
            scratch_shapes=[pltpu.VMEM(w.shape[1:], BF16) for w in (w_gate, w_up, w_down)],
        ),
        out_shape=jax.ShapeDtypeStruct(xs.shape, xs.dtype),
        compiler_params=_params(1),
        name="experts",
    )(*args)


def _sc_gather_rows(table, idx):
    b, width = idx.shape[0], table.shape[1]
    workers = SC_CORES * SC_SUBCORES
    per_worker = b // workers
    assert per_worker * workers == b and per_worker % (SC_CHUNK * SC_INFLIGHT) == 0
    mesh = plsc.VectorSubcoreMesh(core_axis_name="c", subcore_axis_name="s")

    @functools.partial(
        pl.kernel, mesh=mesh,
        out_type=jax.ShapeDtypeStruct((b, width), table.dtype),
        scratch_types=[pltpu.VMEM((SC_CHUNK,), jnp.int32)] * SC_INFLIGHT
                      + [pltpu.VMEM((SC_CHUNK, width), table.dtype)] * SC_INFLIGHT
                      + [pltpu.SemaphoreType.DMA] * (1 + 2 * SC_INFLIGHT),
        name="sc_gather")
    def gather(table_hbm, idx_hbm, out_hbm, *scratch):
        idx_vs = scratch[:SC_INFLIGHT]
        rows_vs = scratch[SC_INFLIGHT:2 * SC_INFLIGHT]
        sem_idx = scratch[2 * SC_INFLIGHT]
        sem_rows = scratch[2 * SC_INFLIGHT + 1:3 * SC_INFLIGHT + 1]
        sem_out = scratch[3 * SC_INFLIGHT + 1:]
        base = _sc_worker_base(per_worker)
        lanes = range(SC_INFLIGHT)

        @pl.loop(0, per_worker // (SC_CHUNK * SC_INFLIGHT))
        def _(gi):
            offs = [pl.multiple_of(base + (gi * SC_INFLIGHT + j) * SC_CHUNK, SC_CHUNK) for j in lanes]
            loads = [pltpu.async_copy(idx_hbm.at[pl.ds(offs[j], SC_CHUNK)], idx_vs[j], sem_idx) for j in lanes]
            for c in loads:
                c.wait()
            gathers = [pltpu.async_copy(table_hbm.at[idx_vs[j]], rows_vs[j], sem_rows[j]) for j in lanes]
            writes = []
            for j in lanes:
                gathers[j].wait()
                writes.append(pltpu.async_copy(rows_vs[j], out_hbm.at[pl.ds(offs[j], SC_CHUNK)], sem_out[j]))
            for c in writes:
                c.wait()

    return gather(table, idx)


def _combine_kernel(x_ref, wk_ref, yk_ref, wsg_ref, wsu_ref, wsd_ref, g_ref, b_ref, *rest):
    o_ref = rest[-1]
    x = x_ref[...]
    xb = x.astype(BF16)
    shared = _dot((jax.nn.silu(_dot(xb, wsg_ref[...])) * _dot(xb, wsu_ref[...])).astype(BF16), wsd_ref[...])
    wk = wk_ref[...]
    routed_hi = routed_lo = None
    for kk in range(TOP_K):
        hi, lo = _unpack_halves(yk_ref[kk])
        w = wk[:, kk:kk + 1]
        routed_hi = hi * w if kk == 0 else routed_hi + hi * w
        routed_lo = lo * w if kk == 0 else routed_lo + lo * w
    routed = jnp.concatenate([routed_hi, routed_lo], axis=1)
    o_ref[...] = _layer_norm(DN_ALPHA * x + (routed + shared), g_ref[...], b_ref[...])


def _combine(x2, w_tok, yk, ws_gate, ws_up, ws_down, ln3_g, ln3_b, row0, n_total, out_prev, after=None,
             part=(0, 1)):
    n = x2.shape[0] // part[1]
    sub0 = part[0] * n
    tt = 512 if n % 512 == 0 and row0 % 512 == 0 else n
    blk0 = (row0 + sub0) // tt
    sub_blk = sub0 // tt
    ws = [ws_gate.astype(BF16), ws_up.astype(BF16), ws_down.astype(BF16)]
    vecs = [ln3_g.reshape(1, D_MODEL), ln3_b.reshape(1, D_MODEL)]
    full = lambda a: pl.BlockSpec(a.shape, lambda i: (0,) * a.ndim)
    args = [x2, w_tok, yk, *ws, *vecs]
    in_specs = ([pl.BlockSpec((tt, D_MODEL), lambda i: (sub_blk + i, 0)),
                 pl.BlockSpec((tt, TOP_K), lambda i: (sub_blk + i, 0)),
                 pl.BlockSpec((TOP_K, tt, D_MODEL // 2), lambda i: (0, sub_blk + i, 0))]
                + [full(a) for a in ws] + [full(v) for v in vecs])
    aliases = {}
    for anchor in (a for a in (after or ()) if a is not None):
        args.append(anchor)
        in_specs.append(pl.BlockSpec(memory_space=pl.ANY))
    if out_prev is not None:
        aliases = {len(args): 0}
        args.append(out_prev)
        in_specs.append(pl.BlockSpec(memory_space=pl.ANY))
    return pl.pallas_call(
        _combine_kernel,
        grid=(n // tt,),
        in_specs=in_specs,
        out_specs=pl.BlockSpec((tt, D_MODEL), lambda i: (blk0 + i, 0)),
        out_shape=jax.ShapeDtypeStruct((n_total, D_MODEL), F32),
        input_output_aliases=aliases,
        compiler_params=_params(1),
        name="combine",
    )(*args)


EXPERT_BLOCK = 1024


def _moe_dispatch(x2, x2p, w_router, router_bias):
    n = x2.shape[0]
    cap = n * TOP_K + N_EXPERTS * EXPERT_BLOCK
    e_k, rank_k, w_k, counts, counts_row = _router(x2, w_router, router_bias)
    dest, blk_e, valid, n_used = _slots(e_k, rank_k, counts, counts_row, EXPERT_BLOCK, cap // EXPERT_BLOCK)
    dest = dest.reshape(-1)
    return dict(w_tok=w_k.T, dest=dest, blk_e=blk_e.reshape(-1), valid=valid.reshape(-1),
                n_used=n_used.reshape(-1), xs=_sc_scatter_rows(x2p, dest, cap))


def _moe_experts(routed, w_gate, w_up, w_down, after):
    y = _experts(routed["blk_e"], routed["valid"], routed["n_used"], routed["xs"], w_gate, w_up, w_down,
                 EXPERT_BLOCK, after=after)
    n = routed["dest"].shape[0] // TOP_K
    return y, _sc_gather_rows(y, routed["dest"]).reshape(TOP_K, n, D_MODEL // 2)


def _layer(x, mem, positions, w_in, cmp_pe_k, cmp_pe_v, cmp_w1_k, cmp_w2_k, cmp_w1_v, cmp_w2_v,
           w_out, ln1_g, ln1_b, w_xq, w_xkv, w_xo, ln2_g, ln2_b, w_router, router_bias,
           w_gate, w_up, w_down, ws_gate, ws_up, ws_down, ln3_g, ln3_b):
    batch, seq, _ = x.shape
    n_total = batch * seq
    x2d = x.reshape(n_total, D_MODEL)
    pos_col = positions.astype(F32).reshape(n_total, 1)
    kvx = _memkv(mem.reshape(batch * MEM_LEN, D_MODEL), w_xkv)
    last = max(1, batch // 4)
    sizes = [batch - last, last] if batch > 1 else [batch]
    starts = [sum(sizes[:g]) for g in range(len(sizes))]

    def mixers_in(g):
        nb, row0 = sizes[g], starts[g] * seq
        (rq, rk, rv, rg, nq, nqr, kc, vc, ks, vs, kw, vw, gates) = _inproj(x2d, pos_col, w_in, row0, nb * seq)
        o_ret = _retention(rq, rk, rv, rg, nb, seq)
        kcmp = _compress(kc, cmp_pe_k, cmp_w1_k, cmp_w2_k, nb, seq)
        vcmp = _compress(vc, cmp_pe_v, cmp_w1_v, cmp_w2_v, nb, seq)
        return o_ret, (nq, nqr, gates, kcmp, vcmp, ks, vs, kw, vw)

    def attend(g, nsa_args, after):
        return _nsa(*nsa_args, sizes[g], seq, after=after)

    def mix_and_route(g, o_ret, o_nsa, after):
        x2, x2p = _postmix(x2d, o_ret, o_nsa, kvx, w_out, w_xq, w_xo, ln1_g, ln1_b, ln2_g, ln2_b,
                           starts[g], sizes[g], seq, after=after)
        return x2, _moe_dispatch(x2, x2p, w_router, router_bias)

    def combine(g, x2, routed, yk, out_prev, after=None, part=(0, 1)):
        return _combine(x2, routed["w_tok"], yk, ws_gate, ws_up, ws_down, ln3_g, ln3_b, starts[g] * seq,
                        n_total, out_prev, after=after, part=part)

    o_ret, nsa_args = mixers_in(0)
    x2, routed = mix_and_route(0, o_ret, attend(0, nsa_args, None), None)
    if len(sizes) == 1:
        y, yk = _moe_experts(routed, w_gate, w_up, w_down, after=None)
        return combine(0, x2, routed, yk, None).reshape(batch, seq, D_MODEL)
    o_ret1, nsa_args1 = mixers_in(1)
    y, yk = _moe_experts(routed, w_gate, w_up, w_down, after=[o_ret1])
    o_nsa1 = attend(1, nsa_args1, [y])
    x2_1, routed1 = mix_and_route(1, o_ret1, o_nsa1, [yk])
    out = combine(0, x2, routed, yk, None, [routed1["dest"]], part=(0, 2))
    y1, yk1 = _moe_experts(routed1, w_gate, w_up, w_down, after=[out])
    out = combine(0, x2, routed, yk, out, [y1], part=(1, 2))
    out = combine(1, x2_1, routed1, yk1, out)
    return out.reshape(batch, seq, D_MODEL)


def kernel(x, mem, positions, w_in, cmp_pe_k, cmp_pe_v, cmp_w1_k, cmp_w2_k, cmp_w1_v, cmp_w2_v, w_out, ln1_g, ln1_b, w_xq, w_xkv, w_xo, ln2_g, ln2_b, w_router, router_bias, w_gate, w_up, w_down, ws_gate, ws_up, ws_down, ln3_g, ln3_b):
    for l in range(DEPTH):
        x = _layer(x, mem, positions, w_in[l], cmp_pe_k[l], cmp_pe_v[l], cmp_w1_k[l], cmp_w2_k[l],
                   cmp_w1_v[l], cmp_w2_v[l], w_out[l], ln1_g[l], ln1_b[l], w_xq[l], w_xkv[l],
                   w_xo[l], ln2_g[l], ln2_b[l], w_router[l], router_bias[l], w_gate[l], w_up[l],
                   w_down[l], ws_gate[l], ws_up[l], ws_down[l], ln3_g[l], ln3_b[l])
    return x
```

```python
import functools

import numpy as np
import jax
import jax.numpy as jnp
from jax import lax
from jax.experimental import pallas as pl
from jax.experimental.pallas import tpu as pltpu
from jax.experimental.pallas import tpu_sc as plsc

D_MODEL = 1024
MEM_LEN = 256
DEPTH = 1
DN_ALPHA = (2 * DEPTH) ** 0.25
LN_EPS = 1e-5
NEG = -1e30

RET_HEADS = 4
RET_DIM = 128
RET_CHUNK = 128
RET_ROPE_BASE = 10000.0
RET_STEP_CHUNKS = 8
RET_WIDTH = RET_HEADS * RET_DIM

NSA_HEADS = 8
NSA_KV_GROUPS = 2
NSA_HPG = NSA_HEADS // NSA_KV_GROUPS
NSA_DIM = 64
NSA_WIDTH = NSA_HEADS * NSA_DIM
KV_WIDTH = NSA_KV_GROUPS * NSA_DIM
CMP_LEN = 32
CMP_STRIDE = 16
CMP_HIDDEN = 256
SEL_LEN = 64
SEL_SHIFT = 6
SEL_TOPK = 16
N_FORCED = 3
WIN = 512
ROPE_THETA = 500000.0
ROPE_DIMS = NSA_DIM // 4
GATE_LANES = 16
NSA_CHAINS = 1
LOG2_E = 1.4426950408889634
SUM_ROWS = 16
WIN_PART = 256

SC_CORES = 2
SC_SUBCORES = 16
SC_CHUNK = 64
SC_INFLIGHT = 2

XATT_HEADS = 4
XATT_DIM = D_MODEL // XATT_HEADS

N_EXPERTS = 64
TOP_K = 8
N_GROUPS = 8
TOPK_GROUPS = 4
EXPERT_FF = 256
SHARED_FF = 256
ROUTED_SCALE = 2.5

LANES = 128
VMEM_LIMIT = 56 * 1024 * 1024

F32 = jnp.float32
BF16 = jnp.bfloat16
NT_DIMS = (((1,), (1,)), ((), ()))


def _params(n_axes):
    return pltpu.CompilerParams(dimension_semantics=("arbitrary",) * n_axes,
                                vmem_limit_bytes=VMEM_LIMIT)


def _dot(a, b):
    return jnp.dot(a, b, preferred_element_type=F32)


def _dot_nt(a, b):
    return lax.dot_general(a, b, NT_DIMS, preferred_element_type=F32)


def _layer_norm(v, g, b):
    mu = jnp.mean(v, axis=-1, keepdims=True)
    d = v - mu
    var = jnp.mean(d * d, axis=-1, keepdims=True)
    return d * lax.rsqrt(var + LN_EPS) * g + b


def _inproj_kernel(x_ref, pos_ref, wret_ref, wnq_ref, wkv_ref, wg_ref, invr_ref, invn_ref,
                   rq_ref, rk_ref, rv_ref, rg_ref, nq_ref, nqr_ref, kc_ref, vc_ref,
                   ks_ref, vs_ref, kw_ref, vw_ref, gate_ref):
    xb = x_ref[...].astype(BF16)
    pos = pos_ref[...]
    lane = lax.broadcasted_iota(jnp.int32, (1, LANES), 1)

    ang = pos * invr_ref[...]
    cos_r = jnp.cos(ang)
    sin_r = jnp.sin(ang)
    sin_r = jnp.where(lane < RET_DIM // 2, -sin_r, sin_r)
    q_all = _dot(xb, wret_ref[:, :RET_WIDTH])
    k_all = _dot(xb, wret_ref[:, RET_WIDTH:2 * RET_WIDTH])
    for h in range(RET_HEADS):
        cols = slice(h * RET_DIM, (h + 1) * RET_DIM)
        q = q_all[:, cols]
        rq_ref[:, cols] = (q * cos_r + pltpu.roll(q, RET_DIM // 2, 1) * sin_r).astype(BF16)
        k = k_all[:, cols]
        k = (k * cos_r + pltpu.roll(k, RET_DIM // 2, 1) * sin_r) * (RET_DIM ** -0.5)
        rk_ref[:, cols] = k.astype(BF16)
    rv_ref[...] = _dot(xb, wret_ref[:, 2 * RET_WIDTH:3 * RET_WIDTH]).astype(BF16)
    rg_ref[...] = _dot(xb, wret_ref[:, 3 * RET_WIDTH:4 * RET_WIDTH]).astype(BF16)

    half = ROPE_DIMS // 2
    j = lane % NSA_DIM
    angn = pos * invn_ref[...]
    cos_n = jnp.cos(angn)
    sin_n = jnp.sin(angn)
    sin_lo = jnp.where(j < half, -sin_n, 0.0)
    sin_hi = jnp.where((j >= half) & (j < 2 * half), sin_n, 0.0)

    def rope_n(v):
        return v * cos_n + pltpu.roll(v, half, 1) * sin_hi + pltpu.roll(v, LANES - half, 1) * sin_lo

    scale = NSA_DIM ** -0.5 * LOG2_E
    nq_all = _dot(xb, wnq_ref[...])
    for c in range(NSA_WIDTH // LANES):
        cols = slice(c * LANES, (c + 1) * LANES)
        q = nq_all[:, cols]
        nq_ref[:, cols] = (q * scale).astype(BF16)
        nqr_ref[:, cols] = (rope_n(q) * scale).astype(BF16)

    kv_all = _dot(xb, wkv_ref[...])

    def kv(i):
        return kv_all[:, i * KV_WIDTH:(i + 1) * KV_WIDTH]

    def split_groups(ref, v):
        for g in range(NSA_KV_GROUPS):
            ref[g] = v[:, g * NSA_DIM:(g + 1) * NSA_DIM].astype(BF16)

    kc_ref[...] = kv(0)
    vc_ref[...] = kv(1)
    split_groups(ks_ref, rope_n(kv(2)))
    split_groups(vs_ref, kv(3))
    split_groups(kw_ref, rope_n(kv(4)))
    split_groups(vw_ref, kv(5))

    gt = jax.nn.sigmoid(_dot_nt(wg_ref[...], xb))
    for g in range(NSA_KV_GROUPS):
        gate_ref[g] = gt[g * GATE_LANES:(g + 1) * GATE_LANES, :]


def _inproj(x2d, pos_col, w_in, row0, n):
    tm = 1024 if n % 1024 == 0 and row0 % 1024 == 0 else n
    blk0 = row0 // tm
    off = np.cumsum([0] + [RET_WIDTH] * 4 + [NSA_WIDTH] + [KV_WIDTH] * 6)
    w_ret = w_in[:, :off[4]].astype(BF16)
    w_nq = w_in[:, off[4]:off[5]].astype(BF16)
    w_kv = w_in[:, off[5]:off[11]].astype(BF16)
    wg = w_in[:, off[11]:].reshape(D_MODEL, NSA_KV_GROUPS, NSA_HPG * 3)
    wg = jnp.pad(wg, ((0, 0), (0, 0), (0, GATE_LANES - NSA_HPG * 3)))
    wg = wg.reshape(D_MODEL, NSA_KV_GROUPS * GATE_LANES).T.astype(BF16)

    lane = np.arange(LANES)
    half_r = RET_DIM // 2
    inv_r = (np.float32(RET_ROPE_BASE) ** (-np.arange(half_r, dtype=np.float32) / np.float32(half_r)))
    inv_r = inv_r.astype(np.float32)[lane % half_r][None, :]
    half_n = ROPE_DIMS // 2
    inv_n = (np.float32(ROPE_THETA) ** (-np.arange(half_n, dtype=np.float32) / np.float32(half_n)))
    jn = lane % NSA_DIM
    inv_n = np.where(jn < ROPE_DIMS, inv_n.astype(np.float32)[jn % half_n], np.float32(0.0))[None, :]

    row = lambda w: pl.BlockSpec((tm, w), lambda i: (i, 0))
    src_row = lambda w: pl.BlockSpec((tm, w), lambda i: (blk0 + i, 0))
    full = lambda a: pl.BlockSpec(a.shape, lambda i: (0,) * a.ndim)
    grp = lambda w: pl.BlockSpec((NSA_KV_GROUPS, tm, w), lambda i: (0, i, 0))
    bf = lambda w: jax.ShapeDtypeStruct((n, w), BF16)
    gbf = jax.ShapeDtypeStruct((NSA_KV_GROUPS, n, NSA_DIM), BF16)
    inv_r = jnp.asarray(inv_r, F32)
    inv_n = jnp.asarray(inv_n, F32)
    return pl.pallas_call(
        _inproj_kernel,
        grid=(n // tm,),
        in_specs=[src_row(D_MODEL), src_row(1), full(w_ret), full(w_nq), full(w_kv), full(wg),
                  full(inv_r), full(inv_n)],
        out_specs=[row(RET_WIDTH)] * 4 + [row(NSA_WIDTH)] * 2 + [row(KV_WIDTH)] * 2
                  + [grp(NSA_DIM)] * 4
                  + [pl.BlockSpec((NSA_KV_GROUPS, GATE_LANES, tm), lambda i: (0, 0, i))],
        out_shape=[bf(RET_WIDTH)] * 4 + [bf(NSA_WIDTH)] * 2
                  + [jax.ShapeDtypeStruct((n, KV_WIDTH), F32)] * 2 + [gbf] * 4
                  + [jax.ShapeDtypeStruct((NSA_KV_GROUPS, GATE_LANES, n), F32)],
        compiler_params=_params(1),
        name="inproj",
    )(x2d, pos_col, w_ret, w_nq, w_kv, wg, inv_r, inv_n)


def _retention_kernel(q_ref, k_ref, v_ref, g_ref, o_ref, state_ref):
    c = RET_CHUNK

    @pl.when(pl.program_id(1) == 0)
    def _():
        state_ref[...] = jnp.zeros_like(state_ref)

    row = lax.broadcasted_iota(jnp.int32, (c, c), 0)
    col = lax.broadcasted_iota(jnp.int32, (c, c), 1)
    rel = (row - col).astype(F32)
    idx = lax.broadcasted_iota(jnp.int32, (c, 1), 0).astype(F32)
    for h in range(RET_HEADS):
        log_g = float(np.log(np.float32(1.0) - np.float32(2.0) ** np.float32(-5.0 - h)))
        cols = slice(h * RET_DIM, (h + 1) * RET_DIM)
        dmask = jnp.where(rel >= 0, jnp.exp(log_g * jnp.maximum(rel, 0.0)), 0.0)
        zeta = jnp.exp(log_g * (c - 1.0 - idx))
        xi = jnp.exp(log_g * (idx + 1.0))
        for j in range(q_ref.shape[0] // c):
            rows = slice(j * c, (j + 1) * c)
            q = q_ref[rows, cols]
            k = k_ref[rows, cols]
            v = v_ref[rows, cols]
            scores = _dot_nt(q, k) * dmask
            inner = _dot(scores.astype(BF16), v)
            prev = state_ref[h]
            cross = _dot(q, prev.astype(BF16)) * xi
            kz = (k.astype(F32) * zeta).astype(BF16)
            kv = lax.dot_general(kz, v, (((0,), (0,)), ((), ())), preferred_element_type=F32)
            state_ref[h] = prev * float(np.exp(np.float32(log_g) * np.float32(c))) + kv
            o = inner + cross
            mu = jnp.mean(o, axis=-1, keepdims=True)
            d = o - mu
            var = jnp.mean(d * d, axis=-1, keepdims=True)
            o = d * lax.rsqrt(var + LN_EPS)
            o_ref[rows, cols] = (jax.nn.silu(g_ref[rows, cols].astype(F32)) * o).astype(BF16)


def _retention(rq, rk, rv, rg, batch, seq):
    per_step = RET_STEP_CHUNKS if (seq // RET_CHUNK) % RET_STEP_CHUNKS == 0 else 1
    nc = seq // (RET_CHUNK * per_step)
    spec = pl.BlockSpec((RET_CHUNK * per_step, RET_WIDTH), lambda b, n: (b * nc + n, 0))
    return pl.pallas_call(
        _retention_kernel,
        grid=(batch, nc),
        in_specs=[spec] * 4,
        out_specs=spec,
        out_shape=jax.ShapeDtypeStruct(rq.shape, BF16),
        scratch_shapes=[pltpu.VMEM((RET_HEADS, RET_DIM, RET_DIM), F32)],
        compiler_params=_params(2),
        name="retention",
    )(rq, rk, rv, rg)


def _compress_kernel(a_ref, pe_ref, w1_ref, w2_ref, o_ref, shift_ref, *, n_cmp):
    rows = a_ref.shape[0]
    a = a_ref[...]
    lo = (a + pe_ref[0]).astype(BF16)
    hi = (a + pe_ref[1]).astype(BF16)
    ridx = lax.broadcasted_iota(jnp.int32, (rows, 1), 0)
    shift_ref[rows:rows + 8, :] = jnp.zeros((8, CMP_HIDDEN), F32)
    for g in range(NSA_KV_GROUPS):
        p = _dot(lo, w1_ref[0, g])
        shift_ref[0:rows, :] = _dot(hi, w1_ref[1, g])
        hid = jax.nn.silu(p + shift_ref[pl.ds(1, rows), :])
        out = _dot(hid.astype(BF16), w2_ref[...])
        o_ref[g] = jnp.where(ridx < n_cmp, out, 0.0).astype(BF16)


def _compress(a, pe, w1, w2, batch, seq):
    rows = seq // CMP_STRIDE
    per = CMP_STRIDE * KV_WIDTH
    n_cmp = (seq - CMP_LEN) // CMP_STRIDE + 1
    a2 = a.reshape(batch * rows, per)
    pe2 = jnp.tile(pe.reshape(2, CMP_STRIDE, 1, NSA_DIM), (1, 1, NSA_KV_GROUPS, 1)).reshape(2, 1, per)
    w1r = w1.reshape(2, CMP_STRIDE, 1, NSA_DIM, CMP_HIDDEN)
    eye = jnp.eye(NSA_KV_GROUPS, dtype=w1.dtype).reshape(1, NSA_KV_GROUPS, 1, NSA_KV_GROUPS, 1, 1)
    w1x = (w1r[:, None] * eye).reshape(2, NSA_KV_GROUPS, per, CMP_HIDDEN).astype(BF16)
    w2b = w2.astype(BF16)
    full = lambda arr: pl.BlockSpec(arr.shape, lambda b: (0,) * arr.ndim)
    return pl.pallas_call(
        functools.partial(_compress_kernel, n_cmp=n_cmp),
        grid=(batch,),
        in_specs=[pl.BlockSpec((rows, per), lambda b: (b, 0)), full(pe2), full(w1x), full(w2b)],
        out_specs=pl.BlockSpec((None, NSA_KV_GROUPS, rows, NSA_DIM), lambda b: (b, 0, 0, 0)),
        out_shape=jax.ShapeDtypeStruct((batch, NSA_KV_GROUPS, rows, NSA_DIM), BF16),
        scratch_shapes=[pltpu.VMEM((rows + 8, CMP_HIDDEN), F32)],
        compiler_params=_params(1),
        name="compress",
    )(a2, pe2, w1x, w2b)


def _heads_to_lanes(ref):
    vt = ref[...].astype(F32).T
    return jnp.concatenate([vt[h * NSA_DIM:(h + 1) * NSA_DIM] for h in range(NSA_HPG)], axis=1).astype(BF16)


def _tile_heads(v):
    return jnp.concatenate([v] * NSA_HPG, axis=1)


def _transpose_into(dst_ref, src_ref, chunk):
    def step(c, _):
        c0 = pl.multiple_of(c * chunk, chunk)
        dst_ref[:NSA_DIM, pl.ds(c0, chunk)] = src_ref[pl.ds(c0, chunk), :].astype(F32).T.astype(BF16)
        return 0
    lax.fori_loop(0, src_ref.shape[0] // chunk, step, 0)


def _nsa_kernel(qraw_ref, qrot_ref, gate_ref, kcmp_ref, vcmp_ref, ovt_ref,
                ks_ref, vs_ref, kw_ref, vw_ref, o_ref, vst_ref, vwt_ref, vct_ref, bias_ref, *, tq, tk, seq):
    i = pl.program_id(2)
    t0 = i * tq
    cols = NSA_HPG * tq
    n_sel = seq // SEL_LEN
    n_cmp_rows = seq // CMP_STRIDE
    blocks_per_tile = tk // SEL_LEN

    @pl.when(i == 0)
    def _():
        chunk = min(512, n_cmp_rows)
        _transpose_into(vst_ref, vs_ref, chunk)
        _transpose_into(vwt_ref, vw_ref, chunk)
        _transpose_into(vct_ref, vcmp_ref, chunk)
        vst_ref[NSA_DIM:, :] = jnp.ones((SUM_ROWS, seq), BF16)
        vwt_ref[NSA_DIM:, :] = jnp.ones((SUM_ROWS, seq), BF16)

    def split_sum(acc):
        return acc[:NSA_DIM] / acc[NSA_DIM:NSA_DIM + 1]

    q_raw = _heads_to_lanes(qraw_ref)
    q_rot = _heads_to_lanes(qrot_ref)
    t_row = t0 + lax.broadcasted_iota(jnp.int32, (1, tq), 1)

    chain_w = cols // NSA_CHAINS
    heads_per_chain = chain_w // tq
    chains = [slice(c * chain_w, (c + 1) * chain_w) for c in range(NSA_CHAINS)]
    tile_chain = lambda v: jnp.concatenate([v] * heads_per_chain, axis=1)

    pw = min(tq, WIN_PART)
    parts = []
    for u in range(tq // pw):
        span = WIN + pw
        ws = pl.multiple_of(jnp.maximum(t0 + u * pw - WIN, 0), pw)
        dist = t_row[:, u * pw:(u + 1) * pw] - (ws + lax.broadcasted_iota(jnp.int32, (span, 1), 0))
        bias_w = jnp.concatenate([jnp.where((dist >= 0) & (dist < WIN), 0.0, NEG)] * NSA_HPG, axis=1)
        q_part = jnp.concatenate([q_rot[:, h * tq + u * pw:h * tq + (u + 1) * pw] for h in range(NSA_HPG)],
                                 axis=1)
        s_w = _dot(kw_ref[pl.ds(ws, span), :], q_part) + bias_w
        p_w = jnp.exp2((s_w - jnp.max(s_w, axis=0, keepdims=True)).astype(BF16))
        parts.append(split_sum(_dot(vwt_ref[:, pl.ds(ws, span)], p_w)))
    o_w = jnp.concatenate([parts[u][:, h * pw:(h + 1) * pw]
                           for h in range(NSA_HPG) for u in range(tq // pw)], axis=1)

    c_idx = lax.broadcasted_iota(jnp.int32, (n_cmp_rows, 1), 0)
    bias_c = tile_chain(jnp.where(c_idx * CMP_STRIDE + (CMP_LEN - 1) <= t_row, 0.0, NEG))
    sees_any = tile_chain(jnp.where(t_row >= CMP_LEN - 1, 1.0, 0.0))
    o_c = []
    p_sum = None
    for c in chains:
        s_c = _dot(kcmp_ref[...], q_raw[:, c]) + bias_c
        e_c = jnp.exp2(s_c - jnp.max(s_c, axis=0, keepdims=True))
        p_c = e_c * (sees_any / jnp.sum(e_c, axis=0, keepdims=True))
        o_c.append(_dot(vct_ref[...], p_c.astype(BF16)))
        for h in range(heads_per_chain):
            p_h = p_c[:, h * tq:(h + 1) * tq]
            p_sum = p_h if p_sum is None else p_sum + p_h
    o_c = jnp.concatenate(o_c, axis=1)

    p_hi = p_sum.astype(BF16)
    p_lo = (p_sum - p_hi.astype(F32)).astype(BF16)
    ovt = ovt_ref[...]
    imp = _dot(ovt, p_hi) + _dot(ovt, p_lo)
    jb = lax.broadcasted_iota(jnp.int32, (n_sel, tq), 0)
    cur = (t0 + lax.broadcasted_iota(jnp.int32, (n_sel, tq), 1)) >> SEL_SHIFT
    forced = (jb == 0) | (jb == cur) | (jb == cur - 1)
    work = jnp.where(forced, -jnp.inf, imp)
    work = jnp.where(jb <= cur, work, NEG)
    sel_t = jnp.where(forced, 1.0, 0.0)
    for _ in range(max(min(SEL_TOPK, n_sel) - N_FORCED, 0)):
        best = jnp.max(work, axis=0, keepdims=True)
        first = jnp.min(jnp.where(work == best, jb, n_sel), axis=0, keepdims=True)
        hit = jb == first
        sel_t = jnp.where(hit, 1.0, sel_t)
        work = jnp.where(hit, -jnp.inf, work)
    bias_ref[...] = jnp.where(sel_t > 0.5, 0.0, NEG)

    def sel_tile(kt, carry, causal):
        k0 = pl.multiple_of(kt * tk, tk)
        bias = jnp.concatenate(
            [jnp.broadcast_to(bias_ref[pl.ds(kt * blocks_per_tile + j, 1), :], (SEL_LEN, tq))
             for j in range(blocks_per_tile)], axis=0)
        if causal:
            kpos = k0 + lax.broadcasted_iota(jnp.int32, (tk, 1), 0)
            bias = jnp.where(kpos <= t_row, bias, NEG)
        bias = tile_chain(bias)
        k_t = ks_ref[pl.ds(k0, tk), :]
        v_t = vst_ref[:, pl.ds(k0, tk)]
        out = []
        scores = [_dot(k_t, q_rot[:, c]) + bias for c in chains]
        for (m, acc), s in zip(carry, scores):
            m_new = jnp.maximum(m, jnp.max(s, axis=0, keepdims=True))
            p = jnp.exp2((s - m_new).astype(BF16))
            acc = jnp.exp2(m - m_new) * acc + _dot(v_t, p)
            out.append((m_new, acc))
        return tuple(out)

    n_full = t0 // tk
    init = tuple((jnp.full((1, chain_w), NEG, F32), jnp.zeros((NSA_DIM + SUM_ROWS, chain_w), F32))
                 for _ in chains)
    carry = lax.fori_loop(0, n_full, functools.partial(sel_tile, causal=False), init)
    for d in range(max(tq // tk, 1)):
        carry = sel_tile(n_full + d, carry, causal=True)
    o_s = jnp.concatenate([split_sum(acc) for _, acc in carry], axis=1)

    gt = gate_ref[...]
    outs = []
    for h in range(NSA_HPG):
        c = slice(h * tq, (h + 1) * tq)
        outs.append(gt[3 * h:3 * h + 1] * o_c[:, c] + gt[3 * h + 1:3 * h + 2] * o_s[:, c]
                    + gt[3 * h + 2:3 * h + 3] * o_w[:, c])
    o_ref[...] = jnp.concatenate(outs, axis=0).T.astype(BF16)


def _anchored(kernel_fn, n_inputs, n_anchors):
    def body(*refs, **static):
        kernel_fn(*refs[:n_inputs], *refs[n_inputs + n_anchors:], **static)
    return body


def _add_anchors(kernel_fn, args, in_specs, after):
    after = [a for a in (after or ()) if a is not None]
    if not after:
        return kernel_fn
    body = _anchored(kernel_fn, len(args), len(after))
    args.extend(after)
    in_specs.extend([pl.BlockSpec(memory_space=pl.ANY)] * len(after))
    return body


def _nsa(nq, nqr, gates, kcmp, vcmp, ks, vs, kw, vw, batch, seq, after=None):
    n = batch * seq
    tq = 512
    tk = 512 if seq % 512 == 0 else seq
    nqb = seq // tq
    n_sel = seq // SEL_LEN
    rows_c = seq // CMP_STRIDE
    gw = NSA_HPG * NSA_DIM
    cs = np.arange(rows_c)[None, :] * CMP_STRIDE
    ss = np.arange(n_sel)[:, None] * SEL_LEN
    n_cmp = (seq - CMP_LEN) // CMP_STRIDE + 1
    ovt = ((cs < ss + SEL_LEN) & (cs + CMP_LEN > ss) & (np.arange(rows_c)[None, :] < n_cmp))
    ovt = jnp.asarray(ovt.astype(np.float32), BF16)

    qspec = pl.BlockSpec((tq, gw), lambda b, g, i: (b * nqb + i, g))
    cspec = pl.BlockSpec((None, None, rows_c, NSA_DIM), lambda b, g, i: (b, g, 0, 0))
    kspec = pl.BlockSpec((None, seq, NSA_DIM), lambda b, g, i: (g, b, 0))
    args = [nq, nqr, gates, kcmp, vcmp, ovt, ks, vs, kw, vw]
    in_specs = [qspec, qspec,
                pl.BlockSpec((None, GATE_LANES, tq), lambda b, g, i: (g, 0, b * nqb + i)),
                cspec, cspec, pl.BlockSpec(ovt.shape, lambda b, g, i: (0, 0)),
                kspec, kspec, kspec, kspec]
    body = functools.partial(_add_anchors(_nsa_kernel, args, in_specs, after), tq=tq, tk=tk, seq=seq)
    return pl.pallas_call(
        body,
        grid=(batch, NSA_KV_GROUPS, nqb),
        in_specs=in_specs,
        out_specs=qspec,
        out_shape=jax.ShapeDtypeStruct((n, NSA_WIDTH), BF16),
        scratch_shapes=[pltpu.VMEM((NSA_DIM + SUM_ROWS, seq), BF16), pltpu.VMEM((NSA_DIM + SUM_ROWS, seq), BF16),
                        pltpu.VMEM((NSA_DIM, rows_c), BF16), pltpu.VMEM((n_sel, tq), F32)],
        compiler_params=_params(3),
        name="nsa",
    )(*args)


def _memkv_kernel(mem_ref, w_ref, kv_ref):
    kv_ref[...] = _dot(mem_ref[...].astype(BF16), w_ref[...]).astype(BF16)


def _memkv(mem2d, w_xkv):
    n = mem2d.shape[0]
    w = w_xkv.astype(BF16)
    return pl.pallas_call(
        _memkv_kernel,
        grid=(n // MEM_LEN,),
        in_specs=[pl.BlockSpec((MEM_LEN, D_MODEL), lambda i: (i, 0)),
                  pl.BlockSpec(w.shape, lambda i: (0, 0))],
        out_specs=pl.BlockSpec((MEM_LEN, 2 * D_MODEL), lambda i: (i, 0)),
        out_shape=jax.ShapeDtypeStruct((n, 2 * D_MODEL), BF16),
        compiler_params=_params(1),
        name="memkv",
    )(mem2d, w)


def _pack_halves(v):
    half = D_MODEL // 2
    hi = pltpu.bitcast(v[:, :half].astype(BF16).astype(F32), jnp.uint32)
    lo = pltpu.bitcast(v[:, half:].astype(BF16).astype(F32), jnp.uint32)
    return hi | (lo >> 16)


def _unpack_halves(words):
    return pltpu.bitcast(words & jnp.uint32(0xFFFF0000), F32), pltpu.bitcast(words << 16, F32)


def _postmix_kernel(x_ref, oret_ref, onsa_ref, kv_ref, wout_ref, wq_ref, wo_ref,
                    g1_ref, b1_ref, g2_ref, b2_ref, x2_ref, x2p_ref):
    mixed = jnp.concatenate([oret_ref[...], onsa_ref[...]], axis=1)
    x1 = _layer_norm(DN_ALPHA * x_ref[...] + _dot(mixed, wout_ref[...]), g1_ref[...], b1_ref[...])
    q = (_dot(x1.astype(BF16), wq_ref[...]) * (XATT_DIM ** -0.5)).astype(BF16)
    heads = []
    for h in range(XATT_HEADS):
        cols = slice(h * XATT_DIM, (h + 1) * XATT_DIM)
        s = _dot_nt(q[:, cols], kv_ref[:, cols])
        m = jnp.max(s, axis=-1, keepdims=True)
        p = jnp.exp(s - m)
        l = jnp.sum(p, axis=-1, keepdims=True)
        heads.append(_dot(p.astype(BF16), kv_ref[:, D_MODEL + h * XATT_DIM:D_MODEL + (h + 1) * XATT_DIM]) / l)
    att = jnp.concatenate(heads, axis=1).astype(BF16)
    x2 = _layer_norm(DN_ALPHA * x1 + _dot(att, wo_ref[...]), g2_ref[...], b2_ref[...])
    x2_ref[...] = x2
    x2p_ref[...] = _pack_halves(x2)


def _postmix(x2d, o_ret, o_nsa, kvx, w_out, w_xq, w_xo, ln1_g, ln1_b, ln2_g, ln2_b, batch0, batch, seq,
             after=None):
    n = batch * seq
    tm = 512 if seq % 512 == 0 else seq
    per_b = seq // tm
    row = lambda w: pl.BlockSpec((tm, w), lambda b, i: (b * per_b + i, 0))
    full = lambda a: pl.BlockSpec(a.shape, lambda b, i: (0,) * a.ndim)
    ws = [w_out.astype(BF16), w_xq.astype(BF16), w_xo.astype(BF16)]
    vecs = [v.reshape(1, D_MODEL) for v in (ln1_g, ln1_b, ln2_g, ln2_b)]
    args = [x2d, o_ret, o_nsa, kvx, *ws, *vecs]
    in_specs = ([pl.BlockSpec((tm, D_MODEL), lambda b, i: ((batch0 + b) * per_b + i, 0)),
                 row(RET_WIDTH), row(NSA_WIDTH),
                 pl.BlockSpec((MEM_LEN, 2 * D_MODEL), lambda b, i: (batch0 + b, 0))]
                + [full(w) for w in ws] + [full(v) for v in vecs])
    return pl.pallas_call(
        _add_anchors(_postmix_kernel, args, in_specs, after),
        grid=(batch, per_b),
        in_specs=in_specs,
        out_specs=[row(D_MODEL),
                   row(D_MODEL // 2)],
        out_shape=[jax.ShapeDtypeStruct((n, D_MODEL), F32),
                   jax.ShapeDtypeStruct((n, D_MODEL // 2), jnp.uint32)],
        compiler_params=_params(2),
        name="postmix",
    )(*args)


def _router_kernel(x_ref, wr_ref, bias_ref, e_ref, rank_ref, w_ref, cnt_ref, cntrow_ref, carry_ref, carryrow_ref):
    tn = x_ref.shape[0]
    per = N_EXPERTS // N_GROUPS

    @pl.when(pl.program_id(0) == 0)
    def _():
        carry_ref[...] = jnp.zeros_like(carry_ref)
        carryrow_ref[...] = jnp.zeros_like(carryrow_ref)

    logits = _dot_nt(wr_ref[...], x_ref[...].astype(BF16))
    scores = jax.nn.sigmoid(logits)
    biased = scores + bias_ref[...]
    b3 = biased.reshape(N_GROUPS, per, tn)
    member = lax.broadcasted_iota(jnp.int32, (N_GROUPS, per, tn), 1)
    top1 = jnp.max(b3, axis=1, keepdims=True)
    first1 = jnp.min(jnp.where(b3 == top1, member, per), axis=1, keepdims=True)
    top2 = jnp.max(jnp.where(member == first1, -jnp.inf, b3), axis=1, keepdims=True)
    gscore = top1 + top2
    gidx = lax.broadcasted_iota(jnp.int32, (N_GROUPS, 1, tn), 0)
    gwork = gscore
    for _ in range(TOPK_GROUPS - 1):
        gbest = jnp.max(gwork, axis=0, keepdims=True)
        gfirst = jnp.min(jnp.where(gwork == gbest, gidx, N_GROUPS), axis=0, keepdims=True)
        gwork = jnp.where(gidx == gfirst, -jnp.inf, gwork)
    kth = jnp.max(gwork, axis=0, keepdims=True)
    work = jnp.where(gscore >= kth, b3, NEG).reshape(N_EXPERTS, tn)
    eidx = lax.broadcasted_iota(jnp.int32, (N_EXPERTS, tn), 0)
    picks = []
    chosen = jnp.zeros((N_EXPERTS, tn), F32)
    for _ in range(TOP_K):
        best = jnp.max(work, axis=0, keepdims=True)
        first = jnp.min(jnp.where(work == best, eidx, N_EXPERTS), axis=0, keepdims=True)
        hit = eidx == first
        picks.append((first, hit))
        chosen = jnp.where(hit, 1.0, chosen)
        work = jnp.where(hit, -jnp.inf, work)

    r_i = lax.broadcasted_iota(jnp.int32, (tn, tn), 0)
    c_i = lax.broadcasted_iota(jnp.int32, (tn, tn), 1)
    before = jnp.where(r_i < c_i, 1.0, 0.0).astype(BF16)
    chosen_b = chosen.astype(BF16)
    rank = _dot(chosen_b, before) + carry_ref[...]
    carry_ref[...] = carry_ref[...] + jnp.sum(chosen, axis=1, keepdims=True)
    carryrow_ref[...] = carryrow_ref[...] + _dot_nt(jnp.ones((8, tn), BF16), chosen_b)
    cnt_ref[...] = carry_ref[...]
    cntrow_ref[...] = carryrow_ref[...]

    wsel = [jnp.sum(jnp.where(hit, scores, 0.0), axis=0, keepdims=True) for _, hit in picks]
    wsum = wsel[0]
    for v in wsel[1:]:
        wsum = wsum + v
    for kk, (first, hit) in enumerate(picks):
        e_ref[kk:kk + 1, :] = first
        rank_ref[kk:kk + 1, :] = jnp.sum(jnp.where(hit, rank, 0.0), axis=0, keepdims=True).astype(jnp.int32)
        w_ref[kk:kk + 1, :] = wsel[kk] / wsum * ROUTED_SCALE


def _router(x2, w_router, router_bias):
    n = x2.shape[0]
    tn = 512 if n % 512 == 0 else n
    wr_t = w_router.T.astype(BF16)
    bias = router_bias.reshape(N_EXPERTS, 1).astype(F32)
    kspec = pl.BlockSpec((TOP_K, tn), lambda i: (0, i))
    return pl.pallas_call(
        _router_kernel,
        grid=(n // tn,),
        in_specs=[pl.BlockSpec((tn, D_MODEL), lambda i: (i, 0)),
                  pl.BlockSpec(wr_t.shape, lambda i: (0, 0)),
                  pl.BlockSpec(bias.shape, lambda i: (0, 0))],
        out_specs=[kspec, kspec, kspec, pl.BlockSpec((N_EXPERTS, 1), lambda i: (0, 0)),
                   pl.BlockSpec((8, N_EXPERTS), lambda i: (0, 0))],
        out_shape=[jax.ShapeDtypeStruct((TOP_K, n), jnp.int32),
                   jax.ShapeDtypeStruct((TOP_K, n), jnp.int32),
                   jax.ShapeDtypeStruct((TOP_K, n), F32),
                   jax.ShapeDtypeStruct((N_EXPERTS, 1), F32),
                   jax.ShapeDtypeStruct((8, N_EXPERTS), F32)],
        scratch_shapes=[pltpu.VMEM((N_EXPERTS, 1), F32), pltpu.VMEM((8, N_EXPERTS), F32)],
        compiler_params=_params(1),
        name="router",
    )(x2, wr_t, bias)


def _slots_kernel(e_ref, rank_ref, cnt_ref, cntrow_ref, dest_ref, blk_e_ref, valid_ref, used_ref,
                  *, blk, n_blocks):
    pad = lambda c: jnp.ceil(c / blk) * blk
    cnt = cnt_ref[...]
    padded = pad(cnt)
    padded_row = pad(cntrow_ref[0:1, :])
    r_i = lax.broadcasted_iota(jnp.int32, (N_EXPERTS, N_EXPERTS), 0)
    c_i = lax.broadcasted_iota(jnp.int32, (N_EXPERTS, N_EXPERTS), 1)
    start = jnp.sum(jnp.where(c_i < r_i, padded_row, 0.0), axis=1, keepdims=True)
    end = start + padded
    e = e_ref[...]
    dest = rank_ref[...]
    for ex in range(N_EXPERTS):
        dest = dest + jnp.where(e == ex, start[ex:ex + 1, :].astype(jnp.int32), 0)
    dest_ref[...] = dest
    bstart = (lax.broadcasted_iota(jnp.int32, (1, n_blocks), 1) * blk).astype(F32)
    owner = jnp.sum(jnp.where(end <= bstart, 1.0, 0.0), axis=0, keepdims=True)
    blk_e_ref[...] = jnp.minimum(owner, N_EXPERTS - 1.0).astype(jnp.int32)
    inside = (start <= bstart) & (bstart < end)
    real = jnp.clip(start + cnt - bstart, 0.0, float(blk))
    valid = jnp.sum(jnp.where(inside, real, 0.0), axis=0, keepdims=True)
    valid_ref[...] = valid.astype(jnp.int32)
    used_ref[...] = jnp.sum(jnp.where(valid > 0.0, 1.0, 0.0), axis=1, keepdims=True).astype(jnp.int32)


def _slots(e_k, rank_k, counts, counts_row, blk, n_blocks):
    n = e_k.shape[1]
    full = lambda shape: pl.BlockSpec(shape, lambda: (0,) * len(shape))
    return pl.pallas_call(
        functools.partial(_slots_kernel, blk=blk, n_blocks=n_blocks),
        in_specs=[full((TOP_K, n)), full((TOP_K, n)), full((N_EXPERTS, 1)), full((8, N_EXPERTS))],
        out_specs=[full((TOP_K, n)), full((1, n_blocks)), full((1, n_blocks)), full((1, 1))],
        out_shape=[jax.ShapeDtypeStruct((TOP_K, n), jnp.int32),
                   jax.ShapeDtypeStruct((1, n_blocks), jnp.int32),
                   jax.ShapeDtypeStruct((1, n_blocks), jnp.int32),
                   jax.ShapeDtypeStruct((1, 1), jnp.int32)],
        compiler_params=pltpu.CompilerParams(vmem_limit_bytes=VMEM_LIMIT),
        name="slots",
    )(e_k, rank_k, counts, counts_row)


def _sc_worker_base(per_worker):
    return (lax.axis_index("s") * SC_CORES + lax.axis_index("c")) * per_worker


def _sc_scatter_rows(rows, idx, n_out):
    n, width = rows.shape
    k_lists = idx.shape[0] // n
    workers = SC_CORES * SC_SUBCORES
    per_worker = n // workers
    assert per_worker * workers == n and per_worker % (SC_CHUNK * SC_INFLIGHT) == 0
    mesh = plsc.VectorSubcoreMesh(core_axis_name="c", subcore_axis_name="s")
    lanes = range(SC_INFLIGHT)

    @functools.partial(
        pl.kernel, mesh=mesh,
        out_type=jax.ShapeDtypeStruct((n_out, width), rows.dtype),
        scratch_types=[pltpu.VMEM((SC_CHUNK, width), rows.dtype)] * SC_INFLIGHT
                      + [pltpu.VMEM((SC_CHUNK,), jnp.int32)] * (k_lists * SC_INFLIGHT)
                      + [pltpu.SemaphoreType.DMA] * (2 * SC_INFLIGHT),
        name="sc_scatter")
    def scatter(rows_hbm, idx_hbm, out_hbm, *scratch):
        rows_vs = scratch[:SC_INFLIGHT]
        idx_vs = [scratch[SC_INFLIGHT + j * k_lists:SC_INFLIGHT + (j + 1) * k_lists] for j in lanes]
        sems = scratch[SC_INFLIGHT * (1 + k_lists):]
        sem_in, sem_out = sems[:SC_INFLIGHT], sems[SC_INFLIGHT:]
        base = _sc_worker_base(per_worker)

        @pl.loop(0, per_worker // (SC_CHUNK * SC_INFLIGHT))
        def _(gi):
            offs = [pl.multiple_of(base + (gi * SC_INFLIGHT + j) * SC_CHUNK, SC_CHUNK) for j in lanes]
            loads = []
            for j in lanes:
                loads.append([pltpu.async_copy(rows_hbm.at[pl.ds(offs[j], SC_CHUNK)], rows_vs[j], sem_in[j])]
                             + [pltpu.async_copy(
                                 idx_hbm.at[pl.ds(pl.multiple_of(k * n + offs[j], SC_CHUNK), SC_CHUNK)],
                                 idx_vs[j][k], sem_in[j]) for k in range(k_lists)])
            copies = []
            for j in lanes:
                for c in loads[j]:
                    c.wait()
                copies.append([pltpu.async_copy(rows_vs[j], out_hbm.at[idx_vs[j][k]], sem_out[j])
                               for k in range(k_lists)])
            for j in lanes:
                for c in copies[j]:
                    c.wait()

    return scatter(rows, idx)


def _experts_kernel(blk_e_ref, valid_ref, used_ref, xs_ref, wg_ref, wu_ref, wd_ref, y_ref, wg_b, wu_b, wd_b):
    del used_ref
    i = pl.program_id(0)
    valid = valid_ref[i]

    @pl.when((i == 0) | (blk_e_ref[i] != blk_e_ref[jnp.maximum(i - 1, 0)]))
    def _():
        wg_b[...] = wg_ref[...].astype(BF16)
        wu_b[...] = wu_ref[...].astype(BF16)
        wd_b[...] = wd_ref[...].astype(BF16)

    @pl.when(valid > 0)
    def _():
        half = D_MODEL // 2
        row = lax.broadcasted_iota(jnp.int32, (xs_ref.shape[0], 1), 0)
        hi, lo = (v.astype(BF16) for v in _unpack_halves(jnp.where(row < valid, xs_ref[...], jnp.uint32(0))))
        gate = _dot(hi, wg_b[:half, :]) + _dot(lo, wg_b[half:, :])
        up = _dot(hi, wu_b[:half, :]) + _dot(lo, wu_b[half:, :])
        y_ref[...] = _pack_halves(_dot((jax.nn.silu(gate) * up).astype(BF16), wd_b[...]))

    @pl.when(valid <= 0)
    def _():
        y_ref[...] = jnp.zeros_like(y_ref)


def _experts(blk_e, valid, n_used, xs, w_gate, w_up, w_down, blk, after=None):
    cap, width = xs.shape
    wspec = lambda a: pl.BlockSpec((None,) + a.shape[1:], lambda i, be, nv, nu: (be[i], 0, 0))
    rows = pl.BlockSpec((blk, width), lambda i, be, nv, nu: (jnp.minimum(i, nu[0]), 0))
    args = [blk_e, valid, n_used, xs, w_gate, w_up, w_down]
    in_specs = [rows, wspec(w_gate), wspec(w_up), wspec(w_down)]
    body = _add_anchors(_experts_kernel, args, in_specs, after)
    return pl.pallas_call(
        body,
        grid_spec=pltpu.PrefetchScalarGridSpec(
            num_scalar_prefetch=3,
            grid=(cap // blk,),
            in_specs=in_specs,
            out_specs=rows,
            scratch_shapes=[pltpu.VMEM(w.shape[1:], BF16) for w in (w_gate, w_up, w_down)],
        ),
        out_shape=jax.ShapeDtypeStruct(xs.shape, xs.dtype),
        compiler_params=_params(1),
        name="experts",
    )(*args)


def _sc_gather_rows(table, idx):
    b, width = idx.shape[0], table.shape[1]
    workers = SC_CORES * SC_SUBCORES
    per_worker = b // workers
    assert per_worker * workers == b and per_worker % (SC_CHUNK * SC_INFLIGHT) == 0
    mesh = plsc.VectorSubcoreMesh(core_axis_name="c", subcore_axis_name="s")

    @functools.partial(
        pl.kernel, mesh=mesh,
        out_type=jax.ShapeDtypeStruct((b, width), table.dtype),
        scratch_types=[pltpu.VMEM((SC_CHUNK,), jnp.int32)] * SC_INFLIGHT
                      + [pltpu.VMEM((SC_CHUNK, width), table.dtype)] * SC_INFLIGHT
                      + [pltpu.SemaphoreType.DMA] * (1 + 2 * SC_INFLIGHT),
        name="sc_gather")
    def gather(table_hbm, idx_hbm, out_hbm, *scratch):
        idx_vs = scratch[:SC_INFLIGHT]
        rows_vs = scratch[SC_INFLIGHT:2 * SC_INFLIGHT]
        sem_idx = scratch[2 * SC_INFLIGHT]
        sem_rows = scratch[2 * SC_INFLIGHT + 1:3 * SC_INFLIGHT + 1]
        sem_out = scratch[3 * SC_INFLIGHT + 1:]
        base = _sc_worker_base(per_worker)
        lanes = range(SC_INFLIGHT)

        @pl.loop(0, per_worker // (SC_CHUNK * SC_INFLIGHT))
        def _(gi):
            offs = [pl.multiple_of(base + (gi * SC_INFLIGHT + j) * SC_CHUNK, SC_CHUNK) for j in lanes]
            loads = [pltpu.async_copy(idx_hbm.at[pl.ds(offs[j], SC_CHUNK)], idx_vs[j], sem_idx) for j in lanes]
            for c in loads:
                c.wait()
            gathers = [pltpu.async_copy(table_hbm.at[idx_vs[j]], rows_vs[j], sem_rows[j]) for j in lanes]
            writes = []
            for j in lanes:
                gathers[j].wait()
                writes.append(pltpu.async_copy(rows_vs[j], out_hbm.at[pl.ds(offs[j], SC_CHUNK)], sem_out[j]))
            for c in writes:
                c.wait()

    return gather(table, idx)


def _combine_kernel(x_ref, wk_ref, yk_ref, wsg_ref, wsu_ref, wsd_ref, g_ref, b_ref, *rest):
    o_ref = rest[-1]
    x = x_ref[...]
    xb = x.astype(BF16)
    shared = _dot((jax.nn.silu(_dot(xb, wsg_ref[...])) * _dot(xb, wsu_ref[...])).astype(BF16), wsd_ref[...])
    wk = wk_ref[...]
    routed_hi = routed_lo = None
    for kk in range(TOP_K):
        hi, lo = _unpack_halves(yk_ref[kk])
        w = wk[:, kk:kk + 1]
        routed_hi = hi * w if kk == 0 else routed_hi + hi * w
        routed_lo = lo * w if kk == 0 else routed_lo + lo * w
    routed = jnp.concatenate([routed_hi, routed_lo], axis=1)
    o_ref[...] = _layer_norm(DN_ALPHA * x + (routed + shared), g_ref[...], b_ref[...])


def _combine(x2, w_tok, yk, ws_gate, ws_up, ws_down, ln3_g, ln3_b, row0, n_total, out_prev, after=None,
             part=(0, 1)):
    n = x2.shape[0] // part[1]
    sub0 = part[0] * n
    tt = 512 if n % 512 == 0 and row0 % 512 == 0 else n
    blk0 = (row0 + sub0) // tt
    sub_blk = sub0 // tt
    ws = [ws_gate.astype(BF16), ws_up.astype(BF16), ws_down.astype(BF16)]
    vecs = [ln3_g.reshape(1, D_MODEL), ln3_b.reshape(1, D_MODEL)]
    full = lambda a: pl.BlockSpec(a.shape, lambda i: (0,) * a.ndim)
    args = [x2, w_tok, yk, *ws, *vecs]
    in_specs = ([pl.BlockSpec((tt, D_MODEL), lambda i: (sub_blk + i, 0)),
                 pl.BlockSpec((tt, TOP_K), lambda i: (sub_blk + i, 0)),
                 pl.BlockSpec((TOP_K, tt, D_MODEL // 2), lambda i: (0, sub_blk + i, 0))]
                + [full(a) for a in ws] + [full(v) for v in vecs])
    aliases = {}
    for anchor in (a for a in (after or ()) if a is not None):
        args.append(anchor)
        in_specs.append(pl.BlockSpec(memory_space=pl.ANY))
    if out_prev is not None:
        aliases = {len(args): 0}
        args.append(out_prev)
        in_specs.append(pl.BlockSpec(memory_space=pl.ANY))
    return pl.pallas_call(
        _combine_kernel,
        grid=(n // tt,),
        in_specs=in_specs,
        out_specs=pl.BlockSpec((tt, D_MODEL), lambda i: (blk0 + i, 0)),
        out_shape=jax.ShapeDtypeStruct((n_total, D_MODEL), F32),
        input_output_aliases=aliases,
        compiler_params=_params(1),
        name="combine",
    )(*args)


EXPERT_BLOCK = 1024


def _moe_dispatch(x2, x2p, w_router, router_bias):
    n = x2.shape[0]
    cap = n * TOP_K + N_EXPERTS * EXPERT_BLOCK
    e_k, rank_k, w_k, counts, counts_row = _router(x2, w_router, router_bias)
    dest, blk_e, valid, n_used = _slots(e_k, rank_k, counts, counts_row, EXPERT_BLOCK, cap // EXPERT_BLOCK)
    dest = dest.reshape(-1)
    return dict(w_tok=w_k.T, dest=dest, blk_e=blk_e.reshape(-1), valid=valid.reshape(-1),
                n_used=n_used.reshape(-1), xs=_sc_scatter_rows(x2p, dest, cap))


def _moe_experts(routed, w_gate, w_up, w_down, after):
    y = _experts(routed["blk_e"], routed["valid"], routed["n_used"], routed["xs"], w_gate, w_up, w_down,
                 EXPERT_BLOCK, after=after)
    n = routed["dest"].shape[0] // TOP_K
    return y, _sc_gather_rows(y, routed["dest"]).reshape(TOP_K, n, D_MODEL // 2)


def _layer(x, mem, positions, w_in, cmp_pe_k, cmp_pe_v, cmp_w1_k, cmp_w2_k, cmp_w1_v, cmp_w2_v,
           w_out, ln1_g, ln1_b, w_xq, w_xkv, w_xo, ln2_g, ln2_b, w_router, router_bias,
           w_gate, w_up, w_down, ws_gate, ws_up, ws_down, ln3_g, ln3_b):
    batch, seq, _ = x.shape
    n_total = batch * seq
    x2d = x.reshape(n_total, D_MODEL)
    pos_col = positions.astype(F32).reshape(n_total, 1)
    kvx = _memkv(mem.reshape(batch * MEM_LEN, D_MODEL), w_xkv)
    last = max(1, batch // 4)
    sizes = [batch - last, last] if batch > 1 else [batch]
    starts = [sum(sizes[:g]) for g in range(len(sizes))]

    def mixers_in(g):
        nb, row0 = sizes[g], starts[g] * seq
        (rq, rk, rv, rg, nq, nqr, kc, vc, ks, vs, kw, vw, gates) = _inproj(x2d, pos_col, w_in, row0, nb * seq)
        o_ret = _retention(rq, rk, rv, rg, nb, seq)
        kcmp = _compress(kc, cmp_pe_k, cmp_w1_k, cmp_w2_k, nb, seq)
        vcmp = _compress(vc, cmp_pe_v, cmp_w1_v, cmp_w2_v, nb, seq)
        return o_ret, (nq, nqr, gates, kcmp, vcmp, ks, vs, kw, vw)

    def attend(g, nsa_args, after):
        return _nsa(*nsa_args, sizes[g], seq, after=after)

    def mix_and_route(g, o_ret, o_nsa, after):
        x2, x2p = _postmix(x2d, o_ret, o_nsa, kvx, w_out, w_xq, w_xo, ln1_g, ln1_b, ln2_g, ln2_b,
                           starts[g], sizes[g], seq, after=after)
        return x2, _moe_dispatch(x2, x2p, w_router, router_bias)

    def combine(g, x2, routed, yk, out_prev, after=None, part=(0, 1)):
        return _combine(x2, routed["w_tok"], yk, ws_gate, ws_up, ws_down, ln3_g, ln3_b, starts[g] * seq,
                        n_total, out_prev, after=after, part=part)

    o_ret, nsa_args = mixers_in(0)
    x2, routed = mix_and_route(0, o_ret, attend(0, nsa_args, None), None)
    if len(sizes) == 1:
        y, yk = _moe_experts(routed, w_gate, w_up, w_down, after=None)
        return combine(0, x2, routed, yk, None).reshape(batch, seq, D_MODEL)
    o_ret1, nsa_args1 = mixers_in(1)
    y, yk = _moe_experts(routed, w_gate, w_up, w_down, after=[o_ret1])
    o_nsa1 = attend(1, nsa_args1, [y])
    x2_1, routed1 = mix_and_route(1, o_ret1, o_nsa1, [yk])
    out = combine(0, x2, routed, yk, None, [routed1["dest"]], part=(0, 2))
    y1, yk1 = _moe_experts(routed1, w_gate, w_up, w_down, after=[out])
    out = combine(0, x2, routed, yk, out, [y1], part=(1, 2))
    out = combine(1, x2_1, routed1, yk1, out)
    return out.reshape(batch, seq, D_MODEL)


def kernel(x, mem, positions, w_in, cmp_pe_k, cmp_pe_v, cmp_w1_k, cmp_w2_k, cmp_w1_v, cmp_w2_v, w_out, ln1_g, ln1_b, w_xq, w_xkv, w_xo, ln2_g, ln2_b, w_router, router_bias, w_gate, w_up, w_down, ws_gate, ws_up, ws_down, ln3_g, ln3_b):
    for l in range(DEPTH):
        x = _layer(x, mem, positions, w_in[l], cmp_pe_k[l], cmp_pe_v[l], cmp_w1_k[l], cmp_w2_k[l],
                   cmp_w1_v[l], cmp_w2_v[l], w_out[l], ln1_g[l], ln1_b[l], w_xq[l], w_xkv[l],
                   w_xo[l], ln2_g[l], ln2_b[l], w_router[l], router_bias[l], w_gate[l], w_up[l],
                   w_down[l], ws_gate[l], ws_up[l], ws_down[l], ln3_g[l], ln3_b[l])
    return x
```

```python
import functools

import numpy as np
import jax
import jax.numpy as jnp
from jax import lax
from jax.experimental import pallas as pl
from jax.experimental.pallas import tpu as pltpu
from jax.experimental.pallas import tpu_sc as plsc

D_MODEL = 1024
MEM_LEN = 256
DEPTH = 1
DN_ALPHA = (2 * DEPTH) ** 0.25
LN_EPS = 1e-5
NEG = -1e30

RET_HEADS = 4
RET_DIM = 128
RET_CHUNK = 128
RET_ROPE_BASE = 10000.0
RET_STEP_CHUNKS = 8
RET_WIDTH = RET_HEADS * RET_DIM

NSA_HEADS = 8
NSA_KV_GROUPS = 2
NSA_HPG = NSA_HEADS // NSA_KV_GROUPS
NSA_DIM = 64
NSA_WIDTH = NSA_HEADS * NSA_DIM
KV_WIDTH = NSA_KV_GROUPS * NSA_DIM
CMP_LEN = 32
CMP_STRIDE = 16
CMP_HIDDEN = 256
SEL_LEN = 64
SEL_SHIFT = 6
SEL_TOPK = 16
N_FORCED = 3
WIN = 512
ROPE_THETA = 500000.0
ROPE_DIMS = NSA_DIM // 4
GATE_LANES = 16
NSA_CHAINS = 1
LOG2_E = 1.4426950408889634
SUM_ROWS = 16
WIN_PART = 256

SC_CORES = 2
SC_SUBCORES = 16
SC_CHUNK = 64
SC_INFLIGHT = 2

XATT_HEADS = 4
XATT_DIM = D_MODEL // XATT_HEADS

N_EXPERTS = 64
TOP_K = 8
N_GROUPS = 8
TOPK_GROUPS = 4
EXPERT_FF = 256
SHARED_FF = 256
ROUTED_SCALE = 2.5

LANES = 128
VMEM_LIMIT = 56 * 1024 * 1024

F32 = jnp.float32
BF16 = jnp.bfloat16
NT_DIMS = (((1,), (1,)), ((), ()))


def _params(n_axes):
    return pltpu.CompilerParams(dimension_semantics=("arbitrary",) * n_axes,
                                vmem_limit_bytes=VMEM_LIMIT)


def _dot(a, b):
    return jnp.dot(a, b, preferred_element_type=F32)


def _dot_nt(a, b):
    return lax.dot_general(a, b, NT_DIMS, preferred_element_type=F32)


def _layer_norm(v, g, b):
    mu = jnp.mean(v, axis=-1, keepdims=True)
    d = v - mu
    var = jnp.mean(d * d, axis=-1, keepdims=True)
    return d * lax.rsqrt(var + LN_EPS) * g + b


def _inproj_kernel(x_ref, pos_ref, wret_ref, wnq_ref, wkv_ref, wg_ref, invr_ref, invn_ref,
                   rq_ref, rk_ref, rv_ref, rg_ref, nq_ref, nqr_ref, kc_ref, vc_ref,
                   ks_ref, vs_ref, kw_ref, vw_ref, gate_ref):
    xb = x_ref[...].astype(BF16)
    pos = pos_ref[...]
    lane = lax.broadcasted_iota(jnp.int32, (1, LANES), 1)

    ang = pos * invr_ref[...]
    cos_r = jnp.cos(ang)
    sin_r = jnp.sin(ang)
    sin_r = jnp.where(lane < RET_DIM // 2, -sin_r, sin_r)
    q_all = _dot(xb, wret_ref[:, :RET_WIDTH])
    k_all = _dot(xb, wret_ref[:, RET_WIDTH:2 * RET_WIDTH])
    for h in range(RET_HEADS):
        cols = slice(h * RET_DIM, (h + 1) * RET_DIM)
        q = q_all[:, cols]
        rq_ref[:, cols] = (q * cos_r + pltpu.roll(q, RET_DIM // 2, 1) * sin_r).astype(BF16)
        k = k_all[:, cols]
        k = (k * cos_r + pltpu.roll(k, RET_DIM // 2, 1) * sin_r) * (RET_DIM ** -0.5)
        rk_ref[:, cols] = k.astype(BF16)
    rv_ref[...] = _dot(xb, wret_ref[:, 2 * RET_WIDTH:3 * RET_WIDTH]).astype(BF16)
    rg_ref[...] = _dot(xb, wret_ref[:, 3 * RET_WIDTH:4 * RET_WIDTH]).astype(BF16)

    half = ROPE_DIMS // 2
    j = lane % NSA_DIM
    angn = pos * invn_ref[...]
    cos_n = jnp.cos(angn)
    sin_n = jnp.sin(angn)
    sin_lo = jnp.where(j < half, -sin_n, 0.0)
    sin_hi = jnp.where((j >= half) & (j < 2 * half), sin_n, 0.0)

    def rope_n(v):
        return v * cos_n + pltpu.roll(v, half, 1) * sin_hi + pltpu.roll(v, LANES - half, 1) * sin_lo

    scale = NSA_DIM ** -0.5 * LOG2_E
    nq_all = _dot(xb, wnq_ref[...])
    for c in range(NSA_WIDTH // LANES):
        cols = slice(c * LANES, (c + 1) * LANES)
        q = nq_all[:, cols]
        nq_ref[:, cols] = (q * scale).astype(BF16)
        nqr_ref[:, cols] = (rope_n(q) * scale).astype(BF16)

    kv_all = _dot(xb, wkv_ref[...])

    def kv(i):
        return kv_all[:, i * KV_WIDTH:(i + 1) * KV_WIDTH]

    def split_groups(ref, v):
        for g in range(NSA_KV_GROUPS):
            ref[g] = v[:, g * NSA_DIM:(g + 1) * NSA_DIM].astype(BF16)

    kc_ref[...] = kv(0)
    vc_ref[...] = kv(1)
    split_groups(ks_ref, rope_n(kv(2)))
    split_groups(vs_ref, kv(3))
    split_groups(kw_ref, rope_n(kv(4)))
    split_groups(vw_ref, kv(5))

    gt = jax.nn.sigmoid(_dot_nt(wg_ref[...], xb))
    for g in range(NSA_KV_GROUPS):
        gate_ref[g] = gt[g * GATE_LANES:(g + 1) * GATE_LANES, :]


def _inproj(x2d, pos_col, w_in, row0, n):
    tm = 1024 if n % 1024 == 0 and row0 % 1024 == 0 else n
    blk0 = row0 // tm
    off = np.cumsum([0] + [RET_WIDTH] * 4 + [NSA_WIDTH] + [KV_WIDTH] * 6)
    w_ret = w_in[:, :off[4]].astype(BF16)
    w_nq = w_in[:, off[4]:off[5]].astype(BF16)
    w_kv = w_in[:, off[5]:off[11]].astype(BF16)
    wg = w_in[:, off[11]:].reshape(D_MODEL, NSA_KV_GROUPS, NSA_HPG * 3)
    wg = jnp.pad(wg, ((0, 0), (0, 0), (0, GATE_LANES - NSA_HPG * 3)))
    wg = wg.reshape(D_MODEL, NSA_KV_GROUPS * GATE_LANES).T.astype(BF16)

    lane = np.arange(LANES)
    half_r = RET_DIM // 2
    inv_r = (np.float32(RET_ROPE_BASE) ** (-np.arange(half_r, dtype=np.float32) / np.float32(half_r)))
    inv_r = inv_r.astype(np.float32)[lane % half_r][None, :]
    half_n = ROPE_DIMS // 2
    inv_n = (np.float32(ROPE_THETA) ** (-np.arange(half_n, dtype=np.float32) / np.float32(half_n)))
    jn = lane % NSA_DIM
    inv_n = np.where(jn < ROPE_DIMS, inv_n.astype(np.float32)[jn % half_n], np.float32(0.0))[None, :]

    row = lambda w: pl.BlockSpec((tm, w), lambda i: (i, 0))
    src_row = lambda w: pl.BlockSpec((tm, w), lambda i: (blk0 + i, 0))
    full = lambda a: pl.BlockSpec(a.shape, lambda i: (0,) * a.ndim)
    grp = lambda w: pl.BlockSpec((NSA_KV_GROUPS, tm, w), lambda i: (0, i, 0))
    bf = lambda w: jax.ShapeDtypeStruct((n, w), BF16)
    gbf = jax.ShapeDtypeStruct((NSA_KV_GROUPS, n, NSA_DIM), BF16)
    inv_r = jnp.asarray(inv_r, F32)
    inv_n = jnp.asarray(inv_n, F32)
    return pl.pallas_call(
        _inproj_kernel,
        grid=(n // tm,),
        in_specs=[src_row(D_MODEL), src_row(1), full(w_ret), full(w_nq), full(w_kv), full(wg),
                  full(inv_r), full(inv_n)],
        out_specs=[row(RET_WIDTH)] * 4 + [row(NSA_WIDTH)] * 2 + [row(KV_WIDTH)] * 2
                  + [grp(NSA_DIM)] * 4
                  + [pl.BlockSpec((NSA_KV_GROUPS, GATE_LANES, tm), lambda i: (0, 0, i))],
        out_shape=[bf(RET_WIDTH)] * 4 + [bf(NSA_WIDTH)] * 2
                  + [jax.ShapeDtypeStruct((n, KV_WIDTH), F32)] * 2 + [gbf] * 4
                  + [jax.ShapeDtypeStruct((NSA_KV_GROUPS, GATE_LANES, n), F32)],
        compiler_params=_params(1),
        name="inproj",
    )(x2d, pos_col, w_ret, w_nq, w_kv, wg, inv_r, inv_n)


def _retention_kernel(q_ref, k_ref, v_ref, g_ref, o_ref, state_ref):
    c = RET_CHUNK

    @pl.when(pl.program_id(1) == 0)
    def _():
        state_ref[...] = jnp.zeros_like(state_ref)

    row = lax.broadcasted_iota(jnp.int32, (c, c), 0)
    col = lax.broadcasted_iota(jnp.int32, (c, c), 1)
    rel = (row - col).astype(F32)
    idx = lax.broadcasted_iota(jnp.int32, (c, 1), 0).astype(F32)
    for h in range(RET_HEADS):
        log_g = float(np.log(np.float32(1.0) - np.float32(2.0) ** np.float32(-5.0 - h)))
        cols = slice(h * RET_DIM, (h + 1) * RET_DIM)
        dmask = jnp.where(rel >= 0, jnp.exp(log_g * jnp.maximum(rel, 0.0)), 0.0)
        zeta = jnp.exp(log_g * (c - 1.0 - idx))
        xi = jnp.exp(log_g * (idx + 1.0))
        for j in range(q_ref.shape[0] // c):
            rows = slice(j * c, (j + 1) * c)
            q = q_ref[rows, cols]
            k = k_ref[rows, cols]
            v = v_ref[rows, cols]
            scores = _dot_nt(q, k) * dmask
            inner = _dot(scores.astype(BF16), v)
            prev = state_ref[h]
            cross = _dot(q, prev.astype(BF16)) * xi
            kz = (k.astype(F32) * zeta).astype(BF16)
            kv = lax.dot_general(kz, v, (((0,), (0,)), ((), ())), preferred_element_type=F32)
            state_ref[h] = prev * float(np.exp(np.float32(log_g) * np.float32(c))) + kv
            o = inner + cross
            mu = jnp.mean(o, axis=-1, keepdims=True)
            d = o - mu
            var = jnp.mean(d * d, axis=-1, keepdims=True)
            o = d * lax.rsqrt(var + LN_EPS)
            o_ref[rows, cols] = (jax.nn.silu(g_ref[rows, cols].astype(F32)) * o).astype(BF16)


def _retention(rq, rk, rv, rg, batch, seq):
    per_step = RET_STEP_CHUNKS if (seq // RET_CHUNK) % RET_STEP_CHUNKS == 0 else 1
    nc = seq // (RET_CHUNK * per_step)
    spec = pl.BlockSpec((RET_CHUNK * per_step, RET_WIDTH), lambda b, n: (b * nc + n, 0))
    return pl.pallas_call(
        _retention_kernel,
        grid=(batch, nc),
        in_specs=[spec] * 4,
        out_specs=spec,
        out_shape=jax.ShapeDtypeStruct(rq.shape, BF16),
        scratch_shapes=[pltpu.VMEM((RET_HEADS, RET_DIM, RET_DIM), F32)],
        compiler_params=_params(2),
        name="retention",
    )(rq, rk, rv, rg)


def _compress_kernel(a_ref, pe_ref, w1_ref, w2_ref, o_ref, shift_ref, *, n_cmp):
    rows = a_ref.shape[0]
    a = a_ref[...]
    lo = (a + pe_ref[0]).astype(BF16)
    hi = (a + pe_ref[1]).astype(BF16)
    ridx = lax.broadcasted_iota(jnp.int32, (rows, 1), 0)
    shift_ref[rows:rows + 8, :] = jnp.zeros((8, CMP_HIDDEN), F32)
    for g in range(NSA_KV_GROUPS):
        p = _dot(lo, w1_ref[0, g])
        shift_ref[0:rows, :] = _dot(hi, w1_ref[1, g])
        hid = jax.nn.silu(p + shift_ref[pl.ds(1, rows), :])
        out = _dot(hid.astype(BF16), w2_ref[...])
        o_ref[g] = jnp.where(ridx < n_cmp, out, 0.0).astype(BF16)


def _compress(a, pe, w1, w2, batch, seq):
    rows = seq // CMP_STRIDE
    per = CMP_STRIDE * KV_WIDTH
    n_cmp = (seq - CMP_LEN) // CMP_STRIDE + 1
    a2 = a.reshape(batch * rows, per)
    pe2 = jnp.tile(pe.reshape(2, CMP_STRIDE, 1, NSA_DIM), (1, 1, NSA_KV_GROUPS, 1)).reshape(2, 1, per)
    w1r = w1.reshape(2, CMP_STRIDE, 1, NSA_DIM, CMP_HIDDEN)
    eye = jnp.eye(NSA_KV_GROUPS, dtype=w1.dtype).reshape(1, NSA_KV_GROUPS, 1, NSA_KV_GROUPS, 1, 1)
    w1x = (w1r[:, None] * eye).reshape(2, NSA_KV_GROUPS, per, CMP_HIDDEN).astype(BF16)
    w2b = w2.astype(BF16)
    full = lambda arr: pl.BlockSpec(arr.shape, lambda b: (0,) * arr.ndim)
    return pl.pallas_call(
        functools.partial(_compress_kernel, n_cmp=n_cmp),
        grid=(batch,),
        in_specs=[pl.BlockSpec((rows, per), lambda b: (b, 0)), full(pe2), full(w1x), full(w2b)],
        out_specs=pl.BlockSpec((None, NSA_KV_GROUPS, rows, NSA_DIM), lambda b: (b, 0, 0, 0)),
        out_shape=jax.ShapeDtypeStruct((batch, NSA_KV_GROUPS, rows, NSA_DIM), BF16),
        scratch_shapes=[pltpu.VMEM((rows + 8, CMP_HIDDEN), F32)],
        compiler_params=_params(1),
        name="compress",
    )(a2, pe2, w1x, w2b)


def _heads_to_lanes(ref):
    vt = ref[...].astype(F32).T
    return jnp.concatenate([vt[h * NSA_DIM:(h + 1) * NSA_DIM] for h in range(NSA_HPG)], axis=1).astype(BF16)


def _transpose_into(dst_ref, src_ref, chunk):
    def step(c, _):
        c0 = pl.multiple_of(c * chunk, chunk)
        dst_ref[:NSA_DIM, pl.ds(c0, chunk)] = src_ref[pl.ds(c0, chunk), :].astype(F32).T.astype(BF16)
        return 0
    lax.fori_loop(0, src_ref.shape[0] // chunk, step, 0)


def _nsa_kernel(qraw_ref, qrot_ref, gate_ref, kcmp_ref, vcmp_ref, ovt_ref,
                ks_ref, vs_ref, kw_ref, vw_ref, o_ref, vst_ref, vwt_ref, vct_ref, bias_ref, *, tq, tk, seq):
    i = pl.program_id(2)
    t0 = i * tq
    cols = NSA_HPG * tq
    n_sel = seq // SEL_LEN
    n_cmp_rows = seq // CMP_STRIDE
    blocks_per_tile = tk // SEL_LEN

    @pl.when(i == 0)
    def _():
        chunk = min(512, n_cmp_rows)
        _transpose_into(vst_ref, vs_ref, chunk)
        _transpose_into(vwt_ref, vw_ref, chunk)
        _transpose_into(vct_ref, vcmp_ref, chunk)
        vst_ref[NSA_DIM:, :] = jnp.ones((SUM_ROWS, seq), BF16)
        vwt_ref[NSA_DIM:, :] = jnp.ones((SUM_ROWS, seq), BF16)

    def split_sum(acc):
        return acc[:NSA_DIM] / acc[NSA_DIM:NSA_DIM + 1]

    q_raw = _heads_to_lanes(qraw_ref)
    q_rot = _heads_to_lanes(qrot_ref)
    t_row = t0 + lax.broadcasted_iota(jnp.int32, (1, tq), 1)

    chain_w = cols // NSA_CHAINS
    heads_per_chain = chain_w // tq
    chains = [slice(c * chain_w, (c + 1) * chain_w) for c in range(NSA_CHAINS)]
    tile_chain = lambda v: jnp.concatenate([v] * heads_per_chain, axis=1)

    pw = min(tq, WIN_PART)
    parts = []
    for u in range(tq // pw):
        span = WIN + pw
        ws = pl.multiple_of(jnp.maximum(t0 + u * pw - WIN, 0), pw)
        dist = t_row[:, u * pw:(u + 1) * pw] - (ws + lax.broadcasted_iota(jnp.int32, (span, 1), 0))
        bias_w = jnp.concatenate([jnp.where((dist >= 0) & (dist < WIN), 0.0, NEG)] * NSA_HPG, axis=1)
        q_part = jnp.concatenate([q_rot[:, h * tq + u * pw:h * tq + (u + 1) * pw] for h in range(NSA_HPG)],
                                 axis=1)
        s_w = _dot(kw_ref[pl.ds(ws, span), :], q_part) + bias_w
        p_w = jnp.exp2((s_w - jnp.max(s_w, axis=0, keepdims=True)).astype(BF16))
        parts.append(split_sum(_dot(vwt_ref[:, pl.ds(ws, span)], p_w)))
    o_w = jnp.concatenate([parts[u][:, h * pw:(h + 1) * pw]
                           for h in range(NSA_HPG) for u in range(tq // pw)], axis=1)

    c_idx = lax.broadcasted_iota(jnp.int32, (n_cmp_rows, 1), 0)
    bias_c = tile_chain(jnp.where(c_idx * CMP_STRIDE + (CMP_LEN - 1) <= t_row, 0.0, NEG))
    sees_any = tile_chain(jnp.where(t_row >= CMP_LEN - 1, 1.0, 0.0))
    o_c = []
    p_sum = None
    for c in chains:
        s_c = _dot(kcmp_ref[...], q_raw[:, c]) + bias_c
        e_c = jnp.exp2(s_c - jnp.max(s_c, axis=0, keepdims=True))
        p_c = e_c * (sees_any / jnp.sum(e_c, axis=0, keepdims=True))
        o_c.append(_dot(vct_ref[...], p_c.astype(BF16)))
        for h in range(heads_per_chain):
            p_h = p_c[:, h * tq:(h + 1) * tq]
            p_sum = p_h if p_sum is None else p_sum + p_h
    o_c = jnp.concatenate(o_c, axis=1)

    p_hi = p_sum.astype(BF16)
    p_lo = (p_sum - p_hi.astype(F32)).astype(BF16)
    ovt = ovt_ref[...]
    imp = _dot(ovt, p_hi) + _dot(ovt, p_lo)
    jb = lax.broadcasted_iota(jnp.int32, (n_sel, tq), 0)
    cur = (t0 + lax.broadcasted_iota(jnp.int32, (n_sel, tq), 1)) >> SEL_SHIFT
    forced = (jb == 0) | (jb == cur) | (jb == cur - 1)
    work = jnp.where(forced, -jnp.inf, imp)
    work = jnp.where(jb <= cur, work, NEG)
    sel_t = jnp.where(forced, 1.0, 0.0)
    for _ in range(max(min(SEL_TOPK, n_sel) - N_FORCED, 0)):
        best = jnp.max(work, axis=0, keepdims=True)
        first = jnp.min(jnp.where(work == best, jb, n_sel), axis=0, keepdims=True)
        hit = jb == first
        sel_t = jnp.where(hit, 1.0, sel_t)
        work = jnp.where(hit, -jnp.inf, work)
    bias_ref[...] = jnp.where(sel_t > 0.5, 0.0, NEG)

    def sel_tile(kt, carry, causal):
        k0 = pl.multiple_of(kt * tk, tk)
        bias = jnp.concatenate(
            [jnp.broadcast_to(bias_ref[pl.ds(kt * blocks_per_tile + j, 1), :], (SEL_LEN, tq))
             for j in range(blocks_per_tile)], axis=0)
        if causal:
            kpos = k0 + lax.broadcasted_iota(jnp.int32, (tk, 1), 0)
            bias = jnp.where(kpos <= t_row, bias, NEG)
        bias = tile_chain(bias)
        k_t = ks_ref[pl.ds(k0, tk), :]
        v_t = vst_ref[:, pl.ds(k0, tk)]
        out = []
        scores = [_dot(k_t, q_rot[:, c]) + bias for c in chains]
        for (m, acc), s in zip(carry, scores):
            m_new = jnp.maximum(m, jnp.max(s, axis=0, keepdims=True))
            p = jnp.exp2((s - m_new).astype(BF16))
            acc = jnp.exp2(m - m_new) * acc + _dot(v_t, p)
            out.append((m_new, acc))
        return tuple(out)

    n_full = t0 // tk
    init = tuple((jnp.full((1, chain_w), NEG, F32), jnp.zeros((NSA_DIM + SUM_ROWS, chain_w), F32))
                 for _ in chains)
    carry = lax.fori_loop(0, n_full, functools.partial(sel_tile, causal=False), init)
    for d in range(max(tq // tk, 1)):
        carry = sel_tile(n_full + d, carry, causal=True)
    o_s = jnp.concatenate([split_sum(acc) for _, acc in carry], axis=1)

    gt = gate_ref[...]
    outs = []
    for h in range(NSA_HPG):
        c = slice(h * tq, (h + 1) * tq)
        outs.append(gt[3 * h:3 * h + 1] * o_c[:, c] + gt[3 * h + 1:3 * h + 2] * o_s[:, c]
                    + gt[3 * h + 2:3 * h + 3] * o_w[:, c])
    o_ref[...] = jnp.concatenate(outs, axis=0).T.astype(BF16)


def _anchored(kernel_fn, n_inputs, n_anchors):
    def body(*refs, **static):
        kernel_fn(*refs[:n_inputs], *refs[n_inputs + n_anchors:], **static)
    return body


def _add_anchors(kernel_fn, args, in_specs, after):
    after = [a for a in (after or ()) if a is not None]
    if not after:
        return kernel_fn
    body = _anchored(kernel_fn, len(args), len(after))
    args.extend(after)
    in_specs.extend([pl.BlockSpec(memory_space=pl.ANY)] * len(after))
    return body


def _nsa(nq, nqr, gates, kcmp, vcmp, ks, vs, kw, vw, batch, seq, after=None):
    n = batch * seq
    tq = 512
    tk = 512 if seq % 512 == 0 else seq
    nqb = seq // tq
    n_sel = seq // SEL_LEN
    rows_c = seq // CMP_STRIDE
    gw = NSA_HPG * NSA_DIM
    cs = np.arange(rows_c)[None, :] * CMP_STRIDE
    ss = np.arange(n_sel)[:, None] * SEL_LEN
    n_cmp = (seq - CMP_LEN) // CMP_STRIDE + 1
    ovt = ((cs < ss + SEL_LEN) & (cs + CMP_LEN > ss) & (np.arange(rows_c)[None, :] < n_cmp))
    ovt = jnp.asarray(ovt.astype(np.float32), BF16)

    qspec = pl.BlockSpec((tq, gw), lambda b, g, i: (b * nqb + i, g))
    cspec = pl.BlockSpec((None, None, rows_c, NSA_DIM), lambda b, g, i: (b, g, 0, 0))
    kspec = pl.BlockSpec((None, seq, NSA_DIM), lambda b, g, i: (g, b, 0))
    args = [nq, nqr, gates, kcmp, vcmp, ovt, ks, vs, kw, vw]
    in_specs = [qspec, qspec,
                pl.BlockSpec((None, GATE_LANES, tq), lambda b, g, i: (g, 0, b * nqb + i)),
                cspec, cspec, pl.BlockSpec(ovt.shape, lambda b, g, i: (0, 0)),
                kspec, kspec, kspec, kspec]
    body = functools.partial(_add_anchors(_nsa_kernel, args, in_specs, after), tq=tq, tk=tk, seq=seq)
    return pl.pallas_call(
        body,
        grid=(batch, NSA_KV_GROUPS, nqb),
        in_specs=in_specs,
        out_specs=qspec,
        out_shape=jax.ShapeDtypeStruct((n, NSA_WIDTH), BF16),
        scratch_shapes=[pltpu.VMEM((NSA_DIM + SUM_ROWS, seq), BF16), pltpu.VMEM((NSA_DIM + SUM_ROWS, seq), BF16),
                        pltpu.VMEM((NSA_DIM, rows_c), BF16), pltpu.VMEM((n_sel, tq), F32)],
        compiler_params=_params(3),
        name="nsa",
    )(*args)


def _memkv_kernel(mem_ref, w_ref, kv_ref):
    kv_ref[...] = _dot(mem_ref[...].astype(BF16), w_ref[...]).astype(BF16)


def _memkv(mem2d, w_xkv):
    n = mem2d.shape[0]
    w = w_xkv.astype(BF16)
    return pl.pallas_call(
        _memkv_kernel,
        grid=(n // MEM_LEN,),
        in_specs=[pl.BlockSpec((MEM_LEN, D_MODEL), lambda i: (i, 0)),
                  pl.BlockSpec(w.shape, lambda i: (0, 0))],
        out_specs=pl.BlockSpec((MEM_LEN, 2 * D_MODEL), lambda i: (i, 0)),
        out_shape=jax.ShapeDtypeStruct((n, 2 * D_MODEL), BF16),
        compiler_params=_params(1),
        name="memkv",
    )(mem2d, w)


def _pack_halves(v):
    half = D_MODEL // 2
    hi = pltpu.bitcast(v[:, :half].astype(BF16).astype(F32), jnp.uint32)
    lo = pltpu.bitcast(v[:, half:].astype(BF16).astype(F32), jnp.uint32)
    return hi | (lo >> 16)


def _unpack_halves(words):
    return pltpu.bitcast(words & jnp.uint32(0xFFFF0000), F32), pltpu.bitcast(words << 16, F32)


def _postmix_kernel(x_ref, oret_ref, onsa_ref, kv_ref, wout_ref, wq_ref, wo_ref,
                    g1_ref, b1_ref, g2_ref, b2_ref, x2_ref, x2p_ref):
    mixed = jnp.concatenate([oret_ref[...], onsa_ref[...]], axis=1)
    x1 = _layer_norm(DN_ALPHA * x_ref[...] + _dot(mixed, wout_ref[...]), g1_ref[...], b1_ref[...])
    q = (_dot(x1.astype(BF16), wq_ref[...]) * (XATT_DIM ** -0.5 * LOG2_E)).astype(BF16)
    heads = []
    for h in range(XATT_HEADS):
        cols = slice(h * XATT_DIM, (h + 1) * XATT_DIM)
        s = _dot_nt(q[:, cols], kv_ref[:, cols])
        m = jnp.max(s, axis=-1, keepdims=True)
        p = jnp.exp2(s - m)
        l = jnp.sum(p, axis=-1, keepdims=True)
        heads.append(_dot(p.astype(BF16), kv_ref[:, D_MODEL + h * XATT_DIM:D_MODEL + (h + 1) * XATT_DIM]) / l)
    att = jnp.concatenate(heads, axis=1).astype(BF16)
    x2 = _layer_norm(DN_ALPHA * x1 + _dot(att, wo_ref[...]), g2_ref[...], b2_ref[...])
    x2_ref[...] = x2
    x2p_ref[...] = _pack_halves(x2)


def _postmix(x2d, o_ret, o_nsa, kvx, w_out, w_xq, w_xo, ln1_g, ln1_b, ln2_g, ln2_b, batch0, batch, seq,
             after=None):
    n = batch * seq
    tm = 512 if seq % 512 == 0 else seq
    per_b = seq // tm
    row = lambda w: pl.BlockSpec((tm, w), lambda b, i: (b * per_b + i, 0))
    full = lambda a: pl.BlockSpec(a.shape, lambda b, i: (0,) * a.ndim)
    ws = [w_out.astype(BF16), w_xq.astype(BF16), w_xo.astype(BF16)]
    vecs = [v.reshape(1, D_MODEL) for v in (ln1_g, ln1_b, ln2_g, ln2_b)]
    args = [x2d, o_ret, o_nsa, kvx, *ws, *vecs]
    in_specs = ([pl.BlockSpec((tm, D_MODEL), lambda b, i: ((batch0 + b) * per_b + i, 0)),
                 row(RET_WIDTH), row(NSA_WIDTH),
                 pl.BlockSpec((MEM_LEN, 2 * D_MODEL), lambda b, i: (batch0 + b, 0))]
                + [full(w) for w in ws] + [full(v) for v in vecs])
    return pl.pallas_call(
        _add_anchors(_postmix_kernel, args, in_specs, after),
        grid=(batch, per_b),
        in_specs=in_specs,
        out_specs=[row(D_MODEL),
                   row(D_MODEL // 2)],
        out_shape=[jax.ShapeDtypeStruct((n, D_MODEL), F32),
                   jax.ShapeDtypeStruct((n, D_MODEL // 2), jnp.uint32)],
        compiler_params=_params(2),
        name="postmix",
    )(*args)


def _router_kernel(x_ref, wr_ref, bias_ref, e_ref, rank_ref, w_ref, cnt_ref, cntrow_ref, carry_ref, carryrow_ref):
    tn = x_ref.shape[0]
    per = N_EXPERTS // N_GROUPS

    @pl.when(pl.program_id(0) == 0)
    def _():
        carry_ref[...] = jnp.zeros_like(carry_ref)
        carryrow_ref[...] = jnp.zeros_like(carryrow_ref)

    logits = _dot_nt(wr_ref[...], x_ref[...].astype(BF16))
    scores = jax.nn.sigmoid(logits)
    biased = scores + bias_ref[...]
    b3 = biased.reshape(N_GROUPS, per, tn)
    member = lax.broadcasted_iota(jnp.int32, (N_GROUPS, per, tn), 1)
    top1 = jnp.max(b3, axis=1, keepdims=True)
    first1 = jnp.min(jnp.where(b3 == top1, member, per), axis=1, keepdims=True)
    top2 = jnp.max(jnp.where(member == first1, -jnp.inf, b3), axis=1, keepdims=True)
    gscore = top1 + top2
    gidx = lax.broadcasted_iota(jnp.int32, (N_GROUPS, 1, tn), 0)
    gwork = gscore
    for _ in range(TOPK_GROUPS - 1):
        gbest = jnp.max(gwork, axis=0, keepdims=True)
        gfirst = jnp.min(jnp.where(gwork == gbest, gidx, N_GROUPS), axis=0, keepdims=True)
        gwork = jnp.where(gidx == gfirst, -jnp.inf, gwork)
    kth = jnp.max(gwork, axis=0, keepdims=True)
    work = jnp.where(gscore >= kth, b3, NEG).reshape(N_EXPERTS, tn)
    eidx = lax.broadcasted_iota(jnp.int32, (N_EXPERTS, tn), 0)
    picks = []
    chosen = jnp.zeros((N_EXPERTS, tn), F32)
    for _ in range(TOP_K):
        best = jnp.max(work, axis=0, keepdims=True)
        first = jnp.min(jnp.where(work == best, eidx, N_EXPERTS), axis=0, keepdims=True)
        hit = eidx == first
        picks.append((first, hit))
        chosen = jnp.where(hit, 1.0, chosen)
        work = jnp.where(hit, -jnp.inf, work)

    r_i = lax.broadcasted_iota(jnp.int32, (tn, tn), 0)
    c_i = lax.broadcasted_iota(jnp.int32, (tn, tn), 1)
    before = jnp.where(r_i < c_i, 1.0, 0.0).astype(BF16)
    chosen_b = chosen.astype(BF16)
    rank = _dot(chosen_b, before) + carry_ref[...]
    carry_ref[...] = carry_ref[...] + jnp.sum(chosen, axis=1, keepdims=True)
    carryrow_ref[...] = carryrow_ref[...] + _dot_nt(jnp.ones((8, tn), BF16), chosen_b)
    cnt_ref[...] = carry_ref[...]
    cntrow_ref[...] = carryrow_ref[...]

    wsel = [jnp.sum(jnp.where(hit, scores, 0.0), axis=0, keepdims=True) for _, hit in picks]
    wsum = wsel[0]
    for v in wsel[1:]:
        wsum = wsum + v
    for kk, (first, hit) in enumerate(picks):
        e_ref[kk:kk + 1, :] = first
        rank_ref[kk:kk + 1, :] = jnp.sum(jnp.where(hit, rank, 0.0), axis=0, keepdims=True).astype(jnp.int32)
        w_ref[kk:kk + 1, :] = wsel[kk] / wsum * ROUTED_SCALE


def _router(x2, w_router, router_bias):
    n = x2.shape[0]
    tn = 512 if n % 512 == 0 else n
    wr_t = w_router.T.astype(BF16)
    bias = router_bias.reshape(N_EXPERTS, 1).astype(F32)
    kspec = pl.BlockSpec((TOP_K, tn), lambda i: (0, i))
    return pl.pallas_call(
        _router_kernel,
        grid=(n // tn,),
        in_specs=[pl.BlockSpec((tn, D_MODEL), lambda i: (i, 0)),
                  pl.BlockSpec(wr_t.shape, lambda i: (0, 0)),
                  pl.BlockSpec(bias.shape, lambda i: (0, 0))],
        out_specs=[kspec, kspec, kspec, pl.BlockSpec((N_EXPERTS, 1), lambda i: (0, 0)),
                   pl.BlockSpec((8, N_EXPERTS), lambda i: (0, 0))],
        out_shape=[jax.ShapeDtypeStruct((TOP_K, n), jnp.int32),
                   jax.ShapeDtypeStruct((TOP_K, n), jnp.int32),
                   jax.ShapeDtypeStruct((TOP_K, n), F32),
                   jax.ShapeDtypeStruct((N_EXPERTS, 1), F32),
                   jax.ShapeDtypeStruct((8, N_EXPERTS), F32)],
        scratch_shapes=[pltpu.VMEM((N_EXPERTS, 1), F32), pltpu.VMEM((8, N_EXPERTS), F32)],
        compiler_params=_params(1),
        name="router",
    )(x2, wr_t, bias)


def _slots_kernel(e_ref, rank_ref, cnt_ref, cntrow_ref, dest_ref, blk_e_ref, valid_ref, used_ref,
                  *, blk, n_blocks):
    pad = lambda c: jnp.ceil(c / blk) * blk
    cnt = cnt_ref[...]
    padded = pad(cnt)
    padded_row = pad(cntrow_ref[0:1, :])
    r_i = lax.broadcasted_iota(jnp.int32, (N_EXPERTS, N_EXPERTS), 0)
    c_i = lax.broadcasted_iota(jnp.int32, (N_EXPERTS, N_EXPERTS), 1)
    start = jnp.sum(jnp.where(c_i < r_i, padded_row, 0.0), axis=1, keepdims=True)
    end = start + padded
    e = e_ref[...]
    dest = rank_ref[...]
    for ex in range(N_EXPERTS):
        dest = dest + jnp.where(e == ex, start[ex:ex + 1, :].astype(jnp.int32), 0)
    dest_ref[...] = dest
    bstart = (lax.broadcasted_iota(jnp.int32, (1, n_blocks), 1) * blk).astype(F32)
    owner = jnp.sum(jnp.where(end <= bstart, 1.0, 0.0), axis=0, keepdims=True)
    blk_e_ref[...] = jnp.minimum(owner, N_EXPERTS - 1.0).astype(jnp.int32)
    inside = (start <= bstart) & (bstart < end)
    real = jnp.clip(start + cnt - bstart, 0.0, float(blk))
    valid = jnp.sum(jnp.where(inside, real, 0.0), axis=0, keepdims=True)
    valid_ref[...] = valid.astype(jnp.int32)
    used_ref[...] = jnp.sum(jnp.where(valid > 0.0, 1.0, 0.0), axis=1, keepdims=True).astype(jnp.int32)


def _slots(e_k, rank_k, counts, counts_row, blk, n_blocks):
    n = e_k.shape[1]
    full = lambda shape: pl.BlockSpec(shape, lambda: (0,) * len(shape))
    return pl.pallas_call(
        functools.partial(_slots_kernel, blk=blk, n_blocks=n_blocks),
        in_specs=[full((TOP_K, n)), full((TOP_K, n)), full((N_EXPERTS, 1)), full((8, N_EXPERTS))],
        out_specs=[full((TOP_K, n)), full((1, n_blocks)), full((1, n_blocks)), full((1, 1))],
        out_shape=[jax.ShapeDtypeStruct((TOP_K, n), jnp.int32),
                   jax.ShapeDtypeStruct((1, n_blocks), jnp.int32),
                   jax.ShapeDtypeStruct((1, n_blocks), jnp.int32),
                   jax.ShapeDtypeStruct((1, 1), jnp.int32)],
        compiler_params=pltpu.CompilerParams(vmem_limit_bytes=VMEM_LIMIT),
        name="slots",
    )(e_k, rank_k, counts, counts_row)


def _sc_worker_base(per_worker):
    return (lax.axis_index("s") * SC_CORES + lax.axis_index("c")) * per_worker


def _sc_scatter_rows(rows, idx, n_out):
    n, width = rows.shape
    k_lists = idx.shape[0] // n
    workers = SC_CORES * SC_SUBCORES
    per_worker = n // workers
    assert per_worker * workers == n and per_worker % (SC_CHUNK * SC_INFLIGHT) == 0
    mesh = plsc.VectorSubcoreMesh(core_axis_name="c", subcore_axis_name="s")
    lanes = range(SC_INFLIGHT)

    @functools.partial(
        pl.kernel, mesh=mesh,
        out_type=jax.ShapeDtypeStruct((n_out, width), rows.dtype),
        scratch_types=[pltpu.VMEM((SC_CHUNK, width), rows.dtype)] * SC_INFLIGHT
                      + [pltpu.VMEM((SC_CHUNK,), jnp.int32)] * (k_lists * SC_INFLIGHT)
                      + [pltpu.SemaphoreType.DMA] * (2 * SC_INFLIGHT),
        name="sc_scatter")
    def scatter(rows_hbm, idx_hbm, out_hbm, *scratch):
        rows_vs = scratch[:SC_INFLIGHT]
        idx_vs = [scratch[SC_INFLIGHT + j * k_lists:SC_INFLIGHT + (j + 1) * k_lists] for j in lanes]
        sems = scratch[SC_INFLIGHT * (1 + k_lists):]
        sem_in, sem_out = sems[:SC_INFLIGHT], sems[SC_INFLIGHT:]
        base = _sc_worker_base(per_worker)

        @pl.loop(0, per_worker // (SC_CHUNK * SC_INFLIGHT))
        def _(gi):
            offs = [pl.multiple_of(base + (gi * SC_INFLIGHT + j) * SC_CHUNK, SC_CHUNK) for j in lanes]
            loads = []
            for j in lanes:
                loads.append([pltpu.async_copy(rows_hbm.at[pl.ds(offs[j], SC_CHUNK)], rows_vs[j], sem_in[j])]
                             + [pltpu.async_copy(
                                 idx_hbm.at[pl.ds(pl.multiple_of(k * n + offs[j], SC_CHUNK), SC_CHUNK)],
                                 idx_vs[j][k], sem_in[j]) for k in range(k_lists)])
            copies = []
            for j in lanes:
                for c in loads[j]:
                    c.wait()
                copies.append([pltpu.async_copy(rows_vs[j], out_hbm.at[idx_vs[j][k]], sem_out[j])
                               for k in range(k_lists)])
            for j in lanes:
                for c in copies[j]:
                    c.wait()

    return scatter(rows, idx)


def _experts_kernel(blk_e_ref, valid_ref, used_ref, xs_ref, wg_ref, wu_ref, wd_ref, y_ref, wg_b, wu_b, wd_b):
    del used_ref
    i = pl.program_id(0)
    valid = valid_ref[i]

    @pl.when((i == 0) | (blk_e_ref[i] != blk_e_ref[jnp.maximum(i - 1, 0)]))
    def _():
        wg_b[...] = wg_ref[...].astype(BF16)
        wu_b[...] = wu_ref[...].astype(BF16)
        wd_b[...] = wd_ref[...].astype(BF16)

    @pl.when(valid > 0)
    def _():
        half = D_MODEL // 2
        row = lax.broadcasted_iota(jnp.int32, (xs_ref.shape[0], 1), 0)
        hi, lo = (v.astype(BF16) for v in _unpack_halves(jnp.where(row < valid, xs_ref[...], jnp.uint32(0))))
        gate = _dot(hi, wg_b[:half, :]) + _dot(lo, wg_b[half:, :])
        up = _dot(hi, wu_b[:half, :]) + _dot(lo, wu_b[half:, :])
        y_ref[...] = _pack_halves(_dot((jax.nn.silu(gate) * up).astype(BF16), wd_b[...]))

    @pl.when(valid <= 0)
    def _():
        y_ref[...] = jnp.zeros_like(y_ref)


def _experts(blk_e, valid, n_used, xs, w_gate, w_up, w_down, blk, after=None):
    cap, width = xs.shape
    wspec = lambda a: pl.BlockSpec((None,) + a.shape[1:], lambda i, be, nv, nu: (be[i], 0, 0))
    rows = pl.BlockSpec((blk, width), lambda i, be, nv, nu: (jnp.minimum(i, nu[0]), 0))
    args = [blk_e, valid, n_used, xs, w_gate, w_up, w_down]
    in_specs = [rows, wspec(w_gate), wspec(w_up), wspec(w_down)]
    body = _add_anchors(_experts_kernel, args, in_specs, after)
    return pl.pallas_call(
        body,
        grid_spec=pltpu.PrefetchScalarGridSpec(
            num_scalar_prefetch=3,
            grid=(cap // blk,),
            in_specs=in_specs,
            out_specs=rows,
            scratch_shapes=[pltpu.VMEM(w.shape[1:], BF16) for w in (w_gate, w_up, w_down)],
        ),
        out_shape=jax.ShapeDtypeStruct(xs.shape, xs.dtype),
        compiler_params=_params(1),
        name="experts",
    )(*args)


def _sc_gather_rows(table, idx):
    b, width = idx.shape[0], table.shape[1]
    workers = SC_CORES * SC_SUBCORES
    per_worker = b // workers
    assert per_worker * workers == b and per_worker % (SC_CHUNK * SC_INFLIGHT) == 0
    mesh = plsc.VectorSubcoreMesh(core_axis_name="c", subcore_axis_name="s")

    @functools.partial(
        pl.kernel, mesh=mesh,
        out_type=jax.ShapeDtypeStruct((b, width), table.dtype),
        scratch_types=[pltpu.VMEM((SC_CHUNK,), jnp.int32)] * SC_INFLIGHT
                      + [pltpu.VMEM((SC_CHUNK, width), table.dtype)] * SC_INFLIGHT
                      + [pltpu.SemaphoreType.DMA] * (1 + 2 * SC_INFLIGHT),
        name="sc_gather")
    def gather(table_hbm, idx_hbm, out_hbm, *scratch):
        idx_vs = scratch[:SC_INFLIGHT]
        rows_vs = scratch[SC_INFLIGHT:2 * SC_INFLIGHT]
        sem_idx = scratch[2 * SC_INFLIGHT]
        sem_rows = scratch[2 * SC_INFLIGHT + 1:3 * SC_INFLIGHT + 1]
        sem_out = scratch[3 * SC_INFLIGHT + 1:]
        base = _sc_worker_base(per_worker)
        lanes = range(SC_INFLIGHT)

        @pl.loop(0, per_worker // (SC_CHUNK * SC_INFLIGHT))
        def _(gi):
            offs = [pl.multiple_of(base + (gi * SC_INFLIGHT + j) * SC_CHUNK, SC_CHUNK) for j in lanes]
            loads = [pltpu.async_copy(idx_hbm.at[pl.ds(offs[j], SC_CHUNK)], idx_vs[j], sem_idx) for j in lanes]
            for c in loads:
                c.wait()
            gathers = [pltpu.async_copy(table_hbm.at[idx_vs[j]], rows_vs[j], sem_rows[j]) for j in lanes]
            writes = []
            for j in lanes:
                gathers[j].wait()
                writes.append(pltpu.async_copy(rows_vs[j], out_hbm.at[pl.ds(offs[j], SC_CHUNK)], sem_out[j]))
            for c in writes:
                c.wait()

    return gather(table, idx)


def _combine_kernel(x_ref, wk_ref, yk_ref, wsg_ref, wsu_ref, wsd_ref, g_ref, b_ref, *rest):
    o_ref = rest[-1]
    x = x_ref[...]
    xb = x.astype(BF16)
    shared = _dot((jax.nn.silu(_dot(xb, wsg_ref[...])) * _dot(xb, wsu_ref[...])).astype(BF16), wsd_ref[...])
    wk = wk_ref[...]
    routed_hi = routed_lo = None
    for kk in range(TOP_K):
        hi, lo = _unpack_halves(yk_ref[kk])
        w = wk[:, kk:kk + 1]
        routed_hi = hi * w if kk == 0 else routed_hi + hi * w
        routed_lo = lo * w if kk == 0 else routed_lo + lo * w
    routed = jnp.concatenate([routed_hi, routed_lo], axis=1)
    o_ref[...] = _layer_norm(DN_ALPHA * x + (routed + shared), g_ref[...], b_ref[...])


def _combine(x2, w_tok, yk, ws_gate, ws_up, ws_down, ln3_g, ln3_b, row0, n_total, out_prev, after=None,
             part=(0, 1)):
    n = x2.shape[0] // part[1]
    sub0 = part[0] * n
    tt = 512 if n % 512 == 0 and row0 % 512 == 0 else n
    blk0 = (row0 + sub0) // tt
    sub_blk = sub0 // tt
    ws = [ws_gate.astype(BF16), ws_up.astype(BF16), ws_down.astype(BF16)]
    vecs = [ln3_g.reshape(1, D_MODEL), ln3_b.reshape(1, D_MODEL)]
    full = lambda a: pl.BlockSpec(a.shape, lambda i: (0,) * a.ndim)
    args = [x2, w_tok, yk, *ws, *vecs]
    in_specs = ([pl.BlockSpec((tt, D_MODEL), lambda i: (sub_blk + i, 0)),
                 pl.BlockSpec((tt, TOP_K), lambda i: (sub_blk + i, 0)),
                 pl.BlockSpec((TOP_K, tt, D_MODEL // 2), lambda i: (0, sub_blk + i, 0))]
                + [full(a) for a in ws] + [full(v) for v in vecs])
    aliases = {}
    for anchor in (a for a in (after or ()) if a is not None):
        args.append(anchor)
        in_specs.append(pl.BlockSpec(memory_space=pl.ANY))
    if out_prev is not None:
        aliases = {len(args): 0}
        args.append(out_prev)
        in_specs.append(pl.BlockSpec(memory_space=pl.ANY))
    return pl.pallas_call(
        _combine_kernel,
        grid=(n // tt,),
        in_specs=in_specs,
        out_specs=pl.BlockSpec((tt, D_MODEL), lambda i: (blk0 + i, 0)),
        out_shape=jax.ShapeDtypeStruct((n_total, D_MODEL), F32),
        input_output_aliases=aliases,
        compiler_params=_params(1),
        name="combine",
    )(*args)


EXPERT_BLOCK = 1024


def _moe_dispatch(x2, x2p, w_router, router_bias):
    n = x2.shape[0]
    cap = n * TOP_K + N_EXPERTS * EXPERT_BLOCK
    e_k, rank_k, w_k, counts, counts_row = _router(x2, w_router, router_bias)
    dest, blk_e, valid, n_used = _slots(e_k, rank_k, counts, counts_row, EXPERT_BLOCK, cap // EXPERT_BLOCK)
    dest = dest.reshape(-1)
    return dict(w_tok=w_k.T, dest=dest, blk_e=blk_e.reshape(-1), valid=valid.reshape(-1),
                n_used=n_used.reshape(-1), xs=_sc_scatter_rows(x2p, dest, cap))


def _moe_experts(routed, w_gate, w_up, w_down, after):
    y = _experts(routed["blk_e"], routed["valid"], routed["n_used"], routed["xs"], w_gate, w_up, w_down,
                 EXPERT_BLOCK, after=after)
    n = routed["dest"].shape[0] // TOP_K
    return y, _sc_gather_rows(y, routed["dest"]).reshape(TOP_K, n, D_MODEL // 2)


def _layer(x, mem, positions, w_in, cmp_pe_k, cmp_pe_v, cmp_w1_k, cmp_w2_k, cmp_w1_v, cmp_w2_v,
           w_out, ln1_g, ln1_b, w_xq, w_xkv, w_xo, ln2_g, ln2_b, w_router, router_bias,
           w_gate, w_up, w_down, ws_gate, ws_up, ws_down, ln3_g, ln3_b):
    batch, seq, _ = x.shape
    n_total = batch * seq
    x2d = x.reshape(n_total, D_MODEL)
    pos_col = positions.astype(F32).reshape(n_total, 1)
    kvx = _memkv(mem.reshape(batch * MEM_LEN, D_MODEL), w_xkv)
    last = max(1, batch // 4)
    sizes = [batch - last, last] if batch > 1 else [batch]
    starts = [sum(sizes[:g]) for g in range(len(sizes))]

    def mixers_in(g):
        nb, row0 = sizes[g], starts[g] * seq
        (rq, rk, rv, rg, nq, nqr, kc, vc, ks, vs, kw, vw, gates) = _inproj(x2d, pos_col, w_in, row0, nb * seq)
        o_ret = _retention(rq, rk, rv, rg, nb, seq)
        kcmp = _compress(kc, cmp_pe_k, cmp_w1_k, cmp_w2_k, nb, seq)
        vcmp = _compress(vc, cmp_pe_v, cmp_w1_v, cmp_w2_v, nb, seq)
        return o_ret, (nq, nqr, gates, kcmp, vcmp, ks, vs, kw, vw)

    def attend(g, nsa_args, after):
        return _nsa(*nsa_args, sizes[g], seq, after=after)

    def mix_and_route(g, o_ret, o_nsa, after):
        x2, x2p = _postmix(x2d, o_ret, o_nsa, kvx, w_out, w_xq, w_xo, ln1_g, ln1_b, ln2_g, ln2_b,
                           starts[g], sizes[g], seq, after=after)
        return x2, _moe_dispatch(x2, x2p, w_router, router_bias)

    def combine(g, x2, routed, yk, out_prev, after=None, part=(0, 1)):
        return _combine(x2, routed["w_tok"], yk, ws_gate, ws_up, ws_down, ln3_g, ln3_b, starts[g] * seq,
                        n_total, out_prev, after=after, part=part)

    o_ret, nsa_args = mixers_in(0)
    x2, routed = mix_and_route(0, o_ret, attend(0, nsa_args, None), None)
    if len(sizes) == 1:
        y, yk = _moe_experts(routed, w_gate, w_up, w_down, after=None)
        return combine(0, x2, routed, yk, None).reshape(batch, seq, D_MODEL)
    o_ret1, nsa_args1 = mixers_in(1)
    y, yk = _moe_experts(routed, w_gate, w_up, w_down, after=[o_ret1])
    o_nsa1 = attend(1, nsa_args1, [y])
    x2_1, routed1 = mix_and_route(1, o_ret1, o_nsa1, [yk])
    out = combine(0, x2, routed, yk, None, [routed1["dest"]], part=(0, 2))
    y1, yk1 = _moe_experts(routed1, w_gate, w_up, w_down, after=[out])
    out = combine(0, x2, routed, yk, out, [y1], part=(1, 2))
    out = combine(1, x2_1, routed1, yk1, out)
    return out.reshape(batch, seq, D_MODEL)


def kernel(x, mem, positions, w_in, cmp_pe_k, cmp_pe_v, cmp_w1_k, cmp_w2_k, cmp_w1_v, cmp_w2_v, w_out, ln1_g, ln1_b, w_xq, w_xkv, w_xo, ln2_g, ln2_b, w_router, router_bias, w_gate, w_up, w_down, ws_gate, ws_up, ws_down, ln3_g, ln3_b):
    for l in range(DEPTH):
        x = _layer(x, mem, positions, w_in[l], cmp_pe_k[l], cmp_pe_v[l], cmp_w1_k[l], cmp_w2_k[l],
                   cmp_w1_v[l], cmp_w2_v[l], w_out[l], ln1_g[l], ln1_b[l], w_xq[l], w_xkv[l],
                   w_xo[l], ln2_g[l], ln2_b[l], w_router[l], router_bias[l], w_gate[l], w_up[l],
                   w_down[l], ws_gate[l], ws_up[l], ws_down[l], ln3_g[l], ln3_b[l])
    return x
```

```python
import functools

import numpy as np
import jax
import jax.numpy as jnp
from jax import lax
from jax.experimental import pallas as pl
from jax.experimental.pallas import tpu as pltpu
from jax.experimental.pallas import tpu_sc as plsc

D_MODEL = 1024
MEM_LEN = 256
DEPTH = 1
DN_ALPHA = (2 * DEPTH) ** 0.25
LN_EPS = 1e-5
NEG = -1e30

RET_HEADS = 4
RET_DIM = 128
RET_CHUNK = 128
RET_ROPE_BASE = 10000.0
RET_STEP_CHUNKS = 8
RET_WIDTH = RET_HEADS * RET_DIM

NSA_HEADS = 8
NSA_KV_GROUPS = 2
NSA_HPG = NSA_HEADS // NSA_KV_GROUPS
NSA_DIM = 64
NSA_WIDTH = NSA_HEADS * NSA_DIM
KV_WIDTH = NSA_KV_GROUPS * NSA_DIM
CMP_LEN = 32
CMP_STRIDE = 16
CMP_HIDDEN = 256
SEL_LEN = 64
SEL_SHIFT = 6
SEL_TOPK = 16
N_FORCED = 3
WIN = 512
ROPE_THETA = 500000.0
ROPE_DIMS = NSA_DIM // 4
GATE_LANES = 16
NSA_CHAINS = 1
LOG2_E = 1.4426950408889634
SUM_ROWS = 16
WIN_PART = 256

SC_CORES = 2
SC_SUBCORES = 16
SC_CHUNK = 64
SC_INFLIGHT = 2

XATT_HEADS = 4
XATT_DIM = D_MODEL // XATT_HEADS

N_EXPERTS = 64
TOP_K = 8
N_GROUPS = 8
TOPK_GROUPS = 4
EXPERT_FF = 256
SHARED_FF = 256
ROUTED_SCALE = 2.5

LANES = 128
VMEM_LIMIT = 56 * 1024 * 1024

F32 = jnp.float32
BF16 = jnp.bfloat16
NT_DIMS = (((1,), (1,)), ((), ()))


def _params(n_axes):
    return pltpu.CompilerParams(dimension_semantics=("arbitrary",) * n_axes,
                                vmem_limit_bytes=VMEM_LIMIT)


def _dot(a, b):
    return jnp.dot(a, b, preferred_element_type=F32)


def _dot_nt(a, b):
    return lax.dot_general(a, b, NT_DIMS, preferred_element_type=F32)


def _layer_norm(v, g, b):
    mu = jnp.mean(v, axis=-1, keepdims=True)
    d = v - mu
    var = jnp.mean(d * d, axis=-1, keepdims=True)
    return d * lax.rsqrt(var + LN_EPS) * g + b


def _inproj_kernel(x_ref, pos_ref, wret_ref, wnq_ref, wkv_ref, wg_ref, invr_ref, invn_ref,
                   rq_ref, rk_ref, rv_ref, rg_ref, nq_ref, nqr_ref, kc_ref, vc_ref,
                   ks_ref, vs_ref, kw_ref, vw_ref, gate_ref):
    xb = x_ref[...].astype(BF16)
    pos = pos_ref[...]
    lane = lax.broadcasted_iota(jnp.int32, (1, LANES), 1)

    ang = pos * invr_ref[...]
    cos_r = jnp.cos(ang)
    sin_r = jnp.sin(ang)
    sin_r = jnp.where(lane < RET_DIM // 2, -sin_r, sin_r)
    q_all = _dot(xb, wret_ref[:, :RET_WIDTH])
    k_all = _dot(xb, wret_ref[:, RET_WIDTH:2 * RET_WIDTH])
    for h in range(RET_HEADS):
        cols = slice(h * RET_DIM, (h + 1) * RET_DIM)
        q = q_all[:, cols]
        rq_ref[:, cols] = (q * cos_r + pltpu.roll(q, RET_DIM // 2, 1) * sin_r).astype(BF16)
        k = k_all[:, cols]
        k = (k * cos_r + pltpu.roll(k, RET_DIM // 2, 1) * sin_r) * (RET_DIM ** -0.5)
        rk_ref[:, cols] = k.astype(BF16)
    rv_ref[...] = _dot(xb, wret_ref[:, 2 * RET_WIDTH:3 * RET_WIDTH]).astype(BF16)
    rg_ref[...] = _dot(xb, wret_ref[:, 3 * RET_WIDTH:4 * RET_WIDTH]).astype(BF16)

    half = ROPE_DIMS // 2
    j = lane % NSA_DIM
    angn = pos * invn_ref[...]
    cos_n = jnp.cos(angn)
    sin_n = jnp.sin(angn)
    sin_lo = jnp.where(j < half, -sin_n, 0.0)
    sin_hi = jnp.where((j >= half) & (j < 2 * half), sin_n, 0.0)

    def rope_n(v):
        return v * cos_n + pltpu.roll(v, half, 1) * sin_hi + pltpu.roll(v, LANES - half, 1) * sin_lo

    scale = NSA_DIM ** -0.5 * LOG2_E
    nq_all = _dot(xb, wnq_ref[...])
    for c in range(NSA_WIDTH // LANES):
        cols = slice(c * LANES, (c + 1) * LANES)
        q = nq_all[:, cols]
        nq_ref[:, cols] = (q * scale).astype(BF16)
        nqr_ref[:, cols] = (rope_n(q) * scale).astype(BF16)

    kv_all = _dot(xb, wkv_ref[...])

    def kv(i):
        return kv_all[:, i * KV_WIDTH:(i + 1) * KV_WIDTH]

    def split_groups(ref, v):
        for g in range(NSA_KV_GROUPS):
            ref[g] = v[:, g * NSA_DIM:(g + 1) * NSA_DIM].astype(BF16)

    kc_ref[...] = kv(0)
    vc_ref[...] = kv(1)
    split_groups(ks_ref, rope_n(kv(2)))
    split_groups(vs_ref, kv(3))
    split_groups(kw_ref, rope_n(kv(4)))
    split_groups(vw_ref, kv(5))

    gt = jax.nn.sigmoid(_dot_nt(wg_ref[...], xb))
    for g in range(NSA_KV_GROUPS):
        gate_ref[g] = gt[g * GATE_LANES:(g + 1) * GATE_LANES, :]


def _inproj(x2d, pos_col, w_in, row0, n):
    tm = 1024 if n % 1024 == 0 and row0 % 1024 == 0 else n
    blk0 = row0 // tm
    off = np.cumsum([0] + [RET_WIDTH] * 4 + [NSA_WIDTH] + [KV_WIDTH] * 6)
    w_ret = w_in[:, :off[4]].astype(BF16)
    w_nq = w_in[:, off[4]:off[5]].astype(BF16)
    w_kv = w_in[:, off[5]:off[11]].astype(BF16)
    wg = w_in[:, off[11]:].reshape(D_MODEL, NSA_KV_GROUPS, NSA_HPG * 3)
    wg = jnp.pad(wg, ((0, 0), (0, 0), (0, GATE_LANES - NSA_HPG * 3)))
    wg = wg.reshape(D_MODEL, NSA_KV_GROUPS * GATE_LANES).T.astype(BF16)

    lane = np.arange(LANES)
    half_r = RET_DIM // 2
    inv_r = (np.float32(RET_ROPE_BASE) ** (-np.arange(half_r, dtype=np.float32) / np.float32(half_r)))
    inv_r = inv_r.astype(np.float32)[lane % half_r][None, :]
    half_n = ROPE_DIMS // 2
    inv_n = (np.float32(ROPE_THETA) ** (-np.arange(half_n, dtype=np.float32) / np.float32(half_n)))
    jn = lane % NSA_DIM
    inv_n = np.where(jn < ROPE_DIMS, inv_n.astype(np.float32)[jn % half_n], np.float32(0.0))[None, :]

    row = lambda w: pl.BlockSpec((tm, w), lambda i: (i, 0))
    src_row = lambda w: pl.BlockSpec((tm, w), lambda i: (blk0 + i, 0))
    full = lambda a: pl.BlockSpec(a.shape, lambda i: (0,) * a.ndim)
    grp = lambda w: pl.BlockSpec((NSA_KV_GROUPS, tm, w), lambda i: (0, i, 0))
    bf = lambda w: jax.ShapeDtypeStruct((n, w), BF16)
    gbf = jax.ShapeDtypeStruct((NSA_KV_GROUPS, n, NSA_DIM), BF16)
    inv_r = jnp.asarray(inv_r, F32)
    inv_n = jnp.asarray(inv_n, F32)
    return pl.pallas_call(
        _inproj_kernel,
        grid=(n // tm,),
        in_specs=[src_row(D_MODEL), src_row(1), full(w_ret), full(w_nq), full(w_kv), full(wg),
                  full(inv_r), full(inv_n)],
        out_specs=[row(RET_WIDTH)] * 4 + [row(NSA_WIDTH)] * 2 + [row(KV_WIDTH)] * 2
                  + [grp(NSA_DIM)] * 4
                  + [pl.BlockSpec((NSA_KV_GROUPS, GATE_LANES, tm), lambda i: (0, 0, i))],
        out_shape=[bf(RET_WIDTH)] * 4 + [bf(NSA_WIDTH)] * 2
                  + [jax.ShapeDtypeStruct((n, KV_WIDTH), F32)] * 2 + [gbf] * 4
                  + [jax.ShapeDtypeStruct((NSA_KV_GROUPS, GATE_LANES, n), F32)],
        compiler_params=_params(1),
        name="inproj",
    )(x2d, pos_col, w_ret, w_nq, w_kv, wg, inv_r, inv_n)


def _retention_kernel(q_ref, k_ref, v_ref, g_ref, o_ref, state_ref):
    c = RET_CHUNK

    @pl.when(pl.program_id(1) == 0)
    def _():
        state_ref[...] = jnp.zeros_like(state_ref)

    row = lax.broadcasted_iota(jnp.int32, (c, c), 0)
    col = lax.broadcasted_iota(jnp.int32, (c, c), 1)
    rel = (row - col).astype(F32)
    idx = lax.broadcasted_iota(jnp.int32, (c, 1), 0).astype(F32)
    for h in range(RET_HEADS):
        log_g = float(np.log(np.float32(1.0) - np.float32(2.0) ** np.float32(-5.0 - h)))
        cols = slice(h * RET_DIM, (h + 1) * RET_DIM)
        dmask = jnp.where(rel >= 0, jnp.exp(log_g * jnp.maximum(rel, 0.0)), 0.0)
        zeta = jnp.exp(log_g * (c - 1.0 - idx))
        xi = jnp.exp(log_g * (idx + 1.0))
        for j in range(q_ref.shape[0] // c):
            rows = slice(j * c, (j + 1) * c)
            q = q_ref[rows, cols]
            k = k_ref[rows, cols]
            v = v_ref[rows, cols]
            scores = _dot_nt(q, k) * dmask
            inner = _dot(scores.astype(BF16), v)
            prev = state_ref[h]
            cross = _dot(q, prev.astype(BF16)) * xi
            kz = (k.astype(F32) * zeta).astype(BF16)
            kv = lax.dot_general(kz, v, (((0,), (0,)), ((), ())), preferred_element_type=F32)
            state_ref[h] = prev * float(np.exp(np.float32(log_g) * np.float32(c))) + kv
            o = inner + cross
            mu = jnp.mean(o, axis=-1, keepdims=True)
            d = o - mu
            var = jnp.mean(d * d, axis=-1, keepdims=True)
            o = d * lax.rsqrt(var + LN_EPS)
            o_ref[rows, cols] = (jax.nn.silu(g_ref[rows, cols].astype(F32)) * o).astype(BF16)


def _retention(rq, rk, rv, rg, batch, seq):
    per_step = RET_STEP_CHUNKS if (seq // RET_CHUNK) % RET_STEP_CHUNKS == 0 else 1
    nc = seq // (RET_CHUNK * per_step)
    spec = pl.BlockSpec((RET_CHUNK * per_step, RET_WIDTH), lambda b, n: (b * nc + n, 0))
    return pl.pallas_call(
        _retention_kernel,
        grid=(batch, nc),
        in_specs=[spec] * 4,
        out_specs=spec,
        out_shape=jax.ShapeDtypeStruct(rq.shape, BF16),
        scratch_shapes=[pltpu.VMEM((RET_HEADS, RET_DIM, RET_DIM), F32)],
        compiler_params=_params(2),
        name="retention",
    )(rq, rk, rv, rg)


def _compress_kernel(a_ref, pe_ref, w1_ref, w2_ref, o_ref, shift_ref, *, n_cmp):
    rows = a_ref.shape[0]
    a = a_ref[...]
    lo = (a + pe_ref[0]).astype(BF16)
    hi = (a + pe_ref[1]).astype(BF16)
    ridx = lax.broadcasted_iota(jnp.int32, (rows, 1), 0)
    shift_ref[rows:rows + 8, :] = jnp.zeros((8, CMP_HIDDEN), F32)
    for g in range(NSA_KV_GROUPS):
        p = _dot(lo, w1_ref[0, g])
        shift_ref[0:rows, :] = _dot(hi, w1_ref[1, g])
        hid = jax.nn.silu(p + shift_ref[pl.ds(1, rows), :])
        out = _dot(hid.astype(BF16), w2_ref[...])
        o_ref[g] = jnp.where(ridx < n_cmp, out, 0.0).astype(BF16)


def _compress(a, pe, w1, w2, batch, seq):
    rows = seq // CMP_STRIDE
    per = CMP_STRIDE * KV_WIDTH
    n_cmp = (seq - CMP_LEN) // CMP_STRIDE + 1
    a2 = a.reshape(batch * rows, per)
    pe2 = jnp.tile(pe.reshape(2, CMP_STRIDE, 1, NSA_DIM), (1, 1, NSA_KV_GROUPS, 1)).reshape(2, 1, per)
    w1r = w1.reshape(2, CMP_STRIDE, 1, NSA_DIM, CMP_HIDDEN)
    eye = jnp.eye(NSA_KV_GROUPS, dtype=w1.dtype).reshape(1, NSA_KV_GROUPS, 1, NSA_KV_GROUPS, 1, 1)
    w1x = (w1r[:, None] * eye).reshape(2, NSA_KV_GROUPS, per, CMP_HIDDEN).astype(BF16)
    w2b = w2.astype(BF16)
    full = lambda arr: pl.BlockSpec(arr.shape, lambda b: (0,) * arr.ndim)
    return pl.pallas_call(
        functools.partial(_compress_kernel, n_cmp=n_cmp),
        grid=(batch,),
        in_specs=[pl.BlockSpec((rows, per), lambda b: (b, 0)), full(pe2), full(w1x), full(w2b)],
        out_specs=pl.BlockSpec((None, NSA_KV_GROUPS, rows, NSA_DIM), lambda b: (b, 0, 0, 0)),
        out_shape=jax.ShapeDtypeStruct((batch, NSA_KV_GROUPS, rows, NSA_DIM), BF16),
        scratch_shapes=[pltpu.VMEM((rows + 8, CMP_HIDDEN), F32)],
        compiler_params=_params(1),
        name="compress",
    )(a2, pe2, w1x, w2b)


def _heads_to_lanes(ref):
    vt = ref[...].astype(F32).T
    return jnp.concatenate([vt[h * NSA_DIM:(h + 1) * NSA_DIM] for h in range(NSA_HPG)], axis=1).astype(BF16)


def _transpose_into(dst_ref, src_ref, chunk):
    def step(c, _):
        c0 = pl.multiple_of(c * chunk, chunk)
        dst_ref[:NSA_DIM, pl.ds(c0, chunk)] = src_ref[pl.ds(c0, chunk), :].astype(F32).T.astype(BF16)
        return 0
    lax.fori_loop(0, src_ref.shape[0] // chunk, step, 0)


def _nsa_kernel(qraw_ref, qrot_ref, gate_ref, kcmp_ref, vcmp_ref, ovt_ref,
                ks_ref, vs_ref, kw_ref, vw_ref, o_ref, vst_ref, vwt_ref, vct_ref, bias_ref, *, tq, tk, seq):
    i = pl.program_id(2)
    t0 = i * tq
    cols = NSA_HPG * tq
    n_sel = seq // SEL_LEN
    n_cmp_rows = seq // CMP_STRIDE
    blocks_per_tile = tk // SEL_LEN

    @pl.when(i == 0)
    def _():
        chunk = min(512, n_cmp_rows)
        _transpose_into(vst_ref, vs_ref, chunk)
        _transpose_into(vwt_ref, vw_ref, chunk)
        _transpose_into(vct_ref, vcmp_ref, chunk)
        vst_ref[NSA_DIM:, :] = jnp.ones((SUM_ROWS, seq), BF16)
        vwt_ref[NSA_DIM:, :] = jnp.ones((SUM_ROWS, seq), BF16)

    def split_sum(acc):
        return acc[:NSA_DIM] / acc[NSA_DIM:NSA_DIM + 1]

    q_raw = _heads_to_lanes(qraw_ref)
    q_rot = _heads_to_lanes(qrot_ref)
    t_row = t0 + lax.broadcasted_iota(jnp.int32, (1, tq), 1)

    chain_w = cols // NSA_CHAINS
    heads_per_chain = chain_w // tq
    chains = [slice(c * chain_w, (c + 1) * chain_w) for c in range(NSA_CHAINS)]
    tile_chain = lambda v: jnp.concatenate([v] * heads_per_chain, axis=1)

    pw = min(tq, WIN_PART)
    parts = []
    for u in range(tq // pw):
        span = WIN + pw
        ws = pl.multiple_of(jnp.maximum(t0 + u * pw - WIN, 0), pw)
        dist = t_row[:, u * pw:(u + 1) * pw] - (ws + lax.broadcasted_iota(jnp.int32, (span, 1), 0))
        bias_w = jnp.concatenate([jnp.where((dist >= 0) & (dist < WIN), 0.0, NEG)] * NSA_HPG, axis=1)
        q_part = jnp.concatenate([q_rot[:, h * tq + u * pw:h * tq + (u + 1) * pw] for h in range(NSA_HPG)],
                                 axis=1)
        s_w = _dot(kw_ref[pl.ds(ws, span), :], q_part) + bias_w
        p_w = jnp.exp2((s_w - jnp.max(s_w, axis=0, keepdims=True)).astype(BF16))
        parts.append(split_sum(_dot(vwt_ref[:, pl.ds(ws, span)], p_w)))
    o_w = jnp.concatenate([parts[u][:, h * pw:(h + 1) * pw]
                           for h in range(NSA_HPG) for u in range(tq // pw)], axis=1)

    c_idx = lax.broadcasted_iota(jnp.int32, (n_cmp_rows, 1), 0)
    bias_c = tile_chain(jnp.where(c_idx * CMP_STRIDE + (CMP_LEN - 1) <= t_row, 0.0, NEG))
    sees_any = tile_chain(jnp.where(t_row >= CMP_LEN - 1, 1.0, 0.0))
    o_c = []
    p_sum = None
    for c in chains:
        s_c = _dot(kcmp_ref[...], q_raw[:, c]) + bias_c
        e_c = jnp.exp2(s_c - jnp.max(s_c, axis=0, keepdims=True))
        p_c = e_c * (sees_any / jnp.sum(e_c, axis=0, keepdims=True))
        o_c.append(_dot(vct_ref[...], p_c.astype(BF16)))
        for h in range(heads_per_chain):
            p_h = p_c[:, h * tq:(h + 1) * tq]
            p_sum = p_h if p_sum is None else p_sum + p_h
    o_c = jnp.concatenate(o_c, axis=1)

    p_hi = p_sum.astype(BF16)
    p_lo = (p_sum - p_hi.astype(F32)).astype(BF16)
    ovt = ovt_ref[...]
    imp = _dot(ovt, p_hi) + _dot(ovt, p_lo)
    jb = lax.broadcasted_iota(jnp.int32, (n_sel, tq), 0)
    cur = (t0 + lax.broadcasted_iota(jnp.int32, (n_sel, tq), 1)) >> SEL_SHIFT
    forced = (jb == 0) | (jb == cur) | (jb == cur - 1)
    work = jnp.where(forced, -jnp.inf, imp)
    work = jnp.where(jb <= cur, work, NEG)
    sel_t = jnp.where(forced, 1.0, 0.0)
    for _ in range(max(min(SEL_TOPK, n_sel) - N_FORCED, 0)):
        best = jnp.max(work, axis=0, keepdims=True)
        first = jnp.min(jnp.where(work == best, jb, n_sel), axis=0, keepdims=True)
        hit = jb == first
        sel_t = jnp.where(hit, 1.0, sel_t)
        work = jnp.where(hit, -jnp.inf, work)
    bias_ref[...] = jnp.where(sel_t > 0.5, 0.0, NEG)

    def sel_tile(kt, carry, causal):
        k0 = pl.multiple_of(kt * tk, tk)
        bias = jnp.concatenate(
            [jnp.broadcast_to(bias_ref[pl.ds(kt * blocks_per_tile + j, 1), :], (SEL_LEN, tq))
             for j in range(blocks_per_tile)], axis=0)
        if causal:
            kpos = k0 + lax.broadcasted_iota(jnp.int32, (tk, 1), 0)
            bias = jnp.where(kpos <= t_row, bias, NEG)
        bias = tile_chain(bias)
        k_t = ks_ref[pl.ds(k0, tk), :]
        v_t = vst_ref[:, pl.ds(k0, tk)]
        out = []
        scores = [_dot(k_t, q_rot[:, c]) + bias for c in chains]
        for (m, acc), s in zip(carry, scores):
            m_new = jnp.maximum(m, jnp.max(s, axis=0, keepdims=True))
            p = jnp.exp2((s - m_new).astype(BF16))
            acc = jnp.exp2(m - m_new) * acc + _dot(v_t, p)
            out.append((m_new, acc))
        return tuple(out)

    n_full = t0 // tk
    init = tuple((jnp.full((1, chain_w), NEG, F32), jnp.zeros((NSA_DIM + SUM_ROWS, chain_w), F32))
                 for _ in chains)
    carry = lax.fori_loop(0, n_full, functools.partial(sel_tile, causal=False), init)
    for d in range(max(tq // tk, 1)):
        carry = sel_tile(n_full + d, carry, causal=True)
    o_s = jnp.concatenate([split_sum(acc) for _, acc in carry], axis=1)

    gt = gate_ref[...]
    outs = []
    for h in range(NSA_HPG):
        c = slice(h * tq, (h + 1) * tq)
        outs.append(gt[3 * h:3 * h + 1] * o_c[:, c] + gt[3 * h + 1:3 * h + 2] * o_s[:, c]
                    + gt[3 * h + 2:3 * h + 3] * o_w[:, c])
    o_ref[...] = jnp.concatenate(outs, axis=0).T.astype(BF16)


def _anchored(kernel_fn, n_inputs, n_anchors):
    def body(*refs, **static):
        kernel_fn(*refs[:n_inputs], *refs[n_inputs + n_anchors:], **static)
    return body


def _add_anchors(kernel_fn, args, in_specs, after):
    after = [a for a in (after or ()) if a is not None]
    if not after:
        return kernel_fn
    body = _anchored(kernel_fn, len(args), len(after))
    args.extend(after)
    in_specs.extend([pl.BlockSpec(memory_space=pl.ANY)] * len(after))
    return body


def _nsa(nq, nqr, gates, kcmp, vcmp, ks, vs, kw, vw, batch, seq, after=None):
    n = batch * seq
    tq = 512
    tk = 512 if seq % 512 == 0 else seq
    nqb = seq // tq
    n_sel = seq // SEL_LEN
    rows_c = seq // CMP_STRIDE
    gw = NSA_HPG * NSA_DIM
    cs = np.arange(rows_c)[None, :] * CMP_STRIDE
    ss = np.arange(n_sel)[:, None] * SEL_LEN
    n_cmp = (seq - CMP_LEN) // CMP_STRIDE + 1
    ovt = ((cs < ss + SEL_LEN) & (cs + CMP_LEN > ss) & (np.arange(rows_c)[None, :] < n_cmp))
    ovt = jnp.asarray(ovt.astype(np.float32), BF16)

    qspec = pl.BlockSpec((tq, gw), lambda b, g, i: (b * nqb + i, g))
    cspec = pl.BlockSpec((None, None, rows_c, NSA_DIM), lambda b, g, i: (b, g, 0, 0))
    kspec = pl.BlockSpec((None, seq, NSA_DIM), lambda b, g, i: (g, b, 0))
    args = [nq, nqr, gates, kcmp, vcmp, ovt, ks, vs, kw, vw]
    in_specs = [qspec, qspec,
                pl.BlockSpec((None, GATE_LANES, tq), lambda b, g, i: (g, 0, b * nqb + i)),
                cspec, cspec, pl.BlockSpec(ovt.shape, lambda b, g, i: (0, 0)),
                kspec, kspec, kspec, kspec]
    body = functools.partial(_add_anchors(_nsa_kernel, args, in_specs, after), tq=tq, tk=tk, seq=seq)
    return pl.pallas_call(
        body,
        grid=(batch, NSA_KV_GROUPS, nqb),
        in_specs=in_specs,
        out_specs=qspec,
        out_shape=jax.ShapeDtypeStruct((n, NSA_WIDTH), BF16),
        scratch_shapes=[pltpu.VMEM((NSA_DIM + SUM_ROWS, seq), BF16), pltpu.VMEM((NSA_DIM + SUM_ROWS, seq), BF16),
                        pltpu.VMEM((NSA_DIM, rows_c), BF16), pltpu.VMEM((n_sel, tq), F32)],
        compiler_params=_params(3),
        name="nsa",
    )(*args)


def _memkv_kernel(mem_ref, w_ref, kv_ref):
    kv_ref[...] = _dot(mem_ref[...].astype(BF16), w_ref[...]).astype(BF16)


def _memkv(mem2d, w_xkv):
    n = mem2d.shape[0]
    w = w_xkv.astype(BF16)
    return pl.pallas_call(
        _memkv_kernel,
        grid=(n // MEM_LEN,),
        in_specs=[pl.BlockSpec((MEM_LEN, D_MODEL), lambda i: (i, 0)),
                  pl.BlockSpec(w.shape, lambda i: (0, 0))],
        out_specs=pl.BlockSpec((MEM_LEN, 2 * D_MODEL), lambda i: (i, 0)),
        out_shape=jax.ShapeDtypeStruct((n, 2 * D_MODEL), BF16),
        compiler_params=_params(1),
        name="memkv",
    )(mem2d, w)


def _pack_halves(v):
    half = D_MODEL // 2
    hi = pltpu.bitcast(v[:, :half].astype(BF16).astype(F32), jnp.uint32)
    lo = pltpu.bitcast(v[:, half:].astype(BF16).astype(F32), jnp.uint32)
    return hi | (lo >> 16)


def _unpack_halves(words):
    return pltpu.bitcast(words & jnp.uint32(0xFFFF0000), F32), pltpu.bitcast(words << 16, F32)


def _postmix_kernel(x_ref, oret_ref, onsa_ref, kv_ref, wout_ref, wq_ref, wo_ref,
                    g1_ref, b1_ref, g2_ref, b2_ref, x2_ref, x2p_ref):
    mixed = jnp.concatenate([oret_ref[...], onsa_ref[...]], axis=1)
    x1 = _layer_norm(DN_ALPHA * x_ref[...] + _dot(mixed, wout_ref[...]), g1_ref[...], b1_ref[...])
    q = (_dot(x1.astype(BF16), wq_ref[...]) * (XATT_DIM ** -0.5 * LOG2_E)).astype(BF16)
    heads = []
    for h in range(XATT_HEADS):
        cols = slice(h * XATT_DIM, (h + 1) * XATT_DIM)
        s = _dot_nt(q[:, cols], kv_ref[:, cols])
        m = jnp.max(s, axis=-1, keepdims=True)
        p = jnp.exp2(s - m)
        l = jnp.sum(p, axis=-1, keepdims=True)
        heads.append(_dot(p.astype(BF16), kv_ref[:, D_MODEL + h * XATT_DIM:D_MODEL + (h + 1) * XATT_DIM]) / l)
    att = jnp.concatenate(heads, axis=1).astype(BF16)
    x2 = _layer_norm(DN_ALPHA * x1 + _dot(att, wo_ref[...]), g2_ref[...], b2_ref[...])
    x2_ref[...] = x2
    x2p_ref[...] = _pack_halves(x2)


def _postmix(x2d, o_ret, o_nsa, kvx, w_out, w_xq, w_xo, ln1_g, ln1_b, ln2_g, ln2_b, batch0, batch, seq,
             after=None):
    n = batch * seq
    tm = 512 if seq % 512 == 0 else seq
    per_b = seq // tm
    row = lambda w: pl.BlockSpec((tm, w), lambda b, i: (b * per_b + i, 0))
    full = lambda a: pl.BlockSpec(a.shape, lambda b, i: (0,) * a.ndim)
    ws = [w_out.astype(BF16), w_xq.astype(BF16), w_xo.astype(BF16)]
    vecs = [v.reshape(1, D_MODEL) for v in (ln1_g, ln1_b, ln2_g, ln2_b)]
    args = [x2d, o_ret, o_nsa, kvx, *ws, *vecs]
    in_specs = ([pl.BlockSpec((tm, D_MODEL), lambda b, i: ((batch0 + b) * per_b + i, 0)),
                 row(RET_WIDTH), row(NSA_WIDTH),
                 pl.BlockSpec((MEM_LEN, 2 * D_MODEL), lambda b, i: (batch0 + b, 0))]
                + [full(w) for w in ws] + [full(v) for v in vecs])
    return pl.pallas_call(
        _add_anchors(_postmix_kernel, args, in_specs, after),
        grid=(batch, per_b),
        in_specs=in_specs,
        out_specs=[row(D_MODEL),
                   row(D_MODEL // 2)],
        out_shape=[jax.ShapeDtypeStruct((n, D_MODEL), F32),
                   jax.ShapeDtypeStruct((n, D_MODEL // 2), jnp.uint32)],
        compiler_params=_params(2),
        name="postmix",
    )(*args)


def _router_kernel(x_ref, wr_ref, bias_ref, e_ref, rank_ref, w_ref, cnt_ref, cntrow_ref, carry_ref, carryrow_ref):
    tn = x_ref.shape[0]
    per = N_EXPERTS // N_GROUPS

    @pl.when(pl.program_id(0) == 0)
    def _():
        carry_ref[...] = jnp.zeros_like(carry_ref)
        carryrow_ref[...] = jnp.zeros_like(carryrow_ref)

    logits = _dot_nt(wr_ref[...], x_ref[...].astype(BF16))
    scores = jax.nn.sigmoid(logits)
    biased = scores + bias_ref[...]
    b3 = biased.reshape(N_GROUPS, per, tn)
    member = lax.broadcasted_iota(jnp.int32, (N_GROUPS, per, tn), 1)
    top1 = jnp.max(b3, axis=1, keepdims=True)
    first1 = jnp.min(jnp.where(b3 == top1, member, per), axis=1, keepdims=True)
    top2 = jnp.max(jnp.where(member == first1, -jnp.inf, b3), axis=1, keepdims=True)
    gscore = top1 + top2
    gidx = lax.broadcasted_iota(jnp.int32, (N_GROUPS, 1, tn), 0)
    gwork = gscore
    for _ in range(TOPK_GROUPS - 1):
        gbest = jnp.max(gwork, axis=0, keepdims=True)
        gfirst = jnp.min(jnp.where(gwork == gbest, gidx, N_GROUPS), axis=0, keepdims=True)
        gwork = jnp.where(gidx == gfirst, -jnp.inf, gwork)
    kth = jnp.max(gwork, axis=0, keepdims=True)
    work = jnp.where(gscore >= kth, b3, NEG).reshape(N_EXPERTS, tn)
    eidx = lax.broadcasted_iota(jnp.int32, (N_EXPERTS, tn), 0)
    picks = []
    chosen = jnp.zeros((N_EXPERTS, tn), F32)
    for _ in range(TOP_K):
        best = jnp.max(work, axis=0, keepdims=True)
        first = jnp.min(jnp.where(work == best, eidx, N_EXPERTS), axis=0, keepdims=True)
        hit = eidx == first
        picks.append((first, hit))
        chosen = jnp.where(hit, 1.0, chosen)
        work = jnp.where(hit, -jnp.inf, work)

    r_i = lax.broadcasted_iota(jnp.int32, (tn, tn), 0)
    c_i = lax.broadcasted_iota(jnp.int32, (tn, tn), 1)
    before = jnp.where(r_i < c_i, 1.0, 0.0).astype(BF16)
    chosen_b = chosen.astype(BF16)
    rank = _dot(chosen_b, before) + carry_ref[...]
    carry_ref[...] = carry_ref[...] + jnp.sum(chosen, axis=1, keepdims=True)
    carryrow_ref[...] = carryrow_ref[...] + _dot_nt(jnp.ones((8, tn), BF16), chosen_b)
    cnt_ref[...] = carry_ref[...]
    cntrow_ref[...] = carryrow_ref[...]

    wsel = [jnp.sum(jnp.where(hit, scores, 0.0), axis=0, keepdims=True) for _, hit in picks]
    wsum = wsel[0]
    for v in wsel[1:]:
        wsum = wsum + v
    for kk, (first, hit) in enumerate(picks):
        e_ref[kk:kk + 1, :] = first
        rank_ref[kk:kk + 1, :] = jnp.sum(jnp.where(hit, rank, 0.0), axis=0, keepdims=True).astype(jnp.int32)
        w_ref[kk:kk + 1, :] = wsel[kk] / wsum * ROUTED_SCALE


def _router(x2, w_router, router_bias):
    n = x2.shape[0]
    tn = 512 if n % 512 == 0 else n
    wr_t = w_router.T.astype(BF16)
    bias = router_bias.reshape(N_EXPERTS, 1).astype(F32)
    kspec = pl.BlockSpec((TOP_K, tn), lambda i: (0, i))
    return pl.pallas_call(
        _router_kernel,
        grid=(n // tn,),
        in_specs=[pl.BlockSpec((tn, D_MODEL), lambda i: (i, 0)),
                  pl.BlockSpec(wr_t.shape, lambda i: (0, 0)),
                  pl.BlockSpec(bias.shape, lambda i: (0, 0))],
        out_specs=[kspec, kspec, kspec, pl.BlockSpec((N_EXPERTS, 1), lambda i: (0, 0)),
                   pl.BlockSpec((8, N_EXPERTS), lambda i: (0, 0))],
        out_shape=[jax.ShapeDtypeStruct((TOP_K, n), jnp.int32),
                   jax.ShapeDtypeStruct((TOP_K, n), jnp.int32),
                   jax.ShapeDtypeStruct((TOP_K, n), F32),
                   jax.ShapeDtypeStruct((N_EXPERTS, 1), F32),
                   jax.ShapeDtypeStruct((8, N_EXPERTS), F32)],
        scratch_shapes=[pltpu.VMEM((N_EXPERTS, 1), F32), pltpu.VMEM((8, N_EXPERTS), F32)],
        compiler_params=_params(1),
        name="router",
    )(x2, wr_t, bias)


def _slots_kernel(e_ref, rank_ref, cnt_ref, cntrow_ref, dest_ref, blk_e_ref, valid_ref, used_ref,
                  *, blk, n_blocks):
    pad = lambda c: jnp.ceil(c / blk) * blk
    cnt = cnt_ref[...]
    padded = pad(cnt)
    padded_row = pad(cntrow_ref[0:1, :])
    r_i = lax.broadcasted_iota(jnp.int32, (N_EXPERTS, N_EXPERTS), 0)
    c_i = lax.broadcasted_iota(jnp.int32, (N_EXPERTS, N_EXPERTS), 1)
    start = jnp.sum(jnp.where(c_i < r_i, padded_row, 0.0), axis=1, keepdims=True)
    end = start + padded
    e = e_ref[...]
    dest = rank_ref[...]
    for ex in range(N_EXPERTS):
        dest = dest + jnp.where(e == ex, start[ex:ex + 1, :].astype(jnp.int32), 0)
    dest_ref[...] = dest
    bstart = (lax.broadcasted_iota(jnp.int32, (1, n_blocks), 1) * blk).astype(F32)
    owner = jnp.sum(jnp.where(end <= bstart, 1.0, 0.0), axis=0, keepdims=True)
    blk_e_ref[...] = jnp.minimum(owner, N_EXPERTS - 1.0).astype(jnp.int32)
    inside = (start <= bstart) & (bstart < end)
    real = jnp.clip(start + cnt - bstart, 0.0, float(blk))
    valid = jnp.sum(jnp.where(inside, real, 0.0), axis=0, keepdims=True)
    valid_ref[...] = valid.astype(jnp.int32)
    used_ref[...] = jnp.sum(jnp.where(valid > 0.0, 1.0, 0.0), axis=1, keepdims=True).astype(jnp.int32)


def _slots(e_k, rank_k, counts, counts_row, blk, n_blocks):
    n = e_k.shape[1]
    full = lambda shape: pl.BlockSpec(shape, lambda: (0,) * len(shape))
    return pl.pallas_call(
        functools.partial(_slots_kernel, blk=blk, n_blocks=n_blocks),
        in_specs=[full((TOP_K, n)), full((TOP_K, n)), full((N_EXPERTS, 1)), full((8, N_EXPERTS))],
        out_specs=[full((TOP_K, n)), full((1, n_blocks)), full((1, n_blocks)), full((1, 1))],
        out_shape=[jax.ShapeDtypeStruct((TOP_K, n), jnp.int32),
                   jax.ShapeDtypeStruct((1, n_blocks), jnp.int32),
                   jax.ShapeDtypeStruct((1, n_blocks), jnp.int32),
                   jax.ShapeDtypeStruct((1, 1), jnp.int32)],
        compiler_params=pltpu.CompilerParams(vmem_limit_bytes=VMEM_LIMIT),
        name="slots",
    )(e_k, rank_k, counts, counts_row)


def _sc_worker_base(per_worker):
    return (lax.axis_index("s") * SC_CORES + lax.axis_index("c")) * per_worker


def _sc_scatter_rows(rows, idx, n_out):
    n, width = rows.shape
    k_lists = idx.shape[0] // n
    workers = SC_CORES * SC_SUBCORES
    per_worker = n // workers
    assert per_worker * workers == n and per_worker % (SC_CHUNK * SC_INFLIGHT) == 0
    mesh = plsc.VectorSubcoreMesh(core_axis_name="c", subcore_axis_name="s")
    lanes = range(SC_INFLIGHT)

    @functools.partial(
        pl.kernel, mesh=mesh,
        out_type=jax.ShapeDtypeStruct((n_out, width), rows.dtype),
        scratch_types=[pltpu.VMEM((SC_CHUNK, width), rows.dtype)] * SC_INFLIGHT
                      + [pltpu.VMEM((SC_CHUNK,), jnp.int32)] * (k_lists * SC_INFLIGHT)
                      + [pltpu.SemaphoreType.DMA] * (2 * SC_INFLIGHT),
        name="sc_scatter")
    def scatter(rows_hbm, idx_hbm, out_hbm, *scratch):
        rows_vs = scratch[:SC_INFLIGHT]
        idx_vs = [scratch[SC_INFLIGHT + j * k_lists:SC_INFLIGHT + (j + 1) * k_lists] for j in lanes]
        sems = scratch[SC_INFLIGHT * (1 + k_lists):]
        sem_in, sem_out = sems[:SC_INFLIGHT], sems[SC_INFLIGHT:]
        base = _sc_worker_base(per_worker)

        @pl.loop(0, per_worker // (SC_CHUNK * SC_INFLIGHT))
        def _(gi):
            offs = [pl.multiple_of(base + (gi * SC_INFLIGHT + j) * SC_CHUNK, SC_CHUNK) for j in lanes]
            loads = []
            for j in lanes:
                loads.append([pltpu.async_copy(rows_hbm.at[pl.ds(offs[j], SC_CHUNK)], rows_vs[j], sem_in[j])]
                             + [pltpu.async_copy(
                                 idx_hbm.at[pl.ds(pl.multiple_of(k * n + offs[j], SC_CHUNK), SC_CHUNK)],
                                 idx_vs[j][k], sem_in[j]) for k in range(k_lists)])
            copies = []
            for j in lanes:
                for c in loads[j]:
                    c.wait()
                copies.append([pltpu.async_copy(rows_vs[j], out_hbm.at[idx_vs[j][k]], sem_out[j])
                               for k in range(k_lists)])
            for j in lanes:
                for c in copies[j]:
                    c.wait()

    return scatter(rows, idx)


def _experts_kernel(blk_e_ref, valid_ref, used_ref, xs_ref, wg_ref, wu_ref, wd_ref, y_ref, wg_b, wu_b, wd_b):
    del used_ref
    i = pl.program_id(0)
    valid = valid_ref[i]

    @pl.when((i == 0) | (blk_e_ref[i] != blk_e_ref[jnp.maximum(i - 1, 0)]))
    def _():
        wg_b[...] = wg_ref[...].astype(BF16)
        wu_b[...] = wu_ref[...].astype(BF16)
        wd_b[...] = wd_ref[...].astype(BF16)

    @pl.when(valid > 0)
    def _():
        half = D_MODEL // 2
        row = lax.broadcasted_iota(jnp.int32, (xs_ref.shape[0], 1), 0)
        hi, lo = (v.astype(BF16) for v in _unpack_halves(jnp.where(row < valid, xs_ref[...], jnp.uint32(0))))
        gate = _dot(hi, wg_b[:half, :]) + _dot(lo, wg_b[half:, :])
        up = _dot(hi, wu_b[:half, :]) + _dot(lo, wu_b[half:, :])
        y_ref[...] = _pack_halves(_dot((jax.nn.silu(gate) * up).astype(BF16), wd_b[...]))

    @pl.when(valid <= 0)
    def _():
        y_ref[...] = jnp.zeros_like(y_ref)


def _experts(blk_e, valid, n_used, xs, w_gate, w_up, w_down, blk, after=None):
    cap, width = xs.shape
    wspec = lambda a: pl.BlockSpec((None,) + a.shape[1:], lambda i, be, nv, nu: (be[i], 0, 0))
    rows = pl.BlockSpec((blk, width), lambda i, be, nv, nu: (jnp.minimum(i, nu[0]), 0))
    args = [blk_e, valid, n_used, xs, w_gate, w_up, w_down]
    in_specs = [rows, wspec(w_gate), wspec(w_up), wspec(w_down)]
    body = _add_anchors(_experts_kernel, args, in_specs, after)
    return pl.pallas_call(
        body,
        grid_spec=pltpu.PrefetchScalarGridSpec(
            num_scalar_prefetch=3,
            grid=(cap // blk,),
            in_specs=in_specs,
            out_specs=rows,
            scratch_shapes=[pltpu.VMEM(w.shape[1:], BF16) for w in (w_gate, w_up, w_down)],
        ),
        out_shape=jax.ShapeDtypeStruct(xs.shape, xs.dtype),
        compiler_params=_params(1),
        name="experts",
    )(*args)


def _sc_gather_rows(table, idx):
    b, width = idx.shape[0], table.shape[1]
    workers = SC_CORES * SC_SUBCORES
    per_worker = b // workers
    assert per_worker * workers == b and per_worker % (SC_CHUNK * SC_INFLIGHT) == 0
    mesh = plsc.VectorSubcoreMesh(core_axis_name="c", subcore_axis_name="s")

    @functools.partial(
        pl.kernel, mesh=mesh,
        out_type=jax.ShapeDtypeStruct((b, width), table.dtype),
        scratch_types=[pltpu.VMEM((SC_CHUNK,), jnp.int32)] * SC_INFLIGHT
                      + [pltpu.VMEM((SC_CHUNK, width), table.dtype)] * SC_INFLIGHT
                      + [pltpu.SemaphoreType.DMA] * (1 + 2 * SC_INFLIGHT),
        name="sc_gather")
    def gather(table_hbm, idx_hbm, out_hbm, *scratch):
        idx_vs = scratch[:SC_INFLIGHT]
        rows_vs = scratch[SC_INFLIGHT:2 * SC_INFLIGHT]
        sem_idx = scratch[2 * SC_INFLIGHT]
        sem_rows = scratch[2 * SC_INFLIGHT + 1:3 * SC_INFLIGHT + 1]
        sem_out = scratch[3 * SC_INFLIGHT + 1:]
        base = _sc_worker_base(per_worker)
        lanes = range(SC_INFLIGHT)

        @pl.loop(0, per_worker // (SC_CHUNK * SC_INFLIGHT))
        def _(gi):
            offs = [pl.multiple_of(base + (gi * SC_INFLIGHT + j) * SC_CHUNK, SC_CHUNK) for j in lanes]
            loads = [pltpu.async_copy(idx_hbm.at[pl.ds(offs[j], SC_CHUNK)], idx_vs[j], sem_idx) for j in lanes]
            for c in loads:
                c.wait()
            gathers = [pltpu.async_copy(table_hbm.at[idx_vs[j]], rows_vs[j], sem_rows[j]) for j in lanes]
            writes = []
            for j in lanes:
                gathers[j].wait()
                writes.append(pltpu.async_copy(rows_vs[j], out_hbm.at[pl.ds(offs[j], SC_CHUNK)], sem_out[j]))
            for c in writes:
                c.wait()

    return gather(table, idx)


def _combine_kernel(x_ref, wk_ref, yk_ref, wsg_ref, wsu_ref, wsd_ref, g_ref, b_ref, *rest):
    o_ref = rest[-1]
    x = x_ref[...]
    xb = x.astype(BF16)
    shared = _dot((jax.nn.silu(_dot(xb, wsg_ref[...])) * _dot(xb, wsu_ref[...])).astype(BF16), wsd_ref[...])
    wk = wk_ref[...]
    routed_hi = routed_lo = None
    for kk in range(TOP_K):
        hi, lo = _unpack_halves(yk_ref[kk])
        w = wk[:, kk:kk + 1]
        routed_hi = hi * w if kk == 0 else routed_hi + hi * w
        routed_lo = lo * w if kk == 0 else routed_lo + lo * w
    routed = jnp.concatenate([routed_hi, routed_lo], axis=1)
    o_ref[...] = _layer_norm(DN_ALPHA * x + (routed + shared), g_ref[...], b_ref[...])


def _combine(x2, w_tok, yk, ws_gate, ws_up, ws_down, ln3_g, ln3_b, row0, n_total, out_prev, after=None,
             part=(0, 1)):
    n = x2.shape[0] // part[1]
    sub0 = part[0] * n
    tt = 512 if n % 512 == 0 and row0 % 512 == 0 else n
    blk0 = (row0 + sub0) // tt
    sub_blk = sub0 // tt
    ws = [ws_gate.astype(BF16), ws_up.astype(BF16), ws_down.astype(BF16)]
    vecs = [ln3_g.reshape(1, D_MODEL), ln3_b.reshape(1, D_MODEL)]
    full = lambda a: pl.BlockSpec(a.shape, lambda i: (0,) * a.ndim)
    args = [x2, w_tok, yk, *ws, *vecs]
    in_specs = ([pl.BlockSpec((tt, D_MODEL), lambda i: (sub_blk + i, 0)),
                 pl.BlockSpec((tt, TOP_K), lambda i: (sub_blk + i, 0)),
                 pl.BlockSpec((TOP_K, tt, D_MODEL // 2), lambda i: (0, sub_blk + i, 0))]
                + [full(a) for a in ws] + [full(v) for v in vecs])
    aliases = {}
    for anchor in (a for a in (after or ()) if a is not None):
        args.append(anchor)
        in_specs.append(pl.BlockSpec(memory_space=pl.ANY))
    if out_prev is not None:
        aliases = {len(args): 0}
        args.append(out_prev)
        in_specs.append(pl.BlockSpec(memory_space=pl.ANY))
    return pl.pallas_call(
        _combine_kernel,
        grid=(n // tt,),
        in_specs=in_specs,
        out_specs=pl.BlockSpec((tt, D_MODEL), lambda i: (blk0 + i, 0)),
        out_shape=jax.ShapeDtypeStruct((n_total, D_MODEL), F32),
        input_output_aliases=aliases,
        compiler_params=_params(1),
        name="combine",
    )(*args)


def _expert_block(n):
    return 1024 if n * TOP_K // N_EXPERTS >= 2048 else 512


def _moe_dispatch(x2, x2p, w_router, router_bias):
    n = x2.shape[0]
    blk = _expert_block(n)
    cap = n * TOP_K + N_EXPERTS * blk
    e_k, rank_k, w_k, counts, counts_row = _router(x2, w_router, router_bias)
    dest, blk_e, valid, n_used = _slots(e_k, rank_k, counts, counts_row, blk, cap // blk)
    dest = dest.reshape(-1)
    return dict(w_tok=w_k.T, dest=dest, blk_e=blk_e.reshape(-1), valid=valid.reshape(-1),
                n_used=n_used.reshape(-1), xs=_sc_scatter_rows(x2p, dest, cap))


def _moe_experts(routed, w_gate, w_up, w_down, after):
    y = _experts(routed["blk_e"], routed["valid"], routed["n_used"], routed["xs"], w_gate, w_up, w_down,
                 _expert_block(routed["dest"].shape[0] // TOP_K), after=after)
    n = routed["dest"].shape[0] // TOP_K
    return y, _sc_gather_rows(y, routed["dest"]).reshape(TOP_K, n, D_MODEL // 2)


def _layer(x, mem, positions, w_in, cmp_pe_k, cmp_pe_v, cmp_w1_k, cmp_w2_k, cmp_w1_v, cmp_w2_v,
           w_out, ln1_g, ln1_b, w_xq, w_xkv, w_xo, ln2_g, ln2_b, w_router, router_bias,
           w_gate, w_up, w_down, ws_gate, ws_up, ws_down, ln3_g, ln3_b):
    batch, seq, _ = x.shape
    n_total = batch * seq
    x2d = x.reshape(n_total, D_MODEL)
    pos_col = positions.astype(F32).reshape(n_total, 1)
    kvx = _memkv(mem.reshape(batch * MEM_LEN, D_MODEL), w_xkv)
    last = max(1, batch // 4)
    sizes = [batch - last, last] if batch > 1 else [batch]
    starts = [sum(sizes[:g]) for g in range(len(sizes))]

    def mixers_in(g):
        nb, row0 = sizes[g], starts[g] * seq
        (rq, rk, rv, rg, nq, nqr, kc, vc, ks, vs, kw, vw, gates) = _inproj(x2d, pos_col, w_in, row0, nb * seq)
        o_ret = _retention(rq, rk, rv, rg, nb, seq)
        kcmp = _compress(kc, cmp_pe_k, cmp_w1_k, cmp_w2_k, nb, seq)
        vcmp = _compress(vc, cmp_pe_v, cmp_w1_v, cmp_w2_v, nb, seq)
        return o_ret, (nq, nqr, gates, kcmp, vcmp, ks, vs, kw, vw)

    def attend(g, nsa_args, after):
        return _nsa(*nsa_args, sizes[g], seq, after=after)

    def mix_and_route(g, o_ret, o_nsa, after):
        x2, x2p = _postmix(x2d, o_ret, o_nsa, kvx, w_out, w_xq, w_xo, ln1_g, ln1_b, ln2_g, ln2_b,
                           starts[g], sizes[g], seq, after=after)
        return x2, _moe_dispatch(x2, x2p, w_router, router_bias)

    def combine(g, x2, routed, yk, out_prev, after=None, part=(0, 1)):
        return _combine(x2, routed["w_tok"], yk, ws_gate, ws_up, ws_down, ln3_g, ln3_b, starts[g] * seq,
                        n_total, out_prev, after=after, part=part)

    o_ret, nsa_args = mixers_in(0)
    x2, routed = mix_and_route(0, o_ret, attend(0, nsa_args, None), None)
    if len(sizes) == 1:
        y, yk = _moe_experts(routed, w_gate, w_up, w_down, after=None)
        return combine(0, x2, routed, yk, None).reshape(batch, seq, D_MODEL)
    o_ret1, nsa_args1 = mixers_in(1)
    y, yk = _moe_experts(routed, w_gate, w_up, w_down, after=[o_ret1])
    o_nsa1 = attend(1, nsa_args1, [y])
    x2_1, routed1 = mix_and_route(1, o_ret1, o_nsa1, [yk])
    out = combine(0, x2, routed, yk, None, [routed1["dest"]], part=(0, 2))
    y1, yk1 = _moe_experts(routed1, w_gate, w_up, w_down, after=[out])
    out = combine(0, x2, routed, yk, out, [y1], part=(1, 2))
    out = combine(1, x2_1, routed1, yk1, out)
    return out.reshape(batch, seq, D_MODEL)


def kernel(x, mem, positions, w_in, cmp_pe_k, cmp_pe_v, cmp_w1_k, cmp_w2_k, cmp_w1_v, cmp_w2_v, w_out, ln1_g, ln1_b, w_xq, w_xkv, w_xo, ln2_g, ln2_b, w_router, router_bias, w_gate, w_up, w_down, ws_gate, ws_up, ws_down, ln3_g, ln3_b):
    for l in range(DEPTH):
        x = _layer(x, mem, positions, w_in[l], cmp_pe_k[l], cmp_pe_v[l], cmp_w1_k[l], cmp_w2_k[l],
                   cmp_w1_v[l], cmp_w2_v[l], w_out[l], ln1_g[l], ln1_b[l], w_xq[l], w_xkv[l],
                   w_xo[l], ln2_g[l], ln2_b[l], w_router[l], router_bias[l], w_gate[l], w_up[l],
                   w_down[l], ws_gate[l], ws_up[l], ws_down[l], ln3_g[l], ln3_b[l])
    return x
```

```python
import functools

import numpy as np
import jax
import jax.numpy as jnp
from jax import lax
from jax.experimental import pallas as pl
from jax.experimental.pallas import tpu as pltpu
from jax.experimental.pallas import tpu_sc as plsc

D_MODEL = 1024
MEM_LEN = 256
DEPTH = 1
DN_ALPHA = (2 * DEPTH) ** 0.25
LN_EPS = 1e-5
NEG = -1e30

RET_HEADS = 4
RET_DIM = 128
RET_CHUNK = 128
RET_ROPE_BASE = 10000.0
RET_STEP_CHUNKS = 16
RET_WIDTH = RET_HEADS * RET_DIM

NSA_HEADS = 8
NSA_KV_GROUPS = 2
NSA_HPG = NSA_HEADS // NSA_KV_GROUPS
NSA_DIM = 64
NSA_WIDTH = NSA_HEADS * NSA_DIM
KV_WIDTH = NSA_KV_GROUPS * NSA_DIM
CMP_LEN = 32
CMP_STRIDE = 16
CMP_HIDDEN = 256
SEL_LEN = 64
SEL_SHIFT = 6
SEL_TOPK = 16
N_FORCED = 3
WIN = 512
ROPE_THETA = 500000.0
ROPE_DIMS = NSA_DIM // 4
GATE_LANES = 16
NSA_CHAINS = 1
LOG2_E = 1.4426950408889634
SUM_ROWS = 16
WIN_PART = 256

SC_CORES = 2
SC_SUBCORES = 16
SC_CHUNK = 64
SC_INFLIGHT = 2

XATT_HEADS = 4
XATT_DIM = D_MODEL // XATT_HEADS

N_EXPERTS = 64
TOP_K = 8
N_GROUPS = 8
TOPK_GROUPS = 4
EXPERT_FF = 256
SHARED_FF = 256
ROUTED_SCALE = 2.5

LANES = 128
VMEM_LIMIT = 56 * 1024 * 1024

F32 = jnp.float32
BF16 = jnp.bfloat16
NT_DIMS = (((1,), (1,)), ((), ()))


def _params(n_axes):
    return pltpu.CompilerParams(dimension_semantics=("arbitrary",) * n_axes,
                                vmem_limit_bytes=VMEM_LIMIT)


def _dot(a, b):
    return jnp.dot(a, b, preferred_element_type=F32)


def _dot_nt(a, b):
    return lax.dot_general(a, b, NT_DIMS, preferred_element_type=F32)


def _layer_norm(v, g, b):
    mu = jnp.mean(v, axis=-1, keepdims=True)
    d = v - mu
    var = jnp.mean(d * d, axis=-1, keepdims=True)
    return d * lax.rsqrt(var + LN_EPS) * g + b


def _inproj_kernel(x_ref, pos_ref, wret_ref, wnq_ref, wkv_ref, wg_ref, invr_ref, invn_ref,
                   rq_ref, rk_ref, rv_ref, rg_ref, nq_ref, nqr_ref, kc_ref, vc_ref,
                   ks_ref, vs_ref, kw_ref, vw_ref, gate_ref):
    xb = x_ref[...].astype(BF16)
    pos = pos_ref[...]
    lane = lax.broadcasted_iota(jnp.int32, (1, LANES), 1)

    ang = pos * invr_ref[...]
    cos_r = jnp.cos(ang)
    sin_r = jnp.sin(ang)
    sin_r = jnp.where(lane < RET_DIM // 2, -sin_r, sin_r)
    q_all = _dot(xb, wret_ref[:, :RET_WIDTH])
    k_all = _dot(xb, wret_ref[:, RET_WIDTH:2 * RET_WIDTH])
    for h in range(RET_HEADS):
        cols = slice(h * RET_DIM, (h + 1) * RET_DIM)
        q = q_all[:, cols]
        rq_ref[:, cols] = (q * cos_r + pltpu.roll(q, RET_DIM // 2, 1) * sin_r).astype(BF16)
        k = k_all[:, cols]
        k = (k * cos_r + pltpu.roll(k, RET_DIM // 2, 1) * sin_r) * (RET_DIM ** -0.5)
        rk_ref[:, cols] = k.astype(BF16)
    rv_ref[...] = _dot(xb, wret_ref[:, 2 * RET_WIDTH:3 * RET_WIDTH]).astype(BF16)
    rg_ref[...] = _dot(xb, wret_ref[:, 3 * RET_WIDTH:4 * RET_WIDTH]).astype(BF16)

    half = ROPE_DIMS // 2
    j = lane % NSA_DIM
    angn = pos * invn_ref[...]
    cos_n = jnp.cos(angn)
    sin_n = jnp.sin(angn)
    sin_lo = jnp.where(j < half, -sin_n, 0.0)
    sin_hi = jnp.where((j >= half) & (j < 2 * half), sin_n, 0.0)

    def rope_n(v):
        return v * cos_n + pltpu.roll(v, half, 1) * sin_hi + pltpu.roll(v, LANES - half, 1) * sin_lo

    scale = NSA_DIM ** -0.5 * LOG2_E
    nq_all = _dot(xb, wnq_ref[...])
    for c in range(NSA_WIDTH // LANES):
        cols = slice(c * LANES, (c + 1) * LANES)
        q = nq_all[:, cols]
        nq_ref[:, cols] = (q * scale).astype(BF16)
        nqr_ref[:, cols] = (rope_n(q) * scale).astype(BF16)

    kv_all = _dot(xb, wkv_ref[...])

    def kv(i):
        return kv_all[:, i * KV_WIDTH:(i + 1) * KV_WIDTH]

    def split_groups(ref, v):
        for g in range(NSA_KV_GROUPS):
            ref[g] = v[:, g * NSA_DIM:(g + 1) * NSA_DIM].astype(BF16)

    kc_ref[...] = kv(0)
    vc_ref[...] = kv(1)
    split_groups(ks_ref, rope_n(kv(2)))
    split_groups(vs_ref, kv(3))
    split_groups(kw_ref, rope_n(kv(4)))
    split_groups(vw_ref, kv(5))

    gt = jax.nn.sigmoid(_dot_nt(wg_ref[...], xb))
    for g in range(NSA_KV_GROUPS):
        gate_ref[g] = gt[g * GATE_LANES:(g + 1) * GATE_LANES, :]


def _inproj(x2d, pos_col, w_in, row0, n):
    tm = 1024 if n % 1024 == 0 and row0 % 1024 == 0 else n
    blk0 = row0 // tm
    off = np.cumsum([0] + [RET_WIDTH] * 4 + [NSA_WIDTH] + [KV_WIDTH] * 6)
    w_ret = w_in[:, :off[4]].astype(BF16)
    w_nq = w_in[:, off[4]:off[5]].astype(BF16)
    w_kv = w_in[:, off[5]:off[11]].astype(BF16)
    wg = w_in[:, off[11]:].reshape(D_MODEL, NSA_KV_GROUPS, NSA_HPG * 3)
    wg = jnp.pad(wg, ((0, 0), (0, 0), (0, GATE_LANES - NSA_HPG * 3)))
    wg = wg.reshape(D_MODEL, NSA_KV_GROUPS * GATE_LANES).T.astype(BF16)

    lane = np.arange(LANES)
    half_r = RET_DIM // 2
    inv_r = (np.float32(RET_ROPE_BASE) ** (-np.arange(half_r, dtype=np.float32) / np.float32(half_r)))
    inv_r = inv_r.astype(np.float32)[lane % half_r][None, :]
    half_n = ROPE_DIMS // 2
    inv_n = (np.float32(ROPE_THETA) ** (-np.arange(half_n, dtype=np.float32) / np.float32(half_n)))
    jn = lane % NSA_DIM
    inv_n = np.where(jn < ROPE_DIMS, inv_n.astype(np.float32)[jn % half_n], np.float32(0.0))[None, :]

    row = lambda w: pl.BlockSpec((tm, w), lambda i: (i, 0))
    src_row = lambda w: pl.BlockSpec((tm, w), lambda i: (blk0 + i, 0))
    full = lambda a: pl.BlockSpec(a.shape, lambda i: (0,) * a.ndim)
    grp = lambda w: pl.BlockSpec((NSA_KV_GROUPS, tm, w), lambda i: (0, i, 0))
    bf = lambda w: jax.ShapeDtypeStruct((n, w), BF16)
    gbf = jax.ShapeDtypeStruct((NSA_KV_GROUPS, n, NSA_DIM), BF16)
    inv_r = jnp.asarray(inv_r, F32)
    inv_n = jnp.asarray(inv_n, F32)
    return pl.pallas_call(
        _inproj_kernel,
        grid=(n // tm,),
        in_specs=[src_row(D_MODEL), src_row(1), full(w_ret), full(w_nq), full(w_kv), full(wg),
                  full(inv_r), full(inv_n)],
        out_specs=[row(RET_WIDTH)] * 4 + [row(NSA_WIDTH)] * 2 + [row(KV_WIDTH)] * 2
                  + [grp(NSA_DIM)] * 4
                  + [pl.BlockSpec((NSA_KV_GROUPS, GATE_LANES, tm), lambda i: (0, 0, i))],
        out_shape=[bf(RET_WIDTH)] * 4 + [bf(NSA_WIDTH)] * 2
                  + [jax.ShapeDtypeStruct((n, KV_WIDTH), F32)] * 2 + [gbf] * 4
                  + [jax.ShapeDtypeStruct((NSA_KV_GROUPS, GATE_LANES, n), F32)],
        compiler_params=_params(1),
        name="inproj",
    )(x2d, pos_col, w_ret, w_nq, w_kv, wg, inv_r, inv_n)


def _retention_kernel(q_ref, k_ref, v_ref, g_ref, o_ref, state_ref):
    c = RET_CHUNK

    @pl.when(pl.program_id(1) == 0)
    def _():
        state_ref[...] = jnp.zeros_like(state_ref)

    row = lax.broadcasted_iota(jnp.int32, (c, c), 0)
    col = lax.broadcasted_iota(jnp.int32, (c, c), 1)
    rel = (row - col).astype(F32)
    idx = lax.broadcasted_iota(jnp.int32, (c, 1), 0).astype(F32)
    for h in range(RET_HEADS):
        log_g = float(np.log(np.float32(1.0) - np.float32(2.0) ** np.float32(-5.0 - h)))
        cols = slice(h * RET_DIM, (h + 1) * RET_DIM)
        dmask = jnp.where(rel >= 0, jnp.exp(log_g * jnp.maximum(rel, 0.0)), 0.0)
        zeta = jnp.exp(log_g * (c - 1.0 - idx))
        xi = jnp.exp(log_g * (idx + 1.0))
        for j in range(q_ref.shape[0] // c):
            rows = slice(j * c, (j + 1) * c)
            q = q_ref[rows, cols]
            k = k_ref[rows, cols]
            v = v_ref[rows, cols]
            scores = _dot_nt(q, k) * dmask
            inner = _dot(scores.astype(BF16), v)
            prev = state_ref[h]
            cross = _dot(q, prev.astype(BF16)) * xi
            kz = (k.astype(F32) * zeta).astype(BF16)
            kv = lax.dot_general(kz, v, (((0,), (0,)), ((), ())), preferred_element_type=F32)
            state_ref[h] = prev * float(np.exp(np.float32(log_g) * np.float32(c))) + kv
            o = inner + cross
            mu = jnp.mean(o, axis=-1, keepdims=True)
            d = o - mu
            var = jnp.mean(d * d, axis=-1, keepdims=True)
            o = d * lax.rsqrt(var + LN_EPS)
            o_ref[rows, cols] = (jax.nn.silu(g_ref[rows, cols].astype(F32)) * o).astype(BF16)


def _retention(rq, rk, rv, rg, batch, seq):
    per_step = RET_STEP_CHUNKS if (seq // RET_CHUNK) % RET_STEP_CHUNKS == 0 else 1
    nc = seq // (RET_CHUNK * per_step)
    spec = pl.BlockSpec((RET_CHUNK * per_step, RET_WIDTH), lambda b, n: (b * nc + n, 0))
    return pl.pallas_call(
        _retention_kernel,
        grid=(batch, nc),
        in_specs=[spec] * 4,
        out_specs=spec,
        out_shape=jax.ShapeDtypeStruct(rq.shape, BF16),
        scratch_shapes=[pltpu.VMEM((RET_HEADS, RET_DIM, RET_DIM), F32)],
        compiler_params=_params(2),
        name="retention",
    )(rq, rk, rv, rg)


def _compress_kernel(a_ref, pe_ref, w1_ref, w2_ref, o_ref, shift_ref, *, n_cmp):
    rows = a_ref.shape[0]
    a = a_ref[...]
    lo = (a + pe_ref[0]).astype(BF16)
    hi = (a + pe_ref[1]).astype(BF16)
    ridx = lax.broadcasted_iota(jnp.int32, (rows, 1), 0)
    shift_ref[rows:rows + 8, :] = jnp.zeros((8, CMP_HIDDEN), F32)
    for g in range(NSA_KV_GROUPS):
        p = _dot(lo, w1_ref[0, g])
        shift_ref[0:rows, :] = _dot(hi, w1_ref[1, g])
        hid = jax.nn.silu(p + shift_ref[pl.ds(1, rows), :])
        out = _dot(hid.astype(BF16), w2_ref[...])
        o_ref[g] = jnp.where(ridx < n_cmp, out, 0.0).astype(BF16)


def _compress(a, pe, w1, w2, batch, seq):
    rows = seq // CMP_STRIDE
    per = CMP_STRIDE * KV_WIDTH
    n_cmp = (seq - CMP_LEN) // CMP_STRIDE + 1
    a2 = a.reshape(batch * rows, per)
    pe2 = jnp.tile(pe.reshape(2, CMP_STRIDE, 1, NSA_DIM), (1, 1, NSA_KV_GROUPS, 1)).reshape(2, 1, per)
    w1r = w1.reshape(2, CMP_STRIDE, 1, NSA_DIM, CMP_HIDDEN)
    eye = jnp.eye(NSA_KV_GROUPS, dtype=w1.dtype).reshape(1, NSA_KV_GROUPS, 1, NSA_KV_GROUPS, 1, 1)
    w1x = (w1r[:, None] * eye).reshape(2, NSA_KV_GROUPS, per, CMP_HIDDEN).astype(BF16)
    w2b = w2.astype(BF16)
    full = lambda arr: pl.BlockSpec(arr.shape, lambda b: (0,) * arr.ndim)
    return pl.pallas_call(
        functools.partial(_compress_kernel, n_cmp=n_cmp),
        grid=(batch,),
        in_specs=[pl.BlockSpec((rows, per), lambda b: (b, 0)), full(pe2), full(w1x), full(w2b)],
        out_specs=pl.BlockSpec((None, NSA_KV_GROUPS, rows, NSA_DIM), lambda b: (b, 0, 0, 0)),
        out_shape=jax.ShapeDtypeStruct((batch, NSA_KV_GROUPS, rows, NSA_DIM), BF16),
        scratch_shapes=[pltpu.VMEM((rows + 8, CMP_HIDDEN), F32)],
        compiler_params=_params(1),
        name="compress",
    )(a2, pe2, w1x, w2b)


def _heads_to_lanes(ref):
    vt = ref[...].astype(F32).T
    return jnp.concatenate([vt[h * NSA_DIM:(h + 1) * NSA_DIM] for h in range(NSA_HPG)], axis=1).astype(BF16)


def _transpose_into(dst_ref, src_ref, chunk):
    def step(c, _):
        c0 = pl.multiple_of(c * chunk, chunk)
        dst_ref[:NSA_DIM, pl.ds(c0, chunk)] = src_ref[pl.ds(c0, chunk), :].astype(F32).T.astype(BF16)
        return 0
    lax.fori_loop(0, src_ref.shape[0] // chunk, step, 0)


def _nsa_kernel(qraw_ref, qrot_ref, gate_ref, kcmp_ref, vcmp_ref, ovt_ref,
                ks_ref, vs_ref, kw_ref, vw_ref, o_ref, vst_ref, vwt_ref, vct_ref, bias_ref, *, tq, tk, seq):
    i = pl.program_id(2)
    t0 = i * tq
    cols = NSA_HPG * tq
    n_sel = seq // SEL_LEN
    n_cmp_rows = seq // CMP_STRIDE
    blocks_per_tile = tk // SEL_LEN

    @pl.when(i == 0)
    def _():
        chunk = min(512, n_cmp_rows)
        _transpose_into(vst_ref, vs_ref, chunk)
        _transpose_into(vwt_ref, vw_ref, chunk)
        _transpose_into(vct_ref, vcmp_ref, chunk)
        vst_ref[NSA_DIM:, :] = jnp.ones((SUM_ROWS, seq), BF16)
        vwt_ref[NSA_DIM:, :] = jnp.ones((SUM_ROWS, seq), BF16)

    def split_sum(acc):
        return acc[:NSA_DIM] / acc[NSA_DIM:NSA_DIM + 1]

    q_raw = _heads_to_lanes(qraw_ref)
    q_rot = _heads_to_lanes(qrot_ref)
    t_row = t0 + lax.broadcasted_iota(jnp.int32, (1, tq), 1)

    chain_w = cols // NSA_CHAINS
    heads_per_chain = chain_w // tq
    chains = [slice(c * chain_w, (c + 1) * chain_w) for c in range(NSA_CHAINS)]
    tile_chain = lambda v: jnp.concatenate([v] * heads_per_chain, axis=1)

    pw = min(tq, WIN_PART)
    parts = []
    for u in range(tq // pw):
        span = WIN + pw
        ws = pl.multiple_of(jnp.maximum(t0 + u * pw - WIN, 0), pw)
        dist = t_row[:, u * pw:(u + 1) * pw] - (ws + lax.broadcasted_iota(jnp.int32, (span, 1), 0))
        bias_w = jnp.concatenate([jnp.where((dist >= 0) & (dist < WIN), 0.0, NEG)] * NSA_HPG, axis=1)
        q_part = jnp.concatenate([q_rot[:, h * tq + u * pw:h * tq + (u + 1) * pw] for h in range(NSA_HPG)],
                                 axis=1)
        s_w = _dot(kw_ref[pl.ds(ws, span), :], q_part) + bias_w
        p_w = jnp.exp2((s_w - jnp.max(s_w, axis=0, keepdims=True)).astype(BF16))
        parts.append(split_sum(_dot(vwt_ref[:, pl.ds(ws, span)], p_w)))
    o_w = jnp.concatenate([parts[u][:, h * pw:(h + 1) * pw]
                           for h in range(NSA_HPG) for u in range(tq // pw)], axis=1)

    c_idx = lax.broadcasted_iota(jnp.int32, (n_cmp_rows, 1), 0)
    bias_c = tile_chain(jnp.where(c_idx * CMP_STRIDE + (CMP_LEN - 1) <= t_row, 0.0, NEG))
    sees_any = tile_chain(jnp.where(t_row >= CMP_LEN - 1, 1.0, 0.0))
    o_c = []
    p_sum = None
    for c in chains:
        s_c = _dot(kcmp_ref[...], q_raw[:, c]) + bias_c
        e_c = jnp.exp2(s_c - jnp.max(s_c, axis=0, keepdims=True))
        p_c = e_c * (sees_any / jnp.sum(e_c, axis=0, keepdims=True))
        o_c.append(_dot(vct_ref[...], p_c.astype(BF16)))
        for h in range(heads_per_chain):
            p_h = p_c[:, h * tq:(h + 1) * tq]
            p_sum = p_h if p_sum is None else p_sum + p_h
    o_c = jnp.concatenate(o_c, axis=1)

    p_hi = p_sum.astype(BF16)
    p_lo = (p_sum - p_hi.astype(F32)).astype(BF16)
    ovt = ovt_ref[...]
    imp = _dot(ovt, p_hi) + _dot(ovt, p_lo)
    jb = lax.broadcasted_iota(jnp.int32, (n_sel, tq), 0)
    cur = (t0 + lax.broadcasted_iota(jnp.int32, (n_sel, tq), 1)) >> SEL_SHIFT
    forced = (jb == 0) | (jb == cur) | (jb == cur - 1)
    work = jnp.where(forced, -jnp.inf, imp)
    work = jnp.where(jb <= cur, work, NEG)
    sel_t = jnp.where(forced, 1.0, 0.0)
    for _ in range(max(min(SEL_TOPK, n_sel) - N_FORCED, 0)):
        best = jnp.max(work, axis=0, keepdims=True)
        first = jnp.min(jnp.where(work == best, jb, n_sel), axis=0, keepdims=True)
        hit = jb == first
        sel_t = jnp.where(hit, 1.0, sel_t)
        work = jnp.where(hit, -jnp.inf, work)
    bias_ref[...] = jnp.where(sel_t > 0.5, 0.0, NEG)

    def sel_tile(kt, carry, causal):
        k0 = pl.multiple_of(kt * tk, tk)
        bias = jnp.concatenate(
            [jnp.broadcast_to(bias_ref[pl.ds(kt * blocks_per_tile + j, 1), :], (SEL_LEN, tq))
             for j in range(blocks_per_tile)], axis=0)
        if causal:
            kpos = k0 + lax.broadcasted_iota(jnp.int32, (tk, 1), 0)
            bias = jnp.where(kpos <= t_row, bias, NEG)
        bias = tile_chain(bias)
        k_t = ks_ref[pl.ds(k0, tk), :]
        v_t = vst_ref[:, pl.ds(k0, tk)]
        out = []
        scores = [_dot(k_t, q_rot[:, c]) + bias for c in chains]
        for (m, acc), s in zip(carry, scores):
            m_new = jnp.maximum(m, jnp.max(s, axis=0, keepdims=True))
            p = jnp.exp2((s - m_new).astype(BF16))
            acc = jnp.exp2(m - m_new) * acc + _dot(v_t, p)
            out.append((m_new, acc))
        return tuple(out)

    n_full = t0 // tk
    init = tuple((jnp.full((1, chain_w), NEG, F32), jnp.zeros((NSA_DIM + SUM_ROWS, chain_w), F32))
                 for _ in chains)
    carry = lax.fori_loop(0, n_full, functools.partial(sel_tile, causal=False), init)
    for d in range(max(tq // tk, 1)):
        carry = sel_tile(n_full + d, carry, causal=True)
    o_s = jnp.concatenate([split_sum(acc) for _, acc in carry], axis=1)

    gt = gate_ref[...]
    outs = []
    for h in range(NSA_HPG):
        c = slice(h * tq, (h + 1) * tq)
        outs.append(gt[3 * h:3 * h + 1] * o_c[:, c] + gt[3 * h + 1:3 * h + 2] * o_s[:, c]
                    + gt[3 * h + 2:3 * h + 3] * o_w[:, c])
    o_ref[...] = jnp.concatenate(outs, axis=0).T.astype(BF16)


def _anchored(kernel_fn, n_inputs, n_anchors):
    def body(*refs, **static):
        kernel_fn(*refs[:n_inputs], *refs[n_inputs + n_anchors:], **static)
    return body


def _add_anchors(kernel_fn, args, in_specs, after):
    after = [a for a in (after or ()) if a is not None]
    if not after:
        return kernel_fn
    body = _anchored(kernel_fn, len(args), len(after))
    args.extend(after)
    in_specs.extend([pl.BlockSpec(memory_space=pl.ANY)] * len(after))
    return body


def _nsa(nq, nqr, gates, kcmp, vcmp, ks, vs, kw, vw, batch, seq, after=None):
    n = batch * seq
    tq = 512
    tk = 512 if seq % 512 == 0 else seq
    nqb = seq // tq
    n_sel = seq // SEL_LEN
    rows_c = seq // CMP_STRIDE
    gw = NSA_HPG * NSA_DIM
    cs = np.arange(rows_c)[None, :] * CMP_STRIDE
    ss = np.arange(n_sel)[:, None] * SEL_LEN
    n_cmp = (seq - CMP_LEN) // CMP_STRIDE + 1
    ovt = ((cs < ss + SEL_LEN) & (cs + CMP_LEN > ss) & (np.arange(rows_c)[None, :] < n_cmp))
    ovt = jnp.asarray(ovt.astype(np.float32), BF16)

    qspec = pl.BlockSpec((tq, gw), lambda b, g, i: (b * nqb + i, g))
    cspec = pl.BlockSpec((None, None, rows_c, NSA_DIM), lambda b, g, i: (b, g, 0, 0))
    kspec = pl.BlockSpec((None, seq, NSA_DIM), lambda b, g, i: (g, b, 0))
    args = [nq, nqr, gates, kcmp, vcmp, ovt, ks, vs, kw, vw]
    in_specs = [qspec, qspec,
                pl.BlockSpec((None, GATE_LANES, tq), lambda b, g, i: (g, 0, b * nqb + i)),
                cspec, cspec, pl.BlockSpec(ovt.shape, lambda b, g, i: (0, 0)),
                kspec, kspec, kspec, kspec]
    body = functools.partial(_add_anchors(_nsa_kernel, args, in_specs, after), tq=tq, tk=tk, seq=seq)
    return pl.pallas_call(
        body,
        grid=(batch, NSA_KV_GROUPS, nqb),
        in_specs=in_specs,
        out_specs=qspec,
        out_shape=jax.ShapeDtypeStruct((n, NSA_WIDTH), BF16),
        scratch_shapes=[pltpu.VMEM((NSA_DIM + SUM_ROWS, seq), BF16), pltpu.VMEM((NSA_DIM + SUM_ROWS, seq), BF16),
                        pltpu.VMEM((NSA_DIM, rows_c), BF16), pltpu.VMEM((n_sel, tq), F32)],
        compiler_params=_params(3),
        name="nsa",
    )(*args)


def _memkv_kernel(mem_ref, w_ref, kv_ref):
    kv_ref[...] = _dot(mem_ref[...].astype(BF16), w_ref[...]).astype(BF16)


def _memkv(mem2d, w_xkv):
    n = mem2d.shape[0]
    w = w_xkv.astype(BF16)
    return pl.pallas_call(
        _memkv_kernel,
        grid=(n // MEM_LEN,),
        in_specs=[pl.BlockSpec((MEM_LEN, D_MODEL), lambda i: (i, 0)),
                  pl.BlockSpec(w.shape, lambda i: (0, 0))],
        out_specs=pl.BlockSpec((MEM_LEN, 2 * D_MODEL), lambda i: (i, 0)),
        out_shape=jax.ShapeDtypeStruct((n, 2 * D_MODEL), BF16),
        compiler_params=_params(1),
        name="memkv",
    )(mem2d, w)


def _pack_halves(v):
    half = D_MODEL // 2
    hi = pltpu.bitcast(v[:, :half].astype(BF16).astype(F32), jnp.uint32)
    lo = pltpu.bitcast(v[:, half:].astype(BF16).astype(F32), jnp.uint32)
    return hi | (lo >> 16)


def _unpack_halves(words):
    return pltpu.bitcast(words & jnp.uint32(0xFFFF0000), F32), pltpu.bitcast(words << 16, F32)


def _postmix_kernel(x_ref, oret_ref, onsa_ref, kv_ref, wout_ref, wq_ref, wo_ref,
                    g1_ref, b1_ref, g2_ref, b2_ref, x2_ref, x2p_ref):
    mixed = jnp.concatenate([oret_ref[...], onsa_ref[...]], axis=1)
    x1 = _layer_norm(DN_ALPHA * x_ref[...] + _dot(mixed, wout_ref[...]), g1_ref[...], b1_ref[...])
    q = (_dot(x1.astype(BF16), wq_ref[...]) * (XATT_DIM ** -0.5 * LOG2_E)).astype(BF16)
    heads = []
    for h in range(XATT_HEADS):
        cols = slice(h * XATT_DIM, (h + 1) * XATT_DIM)
        s = _dot_nt(q[:, cols], kv_ref[:, cols])
        m = jnp.max(s, axis=-1, keepdims=True)
        p = jnp.exp2(s - m)
        l = jnp.sum(p, axis=-1, keepdims=True)
        heads.append(_dot(p.astype(BF16), kv_ref[:, D_MODEL + h * XATT_DIM:D_MODEL + (h + 1) * XATT_DIM]) / l)
    att = jnp.concatenate(heads, axis=1).astype(BF16)
    x2 = _layer_norm(DN_ALPHA * x1 + _dot(att, wo_ref[...]), g2_ref[...], b2_ref[...])
    x2_ref[...] = x2
    x2p_ref[...] = _pack_halves(x2)


def _postmix(x2d, o_ret, o_nsa, kvx, w_out, w_xq, w_xo, ln1_g, ln1_b, ln2_g, ln2_b, batch0, batch, seq,
             after=None):
    n = batch * seq
    tm = 512 if seq % 512 == 0 else seq
    per_b = seq // tm
    row = lambda w: pl.BlockSpec((tm, w), lambda b, i: (b * per_b + i, 0))
    full = lambda a: pl.BlockSpec(a.shape, lambda b, i: (0,) * a.ndim)
    ws = [w_out.astype(BF16), w_xq.astype(BF16), w_xo.astype(BF16)]
    vecs = [v.reshape(1, D_MODEL) for v in (ln1_g, ln1_b, ln2_g, ln2_b)]
    args = [x2d, o_ret, o_nsa, kvx, *ws, *vecs]
    in_specs = ([pl.BlockSpec((tm, D_MODEL), lambda b, i: ((batch0 + b) * per_b + i, 0)),
                 row(RET_WIDTH), row(NSA_WIDTH),
                 pl.BlockSpec((MEM_LEN, 2 * D_MODEL), lambda b, i: (batch0 + b, 0))]
                + [full(w) for w in ws] + [full(v) for v in vecs])
    return pl.pallas_call(
        _add_anchors(_postmix_kernel, args, in_specs, after),
        grid=(batch, per_b),
        in_specs=in_specs,
        out_specs=[row(D_MODEL),
                   row(D_MODEL // 2)],
        out_shape=[jax.ShapeDtypeStruct((n, D_MODEL), F32),
                   jax.ShapeDtypeStruct((n, D_MODEL // 2), jnp.uint32)],
        compiler_params=_params(2),
        name="postmix",
    )(*args)


def _router_kernel(x_ref, wr_ref, bias_ref, e_ref, rank_ref, w_ref, cnt_ref, cntrow_ref, carry_ref, carryrow_ref):
    tn = x_ref.shape[0]
    per = N_EXPERTS // N_GROUPS

    @pl.when(pl.program_id(0) == 0)
    def _():
        carry_ref[...] = jnp.zeros_like(carry_ref)
        carryrow_ref[...] = jnp.zeros_like(carryrow_ref)

    logits = _dot_nt(wr_ref[...], x_ref[...].astype(BF16))
    scores = jax.nn.sigmoid(logits)
    biased = scores + bias_ref[...]
    b3 = biased.reshape(N_GROUPS, per, tn)
    member = lax.broadcasted_iota(jnp.int32, (N_GROUPS, per, tn), 1)
    top1 = jnp.max(b3, axis=1, keepdims=True)
    first1 = jnp.min(jnp.where(b3 == top1, member, per), axis=1, keepdims=True)
    top2 = jnp.max(jnp.where(member == first1, -jnp.inf, b3), axis=1, keepdims=True)
    gscore = top1 + top2
    gidx = lax.broadcasted_iota(jnp.int32, (N_GROUPS, 1, tn), 0)
    gwork = gscore
    for _ in range(TOPK_GROUPS - 1):
        gbest = jnp.max(gwork, axis=0, keepdims=True)
        gfirst = jnp.min(jnp.where(gwork == gbest, gidx, N_GROUPS), axis=0, keepdims=True)
        gwork = jnp.where(gidx == gfirst, -jnp.inf, gwork)
    kth = jnp.max(gwork, axis=0, keepdims=True)
    work = jnp.where(gscore >= kth, b3, NEG).reshape(N_EXPERTS, tn)
    eidx = lax.broadcasted_iota(jnp.int32, (N_EXPERTS, tn), 0)
    picks = []
    chosen = jnp.zeros((N_EXPERTS, tn), F32)
    for _ in range(TOP_K):
        best = jnp.max(work, axis=0, keepdims=True)
        first = jnp.min(jnp.where(work == best, eidx, N_EXPERTS), axis=0, keepdims=True)
        hit = eidx == first
        picks.append((first, hit))
        chosen = jnp.where(hit, 1.0, chosen)
        work = jnp.where(hit, -jnp.inf, work)

    r_i = lax.broadcasted_iota(jnp.int32, (tn, tn), 0)
    c_i = lax.broadcasted_iota(jnp.int32, (tn, tn), 1)
    before = jnp.where(r_i < c_i, 1.0, 0.0).astype(BF16)
    chosen_b = chosen.astype(BF16)
    rank = _dot(chosen_b, before) + carry_ref[...]
    carry_ref[...] = carry_ref[...] + jnp.sum(chosen, axis=1, keepdims=True)
    carryrow_ref[...] = carryrow_ref[...] + _dot_nt(jnp.ones((8, tn), BF16), chosen_b)
    cnt_ref[...] = carry_ref[...]
    cntrow_ref[...] = carryrow_ref[...]

    wsel = [jnp.sum(jnp.where(hit, scores, 0.0), axis=0, keepdims=True) for _, hit in picks]
    wsum = wsel[0]
    for v in wsel[1:]:
        wsum = wsum + v
    for kk, (first, hit) in enumerate(picks):
        e_ref[kk:kk + 1, :] = first
        rank_ref[kk:kk + 1, :] = jnp.sum(jnp.where(hit, rank, 0.0), axis=0, keepdims=True).astype(jnp.int32)
        w_ref[kk:kk + 1, :] = wsel[kk] / wsum * ROUTED_SCALE


def _router(x2, w_router, router_bias):
    n = x2.shape[0]
    tn = 512 if n % 512 == 0 else n
    wr_t = w_router.T.astype(BF16)
    bias = router_bias.reshape(N_EXPERTS, 1).astype(F32)
    kspec = pl.BlockSpec((TOP_K, tn), lambda i: (0, i))
    return pl.pallas_call(
        _router_kernel,
        grid=(n // tn,),
        in_specs=[pl.BlockSpec((tn, D_MODEL), lambda i: (i, 0)),
                  pl.BlockSpec(wr_t.shape, lambda i: (0, 0)),
                  pl.BlockSpec(bias.shape, lambda i: (0, 0))],
        out_specs=[kspec, kspec, kspec, pl.BlockSpec((N_EXPERTS, 1), lambda i: (0, 0)),
                   pl.BlockSpec((8, N_EXPERTS), lambda i: (0, 0))],
        out_shape=[jax.ShapeDtypeStruct((TOP_K, n), jnp.int32),
                   jax.ShapeDtypeStruct((TOP_K, n), jnp.int32),
                   jax.ShapeDtypeStruct((TOP_K, n), F32),
                   jax.ShapeDtypeStruct((N_EXPERTS, 1), F32),
                   jax.ShapeDtypeStruct((8, N_EXPERTS), F32)],
        scratch_shapes=[pltpu.VMEM((N_EXPERTS, 1), F32), pltpu.VMEM((8, N_EXPERTS), F32)],
        compiler_params=_params(1),
        name="router",
    )(x2, wr_t, bias)


def _slots_kernel(e_ref, rank_ref, cnt_ref, cntrow_ref, dest_ref, blk_e_ref, valid_ref, used_ref,
                  *, blk, n_blocks):
    pad = lambda c: jnp.ceil(c / blk) * blk
    cnt = cnt_ref[...]
    padded = pad(cnt)
    padded_row = pad(cntrow_ref[0:1, :])
    r_i = lax.broadcasted_iota(jnp.int32, (N_EXPERTS, N_EXPERTS), 0)
    c_i = lax.broadcasted_iota(jnp.int32, (N_EXPERTS, N_EXPERTS), 1)
    start = jnp.sum(jnp.where(c_i < r_i, padded_row, 0.0), axis=1, keepdims=True)
    end = start + padded
    e = e_ref[...]
    dest = rank_ref[...]
    for ex in range(N_EXPERTS):
        dest = dest + jnp.where(e == ex, start[ex:ex + 1, :].astype(jnp.int32), 0)
    dest_ref[...] = dest
    bstart = (lax.broadcasted_iota(jnp.int32, (1, n_blocks), 1) * blk).astype(F32)
    owner = jnp.sum(jnp.where(end <= bstart, 1.0, 0.0), axis=0, keepdims=True)
    blk_e_ref[...] = jnp.minimum(owner, N_EXPERTS - 1.0).astype(jnp.int32)
    inside = (start <= bstart) & (bstart < end)
    real = jnp.clip(start + cnt - bstart, 0.0, float(blk))
    valid = jnp.sum(jnp.where(inside, real, 0.0), axis=0, keepdims=True)
    valid_ref[...] = valid.astype(jnp.int32)
    used_ref[...] = jnp.sum(jnp.where(valid > 0.0, 1.0, 0.0), axis=1, keepdims=True).astype(jnp.int32)


def _slots(e_k, rank_k, counts, counts_row, blk, n_blocks):
    n = e_k.shape[1]
    full = lambda shape: pl.BlockSpec(shape, lambda: (0,) * len(shape))
    return pl.pallas_call(
        functools.partial(_slots_kernel, blk=blk, n_blocks=n_blocks),
        in_specs=[full((TOP_K, n)), full((TOP_K, n)), full((N_EXPERTS, 1)), full((8, N_EXPERTS))],
        out_specs=[full((TOP_K, n)), full((1, n_blocks)), full((1, n_blocks)), full((1, 1))],
        out_shape=[jax.ShapeDtypeStruct((TOP_K, n), jnp.int32),
                   jax.ShapeDtypeStruct((1, n_blocks), jnp.int32),
                   jax.ShapeDtypeStruct((1, n_blocks), jnp.int32),
                   jax.ShapeDtypeStruct((1, 1), jnp.int32)],
        compiler_params=pltpu.CompilerParams(vmem_limit_bytes=VMEM_LIMIT),
        name="slots",
    )(e_k, rank_k, counts, counts_row)


def _sc_worker_base(per_worker):
    return (lax.axis_index("s") * SC_CORES + lax.axis_index("c")) * per_worker


def _sc_scatter_rows(rows, idx, n_out):
    n, width = rows.shape
    k_lists = idx.shape[0] // n
    workers = SC_CORES * SC_SUBCORES
    per_worker = n // workers
    assert per_worker * workers == n and per_worker % (SC_CHUNK * SC_INFLIGHT) == 0
    mesh = plsc.VectorSubcoreMesh(core_axis_name="c", subcore_axis_name="s")
    lanes = range(SC_INFLIGHT)

    @functools.partial(
        pl.kernel, mesh=mesh,
        out_type=jax.ShapeDtypeStruct((n_out, width), rows.dtype),
        scratch_types=[pltpu.VMEM((SC_CHUNK, width), rows.dtype)] * SC_INFLIGHT
                      + [pltpu.VMEM((SC_CHUNK,), jnp.int32)] * (k_lists * SC_INFLIGHT)
                      + [pltpu.SemaphoreType.DMA] * (2 * SC_INFLIGHT),
        name="sc_scatter")
    def scatter(rows_hbm, idx_hbm, out_hbm, *scratch):
        rows_vs = scratch[:SC_INFLIGHT]
        idx_vs = [scratch[SC_INFLIGHT + j * k_lists:SC_INFLIGHT + (j + 1) * k_lists] for j in lanes]
        sems = scratch[SC_INFLIGHT * (1 + k_lists):]
        sem_in, sem_out = sems[:SC_INFLIGHT], sems[SC_INFLIGHT:]
        base = _sc_worker_base(per_worker)

        @pl.loop(0, per_worker // (SC_CHUNK * SC_INFLIGHT))
        def _(gi):
            offs = [pl.multiple_of(base + (gi * SC_INFLIGHT + j) * SC_CHUNK, SC_CHUNK) for j in lanes]
            loads = []
            for j in lanes:
                loads.append([pltpu.async_copy(rows_hbm.at[pl.ds(offs[j], SC_CHUNK)], rows_vs[j], sem_in[j])]
                             + [pltpu.async_copy(
                                 idx_hbm.at[pl.ds(pl.multiple_of(k * n + offs[j], SC_CHUNK), SC_CHUNK)],
                                 idx_vs[j][k], sem_in[j]) for k in range(k_lists)])
            copies = []
            for j in lanes:
                for c in loads[j]:
                    c.wait()
                copies.append([pltpu.async_copy(rows_vs[j], out_hbm.at[idx_vs[j][k]], sem_out[j])
                               for k in range(k_lists)])
            for j in lanes:
                for c in copies[j]:
                    c.wait()

    return scatter(rows, idx)


def _experts_kernel(blk_e_ref, valid_ref, used_ref, xs_ref, wg_ref, wu_ref, wd_ref, y_ref, wg_b, wu_b, wd_b):
    del used_ref
    i = pl.program_id(0)
    valid = valid_ref[i]

    @pl.when((i == 0) | (blk_e_ref[i] != blk_e_ref[jnp.maximum(i - 1, 0)]))
    def _():
        wg_b[...] = wg_ref[...].astype(BF16)
        wu_b[...] = wu_ref[...].astype(BF16)
        wd_b[...] = wd_ref[...].astype(BF16)

    @pl.when(valid > 0)
    def _():
        half = D_MODEL // 2
        row = lax.broadcasted_iota(jnp.int32, (xs_ref.shape[0], 1), 0)
        hi, lo = (v.astype(BF16) for v in _unpack_halves(jnp.where(row < valid, xs_ref[...], jnp.uint32(0))))
        gate = _dot(hi, wg_b[:half, :]) + _dot(lo, wg_b[half:, :])
        up = _dot(hi, wu_b[:half, :]) + _dot(lo, wu_b[half:, :])
        y_ref[...] = _pack_halves(_dot((jax.nn.silu(gate) * up).astype(BF16), wd_b[...]))

    @pl.when(valid <= 0)
    def _():
        y_ref[...] = jnp.zeros_like(y_ref)


def _experts(blk_e, valid, n_used, xs, w_gate, w_up, w_down, blk, after=None):
    cap, width = xs.shape
    wspec = lambda a: pl.BlockSpec((None,) + a.shape[1:], lambda i, be, nv, nu: (be[i], 0, 0))
    rows = pl.BlockSpec((blk, width), lambda i, be, nv, nu: (jnp.minimum(i, nu[0]), 0))
    args = [blk_e, valid, n_used, xs, w_gate, w_up, w_down]
    in_specs = [rows, wspec(w_gate), wspec(w_up), wspec(w_down)]
    body = _add_anchors(_experts_kernel, args, in_specs, after)
    return pl.pallas_call(
        body,
        grid_spec=pltpu.PrefetchScalarGridSpec(
            num_scalar_prefetch=3,
            grid=(cap // blk,),
            in_specs=in_specs,
            out_specs=rows,
            scratch_shapes=[pltpu.VMEM(w.shape[1:], BF16) for w in (w_gate, w_up, w_down)],
        ),
        out_shape=jax.ShapeDtypeStruct(xs.shape, xs.dtype),
        compiler_params=_params(1),
        name="experts",
    )(*args)


def _sc_gather_rows(table, idx):
    b, width = idx.shape[0], table.shape[1]
    workers = SC_CORES * SC_SUBCORES
    per_worker = b // workers
    assert per_worker * workers == b and per_worker % (SC_CHUNK * SC_INFLIGHT) == 0
    mesh = plsc.VectorSubcoreMesh(core_axis_name="c", subcore_axis_name="s")

    @functools.partial(
        pl.kernel, mesh=mesh,
        out_type=jax.ShapeDtypeStruct((b, width), table.dtype),
        scratch_types=[pltpu.VMEM((SC_CHUNK,), jnp.int32)] * SC_INFLIGHT
                      + [pltpu.VMEM((SC_CHUNK, width), table.dtype)] * SC_INFLIGHT
                      + [pltpu.SemaphoreType.DMA] * (1 + 2 * SC_INFLIGHT),
        name="sc_gather")
    def gather(table_hbm, idx_hbm, out_hbm, *scratch):
        idx_vs = scratch[:SC_INFLIGHT]
        rows_vs = scratch[SC_INFLIGHT:2 * SC_INFLIGHT]
        sem_idx = scratch[2 * SC_INFLIGHT]
        sem_rows = scratch[2 * SC_INFLIGHT + 1:3 * SC_INFLIGHT + 1]
        sem_out = scratch[3 * SC_INFLIGHT + 1:]
        base = _sc_worker_base(per_worker)
        lanes = range(SC_INFLIGHT)

        @pl.loop(0, per_worker // (SC_CHUNK * SC_INFLIGHT))
        def _(gi):
            offs = [pl.multiple_of(base + (gi * SC_INFLIGHT + j) * SC_CHUNK, SC_CHUNK) for j in lanes]
            loads = [pltpu.async_copy(idx_hbm.at[pl.ds(offs[j], SC_CHUNK)], idx_vs[j], sem_idx) for j in lanes]
            for c in loads:
                c.wait()
            gathers = [pltpu.async_copy(table_hbm.at[idx_vs[j]], rows_vs[j], sem_rows[j]) for j in lanes]
            writes = []
            for j in lanes:
                gathers[j].wait()
                writes.append(pltpu.async_copy(rows_vs[j], out_hbm.at[pl.ds(offs[j], SC_CHUNK)], sem_out[j]))
            for c in writes:
                c.wait()

    return gather(table, idx)


def _combine_kernel(x_ref, wk_ref, yk_ref, wsg_ref, wsu_ref, wsd_ref, g_ref, b_ref, *rest):
    o_ref = rest[-1]
    x = x_ref[...]
    xb = x.astype(BF16)
    shared = _dot((jax.nn.silu(_dot(xb, wsg_ref[...])) * _dot(xb, wsu_ref[...])).astype(BF16), wsd_ref[...])
    wk = wk_ref[...]
    routed_hi = routed_lo = None
    for kk in range(TOP_K):
        hi, lo = _unpack_halves(yk_ref[kk])
        w = wk[:, kk:kk + 1]
        routed_hi = hi * w if kk == 0 else routed_hi + hi * w
        routed_lo = lo * w if kk == 0 else routed_lo + lo * w
    routed = jnp.concatenate([routed_hi, routed_lo], axis=1)
    o_ref[...] = _layer_norm(DN_ALPHA * x + (routed + shared), g_ref[...], b_ref[...])


def _combine(x2, w_tok, yk, ws_gate, ws_up, ws_down, ln3_g, ln3_b, row0, n_total, out_prev, after=None,
             part=(0, 1)):
    n = x2.shape[0] // part[1]
    sub0 = part[0] * n
    tt = 512 if n % 512 == 0 and row0 % 512 == 0 else n
    blk0 = (row0 + sub0) // tt
    sub_blk = sub0 // tt
    ws = [ws_gate.astype(BF16), ws_up.astype(BF16), ws_down.astype(BF16)]
    vecs = [ln3_g.reshape(1, D_MODEL), ln3_b.reshape(1, D_MODEL)]
    full = lambda a: pl.BlockSpec(a.shape, lambda i: (0,) * a.ndim)
    args = [x2, w_tok, yk, *ws, *vecs]
    in_specs = ([pl.BlockSpec((tt, D_MODEL), lambda i: (sub_blk + i, 0)),
                 pl.BlockSpec((tt, TOP_K), lambda i: (sub_blk + i, 0)),
                 pl.BlockSpec((TOP_K, tt, D_MODEL // 2), lambda i: (0, sub_blk + i, 0))]
                + [full(a) for a in ws] + [full(v) for v in vecs])
    aliases = {}
    for anchor in (a for a in (after or ()) if a is not None):
        args.append(anchor)
        in_specs.append(pl.BlockSpec(memory_space=pl.ANY))
    if out_prev is not None:
        aliases = {len(args): 0}
        args.append(out_prev)
        in_specs.append(pl.BlockSpec(memory_space=pl.ANY))
    return pl.pallas_call(
        _combine_kernel,
        grid=(n // tt,),
        in_specs=in_specs,
        out_specs=pl.BlockSpec((tt, D_MODEL), lambda i: (blk0 + i, 0)),
        out_shape=jax.ShapeDtypeStruct((n_total, D_MODEL), F32),
        input_output_aliases=aliases,
        compiler_params=_params(1),
        name="combine",
    )(*args)


EXPERT_BLOCK = 1024


def _moe_dispatch(x2, x2p, w_router, router_bias):
    n = x2.shape[0]
    cap = n * TOP_K + N_EXPERTS * EXPERT_BLOCK
    e_k, rank_k, w_k, counts, counts_row = _router(x2, w_router, router_bias)
    dest, blk_e, valid, n_used = _slots(e_k, rank_k, counts, counts_row, EXPERT_BLOCK, cap // EXPERT_BLOCK)
    dest = dest.reshape(-1)
    return dict(w_tok=w_k.T, dest=dest, blk_e=blk_e.reshape(-1), valid=valid.reshape(-1),
                n_used=n_used.reshape(-1), xs=_sc_scatter_rows(x2p, dest, cap))


def _moe_experts(routed, w_gate, w_up, w_down, after):
    y = _experts(routed["blk_e"], routed["valid"], routed["n_used"], routed["xs"], w_gate, w_up, w_down,
                 EXPERT_BLOCK, after=after)
    n = routed["dest"].shape[0] // TOP_K
    return y, _sc_gather_rows(y, routed["dest"]).reshape(TOP_K, n, D_MODEL // 2)


def _layer(x, mem, positions, w_in, cmp_pe_k, cmp_pe_v, cmp_w1_k, cmp_w2_k, cmp_w1_v, cmp_w2_v,
           w_out, ln1_g, ln1_b, w_xq, w_xkv, w_xo, ln2_g, ln2_b, w_router, router_bias,
           w_gate, w_up, w_down, ws_gate, ws_up, ws_down, ln3_g, ln3_b):
    batch, seq, _ = x.shape
    n_total = batch * seq
    x2d = x.reshape(n_total, D_MODEL)
    pos_col = positions.astype(F32).reshape(n_total, 1)
    kvx = _memkv(mem.reshape(batch * MEM_LEN, D_MODEL), w_xkv)
    last = max(1, batch // 4)
    sizes = [batch - last, last] if batch > 1 else [batch]
    starts = [sum(sizes[:g]) for g in range(len(sizes))]

    def mixers_in(g):
        nb, row0 = sizes[g], starts[g] * seq
        (rq, rk, rv, rg, nq, nqr, kc, vc, ks, vs, kw, vw, gates) = _inproj(x2d, pos_col, w_in, row0, nb * seq)
        o_ret = _retention(rq, rk, rv, rg, nb, seq)
        kcmp = _compress(kc, cmp_pe_k, cmp_w1_k, cmp_w2_k, nb, seq)
        vcmp = _compress(vc, cmp_pe_v, cmp_w1_v, cmp_w2_v, nb, seq)
        return o_ret, (nq, nqr, gates, kcmp, vcmp, ks, vs, kw, vw)

    def attend(g, nsa_args, after):
        return _nsa(*nsa_args, sizes[g], seq, after=after)

    def mix_and_route(g, o_ret, o_nsa, after):
        x2, x2p = _postmix(x2d, o_ret, o_nsa, kvx, w_out, w_xq, w_xo, ln1_g, ln1_b, ln2_g, ln2_b,
                           starts[g], sizes[g], seq, after=after)
        return x2, _moe_dispatch(x2, x2p, w_router, router_bias)

    def combine(g, x2, routed, yk, out_prev, after=None, part=(0, 1)):
        return _combine(x2, routed["w_tok"], yk, ws_gate, ws_up, ws_down, ln3_g, ln3_b, starts[g] * seq,
                        n_total, out_prev, after=after, part=part)

    o_ret, nsa_args = mixers_in(0)
    x2, routed = mix_and_route(0, o_ret, attend(0, nsa_args, None), None)
    if len(sizes) == 1:
        y, yk = _moe_experts(routed, w_gate, w_up, w_down, after=None)
        return combine(0, x2, routed, yk, None).reshape(batch, seq, D_MODEL)
    o_ret1, nsa_args1 = mixers_in(1)
    y, yk = _moe_experts(routed, w_gate, w_up, w_down, after=[o_ret1])
    o_nsa1 = attend(1, nsa_args1, [y])
    x2_1, routed1 = mix_and_route(1, o_ret1, o_nsa1, [yk])
    out = combine(0, x2, routed, yk, None, [routed1["dest"]], part=(0, 2))
    y1, yk1 = _moe_experts(routed1, w_gate, w_up, w_down, after=[out])
    out = combine(0, x2, routed, yk, out, [y1], part=(1, 2))
    out = combine(1, x2_1, routed1, yk1, out)
    return out.reshape(batch, seq, D_MODEL)


def kernel(x, mem, positions, w_in, cmp_pe_k, cmp_pe_v, cmp_w1_k, cmp_w2_k, cmp_w1_v, cmp_w2_v, w_out, ln1_g, ln1_b, w_xq, w_xkv, w_xo, ln2_g, ln2_b, w_router, router_bias, w_gate, w_up, w_down, ws_gate, ws_up, ws_down, ln3_g, ln3_b):
    for l in range(DEPTH):
        x = _layer(x, mem, positions, w_in[l], cmp_pe_k[l], cmp_pe_v[l], cmp_w1_k[l], cmp_w2_k[l],
                   cmp_w1_v[l], cmp_w2_v[l], w_out[l], ln1_g[l], ln1_b[l], w_xq[l], w_xkv[l],
                   w_xo[l], ln2_g[l], ln2_b[l], w_router[l], router_bias[l], w_gate[l], w_up[l],
                   w_down[l], ws_gate[l], ws_up[l], ws_down[l], ln3_g[l], ln3_b[l])
    return x
```

```python
import functools

import numpy as np
import jax
import jax.numpy as jnp
from jax import lax
from jax.experimental import pallas as pl
from jax.experimental.pallas import tpu as pltpu
from jax.experimental.pallas import tpu_sc as plsc

D_MODEL = 1024
MEM_LEN = 256
DEPTH = 1
DN_ALPHA = (2 * DEPTH) ** 0.25
LN_EPS = 1e-5
NEG = -1e30

RET_HEADS = 4
RET_DIM = 128
RET_CHUNK = 128
RET_ROPE_BASE = 10000.0
RET_STEP_CHUNKS = 16
RET_WIDTH = RET_HEADS * RET_DIM

NSA_HEADS = 8
NSA_KV_GROUPS = 2
NSA_HPG = NSA_HEADS // NSA_KV_GROUPS
NSA_DIM = 64
NSA_WIDTH = NSA_HEADS * NSA_DIM
KV_WIDTH = NSA_KV_GROUPS * NSA_DIM
CMP_LEN = 32
CMP_STRIDE = 16
CMP_HIDDEN = 256
SEL_LEN = 64
SEL_SHIFT = 6
SEL_TOPK = 16
N_FORCED = 3
WIN = 512
ROPE_THETA = 500000.0
ROPE_DIMS = NSA_DIM // 4
GATE_LANES = 16
NSA_CHAINS = 1
LOG2_E = 1.4426950408889634
SUM_ROWS = 16
WIN_PART = 256

SC_CORES = 2
SC_SUBCORES = 16
SC_CHUNK = 64
SC_INFLIGHT = 2

XATT_HEADS = 4
XATT_DIM = D_MODEL // XATT_HEADS

N_EXPERTS = 64
TOP_K = 8
N_GROUPS = 8
TOPK_GROUPS = 4
EXPERT_FF = 256
SHARED_FF = 256
ROUTED_SCALE = 2.5

LANES = 128
VMEM_LIMIT = 56 * 1024 * 1024

F32 = jnp.float32
BF16 = jnp.bfloat16
NT_DIMS = (((1,), (1,)), ((), ()))


def _params(n_axes):
    return pltpu.CompilerParams(dimension_semantics=("arbitrary",) * n_axes,
                                vmem_limit_bytes=VMEM_LIMIT)


def _dot(a, b):
    return jnp.dot(a, b, preferred_element_type=F32)


def _dot_nt(a, b):
    return lax.dot_general(a, b, NT_DIMS, preferred_element_type=F32)


def _layer_norm(v, g, b):
    mu = jnp.mean(v, axis=-1, keepdims=True)
    d = v - mu
    var = jnp.mean(d * d, axis=-1, keepdims=True)
    return d * lax.rsqrt(var + LN_EPS) * g + b


def _inproj_kernel(x_ref, pos_ref, wret_ref, wnq_ref, wkv_ref, wg_ref, invr_ref, invn_ref,
                   rq_ref, rk_ref, rv_ref, rg_ref, nq_ref, nqr_ref, kc_ref, vc_ref,
                   ks_ref, vs_ref, kw_ref, vw_ref, gate_ref):
    xb = x_ref[...].astype(BF16)
    pos = pos_ref[...]
    lane = lax.broadcasted_iota(jnp.int32, (1, LANES), 1)

    ang = pos * invr_ref[...]
    cos_r = jnp.cos(ang)
    sin_r = jnp.sin(ang)
    sin_r = jnp.where(lane < RET_DIM // 2, -sin_r, sin_r)
    q_all = _dot(xb, wret_ref[:, :RET_WIDTH])
    k_all = _dot(xb, wret_ref[:, RET_WIDTH:2 * RET_WIDTH])
    for h in range(RET_HEADS):
        cols = slice(h * RET_DIM, (h + 1) * RET_DIM)
        q = q_all[:, cols]
        rq_ref[:, cols] = (q * cos_r + pltpu.roll(q, RET_DIM // 2, 1) * sin_r).astype(BF16)
        k = k_all[:, cols]
        k = (k * cos_r + pltpu.roll(k, RET_DIM // 2, 1) * sin_r) * (RET_DIM ** -0.5)
        rk_ref[:, cols] = k.astype(BF16)
    rv_ref[...] = _dot(xb, wret_ref[:, 2 * RET_WIDTH:3 * RET_WIDTH]).astype(BF16)
    rg_ref[...] = _dot(xb, wret_ref[:, 3 * RET_WIDTH:4 * RET_WIDTH]).astype(BF16)

    half = ROPE_DIMS // 2
    j = lane % NSA_DIM
    angn = pos * invn_ref[...]
    cos_n = jnp.cos(angn)
    sin_n = jnp.sin(angn)
    sin_lo = jnp.where(j < half, -sin_n, 0.0)
    sin_hi = jnp.where((j >= half) & (j < 2 * half), sin_n, 0.0)

    def rope_n(v):
        return v * cos_n + pltpu.roll(v, half, 1) * sin_hi + pltpu.roll(v, LANES - half, 1) * sin_lo

    scale = NSA_DIM ** -0.5 * LOG2_E
    nq_all = _dot(xb, wnq_ref[...])
    for c in range(NSA_WIDTH // LANES):
        cols = slice(c * LANES, (c + 1) * LANES)
        q = nq_all[:, cols]
        nq_ref[:, cols] = (q * scale).astype(BF16)
        nqr_ref[:, cols] = (rope_n(q) * scale).astype(BF16)

    kv_all = _dot(xb, wkv_ref[...])

    def kv(i):
        return kv_all[:, i * KV_WIDTH:(i + 1) * KV_WIDTH]

    def split_groups(ref, v):
        for g in range(NSA_KV_GROUPS):
            ref[g] = v[:, g * NSA_DIM:(g + 1) * NSA_DIM].astype(BF16)

    kc_ref[...] = kv(0)
    vc_ref[...] = kv(1)
    split_groups(ks_ref, rope_n(kv(2)))
    split_groups(vs_ref, kv(3))
    split_groups(kw_ref, rope_n(kv(4)))
    split_groups(vw_ref, kv(5))

    gt = jax.nn.sigmoid(_dot_nt(wg_ref[...], xb))
    for g in range(NSA_KV_GROUPS):
        gate_ref[g] = gt[g * GATE_LANES:(g + 1) * GATE_LANES, :]


def _inproj(x2d, pos_col, w_in, row0, n):
    tm = 1024 if n % 1024 == 0 and row0 % 1024 == 0 else n
    blk0 = row0 // tm
    off = np.cumsum([0] + [RET_WIDTH] * 4 + [NSA_WIDTH] + [KV_WIDTH] * 6)
    w_ret = w_in[:, :off[4]].astype(BF16)
    w_nq = w_in[:, off[4]:off[5]].astype(BF16)
    w_kv = w_in[:, off[5]:off[11]].astype(BF16)
    wg = w_in[:, off[11]:].reshape(D_MODEL, NSA_KV_GROUPS, NSA_HPG * 3)
    wg = jnp.pad(wg, ((0, 0), (0, 0), (0, GATE_LANES - NSA_HPG * 3)))
    wg = wg.reshape(D_MODEL, NSA_KV_GROUPS * GATE_LANES).T.astype(BF16)

    lane = np.arange(LANES)
    half_r = RET_DIM // 2
    inv_r = (np.float32(RET_ROPE_BASE) ** (-np.arange(half_r, dtype=np.float32) / np.float32(half_r)))
    inv_r = inv_r.astype(np.float32)[lane % half_r][None, :]
    half_n = ROPE_DIMS // 2
    inv_n = (np.float32(ROPE_THETA) ** (-np.arange(half_n, dtype=np.float32) / np.float32(half_n)))
    jn = lane % NSA_DIM
    inv_n = np.where(jn < ROPE_DIMS, inv_n.astype(np.float32)[jn % half_n], np.float32(0.0))[None, :]

    row = lambda w: pl.BlockSpec((tm, w), lambda i: (i, 0))
    src_row = lambda w: pl.BlockSpec((tm, w), lambda i: (blk0 + i, 0))
    full = lambda a: pl.BlockSpec(a.shape, lambda i: (0,) * a.ndim)
    grp = lambda w: pl.BlockSpec((NSA_KV_GROUPS, tm, w), lambda i: (0, i, 0))
    bf = lambda w: jax.ShapeDtypeStruct((n, w), BF16)
    gbf = jax.ShapeDtypeStruct((NSA_KV_GROUPS, n, NSA_DIM), BF16)
    inv_r = jnp.asarray(inv_r, F32)
    inv_n = jnp.asarray(inv_n, F32)
    return pl.pallas_call(
        _inproj_kernel,
        grid=(n // tm,),
        in_specs=[src_row(D_MODEL), src_row(1), full(w_ret), full(w_nq), full(w_kv), full(wg),
                  full(inv_r), full(inv_n)],
        out_specs=[row(RET_WIDTH)] * 4 + [row(NSA_WIDTH)] * 2 + [row(KV_WIDTH)] * 2
                  + [grp(NSA_DIM)] * 4
                  + [pl.BlockSpec((NSA_KV_GROUPS, GATE_LANES, tm), lambda i: (0, 0, i))],
        out_shape=[bf(RET_WIDTH)] * 4 + [bf(NSA_WIDTH)] * 2
                  + [jax.ShapeDtypeStruct((n, KV_WIDTH), F32)] * 2 + [gbf] * 4
                  + [jax.ShapeDtypeStruct((NSA_KV_GROUPS, GATE_LANES, n), F32)],
        compiler_params=_params(1),
        name="inproj",
    )(x2d, pos_col, w_ret, w_nq, w_kv, wg, inv_r, inv_n)


def _retention_kernel(q_ref, k_ref, v_ref, g_ref, o_ref, state_ref):
    c = RET_CHUNK

    @pl.when(pl.program_id(1) == 0)
    def _():
        state_ref[...] = jnp.zeros_like(state_ref)

    row = lax.broadcasted_iota(jnp.int32, (c, c), 0)
    col = lax.broadcasted_iota(jnp.int32, (c, c), 1)
    rel = (row - col).astype(F32)
    idx = lax.broadcasted_iota(jnp.int32, (c, 1), 0).astype(F32)
    for h in range(RET_HEADS):
        log_g = float(np.log(np.float32(1.0) - np.float32(2.0) ** np.float32(-5.0 - h)))
        cols = slice(h * RET_DIM, (h + 1) * RET_DIM)
        dmask = jnp.where(rel >= 0, jnp.exp(log_g * jnp.maximum(rel, 0.0)), 0.0)
        zeta = jnp.exp(log_g * (c - 1.0 - idx))
        xi = jnp.exp(log_g * (idx + 1.0))
        for j in range(q_ref.shape[0] // c):
            rows = slice(j * c, (j + 1) * c)
            q = q_ref[rows, cols]
            k = k_ref[rows, cols]
            v = v_ref[rows, cols]
            scores = _dot_nt(q, k) * dmask
            inner = _dot(scores.astype(BF16), v)
            prev = state_ref[h]
            cross = _dot(q, prev.astype(BF16)) * xi
            kz = (k.astype(F32) * zeta).astype(BF16)
            kv = lax.dot_general(kz, v, (((0,), (0,)), ((), ())), preferred_element_type=F32)
            state_ref[h] = prev * float(np.exp(np.float32(log_g) * np.float32(c))) + kv
            o = inner + cross
            mu = jnp.mean(o, axis=-1, keepdims=True)
            d = o - mu
            var = jnp.mean(d * d, axis=-1, keepdims=True)
            o = d * lax.rsqrt(var + LN_EPS)
            o_ref[rows, cols] = (jax.nn.silu(g_ref[rows, cols].astype(F32)) * o).astype(BF16)


def _retention(rq, rk, rv, rg, batch, seq):
    per_step = RET_STEP_CHUNKS if (seq // RET_CHUNK) % RET_STEP_CHUNKS == 0 else 1
    nc = seq // (RET_CHUNK * per_step)
    spec = pl.BlockSpec((RET_CHUNK * per_step, RET_WIDTH), lambda b, n: (b * nc + n, 0))
    return pl.pallas_call(
        _retention_kernel,
        grid=(batch, nc),
        in_specs=[spec] * 4,
        out_specs=spec,
        out_shape=jax.ShapeDtypeStruct(rq.shape, BF16),
        scratch_shapes=[pltpu.VMEM((RET_HEADS, RET_DIM, RET_DIM), F32)],
        compiler_params=_params(2),
        name="retention",
    )(rq, rk, rv, rg)


def _compress_kernel(a_ref, pe_ref, w1_ref, w2_ref, o_ref, shift_ref, *, n_cmp):
    rows = a_ref.shape[0]
    a = a_ref[...]
    lo = (a + pe_ref[0]).astype(BF16)
    hi = (a + pe_ref[1]).astype(BF16)
    ridx = lax.broadcasted_iota(jnp.int32, (rows, 1), 0)
    shift_ref[rows:rows + 8, :] = jnp.zeros((8, CMP_HIDDEN), F32)
    for g in range(NSA_KV_GROUPS):
        p = _dot(lo, w1_ref[0, g])
        shift_ref[0:rows, :] = _dot(hi, w1_ref[1, g])
        hid = jax.nn.silu(p + shift_ref[pl.ds(1, rows), :])
        out = _dot(hid.astype(BF16), w2_ref[...])
        o_ref[g] = jnp.where(ridx < n_cmp, out, 0.0).astype(BF16)


def _compress(a, pe, w1, w2, batch, seq):
    rows = seq // CMP_STRIDE
    per = CMP_STRIDE * KV_WIDTH
    n_cmp = (seq - CMP_LEN) // CMP_STRIDE + 1
    a2 = a.reshape(batch * rows, per)
    pe2 = jnp.tile(pe.reshape(2, CMP_STRIDE, 1, NSA_DIM), (1, 1, NSA_KV_GROUPS, 1)).reshape(2, 1, per)
    w1r = w1.reshape(2, CMP_STRIDE, 1, NSA_DIM, CMP_HIDDEN)
    eye = jnp.eye(NSA_KV_GROUPS, dtype=w1.dtype).reshape(1, NSA_KV_GROUPS, 1, NSA_KV_GROUPS, 1, 1)
    w1x = (w1r[:, None] * eye).reshape(2, NSA_KV_GROUPS, per, CMP_HIDDEN).astype(BF16)
    w2b = w2.astype(BF16)
    full = lambda arr: pl.BlockSpec(arr.shape, lambda b: (0,) * arr.ndim)
    return pl.pallas_call(
        functools.partial(_compress_kernel, n_cmp=n_cmp),
        grid=(batch,),
        in_specs=[pl.BlockSpec((rows, per), lambda b: (b, 0)), full(pe2), full(w1x), full(w2b)],
        out_specs=pl.BlockSpec((None, NSA_KV_GROUPS, rows, NSA_DIM), lambda b: (b, 0, 0, 0)),
        out_shape=jax.ShapeDtypeStruct((batch, NSA_KV_GROUPS, rows, NSA_DIM), BF16),
        scratch_shapes=[pltpu.VMEM((rows + 8, CMP_HIDDEN), F32)],
        compiler_params=_params(1),
        name="compress",
    )(a2, pe2, w1x, w2b)


def _heads_to_lanes(ref):
    vt = ref[...].astype(F32).T
    return jnp.concatenate([vt[h * NSA_DIM:(h + 1) * NSA_DIM] for h in range(NSA_HPG)], axis=1).astype(BF16)


def _transpose_into(dst_ref, src_ref, chunk):
    def step(c, _):
        c0 = pl.multiple_of(c * chunk, chunk)
        dst_ref[:NSA_DIM, pl.ds(c0, chunk)] = src_ref[pl.ds(c0, chunk), :].astype(F32).T.astype(BF16)
        return 0
    lax.fori_loop(0, src_ref.shape[0] // chunk, step, 0)


def _nsa_kernel(qraw_ref, qrot_ref, gate_ref, kcmp_ref, vcmp_ref, ovt_ref,
                ks_ref, vs_ref, kw_ref, vw_ref, o_ref, vst_ref, vwt_ref, vct_ref, bias_ref, *, tq, tk, seq):
    i = pl.program_id(2)
    t0 = i * tq
    cols = NSA_HPG * tq
    n_sel = seq // SEL_LEN
    n_cmp_rows = seq // CMP_STRIDE
    blocks_per_tile = tk // SEL_LEN

    @pl.when(i == 0)
    def _():
        chunk = min(512, n_cmp_rows)
        _transpose_into(vst_ref, vs_ref, chunk)
        _transpose_into(vwt_ref, vw_ref, chunk)
        _transpose_into(vct_ref, vcmp_ref, chunk)
        vst_ref[NSA_DIM:, :] = jnp.ones((SUM_ROWS, seq), BF16)
        vwt_ref[NSA_DIM:, :] = jnp.ones((SUM_ROWS, seq), BF16)

    def split_sum(acc):
        return acc[:NSA_DIM] / acc[NSA_DIM:NSA_DIM + 1]

    q_raw = _heads_to_lanes(qraw_ref)
    q_rot = _heads_to_lanes(qrot_ref)
    t_row = t0 + lax.broadcasted_iota(jnp.int32, (1, tq), 1)

    chain_w = cols // NSA_CHAINS
    heads_per_chain = chain_w // tq
    chains = [slice(c * chain_w, (c + 1) * chain_w) for c in range(NSA_CHAINS)]
    tile_chain = lambda v: jnp.concatenate([v] * heads_per_chain, axis=1)

    pw = min(tq, WIN_PART)
    parts = []
    for u in range(tq // pw):
        span = WIN + pw
        ws = pl.multiple_of(jnp.maximum(t0 + u * pw - WIN, 0), pw)
        dist = t_row[:, u * pw:(u + 1) * pw] - (ws + lax.broadcasted_iota(jnp.int32, (span, 1), 0))
        bias_w = jnp.concatenate([jnp.where((dist >= 0) & (dist < WIN), 0.0, NEG)] * NSA_HPG, axis=1)
        q_part = jnp.concatenate([q_rot[:, h * tq + u * pw:h * tq + (u + 1) * pw] for h in range(NSA_HPG)],
                                 axis=1)
        s_w = _dot(kw_ref[pl.ds(ws, span), :], q_part) + bias_w
        p_w = jnp.exp2((s_w - jnp.max(s_w, axis=0, keepdims=True)).astype(BF16))
        parts.append(split_sum(_dot(vwt_ref[:, pl.ds(ws, span)], p_w)))
    o_w = jnp.concatenate([parts[u][:, h * pw:(h + 1) * pw]
                           for h in range(NSA_HPG) for u in range(tq // pw)], axis=1)

    c_idx = lax.broadcasted_iota(jnp.int32, (n_cmp_rows, 1), 0)
    bias_c = tile_chain(jnp.where(c_idx * CMP_STRIDE + (CMP_LEN - 1) <= t_row, 0.0, NEG))
    sees_any = tile_chain(jnp.where(t_row >= CMP_LEN - 1, 1.0, 0.0))
    o_c = []
    p_sum = None
    for c in chains:
        s_c = _dot(kcmp_ref[...], q_raw[:, c]) + bias_c
        e_c = jnp.exp2(s_c - jnp.max(s_c, axis=0, keepdims=True))
        p_c = e_c * (sees_any / jnp.sum(e_c, axis=0, keepdims=True))
        o_c.append(_dot(vct_ref[...], p_c.astype(BF16)))
        for h in range(heads_per_chain):
            p_h = p_c[:, h * tq:(h + 1) * tq]
            p_sum = p_h if p_sum is None else p_sum + p_h
    o_c = jnp.concatenate(o_c, axis=1)

    p_hi = p_sum.astype(BF16)
    p_lo = (p_sum - p_hi.astype(F32)).astype(BF16)
    ovt = ovt_ref[...]
    imp = _dot(ovt, p_hi) + _dot(ovt, p_lo)
    jb = lax.broadcasted_iota(jnp.int32, (n_sel, tq), 0)
    cur = (t0 + lax.broadcasted_iota(jnp.int32, (n_sel, tq), 1)) >> SEL_SHIFT
    forced = (jb == 0) | (jb == cur) | (jb == cur - 1)
    work = jnp.where(forced, -jnp.inf, imp)
    work = jnp.where(jb <= cur, work, NEG)
    sel_t = jnp.where(forced, 1.0, 0.0)
    for _ in range(max(min(SEL_TOPK, n_sel) - N_FORCED, 0)):
        best = jnp.max(work, axis=0, keepdims=True)
        first = jnp.min(jnp.where(work == best, jb, n_sel), axis=0, keepdims=True)
        hit = jb == first
        sel_t = jnp.where(hit, 1.0, sel_t)
        work = jnp.where(hit, -jnp.inf, work)
    bias_ref[...] = jnp.where(sel_t > 0.5, 0.0, NEG)

    def sel_tile(kt, carry, causal):
        k0 = pl.multiple_of(kt * tk, tk)
        bias = jnp.concatenate(
            [jnp.broadcast_to(bias_ref[pl.ds(kt * blocks_per_tile + j, 1), :], (SEL_LEN, tq))
             for j in range(blocks_per_tile)], axis=0)
        if causal:
            kpos = k0 + lax.broadcasted_iota(jnp.int32, (tk, 1), 0)
            bias = jnp.where(kpos <= t_row, bias, NEG)
        bias = tile_chain(bias)
        k_t = ks_ref[pl.ds(k0, tk), :]
        v_t = vst_ref[:, pl.ds(k0, tk)]
        out = []
        scores = [_dot(k_t, q_rot[:, c]) + bias for c in chains]
        for (m, acc), s in zip(carry, scores):
            m_new = jnp.maximum(m, jnp.max(s, axis=0, keepdims=True))
            p = jnp.exp2((s - m_new).astype(BF16))
            acc = jnp.exp2(m - m_new) * acc + _dot(v_t, p)
            out.append((m_new, acc))
        return tuple(out)

    n_full = t0 // tk
    init = tuple((jnp.full((1, chain_w), NEG, F32), jnp.zeros((NSA_DIM + SUM_ROWS, chain_w), F32))
                 for _ in chains)
    carry = lax.fori_loop(0, n_full, functools.partial(sel_tile, causal=False), init)
    for d in range(max(tq // tk, 1)):
        carry = sel_tile(n_full + d, carry, causal=True)
    o_s = jnp.concatenate([split_sum(acc) for _, acc in carry], axis=1)

    gt = gate_ref[...]
    outs = []
    for h in range(NSA_HPG):
        c = slice(h * tq, (h + 1) * tq)
        outs.append(gt[3 * h:3 * h + 1] * o_c[:, c] + gt[3 * h + 1:3 * h + 2] * o_s[:, c]
                    + gt[3 * h + 2:3 * h + 3] * o_w[:, c])
    o_ref[...] = jnp.concatenate(outs, axis=0).T.astype(BF16)


def _anchored(kernel_fn, n_inputs, n_anchors):
    def body(*refs, **static):
        kernel_fn(*refs[:n_inputs], *refs[n_inputs + n_anchors:], **static)
    return body


def _add_anchors(kernel_fn, args, in_specs, after):
    after = [a for a in (after or ()) if a is not None]
    if not after:
        return kernel_fn
    body = _anchored(kernel_fn, len(args), len(after))
    args.extend(after)
    in_specs.extend([pl.BlockSpec(memory_space=pl.ANY)] * len(after))
    return body


def _nsa(nq, nqr, gates, kcmp, vcmp, ks, vs, kw, vw, batch, seq, after=None):
    n = batch * seq
    tq = 512
    tk = 512 if seq % 512 == 0 else seq
    nqb = seq // tq
    n_sel = seq // SEL_LEN
    rows_c = seq // CMP_STRIDE
    gw = NSA_HPG * NSA_DIM
    cs = np.arange(rows_c)[None, :] * CMP_STRIDE
    ss = np.arange(n_sel)[:, None] * SEL_LEN
    n_cmp = (seq - CMP_LEN) // CMP_STRIDE + 1
    ovt = ((cs < ss + SEL_LEN) & (cs + CMP_LEN > ss) & (np.arange(rows_c)[None, :] < n_cmp))
    ovt = jnp.asarray(ovt.astype(np.float32), BF16)

    qspec = pl.BlockSpec((tq, gw), lambda b, g, i: (b * nqb + i, g))
    cspec = pl.BlockSpec((None, None, rows_c, NSA_DIM), lambda b, g, i: (b, g, 0, 0))
    kspec = pl.BlockSpec((None, seq, NSA_DIM), lambda b, g, i: (g, b, 0))
    args = [nq, nqr, gates, kcmp, vcmp, ovt, ks, vs, kw, vw]
    in_specs = [qspec, qspec,
                pl.BlockSpec((None, GATE_LANES, tq), lambda b, g, i: (g, 0, b * nqb + i)),
                cspec, cspec, pl.BlockSpec(ovt.shape, lambda b, g, i: (0, 0)),
                kspec, kspec, kspec, kspec]
    body = functools.partial(_add_anchors(_nsa_kernel, args, in_specs, after), tq=tq, tk=tk, seq=seq)
    return pl.pallas_call(
        body,
        grid=(batch, NSA_KV_GROUPS, nqb),
        in_specs=in_specs,
        out_specs=qspec,
        out_shape=jax.ShapeDtypeStruct((n, NSA_WIDTH), BF16),
        scratch_shapes=[pltpu.VMEM((NSA_DIM + SUM_ROWS, seq), BF16), pltpu.VMEM((NSA_DIM + SUM_ROWS, seq), BF16),
                        pltpu.VMEM((NSA_DIM, rows_c), BF16), pltpu.VMEM((n_sel, tq), F32)],
        compiler_params=_params(3),
        name="nsa",
    )(*args)


def _memkv_kernel(mem_ref, w_ref, kv_ref):
    kv_ref[...] = _dot(mem_ref[...].astype(BF16), w_ref[...]).astype(BF16)


def _memkv(mem2d, w_xkv):
    n = mem2d.shape[0]
    w = w_xkv.astype(BF16)
    return pl.pallas_call(
        _memkv_kernel,
        grid=(n // MEM_LEN,),
        in_specs=[pl.BlockSpec((MEM_LEN, D_MODEL), lambda i: (i, 0)),
                  pl.BlockSpec(w.shape, lambda i: (0, 0))],
        out_specs=pl.BlockSpec((MEM_LEN, 2 * D_MODEL), lambda i: (i, 0)),
        out_shape=jax.ShapeDtypeStruct((n, 2 * D_MODEL), BF16),
        compiler_params=_params(1),
        name="memkv",
    )(mem2d, w)


def _pack_halves(v):
    half = D_MODEL // 2
    hi = pltpu.bitcast(v[:, :half].astype(BF16).astype(F32), jnp.uint32)
    lo = pltpu.bitcast(v[:, half:].astype(BF16).astype(F32), jnp.uint32)
    return hi | (lo >> 16)


def _unpack_halves(words):
    return pltpu.bitcast(words & jnp.uint32(0xFFFF0000), F32), pltpu.bitcast(words << 16, F32)


def _postmix_kernel(x_ref, oret_ref, onsa_ref, kv_ref, wout_ref, wq_ref, wo_ref,
                    g1_ref, b1_ref, g2_ref, b2_ref, wr_ref, rbias_ref, x2_ref, x2p_ref, *route_refs):
    mixed = jnp.concatenate([oret_ref[...], onsa_ref[...]], axis=1)
    x1 = _layer_norm(DN_ALPHA * x_ref[...] + _dot(mixed, wout_ref[...]), g1_ref[...], b1_ref[...])
    q = (_dot(x1.astype(BF16), wq_ref[...]) * (XATT_DIM ** -0.5 * LOG2_E)).astype(BF16)
    heads = []
    for h in range(XATT_HEADS):
        cols = slice(h * XATT_DIM, (h + 1) * XATT_DIM)
        s = _dot_nt(q[:, cols], kv_ref[:, cols])
        m = jnp.max(s, axis=-1, keepdims=True)
        p = jnp.exp2(s - m)
        l = jnp.sum(p, axis=-1, keepdims=True)
        heads.append(_dot(p.astype(BF16), kv_ref[:, D_MODEL + h * XATT_DIM:D_MODEL + (h + 1) * XATT_DIM]) / l)
    att = jnp.concatenate(heads, axis=1).astype(BF16)
    x2 = _layer_norm(DN_ALPHA * x1 + _dot(att, wo_ref[...]), g2_ref[...], b2_ref[...])
    x2_ref[...] = x2
    x2p_ref[...] = _pack_halves(x2)
    first = (pl.program_id(0) == 0) & (pl.program_id(1) == 0)
    _route_tile(x2, first, wr_ref, rbias_ref, *route_refs)


def _postmix(x2d, o_ret, o_nsa, kvx, w_out, w_xq, w_xo, ln1_g, ln1_b, ln2_g, ln2_b, w_router, router_bias,
             batch0, batch, seq, after=None):
    n = batch * seq
    tm = 512 if seq % 512 == 0 else seq
    per_b = seq // tm
    row = lambda w: pl.BlockSpec((tm, w), lambda b, i: (b * per_b + i, 0))
    full = lambda a: pl.BlockSpec(a.shape, lambda b, i: (0,) * a.ndim)
    ws = [w_out.astype(BF16), w_xq.astype(BF16), w_xo.astype(BF16)]
    vecs = [v.reshape(1, D_MODEL) for v in (ln1_g, ln1_b, ln2_g, ln2_b)]
    route_w = [w_router.T.astype(BF16), router_bias.reshape(N_EXPERTS, 1).astype(F32)]
    args = [x2d, o_ret, o_nsa, kvx, *ws, *vecs, *route_w]
    in_specs = ([pl.BlockSpec((tm, D_MODEL), lambda b, i: ((batch0 + b) * per_b + i, 0)),
                 row(RET_WIDTH), row(NSA_WIDTH),
                 pl.BlockSpec((MEM_LEN, 2 * D_MODEL), lambda b, i: (batch0 + b, 0))]
                + [full(w) for w in ws] + [full(v) for v in vecs] + [full(w) for w in route_w])
    kspec = pl.BlockSpec((TOP_K, tm), lambda b, i: (0, b * per_b + i))
    return pl.pallas_call(
        _add_anchors(_postmix_kernel, args, in_specs, after),
        grid=(batch, per_b),
        in_specs=in_specs,
        out_specs=[row(D_MODEL), row(D_MODEL // 2), kspec, kspec, kspec,
                   pl.BlockSpec((N_EXPERTS, 1), lambda b, i: (0, 0)),
                   pl.BlockSpec((8, N_EXPERTS), lambda b, i: (0, 0))],
        out_shape=[jax.ShapeDtypeStruct((n, D_MODEL), F32),
                   jax.ShapeDtypeStruct((n, D_MODEL // 2), jnp.uint32),
                   jax.ShapeDtypeStruct((TOP_K, n), jnp.int32),
                   jax.ShapeDtypeStruct((TOP_K, n), jnp.int32),
                   jax.ShapeDtypeStruct((TOP_K, n), F32),
                   jax.ShapeDtypeStruct((N_EXPERTS, 1), F32),
                   jax.ShapeDtypeStruct((8, N_EXPERTS), F32)],
        scratch_shapes=[pltpu.VMEM((N_EXPERTS, 1), F32), pltpu.VMEM((8, N_EXPERTS), F32)],
        compiler_params=_params(2),
        name="postmix",
    )(*args)


def _route_tile(x, first, wr_ref, bias_ref, e_ref, rank_ref, w_ref, cnt_ref, cntrow_ref, carry_ref, carryrow_ref):
    tn = x.shape[0]
    per = N_EXPERTS // N_GROUPS

    @pl.when(first)
    def _():
        carry_ref[...] = jnp.zeros_like(carry_ref)
        carryrow_ref[...] = jnp.zeros_like(carryrow_ref)

    logits = _dot_nt(wr_ref[...], x.astype(BF16))
    scores = jax.nn.sigmoid(logits)
    biased = scores + bias_ref[...]
    b3 = biased.reshape(N_GROUPS, per, tn)
    member = lax.broadcasted_iota(jnp.int32, (N_GROUPS, per, tn), 1)
    top1 = jnp.max(b3, axis=1, keepdims=True)
    first1 = jnp.min(jnp.where(b3 == top1, member, per), axis=1, keepdims=True)
    top2 = jnp.max(jnp.where(member == first1, -jnp.inf, b3), axis=1, keepdims=True)
    gscore = top1 + top2
    gidx = lax.broadcasted_iota(jnp.int32, (N_GROUPS, 1, tn), 0)
    gwork = gscore
    for _ in range(TOPK_GROUPS - 1):
        gbest = jnp.max(gwork, axis=0, keepdims=True)
        gfirst = jnp.min(jnp.where(gwork == gbest, gidx, N_GROUPS), axis=0, keepdims=True)
        gwork = jnp.where(gidx == gfirst, -jnp.inf, gwork)
    kth = jnp.max(gwork, axis=0, keepdims=True)
    work = jnp.where(gscore >= kth, b3, NEG).reshape(N_EXPERTS, tn)
    eidx = lax.broadcasted_iota(jnp.int32, (N_EXPERTS, tn), 0)
    picks = []
    chosen = jnp.zeros((N_EXPERTS, tn), F32)
    for _ in range(TOP_K):
        best = jnp.max(work, axis=0, keepdims=True)
        first = jnp.min(jnp.where(work == best, eidx, N_EXPERTS), axis=0, keepdims=True)
        hit = eidx == first
        picks.append((first, hit))
        chosen = jnp.where(hit, 1.0, chosen)
        work = jnp.where(hit, -jnp.inf, work)

    r_i = lax.broadcasted_iota(jnp.int32, (tn, tn), 0)
    c_i = lax.broadcasted_iota(jnp.int32, (tn, tn), 1)
    before = jnp.where(r_i < c_i, 1.0, 0.0).astype(BF16)
    chosen_b = chosen.astype(BF16)
    rank = _dot(chosen_b, before) + carry_ref[...]
    carry_ref[...] = carry_ref[...] + jnp.sum(chosen, axis=1, keepdims=True)
    carryrow_ref[...] = carryrow_ref[...] + _dot_nt(jnp.ones((8, tn), BF16), chosen_b)
    cnt_ref[...] = carry_ref[...]
    cntrow_ref[...] = carryrow_ref[...]

    wsel = [jnp.sum(jnp.where(hit, scores, 0.0), axis=0, keepdims=True) for _, hit in picks]
    wsum = wsel[0]
    for v in wsel[1:]:
        wsum = wsum + v
    for kk, (first, hit) in enumerate(picks):
        e_ref[kk:kk + 1, :] = first
        rank_ref[kk:kk + 1, :] = jnp.sum(jnp.where(hit, rank, 0.0), axis=0, keepdims=True).astype(jnp.int32)
        w_ref[kk:kk + 1, :] = wsel[kk] / wsum * ROUTED_SCALE


def _slots_kernel(e_ref, rank_ref, cnt_ref, cntrow_ref, dest_ref, blk_e_ref, valid_ref, used_ref,
                  *, blk, n_blocks):
    pad = lambda c: jnp.ceil(c / blk) * blk
    cnt = cnt_ref[...]
    padded = pad(cnt)
    padded_row = pad(cntrow_ref[0:1, :])
    r_i = lax.broadcasted_iota(jnp.int32, (N_EXPERTS, N_EXPERTS), 0)
    c_i = lax.broadcasted_iota(jnp.int32, (N_EXPERTS, N_EXPERTS), 1)
    start = jnp.sum(jnp.where(c_i < r_i, padded_row, 0.0), axis=1, keepdims=True)
    end = start + padded
    e = e_ref[...]
    dest = rank_ref[...]
    for ex in range(N_EXPERTS):
        dest = dest + jnp.where(e == ex, start[ex:ex + 1, :].astype(jnp.int32), 0)
    dest_ref[...] = dest
    bstart = (lax.broadcasted_iota(jnp.int32, (1, n_blocks), 1) * blk).astype(F32)
    owner = jnp.sum(jnp.where(end <= bstart, 1.0, 0.0), axis=0, keepdims=True)
    blk_e_ref[...] = jnp.minimum(owner, N_EXPERTS - 1.0).astype(jnp.int32)
    inside = (start <= bstart) & (bstart < end)
    real = jnp.clip(start + cnt - bstart, 0.0, float(blk))
    valid = jnp.sum(jnp.where(inside, real, 0.0), axis=0, keepdims=True)
    valid_ref[...] = valid.astype(jnp.int32)
    used_ref[...] = jnp.sum(jnp.where(valid > 0.0, 1.0, 0.0), axis=1, keepdims=True).astype(jnp.int32)


def _slots(e_k, rank_k, counts, counts_row, blk, n_blocks):
    n = e_k.shape[1]
    full = lambda shape: pl.BlockSpec(shape, lambda: (0,) * len(shape))
    return pl.pallas_call(
        functools.partial(_slots_kernel, blk=blk, n_blocks=n_blocks),
        in_specs=[full((TOP_K, n)), full((TOP_K, n)), full((N_EXPERTS, 1)), full((8, N_EXPERTS))],
        out_specs=[full((TOP_K, n)), full((1, n_blocks)), full((1, n_blocks)), full((1, 1))],
        out_shape=[jax.ShapeDtypeStruct((TOP_K, n), jnp.int32),
                   jax.ShapeDtypeStruct((1, n_blocks), jnp.int32),
                   jax.ShapeDtypeStruct((1, n_blocks), jnp.int32),
                   jax.ShapeDtypeStruct((1, 1), jnp.int32)],
        compiler_params=pltpu.CompilerParams(vmem_limit_bytes=VMEM_LIMIT),
        name="slots",
    )(e_k, rank_k, counts, counts_row)


def _sc_worker_base(per_worker):
    return (lax.axis_index("s") * SC_CORES + lax.axis_index("c")) * per_worker


def _sc_scatter_rows(rows, idx, n_out):
    n, width = rows.shape
    k_lists = idx.shape[0] // n
    workers = SC_CORES * SC_SUBCORES
    per_worker = n // workers
    assert per_worker * workers == n and per_worker % (SC_CHUNK * SC_INFLIGHT) == 0
    mesh = plsc.VectorSubcoreMesh(core_axis_name="c", subcore_axis_name="s")
    lanes = range(SC_INFLIGHT)

    @functools.partial(
        pl.kernel, mesh=mesh,
        out_type=jax.ShapeDtypeStruct((n_out, width), rows.dtype),
        scratch_types=[pltpu.VMEM((SC_CHUNK, width), rows.dtype)] * SC_INFLIGHT
                      + [pltpu.VMEM((SC_CHUNK,), jnp.int32)] * (k_lists * SC_INFLIGHT)
                      + [pltpu.SemaphoreType.DMA] * (2 * SC_INFLIGHT),
        name="sc_scatter")
    def scatter(rows_hbm, idx_hbm, out_hbm, *scratch):
        rows_vs = scratch[:SC_INFLIGHT]
        idx_vs = [scratch[SC_INFLIGHT + j * k_lists:SC_INFLIGHT + (j + 1) * k_lists] for j in lanes]
        sems = scratch[SC_INFLIGHT * (1 + k_lists):]
        sem_in, sem_out = sems[:SC_INFLIGHT], sems[SC_INFLIGHT:]
        base = _sc_worker_base(per_worker)

        @pl.loop(0, per_worker // (SC_CHUNK * SC_INFLIGHT))
        def _(gi):
            offs = [pl.multiple_of(base + (gi * SC_INFLIGHT + j) * SC_CHUNK, SC_CHUNK) for j in lanes]
            loads = []
            for j in lanes:
                loads.append([pltpu.async_copy(rows_hbm.at[pl.ds(offs[j], SC_CHUNK)], rows_vs[j], sem_in[j])]
                             + [pltpu.async_copy(
                                 idx_hbm.at[pl.ds(pl.multiple_of(k * n + offs[j], SC_CHUNK), SC_CHUNK)],
                                 idx_vs[j][k], sem_in[j]) for k in range(k_lists)])
            copies = []
            for j in lanes:
                for c in loads[j]:
                    c.wait()
                copies.append([pltpu.async_copy(rows_vs[j], out_hbm.at[idx_vs[j][k]], sem_out[j])
                               for k in range(k_lists)])
            for j in lanes:
                for c in copies[j]:
                    c.wait()

    return scatter(rows, idx)


def _experts_kernel(blk_e_ref, valid_ref, used_ref, xs_ref, wg_ref, wu_ref, wd_ref, y_ref, wg_b, wu_b, wd_b):
    del used_ref
    i = pl.program_id(0)
    valid = valid_ref[i]

    @pl.when((i == 0) | (blk_e_ref[i] != blk_e_ref[jnp.maximum(i - 1, 0)]))
    def _():
        wg_b[...] = wg_ref[...].astype(BF16)
        wu_b[...] = wu_ref[...].astype(BF16)
        wd_b[...] = wd_ref[...].astype(BF16)

    @pl.when(valid > 0)
    def _():
        half = D_MODEL // 2
        row = lax.broadcasted_iota(jnp.int32, (xs_ref.shape[0], 1), 0)
        hi, lo = (v.astype(BF16) for v in _unpack_halves(jnp.where(row < valid, xs_ref[...], jnp.uint32(0))))
        gate = _dot(hi, wg_b[:half, :]) + _dot(lo, wg_b[half:, :])
        up = _dot(hi, wu_b[:half, :]) + _dot(lo, wu_b[half:, :])
        y_ref[...] = _pack_halves(_dot((jax.nn.silu(gate) * up).astype(BF16), wd_b[...]))

    @pl.when(valid <= 0)
    def _():
        y_ref[...] = jnp.zeros_like(y_ref)


def _experts(blk_e, valid, n_used, xs, w_gate, w_up, w_down, blk, after=None):
    cap, width = xs.shape
    wspec = lambda a: pl.BlockSpec((None,) + a.shape[1:], lambda i, be, nv, nu: (be[i], 0, 0))
    rows = pl.BlockSpec((blk, width), lambda i, be, nv, nu: (jnp.minimum(i, nu[0]), 0))
    args = [blk_e, valid, n_used, xs, w_gate, w_up, w_down]
    in_specs = [rows, wspec(w_gate), wspec(w_up), wspec(w_down)]
    body = _add_anchors(_experts_kernel, args, in_specs, after)
    return pl.pallas_call(
        body,
        grid_spec=pltpu.PrefetchScalarGridSpec(
            num_scalar_prefetch=3,
            grid=(cap // blk,),
            in_specs=in_specs,
            out_specs=rows,
            scratch_shapes=[pltpu.VMEM(w.shape[1:], BF16) for w in (w_gate, w_up, w_down)],
        ),
        out_shape=jax.ShapeDtypeStruct(xs.shape, xs.dtype),
        compiler_params=_params(1),
        name="experts",
    )(*args)


def _sc_gather_rows(table, idx):
    b, width = idx.shape[0], table.shape[1]
    workers = SC_CORES * SC_SUBCORES
    per_worker = b // workers
    assert per_worker * workers == b and per_worker % (SC_CHUNK * SC_INFLIGHT) == 0
    mesh = plsc.VectorSubcoreMesh(core_axis_name="c", subcore_axis_name="s")

    @functools.partial(
        pl.kernel, mesh=mesh,
        out_type=jax.ShapeDtypeStruct((b, width), table.dtype),
        scratch_types=[pltpu.VMEM((SC_CHUNK,), jnp.int32)] * SC_INFLIGHT
                      + [pltpu.VMEM((SC_CHUNK, width), table.dtype)] * SC_INFLIGHT
                      + [pltpu.SemaphoreType.DMA] * (1 + 2 * SC_INFLIGHT),
        name="sc_gather")
    def gather(table_hbm, idx_hbm, out_hbm, *scratch):
        idx_vs = scratch[:SC_INFLIGHT]
        rows_vs = scratch[SC_INFLIGHT:2 * SC_INFLIGHT]
        sem_idx = scratch[2 * SC_INFLIGHT]
        sem_rows = scratch[2 * SC_INFLIGHT + 1:3 * SC_INFLIGHT + 1]
        sem_out = scratch[3 * SC_INFLIGHT + 1:]
        base = _sc_worker_base(per_worker)
        lanes = range(SC_INFLIGHT)

        @pl.loop(0, per_worker // (SC_CHUNK * SC_INFLIGHT))
        def _(gi):
            offs = [pl.multiple_of(base + (gi * SC_INFLIGHT + j) * SC_CHUNK, SC_CHUNK) for j in lanes]
            loads = [pltpu.async_copy(idx_hbm.at[pl.ds(offs[j], SC_CHUNK)], idx_vs[j], sem_idx) for j in lanes]
            for c in loads:
                c.wait()
            gathers = [pltpu.async_copy(table_hbm.at[idx_vs[j]], rows_vs[j], sem_rows[j]) for j in lanes]
            writes = []
            for j in lanes:
                gathers[j].wait()
                writes.append(pltpu.async_copy(rows_vs[j], out_hbm.at[pl.ds(offs[j], SC_CHUNK)], sem_out[j]))
            for c in writes:
                c.wait()

    return gather(table, idx)


def _combine_kernel(x_ref, wk_ref, yk_ref, wsg_ref, wsu_ref, wsd_ref, g_ref, b_ref, *rest):
    o_ref = rest[-1]
    x = x_ref[...]
    xb = x.astype(BF16)
    shared = _dot((jax.nn.silu(_dot(xb, wsg_ref[...])) * _dot(xb, wsu_ref[...])).astype(BF16), wsd_ref[...])
    wk = wk_ref[...]
    routed_hi = routed_lo = None
    for kk in range(TOP_K):
        hi, lo = _unpack_halves(yk_ref[kk])
        w = wk[:, kk:kk + 1]
        routed_hi = hi * w if kk == 0 else routed_hi + hi * w
        routed_lo = lo * w if kk == 0 else routed_lo + lo * w
    routed = jnp.concatenate([routed_hi, routed_lo], axis=1)
    o_ref[...] = _layer_norm(DN_ALPHA * x + (routed + shared), g_ref[...], b_ref[...])


def _combine(x2, w_tok, yk, ws_gate, ws_up, ws_down, ln3_g, ln3_b, row0, n_total, out_prev, after=None,
             part=(0, 1)):
    n = x2.shape[0] // part[1]
    sub0 = part[0] * n
    tt = 512 if n % 512 == 0 and row0 % 512 == 0 else n
    blk0 = (row0 + sub0) // tt
    sub_blk = sub0 // tt
    ws = [ws_gate.astype(BF16), ws_up.astype(BF16), ws_down.astype(BF16)]
    vecs = [ln3_g.reshape(1, D_MODEL), ln3_b.reshape(1, D_MODEL)]
    full = lambda a: pl.BlockSpec(a.shape, lambda i: (0,) * a.ndim)
    args = [x2, w_tok, yk, *ws, *vecs]
    in_specs = ([pl.BlockSpec((tt, D_MODEL), lambda i: (sub_blk + i, 0)),
                 pl.BlockSpec((tt, TOP_K), lambda i: (sub_blk + i, 0)),
                 pl.BlockSpec((TOP_K, tt, D_MODEL // 2), lambda i: (0, sub_blk + i, 0))]
                + [full(a) for a in ws] + [full(v) for v in vecs])
    aliases = {}
    for anchor in (a for a in (after or ()) if a is not None):
        args.append(anchor)
        in_specs.append(pl.BlockSpec(memory_space=pl.ANY))
    if out_prev is not None:
        aliases = {len(args): 0}
        args.append(out_prev)
        in_specs.append(pl.BlockSpec(memory_space=pl.ANY))
    return pl.pallas_call(
        _combine_kernel,
        grid=(n // tt,),
        in_specs=in_specs,
        out_specs=pl.BlockSpec((tt, D_MODEL), lambda i: (blk0 + i, 0)),
        out_shape=jax.ShapeDtypeStruct((n_total, D_MODEL), F32),
        input_output_aliases=aliases,
        compiler_params=_params(1),
        name="combine",
    )(*args)


EXPERT_BLOCK = 1024


def _moe_dispatch(x2p, e_k, rank_k, w_k, counts, counts_row):
    n = x2p.shape[0]
    cap = n * TOP_K + N_EXPERTS * EXPERT_BLOCK
    dest, blk_e, valid, n_used = _slots(e_k, rank_k, counts, counts_row, EXPERT_BLOCK, cap // EXPERT_BLOCK)
    dest = dest.reshape(-1)
    return dict(w_tok=w_k.T, dest=dest, blk_e=blk_e.reshape(-1), valid=valid.reshape(-1),
                n_used=n_used.reshape(-1), xs=_sc_scatter_rows(x2p, dest, cap))


def _moe_experts(routed, w_gate, w_up, w_down, after):
    y = _experts(routed["blk_e"], routed["valid"], routed["n_used"], routed["xs"], w_gate, w_up, w_down,
                 EXPERT_BLOCK, after=after)
    n = routed["dest"].shape[0] // TOP_K
    return y, _sc_gather_rows(y, routed["dest"]).reshape(TOP_K, n, D_MODEL // 2)


def _layer(x, mem, positions, w_in, cmp_pe_k, cmp_pe_v, cmp_w1_k, cmp_w2_k, cmp_w1_v, cmp_w2_v,
           w_out, ln1_g, ln1_b, w_xq, w_xkv, w_xo, ln2_g, ln2_b, w_router, router_bias,
           w_gate, w_up, w_down, ws_gate, ws_up, ws_down, ln3_g, ln3_b):
    batch, seq, _ = x.shape
    n_total = batch * seq
    x2d = x.reshape(n_total, D_MODEL)
    pos_col = positions.astype(F32).reshape(n_total, 1)
    kvx = _memkv(mem.reshape(batch * MEM_LEN, D_MODEL), w_xkv)
    last = max(1, batch // 4)
    sizes = [batch - last, last] if batch > 1 else [batch]
    starts = [sum(sizes[:g]) for g in range(len(sizes))]

    def mixers_in(g):
        nb, row0 = sizes[g], starts[g] * seq
        (rq, rk, rv, rg, nq, nqr, kc, vc, ks, vs, kw, vw, gates) = _inproj(x2d, pos_col, w_in, row0, nb * seq)
        o_ret = _retention(rq, rk, rv, rg, nb, seq)
        kcmp = _compress(kc, cmp_pe_k, cmp_w1_k, cmp_w2_k, nb, seq)
        vcmp = _compress(vc, cmp_pe_v, cmp_w1_v, cmp_w2_v, nb, seq)
        return o_ret, (nq, nqr, gates, kcmp, vcmp, ks, vs, kw, vw)

    def attend(g, nsa_args, after):
        return _nsa(*nsa_args, sizes[g], seq, after=after)

    def mix_and_route(g, o_ret, o_nsa, after):
        x2, x2p, *routing = _postmix(x2d, o_ret, o_nsa, kvx, w_out, w_xq, w_xo, ln1_g, ln1_b, ln2_g, ln2_b,
                                     w_router, router_bias, starts[g], sizes[g], seq, after=after)
        return x2, _moe_dispatch(x2p, *routing)

    def combine(g, x2, routed, yk, out_prev, after=None, part=(0, 1)):
        return _combine(x2, routed["w_tok"], yk, ws_gate, ws_up, ws_down, ln3_g, ln3_b, starts[g] * seq,
                        n_total, out_prev, after=after, part=part)

    o_ret, nsa_args = mixers_in(0)
    x2, routed = mix_and_route(0, o_ret, attend(0, nsa_args, None), None)
    if len(sizes) == 1:
        y, yk = _moe_experts(routed, w_gate, w_up, w_down, after=None)
        return combine(0, x2, routed, yk, None).reshape(batch, seq, D_MODEL)
    o_ret1, nsa_args1 = mixers_in(1)
    y, yk = _moe_experts(routed, w_gate, w_up, w_down, after=[o_ret1])
    o_nsa1 = attend(1, nsa_args1, [y])
    x2_1, routed1 = mix_and_route(1, o_ret1, o_nsa1, [yk])
    out = combine(0, x2, routed, yk, None, [routed1["dest"]], part=(0, 2))
    y1, yk1 = _moe_experts(routed1, w_gate, w_up, w_down, after=[out])
    out = combine(0, x2, routed, yk, out, [y1], part=(1, 2))
    out = combine(1, x2_1, routed1, yk1, out)
    return out.reshape(batch, seq, D_MODEL)


def kernel(x, mem, positions, w_in, cmp_pe_k, cmp_pe_v, cmp_w1_k, cmp_w2_k, cmp_w1_v, cmp_w2_v, w_out, ln1_g, ln1_b, w_xq, w_xkv, w_xo, ln2_g, ln2_b, w_router, router_bias, w_gate, w_up, w_down, ws_gate, ws_up, ws_down, ln3_g, ln3_b):
    for l in range(DEPTH):
        x = _layer(x, mem, positions, w_in[l], cmp_pe_k[l], cmp_pe_v[l], cmp_w1_k[l], cmp_w2_k[l],
                   cmp_w1_v[l], cmp_w2_v[l], w_out[l], ln1_g[l], ln1_b[l], w_xq[l], w_xkv[l],
                   w_xo[l], ln2_g[l], ln2_b[l], w_router[l], router_bias[l], w_gate[l], w_up[l],
                   w_down[l], ws_gate[l], ws_up[l], ws_down[l], ln3_g[l], ln3_b[l])
    return x
```
